```python
import math
import jax, jax.numpy as jnp
from jax import lax
import numpy as np

D_MODEL = 1024
BATCH = 8
SEQ = 4096
DEPTH = 2

CHUNK = 64
Q_BLOCK = 128
N_HEADS = 8
QK_NOPE_DIM = 128
QK_ROPE_DIM = 64
V_HEAD_DIM = 128
Q_LORA_RANK = 512
KV_LORA_RANK = 256
ROPE_THETA = 10000.0
MAX_POS_OFFSET = 16384
CONV_WIDTH = 3
D_FF = ((8 * D_MODEL // 3 + 255) // 256) * 256
RMS_EPS = 1e-6
N_MLA_LAYERS = (DEPTH + 1) // 2
N_CONV_LAYERS = DEPTH // 2
MLA_IN_DIM = Q_LORA_RANK + KV_LORA_RANK + QK_ROPE_DIM

kernel_name = "hybrid_mla_shortconv_swiglu_trunk"


def rms_norm(x, g):
    xf = x.astype(jnp.float32)
    y = xf * lax.rsqrt(jnp.mean(xf * xf, axis=-1, keepdims=True) + RMS_EPS)
    return (y * g.astype(jnp.float32)).astype(x.dtype)


def rope_tables(positions):
    inv_freq = 1.0 / (ROPE_THETA ** (jnp.arange(0, QK_ROPE_DIM, 2, dtype=jnp.float32) / QK_ROPE_DIM))
    ang = positions.astype(jnp.float32)[..., None] * inv_freq
    return jnp.cos(ang), jnp.sin(ang)


def apply_rope(x, cos, sin):
    xf = x.astype(jnp.float32)
    x1, x2 = jnp.split(xf, 2, axis=-1)
    out = jnp.concatenate([x1 * cos - x2 * sin, x1 * sin + x2 * cos], axis=-1)
    return out.astype(x.dtype)


def chunk_causal_mla_attention(q_n, q_r, k_n, k_r, v):
    bsz, seq, h, _ = q_n.shape
    nb = seq // Q_BLOCK
    scale = 1.0 / math.sqrt(QK_NOPE_DIM + QK_ROPE_DIM)
    qn_b = q_n.reshape(bsz, nb, Q_BLOCK, h, QK_NOPE_DIM).transpose(1, 0, 2, 3, 4)
    qr_b = q_r.reshape(bsz, nb, Q_BLOCK, h, QK_ROPE_DIM).transpose(1, 0, 2, 3, 4)
    k_chunk = jnp.arange(seq) // CHUNK

    def one_block(args):
        qn, qr, blk = args
        q_chunk = (blk * Q_BLOCK + jnp.arange(Q_BLOCK)) // CHUNK
        s = (jnp.einsum('bqhd,bkhd->bhqk', qn, k_n).astype(jnp.float32)
             + jnp.einsum('bqhd,bkd->bhqk', qr, k_r).astype(jnp.float32)) * scale
        mask = k_chunk[None, :] <= q_chunk[:, None]
        s = jnp.where(mask[None, None], s, jnp.float32(-1e30))
        p = jax.nn.softmax(s, axis=-1).astype(v.dtype)
        return jnp.einsum('bhqk,bkhd->bqhd', p, v)

    out = lax.map(one_block, (qn_b, qr_b, jnp.arange(nb)))
    return out.transpose(1, 0, 2, 3, 4).reshape(bsz, seq, h, V_HEAD_DIM)


def mla_mixer(h, positions, w_in, g_cq, g_ckv, w_uq, w_ukv, w_o):
    bsz, seq, _ = h.shape
    proj = h @ w_in
    c_q = proj[..., :Q_LORA_RANK]
    c_kv = proj[..., Q_LORA_RANK:Q_LORA_RANK + KV_LORA_RANK]
    k_r = proj[..., Q_LORA_RANK + KV_LORA_RANK:]
    c_q = rms_norm(c_q, g_cq)
    c_kv = rms_norm(c_kv, g_ckv)
    q = (c_q @ w_uq).reshape(bsz, seq, N_HEADS, QK_NOPE_DIM + QK_ROPE_DIM)
    q_n, q_r = q[..., :QK_NOPE_DIM], q[..., QK_NOPE_DIM:]
    kv = (c_kv @ w_ukv).reshape(bsz, seq, N_HEADS, QK_NOPE_DIM + V_HEAD_DIM)
    k_n, v = kv[..., :QK_NOPE_DIM], kv[..., QK_NOPE_DIM:]
    cos, sin = rope_tables(positions)
    q_r = apply_rope(q_r, cos[:, :, None, :], sin[:, :, None, :])
    k_r = apply_rope(k_r, cos, sin)
    attn = chunk_causal_mla_attention(q_n, q_r, k_n, k_r, v)
    return attn.reshape(bsz, seq, N_HEADS * V_HEAD_DIM) @ w_o


def short_conv_mixer(h, w_in, conv_w, w_out):
    bcx = h @ w_in
    b_gate = bcx[..., :D_MODEL]
    c_gate = bcx[..., D_MODEL:2 * D_MODEL]
    xp = bcx[..., 2 * D_MODEL:]
    u = c_gate * xp
    u_conv = lax.conv_general_dilated(
        u, conv_w[:, None, :].astype(u.dtype), window_strides=(1,),
        padding=[(CONV_WIDTH - 1, 0)], dimension_numbers=('NWC', 'WIO', 'NWC'),
        feature_group_count=D_MODEL)
    return (b_gate * u_conv) @ w_out


def swiglu(h, w_gate, w_up, w_down):
    return (jax.nn.silu(h @ w_gate) * (h @ w_up)) @ w_down


def _fwd_setup_inputs(seed: int = 0) -> dict:
    key = jax.random.key(seed)
    ks = jax.random.split(key, 24)

    def w(k, shape, fan_in):
        return jax.random.normal(k, shape, jnp.float32) * (fan_in ** -0.5)

    def gain(k, shape):
        return 1.0 + 0.05 * jax.random.normal(k, shape, jnp.float32)

    x = jax.random.normal(ks[0], (BATCH, SEQ, D_MODEL), jnp.float32)
    offset = jax.random.randint(ks[1], (BATCH, 1), 0, MAX_POS_OFFSET, dtype=jnp.int32)
    positions = (offset + jnp.arange(SEQ, dtype=jnp.int32)[None, :]).astype(jnp.int32)
    L_a, L_b = N_MLA_LAYERS, N_CONV_LAYERS
    return {
        "x": x,
        "positions": positions,
        "mla_norm": gain(ks[2], (L_a, D_MODEL)),
        "mla_w_in": w(ks[3], (L_a, D_MODEL, MLA_IN_DIM), D_MODEL),
        "mla_g_cq": gain(ks[4], (L_a, Q_LORA_RANK)),
        "mla_g_ckv": gain(ks[5], (L_a, KV_LORA_RANK)),
        "mla_w_uq": w(ks[6], (L_a, Q_LORA_RANK, N_HEADS * (QK_NOPE_DIM + QK_ROPE_DIM)), Q_LORA_RANK),
        "mla_w_ukv": w(ks[7], (L_a, KV_LORA_RANK, N_HEADS * (QK_NOPE_DIM + V_HEAD_DIM)), KV_LORA_RANK),
        "mla_w_o": w(ks[8], (L_a, N_HEADS * V_HEAD_DIM, D_MODEL), N_HEADS * V_HEAD_DIM),
        "conv_norm": gain(ks[9], (L_b, D_MODEL)),
        "conv_w_in": w(ks[10], (L_b, D_MODEL, 3 * D_MODEL), D_MODEL),
        "conv_w": w(ks[11], (L_b, CONV_WIDTH, D_MODEL), CONV_WIDTH),
        "conv_w_out": w(ks[12], (L_b, D_MODEL, D_MODEL), D_MODEL),
        "ffn_norm": gain(ks[13], (DEPTH, D_MODEL)),
        "ffn_w_gate": w(ks[14], (DEPTH, D_MODEL, D_FF), D_MODEL),
        "ffn_w_up": w(ks[15], (DEPTH, D_MODEL, D_FF), D_MODEL),
        "ffn_w_down": w(ks[16], (DEPTH, D_FF, D_MODEL), D_FF),
        "final_norm": gain(ks[17], (D_MODEL,)),
    }


def _fwd_reference(x, positions, mla_norm, mla_w_in, mla_g_cq, mla_g_ckv, mla_w_uq,
              mla_w_ukv, mla_w_o, conv_norm, conv_w_in, conv_w, conv_w_out,
              ffn_norm, ffn_w_gate, ffn_w_up, ffn_w_down, final_norm):
    h = x
    for i in range(DEPTH):
        j = i // 2
        if i % 2 == 0:
            h = h + mla_mixer(rms_norm(h, mla_norm[j]), positions, mla_w_in[j],
                              mla_g_cq[j], mla_g_ckv[j], mla_w_uq[j], mla_w_ukv[j], mla_w_o[j])
        else:
            h = h + short_conv_mixer(rms_norm(h, conv_norm[j]), conv_w_in[j],
                                     conv_w[j], conv_w_out[j])
        h = h + swiglu(rms_norm(h, ffn_norm[i]), ffn_w_gate[i], ffn_w_up[i], ffn_w_down[i])
    return rms_norm(h, final_norm)


import jax as _jax
import jax.numpy as _jnp

TWIN_FORMAT = 'train_step'
FWD_PARAMS = ['x', 'positions', 'mla_norm', 'mla_w_in', 'mla_g_cq', 'mla_g_ckv', 'mla_w_uq', 'mla_w_ukv', 'mla_w_o', 'conv_norm', 'conv_w_in', 'conv_w', 'conv_w_out', 'ffn_norm', 'ffn_w_gate', 'ffn_w_up', 'ffn_w_down', 'final_norm']
TWIN_WEIGHTS = ['mla_norm', 'mla_w_in', 'mla_g_cq', 'mla_g_ckv', 'mla_w_uq', 'mla_w_ukv', 'mla_w_o', 'conv_norm', 'conv_w_in', 'conv_w', 'conv_w_out', 'ffn_norm', 'ffn_w_gate', 'ffn_w_up', 'ffn_w_down', 'final_norm']
TWIN_DIFF_INPUT = 'x'
TWIN_INPUTS = ['x', 'positions', 'mla_norm', 'mla_w_in', 'mla_g_cq', 'mla_g_ckv', 'mla_w_uq', 'mla_w_ukv', 'mla_w_o', 'conv_norm', 'conv_w_in', 'conv_w', 'conv_w_out', 'ffn_norm', 'ffn_w_gate', 'ffn_w_up', 'ffn_w_down', 'final_norm', 'loss_target', 'm_mla_norm', 'm_mla_w_in', 'm_mla_g_cq', 'm_mla_g_ckv', 'm_mla_w_uq', 'm_mla_w_ukv', 'm_mla_w_o', 'm_conv_norm', 'm_conv_w_in', 'm_conv_w', 'm_conv_w_out', 'm_ffn_norm', 'm_ffn_w_gate', 'm_ffn_w_up', 'm_ffn_w_down', 'm_final_norm', 'v_mla_norm', 'v_mla_w_in', 'v_mla_g_cq', 'v_mla_g_ckv', 'v_mla_w_uq', 'v_mla_w_ukv', 'v_mla_w_o', 'v_conv_norm', 'v_conv_w_in', 'v_conv_w', 'v_conv_w_out', 'v_ffn_norm', 'v_ffn_w_gate', 'v_ffn_w_up', 'v_ffn_w_down', 'v_final_norm']
TWIN_OUTPUTS = ['loss', 'grad_x', 'grad_mla_norm', 'grad_mla_w_in', 'grad_mla_g_cq', 'grad_mla_g_ckv', 'grad_mla_w_uq', 'grad_mla_w_ukv', 'grad_mla_w_o', 'grad_conv_norm', 'grad_conv_w_in', 'grad_conv_w', 'grad_conv_w_out', 'grad_ffn_norm', 'grad_ffn_w_gate', 'grad_ffn_w_up', 'grad_ffn_w_down', 'grad_final_norm', 'delta_mla_norm', 'delta_mla_w_in', 'delta_mla_g_cq', 'delta_mla_g_ckv', 'delta_mla_w_uq', 'delta_mla_w_ukv', 'delta_mla_w_o', 'delta_conv_norm', 'delta_conv_w_in', 'delta_conv_w', 'delta_conv_w_out', 'delta_ffn_norm', 'delta_ffn_w_gate', 'delta_ffn_w_up', 'delta_ffn_w_down', 'delta_final_norm', 'new_m_mla_norm', 'new_m_mla_w_in', 'new_m_mla_g_cq', 'new_m_mla_g_ckv', 'new_m_mla_w_uq', 'new_m_mla_w_ukv', 'new_m_mla_w_o', 'new_m_conv_norm', 'new_m_conv_w_in', 'new_m_conv_w', 'new_m_conv_w_out', 'new_m_ffn_norm', 'new_m_ffn_w_gate', 'new_m_ffn_w_up', 'new_m_ffn_w_down', 'new_m_final_norm', 'new_v_mla_norm', 'new_v_mla_w_in', 'new_v_mla_g_cq', 'new_v_mla_g_ckv', 'new_v_mla_w_uq', 'new_v_mla_w_ukv', 'new_v_mla_w_o', 'new_v_conv_norm', 'new_v_conv_w_in', 'new_v_conv_w', 'new_v_conv_w_out', 'new_v_ffn_norm', 'new_v_ffn_w_gate', 'new_v_ffn_w_up', 'new_v_ffn_w_down', 'new_v_final_norm']
TWIN_LEAF_KINDS = {'loss': 'loss', 'grad_x': 'grad_x', 'grad_mla_norm': 'grad_w', 'grad_mla_w_in': 'grad_w', 'grad_mla_g_cq': 'grad_w', 'grad_mla_g_ckv': 'grad_w', 'grad_mla_w_uq': 'grad_w', 'grad_mla_w_ukv': 'grad_w', 'grad_mla_w_o': 'grad_w', 'grad_conv_norm': 'grad_w', 'grad_conv_w_in': 'grad_w', 'grad_conv_w': 'grad_w', 'grad_conv_w_out': 'grad_w', 'grad_ffn_norm': 'grad_w', 'grad_ffn_w_gate': 'grad_w', 'grad_ffn_w_up': 'grad_w', 'grad_ffn_w_down': 'grad_w', 'grad_final_norm': 'grad_w', 'delta_mla_norm': 'delta_w', 'delta_mla_w_in': 'delta_w', 'delta_mla_g_cq': 'delta_w', 'delta_mla_g_ckv': 'delta_w', 'delta_mla_w_uq': 'delta_w', 'delta_mla_w_ukv': 'delta_w', 'delta_mla_w_o': 'delta_w', 'delta_conv_norm': 'delta_w', 'delta_conv_w_in': 'delta_w', 'delta_conv_w': 'delta_w', 'delta_conv_w_out': 'delta_w', 'delta_ffn_norm': 'delta_w', 'delta_ffn_w_gate': 'delta_w', 'delta_ffn_w_up': 'delta_w', 'delta_ffn_w_down': 'delta_w', 'delta_final_norm': 'delta_w', 'new_m_mla_norm': 'new_m', 'new_m_mla_w_in': 'new_m', 'new_m_mla_g_cq': 'new_m', 'new_m_mla_g_ckv': 'new_m', 'new_m_mla_w_uq': 'new_m', 'new_m_mla_w_ukv': 'new_m', 'new_m_mla_w_o': 'new_m', 'new_m_conv_norm': 'new_m', 'new_m_conv_w_in': 'new_m', 'new_m_conv_w': 'new_m', 'new_m_conv_w_out': 'new_m', 'new_m_ffn_norm': 'new_m', 'new_m_ffn_w_gate': 'new_m', 'new_m_ffn_w_up': 'new_m', 'new_m_ffn_w_down': 'new_m', 'new_m_final_norm': 'new_m', 'new_v_mla_norm': 'new_v', 'new_v_mla_w_in': 'new_v', 'new_v_mla_g_cq': 'new_v', 'new_v_mla_g_ckv': 'new_v', 'new_v_mla_w_uq': 'new_v', 'new_v_mla_w_ukv': 'new_v', 'new_v_mla_w_o': 'new_v', 'new_v_conv_norm': 'new_v', 'new_v_conv_w_in': 'new_v', 'new_v_conv_w': 'new_v', 'new_v_conv_w_out': 'new_v', 'new_v_ffn_norm': 'new_v', 'new_v_ffn_w_gate': 'new_v', 'new_v_ffn_w_up': 'new_v', 'new_v_ffn_w_down': 'new_v', 'new_v_final_norm': 'new_v'}


def _forward(args):
    return _fwd_reference(*[args[k] for k in FWD_PARAMS])


def _output_shape():
    def fwd():
        inp = _fwd_setup_inputs(0)
        return _fwd_reference(*[inp[k] for k in FWD_PARAMS])
    out = _jax.eval_shape(fwd)
    return out.shape, out.dtype

N_MICROBATCH = 1
ADAM_LR = 0.001
ADAM_B1 = 0.9
ADAM_B2 = 0.999
ADAM_EPS = 1e-08
ADAM_WD = 0.01
ADAM_STEP = 10
PER_EXAMPLE_BATCH_AXIS = {'x': 0, 'positions': 0, 'loss_target': 0}
SHARED_INPUTS = []
_WEIGHT_DTYPES = {'mla_norm': _jnp.float32, 'mla_w_in': _jnp.float32, 'mla_g_cq': _jnp.float32, 'mla_g_ckv': _jnp.float32, 'mla_w_uq': _jnp.float32, 'mla_w_ukv': _jnp.float32, 'mla_w_o': _jnp.float32, 'conv_norm': _jnp.float32, 'conv_w_in': _jnp.float32, 'conv_w': _jnp.float32, 'conv_w_out': _jnp.float32, 'ffn_norm': _jnp.float32, 'ffn_w_gate': _jnp.float32, 'ffn_w_up': _jnp.float32, 'ffn_w_down': _jnp.float32, 'final_norm': _jnp.float32}
MOMENT_SCALE = {'mla_norm': 7.195326e-02, 'mla_w_in': 8.058454e-02, 'mla_g_cq': 5.692268e-02, 'mla_g_ckv': 1.267886e-01, 'mla_w_uq': 3.499774e-02, 'mla_w_ukv': 4.085069e-02, 'mla_w_o': 4.503229e-02, 'conv_norm': 2.145007e-01, 'conv_w_in': 1.236890e-01, 'conv_w': 1.237148e-01, 'conv_w_out': 1.269383e-01, 'ffn_norm': 1.461189e-01, 'ffn_w_gate': 6.524686e-02, 'ffn_w_up': 6.333219e-02, 'ffn_w_down': 1.052340e-01, 'final_norm': 3.208496e+01}


def _to_microbatches(a, axis):
    t = _jnp.moveaxis(a, axis, 0)
    t = t.reshape((N_MICROBATCH, t.shape[0] // N_MICROBATCH) + t.shape[1:])
    return _jnp.moveaxis(t, 1, axis + 1)


def setup_inputs(seed: int = 0) -> dict:
    inp = _fwd_setup_inputs(seed)
    key = _jax.random.fold_in(_jax.random.key(seed), 7919)
    shape, _ = _output_shape()
    out = dict(inp)
    out["loss_target"] = _jax.random.normal(_jax.random.fold_in(key, 0), shape, _jnp.float32)
    for i, name in enumerate(TWIN_WEIGHTS):
        w = inp[name].astype(_jnp.float32)
        if MOMENT_SCALE is None:
            s = _jnp.sqrt(_jnp.mean(_jnp.square(w)) + 1e-30)
        else:
            s = MOMENT_SCALE[name]
        km, kv = _jax.random.split(_jax.random.fold_in(key, i + 1))
        out[name] = w
        out["m_" + name] = s * _jax.random.normal(km, w.shape, _jnp.float32)
        out["v_" + name] = (s * s) * _jax.random.uniform(kv, w.shape, _jnp.float32, 0.5, 1.5)
    if N_MICROBATCH > 1:
        for name, axis in PER_EXAMPLE_BATCH_AXIS.items():
            out[name] = _to_microbatches(out[name], axis)
    return {'x': out['x'], 'positions': out['positions'], 'mla_norm': out['mla_norm'], 'mla_w_in': out['mla_w_in'], 'mla_g_cq': out['mla_g_cq'], 'mla_g_ckv': out['mla_g_ckv'], 'mla_w_uq': out['mla_w_uq'], 'mla_w_ukv': out['mla_w_ukv'], 'mla_w_o': out['mla_w_o'], 'conv_norm': out['conv_norm'], 'conv_w_in': out['conv_w_in'], 'conv_w': out['conv_w'], 'conv_w_out': out['conv_w_out'], 'ffn_norm': out['ffn_norm'], 'ffn_w_gate': out['ffn_w_gate'], 'ffn_w_up': out['ffn_w_up'], 'ffn_w_down': out['ffn_w_down'], 'final_norm': out['final_norm'], 'loss_target': out['loss_target'], 'm_mla_norm': out['m_mla_norm'], 'm_mla_w_in': out['m_mla_w_in'], 'm_mla_g_cq': out['m_mla_g_cq'], 'm_mla_g_ckv': out['m_mla_g_ckv'], 'm_mla_w_uq': out['m_mla_w_uq'], 'm_mla_w_ukv': out['m_mla_w_ukv'], 'm_mla_w_o': out['m_mla_w_o'], 'm_conv_norm': out['m_conv_norm'], 'm_conv_w_in': out['m_conv_w_in'], 'm_conv_w': out['m_conv_w'], 'm_conv_w_out': out['m_conv_w_out'], 'm_ffn_norm': out['m_ffn_norm'], 'm_ffn_w_gate': out['m_ffn_w_gate'], 'm_ffn_w_up': out['m_ffn_w_up'], 'm_ffn_w_down': out['m_ffn_w_down'], 'm_final_norm': out['m_final_norm'], 'v_mla_norm': out['v_mla_norm'], 'v_mla_w_in': out['v_mla_w_in'], 'v_mla_g_cq': out['v_mla_g_cq'], 'v_mla_g_ckv': out['v_mla_g_ckv'], 'v_mla_w_uq': out['v_mla_w_uq'], 'v_mla_w_ukv': out['v_mla_w_ukv'], 'v_mla_w_o': out['v_mla_w_o'], 'v_conv_norm': out['v_conv_norm'], 'v_conv_w_in': out['v_conv_w_in'], 'v_conv_w': out['v_conv_w'], 'v_conv_w_out': out['v_conv_w_out'], 'v_ffn_norm': out['v_ffn_norm'], 'v_ffn_w_gate': out['v_ffn_w_gate'], 'v_ffn_w_up': out['v_ffn_w_up'], 'v_ffn_w_down': out['v_ffn_w_down'], 'v_final_norm': out['v_final_norm']}


def _loss(weights, diff, rest, loss_target):
    with _jax.named_scope("forward"):
        args = {**rest, TWIN_DIFF_INPUT: diff, **{k: w.astype(_WEIGHT_DTYPES[k]) for k, w in weights.items()}}
        y = _forward(args)
    with _jax.named_scope("loss_head"):
        err = _jnp.square(y.astype(_jnp.float32) - loss_target)
        return 0.5 * _jnp.sum(_jnp.mean(err, axis=-1)) if err.ndim else 0.5 * err


def _adamw(w, g, m, v):
    m = ADAM_B1 * m + (1.0 - ADAM_B1) * g
    v = ADAM_B2 * v + (1.0 - ADAM_B2) * _jnp.square(g)
    m_hat = m / (1.0 - ADAM_B1 ** ADAM_STEP)
    v_hat = v / (1.0 - ADAM_B2 ** ADAM_STEP)
    delta = -ADAM_LR * (m_hat / (_jnp.sqrt(v_hat) + ADAM_EPS) + ADAM_WD * w)
    return delta, m, v


def reference(x, positions, mla_norm, mla_w_in, mla_g_cq, mla_g_ckv, mla_w_uq, mla_w_ukv, mla_w_o, conv_norm, conv_w_in, conv_w, conv_w_out, ffn_norm, ffn_w_gate, ffn_w_up, ffn_w_down, final_norm, loss_target, m_mla_norm, m_mla_w_in, m_mla_g_cq, m_mla_g_ckv, m_mla_w_uq, m_mla_w_ukv, m_mla_w_o, m_conv_norm, m_conv_w_in, m_conv_w, m_conv_w_out, m_ffn_norm, m_ffn_w_gate, m_ffn_w_up, m_ffn_w_down, m_final_norm, v_mla_norm, v_mla_w_in, v_mla_g_cq, v_mla_g_ckv, v_mla_w_uq, v_mla_w_ukv, v_mla_w_o, v_conv_norm, v_conv_w_in, v_conv_w, v_conv_w_out, v_ffn_norm, v_ffn_w_gate, v_ffn_w_up, v_ffn_w_down, v_final_norm):
    given = dict(x=x, positions=positions, mla_norm=mla_norm, mla_w_in=mla_w_in, mla_g_cq=mla_g_cq, mla_g_ckv=mla_g_ckv, mla_w_uq=mla_w_uq, mla_w_ukv=mla_w_ukv, mla_w_o=mla_w_o, conv_norm=conv_norm, conv_w_in=conv_w_in, conv_w=conv_w, conv_w_out=conv_w_out, ffn_norm=ffn_norm, ffn_w_gate=ffn_w_gate, ffn_w_up=ffn_w_up, ffn_w_down=ffn_w_down, final_norm=final_norm, loss_target=loss_target, m_mla_norm=m_mla_norm, m_mla_w_in=m_mla_w_in, m_mla_g_cq=m_mla_g_cq, m_mla_g_ckv=m_mla_g_ckv, m_mla_w_uq=m_mla_w_uq, m_mla_w_ukv=m_mla_w_ukv, m_mla_w_o=m_mla_w_o, m_conv_norm=m_conv_norm, m_conv_w_in=m_conv_w_in, m_conv_w=m_conv_w, m_conv_w_out=m_conv_w_out, m_ffn_norm=m_ffn_norm, m_ffn_w_gate=m_ffn_w_gate, m_ffn_w_up=m_ffn_w_up, m_ffn_w_down=m_ffn_w_down, m_final_norm=m_final_norm, v_mla_norm=v_mla_norm, v_mla_w_in=v_mla_w_in, v_mla_g_cq=v_mla_g_cq, v_mla_g_ckv=v_mla_g_ckv, v_mla_w_uq=v_mla_w_uq, v_mla_w_ukv=v_mla_w_ukv, v_mla_w_o=v_mla_w_o, v_conv_norm=v_conv_norm, v_conv_w_in=v_conv_w_in, v_conv_w=v_conv_w, v_conv_w_out=v_conv_w_out, v_ffn_norm=v_ffn_norm, v_ffn_w_gate=v_ffn_w_gate, v_ffn_w_up=v_ffn_w_up, v_ffn_w_down=v_ffn_w_down, v_final_norm=v_final_norm)
    weights = {n: given[n] for n in TWIN_WEIGHTS}
    shared = {n: given[n] for n in SHARED_INPUTS}
    per_example = {n: given[n] for n in ['x', 'positions']}
    grad_fn = _jax.value_and_grad(_loss, argnums=(0, 1))

    def one_microbatch(ex, loss_target):
        ex = dict(ex)
        diff = ex.pop(TWIN_DIFF_INPUT)
        return grad_fn(weights, diff, {**shared, **ex}, loss_target)

    if N_MICROBATCH == 1:
        loss, (grad_w, grad_x) = one_microbatch(per_example, given["loss_target"])
    else:
        def body(carry, xs):
            loss_sum, grad_sum = carry
            l_k, (gw_k, gx_k) = one_microbatch(xs[0], xs[1])
            with _jax.named_scope("update"):
                return (loss_sum + l_k, _jax.tree.map(_jnp.add, grad_sum, gw_k)), gx_k

        init = (_jnp.zeros((), _jnp.float32), _jax.tree.map(_jnp.zeros_like, weights))
        (loss, grad_w), grad_x = _jax.lax.scan(body, init, (per_example, given["loss_target"]))
    with _jax.named_scope("update"):
        delta_w, new_m, new_v = {}, {}, {}
        for n in TWIN_WEIGHTS:
            delta_w[n], new_m[n], new_v[n] = _adamw(weights[n], grad_w[n], given["m_" + n], given["v_" + n])
    return (loss, grad_x, *[grad_w[n] for n in TWIN_WEIGHTS], *[delta_w[n] for n in TWIN_WEIGHTS],
            *[new_m[n] for n in TWIN_WEIGHTS], *[new_v[n] for n in TWIN_WEIGHTS])
```

```python
import math

import jax
import jax.numpy as jnp
from jax import lax
from jax.experimental import pallas as pl
from jax.experimental.pallas import tpu as pltpu

F32 = jnp.float32
BF16 = jnp.bfloat16

N_DEV = 8
D_MODEL = 1024
N_HEADS = 8
NOPE = 128
ROPE = 64
V_DIM = 128
HEAD_PAD = 256
CQ = 512
CKV = 256
PROJ = CQ + CKV + ROPE
PROJ_PAD = CQ + CKV + 128
D_FF = 2816
FF_SHARD = D_FF // N_DEV
FF_SHARD_PAD = 384
FF_PAD = FF_SHARD_PAD * N_DEV
CHUNK_SHIFT = 6
RMS_EPS = 1e-6
ROPE_THETA = 10000.0
ATT_SCALE = 1.0 / math.sqrt(NOPE + ROPE)
NEG = -1e30

ADAM_LR = 0.001
ADAM_B1 = 0.9
ADAM_B2 = 0.999
ADAM_EPS = 1e-08
ADAM_WD = 0.01
ADAM_STEP = 10

SMALL_ROWS = 16

_NT = (((1,), (1,)), ((), ()))
_TN = (((0,), (0,)), ((), ()))


def _pc(body, *, name, out_shape, grid=(), in_specs=None, out_specs=None, scratch_shapes=(), vmem_mb=None):
    params = {}
    if vmem_mb is not None:
        params["vmem_limit_bytes"] = vmem_mb << 20
    kwargs = dict(
        name=name, out_shape=out_shape, grid=grid, scratch_shapes=scratch_shapes,
        compiler_params=pltpu.CompilerParams(**params),
    )
    if in_specs is not None:
        kwargs["in_specs"] = in_specs
    if out_specs is not None:
        kwargs["out_specs"] = out_specs
    return pl.pallas_call(body, **kwargs)


def _sds(shape, dtype):
    return jax.ShapeDtypeStruct(shape, dtype)


def _dot(a, b):
    return jnp.dot(a, b, preferred_element_type=F32)


def _dot_nt(a, b):
    return lax.dot_general(a, b, _NT, preferred_element_type=F32)


def _dot_tn(a, b):
    return lax.dot_general(a, b, _TN, preferred_element_type=F32)


def _rstd(x):
    return lax.rsqrt(jnp.mean(x * x, axis=-1, keepdims=True) + RMS_EPS)


def _rms(x, g):
    return (x * _rstd(x)) * g


def _rms_bwd(dy, x, g):
    r = _rstd(x)
    xhat = x * r
    dxhat = dy * g
    dx = r * (dxhat - xhat * jnp.mean(dxhat * xhat, axis=-1, keepdims=True))
    return dx, jnp.sum(dy * xhat, axis=0, keepdims=True)


def _rope(t, cc, sn, sp):
    return t * cc + pltpu.roll(t, 96, 1) * sn + pltpu.roll(t, 32, 1) * sp


def _rope_bwd(dt, cc, sn, sp):
    return dt * cc + pltpu.roll(dt * sn, 32, 1) + pltpu.roll(dt * sp, 96, 1)


def _row_block(s):
    return min(512, s)


def _mla_in_fwd(x, g0, w_in, g_cq, g_ckv, cc, sn, sp):
    s = x.shape[0]
    tm = _row_block(s)

    def body(x_ref, g0_ref, w_ref, gcq_ref, gckv_ref, cc_ref, sn_ref, sp_ref,
             n_ref, proj_ref, cqn_ref, ckvn_ref, krr_ref):
        nb = _rms(x_ref[...], g0_ref[...]).astype(BF16)
        n_ref[...] = nb
        proj = _dot(nb, w_ref[...])
        proj_ref[...] = proj
        cqn_ref[...] = _rms(proj[:, :CQ], gcq_ref[...]).astype(BF16)
        ckvn_ref[...] = _rms(proj[:, CQ:CQ + CKV], gckv_ref[...]).astype(BF16)
        krr_ref[...] = _rope(proj[:, CQ + CKV:], cc_ref[...], sn_ref[...], sp_ref[...]).astype(BF16)

    row = lambda n: pl.BlockSpec((tm, n), lambda i: (i, 0))
    full = lambda a: pl.BlockSpec(a.shape, lambda i: (0, 0))
    return _pc(
        body, name="mla_in_fwd", grid=(s // tm,),
        in_specs=[row(D_MODEL), full(g0), full(w_in), full(g_cq), full(g_ckv), row(128), row(128), row(128)],
        out_specs=[row(D_MODEL), row(PROJ_PAD), row(CQ), row(CKV), row(128)],
        out_shape=[_sds((s, D_MODEL), BF16), _sds((s, PROJ_PAD), F32), _sds((s, CQ), BF16),
                   _sds((s, CKV), BF16), _sds((s, 128), BF16)],
    )(x, g0, w_in, g_cq, g_ckv, cc, sn, sp)


def _qkv_proj(cqn, ckvn, krr, w_uq, w_ukv, cc, sn, sp):
    s = cqn.shape[0]
    tm = _row_block(s)

    def body(cqn_ref, ckvn_ref, krr_ref, wuq_ref, wukv_ref, cc_ref, sn_ref, sp_ref, q_ref, k_ref, v_ref):
        q = _dot(cqn_ref[...], wuq_ref[0])
        q_ref[0, :, :NOPE] = q[:, :NOPE].astype(BF16)
        q_ref[0, :, NOPE:] = _rope(q[:, NOPE:], cc_ref[...], sn_ref[...], sp_ref[...]).astype(BF16)
        kv = _dot(ckvn_ref[...], wukv_ref[0])
        k_ref[0, :, :NOPE] = kv[:, :NOPE].astype(BF16)
        k_ref[0, :, NOPE:] = krr_ref[...]
        v_ref[0] = kv[:, NOPE:].astype(BF16)

    row = lambda n: pl.BlockSpec((tm, n), lambda i, h: (i, 0))
    head_w = lambda a: pl.BlockSpec((1,) + a.shape[1:], lambda i, h: (h, 0, 0))
    head_o = lambda n: pl.BlockSpec((1, tm, n), lambda i, h: (h, i, 0))
    return _pc(
        body, name="qkv_proj", grid=(s // tm, N_HEADS),
        in_specs=[row(CQ), row(CKV), row(128), head_w(w_uq), head_w(w_ukv), row(128), row(128), row(128)],
        out_specs=[head_o(HEAD_PAD), head_o(HEAD_PAD), head_o(V_DIM)],
        out_shape=[_sds((N_HEADS, s, HEAD_PAD), BF16), _sds((N_HEADS, s, HEAD_PAD), BF16),
                   _sds((N_HEADS, s, V_DIM), BF16)],
    )(cqn, ckvn, krr, w_uq, w_ukv, cc, sn, sp)


def _chunk_mask(bq, bk):
    rows = lax.broadcasted_iota(jnp.int32, (bq, bk), 0)
    cols = lax.broadcasted_iota(jnp.int32, (bq, bk), 1)
    return jnp.right_shift(cols, CHUNK_SHIFT) <= jnp.right_shift(rows, CHUNK_SHIFT)


def _attn_fwd(q, k, v):
    s = q.shape[1]
    bq = _row_block(s)

    def body(q_ref, k_ref, v_ref, o_ref, lse_ref):
        i = pl.program_id(1)
        qb = q_ref[0]

        def step(j, carry, masked):
            m, l, acc = carry
            start = pl.multiple_of(j * bq, bq)
            kb = k_ref[0, pl.ds(start, bq), :]
            vb = v_ref[0, pl.ds(start, bq), :]
            sc = _dot_nt(qb, kb) * ATT_SCALE
            if masked:
                sc = jnp.where(_chunk_mask(bq, bq), sc, NEG)
            m_new = jnp.maximum(m, jnp.max(sc, axis=-1, keepdims=True))
            p = jnp.exp(sc - m_new)
            alpha = jnp.exp(m - m_new)
            l = alpha * l + jnp.sum(p, axis=-1, keepdims=True)
            acc = alpha * acc + _dot(p.astype(BF16), vb)
            return m_new, l, acc

        init = (jnp.full((bq, 1), NEG, F32), jnp.zeros((bq, 1), F32), jnp.zeros((bq, V_DIM), F32))
        carry = lax.fori_loop(0, i, lambda j, c: step(j, c, False), init)
        m, l, acc = step(i, carry, True)
        o_ref[...] = (acc / l).astype(BF16)
        lse_ref[0] = jnp.broadcast_to(m + jnp.log(l), (bq, 128))

    return _pc(
        body, name="attn_fwd", grid=(N_HEADS, s // bq),
        in_specs=[pl.BlockSpec((1, bq, HEAD_PAD), lambda h, i: (h, i, 0)),
                  pl.BlockSpec((1, s, HEAD_PAD), lambda h, i: (h, 0, 0)),
                  pl.BlockSpec((1, s, V_DIM), lambda h, i: (h, 0, 0))],
        out_specs=[pl.BlockSpec((bq, V_DIM), lambda h, i: (i, h)),
                   pl.BlockSpec((1, bq, 128), lambda h, i: (h, i, 0))],
        out_shape=[_sds((s, N_HEADS * V_DIM), BF16), _sds((N_HEADS, s, 128), F32)],
    )(q, k, v)


def _matmul_res(a, w, res, name):
    s, kd = a.shape
    n = w.shape[1]
    tm = _row_block(s)

    def body(a_ref, w_ref, r_ref, o_ref):
        o_ref[...] = r_ref[...] + _dot(a_ref[...], w_ref[...])

    return _pc(
        body, name=name, grid=(s // tm,),
        in_specs=[pl.BlockSpec((tm, kd), lambda i: (i, 0)), pl.BlockSpec((kd, n), lambda i: (0, 0)),
                  pl.BlockSpec((tm, n), lambda i: (i, 0))],
        out_specs=pl.BlockSpec((tm, n), lambda i: (i, 0)),
        out_shape=_sds((s, n), F32),
    )(a, w, res)


def _ffn_fwd(h, gain, wg, wu, wd, name):
    s = h.shape[0]
    tm = _row_block(s)
    tf = 512
    nf = FF_PAD // tf

    def body(h_ref, g_ref, wg_ref, wu_ref, wd_ref, o_ref, n_ref, gate_ref, up_ref, acc_ref):
        j = pl.program_id(1)

        @pl.when(j == 0)
        def _():
            n_ref[...] = _rms(h_ref[...], g_ref[...]).astype(BF16)
            acc_ref[...] = jnp.zeros_like(acc_ref)

        nb = n_ref[...]
        gate = _dot(nb, wg_ref[...])
        up = _dot(nb, wu_ref[...])
        gate_ref[...] = gate.astype(BF16)
        up_ref[...] = up.astype(BF16)
        act = gate * jax.nn.sigmoid(gate) * up
        acc_ref[...] += _dot(act.astype(BF16), wd_ref[...])

        @pl.when(j == nf - 1)
        def _():
            o_ref[...] = h_ref[...] + acc_ref[...]

    return _pc(
        body, name=name, grid=(s // tm, nf),
        in_specs=[pl.BlockSpec((tm, D_MODEL), lambda i, j: (i, 0)), pl.BlockSpec((1, D_MODEL), lambda i, j: (0, 0)),
                  pl.BlockSpec((D_MODEL, tf), lambda i, j: (0, j)), pl.BlockSpec((D_MODEL, tf), lambda i, j: (0, j)),
                  pl.BlockSpec((tf, D_MODEL), lambda i, j: (j, 0))],
        out_specs=[pl.BlockSpec((tm, D_MODEL), lambda i, j: (i, 0)), pl.BlockSpec((tm, D_MODEL), lambda i, j: (i, 0)),
                   pl.BlockSpec((tm, tf), lambda i, j: (i, j)), pl.BlockSpec((tm, tf), lambda i, j: (i, j))],
        out_shape=[_sds((s, D_MODEL), F32), _sds((s, D_MODEL), BF16), _sds((s, FF_PAD), BF16),
                   _sds((s, FF_PAD), BF16)],
        scratch_shapes=[pltpu.VMEM((tm, D_MODEL), F32)],
    )(h, gain, wg, wu, wd)


def _rms_matmul(h, gain, w, name):
    s = h.shape[0]
    n = w.shape[1]
    tm = _row_block(s)
    tn = 1024
    nn = n // tn

    def body(h_ref, g_ref, w_ref, n_ref, o_ref):
        @pl.when(pl.program_id(1) == 0)
        def _():
            n_ref[...] = _rms(h_ref[...], g_ref[...]).astype(BF16)

        o_ref[...] = _dot(n_ref[...], w_ref[...]).astype(BF16)

    return _pc(
        body, name=name, grid=(s // tm, nn),
        in_specs=[pl.BlockSpec((tm, D_MODEL), lambda i, j: (i, 0)), pl.BlockSpec((1, D_MODEL), lambda i, j: (0, 0)),
                  pl.BlockSpec((D_MODEL, tn), lambda i, j: (0, j))],
        out_specs=[pl.BlockSpec((tm, D_MODEL), lambda i, j: (i, 0)), pl.BlockSpec((tm, tn), lambda i, j: (i, j))],
        out_shape=[_sds((s, D_MODEL), BF16), _sds((s, n), BF16)],
    )(h, gain, w)


def _shift_down(u, k, rows):
    return jnp.where(rows >= k, pltpu.roll(u, k, 0), 0.0)


def _shift_up(u, k, rows, s):
    return jnp.where(rows < s - k, pltpu.roll(u, s - k, 0), 0.0)


_CONV_COLS = 128


def _conv_fwd(bcx, cw):
    s = bcx.shape[0]
    tc = _CONV_COLS
    nc = D_MODEL // tc

    def body(b_ref, c_ref, xp_ref, w_ref, o_ref):
        rows = lax.broadcasted_iota(jnp.int32, (s, tc), 0)
        u = c_ref[...].astype(F32) * xp_ref[...].astype(F32)
        w = w_ref[...]
        uc = w[2:3] * u + w[1:2] * _shift_down(u, 1, rows) + w[0:1] * _shift_down(u, 2, rows)
        o_ref[...] = (b_ref[...].astype(F32) * uc).astype(BF16)

    col = lambda off: pl.BlockSpec((s, tc), lambda j: (0, off + j))
    return _pc(
        body, name="conv_fwd", grid=(nc,),
        in_specs=[col(0), col(nc), col(2 * nc), pl.BlockSpec((3, tc), lambda j: (0, j))],
        out_specs=pl.BlockSpec((s, tc), lambda j: (0, j)),
        out_shape=_sds((s, D_MODEL), BF16),
    )(bcx, bcx, bcx, cw)


def _final_loss(h, gain, target):
    s = h.shape[0]
    tm = _row_block(s)

    def body(h_ref, g_ref, t_ref, dh_ref, dg_ref, loss_ref):
        i = pl.program_id(0)

        @pl.when(i == 0)
        def _():
            dg_ref[...] = jnp.zeros_like(dg_ref)
            loss_ref[...] = jnp.zeros_like(loss_ref)

        hb = h_ref[...]
        e = _rms(hb, g_ref[...]) - t_ref[...]
        loss_ref[...] += 0.5 * jnp.sum(jnp.mean(e * e, axis=-1, keepdims=True))
        dx, dg = _rms_bwd(e * (1.0 / D_MODEL), hb, g_ref[...])
        dh_ref[...] = dx
        dg_ref[...] += dg

    row = pl.BlockSpec((tm, D_MODEL), lambda i: (i, 0))
    vec = pl.BlockSpec((1, D_MODEL), lambda i: (0, 0))
    return _pc(
        body, name="final_loss", grid=(s // tm,),
        in_specs=[row, vec, row],
        out_specs=[row, vec, pl.BlockSpec((8, 128), lambda i: (0, 0))],
        out_shape=[_sds((s, D_MODEL), F32), _sds((1, D_MODEL), F32), _sds((8, 128), F32)],
    )(h, gain, target)


def _ffn_bwd_x(dho, h, gain, gate, up, wg, wu, wd, name):
    s = h.shape[0]
    tm = _row_block(s)
    tf = 512
    nf = FF_PAD // tf

    def body(dho_ref, h_ref, g_ref, gate_ref, up_ref, wg_ref, wu_ref, wd_ref,
             dh_ref, dhb_ref, dhob_ref, dgate_ref, dup_ref, act_ref, dgain_ref, acc_ref):
        i = pl.program_id(0)
        j = pl.program_id(1)

        @pl.when(j == 0)
        def _():
            dhob_ref[...] = dho_ref[...].astype(BF16)
            acc_ref[...] = jnp.zeros_like(acc_ref)

        @pl.when((i == 0) & (j == 0))
        def _():
            dgain_ref[...] = jnp.zeros_like(dgain_ref)

        dact = _dot_nt(dhob_ref[...], wd_ref[...])
        g = gate_ref[...].astype(F32)
        u = up_ref[...].astype(F32)
        sg = jax.nn.sigmoid(g)
        silu = g * sg
        dg = (dact * u * (sg * (1.0 + g * (1.0 - sg)))).astype(BF16)
        du = (dact * silu).astype(BF16)
        dgate_ref[...] = dg
        dup_ref[...] = du
        act_ref[...] = (silu * u).astype(BF16)
        acc_ref[...] += _dot_nt(dg, wg_ref[...]) + _dot_nt(du, wu_ref[...])

        @pl.when(j == nf - 1)
        def _():
            dx, dgain = _rms_bwd(acc_ref[...], h_ref[...], g_ref[...])
            dh = dho_ref[...] + dx
            dh_ref[...] = dh
            dhb_ref[...] = dh.astype(BF16)
            dgain_ref[...] += dgain

    row = pl.BlockSpec((tm, D_MODEL), lambda i, j: (i, 0))
    vec = pl.BlockSpec((1, D_MODEL), lambda i, j: (0, 0))
    hid = pl.BlockSpec((tm, tf), lambda i, j: (i, j))
    wcol = pl.BlockSpec((D_MODEL, tf), lambda i, j: (0, j))
    wrow = pl.BlockSpec((tf, D_MODEL), lambda i, j: (j, 0))
    return _pc(
        body, name=name, grid=(s // tm, nf),
        in_specs=[row, row, vec, hid, hid, wcol, wcol, wrow],
        out_specs=[row, row, row, hid, hid, hid, vec],
        out_shape=[_sds((s, D_MODEL), F32), _sds((s, D_MODEL), BF16), _sds((s, D_MODEL), BF16),
                   _sds((s, FF_PAD), BF16), _sds((s, FF_PAD), BF16), _sds((s, FF_PAD), BF16),
                   _sds((1, D_MODEL), F32)],
        scratch_shapes=[pltpu.VMEM((tm, D_MODEL), F32)],
    )(dho, h, gain, gate, up, wg, wu, wd)


def _nt_rmsbwd(a, w, h, gain, dho, name):
    s, kd = a.shape
    tm = _row_block(s)
    tk = kd if kd <= 1024 else 1024
    nk = kd // tk

    def body(a_ref, w_ref, h_ref, g_ref, dho_ref, dh_ref, dhb_ref, dgain_ref, acc_ref):
        i = pl.program_id(0)
        j = pl.program_id(1)

        @pl.when(j == 0)
        def _():
            acc_ref[...] = jnp.zeros_like(acc_ref)

        @pl.when((i == 0) & (j == 0))
        def _():
            dgain_ref[...] = jnp.zeros_like(dgain_ref)

        acc_ref[...] += _dot_nt(a_ref[...], w_ref[...])

        @pl.when(j == nk - 1)
        def _():
            dx, dgain = _rms_bwd(acc_ref[...], h_ref[...], g_ref[...])
            dh = dho_ref[...] + dx
            dh_ref[...] = dh
            dhb_ref[...] = dh.astype(BF16)
            dgain_ref[...] += dgain

    row = pl.BlockSpec((tm, D_MODEL), lambda i, j: (i, 0))
    vec = pl.BlockSpec((1, D_MODEL), lambda i, j: (0, 0))
    return _pc(
        body, name=name, grid=(s // tm, nk),
        in_specs=[pl.BlockSpec((tm, tk), lambda i, j: (i, j)), pl.BlockSpec((D_MODEL, tk), lambda i, j: (0, j)),
                  row, vec, row],
        out_specs=[row, row, vec],
        out_shape=[_sds((s, D_MODEL), F32), _sds((s, D_MODEL), BF16), _sds((1, D_MODEL), F32)],
        scratch_shapes=[pltpu.VMEM((tm, D_MODEL), F32)],
    )(a, w, h, gain, dho)


def _matmul_nt(a, w, name):
    s, kd = a.shape
    n = w.shape[0]
    tm = _row_block(s)

    def body(a_ref, w_ref, o_ref):
        o_ref[...] = _dot_nt(a_ref[...], w_ref[...]).astype(BF16)

    return _pc(
        body, name=name, grid=(s // tm,),
        in_specs=[pl.BlockSpec((tm, kd), lambda i: (i, 0)), pl.BlockSpec((n, kd), lambda i: (0, 0))],
        out_specs=pl.BlockSpec((tm, n), lambda i: (i, 0)),
        out_shape=_sds((s, n), BF16),
    )(a, w)


def _matmul_tn(a, b, name):
    s, m = a.shape
    n = b.shape[1]
    tmm = min(m, 1024)
    tn = n if n <= 1024 else 1024
    tk = _row_block(s)
    nk = s // tk

    def body(a_ref, b_ref, o_ref, acc_ref):
        k = pl.program_id(2)

        @pl.when(k == 0)
        def _():
            acc_ref[...] = jnp.zeros_like(acc_ref)

        acc_ref[...] += _dot_tn(a_ref[...], b_ref[...])

        @pl.when(k == nk - 1)
        def _():
            o_ref[...] = acc_ref[...].astype(BF16)

    return _pc(
        body, name=name, grid=(m // tmm, n // tn, nk),
        in_specs=[pl.BlockSpec((tk, tmm), lambda i, j, k: (k, i)), pl.BlockSpec((tk, tn), lambda i, j, k: (k, j))],
        out_specs=pl.BlockSpec((tmm, tn), lambda i, j, k: (i, j)),
        out_shape=_sds((m, n), BF16),
        scratch_shapes=[pltpu.VMEM((tmm, tn), F32)],
    )(a, b)


def _matmul_tn_heads(a, b, name):
    s, m = a.shape
    nh, _, n = b.shape
    tk = _row_block(s)
    nk = s // tk

    def body(a_ref, b_ref, o_ref, acc_ref):
        k = pl.program_id(1)

        @pl.when(k == 0)
        def _():
            acc_ref[...] = jnp.zeros_like(acc_ref)

        acc_ref[...] += _dot_tn(a_ref[...], b_ref[0])

        @pl.when(k == nk - 1)
        def _():
            o_ref[0] = acc_ref[...].astype(BF16)

    return _pc(
        body, name=name, grid=(nh, nk),
        in_specs=[pl.BlockSpec((tk, m), lambda h, k: (k, 0)), pl.BlockSpec((1, tk, n), lambda h, k: (h, k, 0))],
        out_specs=pl.BlockSpec((1, m, n), lambda h, k: (h, 0, 0)),
        out_shape=_sds((nh, m, n), BF16),
        scratch_shapes=[pltpu.VMEM((m, n), F32)],
    )(a, b)


def _conv_bwd(dbuc, bcx, cw):
    s = bcx.shape[0]
    tc = _CONV_COLS
    nc = D_MODEL // tc

    def body(d_ref, b_ref, c_ref, xp_ref, w_ref, db_ref, dc_ref, dxp_ref, dw_ref):
        rows = lax.broadcasted_iota(jnp.int32, (s, tc), 0)
        c = c_ref[...].astype(F32)
        xp = xp_ref[...].astype(F32)
        u = c * xp
        u1 = _shift_down(u, 1, rows)
        u2 = _shift_down(u, 2, rows)
        w = w_ref[...]
        uc = w[2:3] * u + w[1:2] * u1 + w[0:1] * u2
        d = d_ref[...].astype(F32)
        db_ref[...] = (d * uc).astype(BF16)
        duc = d * b_ref[...].astype(F32)
        du = w[2:3] * duc + w[1:2] * _shift_up(duc, 1, rows, s) + w[0:1] * _shift_up(duc, 2, rows, s)
        dc_ref[...] = (du * xp).astype(BF16)
        dxp_ref[...] = (du * c).astype(BF16)
        dw_ref[0:1, :] = jnp.sum(duc * u2, axis=0, keepdims=True)
        dw_ref[1:2, :] = jnp.sum(duc * u1, axis=0, keepdims=True)
        dw_ref[2:3, :] = jnp.sum(duc * u, axis=0, keepdims=True)

    col = lambda off: pl.BlockSpec((s, tc), lambda j: (0, off + j))
    tap = pl.BlockSpec((3, tc), lambda j: (0, j))
    return _pc(
        body, name="conv_bwd", grid=(nc,),
        in_specs=[col(0), col(0), col(nc), col(2 * nc), tap],
        out_specs=[col(0), col(0), col(0), tap],
        out_shape=[_sds((s, D_MODEL), BF16), _sds((s, D_MODEL), BF16), _sds((s, D_MODEL), BF16),
                   _sds((3, D_MODEL), F32)],
    )(dbuc, bcx, bcx, bcx, cw)


def _attn_bwd_dq(q, k, v, do, o, lse, cc, sn, sp):
    s = q.shape[1]
    bq = _row_block(s)

    def body(q_ref, k_ref, v_ref, do_ref, o_ref, lse_ref, cc_ref, sn_ref, sp_ref, dq_ref):
        i = pl.program_id(1)
        qb = q_ref[0]
        dob = do_ref[...]
        delta = jnp.sum(dob.astype(F32) * o_ref[...].astype(F32), axis=-1, keepdims=True)
        lse_col = lse_ref[0][:, :1]

        def step(j, dq, masked):
            start = pl.multiple_of(j * bq, bq)
            kb = k_ref[0, pl.ds(start, bq), :]
            vb = v_ref[0, pl.ds(start, bq), :]
            sc = _dot_nt(qb, kb) * ATT_SCALE
            if masked:
                sc = jnp.where(_chunk_mask(bq, bq), sc, NEG)
            p = jnp.exp(sc - lse_col)
            dp = _dot_nt(dob, vb)
            ds = (p * (dp - delta) * ATT_SCALE).astype(BF16)
            return dq + _dot(ds, kb)

        dq = lax.fori_loop(0, i, lambda j, c: step(j, c, False), jnp.zeros((bq, HEAD_PAD), F32))
        dq = step(i, dq, True)
        dq_ref[0, :, :NOPE] = dq[:, :NOPE].astype(BF16)
        dq_ref[0, :, NOPE:] = _rope_bwd(dq[:, NOPE:], cc_ref[...], sn_ref[...], sp_ref[...]).astype(BF16)

    blk = lambda n: pl.BlockSpec((1, bq, n), lambda h, i: (h, i, 0))
    whole = lambda n: pl.BlockSpec((1, s, n), lambda h, i: (h, 0, 0))
    cols = pl.BlockSpec((bq, V_DIM), lambda h, i: (i, h))
    tab = pl.BlockSpec((bq, 128), lambda h, i: (i, 0))
    return _pc(
        body, name="attn_bwd_dq", grid=(N_HEADS, s // bq),
        in_specs=[blk(HEAD_PAD), whole(HEAD_PAD), whole(V_DIM), cols, cols, blk(128), tab, tab, tab],
        out_specs=blk(HEAD_PAD),
        out_shape=_sds((N_HEADS, s, HEAD_PAD), BF16),
    )(q, k, v, do, o, lse, cc, sn, sp)


def _attn_bwd_dkv(q, k, v, do, o, lse):
    s = q.shape[1]
    bk = _row_block(s)
    nb = s // bk

    def body(q_ref, k_ref, v_ref, do_ref, o_ref, lse_ref, dk_ref, dv_ref):
        j = pl.program_id(1)
        kb = k_ref[0]
        vb = v_ref[0]

        def step(i, carry, masked):
            dk, dv = carry
            start = pl.multiple_of(i * bk, bk)
            qb = q_ref[0, pl.ds(start, bk), :]
            dob = do_ref[pl.ds(start, bk), :]
            ob = o_ref[pl.ds(start, bk), :]
            lse_col = lse_ref[0, pl.ds(start, bk), :][:, :1]
            delta = jnp.sum(dob.astype(F32) * ob.astype(F32), axis=-1, keepdims=True)
            sc = _dot_nt(qb, kb) * ATT_SCALE
            if masked:
                sc = jnp.where(_chunk_mask(bk, bk), sc, NEG)
            p = jnp.exp(sc - lse_col)
            dv = dv + _dot_tn(p.astype(BF16), dob)
            dp = _dot_nt(dob, vb)
            ds = (p * (dp - delta) * ATT_SCALE).astype(BF16)
            dk = dk + _dot_tn(ds, qb)
            return dk, dv

        carry = step(j, (jnp.zeros((bk, HEAD_PAD), F32), jnp.zeros((bk, V_DIM), F32)), True)
        dk, dv = lax.fori_loop(j + 1, nb, lambda i, c: step(i, c, False), carry)
        dk_ref[0] = dk.astype(BF16)
        dv_ref[0] = dv.astype(BF16)

    blk = lambda n: pl.BlockSpec((1, bk, n), lambda h, j: (h, j, 0))
    whole = lambda n: pl.BlockSpec((1, s, n), lambda h, j: (h, 0, 0))
    cols = pl.BlockSpec((s, V_DIM), lambda h, j: (0, h))
    return _pc(
        body, name="attn_bwd_dkv", grid=(N_HEADS, nb),
        in_specs=[whole(HEAD_PAD), blk(HEAD_PAD), blk(V_DIM), cols, cols, whole(128)],
        out_specs=[blk(HEAD_PAD), blk(V_DIM)],
        out_shape=[_sds((N_HEADS, s, HEAD_PAD), BF16), _sds((N_HEADS, s, V_DIM), BF16)],
    )(q, k, v, do, o, lse)


def _mla_mid_bwd(dq, dk, dv, proj, w_uq, w_ukv, g_cq, g_ckv, cc, sn, sp):
    s = dq.shape[1]
    tm = _row_block(s)

    def body(dq_ref, dk_ref, dv_ref, proj_ref, wuq_ref, wukv_ref, gcq_ref, gckv_ref, cc_ref, sn_ref, sp_ref,
             dproj_ref, dkv_ref, dgcq_ref, dgckv_ref, acq_ref, ackv_ref, akr_ref):
        i = pl.program_id(0)
        h = pl.program_id(1)

        @pl.when(h == 0)
        def _():
            acq_ref[...] = jnp.zeros_like(acq_ref)
            ackv_ref[...] = jnp.zeros_like(ackv_ref)
            akr_ref[...] = jnp.zeros_like(akr_ref)

        @pl.when((i == 0) & (h == 0))
        def _():
            dgcq_ref[...] = jnp.zeros_like(dgcq_ref)
            dgckv_ref[...] = jnp.zeros_like(dgckv_ref)

        dkb = dk_ref[0]
        dkv_ref[0, :, :NOPE] = dkb[:, :NOPE]
        dkv_ref[0, :, NOPE:] = dv_ref[0]
        acq_ref[...] += _dot_nt(dq_ref[0], wuq_ref[0])
        ackv_ref[...] += _dot_nt(dkv_ref[0], wukv_ref[0])
        akr_ref[...] += dkb[:, NOPE:].astype(F32)

        @pl.when(h == N_HEADS - 1)
        def _():
            proj = proj_ref[...]
            dcq, dgcq = _rms_bwd(acq_ref[...], proj[:, :CQ], gcq_ref[...])
            dckv, dgckv = _rms_bwd(ackv_ref[...], proj[:, CQ:CQ + CKV], gckv_ref[...])
            dproj_ref[:, :CQ] = dcq.astype(BF16)
            dproj_ref[:, CQ:CQ + CKV] = dckv.astype(BF16)
            dproj_ref[:, CQ + CKV:] = _rope_bwd(akr_ref[...], cc_ref[...], sn_ref[...], sp_ref[...]).astype(BF16)
            dgcq_ref[...] += dgcq
            dgckv_ref[...] += dgckv

    head_blk = lambda n: pl.BlockSpec((1, tm, n), lambda i, h: (h, i, 0))
    head_w = lambda a: pl.BlockSpec((1,) + a.shape[1:], lambda i, h: (h, 0, 0))
    row = lambda n: pl.BlockSpec((tm, n), lambda i, h: (i, 0))
    vec = lambda n: pl.BlockSpec((1, n), lambda i, h: (0, 0))
    return _pc(
        body, name="mla_mid_bwd", grid=(s // tm, N_HEADS),
        in_specs=[head_blk(HEAD_PAD), head_blk(HEAD_PAD), head_blk(V_DIM), row(PROJ_PAD), head_w(w_uq), head_w(w_ukv),
                  vec(CQ), vec(CKV), row(128), row(128), row(128)],
        out_specs=[row(PROJ_PAD), head_blk(HEAD_PAD), vec(CQ), vec(CKV)],
        out_shape=[_sds((s, PROJ_PAD), BF16), _sds((N_HEADS, s, HEAD_PAD), BF16), _sds((1, CQ), F32),
                   _sds((1, CKV), F32)],
        scratch_shapes=[pltpu.VMEM((tm, CQ), F32), pltpu.VMEM((tm, CKV), F32), pltpu.VMEM((tm, 128), F32)],
    )(dq, dk, dv, proj, w_uq, w_ukv, g_cq, g_ckv, cc, sn, sp)


def _peer(k):
    x, y, c = lax.axis_index("x"), lax.axis_index("y"), lax.axis_index("c")
    px = 1 - x if k & 4 else x
    py = 1 - y if k & 2 else y
    pc = 1 - c if k & 1 else c
    return (px, py, pc), 4 * px + 2 * py + pc


def _exchange(arrays, scatter, name):
    nw = len(arrays)

    def body(*refs):
        ins, outs = refs[:nw], refs[nw:2 * nw]
        send_sems, recv_sems, local_sems = refs[2 * nw:]
        _, me = _peer(0)

        def src(w, idx):
            return ins[w].at[idx] if scatter else ins[w]

        locals_ = [pltpu.make_async_copy(src(w, me), outs[w].at[me], local_sems.at[w]) for w in range(nw)]
        for cp in locals_:
            cp.start()
        copies = []
        for k in range(1, N_DEV):
            dev, idx = _peer(k)
            for w in range(nw):
                copies.append(pltpu.make_async_remote_copy(
                    src_ref=src(w, idx), dst_ref=outs[w].at[me],
                    send_sem=send_sems.at[w * N_DEV + k], recv_sem=recv_sems.at[w * N_DEV + k],
                    device_id=dev, device_id_type=pl.DeviceIdType.MESH))
        for cp in copies:
            cp.start()
        for cp in locals_:
            cp.wait()
        for cp in copies:
            cp.wait_send()
        for k in range(1, N_DEV):
            dev, idx = _peer(k)
            for w in range(nw):
                pltpu.make_async_remote_copy(
                    src_ref=src(w, idx), dst_ref=outs[w].at[idx],
                    send_sem=send_sems.at[w * N_DEV + k], recv_sem=recv_sems.at[w * N_DEV + k],
                    device_id=dev, device_id_type=pl.DeviceIdType.MESH).wait_recv()

    any_spec = pl.BlockSpec(memory_space=pl.ANY)
    out_shape = [_sds(a.shape if scatter else (N_DEV,) + a.shape, a.dtype) for a in arrays]
    return _pc(
        body, name=name,
        in_specs=[any_spec] * nw, out_specs=[any_spec] * nw, out_shape=out_shape,
        scratch_shapes=[pltpu.SemaphoreType.DMA((nw * N_DEV,)), pltpu.SemaphoreType.DMA((nw * N_DEV,)),
                        pltpu.SemaphoreType.DMA((nw,))],
    )(*arrays)


def _adam_math(g, w, m, v):
    m = ADAM_B1 * m + (1.0 - ADAM_B1) * g
    v = ADAM_B2 * v + (1.0 - ADAM_B2) * jnp.square(g)
    m_hat = m / (1.0 - ADAM_B1 ** ADAM_STEP)
    v_hat = v / (1.0 - ADAM_B2 ** ADAM_STEP)
    delta = -ADAM_LR * (m_hat / (jnp.sqrt(v_hat) + ADAM_EPS) + ADAM_WD * w)
    return delta, m, v


def _adam_rows(r):
    for t in (256, 128, 64, 32, 16):
        if r % t == 0:
            return t
    return r


def _adamw(parts, w, m, v, name):
    r, c = w.shape
    tr = _adam_rows(r)

    def body(p_ref, w_ref, m_ref, v_ref, g_ref, d_ref, mo_ref, vo_ref):
        g = p_ref[0].astype(F32)
        for src in range(1, N_DEV):
            g = g + p_ref[src].astype(F32)
        delta, m2, v2 = _adam_math(g, w_ref[...], m_ref[...], v_ref[...])
        g_ref[...] = g
        d_ref[...] = delta
        mo_ref[...] = m2
        vo_ref[...] = v2

    blk = pl.BlockSpec((tr, c), lambda i: (i, 0))
    return _pc(
        body, name=name, grid=(r // tr,),
        in_specs=[pl.BlockSpec((N_DEV, tr, c), lambda i: (0, i, 0)), blk, blk, blk],
        out_specs=[blk, blk, blk, blk],
        out_shape=[_sds((r, c), F32)] * 4,
    )(parts, w, m, v)


def _small_allreduce_adamw(gpack, wpack, mpack, vpack):
    shape = gpack.shape

    def body(g_ref, w_ref, m_ref, v_ref, go_ref, d_ref, mo_ref, vo_ref, gath_ref, send_sems, recv_sems):
        _, me = _peer(0)
        gath_ref[me] = g_ref[...]
        copies = []
        for k in range(1, N_DEV):
            dev, idx = _peer(k)
            copies.append(pltpu.make_async_remote_copy(
                src_ref=g_ref, dst_ref=gath_ref.at[me], send_sem=send_sems.at[k], recv_sem=recv_sems.at[k],
                device_id=dev, device_id_type=pl.DeviceIdType.MESH))
        for cp in copies:
            cp.start()
        for cp in copies:
            cp.wait_send()
        for k in range(1, N_DEV):
            dev, idx = _peer(k)
            pltpu.make_async_remote_copy(
                src_ref=g_ref, dst_ref=gath_ref.at[idx], send_sem=send_sems.at[k], recv_sem=recv_sems.at[k],
                device_id=dev, device_id_type=pl.DeviceIdType.MESH).wait_recv()
        g = gath_ref[0]
        for src in range(1, N_DEV):
            g = g + gath_ref[src]
        delta, m2, v2 = _adam_math(g, w_ref[...], m_ref[...], v_ref[...])
        go_ref[...] = g
        d_ref[...] = delta
        mo_ref[...] = m2
        vo_ref[...] = v2

    vm = pl.BlockSpec(memory_space=pltpu.VMEM)
    return _pc(
        body, name="small_allreduce_adamw",
        in_specs=[vm] * 4, out_specs=[vm] * 4, out_shape=[_sds(shape, F32)] * 4,
        scratch_shapes=[pltpu.VMEM((N_DEV,) + shape, F32), pltpu.SemaphoreType.DMA((N_DEV,)),
                        pltpu.SemaphoreType.DMA((N_DEV,))],
    )(gpack, wpack, mpack, vpack)


def _cols_from_shards(g, pad_to=None):
    if pad_to is not None and pad_to != g.shape[-1]:
        g = jnp.pad(g, ((0, 0), (0, 0), (0, pad_to - g.shape[-1])))
    return jnp.transpose(g, (1, 0, 2)).reshape(g.shape[1], N_DEV * g.shape[2])


def _shards_from_cols(a, n):
    k = a.shape[0]
    a = a.reshape(k, N_DEV, a.shape[1] // N_DEV)[:, :, :n]
    return jnp.transpose(a, (1, 0, 2))


def _rows_from_shards(g, pad_to=None):
    if pad_to is not None and pad_to != g.shape[1]:
        g = jnp.pad(g, ((0, 0), (0, pad_to - g.shape[1]), (0, 0)))
    return g.reshape(N_DEV * g.shape[1], g.shape[2])


def _shards_from_rows(a, r):
    n = a.shape[1]
    return a.reshape(N_DEV, a.shape[0] // N_DEV, n)[:, :r, :]


def _rope_tables(pos):
    inv_freq = 1.0 / (ROPE_THETA ** (jnp.arange(0, ROPE, 2, dtype=F32) / ROPE))
    ang = pos.astype(F32)[:, None] * inv_freq
    cos, sin = jnp.cos(ang), jnp.sin(ang)
    z32, z64, z96 = (jnp.zeros((pos.shape[0], n), F32) for n in (32, 64, 96))
    return (jnp.concatenate([cos, cos, z64], axis=1), jnp.concatenate([-sin, z96], axis=1),
            jnp.concatenate([z32, sin, z64], axis=1))


def _pad_row(vec):
    vec = vec.reshape(1, -1)
    return jnp.pad(vec, ((0, 0), (0, D_MODEL - vec.shape[1])))


def _local_step(x, target, tables, wts):
    cc, sn, sp = tables
    ffn_g = wts["ffn_norm"]

    n0, proj, cqn, ckvn, krr = _mla_in_fwd(x, wts["mla_norm"], wts["w_in"], wts["g_cq"], wts["g_ckv"], cc, sn, sp)
    q, k, v = _qkv_proj(cqn, ckvn, krr, wts["w_uq"], wts["w_ukv"], cc, sn, sp)
    o, lse = _attn_fwd(q, k, v)
    h1 = _matmul_res(o, wts["w_o"], x, "mla_out_fwd")
    h2, n1, gate0, up0 = _ffn_fwd(h1, ffn_g[0:1], wts["wg"][0], wts["wu"][0], wts["wd"][0], "ffn0_fwd")
    n2, bcx = _rms_matmul(h2, wts["conv_norm"], wts["c_in"], "conv_in_fwd")
    bu = _conv_fwd(bcx, wts["conv_w"])
    h3 = _matmul_res(bu, wts["c_out"], h2, "conv_out_fwd")
    h4, n3, gate1, up1 = _ffn_fwd(h3, ffn_g[1:2], wts["wg"][1], wts["wu"][1], wts["wd"][1], "ffn1_fwd")
    dh4, d_final, loss = _final_loss(h4, wts["final_norm"], target)

    grads = {}
    dh3, dh3_b, dh4_b, dgate, dup, act, d_ffn1 = _ffn_bwd_x(
        dh4, h3, ffn_g[1:2], gate1, up1, wts["wg"][1], wts["wu"][1], wts["wd"][1], "ffn1_bwd")
    dwg1 = _matmul_tn(n3, dgate, "ffn1_dwg")
    dwu1 = _matmul_tn(n3, dup, "ffn1_dwu")
    dwd1 = _matmul_tn(act, dh4_b, "ffn1_dwd")
    dbuc = _matmul_nt(dh3_b, wts["c_out"], "conv_out_bwd")
    grads["c_out"] = _matmul_tn(bu, dh3_b, "conv_dwout")
    db, dc, dxp, grads["conv_w"] = _conv_bwd(dbuc, bcx, wts["conv_w"])
    dbcx = jnp.concatenate([db, dc, dxp], axis=1)
    grads["c_in"] = _matmul_tn(n2, dbcx, "conv_dwin")
    dh2, dh2_b, grads["conv_norm"] = _nt_rmsbwd(dbcx, wts["c_in"], h2, wts["conv_norm"], dh3, "conv_in_bwd")
    dh1, dh1_b, dh2_b2, dgate, dup, act, d_ffn0 = _ffn_bwd_x(
        dh2, h1, ffn_g[0:1], gate0, up0, wts["wg"][0], wts["wu"][0], wts["wd"][0], "ffn0_bwd")
    dwg0 = _matmul_tn(n1, dgate, "ffn0_dwg")
    dwu0 = _matmul_tn(n1, dup, "ffn0_dwu")
    dwd0 = _matmul_tn(act, dh2_b2, "ffn0_dwd")
    grads["wg"] = (dwg0, dwg1)
    grads["wu"] = (dwu0, dwu1)
    grads["wd"] = (dwd0, dwd1)
    grads["ffn_norm"] = jnp.concatenate([d_ffn0, d_ffn1], axis=0)
    do = _matmul_nt(dh1_b, wts["w_o"], "mla_out_bwd")
    grads["w_o"] = _matmul_tn(o, dh1_b, "mla_dwo")
    dq = _attn_bwd_dq(q, k, v, do, o, lse, cc, sn, sp)
    dk, dv = _attn_bwd_dkv(q, k, v, do, o, lse)
    dproj, dkv, grads["g_cq"], grads["g_ckv"] = _mla_mid_bwd(
        dq, dk, dv, proj, wts["w_uq"], wts["w_ukv"], wts["g_cq"], wts["g_ckv"], cc, sn, sp)
    grads["w_uq"] = _matmul_tn_heads(cqn, dq, "mla_dwuq")
    grads["w_ukv"] = _matmul_tn_heads(ckvn, dkv, "mla_dwukv")
    grads["w_in"] = _matmul_tn(n0, dproj, "mla_dwin")
    dx, _, grads["mla_norm"] = _nt_rmsbwd(dproj, wts["w_in"], x, wts["mla_norm"], dh1, "mla_in_bwd")
    grads["final_norm"] = d_final
    return loss[0, 0], dx, grads


def kernel(x, positions, mla_norm, mla_w_in, mla_g_cq, mla_g_ckv, mla_w_uq, mla_w_ukv, mla_w_o, conv_norm, conv_w_in, conv_w, conv_w_out, ffn_norm, ffn_w_gate, ffn_w_up, ffn_w_down, final_norm, loss_target, m_mla_norm, m_mla_w_in, m_mla_g_cq, m_mla_g_ckv, m_mla_w_uq, m_mla_w_ukv, m_mla_w_o, m_conv_norm, m_conv_w_in, m_conv_w, m_conv_w_out, m_ffn_norm, m_ffn_w_gate, m_ffn_w_up, m_ffn_w_down, m_final_norm, v_mla_norm, v_mla_w_in, v_mla_g_cq, v_mla_g_ckv, v_mla_w_uq, v_mla_w_ukv, v_mla_w_o, v_conv_norm, v_conv_w_in, v_conv_w, v_conv_w_out, v_ffn_norm, v_ffn_w_gate, v_ffn_w_up, v_ffn_w_down, v_final_norm):
    me = 4 * lax.axis_index("x") + 2 * lax.axis_index("y") + lax.axis_index("c")

    big = dict(w_in=mla_w_in[0], w_uq=mla_w_uq[0], w_ukv=mla_w_ukv[0], w_o=mla_w_o[0], c_in=conv_w_in[0],
               c_out=conv_w_out[0], wg=ffn_w_gate, wu=ffn_w_up, wd=ffn_w_down)
    names = list(big)
    gathered = _exchange([big[n].astype(BF16) for n in names] + [conv_norm, conv_w[0]], False, "gather_weights")
    gw = dict(zip(names, gathered[:len(names)]))
    conv_norm_g, conv_w_g = gathered[len(names):]

    wts = dict(
        mla_norm=mla_norm, g_cq=mla_g_cq, g_ckv=mla_g_ckv, ffn_norm=ffn_norm, final_norm=final_norm.reshape(1, -1),
        w_in=jnp.pad(_rows_from_shards(gw["w_in"]), ((0, 0), (0, PROJ_PAD - PROJ))),
        w_uq=jnp.pad(gw["w_uq"], ((0, 0), (0, 0), (0, HEAD_PAD - NOPE - ROPE))),
        w_ukv=gw["w_ukv"],
        w_o=_rows_from_shards(gw["w_o"]),
        c_in=_cols_from_shards(gw["c_in"]),
        c_out=_rows_from_shards(gw["c_out"]),
        wg=[_cols_from_shards(gw["wg"][:, l], FF_SHARD_PAD) for l in range(2)],
        wu=[_cols_from_shards(gw["wu"][:, l], FF_SHARD_PAD) for l in range(2)],
        wd=[_rows_from_shards(gw["wd"][:, l], FF_SHARD_PAD) for l in range(2)],
        conv_norm=conv_norm_g.reshape(1, D_MODEL),
        conv_w=jnp.transpose(conv_w_g, (1, 0, 2)).reshape(3, D_MODEL),
    )

    loss_local, dx, grads = _local_step(x[0], loss_target[0], _rope_tables(positions[0]), wts)
    loss = lax.psum(loss_local, ("x", "y", "c"))

    send = dict(
        w_in=_shards_from_rows(grads["w_in"][:, :PROJ], D_MODEL // N_DEV),
        w_uq=grads["w_uq"][:, :, :NOPE + ROPE],
        w_ukv=grads["w_ukv"],
        w_o=_shards_from_rows(grads["w_o"], D_MODEL // N_DEV),
        c_in=_shards_from_cols(grads["c_in"], 3 * D_MODEL // N_DEV),
        c_out=_shards_from_rows(grads["c_out"], D_MODEL // N_DEV),
        wg=jnp.stack([_shards_from_cols(g, FF_SHARD) for g in grads["wg"]], axis=1),
        wu=jnp.stack([_shards_from_cols(g, FF_SHARD) for g in grads["wu"]], axis=1),
        wd=jnp.stack([_shards_from_rows(g, FF_SHARD) for g in grads["wd"]], axis=1),
    )
    parts = dict(zip(names, _exchange([send[n] for n in names], True, "scatter_grads")))

    moments = dict(w_in=(m_mla_w_in, v_mla_w_in), w_uq=(m_mla_w_uq, v_mla_w_uq), w_ukv=(m_mla_w_ukv, v_mla_w_ukv),
                   w_o=(m_mla_w_o, v_mla_w_o), c_in=(m_conv_w_in, v_conv_w_in), c_out=(m_conv_w_out, v_conv_w_out),
                   wg=(m_ffn_w_gate, v_ffn_w_gate), wu=(m_ffn_w_up, v_ffn_w_up), wd=(m_ffn_w_down, v_ffn_w_down))
    full = dict(w_in=mla_w_in, w_uq=mla_w_uq, w_ukv=mla_w_ukv, w_o=mla_w_o, c_in=conv_w_in, c_out=conv_w_out,
                wg=ffn_w_gate, wu=ffn_w_up, wd=ffn_w_down)
    res = {}
    for n in names:
        w = full[n]
        shape = w.shape
        c = shape[-1]
        flat = lambda a: a.reshape(-1, c)
        outs = _adamw(parts[n].reshape(N_DEV, -1, c), flat(w), flat(moments[n][0]), flat(moments[n][1]), "adamw_" + n)
        res[n] = [o.reshape(shape) for o in outs]

    col0 = me * (D_MODEL // N_DEV)
    zeros_row = jnp.zeros((1, D_MODEL), F32)

    def place(shard):
        return lax.dynamic_update_slice(jnp.zeros((shard.shape[0], D_MODEL), F32), shard, (0, col0))

    def pack(mla_n, g_cq, g_ckv, ffn_n, fin_n, conv_n, conv_taps):
        rows = [mla_n, _pad_row(g_cq), _pad_row(g_ckv), ffn_n, fin_n.reshape(1, -1), conv_n, conv_taps]
        rows.append(jnp.zeros((SMALL_ROWS - sum(r.shape[0] for r in rows), D_MODEL), F32))
        return jnp.concatenate(rows, axis=0)

    gpack = pack(grads["mla_norm"], grads["g_cq"], grads["g_ckv"], grads["ffn_norm"], grads["final_norm"],
                 grads["conv_norm"], grads["conv_w"])
    wpack = pack(mla_norm, mla_g_cq, mla_g_ckv, ffn_norm, final_norm, place(conv_norm), place(conv_w[0]))
    mpack = pack(m_mla_norm, m_mla_g_cq, m_mla_g_ckv, m_ffn_norm, m_final_norm, place(m_conv_norm), place(m_conv_w[0]))
    vpack = pack(v_mla_norm, v_mla_g_cq, v_mla_g_ckv, v_ffn_norm, v_final_norm, place(v_conv_norm), place(v_conv_w[0]))
    small = _small_allreduce_adamw(gpack, wpack, mpack, vpack)

    def unpack(p):
        own = lambda rows: lax.dynamic_slice(rows, (0, col0), (rows.shape[0], D_MODEL // N_DEV))
        return dict(mla_norm=p[0:1], g_cq=p[1:2, :CQ], g_ckv=p[2:3, :CKV], ffn_norm=p[3:5], final_norm=p[5],
                    conv_norm=own(p[6:7]), conv_w=own(p[7:10])[None])

    small = [unpack(p) for p in small]
    order = ["mla_norm", "w_in", "g_cq", "g_ckv", "w_uq", "w_ukv", "w_o", "conv_norm", "c_in", "conv_w", "c_out",
             "ffn_norm", "wg", "wu", "wd", "final_norm"]
    out = [loss, dx[None]]
    for kind in range(4):
        for n in order:
            out.append(res[n][kind] if n in res else small[kind][n])
    return tuple(out)
```

```python
import math

import jax
import jax.numpy as jnp
from jax import lax
from jax.experimental import pallas as pl
from jax.experimental.pallas import tpu as pltpu

F32 = jnp.float32
BF16 = jnp.bfloat16

N_DEV = 8
D_MODEL = 1024
N_HEADS = 8
NOPE = 128
ROPE = 64
V_DIM = 128
HEAD_PAD = 256
CQ = 512
CKV = 256
PROJ = CQ + CKV + ROPE
PROJ_PAD = CQ + CKV + 128
D_FF = 2816
FF_SHARD = D_FF // N_DEV
FF_SHARD_PAD = 384
FF_PAD = FF_SHARD_PAD * N_DEV
CHUNK_SHIFT = 6
RMS_EPS = 1e-6
ROPE_THETA = 10000.0
ATT_SCALE = 1.0 / math.sqrt(NOPE + ROPE)
NEG = -1e30

ADAM_LR = 0.001
ADAM_B1 = 0.9
ADAM_B2 = 0.999
ADAM_EPS = 1e-08
ADAM_WD = 0.01
ADAM_STEP = 10

SMALL_ROWS = 16

_NT = (((1,), (1,)), ((), ()))
_TN = (((0,), (0,)), ((), ()))


def _pc(body, *, name, out_shape, grid=(), in_specs=None, out_specs=None, scratch_shapes=(), vmem_mb=None):
    params = {}
    if vmem_mb is not None:
        params["vmem_limit_bytes"] = vmem_mb << 20
    kwargs = dict(
        name=name, out_shape=out_shape, grid=grid, scratch_shapes=scratch_shapes,
        compiler_params=pltpu.CompilerParams(**params),
    )
    if in_specs is not None:
        kwargs["in_specs"] = in_specs
    if out_specs is not None:
        kwargs["out_specs"] = out_specs
    return pl.pallas_call(body, **kwargs)


def _pc_carrying(body, carry, operands, *, name, out_shape, grid, in_specs, out_specs, scratch_shapes=()):
    if carry is None:
        return _pc(body, name=name, out_shape=out_shape, grid=grid, in_specs=in_specs, out_specs=out_specs,
                   scratch_shapes=scratch_shapes)(*operands), None
    arrays, scatter = carry
    nw, n_in, n_out, n_scr = len(arrays), len(in_specs), len(out_shape), len(scratch_shapes)
    start, wait, landed_shapes, sems = _exchange_ops(arrays, scatter)

    def wrapped(*refs):
        ins, rest = refs[:n_in], refs[n_in:]
        cin, rest = rest[:nw], rest[nw:]
        outs, rest = rest[:n_out], rest[n_out:]
        cout, rest = rest[:nw], rest[nw:]
        scr, csem = rest[:n_scr], rest[n_scr:]
        ids = [pl.program_id(a) for a in range(len(grid))]
        first = ids[0] == 0
        last = ids[0] == grid[0] - 1
        for a in range(1, len(grid)):
            first = first & (ids[a] == 0)
            last = last & (ids[a] == grid[a] - 1)

        @pl.when(first)
        def _():
            start(cin, cout, *csem)

        body(*ins, *outs, *scr)

        @pl.when(last)
        def _():
            wait(cin, cout, *csem)

    any_spec = pl.BlockSpec(memory_space=pl.ANY)
    res = _pc(
        wrapped, name=name, grid=grid,
        in_specs=list(in_specs) + [any_spec] * nw, out_specs=list(out_specs) + [any_spec] * nw,
        out_shape=list(out_shape) + landed_shapes, scratch_shapes=list(scratch_shapes) + sems,
    )(*operands, *arrays)
    return res[:n_out], res[n_out:]


def _sds(shape, dtype):
    return jax.ShapeDtypeStruct(shape, dtype)


def _dot(a, b):
    return jnp.dot(a, b, preferred_element_type=F32)


def _dot_nt(a, b):
    return lax.dot_general(a, b, _NT, preferred_element_type=F32)


def _dot_tn(a, b):
    return lax.dot_general(a, b, _TN, preferred_element_type=F32)


def _rstd(x):
    return lax.rsqrt(jnp.mean(x * x, axis=-1, keepdims=True) + RMS_EPS)


def _rms(x, g):
    return (x * _rstd(x)) * g


def _rms_bwd(dy, x, g):
    r = _rstd(x)
    xhat = x * r
    dxhat = dy * g
    dx = r * (dxhat - xhat * jnp.mean(dxhat * xhat, axis=-1, keepdims=True))
    return dx, jnp.sum(dy * xhat, axis=0, keepdims=True)


def _rope(t, cc, sn, sp):
    return t * cc + pltpu.roll(t, 96, 1) * sn + pltpu.roll(t, 32, 1) * sp


def _rope_bwd(dt, cc, sn, sp):
    return dt * cc + pltpu.roll(dt * sn, 32, 1) + pltpu.roll(dt * sp, 96, 1)


def _row_block(s):
    return min(512, s)


def _mla_in_fwd(x, g0, w_in, g_cq, g_ckv, cc, sn, sp):
    s = x.shape[0]
    tm = _row_block(s)

    def body(x_ref, g0_ref, w_ref, gcq_ref, gckv_ref, cc_ref, sn_ref, sp_ref,
             n_ref, proj_ref, cqn_ref, ckvn_ref, krr_ref):
        nb = _rms(x_ref[...], g0_ref[...]).astype(BF16)
        n_ref[...] = nb
        proj = _dot(nb, w_ref[...])
        proj_ref[...] = proj
        cqn_ref[...] = _rms(proj[:, :CQ], gcq_ref[...]).astype(BF16)
        ckvn_ref[...] = _rms(proj[:, CQ:CQ + CKV], gckv_ref[...]).astype(BF16)
        krr_ref[...] = _rope(proj[:, CQ + CKV:], cc_ref[...], sn_ref[...], sp_ref[...]).astype(BF16)

    row = lambda n: pl.BlockSpec((tm, n), lambda i: (i, 0))
    full = lambda a: pl.BlockSpec(a.shape, lambda i: (0, 0))
    return _pc(
        body, name="mla_in_fwd", grid=(s // tm,),
        in_specs=[row(D_MODEL), full(g0), full(w_in), full(g_cq), full(g_ckv), row(128), row(128), row(128)],
        out_specs=[row(D_MODEL), row(PROJ_PAD), row(CQ), row(CKV), row(128)],
        out_shape=[_sds((s, D_MODEL), BF16), _sds((s, PROJ_PAD), F32), _sds((s, CQ), BF16),
                   _sds((s, CKV), BF16), _sds((s, 128), BF16)],
    )(x, g0, w_in, g_cq, g_ckv, cc, sn, sp)


def _qkv_proj(cqn, ckvn, krr, w_uq, w_ukv, cc, sn, sp):
    s = cqn.shape[0]
    tm = _row_block(s)

    def body(cqn_ref, ckvn_ref, krr_ref, wuq_ref, wukv_ref, cc_ref, sn_ref, sp_ref, q_ref, k_ref, v_ref):
        q = _dot(cqn_ref[...], wuq_ref[0])
        q_ref[0, :, :NOPE] = q[:, :NOPE].astype(BF16)
        q_ref[0, :, NOPE:] = _rope(q[:, NOPE:], cc_ref[...], sn_ref[...], sp_ref[...]).astype(BF16)
        kv = _dot(ckvn_ref[...], wukv_ref[0])
        k_ref[0, :, :NOPE] = kv[:, :NOPE].astype(BF16)
        k_ref[0, :, NOPE:] = krr_ref[...]
        v_ref[0] = kv[:, NOPE:].astype(BF16)

    row = lambda n: pl.BlockSpec((tm, n), lambda i, h: (i, 0))
    head_w = lambda a: pl.BlockSpec((1,) + a.shape[1:], lambda i, h: (h, 0, 0))
    head_o = lambda n: pl.BlockSpec((1, tm, n), lambda i, h: (h, i, 0))
    return _pc(
        body, name="qkv_proj", grid=(s // tm, N_HEADS),
        in_specs=[row(CQ), row(CKV), row(128), head_w(w_uq), head_w(w_ukv), row(128), row(128), row(128)],
        out_specs=[head_o(HEAD_PAD), head_o(HEAD_PAD), head_o(V_DIM)],
        out_shape=[_sds((N_HEADS, s, HEAD_PAD), BF16), _sds((N_HEADS, s, HEAD_PAD), BF16),
                   _sds((N_HEADS, s, V_DIM), BF16)],
    )(cqn, ckvn, krr, w_uq, w_ukv, cc, sn, sp)


def _chunk_mask(bq, bk):
    rows = lax.broadcasted_iota(jnp.int32, (bq, bk), 0)
    cols = lax.broadcasted_iota(jnp.int32, (bq, bk), 1)
    return jnp.right_shift(cols, CHUNK_SHIFT) <= jnp.right_shift(rows, CHUNK_SHIFT)


def _attn_fwd(q, k, v, carry=None):
    s = q.shape[1]
    bq = _row_block(s)

    def body(q_ref, k_ref, v_ref, o_ref, lse_ref):
        i = pl.program_id(1)
        qb = q_ref[0]

        def step(j, carry, masked):
            m, l, acc = carry
            start = pl.multiple_of(j * bq, bq)
            kb = k_ref[0, pl.ds(start, bq), :]
            vb = v_ref[0, pl.ds(start, bq), :]
            sc = _dot_nt(qb, kb) * ATT_SCALE
            if masked:
                sc = jnp.where(_chunk_mask(bq, bq), sc, NEG)
            m_new = jnp.maximum(m, jnp.max(sc, axis=-1, keepdims=True))
            p = jnp.exp(sc - m_new)
            alpha = jnp.exp(m - m_new)
            l = alpha * l + jnp.sum(p, axis=-1, keepdims=True)
            acc = alpha * acc + _dot(p.astype(BF16), vb)
            return m_new, l, acc

        init = (jnp.full((bq, 1), NEG, F32), jnp.zeros((bq, 1), F32), jnp.zeros((bq, V_DIM), F32))
        carry = lax.fori_loop(0, i, lambda j, c: step(j, c, False), init)
        m, l, acc = step(i, carry, True)
        o_ref[...] = (acc / l).astype(BF16)
        lse_ref[0] = jnp.broadcast_to(m + jnp.log(l), (bq, 128))

    return _pc_carrying(
        body, carry, (q, k, v), name="attn_fwd", grid=(N_HEADS, s // bq),
        in_specs=[pl.BlockSpec((1, bq, HEAD_PAD), lambda h, i: (h, i, 0)),
                  pl.BlockSpec((1, s, HEAD_PAD), lambda h, i: (h, 0, 0)),
                  pl.BlockSpec((1, s, V_DIM), lambda h, i: (h, 0, 0))],
        out_specs=[pl.BlockSpec((bq, V_DIM), lambda h, i: (i, h)),
                   pl.BlockSpec((1, bq, 128), lambda h, i: (h, i, 0))],
        out_shape=[_sds((s, N_HEADS * V_DIM), BF16), _sds((N_HEADS, s, 128), F32)],
    )


def _matmul_res(a, w, res, name):
    s, kd = a.shape
    n = w.shape[1]
    tm = _row_block(s)

    def body(a_ref, w_ref, r_ref, o_ref):
        o_ref[...] = r_ref[...] + _dot(a_ref[...], w_ref[...])

    return _pc(
        body, name=name, grid=(s // tm,),
        in_specs=[pl.BlockSpec((tm, kd), lambda i: (i, 0)), pl.BlockSpec((kd, n), lambda i: (0, 0)),
                  pl.BlockSpec((tm, n), lambda i: (i, 0))],
        out_specs=pl.BlockSpec((tm, n), lambda i: (i, 0)),
        out_shape=_sds((s, n), F32),
    )(a, w, res)


def _ffn_fwd(h, gain, wg, wu, wd, name, carry=None):
    s = h.shape[0]
    tm = _row_block(s)
    tf = 512
    nf = FF_PAD // tf

    def body(h_ref, g_ref, wg_ref, wu_ref, wd_ref, o_ref, n_ref, gate_ref, up_ref, acc_ref):
        j = pl.program_id(1)

        @pl.when(j == 0)
        def _():
            n_ref[...] = _rms(h_ref[...], g_ref[...]).astype(BF16)
            acc_ref[...] = jnp.zeros_like(acc_ref)

        nb = n_ref[...]
        gate = _dot(nb, wg_ref[...])
        up = _dot(nb, wu_ref[...])
        gate_ref[...] = gate.astype(BF16)
        up_ref[...] = up.astype(BF16)
        act = gate * jax.nn.sigmoid(gate) * up
        acc_ref[...] += _dot(act.astype(BF16), wd_ref[...])

        @pl.when(j == nf - 1)
        def _():
            o_ref[...] = h_ref[...] + acc_ref[...]

    return _pc_carrying(
        body, carry, (h, gain, wg, wu, wd), name=name, grid=(s // tm, nf),
        in_specs=[pl.BlockSpec((tm, D_MODEL), lambda i, j: (i, 0)), pl.BlockSpec((1, D_MODEL), lambda i, j: (0, 0)),
                  pl.BlockSpec((D_MODEL, tf), lambda i, j: (0, j)), pl.BlockSpec((D_MODEL, tf), lambda i, j: (0, j)),
                  pl.BlockSpec((tf, D_MODEL), lambda i, j: (j, 0))],
        out_specs=[pl.BlockSpec((tm, D_MODEL), lambda i, j: (i, 0)), pl.BlockSpec((tm, D_MODEL), lambda i, j: (i, 0)),
                   pl.BlockSpec((tm, tf), lambda i, j: (i, j)), pl.BlockSpec((tm, tf), lambda i, j: (i, j))],
        out_shape=[_sds((s, D_MODEL), F32), _sds((s, D_MODEL), BF16), _sds((s, FF_PAD), BF16),
                   _sds((s, FF_PAD), BF16)],
        scratch_shapes=[pltpu.VMEM((tm, D_MODEL), F32)],
    )


def _rms_matmul(h, gain, w, name):
    s = h.shape[0]
    n = w.shape[1]
    tm = _row_block(s)
    tn = 1024
    nn = n // tn

    def body(h_ref, g_ref, w_ref, n_ref, o_ref):
        @pl.when(pl.program_id(1) == 0)
        def _():
            n_ref[...] = _rms(h_ref[...], g_ref[...]).astype(BF16)

        o_ref[...] = _dot(n_ref[...], w_ref[...]).astype(BF16)

    return _pc(
        body, name=name, grid=(s // tm, nn),
        in_specs=[pl.BlockSpec((tm, D_MODEL), lambda i, j: (i, 0)), pl.BlockSpec((1, D_MODEL), lambda i, j: (0, 0)),
                  pl.BlockSpec((D_MODEL, tn), lambda i, j: (0, j))],
        out_specs=[pl.BlockSpec((tm, D_MODEL), lambda i, j: (i, 0)), pl.BlockSpec((tm, tn), lambda i, j: (i, j))],
        out_shape=[_sds((s, D_MODEL), BF16), _sds((s, n), BF16)],
    )(h, gain, w)


def _shift_down(u, k, rows):
    return jnp.where(rows >= k, pltpu.roll(u, k, 0), 0.0)


def _shift_up(u, k, rows, s):
    return jnp.where(rows < s - k, pltpu.roll(u, s - k, 0), 0.0)


_CONV_COLS = 128


def _conv_fwd(bcx, cw):
    s = bcx.shape[0]
    tc = _CONV_COLS
    nc = D_MODEL // tc

    def body(b_ref, c_ref, xp_ref, w_ref, o_ref):
        rows = lax.broadcasted_iota(jnp.int32, (s, tc), 0)
        u = c_ref[...].astype(F32) * xp_ref[...].astype(F32)
        w = w_ref[...]
        uc = w[2:3] * u + w[1:2] * _shift_down(u, 1, rows) + w[0:1] * _shift_down(u, 2, rows)
        o_ref[...] = (b_ref[...].astype(F32) * uc).astype(BF16)

    col = lambda off: pl.BlockSpec((s, tc), lambda j: (0, off + j))
    return _pc(
        body, name="conv_fwd", grid=(nc,),
        in_specs=[col(0), col(nc), col(2 * nc), pl.BlockSpec((3, tc), lambda j: (0, j))],
        out_specs=pl.BlockSpec((s, tc), lambda j: (0, j)),
        out_shape=_sds((s, D_MODEL), BF16),
    )(bcx, bcx, bcx, cw)


def _final_loss(h, gain, target):
    s = h.shape[0]
    tm = _row_block(s)

    def body(h_ref, g_ref, t_ref, dh_ref, dg_ref, loss_ref):
        i = pl.program_id(0)

        @pl.when(i == 0)
        def _():
            dg_ref[...] = jnp.zeros_like(dg_ref)
            loss_ref[...] = jnp.zeros_like(loss_ref)

        hb = h_ref[...]
        e = _rms(hb, g_ref[...]) - t_ref[...]
        loss_ref[...] += 0.5 * jnp.sum(jnp.mean(e * e, axis=-1, keepdims=True))
        dx, dg = _rms_bwd(e * (1.0 / D_MODEL), hb, g_ref[...])
        dh_ref[...] = dx
        dg_ref[...] += dg

    row = pl.BlockSpec((tm, D_MODEL), lambda i: (i, 0))
    vec = pl.BlockSpec((1, D_MODEL), lambda i: (0, 0))
    return _pc(
        body, name="final_loss", grid=(s // tm,),
        in_specs=[row, vec, row],
        out_specs=[row, vec, pl.BlockSpec((8, 128), lambda i: (0, 0))],
        out_shape=[_sds((s, D_MODEL), F32), _sds((1, D_MODEL), F32), _sds((8, 128), F32)],
    )(h, gain, target)


def _ffn_bwd_x(dho, h, gain, gate, up, wg, wu, wd, name):
    s = h.shape[0]
    tm = _row_block(s)
    tf = 512
    nf = FF_PAD // tf

    def body(dho_ref, h_ref, g_ref, gate_ref, up_ref, wg_ref, wu_ref, wd_ref,
             dh_ref, dhb_ref, dhob_ref, dgate_ref, dup_ref, act_ref, dgain_ref, acc_ref):
        i = pl.program_id(0)
        j = pl.program_id(1)

        @pl.when(j == 0)
        def _():
            dhob_ref[...] = dho_ref[...].astype(BF16)
            acc_ref[...] = jnp.zeros_like(acc_ref)

        @pl.when((i == 0) & (j == 0))
        def _():
            dgain_ref[...] = jnp.zeros_like(dgain_ref)

        dact = _dot_nt(dhob_ref[...], wd_ref[...])
        g = gate_ref[...].astype(F32)
        u = up_ref[...].astype(F32)
        sg = jax.nn.sigmoid(g)
        silu = g * sg
        dg = (dact * u * (sg * (1.0 + g * (1.0 - sg)))).astype(BF16)
        du = (dact * silu).astype(BF16)
        dgate_ref[...] = dg
        dup_ref[...] = du
        act_ref[...] = (silu * u).astype(BF16)
        acc_ref[...] += _dot_nt(dg, wg_ref[...]) + _dot_nt(du, wu_ref[...])

        @pl.when(j == nf - 1)
        def _():
            dx, dgain = _rms_bwd(acc_ref[...], h_ref[...], g_ref[...])
            dh = dho_ref[...] + dx
            dh_ref[...] = dh
            dhb_ref[...] = dh.astype(BF16)
            dgain_ref[...] += dgain

    row = pl.BlockSpec((tm, D_MODEL), lambda i, j: (i, 0))
    vec = pl.BlockSpec((1, D_MODEL), lambda i, j: (0, 0))
    hid = pl.BlockSpec((tm, tf), lambda i, j: (i, j))
    wcol = pl.BlockSpec((D_MODEL, tf), lambda i, j: (0, j))
    wrow = pl.BlockSpec((tf, D_MODEL), lambda i, j: (j, 0))
    return _pc(
        body, name=name, grid=(s // tm, nf),
        in_specs=[row, row, vec, hid, hid, wcol, wcol, wrow],
        out_specs=[row, row, row, hid, hid, hid, vec],
        out_shape=[_sds((s, D_MODEL), F32), _sds((s, D_MODEL), BF16), _sds((s, D_MODEL), BF16),
                   _sds((s, FF_PAD), BF16), _sds((s, FF_PAD), BF16), _sds((s, FF_PAD), BF16),
                   _sds((1, D_MODEL), F32)],
        scratch_shapes=[pltpu.VMEM((tm, D_MODEL), F32)],
    )(dho, h, gain, gate, up, wg, wu, wd)


def _nt_rmsbwd(a, w, h, gain, dho, name):
    s, kd = a.shape
    tm = _row_block(s)
    tk = kd if kd <= 1024 else 1024
    nk = kd // tk

    def body(a_ref, w_ref, h_ref, g_ref, dho_ref, dh_ref, dhb_ref, dgain_ref, acc_ref):
        i = pl.program_id(0)
        j = pl.program_id(1)

        @pl.when(j == 0)
        def _():
            acc_ref[...] = jnp.zeros_like(acc_ref)

        @pl.when((i == 0) & (j == 0))
        def _():
            dgain_ref[...] = jnp.zeros_like(dgain_ref)

        acc_ref[...] += _dot_nt(a_ref[...], w_ref[...])

        @pl.when(j == nk - 1)
        def _():
            dx, dgain = _rms_bwd(acc_ref[...], h_ref[...], g_ref[...])
            dh = dho_ref[...] + dx
            dh_ref[...] = dh
            dhb_ref[...] = dh.astype(BF16)
            dgain_ref[...] += dgain

    row = pl.BlockSpec((tm, D_MODEL), lambda i, j: (i, 0))
    vec = pl.BlockSpec((1, D_MODEL), lambda i, j: (0, 0))
    return _pc(
        body, name=name, grid=(s // tm, nk),
        in_specs=[pl.BlockSpec((tm, tk), lambda i, j: (i, j)), pl.BlockSpec((D_MODEL, tk), lambda i, j: (0, j)),
                  row, vec, row],
        out_specs=[row, row, vec],
        out_shape=[_sds((s, D_MODEL), F32), _sds((s, D_MODEL), BF16), _sds((1, D_MODEL), F32)],
        scratch_shapes=[pltpu.VMEM((tm, D_MODEL), F32)],
    )(a, w, h, gain, dho)


def _matmul_nt(a, w, name):
    s, kd = a.shape
    n = w.shape[0]
    tm = _row_block(s)

    def body(a_ref, w_ref, o_ref):
        o_ref[...] = _dot_nt(a_ref[...], w_ref[...]).astype(BF16)

    return _pc(
        body, name=name, grid=(s // tm,),
        in_specs=[pl.BlockSpec((tm, kd), lambda i: (i, 0)), pl.BlockSpec((n, kd), lambda i: (0, 0))],
        out_specs=pl.BlockSpec((tm, n), lambda i: (i, 0)),
        out_shape=_sds((s, n), BF16),
    )(a, w)


def _matmul_tn(a, b, name):
    s, m = a.shape
    n = b.shape[1]
    tmm = min(m, 1024)
    tn = n if n <= 1024 else 1024
    tk = _row_block(s)
    nk = s // tk

    def body(a_ref, b_ref, o_ref, acc_ref):
        k = pl.program_id(2)

        @pl.when(k == 0)
        def _():
            acc_ref[...] = jnp.zeros_like(acc_ref)

        acc_ref[...] += _dot_tn(a_ref[...], b_ref[...])

        @pl.when(k == nk - 1)
        def _():
            o_ref[...] = acc_ref[...].astype(BF16)

    return _pc(
        body, name=name, grid=(m // tmm, n // tn, nk),
        in_specs=[pl.BlockSpec((tk, tmm), lambda i, j, k: (k, i)), pl.BlockSpec((tk, tn), lambda i, j, k: (k, j))],
        out_specs=pl.BlockSpec((tmm, tn), lambda i, j, k: (i, j)),
        out_shape=_sds((m, n), BF16),
        scratch_shapes=[pltpu.VMEM((tmm, tn), F32)],
    )(a, b)


def _matmul_tn_heads(a, b, name):
    s, m = a.shape
    nh, _, n = b.shape
    tk = _row_block(s)
    nk = s // tk

    def body(a_ref, b_ref, o_ref, acc_ref):
        k = pl.program_id(1)

        @pl.when(k == 0)
        def _():
            acc_ref[...] = jnp.zeros_like(acc_ref)

        acc_ref[...] += _dot_tn(a_ref[...], b_ref[0])

        @pl.when(k == nk - 1)
        def _():
            o_ref[0] = acc_ref[...].astype(BF16)

    return _pc(
        body, name=name, grid=(nh, nk),
        in_specs=[pl.BlockSpec((tk, m), lambda h, k: (k, 0)), pl.BlockSpec((1, tk, n), lambda h, k: (h, k, 0))],
        out_specs=pl.BlockSpec((1, m, n), lambda h, k: (h, 0, 0)),
        out_shape=_sds((nh, m, n), BF16),
        scratch_shapes=[pltpu.VMEM((m, n), F32)],
    )(a, b)


def _conv_bwd(dbuc, bcx, cw):
    s = bcx.shape[0]
    tc = _CONV_COLS
    nc = D_MODEL // tc

    def body(d_ref, b_ref, c_ref, xp_ref, w_ref, db_ref, dc_ref, dxp_ref, dw_ref):
        rows = lax.broadcasted_iota(jnp.int32, (s, tc), 0)
        c = c_ref[...].astype(F32)
        xp = xp_ref[...].astype(F32)
        u = c * xp
        u1 = _shift_down(u, 1, rows)
        u2 = _shift_down(u, 2, rows)
        w = w_ref[...]
        uc = w[2:3] * u + w[1:2] * u1 + w[0:1] * u2
        d = d_ref[...].astype(F32)
        db_ref[...] = (d * uc).astype(BF16)
        duc = d * b_ref[...].astype(F32)
        du = w[2:3] * duc + w[1:2] * _shift_up(duc, 1, rows, s) + w[0:1] * _shift_up(duc, 2, rows, s)
        dc_ref[...] = (du * xp).astype(BF16)
        dxp_ref[...] = (du * c).astype(BF16)
        dw_ref[0:1, :] = jnp.sum(duc * u2, axis=0, keepdims=True)
        dw_ref[1:2, :] = jnp.sum(duc * u1, axis=0, keepdims=True)
        dw_ref[2:3, :] = jnp.sum(duc * u, axis=0, keepdims=True)

    col = lambda off: pl.BlockSpec((s, tc), lambda j: (0, off + j))
    tap = pl.BlockSpec((3, tc), lambda j: (0, j))
    return _pc(
        body, name="conv_bwd", grid=(nc,),
        in_specs=[col(0), col(0), col(nc), col(2 * nc), tap],
        out_specs=[col(0), col(0), col(0), tap],
        out_shape=[_sds((s, D_MODEL), BF16), _sds((s, D_MODEL), BF16), _sds((s, D_MODEL), BF16),
                   _sds((3, D_MODEL), F32)],
    )(dbuc, bcx, bcx, bcx, cw)


def _attn_bwd_dq(q, k, v, do, o, lse, cc, sn, sp, carry=None):
    s = q.shape[1]
    bq = _row_block(s)

    def body(q_ref, k_ref, v_ref, do_ref, o_ref, lse_ref, cc_ref, sn_ref, sp_ref, dq_ref):
        i = pl.program_id(1)
        qb = q_ref[0]
        dob = do_ref[...]
        delta = jnp.sum(dob.astype(F32) * o_ref[...].astype(F32), axis=-1, keepdims=True)
        lse_col = lse_ref[0][:, :1]

        def step(j, dq, masked):
            start = pl.multiple_of(j * bq, bq)
            kb = k_ref[0, pl.ds(start, bq), :]
            vb = v_ref[0, pl.ds(start, bq), :]
            sc = _dot_nt(qb, kb) * ATT_SCALE
            if masked:
                sc = jnp.where(_chunk_mask(bq, bq), sc, NEG)
            p = jnp.exp(sc - lse_col)
            dp = _dot_nt(dob, vb)
            ds = (p * (dp - delta) * ATT_SCALE).astype(BF16)
            return dq + _dot(ds, kb)

        dq = lax.fori_loop(0, i, lambda j, c: step(j, c, False), jnp.zeros((bq, HEAD_PAD), F32))
        dq = step(i, dq, True)
        dq_ref[0, :, :NOPE] = dq[:, :NOPE].astype(BF16)
        dq_ref[0, :, NOPE:] = _rope_bwd(dq[:, NOPE:], cc_ref[...], sn_ref[...], sp_ref[...]).astype(BF16)

    blk = lambda n: pl.BlockSpec((1, bq, n), lambda h, i: (h, i, 0))
    whole = lambda n: pl.BlockSpec((1, s, n), lambda h, i: (h, 0, 0))
    cols = pl.BlockSpec((bq, V_DIM), lambda h, i: (i, h))
    tab = pl.BlockSpec((bq, 128), lambda h, i: (i, 0))
    return _pc_carrying(
        body, carry, (q, k, v, do, o, lse, cc, sn, sp), name="attn_bwd_dq", grid=(N_HEADS, s // bq),
        in_specs=[blk(HEAD_PAD), whole(HEAD_PAD), whole(V_DIM), cols, cols, blk(128), tab, tab, tab],
        out_specs=[blk(HEAD_PAD)],
        out_shape=[_sds((N_HEADS, s, HEAD_PAD), BF16)],
    )


def _attn_bwd_dkv(q, k, v, do, o, lse, carry=None):
    s = q.shape[1]
    bk = _row_block(s)
    nb = s // bk

    def body(q_ref, k_ref, v_ref, do_ref, o_ref, lse_ref, dk_ref, dv_ref):
        j = pl.program_id(1)
        kb = k_ref[0]
        vb = v_ref[0]

        def step(i, carry, masked):
            dk, dv = carry
            start = pl.multiple_of(i * bk, bk)
            qb = q_ref[0, pl.ds(start, bk), :]
            dob = do_ref[pl.ds(start, bk), :]
            ob = o_ref[pl.ds(start, bk), :]
            lse_col = lse_ref[0, pl.ds(start, bk), :][:, :1]
            delta = jnp.sum(dob.astype(F32) * ob.astype(F32), axis=-1, keepdims=True)
            sc = _dot_nt(qb, kb) * ATT_SCALE
            if masked:
                sc = jnp.where(_chunk_mask(bk, bk), sc, NEG)
            p = jnp.exp(sc - lse_col)
            dv = dv + _dot_tn(p.astype(BF16), dob)
            dp = _dot_nt(dob, vb)
            ds = (p * (dp - delta) * ATT_SCALE).astype(BF16)
            dk = dk + _dot_tn(ds, qb)
            return dk, dv

        carry = step(j, (jnp.zeros((bk, HEAD_PAD), F32), jnp.zeros((bk, V_DIM), F32)), True)
        dk, dv = lax.fori_loop(j + 1, nb, lambda i, c: step(i, c, False), carry)
        dk_ref[0] = dk.astype(BF16)
        dv_ref[0] = dv.astype(BF16)

    blk = lambda n: pl.BlockSpec((1, bk, n), lambda h, j: (h, j, 0))
    whole = lambda n: pl.BlockSpec((1, s, n), lambda h, j: (h, 0, 0))
    cols = pl.BlockSpec((s, V_DIM), lambda h, j: (0, h))
    return _pc_carrying(
        body, carry, (q, k, v, do, o, lse), name="attn_bwd_dkv", grid=(N_HEADS, nb),
        in_specs=[whole(HEAD_PAD), blk(HEAD_PAD), blk(V_DIM), cols, cols, whole(128)],
        out_specs=[blk(HEAD_PAD), blk(V_DIM)],
        out_shape=[_sds((N_HEADS, s, HEAD_PAD), BF16), _sds((N_HEADS, s, V_DIM), BF16)],
    )


def _mla_mid_bwd(dq, dk, dv, proj, w_uq, w_ukv, g_cq, g_ckv, cc, sn, sp):
    s = dq.shape[1]
    tm = _row_block(s)

    def body(dq_ref, dk_ref, dv_ref, proj_ref, wuq_ref, wukv_ref, gcq_ref, gckv_ref, cc_ref, sn_ref, sp_ref,
             dproj_ref, dkv_ref, dgcq_ref, dgckv_ref, acq_ref, ackv_ref, akr_ref):
        i = pl.program_id(0)
        h = pl.program_id(1)

        @pl.when(h == 0)
        def _():
            acq_ref[...] = jnp.zeros_like(acq_ref)
            ackv_ref[...] = jnp.zeros_like(ackv_ref)
            akr_ref[...] = jnp.zeros_like(akr_ref)

        @pl.when((i == 0) & (h == 0))
        def _():
            dgcq_ref[...] = jnp.zeros_like(dgcq_ref)
            dgckv_ref[...] = jnp.zeros_like(dgckv_ref)

        dkb = dk_ref[0]
        dkv_ref[0, :, :NOPE] = dkb[:, :NOPE]
        dkv_ref[0, :, NOPE:] = dv_ref[0]
        acq_ref[...] += _dot_nt(dq_ref[0], wuq_ref[0])
        ackv_ref[...] += _dot_nt(dkv_ref[0], wukv_ref[0])
        akr_ref[...] += dkb[:, NOPE:].astype(F32)

        @pl.when(h == N_HEADS - 1)
        def _():
            proj = proj_ref[...]
            dcq, dgcq = _rms_bwd(acq_ref[...], proj[:, :CQ], gcq_ref[...])
            dckv, dgckv = _rms_bwd(ackv_ref[...], proj[:, CQ:CQ + CKV], gckv_ref[...])
            dproj_ref[:, :CQ] = dcq.astype(BF16)
            dproj_ref[:, CQ:CQ + CKV] = dckv.astype(BF16)
            dproj_ref[:, CQ + CKV:] = _rope_bwd(akr_ref[...], cc_ref[...], sn_ref[...], sp_ref[...]).astype(BF16)
            dgcq_ref[...] += dgcq
            dgckv_ref[...] += dgckv

    head_blk = lambda n: pl.BlockSpec((1, tm, n), lambda i, h: (h, i, 0))
    head_w = lambda a: pl.BlockSpec((1,) + a.shape[1:], lambda i, h: (h, 0, 0))
    row = lambda n: pl.BlockSpec((tm, n), lambda i, h: (i, 0))
    vec = lambda n: pl.BlockSpec((1, n), lambda i, h: (0, 0))
    return _pc(
        body, name="mla_mid_bwd", grid=(s // tm, N_HEADS),
        in_specs=[head_blk(HEAD_PAD), head_blk(HEAD_PAD), head_blk(V_DIM), row(PROJ_PAD), head_w(w_uq), head_w(w_ukv),
                  vec(CQ), vec(CKV), row(128), row(128), row(128)],
        out_specs=[row(PROJ_PAD), head_blk(HEAD_PAD), vec(CQ), vec(CKV)],
        out_shape=[_sds((s, PROJ_PAD), BF16), _sds((N_HEADS, s, HEAD_PAD), BF16), _sds((1, CQ), F32),
                   _sds((1, CKV), F32)],
        scratch_shapes=[pltpu.VMEM((tm, CQ), F32), pltpu.VMEM((tm, CKV), F32), pltpu.VMEM((tm, 128), F32)],
    )(dq, dk, dv, proj, w_uq, w_ukv, g_cq, g_ckv, cc, sn, sp)


def _peer(k):
    x, y, c = lax.axis_index("x"), lax.axis_index("y"), lax.axis_index("c")
    px = 1 - x if k & 4 else x
    py = 1 - y if k & 2 else y
    pc = 1 - c if k & 1 else c
    return (px, py, pc), 4 * px + 2 * py + pc


def _exchange_ops(arrays, scatter):
    nw = len(arrays)

    def descriptors(ins, outs, send_sems, recv_sems, local_sems, with_arrivals):
        _, me = _peer(0)

        def src(w, idx):
            return ins[w].at[idx] if scatter else ins[w]

        own = [pltpu.make_async_copy(src(w, me), outs[w].at[me], local_sems.at[w]) for w in range(nw)]
        sends, arrivals = [], []
        for k in range(1, N_DEV):
            dev, idx = _peer(k)
            for w in range(nw):
                sems = dict(send_sem=send_sems.at[w * N_DEV + k], recv_sem=recv_sems.at[w * N_DEV + k],
                            device_id=dev, device_id_type=pl.DeviceIdType.MESH)
                sends.append(pltpu.make_async_remote_copy(src_ref=src(w, idx), dst_ref=outs[w].at[me], **sems))
                if with_arrivals:
                    arrivals.append(
                        pltpu.make_async_remote_copy(src_ref=src(w, idx), dst_ref=outs[w].at[idx], **sems))
        return own, sends, arrivals

    def start(ins, outs, send_sems, recv_sems, local_sems):
        own, sends, _ = descriptors(ins, outs, send_sems, recv_sems, local_sems, False)
        for cp in own + sends:
            cp.start()

    def wait(ins, outs, send_sems, recv_sems, local_sems):
        own, sends, arrivals = descriptors(ins, outs, send_sems, recv_sems, local_sems, True)
        for cp in own:
            cp.wait()
        for cp in sends:
            cp.wait_send()
        for cp in arrivals:
            cp.wait_recv()

    landed = [_sds(a.shape if scatter else (N_DEV,) + a.shape, a.dtype) for a in arrays]
    sems = [pltpu.SemaphoreType.DMA((nw * N_DEV,)), pltpu.SemaphoreType.DMA((nw * N_DEV,)),
            pltpu.SemaphoreType.DMA((nw,))]
    return start, wait, landed, sems


def _exchange(arrays, scatter, name):
    nw = len(arrays)
    start, wait, landed, sems = _exchange_ops(arrays, scatter)

    def body(*refs):
        ins, outs, csem = refs[:nw], refs[nw:2 * nw], refs[2 * nw:]
        start(ins, outs, *csem)
        wait(ins, outs, *csem)

    any_spec = pl.BlockSpec(memory_space=pl.ANY)
    return _pc(body, name=name, in_specs=[any_spec] * nw, out_specs=[any_spec] * nw, out_shape=landed,
               scratch_shapes=sems)(*arrays)


def _adam_math(g, w, m, v):
    m = ADAM_B1 * m + (1.0 - ADAM_B1) * g
    v = ADAM_B2 * v + (1.0 - ADAM_B2) * jnp.square(g)
    m_hat = m / (1.0 - ADAM_B1 ** ADAM_STEP)
    v_hat = v / (1.0 - ADAM_B2 ** ADAM_STEP)
    delta = -ADAM_LR * (m_hat / (jnp.sqrt(v_hat) + ADAM_EPS) + ADAM_WD * w)
    return delta, m, v


def _adam_rows(r):
    for t in (256, 128, 64, 32, 16):
        if r % t == 0:
            return t
    return r


def _adamw(parts, w, m, v, name):
    r, c = w.shape
    tr = _adam_rows(r)

    def body(p_ref, w_ref, m_ref, v_ref, g_ref, d_ref, mo_ref, vo_ref):
        g = p_ref[0].astype(F32)
        for src in range(1, N_DEV):
            g = g + p_ref[src].astype(F32)
        delta, m2, v2 = _adam_math(g, w_ref[...], m_ref[...], v_ref[...])
        g_ref[...] = g
        d_ref[...] = delta
        mo_ref[...] = m2
        vo_ref[...] = v2

    blk = pl.BlockSpec((tr, c), lambda i: (i, 0))
    return _pc(
        body, name=name, grid=(r // tr,),
        in_specs=[pl.BlockSpec((N_DEV, tr, c), lambda i: (0, i, 0)), blk, blk, blk],
        out_specs=[blk, blk, blk, blk],
        out_shape=[_sds((r, c), F32)] * 4,
    )(parts, w, m, v)


def _small_allreduce_adamw(gpack, wpack, mpack, vpack):
    shape = gpack.shape

    def body(g_ref, w_ref, m_ref, v_ref, go_ref, d_ref, mo_ref, vo_ref, gath_ref, send_sems, recv_sems):
        _, me = _peer(0)
        gath_ref[me] = g_ref[...]
        copies = []
        for k in range(1, N_DEV):
            dev, idx = _peer(k)
            copies.append(pltpu.make_async_remote_copy(
                src_ref=g_ref, dst_ref=gath_ref.at[me], send_sem=send_sems.at[k], recv_sem=recv_sems.at[k],
                device_id=dev, device_id_type=pl.DeviceIdType.MESH))
        for cp in copies:
            cp.start()
        for cp in copies:
            cp.wait_send()
        for k in range(1, N_DEV):
            dev, idx = _peer(k)
            pltpu.make_async_remote_copy(
                src_ref=g_ref, dst_ref=gath_ref.at[idx], send_sem=send_sems.at[k], recv_sem=recv_sems.at[k],
                device_id=dev, device_id_type=pl.DeviceIdType.MESH).wait_recv()
        g = gath_ref[0]
        for src in range(1, N_DEV):
            g = g + gath_ref[src]
        delta, m2, v2 = _adam_math(g, w_ref[...], m_ref[...], v_ref[...])
        go_ref[...] = g
        d_ref[...] = delta
        mo_ref[...] = m2
        vo_ref[...] = v2

    vm = pl.BlockSpec(memory_space=pltpu.VMEM)
    return _pc(
        body, name="small_allreduce_adamw",
        in_specs=[vm] * 4, out_specs=[vm] * 4, out_shape=[_sds(shape, F32)] * 4,
        scratch_shapes=[pltpu.VMEM((N_DEV,) + shape, F32), pltpu.SemaphoreType.DMA((N_DEV,)),
                        pltpu.SemaphoreType.DMA((N_DEV,))],
    )(gpack, wpack, mpack, vpack)


def _cols_from_shards(g, pad_to=None):
    if pad_to is not None and pad_to != g.shape[-1]:
        g = jnp.pad(g, ((0, 0), (0, 0), (0, pad_to - g.shape[-1])))
    return jnp.transpose(g, (1, 0, 2)).reshape(g.shape[1], N_DEV * g.shape[2])


def _shards_from_cols(a, n):
    k = a.shape[0]
    a = a.reshape(k, N_DEV, a.shape[1] // N_DEV)[:, :, :n]
    return jnp.transpose(a, (1, 0, 2))


def _rows_from_shards(g, pad_to=None):
    if pad_to is not None and pad_to != g.shape[1]:
        g = jnp.pad(g, ((0, 0), (0, pad_to - g.shape[1]), (0, 0)))
    return g.reshape(N_DEV * g.shape[1], g.shape[2])


def _shards_from_rows(a, r):
    n = a.shape[1]
    return a.reshape(N_DEV, a.shape[0] // N_DEV, n)[:, :r, :]


def _rope_tables(pos):
    inv_freq = 1.0 / (ROPE_THETA ** (jnp.arange(0, ROPE, 2, dtype=F32) / ROPE))
    ang = pos.astype(F32)[:, None] * inv_freq
    cos, sin = jnp.cos(ang), jnp.sin(ang)
    z32, z64, z96 = (jnp.zeros((pos.shape[0], n), F32) for n in (32, 64, 96))
    return (jnp.concatenate([cos, cos, z64], axis=1), jnp.concatenate([-sin, z96], axis=1),
            jnp.concatenate([z32, sin, z64], axis=1))


def _pad_row(vec):
    vec = vec.reshape(1, -1)
    return jnp.pad(vec, ((0, 0), (0, D_MODEL - vec.shape[1])))


def _forward_backward(x, target, tables, gains, shards):
    cc, sn, sp = tables
    ffn_g = gains["ffn_norm"]

    def gather(names):
        return [shards[n] for n in names], False

    names = ["w_in", "w_uq", "w_ukv", "w_o", "conv_norm", "conv_w"]
    got = dict(zip(names, _exchange(gather(names)[0], False, "gather_mla")))
    w_in = jnp.pad(_rows_from_shards(got["w_in"]), ((0, 0), (0, PROJ_PAD - PROJ)))
    w_uq = jnp.pad(got["w_uq"], ((0, 0), (0, 0), (0, HEAD_PAD - NOPE - ROPE)))
    w_ukv = got["w_ukv"]
    w_o = _rows_from_shards(got["w_o"])
    conv_norm = got["conv_norm"].reshape(1, D_MODEL)
    conv_w = jnp.transpose(got["conv_w"], (1, 0, 2)).reshape(3, D_MODEL)

    n0, proj, cqn, ckvn, krr = _mla_in_fwd(x, gains["mla_norm"], w_in, gains["g_cq"], gains["g_ckv"], cc, sn, sp)
    q, k, v = _qkv_proj(cqn, ckvn, krr, w_uq, w_ukv, cc, sn, sp)
    names = ["wg0", "wu0", "wd0", "c_in", "c_out"]
    (o, lse), got = _attn_fwd(q, k, v, carry=gather(names))
    got = dict(zip(names, got))
    wg0, wu0 = (_cols_from_shards(got[n], FF_SHARD_PAD) for n in ("wg0", "wu0"))
    wd0 = _rows_from_shards(got["wd0"], FF_SHARD_PAD)
    c_in = _cols_from_shards(got["c_in"])
    c_out = _rows_from_shards(got["c_out"])
    h1 = _matmul_res(o, w_o, x, "mla_out_fwd")
    names = ["wg1", "wu1", "wd1"]
    (h2, n1, gate0, up0), got = _ffn_fwd(h1, ffn_g[0:1], wg0, wu0, wd0, "ffn0_fwd", carry=gather(names))
    got = dict(zip(names, got))
    wg1, wu1 = (_cols_from_shards(got[n], FF_SHARD_PAD) for n in ("wg1", "wu1"))
    wd1 = _rows_from_shards(got["wd1"], FF_SHARD_PAD)
    n2, bcx = _rms_matmul(h2, conv_norm, c_in, "conv_in_fwd")
    bu = _conv_fwd(bcx, conv_w)
    h3 = _matmul_res(bu, c_out, h2, "conv_out_fwd")
    (h4, n3, gate1, up1), _ = _ffn_fwd(h3, ffn_g[1:2], wg1, wu1, wd1, "ffn1_fwd")
    dh4, d_final, loss = _final_loss(h4, gains["final_norm"], target)

    small = {"final_norm": d_final}
    dh3, dh3_b, dh4_b, dgate, dup, act, d_ffn1 = _ffn_bwd_x(dh4, h3, ffn_g[1:2], gate1, up1, wg1, wu1, wd1, "ffn1_bwd")
    dwg1 = _matmul_tn(n3, dgate, "ffn1_dwg")
    dwu1 = _matmul_tn(n3, dup, "ffn1_dwu")
    dwd1 = _matmul_tn(act, dh4_b, "ffn1_dwd")
    dbuc = _matmul_nt(dh3_b, c_out, "conv_out_bwd")
    d_c_out = _matmul_tn(bu, dh3_b, "conv_dwout")
    db, dc, dxp, small["conv_w"] = _conv_bwd(dbuc, bcx, conv_w)
    dbcx = jnp.concatenate([db, dc, dxp], axis=1)
    d_c_in = _matmul_tn(n2, dbcx, "conv_dwin")
    dh2, dh2_b, small["conv_norm"] = _nt_rmsbwd(dbcx, c_in, h2, conv_norm, dh3, "conv_in_bwd")
    dh1, dh1_b, dh2_b2, dgate, dup, act, d_ffn0 = _ffn_bwd_x(dh2, h1, ffn_g[0:1], gate0, up0, wg0, wu0, wd0, "ffn0_bwd")
    dwg0 = _matmul_tn(n1, dgate, "ffn0_dwg")
    dwu0 = _matmul_tn(n1, dup, "ffn0_dwu")
    dwd0 = _matmul_tn(act, dh2_b2, "ffn0_dwd")
    small["ffn_norm"] = jnp.concatenate([d_ffn0, d_ffn1], axis=0)
    do = _matmul_nt(dh1_b, w_o, "mla_out_bwd")
    d_w_o = _matmul_tn(o, dh1_b, "mla_dwo")
    send = dict(
        wg=jnp.stack([_shards_from_cols(g, FF_SHARD) for g in (dwg0, dwg1)], axis=1),
        wu=jnp.stack([_shards_from_cols(g, FF_SHARD) for g in (dwu0, dwu1)], axis=1),
        wd=jnp.stack([_shards_from_rows(g, FF_SHARD) for g in (dwd0, dwd1)], axis=1),
        c_in=_shards_from_cols(d_c_in, 3 * D_MODEL // N_DEV),
        c_out=_shards_from_rows(d_c_out, D_MODEL // N_DEV),
        w_o=_shards_from_rows(d_w_o, D_MODEL // N_DEV),
    )
    parts = {}
    names = ["wg", "wu"]
    (dq,), got = _attn_bwd_dq(q, k, v, do, o, lse, cc, sn, sp, carry=([send[n] for n in names], True))
    parts.update(zip(names, got))
    names = ["wd", "c_in", "c_out", "w_o"]
    (dk, dv), got = _attn_bwd_dkv(q, k, v, do, o, lse, carry=([send[n] for n in names], True))
    parts.update(zip(names, got))
    dproj, dkv, small["g_cq"], small["g_ckv"] = _mla_mid_bwd(
        dq, dk, dv, proj, w_uq, w_ukv, gains["g_cq"], gains["g_ckv"], cc, sn, sp)
    d_w_uq = _matmul_tn_heads(cqn, dq, "mla_dwuq")
    d_w_ukv = _matmul_tn_heads(ckvn, dkv, "mla_dwukv")
    d_w_in = _matmul_tn(n0, dproj, "mla_dwin")
    dx, _, small["mla_norm"] = _nt_rmsbwd(dproj, w_in, x, gains["mla_norm"], dh1, "mla_in_bwd")
    names = ["w_in", "w_uq", "w_ukv"]
    last = [_shards_from_rows(d_w_in[:, :PROJ], D_MODEL // N_DEV), d_w_uq[:, :, :NOPE + ROPE], d_w_ukv]
    parts.update(zip(names, _exchange(last, True, "scatter_mla")))
    return loss[0, 0], dx, parts, small


def kernel(x, positions, mla_norm, mla_w_in, mla_g_cq, mla_g_ckv, mla_w_uq, mla_w_ukv, mla_w_o, conv_norm, conv_w_in, conv_w, conv_w_out, ffn_norm, ffn_w_gate, ffn_w_up, ffn_w_down, final_norm, loss_target, m_mla_norm, m_mla_w_in, m_mla_g_cq, m_mla_g_ckv, m_mla_w_uq, m_mla_w_ukv, m_mla_w_o, m_conv_norm, m_conv_w_in, m_conv_w, m_conv_w_out, m_ffn_norm, m_ffn_w_gate, m_ffn_w_up, m_ffn_w_down, m_final_norm, v_mla_norm, v_mla_w_in, v_mla_g_cq, v_mla_g_ckv, v_mla_w_uq, v_mla_w_ukv, v_mla_w_o, v_conv_norm, v_conv_w_in, v_conv_w, v_conv_w_out, v_ffn_norm, v_ffn_w_gate, v_ffn_w_up, v_ffn_w_down, v_final_norm):
    me = 4 * lax.axis_index("x") + 2 * lax.axis_index("y") + lax.axis_index("c")

    shards = dict(w_in=mla_w_in[0], w_uq=mla_w_uq[0], w_ukv=mla_w_ukv[0], w_o=mla_w_o[0], c_in=conv_w_in[0],
                  c_out=conv_w_out[0], wg0=ffn_w_gate[0], wg1=ffn_w_gate[1], wu0=ffn_w_up[0], wu1=ffn_w_up[1],
                  wd0=ffn_w_down[0], wd1=ffn_w_down[1])
    shards = {n: a.astype(BF16) for n, a in shards.items()}
    shards.update(conv_norm=conv_norm, conv_w=conv_w[0])
    gains = dict(mla_norm=mla_norm, g_cq=mla_g_cq, g_ckv=mla_g_ckv, ffn_norm=ffn_norm,
                 final_norm=final_norm.reshape(1, -1))
    loss_local, dx, parts, grads = _forward_backward(x[0], loss_target[0], _rope_tables(positions[0]), gains, shards)
    loss = lax.psum(loss_local, ("x", "y", "c"))
    names = ["w_in", "w_uq", "w_ukv", "w_o", "c_in", "c_out", "wg", "wu", "wd"]

    moments = dict(w_in=(m_mla_w_in, v_mla_w_in), w_uq=(m_mla_w_uq, v_mla_w_uq), w_ukv=(m_mla_w_ukv, v_mla_w_ukv),
                   w_o=(m_mla_w_o, v_mla_w_o), c_in=(m_conv_w_in, v_conv_w_in), c_out=(m_conv_w_out, v_conv_w_out),
                   wg=(m_ffn_w_gate, v_ffn_w_gate), wu=(m_ffn_w_up, v_ffn_w_up), wd=(m_ffn_w_down, v_ffn_w_down))
    full = dict(w_in=mla_w_in, w_uq=mla_w_uq, w_ukv=mla_w_ukv, w_o=mla_w_o, c_in=conv_w_in, c_out=conv_w_out,
                wg=ffn_w_gate, wu=ffn_w_up, wd=ffn_w_down)
    res = {}
    for n in names:
        w = full[n]
        shape = w.shape
        c = shape[-1]
        flat = lambda a: a.reshape(-1, c)
        outs = _adamw(parts[n].reshape(N_DEV, -1, c), flat(w), flat(moments[n][0]), flat(moments[n][1]), "adamw_" + n)
        res[n] = [o.reshape(shape) for o in outs]

    col0 = me * (D_MODEL // N_DEV)
    zeros_row = jnp.zeros((1, D_MODEL), F32)

    def place(shard):
        return lax.dynamic_update_slice(jnp.zeros((shard.shape[0], D_MODEL), F32), shard, (0, col0))

    def pack(mla_n, g_cq, g_ckv, ffn_n, fin_n, conv_n, conv_taps):
        rows = [mla_n, _pad_row(g_cq), _pad_row(g_ckv), ffn_n, fin_n.reshape(1, -1), conv_n, conv_taps]
        rows.append(jnp.zeros((SMALL_ROWS - sum(r.shape[0] for r in rows), D_MODEL), F32))
        return jnp.concatenate(rows, axis=0)

    gpack = pack(grads["mla_norm"], grads["g_cq"], grads["g_ckv"], grads["ffn_norm"], grads["final_norm"],
                 grads["conv_norm"], grads["conv_w"])
    wpack = pack(mla_norm, mla_g_cq, mla_g_ckv, ffn_norm, final_norm, place(conv_norm), place(conv_w[0]))
    mpack = pack(m_mla_norm, m_mla_g_cq, m_mla_g_ckv, m_ffn_norm, m_final_norm, place(m_conv_norm), place(m_conv_w[0]))
    vpack = pack(v_mla_norm, v_mla_g_cq, v_mla_g_ckv, v_ffn_norm, v_final_norm, place(v_conv_norm), place(v_conv_w[0]))
    small = _small_allreduce_adamw(gpack, wpack, mpack, vpack)

    def unpack(p):
        own = lambda rows: lax.dynamic_slice(rows, (0, col0), (rows.shape[0], D_MODEL // N_DEV))
        return dict(mla_norm=p[0:1], g_cq=p[1:2, :CQ], g_ckv=p[2:3, :CKV], ffn_norm=p[3:5], final_norm=p[5],
                    conv_norm=own(p[6:7]), conv_w=own(p[7:10])[None])

    small = [unpack(p) for p in small]
    order = ["mla_norm", "w_in", "g_cq", "g_ckv", "w_uq", "w_ukv", "w_o", "conv_norm", "c_in", "conv_w", "c_out",
             "ffn_norm", "wg", "wu", "wd", "final_norm"]
    out = [loss, dx[None]]
    for kind in range(4):
        for n in order:
            out.append(res[n][kind] if n in res else small[kind][n])
    return tuple(out)
```

```python
import math

import jax
import jax.numpy as jnp
from jax import lax
from jax.experimental import pallas as pl
from jax.experimental.pallas import tpu as pltpu

F32 = jnp.float32
BF16 = jnp.bfloat16

N_DEV = 8
D_MODEL = 1024
N_HEADS = 8
NOPE = 128
ROPE = 64
V_DIM = 128
HEAD_PAD = 256
CQ = 512
CKV = 256
PROJ = CQ + CKV + ROPE
PROJ_PAD = CQ + CKV + 128
D_FF = 2816
FF_SHARD = D_FF // N_DEV
FF_SHARD_PAD = 384
FF_PAD = FF_SHARD_PAD * N_DEV
CHUNK_SHIFT = 6
RMS_EPS = 1e-6
ROPE_THETA = 10000.0
ATT_SCALE = 1.0 / math.sqrt(NOPE + ROPE)
LOG2_E = math.log2(math.e)
LN_2 = math.log(2.0)
Q_SCALE = ATT_SCALE * LOG2_E
NEG = -1e30

ADAM_LR = 0.001
ADAM_B1 = 0.9
ADAM_B2 = 0.999
ADAM_EPS = 1e-08
ADAM_WD = 0.01
ADAM_STEP = 10

SMALL_ROWS = 16

_NT = (((1,), (1,)), ((), ()))
_TN = (((0,), (0,)), ((), ()))


def _pc(body, *, name, out_shape, grid=(), in_specs=None, out_specs=None, scratch_shapes=(), vmem_mb=None):
    params = {}
    if vmem_mb is not None:
        params["vmem_limit_bytes"] = vmem_mb << 20
    kwargs = dict(
        name=name, out_shape=out_shape, grid=grid, scratch_shapes=scratch_shapes,
        compiler_params=pltpu.CompilerParams(**params),
    )
    if in_specs is not None:
        kwargs["in_specs"] = in_specs
    if out_specs is not None:
        kwargs["out_specs"] = out_specs
    return pl.pallas_call(body, **kwargs)


def _pc_carrying(body, carry, operands, *, name, out_shape, grid, in_specs, out_specs, scratch_shapes=()):
    if carry is None:
        return _pc(body, name=name, out_shape=out_shape, grid=grid, in_specs=in_specs, out_specs=out_specs,
                   scratch_shapes=scratch_shapes)(*operands), None
    arrays, scatter, relay_at = carry
    nw, n_in, n_out, n_scr = len(arrays), len(in_specs), len(out_shape), len(scratch_shapes)
    start, relay, wait, landed_shapes, sems = _exchange_ops(arrays, scatter)
    n_steps = math.prod(grid)
    relay_step = None if relay is None else max(1, min(int(relay_at * n_steps), n_steps - 2))

    def wrapped(*refs):
        ins, rest = refs[:n_in], refs[n_in:]
        cin, rest = rest[:nw], rest[nw:]
        outs, rest = rest[:n_out], rest[n_out:]
        cout, rest = rest[:nw], rest[nw:]
        scr, csem = rest[:n_scr], rest[n_scr:]
        step = pl.program_id(0)
        for a in range(1, len(grid)):
            step = step * grid[a] + pl.program_id(a)

        @pl.when(step == 0)
        def _():
            start(cin, cout, *csem)

        if relay is not None:
            @pl.when(step == relay_step)
            def _():
                relay(cin, cout, *csem)

        body(*ins, *outs, *scr)

        @pl.when(step == n_steps - 1)
        def _():
            wait(cin, cout, *csem)

    any_spec = pl.BlockSpec(memory_space=pl.ANY)
    res = _pc(
        wrapped, name=name, grid=grid,
        in_specs=list(in_specs) + [any_spec] * nw, out_specs=list(out_specs) + [any_spec] * nw,
        out_shape=list(out_shape) + landed_shapes, scratch_shapes=list(scratch_shapes) + sems,
    )(*operands, *arrays)
    return res[:n_out], res[n_out:]


def _sds(shape, dtype):
    return jax.ShapeDtypeStruct(shape, dtype)


def _dot(a, b):
    return jnp.dot(a, b, preferred_element_type=F32)


def _dot_nt(a, b):
    return lax.dot_general(a, b, _NT, preferred_element_type=F32)


def _dot_tn(a, b):
    return lax.dot_general(a, b, _TN, preferred_element_type=F32)


def _rstd(x):
    return lax.rsqrt(jnp.mean(x * x, axis=-1, keepdims=True) + RMS_EPS)


def _rms(x, g):
    return (x * _rstd(x)) * g


def _rms_bwd(dy, x, g):
    r = _rstd(x)
    xhat = x * r
    dxhat = dy * g
    dx = r * (dxhat - xhat * jnp.mean(dxhat * xhat, axis=-1, keepdims=True))
    return dx, jnp.sum(dy * xhat, axis=0, keepdims=True)


def _rope(t, cc, sn, sp):
    return t * cc + pltpu.roll(t, 96, 1) * sn + pltpu.roll(t, 32, 1) * sp


def _rope_bwd(dt, cc, sn, sp):
    return dt * cc + pltpu.roll(dt * sn, 32, 1) + pltpu.roll(dt * sp, 96, 1)


def _row_block(s):
    return min(512, s)


def _mla_in_fwd(x, g0, w_in, g_cq, g_ckv, cc, sn, sp):
    s = x.shape[0]
    tm = _row_block(s)

    def body(x_ref, g0_ref, w_ref, gcq_ref, gckv_ref, cc_ref, sn_ref, sp_ref,
             n_ref, proj_ref, cqn_ref, ckvn_ref, krr_ref):
        nb = _rms(x_ref[...], g0_ref[...]).astype(BF16)
        n_ref[...] = nb
        proj = _dot(nb, w_ref[...])
        proj_ref[...] = proj
        cqn_ref[...] = _rms(proj[:, :CQ], gcq_ref[...]).astype(BF16)
        ckvn_ref[...] = _rms(proj[:, CQ:CQ + CKV], gckv_ref[...]).astype(BF16)
        krr_ref[...] = _rope(proj[:, CQ + CKV:], cc_ref[...], sn_ref[...], sp_ref[...]).astype(BF16)

    row = lambda n: pl.BlockSpec((tm, n), lambda i: (i, 0))
    full = lambda a: pl.BlockSpec(a.shape, lambda i: (0, 0))
    return _pc(
        body, name="mla_in_fwd", grid=(s // tm,),
        in_specs=[row(D_MODEL), full(g0), full(w_in), full(g_cq), full(g_ckv), row(128), row(128), row(128)],
        out_specs=[row(D_MODEL), row(PROJ_PAD), row(CQ), row(CKV), row(128)],
        out_shape=[_sds((s, D_MODEL), BF16), _sds((s, PROJ_PAD), F32), _sds((s, CQ), BF16),
                   _sds((s, CKV), BF16), _sds((s, 128), BF16)],
    )(x, g0, w_in, g_cq, g_ckv, cc, sn, sp)


def _qkv_proj(cqn, ckvn, krr, w_uq, w_ukv, cc, sn, sp):
    s = cqn.shape[0]
    tm = _row_block(s)

    def body(cqn_ref, ckvn_ref, krr_ref, wuq_ref, wukv_ref, cc_ref, sn_ref, sp_ref, q_ref, k_ref, v_ref):
        q = _dot(cqn_ref[...], wuq_ref[0]) * Q_SCALE
        q_ref[0, :, :NOPE] = q[:, :NOPE].astype(BF16)
        q_ref[0, :, NOPE:] = _rope(q[:, NOPE:], cc_ref[...], sn_ref[...], sp_ref[...]).astype(BF16)
        kv = _dot(ckvn_ref[...], wukv_ref[0])
        k_ref[0, :, :NOPE] = kv[:, :NOPE].astype(BF16)
        k_ref[0, :, NOPE:] = krr_ref[...]
        v_ref[0] = kv[:, NOPE:].astype(BF16)

    row = lambda n: pl.BlockSpec((tm, n), lambda i, h: (i, 0))
    head_w = lambda a: pl.BlockSpec((1,) + a.shape[1:], lambda i, h: (h, 0, 0))
    head_o = lambda n: pl.BlockSpec((1, tm, n), lambda i, h: (h, i, 0))
    return _pc(
        body, name="qkv_proj", grid=(s // tm, N_HEADS),
        in_specs=[row(CQ), row(CKV), row(128), head_w(w_uq), head_w(w_ukv), row(128), row(128), row(128)],
        out_specs=[head_o(HEAD_PAD), head_o(HEAD_PAD), head_o(V_DIM)],
        out_shape=[_sds((N_HEADS, s, HEAD_PAD), BF16), _sds((N_HEADS, s, HEAD_PAD), BF16),
                   _sds((N_HEADS, s, V_DIM), BF16)],
    )(cqn, ckvn, krr, w_uq, w_ukv, cc, sn, sp)


def _chunk_mask(bq, bk):
    rows = lax.broadcasted_iota(jnp.int32, (bq, bk), 0)
    cols = lax.broadcasted_iota(jnp.int32, (bq, bk), 1)
    return jnp.right_shift(cols, CHUNK_SHIFT) <= jnp.right_shift(rows, CHUNK_SHIFT)


def _attn_fwd(q, k, v, carry=None):
    s = q.shape[1]
    bq = _row_block(s)

    def body(q_ref, k_ref, v_ref, o_ref, lse_ref):
        i = pl.program_id(1)
        qb = q_ref[0]

        def step(j, carry, masked):
            m, l, acc = carry
            start = pl.multiple_of(j * bq, bq)
            kb = k_ref[0, pl.ds(start, bq), :]
            vb = v_ref[0, pl.ds(start, bq), :]
            sc = _dot_nt(qb, kb)
            if masked:
                sc = jnp.where(_chunk_mask(bq, bq), sc, NEG)
            m_new = jnp.maximum(m, jnp.max(sc, axis=-1, keepdims=True))
            p = jnp.exp2(sc - m_new)
            alpha = jnp.exp2(m - m_new)
            l = alpha * l + jnp.sum(p, axis=-1, keepdims=True)
            acc = alpha * acc + _dot(p.astype(BF16), vb)
            return m_new, l, acc

        init = (jnp.full((bq, 1), NEG, F32), jnp.zeros((bq, 1), F32), jnp.zeros((bq, V_DIM), F32))
        carry = lax.fori_loop(0, i, lambda j, c: step(j, c, False), init)
        m, l, acc = step(i, carry, True)
        o_ref[...] = (acc / l).astype(BF16)
        lse_ref[0] = jnp.broadcast_to(m + jnp.log(l) * LOG2_E, (bq, 128))

    return _pc_carrying(
        body, carry, (q, k, v), name="attn_fwd", grid=(N_HEADS, s // bq),
        in_specs=[pl.BlockSpec((1, bq, HEAD_PAD), lambda h, i: (h, i, 0)),
                  pl.BlockSpec((1, s, HEAD_PAD), lambda h, i: (h, 0, 0)),
                  pl.BlockSpec((1, s, V_DIM), lambda h, i: (h, 0, 0))],
        out_specs=[pl.BlockSpec((bq, V_DIM), lambda h, i: (i, h)),
                   pl.BlockSpec((1, bq, 128), lambda h, i: (h, i, 0))],
        out_shape=[_sds((s, N_HEADS * V_DIM), BF16), _sds((N_HEADS, s, 128), F32)],
    )


def _matmul_res(a, w, res, name):
    s, kd = a.shape
    n = w.shape[1]
    tm = _row_block(s)

    def body(a_ref, w_ref, r_ref, o_ref):
        o_ref[...] = r_ref[...] + _dot(a_ref[...], w_ref[...])

    return _pc(
        body, name=name, grid=(s // tm,),
        in_specs=[pl.BlockSpec((tm, kd), lambda i: (i, 0)), pl.BlockSpec((kd, n), lambda i: (0, 0)),
                  pl.BlockSpec((tm, n), lambda i: (i, 0))],
        out_specs=pl.BlockSpec((tm, n), lambda i: (i, 0)),
        out_shape=_sds((s, n), F32),
    )(a, w, res)


def _ffn_fwd(h, gain, wg, wu, wd, name, carry=None):
    s = h.shape[0]
    tm = _row_block(s)
    tf = 512
    nf = FF_PAD // tf

    def body(h_ref, g_ref, wg_ref, wu_ref, wd_ref, o_ref, n_ref, gate_ref, up_ref, acc_ref):
        j = pl.program_id(1)

        @pl.when(j == 0)
        def _():
            n_ref[...] = _rms(h_ref[...], g_ref[...]).astype(BF16)
            acc_ref[...] = jnp.zeros_like(acc_ref)

        nb = n_ref[...]
        gate = _dot(nb, wg_ref[...])
        up = _dot(nb, wu_ref[...])
        gate_ref[...] = gate.astype(BF16)
        up_ref[...] = up.astype(BF16)
        act = gate * jax.nn.sigmoid(gate) * up
        acc_ref[...] += _dot(act.astype(BF16), wd_ref[...])

        @pl.when(j == nf - 1)
        def _():
            o_ref[...] = h_ref[...] + acc_ref[...]

    return _pc_carrying(
        body, carry, (h, gain, wg, wu, wd), name=name, grid=(s // tm, nf),
        in_specs=[pl.BlockSpec((tm, D_MODEL), lambda i, j: (i, 0)), pl.BlockSpec((1, D_MODEL), lambda i, j: (0, 0)),
                  pl.BlockSpec((D_MODEL, tf), lambda i, j: (0, j)), pl.BlockSpec((D_MODEL, tf), lambda i, j: (0, j)),
                  pl.BlockSpec((tf, D_MODEL), lambda i, j: (j, 0))],
        out_specs=[pl.BlockSpec((tm, D_MODEL), lambda i, j: (i, 0)), pl.BlockSpec((tm, D_MODEL), lambda i, j: (i, 0)),
                   pl.BlockSpec((tm, tf), lambda i, j: (i, j)), pl.BlockSpec((tm, tf), lambda i, j: (i, j))],
        out_shape=[_sds((s, D_MODEL), F32), _sds((s, D_MODEL), BF16), _sds((s, FF_PAD), BF16),
                   _sds((s, FF_PAD), BF16)],
        scratch_shapes=[pltpu.VMEM((tm, D_MODEL), F32)],
    )


def _rms_matmul(h, gain, w, name):
    s = h.shape[0]
    n = w.shape[1]
    tm = _row_block(s)
    tn = 1024
    nn = n // tn

    def body(h_ref, g_ref, w_ref, n_ref, o_ref):
        @pl.when(pl.program_id(1) == 0)
        def _():
            n_ref[...] = _rms(h_ref[...], g_ref[...]).astype(BF16)

        o_ref[...] = _dot(n_ref[...], w_ref[...]).astype(BF16)

    return _pc(
        body, name=name, grid=(s // tm, nn),
        in_specs=[pl.BlockSpec((tm, D_MODEL), lambda i, j: (i, 0)), pl.BlockSpec((1, D_MODEL), lambda i, j: (0, 0)),
                  pl.BlockSpec((D_MODEL, tn), lambda i, j: (0, j))],
        out_specs=[pl.BlockSpec((tm, D_MODEL), lambda i, j: (i, 0)), pl.BlockSpec((tm, tn), lambda i, j: (i, j))],
        out_shape=[_sds((s, D_MODEL), BF16), _sds((s, n), BF16)],
    )(h, gain, w)


def _shift_down(u, k, rows):
    return jnp.where(rows >= k, pltpu.roll(u, k, 0), 0.0)


def _shift_up(u, k, rows, s):
    return jnp.where(rows < s - k, pltpu.roll(u, s - k, 0), 0.0)


_CONV_COLS = 128


def _conv_fwd(bcx, cw):
    s = bcx.shape[0]
    tc = _CONV_COLS
    nc = D_MODEL // tc

    def body(b_ref, c_ref, xp_ref, w_ref, o_ref):
        rows = lax.broadcasted_iota(jnp.int32, (s, tc), 0)
        u = c_ref[...].astype(F32) * xp_ref[...].astype(F32)
        w = w_ref[...]
        uc = w[2:3] * u + w[1:2] * _shift_down(u, 1, rows) + w[0:1] * _shift_down(u, 2, rows)
        o_ref[...] = (b_ref[...].astype(F32) * uc).astype(BF16)

    col = lambda off: pl.BlockSpec((s, tc), lambda j: (0, off + j))
    return _pc(
        body, name="conv_fwd", grid=(nc,),
        in_specs=[col(0), col(nc), col(2 * nc), pl.BlockSpec((3, tc), lambda j: (0, j))],
        out_specs=pl.BlockSpec((s, tc), lambda j: (0, j)),
        out_shape=_sds((s, D_MODEL), BF16),
    )(bcx, bcx, bcx, cw)


def _final_loss(h, gain, target):
    s = h.shape[0]
    tm = _row_block(s)

    def body(h_ref, g_ref, t_ref, dh_ref, dg_ref, loss_ref):
        i = pl.program_id(0)

        @pl.when(i == 0)
        def _():
            dg_ref[...] = jnp.zeros_like(dg_ref)
            loss_ref[...] = jnp.zeros_like(loss_ref)

        hb = h_ref[...]
        e = _rms(hb, g_ref[...]) - t_ref[...]
        loss_ref[...] += 0.5 * jnp.sum(jnp.mean(e * e, axis=-1, keepdims=True))
        dx, dg = _rms_bwd(e * (1.0 / D_MODEL), hb, g_ref[...])
        dh_ref[...] = dx
        dg_ref[...] += dg

    row = pl.BlockSpec((tm, D_MODEL), lambda i: (i, 0))
    vec = pl.BlockSpec((1, D_MODEL), lambda i: (0, 0))
    return _pc(
        body, name="final_loss", grid=(s // tm,),
        in_specs=[row, vec, row],
        out_specs=[row, vec, pl.BlockSpec((8, 128), lambda i: (0, 0))],
        out_shape=[_sds((s, D_MODEL), F32), _sds((1, D_MODEL), F32), _sds((8, 128), F32)],
    )(h, gain, target)


def _ffn_bwd_x(dho, h, gain, gate, up, wg, wu, wd, name):
    s = h.shape[0]
    tm = _row_block(s)
    tf = 512
    nf = FF_PAD // tf

    def body(dho_ref, h_ref, g_ref, gate_ref, up_ref, wg_ref, wu_ref, wd_ref,
             dh_ref, dhb_ref, dhob_ref, dgate_ref, dup_ref, act_ref, dgain_ref, acc_ref):
        i = pl.program_id(0)
        j = pl.program_id(1)

        @pl.when(j == 0)
        def _():
            dhob_ref[...] = dho_ref[...].astype(BF16)
            acc_ref[...] = jnp.zeros_like(acc_ref)

        @pl.when((i == 0) & (j == 0))
        def _():
            dgain_ref[...] = jnp.zeros_like(dgain_ref)

        dact = _dot_nt(dhob_ref[...], wd_ref[...])
        g = gate_ref[...].astype(F32)
        u = up_ref[...].astype(F32)
        sg = jax.nn.sigmoid(g)
        silu = g * sg
        dg = (dact * u * (sg * (1.0 + g * (1.0 - sg)))).astype(BF16)
        du = (dact * silu).astype(BF16)
        dgate_ref[...] = dg
        dup_ref[...] = du
        act_ref[...] = (silu * u).astype(BF16)
        acc_ref[...] += _dot_nt(dg, wg_ref[...]) + _dot_nt(du, wu_ref[...])

        @pl.when(j == nf - 1)
        def _():
            dx, dgain = _rms_bwd(acc_ref[...], h_ref[...], g_ref[...])
            dh = dho_ref[...] + dx
            dh_ref[...] = dh
            dhb_ref[...] = dh.astype(BF16)
            dgain_ref[...] += dgain

    row = pl.BlockSpec((tm, D_MODEL), lambda i, j: (i, 0))
    vec = pl.BlockSpec((1, D_MODEL), lambda i, j: (0, 0))
    hid = pl.BlockSpec((tm, tf), lambda i, j: (i, j))
    wcol = pl.BlockSpec((D_MODEL, tf), lambda i, j: (0, j))
    wrow = pl.BlockSpec((tf, D_MODEL), lambda i, j: (j, 0))
    return _pc(
        body, name=name, grid=(s // tm, nf),
        in_specs=[row, row, vec, hid, hid, wcol, wcol, wrow],
        out_specs=[row, row, row, hid, hid, hid, vec],
        out_shape=[_sds((s, D_MODEL), F32), _sds((s, D_MODEL), BF16), _sds((s, D_MODEL), BF16),
                   _sds((s, FF_PAD), BF16), _sds((s, FF_PAD), BF16), _sds((s, FF_PAD), BF16),
                   _sds((1, D_MODEL), F32)],
        scratch_shapes=[pltpu.VMEM((tm, D_MODEL), F32)],
    )(dho, h, gain, gate, up, wg, wu, wd)


def _nt_rmsbwd(a, w, h, gain, dho, name):
    s, kd = a.shape
    tm = _row_block(s)
    tk = kd if kd <= 1024 else 1024
    nk = kd // tk

    def body(a_ref, w_ref, h_ref, g_ref, dho_ref, dh_ref, dhb_ref, dgain_ref, acc_ref):
        i = pl.program_id(0)
        j = pl.program_id(1)

        @pl.when(j == 0)
        def _():
            acc_ref[...] = jnp.zeros_like(acc_ref)

        @pl.when((i == 0) & (j == 0))
        def _():
            dgain_ref[...] = jnp.zeros_like(dgain_ref)

        acc_ref[...] += _dot_nt(a_ref[...], w_ref[...])

        @pl.when(j == nk - 1)
        def _():
            dx, dgain = _rms_bwd(acc_ref[...], h_ref[...], g_ref[...])
            dh = dho_ref[...] + dx
            dh_ref[...] = dh
            dhb_ref[...] = dh.astype(BF16)
            dgain_ref[...] += dgain

    row = pl.BlockSpec((tm, D_MODEL), lambda i, j: (i, 0))
    vec = pl.BlockSpec((1, D_MODEL), lambda i, j: (0, 0))
    return _pc(
        body, name=name, grid=(s // tm, nk),
        in_specs=[pl.BlockSpec((tm, tk), lambda i, j: (i, j)), pl.BlockSpec((D_MODEL, tk), lambda i, j: (0, j)),
                  row, vec, row],
        out_specs=[row, row, vec],
        out_shape=[_sds((s, D_MODEL), F32), _sds((s, D_MODEL), BF16), _sds((1, D_MODEL), F32)],
        scratch_shapes=[pltpu.VMEM((tm, D_MODEL), F32)],
    )(a, w, h, gain, dho)


def _matmul_nt(a, w, name):
    s, kd = a.shape
    n = w.shape[0]
    tm = _row_block(s)

    def body(a_ref, w_ref, o_ref):
        o_ref[...] = _dot_nt(a_ref[...], w_ref[...]).astype(BF16)

    return _pc(
        body, name=name, grid=(s // tm,),
        in_specs=[pl.BlockSpec((tm, kd), lambda i: (i, 0)), pl.BlockSpec((n, kd), lambda i: (0, 0))],
        out_specs=pl.BlockSpec((tm, n), lambda i: (i, 0)),
        out_shape=_sds((s, n), BF16),
    )(a, w)


def _matmul_tn(a, b, name):
    s, m = a.shape
    n = b.shape[1]
    tmm = min(m, 1024)
    tn = n if n <= 1024 else 1024
    tk = _row_block(s)
    nk = s // tk

    def body(a_ref, b_ref, o_ref, acc_ref):
        k = pl.program_id(2)

        @pl.when(k == 0)
        def _():
            acc_ref[...] = jnp.zeros_like(acc_ref)

        acc_ref[...] += _dot_tn(a_ref[...], b_ref[...])

        @pl.when(k == nk - 1)
        def _():
            o_ref[...] = acc_ref[...].astype(BF16)

    return _pc(
        body, name=name, grid=(m // tmm, n // tn, nk),
        in_specs=[pl.BlockSpec((tk, tmm), lambda i, j, k: (k, i)), pl.BlockSpec((tk, tn), lambda i, j, k: (k, j))],
        out_specs=pl.BlockSpec((tmm, tn), lambda i, j, k: (i, j)),
        out_shape=_sds((m, n), BF16),
        scratch_shapes=[pltpu.VMEM((tmm, tn), F32)],
    )(a, b)


def _matmul_tn_heads(a, b, name):
    s, m = a.shape
    nh, _, n = b.shape
    tk = _row_block(s)
    nk = s // tk

    def body(a_ref, b_ref, o_ref, acc_ref):
        k = pl.program_id(1)

        @pl.when(k == 0)
        def _():
            acc_ref[...] = jnp.zeros_like(acc_ref)

        acc_ref[...] += _dot_tn(a_ref[...], b_ref[0])

        @pl.when(k == nk - 1)
        def _():
            o_ref[0] = acc_ref[...].astype(BF16)

    return _pc(
        body, name=name, grid=(nh, nk),
        in_specs=[pl.BlockSpec((tk, m), lambda h, k: (k, 0)), pl.BlockSpec((1, tk, n), lambda h, k: (h, k, 0))],
        out_specs=pl.BlockSpec((1, m, n), lambda h, k: (h, 0, 0)),
        out_shape=_sds((nh, m, n), BF16),
        scratch_shapes=[pltpu.VMEM((m, n), F32)],
    )(a, b)


def _conv_bwd(dbuc, bcx, cw):
    s = bcx.shape[0]
    tc = _CONV_COLS
    nc = D_MODEL // tc

    def body(d_ref, b_ref, c_ref, xp_ref, w_ref, db_ref, dc_ref, dxp_ref, dw_ref):
        rows = lax.broadcasted_iota(jnp.int32, (s, tc), 0)
        c = c_ref[...].astype(F32)
        xp = xp_ref[...].astype(F32)
        u = c * xp
        u1 = _shift_down(u, 1, rows)
        u2 = _shift_down(u, 2, rows)
        w = w_ref[...]
        uc = w[2:3] * u + w[1:2] * u1 + w[0:1] * u2
        d = d_ref[...].astype(F32)
        db_ref[...] = (d * uc).astype(BF16)
        duc = d * b_ref[...].astype(F32)
        du = w[2:3] * duc + w[1:2] * _shift_up(duc, 1, rows, s) + w[0:1] * _shift_up(duc, 2, rows, s)
        dc_ref[...] = (du * xp).astype(BF16)
        dxp_ref[...] = (du * c).astype(BF16)
        dw_ref[0:1, :] = jnp.sum(duc * u2, axis=0, keepdims=True)
        dw_ref[1:2, :] = jnp.sum(duc * u1, axis=0, keepdims=True)
        dw_ref[2:3, :] = jnp.sum(duc * u, axis=0, keepdims=True)

    col = lambda off: pl.BlockSpec((s, tc), lambda j: (0, off + j))
    tap = pl.BlockSpec((3, tc), lambda j: (0, j))
    return _pc(
        body, name="conv_bwd", grid=(nc,),
        in_specs=[col(0), col(0), col(nc), col(2 * nc), tap],
        out_specs=[col(0), col(0), col(0), tap],
        out_shape=[_sds((s, D_MODEL), BF16), _sds((s, D_MODEL), BF16), _sds((s, D_MODEL), BF16),
                   _sds((3, D_MODEL), F32)],
    )(dbuc, bcx, bcx, bcx, cw)


def _attn_bwd_dq(q, k, v, do, o, lse, cc, sn, sp, carry=None):
    s = q.shape[1]
    bq = _row_block(s)

    def body(q_ref, k_ref, v_ref, do_ref, o_ref, lse_ref, cc_ref, sn_ref, sp_ref, dq_ref):
        i = pl.program_id(1)
        qb = q_ref[0]
        dob = do_ref[...]
        delta = jnp.sum(dob.astype(F32) * o_ref[...].astype(F32), axis=-1, keepdims=True)
        lse_col = lse_ref[0][:, :1]

        def step(j, dq, masked):
            start = pl.multiple_of(j * bq, bq)
            kb = k_ref[0, pl.ds(start, bq), :]
            vb = v_ref[0, pl.ds(start, bq), :]
            sc = _dot_nt(qb, kb)
            if masked:
                sc = jnp.where(_chunk_mask(bq, bq), sc, NEG)
            p = jnp.exp2(sc - lse_col)
            dp = _dot_nt(dob, vb)
            ds = (p * (dp - delta)).astype(BF16)
            return dq + _dot(ds, kb)

        dq = lax.fori_loop(0, i, lambda j, c: step(j, c, False), jnp.zeros((bq, HEAD_PAD), F32))
        dq = step(i, dq, True) * ATT_SCALE
        dq_ref[0, :, :NOPE] = dq[:, :NOPE].astype(BF16)
        dq_ref[0, :, NOPE:] = _rope_bwd(dq[:, NOPE:], cc_ref[...], sn_ref[...], sp_ref[...]).astype(BF16)

    blk = lambda n: pl.BlockSpec((1, bq, n), lambda h, i: (h, i, 0))
    whole = lambda n: pl.BlockSpec((1, s, n), lambda h, i: (h, 0, 0))
    cols = pl.BlockSpec((bq, V_DIM), lambda h, i: (i, h))
    tab = pl.BlockSpec((bq, 128), lambda h, i: (i, 0))
    return _pc_carrying(
        body, carry, (q, k, v, do, o, lse, cc, sn, sp), name="attn_bwd_dq", grid=(N_HEADS, s // bq),
        in_specs=[blk(HEAD_PAD), whole(HEAD_PAD), whole(V_DIM), cols, cols, blk(128), tab, tab, tab],
        out_specs=[blk(HEAD_PAD)],
        out_shape=[_sds((N_HEADS, s, HEAD_PAD), BF16)],
    )


def _attn_bwd_dkv(q, k, v, do, o, lse, carry=None):
    s = q.shape[1]
    bk = _row_block(s)
    nb = s // bk

    def body(q_ref, k_ref, v_ref, do_ref, o_ref, lse_ref, dk_ref, dv_ref):
        j = pl.program_id(1)
        kb = k_ref[0]
        vb = v_ref[0]

        def step(i, carry, masked):
            dk, dv = carry
            start = pl.multiple_of(i * bk, bk)
            qb = q_ref[0, pl.ds(start, bk), :]
            dob = do_ref[pl.ds(start, bk), :]
            ob = o_ref[pl.ds(start, bk), :]
            lse_col = lse_ref[0, pl.ds(start, bk), :][:, :1]
            delta = jnp.sum(dob.astype(F32) * ob.astype(F32), axis=-1, keepdims=True)
            sc = _dot_nt(qb, kb)
            if masked:
                sc = jnp.where(_chunk_mask(bk, bk), sc, NEG)
            p = jnp.exp2(sc - lse_col)
            dv = dv + _dot_tn(p.astype(BF16), dob)
            dp = _dot_nt(dob, vb)
            ds = (p * (dp - delta)).astype(BF16)
            dk = dk + _dot_tn(ds, qb)
            return dk, dv

        carry = step(j, (jnp.zeros((bk, HEAD_PAD), F32), jnp.zeros((bk, V_DIM), F32)), True)
        dk, dv = lax.fori_loop(j + 1, nb, lambda i, c: step(i, c, False), carry)
        dk_ref[0] = (dk * LN_2).astype(BF16)
        dv_ref[0] = dv.astype(BF16)

    blk = lambda n: pl.BlockSpec((1, bk, n), lambda h, j: (h, j, 0))
    whole = lambda n: pl.BlockSpec((1, s, n), lambda h, j: (h, 0, 0))
    cols = pl.BlockSpec((s, V_DIM), lambda h, j: (0, h))
    return _pc_carrying(
        body, carry, (q, k, v, do, o, lse), name="attn_bwd_dkv", grid=(N_HEADS, nb),
        in_specs=[whole(HEAD_PAD), blk(HEAD_PAD), blk(V_DIM), cols, cols, whole(128)],
        out_specs=[blk(HEAD_PAD), blk(V_DIM)],
        out_shape=[_sds((N_HEADS, s, HEAD_PAD), BF16), _sds((N_HEADS, s, V_DIM), BF16)],
    )


def _mla_mid_bwd(dq, dk, dv, proj, w_uq, w_ukv, g_cq, g_ckv, cc, sn, sp):
    s = dq.shape[1]
    tm = _row_block(s)

    def body(dq_ref, dk_ref, dv_ref, proj_ref, wuq_ref, wukv_ref, gcq_ref, gckv_ref, cc_ref, sn_ref, sp_ref,
             dproj_ref, dkv_ref, dgcq_ref, dgckv_ref, acq_ref, ackv_ref, akr_ref):
        i = pl.program_id(0)
        h = pl.program_id(1)

        @pl.when(h == 0)
        def _():
            acq_ref[...] = jnp.zeros_like(acq_ref)
            ackv_ref[...] = jnp.zeros_like(ackv_ref)
            akr_ref[...] = jnp.zeros_like(akr_ref)

        @pl.when((i == 0) & (h == 0))
        def _():
            dgcq_ref[...] = jnp.zeros_like(dgcq_ref)
            dgckv_ref[...] = jnp.zeros_like(dgckv_ref)

        dkb = dk_ref[0]
        dkv_ref[0, :, :NOPE] = dkb[:, :NOPE]
        dkv_ref[0, :, NOPE:] = dv_ref[0]
        acq_ref[...] += _dot_nt(dq_ref[0], wuq_ref[0])
        ackv_ref[...] += _dot_nt(dkv_ref[0], wukv_ref[0])
        akr_ref[...] += dkb[:, NOPE:].astype(F32)

        @pl.when(h == N_HEADS - 1)
        def _():
            proj = proj_ref[...]
            dcq, dgcq = _rms_bwd(acq_ref[...], proj[:, :CQ], gcq_ref[...])
            dckv, dgckv = _rms_bwd(ackv_ref[...], proj[:, CQ:CQ + CKV], gckv_ref[...])
            dproj_ref[:, :CQ] = dcq.astype(BF16)
            dproj_ref[:, CQ:CQ + CKV] = dckv.astype(BF16)
            dproj_ref[:, CQ + CKV:] = _rope_bwd(akr_ref[...], cc_ref[...], sn_ref[...], sp_ref[...]).astype(BF16)
            dgcq_ref[...] += dgcq
            dgckv_ref[...] += dgckv

    head_blk = lambda n: pl.BlockSpec((1, tm, n), lambda i, h: (h, i, 0))
    head_w = lambda a: pl.BlockSpec((1,) + a.shape[1:], lambda i, h: (h, 0, 0))
    row = lambda n: pl.BlockSpec((tm, n), lambda i, h: (i, 0))
    vec = lambda n: pl.BlockSpec((1, n), lambda i, h: (0, 0))
    return _pc(
        body, name="mla_mid_bwd", grid=(s // tm, N_HEADS),
        in_specs=[head_blk(HEAD_PAD), head_blk(HEAD_PAD), head_blk(V_DIM), row(PROJ_PAD), head_w(w_uq), head_w(w_ukv),
                  vec(CQ), vec(CKV), row(128), row(128), row(128)],
        out_specs=[row(PROJ_PAD), head_blk(HEAD_PAD), vec(CQ), vec(CKV)],
        out_shape=[_sds((s, PROJ_PAD), BF16), _sds((N_HEADS, s, HEAD_PAD), BF16), _sds((1, CQ), F32),
                   _sds((1, CKV), F32)],
        scratch_shapes=[pltpu.VMEM((tm, CQ), F32), pltpu.VMEM((tm, CKV), F32), pltpu.VMEM((tm, 128), F32)],
    )(dq, dk, dv, proj, w_uq, w_ukv, g_cq, g_ckv, cc, sn, sp)


def _peer(k):
    x, y, c = lax.axis_index("x"), lax.axis_index("y"), lax.axis_index("c")
    px = 1 - x if k & 4 else x
    py = 1 - y if k & 2 else y
    pc = 1 - c if k & 1 else c
    return (px, py, pc), 4 * px + 2 * py + pc


def _exchange_ops(arrays, scatter):
    nw = len(arrays)
    direct = range(1, N_DEV) if scatter else (1, 2, 4, 6)

    def copy(w, k, src, dst, to, send_sems, recv_sems):
        return pltpu.make_async_remote_copy(
            src_ref=src, dst_ref=dst, send_sem=send_sems.at[w * N_DEV + k], recv_sem=recv_sems.at[w * N_DEV + k],
            device_id=to, device_id_type=pl.DeviceIdType.MESH)

    def own_copies(ins, outs, local_sems):
        _, me = _peer(0)
        return [pltpu.make_async_copy(ins[w].at[me] if scatter else ins[w], outs[w].at[me], local_sems.at[w])
                for w in range(nw)]

    def sends(ins, outs, send_sems, recv_sems):
        _, me = _peer(0)
        out = []
        for k in direct:
            dev, idx = _peer(k)
            for w in range(nw):
                out.append(copy(w, k, ins[w].at[idx] if scatter else ins[w], outs[w].at[me], dev, send_sems, recv_sems))
        return out

    def relays(outs, send_sems, recv_sems):
        sibling, _ = _peer(1)
        out = []
        for k in (2, 4, 6):
            _, idx = _peer(k)
            for w in range(nw):
                out.append(copy(w, k + 1, outs[w].at[idx], outs[w].at[idx], sibling, send_sems, recv_sems))
        return out

    def arrival(outs, w, k, send_sems, recv_sems):
        dev, idx = _peer(k)
        return copy(w, k, outs[w].at[idx], outs[w].at[idx], dev, send_sems, recv_sems)

    def start(ins, outs, send_sems, recv_sems, local_sems):
        for cp in own_copies(ins, outs, local_sems) + sends(ins, outs, send_sems, recv_sems):
            cp.start()

    def relay(ins, outs, send_sems, recv_sems, local_sems):
        for k in (2, 4, 6):
            for w in range(nw):
                arrival(outs, w, k, send_sems, recv_sems).wait_recv()
        for cp in relays(outs, send_sems, recv_sems):
            cp.start()

    def wait(ins, outs, send_sems, recv_sems, local_sems):
        for cp in own_copies(ins, outs, local_sems):
            cp.wait()
        for cp in sends(ins, outs, send_sems, recv_sems) + ([] if scatter else relays(outs, send_sems, recv_sems)):
            cp.wait_send()
        for k in (range(1, N_DEV) if scatter else (1, 3, 5, 7)):
            for w in range(nw):
                arrival(outs, w, k, send_sems, recv_sems).wait_recv()

    landed = [_sds(a.shape if scatter else (N_DEV,) + a.shape, a.dtype) for a in arrays]
    sems = [pltpu.SemaphoreType.DMA((nw * N_DEV,)), pltpu.SemaphoreType.DMA((nw * N_DEV,)),
            pltpu.SemaphoreType.DMA((nw,))]
    return start, (None if scatter else relay), wait, landed, sems


def _exchange(arrays, scatter, name):
    nw = len(arrays)
    start, relay, wait, landed, sems = _exchange_ops(arrays, scatter)

    def body(*refs):
        ins, outs, csem = refs[:nw], refs[nw:2 * nw], refs[2 * nw:]
        start(ins, outs, *csem)
        if relay is not None:
            relay(ins, outs, *csem)
        wait(ins, outs, *csem)

    any_spec = pl.BlockSpec(memory_space=pl.ANY)
    return _pc(body, name=name, in_specs=[any_spec] * nw, out_specs=[any_spec] * nw, out_shape=landed,
               scratch_shapes=sems)(*arrays)


def _adam_math(g, w, m, v):
    m = ADAM_B1 * m + (1.0 - ADAM_B1) * g
    v = ADAM_B2 * v + (1.0 - ADAM_B2) * jnp.square(g)
    m_hat = m / (1.0 - ADAM_B1 ** ADAM_STEP)
    v_hat = v / (1.0 - ADAM_B2 ** ADAM_STEP)
    delta = -ADAM_LR * (m_hat / (jnp.sqrt(v_hat) + ADAM_EPS) + ADAM_WD * w)
    return delta, m, v


def _adam_rows(r):
    for t in (256, 128, 64, 32, 16):
        if r % t == 0:
            return t
    return r


def _adamw(parts, w, m, v, name):
    r, c = w.shape
    tr = _adam_rows(r)

    def body(p_ref, w_ref, m_ref, v_ref, g_ref, d_ref, mo_ref, vo_ref):
        g = p_ref[0].astype(F32)
        for src in range(1, N_DEV):
            g = g + p_ref[src].astype(F32)
        delta, m2, v2 = _adam_math(g, w_ref[...], m_ref[...], v_ref[...])
        g_ref[...] = g
        d_ref[...] = delta
        mo_ref[...] = m2
        vo_ref[...] = v2

    blk = pl.BlockSpec((tr, c), lambda i: (i, 0))
    return _pc(
        body, name=name, grid=(r // tr,),
        in_specs=[pl.BlockSpec((N_DEV, tr, c), lambda i: (0, i, 0)), blk, blk, blk],
        out_specs=[blk, blk, blk, blk],
        out_shape=[_sds((r, c), F32)] * 4,
    )(parts, w, m, v)


def _small_allreduce_adamw(gpack, wpack, mpack, vpack):
    shape = gpack.shape

    def body(g_ref, w_ref, m_ref, v_ref, go_ref, d_ref, mo_ref, vo_ref, gath_ref, send_sems, recv_sems):
        _, me = _peer(0)
        gath_ref[me] = g_ref[...]
        copies = []
        for k in range(1, N_DEV):
            dev, idx = _peer(k)
            copies.append(pltpu.make_async_remote_copy(
                src_ref=g_ref, dst_ref=gath_ref.at[me], send_sem=send_sems.at[k], recv_sem=recv_sems.at[k],
                device_id=dev, device_id_type=pl.DeviceIdType.MESH))
        for cp in copies:
            cp.start()
        for cp in copies:
            cp.wait_send()
        for k in range(1, N_DEV):
            dev, idx = _peer(k)
            pltpu.make_async_remote_copy(
                src_ref=g_ref, dst_ref=gath_ref.at[idx], send_sem=send_sems.at[k], recv_sem=recv_sems.at[k],
                device_id=dev, device_id_type=pl.DeviceIdType.MESH).wait_recv()
        g = gath_ref[0]
        for src in range(1, N_DEV):
            g = g + gath_ref[src]
        delta, m2, v2 = _adam_math(g, w_ref[...], m_ref[...], v_ref[...])
        go_ref[...] = g
        d_ref[...] = delta
        mo_ref[...] = m2
        vo_ref[...] = v2

    vm = pl.BlockSpec(memory_space=pltpu.VMEM)
    return _pc(
        body, name="small_allreduce_adamw",
        in_specs=[vm] * 4, out_specs=[vm] * 4, out_shape=[_sds(shape, F32)] * 4,
        scratch_shapes=[pltpu.VMEM((N_DEV,) + shape, F32), pltpu.SemaphoreType.DMA((N_DEV,)),
                        pltpu.SemaphoreType.DMA((N_DEV,))],
    )(gpack, wpack, mpack, vpack)


def _cols_from_shards(g, pad_to=None):
    if pad_to is not None and pad_to != g.shape[-1]:
        g = jnp.pad(g, ((0, 0), (0, 0), (0, pad_to - g.shape[-1])))
    return jnp.transpose(g, (1, 0, 2)).reshape(g.shape[1], N_DEV * g.shape[2])


def _shards_from_cols(a, n):
    k = a.shape[0]
    a = a.reshape(k, N_DEV, a.shape[1] // N_DEV)[:, :, :n]
    return jnp.transpose(a, (1, 0, 2))


def _rows_from_shards(g, pad_to=None):
    if pad_to is not None and pad_to != g.shape[1]:
        g = jnp.pad(g, ((0, 0), (0, pad_to - g.shape[1]), (0, 0)))
    return g.reshape(N_DEV * g.shape[1], g.shape[2])


def _shards_from_rows(a, r):
    n = a.shape[1]
    return a.reshape(N_DEV, a.shape[0] // N_DEV, n)[:, :r, :]


def _rope_tables(pos):
    inv_freq = 1.0 / (ROPE_THETA ** (jnp.arange(0, ROPE, 2, dtype=F32) / ROPE))
    ang = pos.astype(F32)[:, None] * inv_freq
    cos, sin = jnp.cos(ang), jnp.sin(ang)
    z32, z64, z96 = (jnp.zeros((pos.shape[0], n), F32) for n in (32, 64, 96))
    return (jnp.concatenate([cos, cos, z64], axis=1), jnp.concatenate([-sin, z96], axis=1),
            jnp.concatenate([z32, sin, z64], axis=1))


def _pad_row(vec):
    vec = vec.reshape(1, -1)
    return jnp.pad(vec, ((0, 0), (0, D_MODEL - vec.shape[1])))


def _forward_backward(x, target, tables, gains, shards):
    cc, sn, sp = tables
    ffn_g = gains["ffn_norm"]

    def gather(names, relay_at):
        return [shards[n] for n in names], False, relay_at

    names = ["w_in", "w_uq", "w_ukv", "w_o", "conv_norm", "conv_w"]
    got = dict(zip(names, _exchange([shards[n] for n in names], False, "gather_mla")))
    w_in = jnp.pad(_rows_from_shards(got["w_in"]), ((0, 0), (0, PROJ_PAD - PROJ)))
    w_uq = jnp.pad(got["w_uq"], ((0, 0), (0, 0), (0, HEAD_PAD - NOPE - ROPE)))
    w_ukv = got["w_ukv"]
    w_o = _rows_from_shards(got["w_o"])
    conv_norm = got["conv_norm"].reshape(1, D_MODEL)
    conv_w = jnp.transpose(got["conv_w"], (1, 0, 2)).reshape(3, D_MODEL)

    n0, proj, cqn, ckvn, krr = _mla_in_fwd(x, gains["mla_norm"], w_in, gains["g_cq"], gains["g_ckv"], cc, sn, sp)
    q, k, v = _qkv_proj(cqn, ckvn, krr, w_uq, w_ukv, cc, sn, sp)
    names = ["wg0", "wu0", "wd0", "c_in", "c_out"]
    (o, lse), got = _attn_fwd(q, k, v, carry=gather(names, 0.65))
    got = dict(zip(names, got))
    wg0, wu0 = (_cols_from_shards(got[n], FF_SHARD_PAD) for n in ("wg0", "wu0"))
    wd0 = _rows_from_shards(got["wd0"], FF_SHARD_PAD)
    c_in = _cols_from_shards(got["c_in"])
    c_out = _rows_from_shards(got["c_out"])
    h1 = _matmul_res(o, w_o, x, "mla_out_fwd")
    names = ["wg1", "wu1", "wd1"]
    (h2, n1, gate0, up0), got = _ffn_fwd(h1, ffn_g[0:1], wg0, wu0, wd0, "ffn0_fwd", carry=gather(names, 0.85))
    got = dict(zip(names, got))
    wg1, wu1 = (_cols_from_shards(got[n], FF_SHARD_PAD) for n in ("wg1", "wu1"))
    wd1 = _rows_from_shards(got["wd1"], FF_SHARD_PAD)
    n2, bcx = _rms_matmul(h2, conv_norm, c_in, "conv_in_fwd")
    bu = _conv_fwd(bcx, conv_w)
    h3 = _matmul_res(bu, c_out, h2, "conv_out_fwd")
    (h4, n3, gate1, up1), _ = _ffn_fwd(h3, ffn_g[1:2], wg1, wu1, wd1, "ffn1_fwd")
    dh4, d_final, loss = _final_loss(h4, gains["final_norm"], target)

    small = {"final_norm": d_final}
    dh3, dh3_b, dh4_b, dgate, dup, act, d_ffn1 = _ffn_bwd_x(dh4, h3, ffn_g[1:2], gate1, up1, wg1, wu1, wd1, "ffn1_bwd")
    dwg1 = _matmul_tn(n3, dgate, "ffn1_dwg")
    dwu1 = _matmul_tn(n3, dup, "ffn1_dwu")
    dwd1 = _matmul_tn(act, dh4_b, "ffn1_dwd")
    dbuc = _matmul_nt(dh3_b, c_out, "conv_out_bwd")
    d_c_out = _matmul_tn(bu, dh3_b, "conv_dwout")
    db, dc, dxp, small["conv_w"] = _conv_bwd(dbuc, bcx, conv_w)
    dbcx = jnp.concatenate([db, dc, dxp], axis=1)
    d_c_in = _matmul_tn(n2, dbcx, "conv_dwin")
    dh2, dh2_b, small["conv_norm"] = _nt_rmsbwd(dbcx, c_in, h2, conv_norm, dh3, "conv_in_bwd")
    dh1, dh1_b, dh2_b2, dgate, dup, act, d_ffn0 = _ffn_bwd_x(dh2, h1, ffn_g[0:1], gate0, up0, wg0, wu0, wd0, "ffn0_bwd")
    dwg0 = _matmul_tn(n1, dgate, "ffn0_dwg")
    dwu0 = _matmul_tn(n1, dup, "ffn0_dwu")
    dwd0 = _matmul_tn(act, dh2_b2, "ffn0_dwd")
    small["ffn_norm"] = jnp.concatenate([d_ffn0, d_ffn1], axis=0)
    do = _matmul_nt(dh1_b, w_o, "mla_out_bwd")
    d_w_o = _matmul_tn(o, dh1_b, "mla_dwo")
    send = dict(
        wg=jnp.stack([_shards_from_cols(g, FF_SHARD) for g in (dwg0, dwg1)], axis=1),
        wu=jnp.stack([_shards_from_cols(g, FF_SHARD) for g in (dwu0, dwu1)], axis=1),
        wd=jnp.stack([_shards_from_rows(g, FF_SHARD) for g in (dwd0, dwd1)], axis=1),
        c_in=_shards_from_cols(d_c_in, 3 * D_MODEL // N_DEV),
        c_out=_shards_from_rows(d_c_out, D_MODEL // N_DEV),
        w_o=_shards_from_rows(d_w_o, D_MODEL // N_DEV),
    )
    parts = {}
    names = ["wg", "wu"]
    (dq,), got = _attn_bwd_dq(q, k, v, do, o, lse, cc, sn, sp, carry=([send[n] for n in names], True, None))
    parts.update(zip(names, got))
    names = ["wd", "c_in", "c_out", "w_o"]
    (dk, dv), got = _attn_bwd_dkv(q, k, v, do, o, lse, carry=([send[n] for n in names], True, None))
    parts.update(zip(names, got))
    dproj, dkv, small["g_cq"], small["g_ckv"] = _mla_mid_bwd(
        dq, dk, dv, proj, w_uq, w_ukv, gains["g_cq"], gains["g_ckv"], cc, sn, sp)
    d_w_uq = _matmul_tn_heads(cqn, dq, "mla_dwuq")
    d_w_ukv = _matmul_tn_heads(ckvn, dkv, "mla_dwukv")
    d_w_in = _matmul_tn(n0, dproj, "mla_dwin")
    dx, _, small["mla_norm"] = _nt_rmsbwd(dproj, w_in, x, gains["mla_norm"], dh1, "mla_in_bwd")
    names = ["w_in", "w_uq", "w_ukv"]
    last = [_shards_from_rows(d_w_in[:, :PROJ], D_MODEL // N_DEV), d_w_uq[:, :, :NOPE + ROPE], d_w_ukv]
    parts.update(zip(names, _exchange(last, True, "scatter_mla")))
    return loss[0, 0], dx, parts, small


def kernel(x, positions, mla_norm, mla_w_in, mla_g_cq, mla_g_ckv, mla_w_uq, mla_w_ukv, mla_w_o, conv_norm, conv_w_in, conv_w, conv_w_out, ffn_norm, ffn_w_gate, ffn_w_up, ffn_w_down, final_norm, loss_target, m_mla_norm, m_mla_w_in, m_mla_g_cq, m_mla_g_ckv, m_mla_w_uq, m_mla_w_ukv, m_mla_w_o, m_conv_norm, m_conv_w_in, m_conv_w, m_conv_w_out, m_ffn_norm, m_ffn_w_gate, m_ffn_w_up, m_ffn_w_down, m_final_norm, v_mla_norm, v_mla_w_in, v_mla_g_cq, v_mla_g_ckv, v_mla_w_uq, v_mla_w_ukv, v_mla_w_o, v_conv_norm, v_conv_w_in, v_conv_w, v_conv_w_out, v_ffn_norm, v_ffn_w_gate, v_ffn_w_up, v_ffn_w_down, v_final_norm):
    me = 4 * lax.axis_index("x") + 2 * lax.axis_index("y") + lax.axis_index("c")

    shards = dict(w_in=mla_w_in[0], w_uq=mla_w_uq[0], w_ukv=mla_w_ukv[0], w_o=mla_w_o[0], c_in=conv_w_in[0],
                  c_out=conv_w_out[0], wg0=ffn_w_gate[0], wg1=ffn_w_gate[1], wu0=ffn_w_up[0], wu1=ffn_w_up[1],
                  wd0=ffn_w_down[0], wd1=ffn_w_down[1])
    shards = {n: a.astype(BF16) for n, a in shards.items()}
    shards.update(conv_norm=conv_norm, conv_w=conv_w[0])
    gains = dict(mla_norm=mla_norm, g_cq=mla_g_cq, g_ckv=mla_g_ckv, ffn_norm=ffn_norm,
                 final_norm=final_norm.reshape(1, -1))
    loss_local, dx, parts, grads = _forward_backward(x[0], loss_target[0], _rope_tables(positions[0]), gains, shards)
    loss = lax.psum(loss_local, ("x", "y", "c"))
    names = ["w_in", "w_uq", "w_ukv", "w_o", "c_in", "c_out", "wg", "wu", "wd"]

    moments = dict(w_in=(m_mla_w_in, v_mla_w_in), w_uq=(m_mla_w_uq, v_mla_w_uq), w_ukv=(m_mla_w_ukv, v_mla_w_ukv),
                   w_o=(m_mla_w_o, v_mla_w_o), c_in=(m_conv_w_in, v_conv_w_in), c_out=(m_conv_w_out, v_conv_w_out),
                   wg=(m_ffn_w_gate, v_ffn_w_gate), wu=(m_ffn_w_up, v_ffn_w_up), wd=(m_ffn_w_down, v_ffn_w_down))
    full = dict(w_in=mla_w_in, w_uq=mla_w_uq, w_ukv=mla_w_ukv, w_o=mla_w_o, c_in=conv_w_in, c_out=conv_w_out,
                wg=ffn_w_gate, wu=ffn_w_up, wd=ffn_w_down)
    res = {}
    for n in names:
        w = full[n]
        shape = w.shape
        c = shape[-1]
        flat = lambda a: a.reshape(-1, c)
        outs = _adamw(parts[n].reshape(N_DEV, -1, c), flat(w), flat(moments[n][0]), flat(moments[n][1]), "adamw_" + n)
        res[n] = [o.reshape(shape) for o in outs]

    col0 = me * (D_MODEL // N_DEV)
    zeros_row = jnp.zeros((1, D_MODEL), F32)

    def place(shard):
        return lax.dynamic_update_slice(jnp.zeros((shard.shape[0], D_MODEL), F32), shard, (0, col0))

    def pack(mla_n, g_cq, g_ckv, ffn_n, fin_n, conv_n, conv_taps):
        rows = [mla_n, _pad_row(g_cq), _pad_row(g_ckv), ffn_n, fin_n.reshape(1, -1), conv_n, conv_taps]
        rows.append(jnp.zeros((SMALL_ROWS - sum(r.shape[0] for r in rows), D_MODEL), F32))
        return jnp.concatenate(rows, axis=0)

    gpack = pack(grads["mla_norm"], grads["g_cq"], grads["g_ckv"], grads["ffn_norm"], grads["final_norm"],
                 grads["conv_norm"], grads["conv_w"])
    wpack = pack(mla_norm, mla_g_cq, mla_g_ckv, ffn_norm, final_norm, place(conv_norm), place(conv_w[0]))
    mpack = pack(m_mla_norm, m_mla_g_cq, m_mla_g_ckv, m_ffn_norm, m_final_norm, place(m_conv_norm), place(m_conv_w[0]))
    vpack = pack(v_mla_norm, v_mla_g_cq, v_mla_g_ckv, v_ffn_norm, v_final_norm, place(v_conv_norm), place(v_conv_w[0]))
    small = _small_allreduce_adamw(gpack, wpack, mpack, vpack)

    def unpack(p):
        own = lambda rows: lax.dynamic_slice(rows, (0, col0), (rows.shape[0], D_MODEL // N_DEV))
        return dict(mla_norm=p[0:1], g_cq=p[1:2, :CQ], g_ckv=p[2:3, :CKV], ffn_norm=p[3:5], final_norm=p[5],
                    conv_norm=own(p[6:7]), conv_w=own(p[7:10])[None])

    small = [unpack(p) for p in small]
    order = ["mla_norm", "w_in", "g_cq", "g_ckv", "w_uq", "w_ukv", "w_o", "conv_norm", "c_in", "conv_w", "c_out",
             "ffn_norm", "wg", "wu", "wd", "final_norm"]
    out = [loss, dx[None]]
    for kind in range(4):
        for n in order:
            out.append(res[n][kind] if n in res else small[kind][n])
    return tuple(out)
```

```python
import math

import jax
import jax.numpy as jnp
from jax import lax
from jax.experimental import pallas as pl
from jax.experimental.pallas import tpu as pltpu

F32 = jnp.float32
BF16 = jnp.bfloat16

N_DEV = 8
D_MODEL = 1024
N_HEADS = 8
NOPE = 128
ROPE = 64
V_DIM = 128
HEAD_PAD = 256
CQ = 512
CKV = 256
PROJ = CQ + CKV + ROPE
PROJ_PAD = CQ + CKV + 128
D_FF = 2816
FF_SHARD = D_FF // N_DEV
FF_SHARD_PAD = 384
FF_PAD = FF_SHARD_PAD * N_DEV
CHUNK_SHIFT = 6
RMS_EPS = 1e-6
ROPE_THETA = 10000.0
ATT_SCALE = 1.0 / math.sqrt(NOPE + ROPE)
LOG2_E = math.log2(math.e)
LN_2 = math.log(2.0)
Q_SCALE = ATT_SCALE * LOG2_E
NEG = -1e30

ADAM_LR = 0.001
ADAM_B1 = 0.9
ADAM_B2 = 0.999
ADAM_EPS = 1e-08
ADAM_WD = 0.01
ADAM_STEP = 10

SMALL_PIECES = (1, 1, 1, 2, 1, 1, 3)
SMALL_ROWS = 16

_NT = (((1,), (1,)), ((), ()))
_TN = (((0,), (0,)), ((), ()))


def _pc(body, *, name, out_shape, grid=(), in_specs=None, out_specs=None, scratch_shapes=(), vmem_mb=None):
    params = {}
    if vmem_mb is not None:
        params["vmem_limit_bytes"] = vmem_mb << 20
    kwargs = dict(
        name=name, out_shape=out_shape, grid=grid, scratch_shapes=scratch_shapes,
        compiler_params=pltpu.CompilerParams(**params),
    )
    if in_specs is not None:
        kwargs["in_specs"] = in_specs
    if out_specs is not None:
        kwargs["out_specs"] = out_specs
    return pl.pallas_call(body, **kwargs)


def _pc_carrying(body, carry, operands, *, name, out_shape, grid, in_specs, out_specs, scratch_shapes=()):
    if carry is None:
        return _pc(body, name=name, out_shape=out_shape, grid=grid, in_specs=in_specs, out_specs=out_specs,
                   scratch_shapes=scratch_shapes)(*operands), None
    arrays, scatter, relay_at = carry
    nw, n_in, n_out, n_scr = len(arrays), len(in_specs), len(out_shape), len(scratch_shapes)
    start, relay, wait, landed_shapes, sems = _exchange_ops(arrays, scatter)
    n_steps = math.prod(grid)
    relay_step = None if relay is None else max(1, min(int(relay_at * n_steps), n_steps - 2))

    def wrapped(*refs):
        ins, rest = refs[:n_in], refs[n_in:]
        cin, rest = rest[:nw], rest[nw:]
        outs, rest = rest[:n_out], rest[n_out:]
        cout, rest = rest[:nw], rest[nw:]
        scr, csem = rest[:n_scr], rest[n_scr:]
        step = pl.program_id(0)
        for a in range(1, len(grid)):
            step = step * grid[a] + pl.program_id(a)

        @pl.when(step == 0)
        def _():
            start(cin, cout, *csem)

        if relay is not None:
            @pl.when(step == relay_step)
            def _():
                relay(cin, cout, *csem)

        body(*ins, *outs, *scr)

        @pl.when(step == n_steps - 1)
        def _():
            wait(cin, cout, *csem)

    any_spec = pl.BlockSpec(memory_space=pl.ANY)
    res = _pc(
        wrapped, name=name, grid=grid,
        in_specs=list(in_specs) + [any_spec] * nw, out_specs=list(out_specs) + [any_spec] * nw,
        out_shape=list(out_shape) + landed_shapes, scratch_shapes=list(scratch_shapes) + sems,
    )(*operands, *arrays)
    return res[:n_out], res[n_out:]


def _sds(shape, dtype):
    return jax.ShapeDtypeStruct(shape, dtype)


def _dot(a, b):
    return jnp.dot(a, b, preferred_element_type=F32)


def _dot_nt(a, b):
    return lax.dot_general(a, b, _NT, preferred_element_type=F32)


def _dot_tn(a, b):
    return lax.dot_general(a, b, _TN, preferred_element_type=F32)


def _rstd(x):
    return lax.rsqrt(jnp.mean(x * x, axis=-1, keepdims=True) + RMS_EPS)


def _rms(x, g):
    return (x * _rstd(x)) * g


def _rms_bwd(dy, x, g):
    r = _rstd(x)
    xhat = x * r
    dxhat = dy * g
    dx = r * (dxhat - xhat * jnp.mean(dxhat * xhat, axis=-1, keepdims=True))
    return dx, jnp.sum(dy * xhat, axis=0, keepdims=True)


def _rope(t, cc, sn, sp):
    return t * cc + pltpu.roll(t, 96, 1) * sn + pltpu.roll(t, 32, 1) * sp


def _rope_bwd(dt, cc, sn, sp):
    return dt * cc + pltpu.roll(dt * sn, 32, 1) + pltpu.roll(dt * sp, 96, 1)


def _row_block(s):
    return min(512, s)


def _mla_in_fwd(x, g0, w_in, g_cq, g_ckv, cc, sn, sp):
    s = x.shape[0]
    tm = _row_block(s)

    def body(x_ref, g0_ref, w_ref, gcq_ref, gckv_ref, cc_ref, sn_ref, sp_ref,
             n_ref, proj_ref, cqn_ref, ckvn_ref, krr_ref):
        nb = _rms(x_ref[...], g0_ref[...]).astype(BF16)
        n_ref[...] = nb
        proj = _dot(nb, w_ref[...])
        proj_ref[...] = proj
        cqn_ref[...] = _rms(proj[:, :CQ], gcq_ref[...]).astype(BF16)
        ckvn_ref[...] = _rms(proj[:, CQ:CQ + CKV], gckv_ref[...]).astype(BF16)
        krr_ref[...] = _rope(proj[:, CQ + CKV:], cc_ref[...], sn_ref[...], sp_ref[...]).astype(BF16)

    row = lambda n: pl.BlockSpec((tm, n), lambda i: (i, 0))
    full = lambda a: pl.BlockSpec(a.shape, lambda i: (0, 0))
    return _pc(
        body, name="mla_in_fwd", grid=(s // tm,),
        in_specs=[row(D_MODEL), full(g0), full(w_in), full(g_cq), full(g_ckv), row(128), row(128), row(128)],
        out_specs=[row(D_MODEL), row(PROJ_PAD), row(CQ), row(CKV), row(128)],
        out_shape=[_sds((s, D_MODEL), BF16), _sds((s, PROJ_PAD), F32), _sds((s, CQ), BF16),
                   _sds((s, CKV), BF16), _sds((s, 128), BF16)],
    )(x, g0, w_in, g_cq, g_ckv, cc, sn, sp)


def _qkv_proj(cqn, ckvn, krr, w_uq, w_ukv, cc, sn, sp):
    s = cqn.shape[0]
    tm = _row_block(s)

    def body(cqn_ref, ckvn_ref, krr_ref, wuq_ref, wukv_ref, cc_ref, sn_ref, sp_ref, q_ref, k_ref, v_ref):
        q = _dot(cqn_ref[...], wuq_ref[0]) * Q_SCALE
        q_ref[0, :, :NOPE] = q[:, :NOPE].astype(BF16)
        q_ref[0, :, NOPE:] = _rope(q[:, NOPE:], cc_ref[...], sn_ref[...], sp_ref[...]).astype(BF16)
        kv = _dot(ckvn_ref[...], wukv_ref[0])
        k_ref[0, :, :NOPE] = kv[:, :NOPE].astype(BF16)
        k_ref[0, :, NOPE:] = krr_ref[...]
        v_ref[0] = kv[:, NOPE:].astype(BF16)

    row = lambda n: pl.BlockSpec((tm, n), lambda i, h: (i, 0))
    head_w = lambda a: pl.BlockSpec((1,) + a.shape[1:], lambda i, h: (h, 0, 0))
    head_o = lambda n: pl.BlockSpec((1, tm, n), lambda i, h: (h, i, 0))
    return _pc(
        body, name="qkv_proj", grid=(s // tm, N_HEADS),
        in_specs=[row(CQ), row(CKV), row(128), head_w(w_uq), head_w(w_ukv), row(128), row(128), row(128)],
        out_specs=[head_o(HEAD_PAD), head_o(HEAD_PAD), head_o(V_DIM)],
        out_shape=[_sds((N_HEADS, s, HEAD_PAD), BF16), _sds((N_HEADS, s, HEAD_PAD), BF16),
                   _sds((N_HEADS, s, V_DIM), BF16)],
    )(cqn, ckvn, krr, w_uq, w_ukv, cc, sn, sp)


def _chunk_mask(bq, bk):
    rows = lax.broadcasted_iota(jnp.int32, (bq, bk), 0)
    cols = lax.broadcasted_iota(jnp.int32, (bq, bk), 1)
    return jnp.right_shift(cols, CHUNK_SHIFT) <= jnp.right_shift(rows, CHUNK_SHIFT)


def _attn_fwd(q, k, v, carry=None):
    s = q.shape[1]
    bq = _row_block(s)

    def body(q_ref, k_ref, v_ref, o_ref, lse_ref):
        i = pl.program_id(1)
        qb = q_ref[0]

        def step(j, carry, masked):
            m, l, acc = carry
            start = pl.multiple_of(j * bq, bq)
            kb = k_ref[0, pl.ds(start, bq), :]
            vb = v_ref[0, pl.ds(start, bq), :]
            sc = _dot_nt(qb, kb)
            if masked:
                sc = jnp.where(_chunk_mask(bq, bq), sc, NEG)
            m_new = jnp.maximum(m, jnp.max(sc, axis=-1, keepdims=True))
            p = jnp.exp2(sc - m_new)
            alpha = jnp.exp2(m - m_new)
            l = alpha * l + jnp.sum(p, axis=-1, keepdims=True)
            acc = alpha * acc + _dot(p.astype(BF16), vb)
            return m_new, l, acc

        init = (jnp.full((bq, 1), NEG, F32), jnp.zeros((bq, 1), F32), jnp.zeros((bq, V_DIM), F32))
        carry = lax.fori_loop(0, i, lambda j, c: step(j, c, False), init)
        m, l, acc = step(i, carry, True)
        o_ref[...] = (acc / l).astype(BF16)
        lse_ref[0] = jnp.broadcast_to(m + jnp.log(l) * LOG2_E, (bq, 128))

    return _pc_carrying(
        body, carry, (q, k, v), name="attn_fwd", grid=(N_HEADS, s // bq),
        in_specs=[pl.BlockSpec((1, bq, HEAD_PAD), lambda h, i: (h, i, 0)),
                  pl.BlockSpec((1, s, HEAD_PAD), lambda h, i: (h, 0, 0)),
                  pl.BlockSpec((1, s, V_DIM), lambda h, i: (h, 0, 0))],
        out_specs=[pl.BlockSpec((bq, V_DIM), lambda h, i: (i, h)),
                   pl.BlockSpec((1, bq, 128), lambda h, i: (h, i, 0))],
        out_shape=[_sds((s, N_HEADS * V_DIM), BF16), _sds((N_HEADS, s, 128), F32)],
    )


def _matmul_res(a, w, res, name):
    s, kd = a.shape
    n = w.shape[1]
    tm = _row_block(s)

    def body(a_ref, w_ref, r_ref, o_ref):
        o_ref[...] = r_ref[...] + _dot(a_ref[...], w_ref[...])

    return _pc(
        body, name=name, grid=(s // tm,),
        in_specs=[pl.BlockSpec((tm, kd), lambda i: (i, 0)), pl.BlockSpec((kd, n), lambda i: (0, 0)),
                  pl.BlockSpec((tm, n), lambda i: (i, 0))],
        out_specs=pl.BlockSpec((tm, n), lambda i: (i, 0)),
        out_shape=_sds((s, n), F32),
    )(a, w, res)


def _ffn_fwd(h, gain, wg, wu, wd, name, carry=None):
    s = h.shape[0]
    tm = _row_block(s)
    tf = 512
    nf = FF_PAD // tf

    def body(h_ref, g_ref, wg_ref, wu_ref, wd_ref, o_ref, n_ref, gate_ref, up_ref, acc_ref):
        j = pl.program_id(1)

        @pl.when(j == 0)
        def _():
            n_ref[...] = _rms(h_ref[...], g_ref[...]).astype(BF16)
            acc_ref[...] = jnp.zeros_like(acc_ref)

        nb = n_ref[...]
        gate = _dot(nb, wg_ref[...])
        up = _dot(nb, wu_ref[...])
        gate_ref[...] = gate.astype(BF16)
        up_ref[...] = up.astype(BF16)
        act = gate * jax.nn.sigmoid(gate) * up
        acc_ref[...] += _dot(act.astype(BF16), wd_ref[...])

        @pl.when(j == nf - 1)
        def _():
            o_ref[...] = h_ref[...] + acc_ref[...]

    return _pc_carrying(
        body, carry, (h, gain, wg, wu, wd), name=name, grid=(s // tm, nf),
        in_specs=[pl.BlockSpec((tm, D_MODEL), lambda i, j: (i, 0)), pl.BlockSpec((1, D_MODEL), lambda i, j: (0, 0)),
                  pl.BlockSpec((D_MODEL, tf), lambda i, j: (0, j)), pl.BlockSpec((D_MODEL, tf), lambda i, j: (0, j)),
                  pl.BlockSpec((tf, D_MODEL), lambda i, j: (j, 0))],
        out_specs=[pl.BlockSpec((tm, D_MODEL), lambda i, j: (i, 0)), pl.BlockSpec((tm, D_MODEL), lambda i, j: (i, 0)),
                   pl.BlockSpec((tm, tf), lambda i, j: (i, j)), pl.BlockSpec((tm, tf), lambda i, j: (i, j))],
        out_shape=[_sds((s, D_MODEL), F32), _sds((s, D_MODEL), BF16), _sds((s, FF_PAD), BF16),
                   _sds((s, FF_PAD), BF16)],
        scratch_shapes=[pltpu.VMEM((tm, D_MODEL), F32)],
    )


def _rms_matmul(h, gain, w, name):
    s = h.shape[0]
    n = w.shape[1]
    tm = _row_block(s)
    tn = 1024
    nn = n // tn

    def body(h_ref, g_ref, w_ref, n_ref, o_ref):
        @pl.when(pl.program_id(1) == 0)
        def _():
            n_ref[...] = _rms(h_ref[...], g_ref[...]).astype(BF16)

        o_ref[...] = _dot(n_ref[...], w_ref[...]).astype(BF16)

    return _pc(
        body, name=name, grid=(s // tm, nn),
        in_specs=[pl.BlockSpec((tm, D_MODEL), lambda i, j: (i, 0)), pl.BlockSpec((1, D_MODEL), lambda i, j: (0, 0)),
                  pl.BlockSpec((D_MODEL, tn), lambda i, j: (0, j))],
        out_specs=[pl.BlockSpec((tm, D_MODEL), lambda i, j: (i, 0)), pl.BlockSpec((tm, tn), lambda i, j: (i, j))],
        out_shape=[_sds((s, D_MODEL), BF16), _sds((s, n), BF16)],
    )(h, gain, w)


def _shift_down(u, k, rows):
    return jnp.where(rows >= k, pltpu.roll(u, k, 0), 0.0)


def _shift_up(u, k, rows, s):
    return jnp.where(rows < s - k, pltpu.roll(u, s - k, 0), 0.0)


_CONV_COLS = 128


def _conv_fwd(bcx, cw):
    s = bcx.shape[0]
    tc = _CONV_COLS
    nc = D_MODEL // tc

    def body(b_ref, c_ref, xp_ref, w_ref, o_ref):
        rows = lax.broadcasted_iota(jnp.int32, (s, tc), 0)
        u = c_ref[...].astype(F32) * xp_ref[...].astype(F32)
        w = w_ref[...]
        uc = w[2:3] * u + w[1:2] * _shift_down(u, 1, rows) + w[0:1] * _shift_down(u, 2, rows)
        o_ref[...] = (b_ref[...].astype(F32) * uc).astype(BF16)

    col = lambda off: pl.BlockSpec((s, tc), lambda j: (0, off + j))
    return _pc(
        body, name="conv_fwd", grid=(nc,),
        in_specs=[col(0), col(nc), col(2 * nc), pl.BlockSpec((3, tc), lambda j: (0, j))],
        out_specs=pl.BlockSpec((s, tc), lambda j: (0, j)),
        out_shape=_sds((s, D_MODEL), BF16),
    )(bcx, bcx, bcx, cw)


def _final_loss(h, gain, target):
    s = h.shape[0]
    tm = _row_block(s)

    def body(h_ref, g_ref, t_ref, dh_ref, dg_ref, loss_ref):
        i = pl.program_id(0)

        @pl.when(i == 0)
        def _():
            dg_ref[...] = jnp.zeros_like(dg_ref)
            loss_ref[...] = jnp.zeros_like(loss_ref)

        hb = h_ref[...]
        e = _rms(hb, g_ref[...]) - t_ref[...]
        loss_ref[...] += 0.5 * jnp.sum(jnp.mean(e * e, axis=-1, keepdims=True))
        dx, dg = _rms_bwd(e * (1.0 / D_MODEL), hb, g_ref[...])
        dh_ref[...] = dx
        dg_ref[...] += dg

    row = pl.BlockSpec((tm, D_MODEL), lambda i: (i, 0))
    vec = pl.BlockSpec((1, D_MODEL), lambda i: (0, 0))
    return _pc(
        body, name="final_loss", grid=(s // tm,),
        in_specs=[row, vec, row],
        out_specs=[row, vec, pl.BlockSpec((8, 128), lambda i: (0, 0))],
        out_shape=[_sds((s, D_MODEL), F32), _sds((1, D_MODEL), F32), _sds((8, 128), F32)],
    )(h, gain, target)


def _ffn_bwd_x(dho, h, gain, gate, up, wg, wu, wd, name, carry=None):
    s = h.shape[0]
    tm = _row_block(s)
    tf = 512
    nf = FF_PAD // tf

    def body(dho_ref, h_ref, g_ref, gate_ref, up_ref, wg_ref, wu_ref, wd_ref,
             dh_ref, dhb_ref, dhob_ref, dgate_ref, dup_ref, act_ref, dgain_ref, acc_ref):
        i = pl.program_id(0)
        j = pl.program_id(1)

        @pl.when(j == 0)
        def _():
            dhob_ref[...] = dho_ref[...].astype(BF16)
            acc_ref[...] = jnp.zeros_like(acc_ref)

        @pl.when((i == 0) & (j == 0))
        def _():
            dgain_ref[...] = jnp.zeros_like(dgain_ref)

        dact = _dot_nt(dhob_ref[...], wd_ref[...])
        g = gate_ref[...].astype(F32)
        u = up_ref[...].astype(F32)
        sg = jax.nn.sigmoid(g)
        silu = g * sg
        dg = (dact * u * (sg * (1.0 + g * (1.0 - sg)))).astype(BF16)
        du = (dact * silu).astype(BF16)
        dgate_ref[...] = dg
        dup_ref[...] = du
        act_ref[...] = (silu * u).astype(BF16)
        acc_ref[...] += _dot_nt(dg, wg_ref[...]) + _dot_nt(du, wu_ref[...])

        @pl.when(j == nf - 1)
        def _():
            dx, dgain = _rms_bwd(acc_ref[...], h_ref[...], g_ref[...])
            dh = dho_ref[...] + dx
            dh_ref[...] = dh
            dhb_ref[...] = dh.astype(BF16)
            dgain_ref[...] += dgain

    row = pl.BlockSpec((tm, D_MODEL), lambda i, j: (i, 0))
    vec = pl.BlockSpec((1, D_MODEL), lambda i, j: (0, 0))
    hid = pl.BlockSpec((tm, tf), lambda i, j: (i, j))
    wcol = pl.BlockSpec((D_MODEL, tf), lambda i, j: (0, j))
    wrow = pl.BlockSpec((tf, D_MODEL), lambda i, j: (j, 0))
    return _pc_carrying(
        body, carry, (dho, h, gain, gate, up, wg, wu, wd), name=name, grid=(s // tm, nf),
        in_specs=[row, row, vec, hid, hid, wcol, wcol, wrow],
        out_specs=[row, row, row, hid, hid, hid, vec],
        out_shape=[_sds((s, D_MODEL), F32), _sds((s, D_MODEL), BF16), _sds((s, D_MODEL), BF16),
                   _sds((s, FF_PAD), BF16), _sds((s, FF_PAD), BF16), _sds((s, FF_PAD), BF16),
                   _sds((1, D_MODEL), F32)],
        scratch_shapes=[pltpu.VMEM((tm, D_MODEL), F32)],
    )


def _nt_rmsbwd(a, w, h, gain, dho, name):
    s, kd = a.shape
    tm = _row_block(s)
    tk = kd if kd <= 1024 else 1024
    nk = kd // tk

    def body(a_ref, w_ref, h_ref, g_ref, dho_ref, dh_ref, dhb_ref, dgain_ref, acc_ref):
        i = pl.program_id(0)
        j = pl.program_id(1)

        @pl.when(j == 0)
        def _():
            acc_ref[...] = jnp.zeros_like(acc_ref)

        @pl.when((i == 0) & (j == 0))
        def _():
            dgain_ref[...] = jnp.zeros_like(dgain_ref)

        acc_ref[...] += _dot_nt(a_ref[...], w_ref[...])

        @pl.when(j == nk - 1)
        def _():
            dx, dgain = _rms_bwd(acc_ref[...], h_ref[...], g_ref[...])
            dh = dho_ref[...] + dx
            dh_ref[...] = dh
            dhb_ref[...] = dh.astype(BF16)
            dgain_ref[...] += dgain

    row = pl.BlockSpec((tm, D_MODEL), lambda i, j: (i, 0))
    vec = pl.BlockSpec((1, D_MODEL), lambda i, j: (0, 0))
    return _pc(
        body, name=name, grid=(s // tm, nk),
        in_specs=[pl.BlockSpec((tm, tk), lambda i, j: (i, j)), pl.BlockSpec((D_MODEL, tk), lambda i, j: (0, j)),
                  row, vec, row],
        out_specs=[row, row, vec],
        out_shape=[_sds((s, D_MODEL), F32), _sds((s, D_MODEL), BF16), _sds((1, D_MODEL), F32)],
        scratch_shapes=[pltpu.VMEM((tm, D_MODEL), F32)],
    )(a, w, h, gain, dho)


def _matmul_nt(a, w, name):
    s, kd = a.shape
    n = w.shape[0]
    tm = _row_block(s)

    def body(a_ref, w_ref, o_ref):
        o_ref[...] = _dot_nt(a_ref[...], w_ref[...]).astype(BF16)

    return _pc(
        body, name=name, grid=(s // tm,),
        in_specs=[pl.BlockSpec((tm, kd), lambda i: (i, 0)), pl.BlockSpec((n, kd), lambda i: (0, 0))],
        out_specs=pl.BlockSpec((tm, n), lambda i: (i, 0)),
        out_shape=_sds((s, n), BF16),
    )(a, w)


def _matmul_tn(a, b, name):
    s, m = a.shape
    n = b.shape[1]
    tmm = min(m, 1024)
    tn = n if n <= 1024 else 1024
    tk = _row_block(s)
    nk = s // tk

    def body(a_ref, b_ref, o_ref, acc_ref):
        k = pl.program_id(2)

        @pl.when(k == 0)
        def _():
            acc_ref[...] = jnp.zeros_like(acc_ref)

        acc_ref[...] += _dot_tn(a_ref[...], b_ref[...])

        @pl.when(k == nk - 1)
        def _():
            o_ref[...] = acc_ref[...].astype(BF16)

    return _pc(
        body, name=name, grid=(m // tmm, n // tn, nk),
        in_specs=[pl.BlockSpec((tk, tmm), lambda i, j, k: (k, i)), pl.BlockSpec((tk, tn), lambda i, j, k: (k, j))],
        out_specs=pl.BlockSpec((tmm, tn), lambda i, j, k: (i, j)),
        out_shape=_sds((m, n), BF16),
        scratch_shapes=[pltpu.VMEM((tmm, tn), F32)],
    )(a, b)


def _matmul_tn_heads(a, b, name):
    s, m = a.shape
    nh, _, n = b.shape
    tk = _row_block(s)
    nk = s // tk

    def body(a_ref, b_ref, o_ref, acc_ref):
        k = pl.program_id(1)

        @pl.when(k == 0)
        def _():
            acc_ref[...] = jnp.zeros_like(acc_ref)

        acc_ref[...] += _dot_tn(a_ref[...], b_ref[0])

        @pl.when(k == nk - 1)
        def _():
            o_ref[0] = acc_ref[...].astype(BF16)

    return _pc(
        body, name=name, grid=(nh, nk),
        in_specs=[pl.BlockSpec((tk, m), lambda h, k: (k, 0)), pl.BlockSpec((1, tk, n), lambda h, k: (h, k, 0))],
        out_specs=pl.BlockSpec((1, m, n), lambda h, k: (h, 0, 0)),
        out_shape=_sds((nh, m, n), BF16),
        scratch_shapes=[pltpu.VMEM((m, n), F32)],
    )(a, b)


def _conv_bwd(dbuc, bcx, cw):
    s = bcx.shape[0]
    tc = _CONV_COLS
    nc = D_MODEL // tc

    def body(d_ref, b_ref, c_ref, xp_ref, w_ref, db_ref, dc_ref, dxp_ref, dw_ref):
        rows = lax.broadcasted_iota(jnp.int32, (s, tc), 0)
        c = c_ref[...].astype(F32)
        xp = xp_ref[...].astype(F32)
        u = c * xp
        u1 = _shift_down(u, 1, rows)
        u2 = _shift_down(u, 2, rows)
        w = w_ref[...]
        uc = w[2:3] * u + w[1:2] * u1 + w[0:1] * u2
        d = d_ref[...].astype(F32)
        db_ref[...] = (d * uc).astype(BF16)
        duc = d * b_ref[...].astype(F32)
        du = w[2:3] * duc + w[1:2] * _shift_up(duc, 1, rows, s) + w[0:1] * _shift_up(duc, 2, rows, s)
        dc_ref[...] = (du * xp).astype(BF16)
        dxp_ref[...] = (du * c).astype(BF16)
        dw_ref[0:1, :] = jnp.sum(duc * u2, axis=0, keepdims=True)
        dw_ref[1:2, :] = jnp.sum(duc * u1, axis=0, keepdims=True)
        dw_ref[2:3, :] = jnp.sum(duc * u, axis=0, keepdims=True)

    col = lambda off: pl.BlockSpec((s, tc), lambda j: (0, off + j))
    tap = pl.BlockSpec((3, tc), lambda j: (0, j))
    return _pc(
        body, name="conv_bwd", grid=(nc,),
        in_specs=[col(0), col(0), col(nc), col(2 * nc), tap],
        out_specs=[col(0), col(0), col(0), tap],
        out_shape=[_sds((s, D_MODEL), BF16), _sds((s, D_MODEL), BF16), _sds((s, D_MODEL), BF16),
                   _sds((3, D_MODEL), F32)],
    )(dbuc, bcx, bcx, bcx, cw)


def _attn_bwd(q, k, v, do, o, lse, cc, sn, sp, carry=None):
    s = q.shape[1]
    bk = _row_block(s)
    nb = s // bk

    def body(q_ref, k_ref, v_ref, do_ref, o_ref, lse_ref, cc_ref, sn_ref, sp_ref,
             dq_ref, dk_ref, dv_ref, dqacc_ref, delta_ref):
        j = pl.program_id(1)
        kb = k_ref[0]
        vb = v_ref[0]

        @pl.when(j == 0)
        def _():
            dqacc_ref[...] = jnp.zeros_like(dqacc_ref)

            def fill(i, _):
                rows = pl.ds(pl.multiple_of(i * bk, bk), bk)
                d = jnp.sum(do_ref[rows, :].astype(F32) * o_ref[rows, :].astype(F32), axis=-1, keepdims=True)
                delta_ref[rows, :] = jnp.broadcast_to(d, (bk, 128))
                return 0

            lax.fori_loop(0, nb, fill, 0)

        def step(i, carry, masked):
            dk, dv = carry
            rows = pl.ds(pl.multiple_of(i * bk, bk), bk)
            qb = q_ref[0, rows, :]
            dob = do_ref[rows, :]
            sc = _dot_nt(qb, kb)
            if masked:
                sc = jnp.where(_chunk_mask(bk, bk), sc, NEG)
            p = jnp.exp2(sc - lse_ref[0, rows, :][:, :1])
            dv = dv + _dot_tn(p.astype(BF16), dob)
            ds = (p * (_dot_nt(dob, vb) - delta_ref[rows, :][:, :1])).astype(BF16)
            dk = dk + _dot_tn(ds, qb)
            dqacc_ref[rows, :] += _dot(ds, kb)
            return dk, dv

        carry = step(j, (jnp.zeros((bk, HEAD_PAD), F32), jnp.zeros((bk, V_DIM), F32)), True)
        dq = dqacc_ref[pl.ds(pl.multiple_of(j * bk, bk), bk), :] * ATT_SCALE
        dq_ref[0, :, :NOPE] = dq[:, :NOPE].astype(BF16)
        dq_ref[0, :, NOPE:] = _rope_bwd(dq[:, NOPE:], cc_ref[...], sn_ref[...], sp_ref[...]).astype(BF16)
        dk, dv = lax.fori_loop(j + 1, nb, lambda i, c: step(i, c, False), carry)
        dk_ref[0] = (dk * LN_2).astype(BF16)
        dv_ref[0] = dv.astype(BF16)

    blk = lambda n: pl.BlockSpec((1, bk, n), lambda h, j: (h, j, 0))
    whole = lambda n: pl.BlockSpec((1, s, n), lambda h, j: (h, 0, 0))
    cols = pl.BlockSpec((s, V_DIM), lambda h, j: (0, h))
    tab = pl.BlockSpec((bk, 128), lambda h, j: (j, 0))
    return _pc_carrying(
        body, carry, (q, k, v, do, o, lse, cc, sn, sp), name="attn_bwd", grid=(N_HEADS, nb),
        in_specs=[whole(HEAD_PAD), blk(HEAD_PAD), blk(V_DIM), cols, cols, whole(128), tab, tab, tab],
        out_specs=[blk(HEAD_PAD), blk(HEAD_PAD), blk(V_DIM)],
        out_shape=[_sds((N_HEADS, s, HEAD_PAD), BF16), _sds((N_HEADS, s, HEAD_PAD), BF16),
                   _sds((N_HEADS, s, V_DIM), BF16)],
        scratch_shapes=[pltpu.VMEM((s, HEAD_PAD), F32), pltpu.VMEM((s, 128), F32)],
    )


def _mla_mid_bwd(dq, dk, dv, proj, w_uq, w_ukv, g_cq, g_ckv, cc, sn, sp, carry=None):
    s = dq.shape[1]
    tm = _row_block(s)

    def body(dq_ref, dk_ref, dv_ref, proj_ref, wuq_ref, wukv_ref, gcq_ref, gckv_ref, cc_ref, sn_ref, sp_ref,
             dproj_ref, dkv_ref, dgcq_ref, dgckv_ref, acq_ref, ackv_ref, akr_ref):
        i = pl.program_id(0)
        h = pl.program_id(1)

        @pl.when(h == 0)
        def _():
            acq_ref[...] = jnp.zeros_like(acq_ref)
            ackv_ref[...] = jnp.zeros_like(ackv_ref)
            akr_ref[...] = jnp.zeros_like(akr_ref)

        @pl.when((i == 0) & (h == 0))
        def _():
            dgcq_ref[...] = jnp.zeros_like(dgcq_ref)
            dgckv_ref[...] = jnp.zeros_like(dgckv_ref)

        dkb = dk_ref[0]
        dkv_ref[0, :, :NOPE] = dkb[:, :NOPE]
        dkv_ref[0, :, NOPE:] = dv_ref[0]
        acq_ref[...] += _dot_nt(dq_ref[0], wuq_ref[0])
        ackv_ref[...] += _dot_nt(dkv_ref[0], wukv_ref[0])
        akr_ref[...] += dkb[:, NOPE:].astype(F32)

        @pl.when(h == N_HEADS - 1)
        def _():
            proj = proj_ref[...]
            dcq, dgcq = _rms_bwd(acq_ref[...], proj[:, :CQ], gcq_ref[...])
            dckv, dgckv = _rms_bwd(ackv_ref[...], proj[:, CQ:CQ + CKV], gckv_ref[...])
            dproj_ref[:, :CQ] = dcq.astype(BF16)
            dproj_ref[:, CQ:CQ + CKV] = dckv.astype(BF16)
            dproj_ref[:, CQ + CKV:] = _rope_bwd(akr_ref[...], cc_ref[...], sn_ref[...], sp_ref[...]).astype(BF16)
            dgcq_ref[...] += dgcq
            dgckv_ref[...] += dgckv

    head_blk = lambda n: pl.BlockSpec((1, tm, n), lambda i, h: (h, i, 0))
    head_w = lambda a: pl.BlockSpec((1,) + a.shape[1:], lambda i, h: (h, 0, 0))
    row = lambda n: pl.BlockSpec((tm, n), lambda i, h: (i, 0))
    vec = lambda n: pl.BlockSpec((1, n), lambda i, h: (0, 0))
    return _pc_carrying(
        body, carry, (dq, dk, dv, proj, w_uq, w_ukv, g_cq, g_ckv, cc, sn, sp),
        name="mla_mid_bwd", grid=(s // tm, N_HEADS),
        in_specs=[head_blk(HEAD_PAD), head_blk(HEAD_PAD), head_blk(V_DIM), row(PROJ_PAD), head_w(w_uq), head_w(w_ukv),
                  vec(CQ), vec(CKV), row(128), row(128), row(128)],
        out_specs=[row(PROJ_PAD), head_blk(HEAD_PAD), vec(CQ), vec(CKV)],
        out_shape=[_sds((s, PROJ_PAD), BF16), _sds((N_HEADS, s, HEAD_PAD), BF16), _sds((1, CQ), F32),
                   _sds((1, CKV), F32)],
        scratch_shapes=[pltpu.VMEM((tm, CQ), F32), pltpu.VMEM((tm, CKV), F32), pltpu.VMEM((tm, 128), F32)],
    )


def _peer(k):
    x, y, c = lax.axis_index("x"), lax.axis_index("y"), lax.axis_index("c")
    px = 1 - x if k & 4 else x
    py = 1 - y if k & 2 else y
    pc = 1 - c if k & 1 else c
    return (px, py, pc), 4 * px + 2 * py + pc


def _exchange_ops(arrays, scatter):
    nw = len(arrays)
    direct = range(1, N_DEV) if scatter else (1, 2, 4, 6)

    def copy(w, k, src, dst, to, send_sems, recv_sems):
        return pltpu.make_async_remote_copy(
            src_ref=src, dst_ref=dst, send_sem=send_sems.at[w * N_DEV + k], recv_sem=recv_sems.at[w * N_DEV + k],
            device_id=to, device_id_type=pl.DeviceIdType.MESH)

    def own_copies(ins, outs, local_sems):
        _, me = _peer(0)
        return [pltpu.make_async_copy(ins[w].at[me] if scatter else ins[w], outs[w].at[me], local_sems.at[w])
                for w in range(nw)]

    def sends(ins, outs, send_sems, recv_sems):
        _, me = _peer(0)
        out = []
        for k in direct:
            dev, idx = _peer(k)
            for w in range(nw):
                out.append(copy(w, k, ins[w].at[idx] if scatter else ins[w], outs[w].at[me], dev, send_sems, recv_sems))
        return out

    def relays(outs, send_sems, recv_sems):
        sibling, _ = _peer(1)
        out = []
        for k in (2, 4, 6):
            _, idx = _peer(k)
            for w in range(nw):
                out.append(copy(w, k + 1, outs[w].at[idx], outs[w].at[idx], sibling, send_sems, recv_sems))
        return out

    def arrival(outs, w, k, send_sems, recv_sems):
        dev, idx = _peer(k)
        return copy(w, k, outs[w].at[idx], outs[w].at[idx], dev, send_sems, recv_sems)

    def start(ins, outs, send_sems, recv_sems, local_sems):
        for cp in own_copies(ins, outs, local_sems) + sends(ins, outs, send_sems, recv_sems):
            cp.start()

    def relay(ins, outs, send_sems, recv_sems, local_sems):
        for k in (2, 4, 6):
            for w in range(nw):
                arrival(outs, w, k, send_sems, recv_sems).wait_recv()
        for cp in relays(outs, send_sems, recv_sems):
            cp.start()

    def wait(ins, outs, send_sems, recv_sems, local_sems):
        for cp in own_copies(ins, outs, local_sems):
            cp.wait()
        for cp in sends(ins, outs, send_sems, recv_sems) + ([] if scatter else relays(outs, send_sems, recv_sems)):
            cp.wait_send()
        for k in (range(1, N_DEV) if scatter else (1, 3, 5, 7)):
            for w in range(nw):
                arrival(outs, w, k, send_sems, recv_sems).wait_recv()

    landed = [_sds(a.shape if scatter else (N_DEV,) + a.shape, a.dtype) for a in arrays]
    sems = [pltpu.SemaphoreType.DMA((nw * N_DEV,)), pltpu.SemaphoreType.DMA((nw * N_DEV,)),
            pltpu.SemaphoreType.DMA((nw,))]
    return start, (None if scatter else relay), wait, landed, sems


def _exchange(arrays, scatter, name):
    nw = len(arrays)
    start, relay, wait, landed, sems = _exchange_ops(arrays, scatter)

    def body(*refs):
        ins, outs, csem = refs[:nw], refs[nw:2 * nw], refs[2 * nw:]
        start(ins, outs, *csem)
        if relay is not None:
            relay(ins, outs, *csem)
        wait(ins, outs, *csem)

    any_spec = pl.BlockSpec(memory_space=pl.ANY)
    return _pc(body, name=name, in_specs=[any_spec] * nw, out_specs=[any_spec] * nw, out_shape=landed,
               scratch_shapes=sems)(*arrays)


def _adam_math(g, w, m, v):
    m = ADAM_B1 * m + (1.0 - ADAM_B1) * g
    v = ADAM_B2 * v + (1.0 - ADAM_B2) * jnp.square(g)
    m_hat = m / (1.0 - ADAM_B1 ** ADAM_STEP)
    v_hat = v / (1.0 - ADAM_B2 ** ADAM_STEP)
    delta = -ADAM_LR * (m_hat / (jnp.sqrt(v_hat) + ADAM_EPS) + ADAM_WD * w)
    return delta, m, v


def _adam_rows(r):
    for t in (256, 128, 64, 32, 16):
        if r % t == 0:
            return t
    return r


def _adamw(parts, w, m, v, name):
    r, c = w.shape
    tr = _adam_rows(r)

    def body(p_ref, w_ref, m_ref, v_ref, g_ref, d_ref, mo_ref, vo_ref):
        g = p_ref[0].astype(F32)
        for src in range(1, N_DEV):
            g = g + p_ref[src].astype(F32)
        delta, m2, v2 = _adam_math(g, w_ref[...], m_ref[...], v_ref[...])
        g_ref[...] = g
        d_ref[...] = delta
        mo_ref[...] = m2
        vo_ref[...] = v2

    blk = pl.BlockSpec((tr, c), lambda i: (i, 0))
    return _pc(
        body, name=name, grid=(r // tr,),
        in_specs=[pl.BlockSpec((N_DEV, tr, c), lambda i: (0, i, 0)), blk, blk, blk],
        out_specs=[blk, blk, blk, blk],
        out_shape=[_sds((r, c), F32)] * 4,
    )(parts, w, m, v)


def _small_allreduce_adamw(gpack, wpack, mpack, vpack):
    shape = gpack.shape
    compact = (SMALL_ROWS, D_MODEL)

    def body(g_ref, w_ref, m_ref, v_ref, go_ref, d_ref, mo_ref, vo_ref, comp_ref, gath_ref, send_sems, recv_sems):
        _, me = _peer(0)
        comp_ref[...] = jnp.zeros(compact, F32)
        r = 0
        for p, n in enumerate(SMALL_PIECES):
            comp_ref[r:r + n, :] = g_ref[8 * p:8 * p + n, :]
            r += n
        gath_ref[me] = comp_ref[...]
        copies = []
        for k in range(1, N_DEV):
            dev, idx = _peer(k)
            copies.append(pltpu.make_async_remote_copy(
                src_ref=comp_ref, dst_ref=gath_ref.at[me], send_sem=send_sems.at[k], recv_sem=recv_sems.at[k],
                device_id=dev, device_id_type=pl.DeviceIdType.MESH))
        for cp in copies:
            cp.start()
        for cp in copies:
            cp.wait_send()
        for k in range(1, N_DEV):
            dev, idx = _peer(k)
            pltpu.make_async_remote_copy(
                src_ref=comp_ref, dst_ref=gath_ref.at[idx], send_sem=send_sems.at[k], recv_sem=recv_sems.at[k],
                device_id=dev, device_id_type=pl.DeviceIdType.MESH).wait_recv()
        g = gath_ref[0]
        for src in range(1, N_DEV):
            g = g + gath_ref[src]
        go_ref[...] = jnp.zeros(shape, F32)
        r = 0
        for p, n in enumerate(SMALL_PIECES):
            go_ref[8 * p:8 * p + n, :] = g[r:r + n, :]
            r += n
        delta, m2, v2 = _adam_math(go_ref[...], w_ref[...], m_ref[...], v_ref[...])
        d_ref[...] = delta
        mo_ref[...] = m2
        vo_ref[...] = v2

    vm = pl.BlockSpec(memory_space=pltpu.VMEM)
    return _pc(
        body, name="small_allreduce_adamw",
        in_specs=[vm] * 4, out_specs=[vm] * 4, out_shape=[_sds(shape, F32)] * 4,
        scratch_shapes=[pltpu.VMEM(compact, F32), pltpu.VMEM((N_DEV,) + compact, F32),
                        pltpu.SemaphoreType.DMA((N_DEV,)), pltpu.SemaphoreType.DMA((N_DEV,))],
    )(gpack, wpack, mpack, vpack)


def _cols_from_shards(g, pad_to=None):
    if pad_to is not None and pad_to != g.shape[-1]:
        g = jnp.pad(g, ((0, 0), (0, 0), (0, pad_to - g.shape[-1])))
    return jnp.transpose(g, (1, 0, 2)).reshape(g.shape[1], N_DEV * g.shape[2])


def _shards_from_cols(a, n):
    k = a.shape[0]
    a = a.reshape(k, N_DEV, a.shape[1] // N_DEV)[:, :, :n]
    return jnp.transpose(a, (1, 0, 2))


def _rows_from_shards(g, pad_to=None):
    if pad_to is not None and pad_to != g.shape[1]:
        g = jnp.pad(g, ((0, 0), (0, pad_to - g.shape[1]), (0, 0)))
    return g.reshape(N_DEV * g.shape[1], g.shape[2])


def _shards_from_rows(a, r):
    n = a.shape[1]
    return a.reshape(N_DEV, a.shape[0] // N_DEV, n)[:, :r, :]


def _rope_tables(pos):
    inv_freq = 1.0 / (ROPE_THETA ** (jnp.arange(0, ROPE, 2, dtype=F32) / ROPE))
    ang = pos.astype(F32)[:, None] * inv_freq
    cos, sin = jnp.cos(ang), jnp.sin(ang)
    z32, z64, z96 = (jnp.zeros((pos.shape[0], n), F32) for n in (32, 64, 96))
    return (jnp.concatenate([cos, cos, z64], axis=1), jnp.concatenate([-sin, z96], axis=1),
            jnp.concatenate([z32, sin, z64], axis=1))


def _pad_row(vec):
    vec = vec.reshape(1, -1)
    return jnp.pad(vec, ((0, 0), (0, D_MODEL - vec.shape[1])))


def _forward_backward(x, target, tables, gains, shards):
    cc, sn, sp = tables
    ffn_g = gains["ffn_norm"]

    def gather(names, relay_at):
        return [shards[n] for n in names], False, relay_at

    names = ["w_in", "w_uq", "w_ukv", "w_o", "conv_norm", "conv_w"]
    got = dict(zip(names, _exchange([shards[n] for n in names], False, "gather_mla")))
    w_in = jnp.pad(_rows_from_shards(got["w_in"]), ((0, 0), (0, PROJ_PAD - PROJ)))
    w_uq = jnp.pad(got["w_uq"], ((0, 0), (0, 0), (0, HEAD_PAD - NOPE - ROPE)))
    w_ukv = got["w_ukv"]
    w_o = _rows_from_shards(got["w_o"])
    conv_norm = got["conv_norm"].reshape(1, D_MODEL)
    conv_w = jnp.transpose(got["conv_w"], (1, 0, 2)).reshape(3, D_MODEL)

    n0, proj, cqn, ckvn, krr = _mla_in_fwd(x, gains["mla_norm"], w_in, gains["g_cq"], gains["g_ckv"], cc, sn, sp)
    q, k, v = _qkv_proj(cqn, ckvn, krr, w_uq, w_ukv, cc, sn, sp)
    names = ["wg0", "wu0", "wd0", "c_in", "c_out"]
    (o, lse), got = _attn_fwd(q, k, v, carry=gather(names, 0.65))
    got = dict(zip(names, got))
    wg0, wu0 = (_cols_from_shards(got[n], FF_SHARD_PAD) for n in ("wg0", "wu0"))
    wd0 = _rows_from_shards(got["wd0"], FF_SHARD_PAD)
    c_in = _cols_from_shards(got["c_in"])
    c_out = _rows_from_shards(got["c_out"])
    h1 = _matmul_res(o, w_o, x, "mla_out_fwd")
    names = ["wg1", "wu1", "wd1"]
    (h2, n1, gate0, up0), got = _ffn_fwd(h1, ffn_g[0:1], wg0, wu0, wd0, "ffn0_fwd", carry=gather(names, 0.85))
    got = dict(zip(names, got))
    wg1, wu1 = (_cols_from_shards(got[n], FF_SHARD_PAD) for n in ("wg1", "wu1"))
    wd1 = _rows_from_shards(got["wd1"], FF_SHARD_PAD)
    n2, bcx = _rms_matmul(h2, conv_norm, c_in, "conv_in_fwd")
    bu = _conv_fwd(bcx, conv_w)
    h3 = _matmul_res(bu, c_out, h2, "conv_out_fwd")
    (h4, n3, gate1, up1), _ = _ffn_fwd(h3, ffn_g[1:2], wg1, wu1, wd1, "ffn1_fwd")
    dh4, d_final, loss = _final_loss(h4, gains["final_norm"], target)

    small = {"final_norm": d_final}
    parts = {}

    def scatter(**blocks):
        return list(blocks), (list(blocks.values()), True, None)

    (dh3, dh3_b, dh4_b, dgate, dup, act, d_ffn1), _ = _ffn_bwd_x(
        dh4, h3, ffn_g[1:2], gate1, up1, wg1, wu1, wd1, "ffn1_bwd")
    dwg1 = _matmul_tn(n3, dgate, "ffn1_dwg")
    dwu1 = _matmul_tn(n3, dup, "ffn1_dwu")
    dwd1 = _matmul_tn(act, dh4_b, "ffn1_dwd")
    dbuc = _matmul_nt(dh3_b, c_out, "conv_out_bwd")
    d_c_out = _matmul_tn(bu, dh3_b, "conv_dwout")
    db, dc, dxp, small["conv_w"] = _conv_bwd(dbuc, bcx, conv_w)
    dbcx = jnp.concatenate([db, dc, dxp], axis=1)
    d_c_in = _matmul_tn(n2, dbcx, "conv_dwin")
    dh2, dh2_b, small["conv_norm"] = _nt_rmsbwd(dbcx, c_in, h2, conv_norm, dh3, "conv_in_bwd")
    names, carry = scatter(c_in=_shards_from_cols(d_c_in, 3 * D_MODEL // N_DEV),
                           c_out=_shards_from_rows(d_c_out, D_MODEL // N_DEV))
    (dh1, dh1_b, dh2_b2, dgate, dup, act, d_ffn0), got = _ffn_bwd_x(
        dh2, h1, ffn_g[0:1], gate0, up0, wg0, wu0, wd0, "ffn0_bwd", carry=carry)
    parts.update(zip(names, got))
    dwg0 = _matmul_tn(n1, dgate, "ffn0_dwg")
    dwu0 = _matmul_tn(n1, dup, "ffn0_dwu")
    dwd0 = _matmul_tn(act, dh2_b2, "ffn0_dwd")
    small["ffn_norm"] = jnp.pad(d_ffn0, ((0, 7), (0, 0))) + jnp.pad(d_ffn1, ((1, 6), (0, 0)))
    do = _matmul_nt(dh1_b, w_o, "mla_out_bwd")
    d_w_o = _matmul_tn(o, dh1_b, "mla_dwo")
    names, carry = scatter(
        wg=jnp.stack([_shards_from_cols(g, FF_SHARD) for g in (dwg0, dwg1)], axis=1),
        wu=jnp.stack([_shards_from_cols(g, FF_SHARD) for g in (dwu0, dwu1)], axis=1),
        wd=jnp.stack([_shards_from_rows(g, FF_SHARD) for g in (dwd0, dwd1)], axis=1))
    (dq, dk, dv), got = _attn_bwd(q, k, v, do, o, lse, cc, sn, sp, carry=carry)
    parts.update(zip(names, got))
    names, carry = scatter(w_o=_shards_from_rows(d_w_o, D_MODEL // N_DEV))
    (dproj, dkv, small["g_cq"], small["g_ckv"]), got = _mla_mid_bwd(
        dq, dk, dv, proj, w_uq, w_ukv, gains["g_cq"], gains["g_ckv"], cc, sn, sp, carry=carry)
    parts.update(zip(names, got))
    d_w_uq = _matmul_tn_heads(cqn, dq, "mla_dwuq")
    d_w_ukv = _matmul_tn_heads(ckvn, dkv, "mla_dwukv")
    d_w_in = _matmul_tn(n0, dproj, "mla_dwin")
    dx, _, small["mla_norm"] = _nt_rmsbwd(dproj, w_in, x, gains["mla_norm"], dh1, "mla_in_bwd")
    names = ["w_in", "w_uq", "w_ukv"]
    last = [_shards_from_rows(d_w_in[:, :PROJ], D_MODEL // N_DEV), d_w_uq[:, :, :NOPE + ROPE], d_w_ukv]
    parts.update(zip(names, _exchange(last, True, "scatter_mla")))
    return loss[0, 0], dx, parts, small


def kernel(x, positions, mla_norm, mla_w_in, mla_g_cq, mla_g_ckv, mla_w_uq, mla_w_ukv, mla_w_o, conv_norm, conv_w_in, conv_w, conv_w_out, ffn_norm, ffn_w_gate, ffn_w_up, ffn_w_down, final_norm, loss_target, m_mla_norm, m_mla_w_in, m_mla_g_cq, m_mla_g_ckv, m_mla_w_uq, m_mla_w_ukv, m_mla_w_o, m_conv_norm, m_conv_w_in, m_conv_w, m_conv_w_out, m_ffn_norm, m_ffn_w_gate, m_ffn_w_up, m_ffn_w_down, m_final_norm, v_mla_norm, v_mla_w_in, v_mla_g_cq, v_mla_g_ckv, v_mla_w_uq, v_mla_w_ukv, v_mla_w_o, v_conv_norm, v_conv_w_in, v_conv_w, v_conv_w_out, v_ffn_norm, v_ffn_w_gate, v_ffn_w_up, v_ffn_w_down, v_final_norm):
    me = 4 * lax.axis_index("x") + 2 * lax.axis_index("y") + lax.axis_index("c")

    shards = dict(w_in=mla_w_in[0], w_uq=mla_w_uq[0], w_ukv=mla_w_ukv[0], w_o=mla_w_o[0], c_in=conv_w_in[0],
                  c_out=conv_w_out[0], wg0=ffn_w_gate[0], wg1=ffn_w_gate[1], wu0=ffn_w_up[0], wu1=ffn_w_up[1],
                  wd0=ffn_w_down[0], wd1=ffn_w_down[1])
    shards = {n: a.astype(BF16) for n, a in shards.items()}
    shards.update(conv_norm=conv_norm, conv_w=conv_w[0])
    gains = dict(mla_norm=mla_norm, g_cq=mla_g_cq, g_ckv=mla_g_ckv, ffn_norm=ffn_norm,
                 final_norm=final_norm.reshape(1, -1))
    loss_local, dx, parts, grads = _forward_backward(x[0], loss_target[0], _rope_tables(positions[0]), gains, shards)
    loss = lax.psum(loss_local, ("x", "y", "c"))
    names = ["w_in", "w_uq", "w_ukv", "w_o", "c_in", "c_out", "wg", "wu", "wd"]

    moments = dict(w_in=(m_mla_w_in, v_mla_w_in), w_uq=(m_mla_w_uq, v_mla_w_uq), w_ukv=(m_mla_w_ukv, v_mla_w_ukv),
                   w_o=(m_mla_w_o, v_mla_w_o), c_in=(m_conv_w_in, v_conv_w_in), c_out=(m_conv_w_out, v_conv_w_out),
                   wg=(m_ffn_w_gate, v_ffn_w_gate), wu=(m_ffn_w_up, v_ffn_w_up), wd=(m_ffn_w_down, v_ffn_w_down))
    full = dict(w_in=mla_w_in, w_uq=mla_w_uq, w_ukv=mla_w_ukv, w_o=mla_w_o, c_in=conv_w_in, c_out=conv_w_out,
                wg=ffn_w_gate, wu=ffn_w_up, wd=ffn_w_down)
    res = {}
    for n in names:
        w = full[n]
        shape = w.shape
        c = shape[-1]
        flat = lambda a: a.reshape(-1, c)
        outs = _adamw(parts[n].reshape(N_DEV, -1, c), flat(w), flat(moments[n][0]), flat(moments[n][1]), "adamw_" + n)
        res[n] = [o.reshape(shape) for o in outs]

    col0 = me * (D_MODEL // N_DEV)

    def place(shard):
        return lax.dynamic_update_slice(jnp.zeros((shard.shape[0], D_MODEL), F32), shard, (0, col0))

    def pack(mla_n, g_cq, g_ckv, ffn_n, fin_n, conv_n, conv_taps):
        rows = [mla_n, _pad_row(g_cq), _pad_row(g_ckv), ffn_n, fin_n.reshape(1, -1), conv_n, conv_taps]
        assert all(r.shape[0] in (n, 8) for r, n in zip(rows, SMALL_PIECES))
        return jnp.concatenate([jnp.pad(r, ((0, 8 - r.shape[0]), (0, 0))) for r in rows], axis=0)

    gpack = pack(grads["mla_norm"], grads["g_cq"], grads["g_ckv"], grads["ffn_norm"], grads["final_norm"],
                 grads["conv_norm"], grads["conv_w"])
    wpack = pack(mla_norm, mla_g_cq, mla_g_ckv, ffn_norm, final_norm, place(conv_norm), place(conv_w[0]))
    mpack = pack(m_mla_norm, m_mla_g_cq, m_mla_g_ckv, m_ffn_norm, m_final_norm, place(m_conv_norm), place(m_conv_w[0]))
    vpack = pack(v_mla_norm, v_mla_g_cq, v_mla_g_ckv, v_ffn_norm, v_final_norm, place(v_conv_norm), place(v_conv_w[0]))
    small = _small_allreduce_adamw(gpack, wpack, mpack, vpack)

    def unpack(p):
        own = lambda rows: lax.dynamic_slice(rows, (0, col0), (rows.shape[0], D_MODEL // N_DEV))
        return dict(mla_norm=p[0:1], g_cq=p[8:9, :CQ], g_ckv=p[16:17, :CKV], ffn_norm=p[24:26], final_norm=p[32],
                    conv_norm=own(p[40:41]), conv_w=own(p[48:51])[None])

    small = [unpack(p) for p in small]
    order = ["mla_norm", "w_in", "g_cq", "g_ckv", "w_uq", "w_ukv", "w_o", "conv_norm", "c_in", "conv_w", "c_out",
             "ffn_norm", "wg", "wu", "wd", "final_norm"]
    out = [loss, dx[None]]
    for kind in range(4):
        for n in order:
            out.append(res[n][kind] if n in res else small[kind][n])
    return tuple(out)
```

```python
import math

import jax
import jax.numpy as jnp
from jax import lax
from jax.experimental import pallas as pl
from jax.experimental.pallas import tpu as pltpu

F32 = jnp.float32
BF16 = jnp.bfloat16

N_DEV = 8
D_MODEL = 1024
N_HEADS = 8
NOPE = 128
ROPE = 64
V_DIM = 128
HEAD_PAD = 256
CQ = 512
CKV = 256
PROJ = CQ + CKV + ROPE
PROJ_PAD = CQ + CKV + 128
D_FF = 2816
FF_SHARD = D_FF // N_DEV
FF_SHARD_PAD = 384
FF_PAD = FF_SHARD_PAD * N_DEV
CHUNK_SHIFT = 6
RMS_EPS = 1e-6
ROPE_THETA = 10000.0
ATT_SCALE = 1.0 / math.sqrt(NOPE + ROPE)
LOG2_E = math.log2(math.e)
LN_2 = math.log(2.0)
Q_SCALE = ATT_SCALE * LOG2_E
NEG = -1e30

ADAM_LR = 0.001
ADAM_B1 = 0.9
ADAM_B2 = 0.999
ADAM_EPS = 1e-08
ADAM_WD = 0.01
ADAM_STEP = 10

SMALL_PIECES = (1, 1, 1, 2, 1, 1, 3, 1)
SMALL_ROWS = 16

_NT = (((1,), (1,)), ((), ()))
_TN = (((0,), (0,)), ((), ()))


def _pc(body, *, name, out_shape, grid=(), in_specs=None, out_specs=None, scratch_shapes=(), vmem_mb=None):
    params = {}
    if vmem_mb is not None:
        params["vmem_limit_bytes"] = vmem_mb << 20
    kwargs = dict(
        name=name, out_shape=out_shape, grid=grid, scratch_shapes=scratch_shapes,
        compiler_params=pltpu.CompilerParams(**params),
    )
    if in_specs is not None:
        kwargs["in_specs"] = in_specs
    if out_specs is not None:
        kwargs["out_specs"] = out_specs
    return pl.pallas_call(body, **kwargs)


def _pc_carrying(body, carry, operands, *, name, out_shape, grid, in_specs, out_specs, scratch_shapes=()):
    if carry is None:
        return _pc(body, name=name, out_shape=out_shape, grid=grid, in_specs=in_specs, out_specs=out_specs,
                   scratch_shapes=scratch_shapes)(*operands), None
    arrays, scatter, relay_at = carry
    nw, n_in, n_out, n_scr = len(arrays), len(in_specs), len(out_shape), len(scratch_shapes)
    start, relay, wait, landed_shapes, sems = _exchange_ops(arrays, scatter)
    n_steps = math.prod(grid)
    relay_step = None if relay is None else max(1, min(int(relay_at * n_steps), n_steps - 2))

    def wrapped(*refs):
        ins, rest = refs[:n_in], refs[n_in:]
        cin, rest = rest[:nw], rest[nw:]
        outs, rest = rest[:n_out], rest[n_out:]
        cout, rest = rest[:nw], rest[nw:]
        scr, csem = rest[:n_scr], rest[n_scr:]
        step = pl.program_id(0)
        for a in range(1, len(grid)):
            step = step * grid[a] + pl.program_id(a)

        @pl.when(step == 0)
        def _():
            start(cin, cout, *csem)

        if relay is not None:
            @pl.when(step == relay_step)
            def _():
                relay(cin, cout, *csem)

        body(*ins, *outs, *scr)

        @pl.when(step == n_steps - 1)
        def _():
            wait(cin, cout, *csem)

    any_spec = pl.BlockSpec(memory_space=pl.ANY)
    res = _pc(
        wrapped, name=name, grid=grid,
        in_specs=list(in_specs) + [any_spec] * nw, out_specs=list(out_specs) + [any_spec] * nw,
        out_shape=list(out_shape) + landed_shapes, scratch_shapes=list(scratch_shapes) + sems,
    )(*operands, *arrays)
    return res[:n_out], res[n_out:]


def _sds(shape, dtype):
    return jax.ShapeDtypeStruct(shape, dtype)


def _dot(a, b):
    return jnp.dot(a, b, preferred_element_type=F32)


def _dot_nt(a, b):
    return lax.dot_general(a, b, _NT, preferred_element_type=F32)


def _dot_tn(a, b):
    return lax.dot_general(a, b, _TN, preferred_element_type=F32)


def _rstd(x):
    return lax.rsqrt(jnp.mean(x * x, axis=-1, keepdims=True) + RMS_EPS)


def _rms(x, g):
    return (x * _rstd(x)) * g


def _rms_bwd(dy, x, g):
    r = _rstd(x)
    xhat = x * r
    dxhat = dy * g
    dx = r * (dxhat - xhat * jnp.mean(dxhat * xhat, axis=-1, keepdims=True))
    return dx, jnp.sum(dy * xhat, axis=0, keepdims=True)


def _rope(t, cc, sn, sp):
    return t * cc + pltpu.roll(t, 96, 1) * sn + pltpu.roll(t, 32, 1) * sp


def _rope_bwd(dt, cc, sn, sp):
    return dt * cc + pltpu.roll(dt * sn, 32, 1) + pltpu.roll(dt * sp, 96, 1)


def _row_block(s):
    return min(512, s)


def _mla_in_fwd(x, g0, w_in, g_cq, g_ckv, cc, sn, sp, carry=None):
    s = x.shape[0]
    tm = _row_block(s)

    def body(x_ref, g0_ref, w_ref, gcq_ref, gckv_ref, cc_ref, sn_ref, sp_ref,
             n_ref, proj_ref, cqn_ref, ckvn_ref, krr_ref):
        nb = _rms(x_ref[...], g0_ref[...]).astype(BF16)
        n_ref[...] = nb
        proj = _dot(nb, w_ref[...])
        proj_ref[...] = proj
        cqn_ref[...] = _rms(proj[:, :CQ], gcq_ref[...]).astype(BF16)
        ckvn_ref[...] = _rms(proj[:, CQ:CQ + CKV], gckv_ref[...]).astype(BF16)
        krr_ref[...] = _rope(proj[:, CQ + CKV:], cc_ref[...], sn_ref[...], sp_ref[...]).astype(BF16)

    row = lambda n: pl.BlockSpec((tm, n), lambda i: (i, 0))
    full = lambda a: pl.BlockSpec(a.shape, lambda i: (0, 0))
    return _pc_carrying(
        body, carry, (x, g0, w_in, g_cq, g_ckv, cc, sn, sp), name="mla_in_fwd", grid=(s // tm,),
        in_specs=[row(D_MODEL), full(g0), full(w_in), full(g_cq), full(g_ckv), row(128), row(128), row(128)],
        out_specs=[row(D_MODEL), row(PROJ_PAD), row(CQ), row(CKV), row(128)],
        out_shape=[_sds((s, D_MODEL), BF16), _sds((s, PROJ_PAD), F32), _sds((s, CQ), BF16),
                   _sds((s, CKV), BF16), _sds((s, 128), BF16)],
    )


def _qkv_proj(cqn, ckvn, krr, w_uq, w_ukv, cc, sn, sp, carry=None):
    s = cqn.shape[0]
    tm = _row_block(s)

    def body(cqn_ref, ckvn_ref, krr_ref, wuq_ref, wukv_ref, cc_ref, sn_ref, sp_ref, q_ref, k_ref, v_ref):
        q = _dot(cqn_ref[...], wuq_ref[0]) * Q_SCALE
        q_ref[0, :, :NOPE] = q[:, :NOPE].astype(BF16)
        q_ref[0, :, NOPE:] = _rope(q[:, NOPE:], cc_ref[...], sn_ref[...], sp_ref[...]).astype(BF16)
        kv = _dot(ckvn_ref[...], wukv_ref[0])
        k_ref[0, :, :NOPE] = kv[:, :NOPE].astype(BF16)
        k_ref[0, :, NOPE:] = krr_ref[...]
        v_ref[0] = kv[:, NOPE:].astype(BF16)

    row = lambda n: pl.BlockSpec((tm, n), lambda i, h: (i, 0))
    head_w = lambda a: pl.BlockSpec((1,) + a.shape[1:], lambda i, h: (h, 0, 0))
    head_o = lambda n: pl.BlockSpec((1, tm, n), lambda i, h: (h, i, 0))
    return _pc_carrying(
        body, carry, (cqn, ckvn, krr, w_uq, w_ukv, cc, sn, sp), name="qkv_proj", grid=(s // tm, N_HEADS),
        in_specs=[row(CQ), row(CKV), row(128), head_w(w_uq), head_w(w_ukv), row(128), row(128), row(128)],
        out_specs=[head_o(HEAD_PAD), head_o(HEAD_PAD), head_o(V_DIM)],
        out_shape=[_sds((N_HEADS, s, HEAD_PAD), BF16), _sds((N_HEADS, s, HEAD_PAD), BF16),
                   _sds((N_HEADS, s, V_DIM), BF16)],
    )


def _chunk_bias(bq, bk):
    rows = lax.broadcasted_iota(jnp.int32, (bq, bk), 0)
    cols = lax.broadcasted_iota(jnp.int32, (bq, bk), 1)
    visible = jnp.right_shift(cols, CHUNK_SHIFT) <= jnp.right_shift(rows, CHUNK_SHIFT)
    return jnp.where(visible, 0.0, NEG).astype(F32)


def _attn_fwd(q, k, v, carry=None):
    s = q.shape[1]
    bq = _row_block(s)

    def body(q_ref, k_ref, v_ref, o_ref, lse_ref, bias_ref):
        i = pl.program_id(1)
        qb = q_ref[0]

        @pl.when((pl.program_id(0) == 0) & (i == 0))
        def _():
            bias_ref[...] = _chunk_bias(bq, bq)

        def step(j, carry, masked):
            m, l, acc = carry
            start = pl.multiple_of(j * bq, bq)
            kb = k_ref[0, pl.ds(start, bq), :]
            vb = v_ref[0, pl.ds(start, bq), :]
            sc = _dot_nt(qb, kb)
            if masked:
                sc = sc + bias_ref[...]
            m_new = jnp.maximum(m, jnp.max(sc, axis=-1, keepdims=True))
            p = jnp.exp2(sc - m_new)
            alpha = jnp.exp2(m - m_new)
            l = alpha * l + jnp.sum(p, axis=-1, keepdims=True)
            acc = alpha * acc + _dot(p.astype(BF16), vb)
            return m_new, l, acc

        init = (jnp.full((bq, 1), NEG, F32), jnp.zeros((bq, 1), F32), jnp.zeros((bq, V_DIM), F32))
        carry = lax.fori_loop(0, i, lambda j, c: step(j, c, False), init)
        m, l, acc = step(i, carry, True)
        o_ref[...] = (acc / l).astype(BF16)
        lse_ref[0] = jnp.broadcast_to(m + jnp.log(l) * LOG2_E, (bq, 128))

    return _pc_carrying(
        body, carry, (q, k, v), name="attn_fwd", grid=(N_HEADS, s // bq),
        in_specs=[pl.BlockSpec((1, bq, HEAD_PAD), lambda h, i: (h, i, 0)),
                  pl.BlockSpec((1, s, HEAD_PAD), lambda h, i: (h, 0, 0)),
                  pl.BlockSpec((1, s, V_DIM), lambda h, i: (h, 0, 0))],
        out_specs=[pl.BlockSpec((bq, V_DIM), lambda h, i: (i, h)),
                   pl.BlockSpec((1, bq, 128), lambda h, i: (h, i, 0))],
        out_shape=[_sds((s, N_HEADS * V_DIM), BF16), _sds((N_HEADS, s, 128), F32)],
        scratch_shapes=[pltpu.VMEM((bq, bq), F32)],
    )


def _matmul_res(a, w, res, name):
    s, kd = a.shape
    n = w.shape[1]
    tm = _row_block(s)

    def body(a_ref, w_ref, r_ref, o_ref):
        o_ref[...] = r_ref[...] + _dot(a_ref[...], w_ref[...])

    return _pc(
        body, name=name, grid=(s // tm,),
        in_specs=[pl.BlockSpec((tm, kd), lambda i: (i, 0)), pl.BlockSpec((kd, n), lambda i: (0, 0)),
                  pl.BlockSpec((tm, n), lambda i: (i, 0))],
        out_specs=pl.BlockSpec((tm, n), lambda i: (i, 0)),
        out_shape=_sds((s, n), F32),
    )(a, w, res)


def _ffn_fwd(h, gain, wg, wu, wd, name, carry=None):
    s = h.shape[0]
    tm = _row_block(s)
    tf = 512
    nf = FF_PAD // tf

    def body(h_ref, g_ref, wg_ref, wu_ref, wd_ref, o_ref, n_ref, gate_ref, up_ref, acc_ref):
        j = pl.program_id(1)

        @pl.when(j == 0)
        def _():
            n_ref[...] = _rms(h_ref[...], g_ref[...]).astype(BF16)
            acc_ref[...] = jnp.zeros_like(acc_ref)

        nb = n_ref[...]
        gate = _dot(nb, wg_ref[...])
        up = _dot(nb, wu_ref[...])
        gate_ref[...] = gate.astype(BF16)
        up_ref[...] = up.astype(BF16)
        act = gate * jax.nn.sigmoid(gate) * up
        acc_ref[...] += _dot(act.astype(BF16), wd_ref[...])

        @pl.when(j == nf - 1)
        def _():
            o_ref[...] = h_ref[...] + acc_ref[...]

    return _pc_carrying(
        body, carry, (h, gain, wg, wu, wd), name=name, grid=(s // tm, nf),
        in_specs=[pl.BlockSpec((tm, D_MODEL), lambda i, j: (i, 0)), pl.BlockSpec((1, D_MODEL), lambda i, j: (0, 0)),
                  pl.BlockSpec((D_MODEL, tf), lambda i, j: (0, j)), pl.BlockSpec((D_MODEL, tf), lambda i, j: (0, j)),
                  pl.BlockSpec((tf, D_MODEL), lambda i, j: (j, 0))],
        out_specs=[pl.BlockSpec((tm, D_MODEL), lambda i, j: (i, 0)), pl.BlockSpec((tm, D_MODEL), lambda i, j: (i, 0)),
                   pl.BlockSpec((tm, tf), lambda i, j: (i, j)), pl.BlockSpec((tm, tf), lambda i, j: (i, j))],
        out_shape=[_sds((s, D_MODEL), F32), _sds((s, D_MODEL), BF16), _sds((s, FF_PAD), BF16),
                   _sds((s, FF_PAD), BF16)],
        scratch_shapes=[pltpu.VMEM((tm, D_MODEL), F32)],
    )


def _rms_matmul(h, gain, w, name):
    s = h.shape[0]
    n = w.shape[1]
    tm = _row_block(s)
    tn = 1024
    nn = n // tn

    def body(h_ref, g_ref, w_ref, n_ref, o_ref):
        @pl.when(pl.program_id(1) == 0)
        def _():
            n_ref[...] = _rms(h_ref[...], g_ref[...]).astype(BF16)

        o_ref[...] = _dot(n_ref[...], w_ref[...]).astype(BF16)

    return _pc(
        body, name=name, grid=(s // tm, nn),
        in_specs=[pl.BlockSpec((tm, D_MODEL), lambda i, j: (i, 0)), pl.BlockSpec((1, D_MODEL), lambda i, j: (0, 0)),
                  pl.BlockSpec((D_MODEL, tn), lambda i, j: (0, j))],
        out_specs=[pl.BlockSpec((tm, D_MODEL), lambda i, j: (i, 0)), pl.BlockSpec((tm, tn), lambda i, j: (i, j))],
        out_shape=[_sds((s, D_MODEL), BF16), _sds((s, n), BF16)],
    )(h, gain, w)


def _shift_down(u, k, rows):
    return jnp.where(rows >= k, pltpu.roll(u, k, 0), 0.0)


def _shift_up(u, k, rows, s):
    return jnp.where(rows < s - k, pltpu.roll(u, s - k, 0), 0.0)


_CONV_COLS = 128


def _conv_fwd(bcx, cw):
    s = bcx.shape[0]
    tc = _CONV_COLS
    nc = D_MODEL // tc

    def body(b_ref, c_ref, xp_ref, w_ref, o_ref):
        rows = lax.broadcasted_iota(jnp.int32, (s, tc), 0)
        u = c_ref[...].astype(F32) * xp_ref[...].astype(F32)
        w = w_ref[...]
        uc = w[2:3] * u + w[1:2] * _shift_down(u, 1, rows) + w[0:1] * _shift_down(u, 2, rows)
        o_ref[...] = (b_ref[...].astype(F32) * uc).astype(BF16)

    col = lambda off: pl.BlockSpec((s, tc), lambda j: (0, off + j))
    return _pc(
        body, name="conv_fwd", grid=(nc,),
        in_specs=[col(0), col(nc), col(2 * nc), pl.BlockSpec((3, tc), lambda j: (0, j))],
        out_specs=pl.BlockSpec((s, tc), lambda j: (0, j)),
        out_shape=_sds((s, D_MODEL), BF16),
    )(bcx, bcx, bcx, cw)


def _final_loss(h, gain, target):
    s = h.shape[0]
    tm = _row_block(s)

    def body(h_ref, g_ref, t_ref, dh_ref, dg_ref, loss_ref):
        i = pl.program_id(0)

        @pl.when(i == 0)
        def _():
            dg_ref[...] = jnp.zeros_like(dg_ref)
            loss_ref[...] = jnp.zeros_like(loss_ref)

        hb = h_ref[...]
        e = _rms(hb, g_ref[...]) - t_ref[...]
        loss_ref[...] += 0.5 * jnp.sum(jnp.mean(e * e, axis=-1, keepdims=True))
        dx, dg = _rms_bwd(e * (1.0 / D_MODEL), hb, g_ref[...])
        dh_ref[...] = dx
        dg_ref[...] += dg

    row = pl.BlockSpec((tm, D_MODEL), lambda i: (i, 0))
    vec = pl.BlockSpec((1, D_MODEL), lambda i: (0, 0))
    return _pc(
        body, name="final_loss", grid=(s // tm,),
        in_specs=[row, vec, row],
        out_specs=[row, vec, pl.BlockSpec((8, D_MODEL), lambda i: (0, 0))],
        out_shape=[_sds((s, D_MODEL), F32), _sds((1, D_MODEL), F32), _sds((8, D_MODEL), F32)],
    )(h, gain, target)


def _ffn_bwd_x(dho, h, gain, gate, up, wg, wu, wd, name, carry=None):
    s = h.shape[0]
    tm = _row_block(s)
    tf = 512
    nf = FF_PAD // tf

    def body(dho_ref, h_ref, g_ref, gate_ref, up_ref, wg_ref, wu_ref, wd_ref,
             dh_ref, dhb_ref, dhob_ref, dgate_ref, dup_ref, act_ref, dgain_ref, acc_ref):
        i = pl.program_id(0)
        j = pl.program_id(1)

        @pl.when(j == 0)
        def _():
            dhob_ref[...] = dho_ref[...].astype(BF16)
            acc_ref[...] = jnp.zeros_like(acc_ref)

        @pl.when((i == 0) & (j == 0))
        def _():
            dgain_ref[...] = jnp.zeros_like(dgain_ref)

        dact = _dot_nt(dhob_ref[...], wd_ref[...])
        g = gate_ref[...].astype(F32)
        u = up_ref[...].astype(F32)
        sg = jax.nn.sigmoid(g)
        silu = g * sg
        dg = (dact * u * (sg * (1.0 + g * (1.0 - sg)))).astype(BF16)
        du = (dact * silu).astype(BF16)
        dgate_ref[...] = dg
        dup_ref[...] = du
        act_ref[...] = (silu * u).astype(BF16)
        acc_ref[...] += _dot_nt(dg, wg_ref[...]) + _dot_nt(du, wu_ref[...])

        @pl.when(j == nf - 1)
        def _():
            dx, dgain = _rms_bwd(acc_ref[...], h_ref[...], g_ref[...])
            dh = dho_ref[...] + dx
            dh_ref[...] = dh
            dhb_ref[...] = dh.astype(BF16)
            dgain_ref[...] += dgain

    row = pl.BlockSpec((tm, D_MODEL), lambda i, j: (i, 0))
    vec = pl.BlockSpec((1, D_MODEL), lambda i, j: (0, 0))
    hid = pl.BlockSpec((tm, tf), lambda i, j: (i, j))
    wcol = pl.BlockSpec((D_MODEL, tf), lambda i, j: (0, j))
    wrow = pl.BlockSpec((tf, D_MODEL), lambda i, j: (j, 0))
    return _pc_carrying(
        body, carry, (dho, h, gain, gate, up, wg, wu, wd), name=name, grid=(s // tm, nf),
        in_specs=[row, row, vec, hid, hid, wcol, wcol, wrow],
        out_specs=[row, row, row, hid, hid, hid, vec],
        out_shape=[_sds((s, D_MODEL), F32), _sds((s, D_MODEL), BF16), _sds((s, D_MODEL), BF16),
                   _sds((s, FF_PAD), BF16), _sds((s, FF_PAD), BF16), _sds((s, FF_PAD), BF16),
                   _sds((1, D_MODEL), F32)],
        scratch_shapes=[pltpu.VMEM((tm, D_MODEL), F32)],
    )


def _nt_rmsbwd(a, w, h, gain, dho, name, carry=None):
    s, kd = a.shape
    tm = _row_block(s)
    tk = kd if kd <= 1024 else 1024
    nk = kd // tk

    def body(a_ref, w_ref, h_ref, g_ref, dho_ref, dh_ref, dhb_ref, dgain_ref, acc_ref):
        i = pl.program_id(0)
        j = pl.program_id(1)

        @pl.when(j == 0)
        def _():
            acc_ref[...] = jnp.zeros_like(acc_ref)

        @pl.when((i == 0) & (j == 0))
        def _():
            dgain_ref[...] = jnp.zeros_like(dgain_ref)

        acc_ref[...] += _dot_nt(a_ref[...], w_ref[...])

        @pl.when(j == nk - 1)
        def _():
            dx, dgain = _rms_bwd(acc_ref[...], h_ref[...], g_ref[...])
            dh = dho_ref[...] + dx
            dh_ref[...] = dh
            dhb_ref[...] = dh.astype(BF16)
            dgain_ref[...] += dgain

    row = pl.BlockSpec((tm, D_MODEL), lambda i, j: (i, 0))
    vec = pl.BlockSpec((1, D_MODEL), lambda i, j: (0, 0))
    return _pc_carrying(
        body, carry, (a, w, h, gain, dho), name=name, grid=(s // tm, nk),
        in_specs=[pl.BlockSpec((tm, tk), lambda i, j: (i, j)), pl.BlockSpec((D_MODEL, tk), lambda i, j: (0, j)),
                  row, vec, row],
        out_specs=[row, row, vec],
        out_shape=[_sds((s, D_MODEL), F32), _sds((s, D_MODEL), BF16), _sds((1, D_MODEL), F32)],
        scratch_shapes=[pltpu.VMEM((tm, D_MODEL), F32)],
    )


def _matmul_nt(a, w, name):
    s, kd = a.shape
    n = w.shape[0]
    tm = _row_block(s)

    def body(a_ref, w_ref, o_ref):
        o_ref[...] = _dot_nt(a_ref[...], w_ref[...]).astype(BF16)

    return _pc(
        body, name=name, grid=(s // tm,),
        in_specs=[pl.BlockSpec((tm, kd), lambda i: (i, 0)), pl.BlockSpec((n, kd), lambda i: (0, 0))],
        out_specs=pl.BlockSpec((tm, n), lambda i: (i, 0)),
        out_shape=_sds((s, n), BF16),
    )(a, w)


def _matmul_tn(a, b, name):
    s, m = a.shape
    n = b.shape[1]
    tmm = min(m, 1024)
    tn = n if n <= 1024 else 1024
    tk = _row_block(s)
    nk = s // tk

    def body(a_ref, b_ref, o_ref, acc_ref):
        k = pl.program_id(2)

        @pl.when(k == 0)
        def _():
            acc_ref[...] = jnp.zeros_like(acc_ref)

        acc_ref[...] += _dot_tn(a_ref[...], b_ref[...])

        @pl.when(k == nk - 1)
        def _():
            o_ref[...] = acc_ref[...].astype(BF16)

    return _pc(
        body, name=name, grid=(m // tmm, n // tn, nk),
        in_specs=[pl.BlockSpec((tk, tmm), lambda i, j, k: (k, i)), pl.BlockSpec((tk, tn), lambda i, j, k: (k, j))],
        out_specs=pl.BlockSpec((tmm, tn), lambda i, j, k: (i, j)),
        out_shape=_sds((m, n), BF16),
        scratch_shapes=[pltpu.VMEM((tmm, tn), F32)],
    )(a, b)


def _conv_bwd(dbuc, bcx, cw):
    s = bcx.shape[0]
    tc = _CONV_COLS
    nc = D_MODEL // tc

    def body(d_ref, b_ref, c_ref, xp_ref, w_ref, db_ref, dc_ref, dxp_ref, dw_ref):
        rows = lax.broadcasted_iota(jnp.int32, (s, tc), 0)
        c = c_ref[...].astype(F32)
        xp = xp_ref[...].astype(F32)
        u = c * xp
        u1 = _shift_down(u, 1, rows)
        u2 = _shift_down(u, 2, rows)
        w = w_ref[...]
        uc = w[2:3] * u + w[1:2] * u1 + w[0:1] * u2
        d = d_ref[...].astype(F32)
        db_ref[...] = (d * uc).astype(BF16)
        duc = d * b_ref[...].astype(F32)
        du = w[2:3] * duc + w[1:2] * _shift_up(duc, 1, rows, s) + w[0:1] * _shift_up(duc, 2, rows, s)
        dc_ref[...] = (du * xp).astype(BF16)
        dxp_ref[...] = (du * c).astype(BF16)
        dw_ref[0:1, :] = jnp.sum(duc * u2, axis=0, keepdims=True)
        dw_ref[1:2, :] = jnp.sum(duc * u1, axis=0, keepdims=True)
        dw_ref[2:3, :] = jnp.sum(duc * u, axis=0, keepdims=True)

    col = lambda off: pl.BlockSpec((s, tc), lambda j: (0, off + j))
    tap = pl.BlockSpec((3, tc), lambda j: (0, j))
    return _pc(
        body, name="conv_bwd", grid=(nc,),
        in_specs=[col(0), col(0), col(nc), col(2 * nc), tap],
        out_specs=[col(0), col(0), col(0), tap],
        out_shape=[_sds((s, D_MODEL), BF16), _sds((s, D_MODEL), BF16), _sds((s, D_MODEL), BF16),
                   _sds((3, D_MODEL), F32)],
    )(dbuc, bcx, bcx, bcx, cw)


def _attn_bwd(q, k, v, do, o, lse, cc, sn, sp, carry=None):
    s = q.shape[1]
    bk = _row_block(s)
    nb = s // bk

    def body(q_ref, k_ref, v_ref, do_ref, o_ref, lse_ref, cc_ref, sn_ref, sp_ref,
             dq_ref, dk_ref, dv_ref, dqacc_ref, delta_ref, bias_ref):
        j = pl.program_id(1)
        kb = k_ref[0]
        vb = v_ref[0]

        @pl.when((pl.program_id(0) == 0) & (j == 0))
        def _():
            bias_ref[...] = _chunk_bias(bk, bk)

        @pl.when(j == 0)
        def _():
            dqacc_ref[...] = jnp.zeros_like(dqacc_ref)

            def fill(i, _):
                rows = pl.ds(pl.multiple_of(i * bk, bk), bk)
                d = jnp.sum(do_ref[rows, :].astype(F32) * o_ref[rows, :].astype(F32), axis=-1, keepdims=True)
                delta_ref[rows, :] = jnp.broadcast_to(d, (bk, 128))
                return 0

            lax.fori_loop(0, nb, fill, 0)

        def step(i, carry, masked):
            dk, dv = carry
            rows = pl.ds(pl.multiple_of(i * bk, bk), bk)
            qb = q_ref[0, rows, :]
            dob = do_ref[rows, :]
            sc = _dot_nt(qb, kb)
            if masked:
                sc = sc + bias_ref[...]
            p = jnp.exp2(sc - lse_ref[0, rows, :][:, :1])
            dv = dv + _dot_tn(p.astype(BF16), dob)
            ds = (p * (_dot_nt(dob, vb) - delta_ref[rows, :][:, :1])).astype(BF16)
            dk = dk + _dot_tn(ds, qb)
            dqacc_ref[rows, :] += _dot(ds, kb)
            return dk, dv

        carry = step(j, (jnp.zeros((bk, HEAD_PAD), F32), jnp.zeros((bk, V_DIM), F32)), True)
        dq = dqacc_ref[pl.ds(pl.multiple_of(j * bk, bk), bk), :] * ATT_SCALE
        dq_ref[:, :NOPE] = dq[:, :NOPE].astype(BF16)
        dq_ref[:, NOPE:] = _rope_bwd(dq[:, NOPE:], cc_ref[...], sn_ref[...], sp_ref[...]).astype(BF16)
        dk, dv = lax.fori_loop(j + 1, nb, lambda i, c: step(i, c, False), carry)
        dk_ref[0] = (dk * LN_2).astype(BF16)
        dv_ref[0] = dv.astype(BF16)

    blk = lambda n: pl.BlockSpec((1, bk, n), lambda h, j: (h, j, 0))
    whole = lambda n: pl.BlockSpec((1, s, n), lambda h, j: (h, 0, 0))
    cols = pl.BlockSpec((s, V_DIM), lambda h, j: (0, h))
    tab = pl.BlockSpec((bk, 128), lambda h, j: (j, 0))
    return _pc_carrying(
        body, carry, (q, k, v, do, o, lse, cc, sn, sp), name="attn_bwd", grid=(N_HEADS, nb),
        in_specs=[whole(HEAD_PAD), blk(HEAD_PAD), blk(V_DIM), cols, cols, whole(128), tab, tab, tab],
        out_specs=[pl.BlockSpec((bk, HEAD_PAD), lambda h, j: (j, h)), blk(HEAD_PAD), blk(V_DIM)],
        out_shape=[_sds((s, N_HEADS * HEAD_PAD), BF16), _sds((N_HEADS, s, HEAD_PAD), BF16),
                   _sds((N_HEADS, s, V_DIM), BF16)],
        scratch_shapes=[pltpu.VMEM((s, HEAD_PAD), F32), pltpu.VMEM((s, 128), F32), pltpu.VMEM((bk, bk), F32)],
    )


def _mla_mid_bwd(dq, dk, dv, proj, w_uq, w_ukv, g_cq, g_ckv, cc, sn, sp, carry=None):
    s = dq.shape[0]
    tm = _row_block(s)

    def body(dq_ref, dk_ref, dv_ref, proj_ref, wuq_ref, wukv_ref, gcq_ref, gckv_ref, cc_ref, sn_ref, sp_ref,
             dproj_ref, dkv_ref, dgcq_ref, dgckv_ref, acq_ref, ackv_ref, akr_ref):
        i = pl.program_id(0)
        h = pl.program_id(1)

        @pl.when(h == 0)
        def _():
            acq_ref[...] = jnp.zeros_like(acq_ref)
            ackv_ref[...] = jnp.zeros_like(ackv_ref)
            akr_ref[...] = jnp.zeros_like(akr_ref)

        @pl.when((i == 0) & (h == 0))
        def _():
            dgcq_ref[...] = jnp.zeros_like(dgcq_ref)
            dgckv_ref[...] = jnp.zeros_like(dgckv_ref)

        dkb = dk_ref[0]
        dkv_ref[:, :NOPE] = dkb[:, :NOPE]
        dkv_ref[:, NOPE:] = dv_ref[0]
        acq_ref[...] += _dot_nt(dq_ref[...], wuq_ref[0])
        ackv_ref[...] += _dot_nt(dkv_ref[...], wukv_ref[0])
        akr_ref[...] += dkb[:, NOPE:].astype(F32)

        @pl.when(h == N_HEADS - 1)
        def _():
            proj = proj_ref[...]
            dcq, dgcq = _rms_bwd(acq_ref[...], proj[:, :CQ], gcq_ref[...])
            dckv, dgckv = _rms_bwd(ackv_ref[...], proj[:, CQ:CQ + CKV], gckv_ref[...])
            dproj_ref[:, :CQ] = dcq.astype(BF16)
            dproj_ref[:, CQ:CQ + CKV] = dckv.astype(BF16)
            dproj_ref[:, CQ + CKV:] = _rope_bwd(akr_ref[...], cc_ref[...], sn_ref[...], sp_ref[...]).astype(BF16)
            dgcq_ref[...] += dgcq
            dgckv_ref[...] += dgckv

    head_blk = lambda n: pl.BlockSpec((1, tm, n), lambda i, h: (h, i, 0))
    head_cols = pl.BlockSpec((tm, HEAD_PAD), lambda i, h: (i, h))
    head_w = lambda a: pl.BlockSpec((1,) + a.shape[1:], lambda i, h: (h, 0, 0))
    row = lambda n: pl.BlockSpec((tm, n), lambda i, h: (i, 0))
    vec = lambda n: pl.BlockSpec((1, n), lambda i, h: (0, 0))
    return _pc_carrying(
        body, carry, (dq, dk, dv, proj, w_uq, w_ukv, g_cq, g_ckv, cc, sn, sp),
        name="mla_mid_bwd", grid=(s // tm, N_HEADS),
        in_specs=[head_cols, head_blk(HEAD_PAD), head_blk(V_DIM), row(PROJ_PAD), head_w(w_uq), head_w(w_ukv),
                  vec(CQ), vec(CKV), row(128), row(128), row(128)],
        out_specs=[row(PROJ_PAD), head_cols, vec(CQ), vec(CKV)],
        out_shape=[_sds((s, PROJ_PAD), BF16), _sds((s, N_HEADS * HEAD_PAD), BF16), _sds((1, CQ), F32),
                   _sds((1, CKV), F32)],
        scratch_shapes=[pltpu.VMEM((tm, CQ), F32), pltpu.VMEM((tm, CKV), F32), pltpu.VMEM((tm, 128), F32)],
    )


def _peer(k):
    x, y, c = lax.axis_index("x"), lax.axis_index("y"), lax.axis_index("c")
    px = 1 - x if k & 4 else x
    py = 1 - y if k & 2 else y
    pc = 1 - c if k & 1 else c
    return (px, py, pc), 4 * px + 2 * py + pc


def _exchange_ops(arrays, scatter):
    nw = len(arrays)
    direct = range(1, N_DEV) if scatter else (1, 2, 4, 6)

    def copy(w, k, src, dst, to, send_sems, recv_sems):
        return pltpu.make_async_remote_copy(
            src_ref=src, dst_ref=dst, send_sem=send_sems.at[w * N_DEV + k], recv_sem=recv_sems.at[w * N_DEV + k],
            device_id=to, device_id_type=pl.DeviceIdType.MESH)

    def own_copies(ins, outs, local_sems):
        _, me = _peer(0)
        return [pltpu.make_async_copy(ins[w].at[me] if scatter else ins[w], outs[w].at[me], local_sems.at[w])
                for w in range(nw)]

    def sends(ins, outs, send_sems, recv_sems):
        _, me = _peer(0)
        out = []
        for k in direct:
            dev, idx = _peer(k)
            for w in range(nw):
                out.append(copy(w, k, ins[w].at[idx] if scatter else ins[w], outs[w].at[me], dev, send_sems, recv_sems))
        return out

    def relays(outs, send_sems, recv_sems):
        sibling, _ = _peer(1)
        out = []
        for k in (2, 4, 6):
            _, idx = _peer(k)
            for w in range(nw):
                out.append(copy(w, k + 1, outs[w].at[idx], outs[w].at[idx], sibling, send_sems, recv_sems))
        return out

    def arrival(outs, w, k, send_sems, recv_sems):
        dev, idx = _peer(k)
        return copy(w, k, outs[w].at[idx], outs[w].at[idx], dev, send_sems, recv_sems)

    def start(ins, outs, send_sems, recv_sems, local_sems):
        for cp in own_copies(ins, outs, local_sems) + sends(ins, outs, send_sems, recv_sems):
            cp.start()

    def relay(ins, outs, send_sems, recv_sems, local_sems):
        for k in (2, 4, 6):
            for w in range(nw):
                arrival(outs, w, k, send_sems, recv_sems).wait_recv()
        for cp in relays(outs, send_sems, recv_sems):
            cp.start()

    def wait(ins, outs, send_sems, recv_sems, local_sems):
        for cp in own_copies(ins, outs, local_sems):
            cp.wait()
        for cp in sends(ins, outs, send_sems, recv_sems) + ([] if scatter else relays(outs, send_sems, recv_sems)):
            cp.wait_send()
        for k in (range(1, N_DEV) if scatter else (1, 3, 5, 7)):
            for w in range(nw):
                arrival(outs, w, k, send_sems, recv_sems).wait_recv()

    landed = [_sds(a.shape if scatter else (N_DEV,) + a.shape, a.dtype) for a in arrays]
    sems = [pltpu.SemaphoreType.DMA((nw * N_DEV,)), pltpu.SemaphoreType.DMA((nw * N_DEV,)),
            pltpu.SemaphoreType.DMA((nw,))]
    return start, (None if scatter else relay), wait, landed, sems


def _exchange(arrays, scatter, name):
    nw = len(arrays)
    start, relay, wait, landed, sems = _exchange_ops(arrays, scatter)

    def body(*refs):
        ins, outs, csem = refs[:nw], refs[nw:2 * nw], refs[2 * nw:]
        start(ins, outs, *csem)
        if relay is not None:
            relay(ins, outs, *csem)
        wait(ins, outs, *csem)

    any_spec = pl.BlockSpec(memory_space=pl.ANY)
    return _pc(body, name=name, in_specs=[any_spec] * nw, out_specs=[any_spec] * nw, out_shape=landed,
               scratch_shapes=sems)(*arrays)


def _adam_math(g, w, m, v):
    m = ADAM_B1 * m + (1.0 - ADAM_B1) * g
    v = ADAM_B2 * v + (1.0 - ADAM_B2) * jnp.square(g)
    m_hat = m / (1.0 - ADAM_B1 ** ADAM_STEP)
    v_hat = v / (1.0 - ADAM_B2 ** ADAM_STEP)
    delta = -ADAM_LR * (m_hat / (jnp.sqrt(v_hat) + ADAM_EPS) + ADAM_WD * w)
    return delta, m, v


def _adam_rows(r):
    for t in (256, 128, 64, 32, 16):
        if r % t == 0:
            return t
    return r


def _adamw(parts, w, m, v, name):
    r, c = w.shape
    tr = _adam_rows(r)

    def body(p_ref, w_ref, m_ref, v_ref, g_ref, d_ref, mo_ref, vo_ref):
        g = p_ref[0].astype(F32)
        for src in range(1, N_DEV):
            g = g + p_ref[src].astype(F32)
        delta, m2, v2 = _adam_math(g, w_ref[...], m_ref[...], v_ref[...])
        g_ref[...] = g
        d_ref[...] = delta
        mo_ref[...] = m2
        vo_ref[...] = v2

    blk = pl.BlockSpec((tr, c), lambda i: (i, 0))
    return _pc(
        body, name=name, grid=(r // tr,),
        in_specs=[pl.BlockSpec((N_DEV, tr, c), lambda i: (0, i, 0)), blk, blk, blk],
        out_specs=[blk, blk, blk, blk],
        out_shape=[_sds((r, c), F32)] * 4,
    )(parts, w, m, v)


def _small_allreduce_adamw(gpack, wpack, mpack, vpack):
    shape = gpack.shape
    compact = (SMALL_ROWS, D_MODEL)

    def body(g_ref, w_ref, m_ref, v_ref, go_ref, d_ref, mo_ref, vo_ref, comp_ref, gath_ref, send_sems, recv_sems):
        _, me = _peer(0)
        comp_ref[...] = jnp.zeros(compact, F32)
        r = 0
        for p, n in enumerate(SMALL_PIECES):
            comp_ref[r:r + n, :] = g_ref[8 * p:8 * p + n, :]
            r += n
        gath_ref[me] = comp_ref[...]
        copies = []
        for k in range(1, N_DEV):
            dev, idx = _peer(k)
            copies.append(pltpu.make_async_remote_copy(
                src_ref=comp_ref, dst_ref=gath_ref.at[me], send_sem=send_sems.at[k], recv_sem=recv_sems.at[k],
                device_id=dev, device_id_type=pl.DeviceIdType.MESH))
        for cp in copies:
            cp.start()
        for cp in copies:
            cp.wait_send()
        for k in range(1, N_DEV):
            dev, idx = _peer(k)
            pltpu.make_async_remote_copy(
                src_ref=comp_ref, dst_ref=gath_ref.at[idx], send_sem=send_sems.at[k], recv_sem=recv_sems.at[k],
                device_id=dev, device_id_type=pl.DeviceIdType.MESH).wait_recv()
        g = gath_ref[0]
        for src in range(1, N_DEV):
            g = g + gath_ref[src]
        go_ref[...] = jnp.zeros(shape, F32)
        r = 0
        for p, n in enumerate(SMALL_PIECES):
            go_ref[8 * p:8 * p + n, :] = g[r:r + n, :]
            r += n
        delta, m2, v2 = _adam_math(go_ref[...], w_ref[...], m_ref[...], v_ref[...])
        d_ref[...] = delta
        mo_ref[...] = m2
        vo_ref[...] = v2

    vm = pl.BlockSpec(memory_space=pltpu.VMEM)
    return _pc(
        body, name="small_allreduce_adamw",
        in_specs=[vm] * 4, out_specs=[vm] * 4, out_shape=[_sds(shape, F32)] * 4,
        scratch_shapes=[pltpu.VMEM(compact, F32), pltpu.VMEM((N_DEV,) + compact, F32),
                        pltpu.SemaphoreType.DMA((N_DEV,)), pltpu.SemaphoreType.DMA((N_DEV,))],
    )(gpack, wpack, mpack, vpack)


def _cols_from_shards(g, pad_to=None):
    if pad_to is not None and pad_to != g.shape[-1]:
        g = jnp.pad(g, ((0, 0), (0, 0), (0, pad_to - g.shape[-1])))
    return jnp.transpose(g, (1, 0, 2)).reshape(g.shape[1], N_DEV * g.shape[2])


def _shards_from_cols(a, n):
    k = a.shape[0]
    a = a.reshape(k, N_DEV, a.shape[1] // N_DEV)[:, :, :n]
    return jnp.transpose(a, (1, 0, 2))


def _rows_from_shards(g, pad_to=None):
    if pad_to is not None and pad_to != g.shape[1]:
        g = jnp.pad(g, ((0, 0), (0, pad_to - g.shape[1]), (0, 0)))
    return g.reshape(N_DEV * g.shape[1], g.shape[2])


def _shards_from_rows(a, r):
    n = a.shape[1]
    return a.reshape(N_DEV, a.shape[0] // N_DEV, n)[:, :r, :]


def _rope_tables(pos):
    inv_freq = 1.0 / (ROPE_THETA ** (jnp.arange(0, ROPE, 2, dtype=F32) / ROPE))
    ang = pos.astype(F32)[:, None] * inv_freq
    cos, sin = jnp.cos(ang), jnp.sin(ang)
    z32, z64, z96 = (jnp.zeros((pos.shape[0], n), F32) for n in (32, 64, 96))
    return (jnp.concatenate([cos, cos, z64], axis=1), jnp.concatenate([-sin, z96], axis=1),
            jnp.concatenate([z32, sin, z64], axis=1))


def _pad_row(vec):
    vec = vec.reshape(1, -1)
    return jnp.pad(vec, ((0, 0), (0, D_MODEL - vec.shape[1])))


def _forward_backward(x, target, tables, gains, shards):
    cc, sn, sp = tables
    ffn_g = gains["ffn_norm"]

    def gather(names, relay_at):
        return [shards[n] for n in names], False, relay_at

    (got,) = _exchange([shards["w_in"]], False, "gather_w_in")
    w_in = jnp.pad(_rows_from_shards(got), ((0, 0), (0, PROJ_PAD - PROJ)))
    names = ["w_uq", "w_ukv"]
    (n0, proj, cqn, ckvn, krr), got = _mla_in_fwd(
        x, gains["mla_norm"], w_in, gains["g_cq"], gains["g_ckv"], cc, sn, sp, carry=gather(names, 0.5))
    got = dict(zip(names, got))
    w_uq = jnp.pad(got["w_uq"], ((0, 0), (0, 0), (0, HEAD_PAD - NOPE - ROPE)))
    w_ukv = got["w_ukv"]
    names = ["w_o", "conv_norm", "conv_w"]
    (q, k, v), got = _qkv_proj(cqn, ckvn, krr, w_uq, w_ukv, cc, sn, sp, carry=gather(names, 0.5))
    got = dict(zip(names, got))
    w_o = _rows_from_shards(got["w_o"])
    conv_norm = got["conv_norm"].reshape(1, D_MODEL)
    conv_w = jnp.transpose(got["conv_w"], (1, 0, 2)).reshape(3, D_MODEL)
    names = ["wg0", "wu0", "wd0", "c_in", "c_out"]
    (o, lse), got = _attn_fwd(q, k, v, carry=gather(names, 0.65))
    got = dict(zip(names, got))
    wg0, wu0 = (_cols_from_shards(got[n], FF_SHARD_PAD) for n in ("wg0", "wu0"))
    wd0 = _rows_from_shards(got["wd0"], FF_SHARD_PAD)
    c_in = _cols_from_shards(got["c_in"])
    c_out = _rows_from_shards(got["c_out"])
    h1 = _matmul_res(o, w_o, x, "mla_out_fwd")
    names = ["wg1", "wu1", "wd1"]
    (h2, n1, gate0, up0), got = _ffn_fwd(h1, ffn_g[0:1], wg0, wu0, wd0, "ffn0_fwd", carry=gather(names, 0.85))
    got = dict(zip(names, got))
    wg1, wu1 = (_cols_from_shards(got[n], FF_SHARD_PAD) for n in ("wg1", "wu1"))
    wd1 = _rows_from_shards(got["wd1"], FF_SHARD_PAD)
    n2, bcx = _rms_matmul(h2, conv_norm, c_in, "conv_in_fwd")
    bu = _conv_fwd(bcx, conv_w)
    h3 = _matmul_res(bu, c_out, h2, "conv_out_fwd")
    (h4, n3, gate1, up1), _ = _ffn_fwd(h3, ffn_g[1:2], wg1, wu1, wd1, "ffn1_fwd")
    dh4, d_final, loss = _final_loss(h4, gains["final_norm"], target)

    small = {"final_norm": d_final}
    parts = {}

    def scatter(**blocks):
        return list(blocks), (list(blocks.values()), True, None)

    (dh3, dh3_b, dh4_b, dgate, dup, act, d_ffn1), _ = _ffn_bwd_x(
        dh4, h3, ffn_g[1:2], gate1, up1, wg1, wu1, wd1, "ffn1_bwd")
    dwg1 = _matmul_tn(n3, dgate, "ffn1_dwg")
    dwu1 = _matmul_tn(n3, dup, "ffn1_dwu")
    dwd1 = _matmul_tn(act, dh4_b, "ffn1_dwd")
    dbuc = _matmul_nt(dh3_b, c_out, "conv_out_bwd")
    d_c_out = _matmul_tn(bu, dh3_b, "conv_dwout")
    db, dc, dxp, small["conv_w"] = _conv_bwd(dbuc, bcx, conv_w)
    dbcx = jnp.concatenate([db, dc, dxp], axis=1)
    d_c_in = _matmul_tn(n2, dbcx, "conv_dwin")
    (dh2, dh2_b, small["conv_norm"]), _ = _nt_rmsbwd(dbcx, c_in, h2, conv_norm, dh3, "conv_in_bwd")
    names, carry = scatter(c_in=_shards_from_cols(d_c_in, 3 * D_MODEL // N_DEV),
                           c_out=_shards_from_rows(d_c_out, D_MODEL // N_DEV))
    (dh1, dh1_b, dh2_b2, dgate, dup, act, d_ffn0), got = _ffn_bwd_x(
        dh2, h1, ffn_g[0:1], gate0, up0, wg0, wu0, wd0, "ffn0_bwd", carry=carry)
    parts.update(zip(names, got))
    dwg0 = _matmul_tn(n1, dgate, "ffn0_dwg")
    dwu0 = _matmul_tn(n1, dup, "ffn0_dwu")
    dwd0 = _matmul_tn(act, dh2_b2, "ffn0_dwd")
    small["ffn_norm"] = jnp.pad(d_ffn0, ((0, 7), (0, 0))) + jnp.pad(d_ffn1, ((1, 6), (0, 0)))
    do = _matmul_nt(dh1_b, w_o, "mla_out_bwd")
    d_w_o = _matmul_tn(o, dh1_b, "mla_dwo")
    names, carry = scatter(
        wg=jnp.stack([_shards_from_cols(g, FF_SHARD) for g in (dwg0, dwg1)], axis=1),
        wu=jnp.stack([_shards_from_cols(g, FF_SHARD) for g in (dwu0, dwu1)], axis=1),
        wd=jnp.stack([_shards_from_rows(g, FF_SHARD) for g in (dwd0, dwd1)], axis=1))
    (dq, dk, dv), got = _attn_bwd(q, k, v, do, o, lse, cc, sn, sp, carry=carry)
    parts.update(zip(names, got))

    def head_shards(d, width):
        return jnp.transpose(d.reshape(d.shape[0], N_HEADS, HEAD_PAD)[:, :, :width], (1, 0, 2))

    d_w_uq = _matmul_tn(cqn, dq, "mla_dwuq")
    names, carry = scatter(w_o=_shards_from_rows(d_w_o, D_MODEL // N_DEV), w_uq=head_shards(d_w_uq, NOPE + ROPE))
    (dproj, dkv, small["g_cq"], small["g_ckv"]), got = _mla_mid_bwd(
        dq, dk, dv, proj, w_uq, w_ukv, gains["g_cq"], gains["g_ckv"], cc, sn, sp, carry=carry)
    parts.update(zip(names, got))
    d_w_ukv = _matmul_tn(ckvn, dkv, "mla_dwukv")
    d_w_in = _matmul_tn(n0, dproj, "mla_dwin")
    names, carry = scatter(w_in=_shards_from_rows(d_w_in[:, :PROJ], D_MODEL // N_DEV),
                           w_ukv=head_shards(d_w_ukv, HEAD_PAD))
    (dx, _, small["mla_norm"]), got = _nt_rmsbwd(
        dproj, w_in, x, gains["mla_norm"], dh1, "mla_in_bwd", carry=carry)
    parts.update(zip(names, got))
    return loss, dx, parts, small


def kernel(x, positions, mla_norm, mla_w_in, mla_g_cq, mla_g_ckv, mla_w_uq, mla_w_ukv, mla_w_o, conv_norm, conv_w_in, conv_w, conv_w_out, ffn_norm, ffn_w_gate, ffn_w_up, ffn_w_down, final_norm, loss_target, m_mla_norm, m_mla_w_in, m_mla_g_cq, m_mla_g_ckv, m_mla_w_uq, m_mla_w_ukv, m_mla_w_o, m_conv_norm, m_conv_w_in, m_conv_w, m_conv_w_out, m_ffn_norm, m_ffn_w_gate, m_ffn_w_up, m_ffn_w_down, m_final_norm, v_mla_norm, v_mla_w_in, v_mla_g_cq, v_mla_g_ckv, v_mla_w_uq, v_mla_w_ukv, v_mla_w_o, v_conv_norm, v_conv_w_in, v_conv_w, v_conv_w_out, v_ffn_norm, v_ffn_w_gate, v_ffn_w_up, v_ffn_w_down, v_final_norm):
    me = 4 * lax.axis_index("x") + 2 * lax.axis_index("y") + lax.axis_index("c")

    shards = dict(w_in=mla_w_in[0], w_uq=mla_w_uq[0], w_ukv=mla_w_ukv[0], w_o=mla_w_o[0], c_in=conv_w_in[0],
                  c_out=conv_w_out[0], wg0=ffn_w_gate[0], wg1=ffn_w_gate[1], wu0=ffn_w_up[0], wu1=ffn_w_up[1],
                  wd0=ffn_w_down[0], wd1=ffn_w_down[1])
    shards = {n: a.astype(BF16) for n, a in shards.items()}
    shards.update(conv_norm=conv_norm, conv_w=conv_w[0])
    gains = dict(mla_norm=mla_norm, g_cq=mla_g_cq, g_ckv=mla_g_ckv, ffn_norm=ffn_norm,
                 final_norm=final_norm.reshape(1, -1))
    loss_local, dx, parts, grads = _forward_backward(x[0], loss_target[0], _rope_tables(positions[0]), gains, shards)
    names = ["w_in", "w_uq", "w_ukv", "w_o", "c_in", "c_out", "wg", "wu", "wd"]

    moments = dict(w_in=(m_mla_w_in, v_mla_w_in), w_uq=(m_mla_w_uq, v_mla_w_uq), w_ukv=(m_mla_w_ukv, v_mla_w_ukv),
                   w_o=(m_mla_w_o, v_mla_w_o), c_in=(m_conv_w_in, v_conv_w_in), c_out=(m_conv_w_out, v_conv_w_out),
                   wg=(m_ffn_w_gate, v_ffn_w_gate), wu=(m_ffn_w_up, v_ffn_w_up), wd=(m_ffn_w_down, v_ffn_w_down))
    full = dict(w_in=mla_w_in, w_uq=mla_w_uq, w_ukv=mla_w_ukv, w_o=mla_w_o, c_in=conv_w_in, c_out=conv_w_out,
                wg=ffn_w_gate, wu=ffn_w_up, wd=ffn_w_down)
    res = {}
    for n in names:
        w = full[n]
        shape = w.shape
        c = shape[-1]
        flat = lambda a: a.reshape(-1, c)
        outs = _adamw(parts[n].reshape(N_DEV, -1, c), flat(w), flat(moments[n][0]), flat(moments[n][1]), "adamw_" + n)
        res[n] = [o.reshape(shape) for o in outs]

    col0 = me * (D_MODEL // N_DEV)

    def place(shard):
        return lax.dynamic_update_slice(jnp.zeros((shard.shape[0], D_MODEL), F32), shard, (0, col0))

    no_loss = jnp.zeros((8, D_MODEL), F32)

    def pack(mla_n, g_cq, g_ckv, ffn_n, fin_n, conv_n, conv_taps, loss_tile):
        rows = [mla_n, _pad_row(g_cq), _pad_row(g_ckv), ffn_n, fin_n.reshape(1, -1), conv_n, conv_taps, loss_tile]
        assert all(r.shape[0] in (n, 8) for r, n in zip(rows, SMALL_PIECES))
        return jnp.concatenate([jnp.pad(r, ((0, 8 - r.shape[0]), (0, 0))) for r in rows], axis=0)

    gpack = pack(grads["mla_norm"], grads["g_cq"], grads["g_ckv"], grads["ffn_norm"], grads["final_norm"],
                 grads["conv_norm"], grads["conv_w"], loss_local)
    wpack = pack(mla_norm, mla_g_cq, mla_g_ckv, ffn_norm, final_norm, place(conv_norm), place(conv_w[0]), no_loss)
    mpack = pack(m_mla_norm, m_mla_g_cq, m_mla_g_ckv, m_ffn_norm, m_final_norm, place(m_conv_norm),
                 place(m_conv_w[0]), no_loss)
    vpack = pack(v_mla_norm, v_mla_g_cq, v_mla_g_ckv, v_ffn_norm, v_final_norm, place(v_conv_norm),
                 place(v_conv_w[0]), no_loss)
    small = _small_allreduce_adamw(gpack, wpack, mpack, vpack)
    loss = small[0][56, 0]

    def unpack(p):
        own = lambda rows: lax.dynamic_slice(rows, (0, col0), (rows.shape[0], D_MODEL // N_DEV))
        return dict(mla_norm=p[0:1], g_cq=p[8:9, :CQ], g_ckv=p[16:17, :CKV], ffn_norm=p[24:26], final_norm=p[32],
                    conv_norm=own(p[40:41]), conv_w=own(p[48:51])[None])

    small = [unpack(p) for p in small]
    order = ["mla_norm", "w_in", "g_cq", "g_ckv", "w_uq", "w_ukv", "w_o", "conv_norm", "c_in", "conv_w", "c_out",
             "ffn_norm", "wg", "wu", "wd", "final_norm"]
    out = [loss, dx[None]]
    for kind in range(4):
        for n in order:
            out.append(res[n][kind] if n in res else small[kind][n])
    return tuple(out)
```

```python
import math

import jax
import jax.numpy as jnp
from jax import lax
from jax.experimental import pallas as pl
from jax.experimental.pallas import tpu as pltpu

F32 = jnp.float32
BF16 = jnp.bfloat16

N_DEV = 8
D_MODEL = 1024
N_HEADS = 8
NOPE = 128
ROPE = 64
V_DIM = 128
HEAD_PAD = 256
CQ = 512
CKV = 256
PROJ = CQ + CKV + ROPE
PROJ_PAD = CQ + CKV + 128
D_FF = 2816
FF_SHARD = D_FF // N_DEV
FF_SHARD_PAD = 384
FF_PAD = FF_SHARD_PAD * N_DEV
CHUNK_SHIFT = 6
RMS_EPS = 1e-6
ROPE_THETA = 10000.0
ATT_SCALE = 1.0 / math.sqrt(NOPE + ROPE)
LOG2_E = math.log2(math.e)
LN_2 = math.log(2.0)
Q_SCALE = ATT_SCALE * LOG2_E
NEG = -1e30

ADAM_LR = 0.001
ADAM_B1 = 0.9
ADAM_B2 = 0.999
ADAM_EPS = 1e-08
ADAM_WD = 0.01
ADAM_STEP = 10

SMALL_PIECES = (1, 1, 1, 2, 1, 1, 3, 1)
SMALL_ROWS = 16

_NT = (((1,), (1,)), ((), ()))
_TN = (((0,), (0,)), ((), ()))


def _pc(body, *, name, out_shape, grid=(), in_specs=None, out_specs=None, scratch_shapes=(), vmem_mb=None):
    params = {}
    if vmem_mb is not None:
        params["vmem_limit_bytes"] = vmem_mb << 20
    kwargs = dict(
        name=name, out_shape=out_shape, grid=grid, scratch_shapes=scratch_shapes,
        compiler_params=pltpu.CompilerParams(**params),
    )
    if in_specs is not None:
        kwargs["in_specs"] = in_specs
    if out_specs is not None:
        kwargs["out_specs"] = out_specs
    return pl.pallas_call(body, **kwargs)


def _pc_carrying(body, carry, operands, *, name, out_shape, grid, in_specs, out_specs, scratch_shapes=()):
    if carry is None:
        return _pc(body, name=name, out_shape=out_shape, grid=grid, in_specs=in_specs, out_specs=out_specs,
                   scratch_shapes=scratch_shapes)(*operands), None
    arrays, scatter, relay_at = carry
    nw, n_in, n_out, n_scr = len(arrays), len(in_specs), len(out_shape), len(scratch_shapes)
    start, relay, wait, landed_shapes, sems, arrays = _exchange_ops(arrays, scatter)
    n_steps = math.prod(grid)
    relay_step = None if relay is None else max(1, min(int(relay_at * n_steps), n_steps - 2))

    def wrapped(*refs):
        ins, rest = refs[:n_in], refs[n_in:]
        cin, rest = rest[:nw], rest[nw:]
        outs, rest = rest[:n_out], rest[n_out:]
        cout, rest = rest[:nw], rest[nw:]
        scr, csem = rest[:n_scr], rest[n_scr:]
        step = pl.program_id(0)
        for a in range(1, len(grid)):
            step = step * grid[a] + pl.program_id(a)

        @pl.when(step == 0)
        def _():
            start(cin, cout, *csem)

        if relay is not None:
            @pl.when(step == relay_step)
            def _():
                relay(cin, cout, *csem)

        body(*ins, *outs, *scr)

        @pl.when(step == n_steps - 1)
        def _():
            wait(cin, cout, *csem)

    any_spec = pl.BlockSpec(memory_space=pl.ANY)
    res = _pc(
        wrapped, name=name, grid=grid,
        in_specs=list(in_specs) + [any_spec] * nw, out_specs=list(out_specs) + [any_spec] * nw,
        out_shape=list(out_shape) + landed_shapes, scratch_shapes=list(scratch_shapes) + sems,
    )(*operands, *arrays)
    return res[:n_out], res[n_out:]


def _sds(shape, dtype):
    return jax.ShapeDtypeStruct(shape, dtype)


def _dot(a, b):
    return jnp.dot(a, b, preferred_element_type=F32)


def _dot_nt(a, b):
    return lax.dot_general(a, b, _NT, preferred_element_type=F32)


def _dot_tn(a, b):
    return lax.dot_general(a, b, _TN, preferred_element_type=F32)


def _rstd(x):
    return lax.rsqrt(jnp.mean(x * x, axis=-1, keepdims=True) + RMS_EPS)


def _rms(x, g):
    return (x * _rstd(x)) * g


def _rms_bwd(dy, x, g):
    r = _rstd(x)
    xhat = x * r
    dxhat = dy * g
    dx = r * (dxhat - xhat * jnp.mean(dxhat * xhat, axis=-1, keepdims=True))
    return dx, jnp.sum(dy * xhat, axis=0, keepdims=True)


def _rope(t, cc, sn, sp):
    return t * cc + pltpu.roll(t, 96, 1) * sn + pltpu.roll(t, 32, 1) * sp


def _rope_bwd(dt, cc, sn, sp):
    return dt * cc + pltpu.roll(dt * sn, 32, 1) + pltpu.roll(dt * sp, 96, 1)


def _row_block(s):
    return min(512, s)


def _mla_in_fwd(x, g0, w_in, g_cq, g_ckv, cc, sn, sp, carry=None):
    s = x.shape[0]
    tm = _row_block(s)

    def body(x_ref, g0_ref, w_ref, gcq_ref, gckv_ref, cc_ref, sn_ref, sp_ref,
             n_ref, proj_ref, cqn_ref, ckvn_ref, krr_ref):
        nb = _rms(x_ref[...], g0_ref[...]).astype(BF16)
        n_ref[...] = nb
        proj = _dot(nb, w_ref[...])
        proj_ref[...] = proj
        cqn_ref[...] = _rms(proj[:, :CQ], gcq_ref[...]).astype(BF16)
        ckvn_ref[...] = _rms(proj[:, CQ:CQ + CKV], gckv_ref[...]).astype(BF16)
        krr_ref[...] = _rope(proj[:, CQ + CKV:], cc_ref[...], sn_ref[...], sp_ref[...]).astype(BF16)

    row = lambda n: pl.BlockSpec((tm, n), lambda i: (i, 0))
    full = lambda a: pl.BlockSpec(a.shape, lambda i: (0, 0))
    return _pc_carrying(
        body, carry, (x, g0, w_in, g_cq, g_ckv, cc, sn, sp), name="mla_in_fwd", grid=(s // tm,),
        in_specs=[row(D_MODEL), full(g0), full(w_in), full(g_cq), full(g_ckv), row(128), row(128), row(128)],
        out_specs=[row(D_MODEL), row(PROJ_PAD), row(CQ), row(CKV), row(128)],
        out_shape=[_sds((s, D_MODEL), BF16), _sds((s, PROJ_PAD), F32), _sds((s, CQ), BF16),
                   _sds((s, CKV), BF16), _sds((s, 128), BF16)],
    )


def _qkv_proj(cqn, ckvn, krr, w_uq, w_ukv, cc, sn, sp, carry=None):
    s = cqn.shape[0]
    tm = _row_block(s)

    def body(cqn_ref, ckvn_ref, krr_ref, wuq_ref, wukv_ref, cc_ref, sn_ref, sp_ref, q_ref, k_ref, v_ref):
        q = _dot(cqn_ref[...], wuq_ref[0]) * Q_SCALE
        q_ref[0, :, :NOPE] = q[:, :NOPE].astype(BF16)
        q_ref[0, :, NOPE:] = _rope(q[:, NOPE:], cc_ref[...], sn_ref[...], sp_ref[...]).astype(BF16)
        kv = _dot(ckvn_ref[...], wukv_ref[0])
        k_ref[0, :, :NOPE] = kv[:, :NOPE].astype(BF16)
        k_ref[0, :, NOPE:] = krr_ref[...]
        v_ref[0] = kv[:, NOPE:].astype(BF16)

    row = lambda n: pl.BlockSpec((tm, n), lambda i, h: (i, 0))
    head_w = lambda a: pl.BlockSpec((1,) + a.shape[1:], lambda i, h: (h, 0, 0))
    head_o = lambda n: pl.BlockSpec((1, tm, n), lambda i, h: (h, i, 0))
    return _pc_carrying(
        body, carry, (cqn, ckvn, krr, w_uq, w_ukv, cc, sn, sp), name="qkv_proj", grid=(s // tm, N_HEADS),
        in_specs=[row(CQ), row(CKV), row(128), head_w(w_uq), head_w(w_ukv), row(128), row(128), row(128)],
        out_specs=[head_o(HEAD_PAD), head_o(HEAD_PAD), head_o(V_DIM)],
        out_shape=[_sds((N_HEADS, s, HEAD_PAD), BF16), _sds((N_HEADS, s, HEAD_PAD), BF16),
                   _sds((N_HEADS, s, V_DIM), BF16)],
    )


def _chunk_bias(bq, bk):
    rows = lax.broadcasted_iota(jnp.int32, (bq, bk), 0)
    cols = lax.broadcasted_iota(jnp.int32, (bq, bk), 1)
    visible = jnp.right_shift(cols, CHUNK_SHIFT) <= jnp.right_shift(rows, CHUNK_SHIFT)
    return jnp.where(visible, 0.0, NEG).astype(F32)


def _attn_fwd(q, k, v, carry=None):
    s = q.shape[1]
    bq = _row_block(s)

    def body(q_ref, k_ref, v_ref, o_ref, lse_ref, bias_ref):
        i = pl.program_id(1)
        qb = q_ref[0]

        @pl.when((pl.program_id(0) == 0) & (i == 0))
        def _():
            bias_ref[...] = _chunk_bias(bq, bq)

        def step(j, carry, masked):
            m, l, acc = carry
            start = pl.multiple_of(j * bq, bq)
            kb = k_ref[0, pl.ds(start, bq), :]
            vb = v_ref[0, pl.ds(start, bq), :]
            sc = _dot_nt(qb, kb)
            if masked:
                sc = sc + bias_ref[...]
            m_new = jnp.maximum(m, jnp.max(sc, axis=-1, keepdims=True))
            p = jnp.exp2(sc - m_new)
            alpha = jnp.exp2(m - m_new)
            l = alpha * l + jnp.sum(p, axis=-1, keepdims=True)
            acc = alpha * acc + _dot(p.astype(BF16), vb)
            return m_new, l, acc

        init = (jnp.full((bq, 1), NEG, F32), jnp.zeros((bq, 1), F32), jnp.zeros((bq, V_DIM), F32))
        carry = lax.fori_loop(0, i, lambda j, c: step(j, c, False), init)
        m, l, acc = step(i, carry, True)
        o_ref[...] = (acc / l).astype(BF16)
        lse_ref[0] = jnp.broadcast_to(m + jnp.log(l) * LOG2_E, (bq, 128))

    return _pc_carrying(
        body, carry, (q, k, v), name="attn_fwd", grid=(N_HEADS, s // bq),
        in_specs=[pl.BlockSpec((1, bq, HEAD_PAD), lambda h, i: (h, i, 0)),
                  pl.BlockSpec((1, s, HEAD_PAD), lambda h, i: (h, 0, 0)),
                  pl.BlockSpec((1, s, V_DIM), lambda h, i: (h, 0, 0))],
        out_specs=[pl.BlockSpec((bq, V_DIM), lambda h, i: (i, h)),
                   pl.BlockSpec((1, bq, 128), lambda h, i: (h, i, 0))],
        out_shape=[_sds((s, N_HEADS * V_DIM), BF16), _sds((N_HEADS, s, 128), F32)],
        scratch_shapes=[pltpu.VMEM((bq, bq), F32)],
    )


def _matmul_res(a, w, res, name):
    s, kd = a.shape
    n = w.shape[1]
    tm = _row_block(s)

    def body(a_ref, w_ref, r_ref, o_ref):
        o_ref[...] = r_ref[...] + _dot(a_ref[...], w_ref[...])

    return _pc(
        body, name=name, grid=(s // tm,),
        in_specs=[pl.BlockSpec((tm, kd), lambda i: (i, 0)), pl.BlockSpec((kd, n), lambda i: (0, 0)),
                  pl.BlockSpec((tm, n), lambda i: (i, 0))],
        out_specs=pl.BlockSpec((tm, n), lambda i: (i, 0)),
        out_shape=_sds((s, n), F32),
    )(a, w, res)


def _ffn_fwd(h, gain, wg, wu, wd, name, carry=None):
    s = h.shape[0]
    tm = _row_block(s)
    tf = 512
    nf = FF_PAD // tf

    def body(h_ref, g_ref, wg_ref, wu_ref, wd_ref, o_ref, n_ref, gate_ref, up_ref, acc_ref):
        j = pl.program_id(1)

        @pl.when(j == 0)
        def _():
            n_ref[...] = _rms(h_ref[...], g_ref[...]).astype(BF16)
            acc_ref[...] = jnp.zeros_like(acc_ref)

        nb = n_ref[...]
        gate = _dot(nb, wg_ref[...])
        up = _dot(nb, wu_ref[...])
        gate_ref[...] = gate.astype(BF16)
        up_ref[...] = up.astype(BF16)
        act = gate * jax.nn.sigmoid(gate) * up
        acc_ref[...] += _dot(act.astype(BF16), wd_ref[...])

        @pl.when(j == nf - 1)
        def _():
            o_ref[...] = h_ref[...] + acc_ref[...]

    return _pc_carrying(
        body, carry, (h, gain, wg, wu, wd), name=name, grid=(s // tm, nf),
        in_specs=[pl.BlockSpec((tm, D_MODEL), lambda i, j: (i, 0)), pl.BlockSpec((1, D_MODEL), lambda i, j: (0, 0)),
                  pl.BlockSpec((D_MODEL, tf), lambda i, j: (0, j)), pl.BlockSpec((D_MODEL, tf), lambda i, j: (0, j)),
                  pl.BlockSpec((tf, D_MODEL), lambda i, j: (j, 0))],
        out_specs=[pl.BlockSpec((tm, D_MODEL), lambda i, j: (i, 0)), pl.BlockSpec((tm, D_MODEL), lambda i, j: (i, 0)),
                   pl.BlockSpec((tm, tf), lambda i, j: (i, j)), pl.BlockSpec((tm, tf), lambda i, j: (i, j))],
        out_shape=[_sds((s, D_MODEL), F32), _sds((s, D_MODEL), BF16), _sds((s, FF_PAD), BF16),
                   _sds((s, FF_PAD), BF16)],
        scratch_shapes=[pltpu.VMEM((tm, D_MODEL), F32)],
    )


def _rms_matmul(h, gain, w, name):
    s = h.shape[0]
    n = w.shape[1]
    tm = _row_block(s)
    tn = 1024
    nn = n // tn

    def body(h_ref, g_ref, w_ref, n_ref, o_ref):
        @pl.when(pl.program_id(1) == 0)
        def _():
            n_ref[...] = _rms(h_ref[...], g_ref[...]).astype(BF16)

        o_ref[...] = _dot(n_ref[...], w_ref[...]).astype(BF16)

    return _pc(
        body, name=name, grid=(s // tm, nn),
        in_specs=[pl.BlockSpec((tm, D_MODEL), lambda i, j: (i, 0)), pl.BlockSpec((1, D_MODEL), lambda i, j: (0, 0)),
                  pl.BlockSpec((D_MODEL, tn), lambda i, j: (0, j))],
        out_specs=[pl.BlockSpec((tm, D_MODEL), lambda i, j: (i, 0)), pl.BlockSpec((tm, tn), lambda i, j: (i, j))],
        out_shape=[_sds((s, D_MODEL), BF16), _sds((s, n), BF16)],
    )(h, gain, w)


def _shift_down(u, k, rows):
    return jnp.where(rows >= k, pltpu.roll(u, k, 0), 0.0)


def _shift_up(u, k, rows, s):
    return jnp.where(rows < s - k, pltpu.roll(u, s - k, 0), 0.0)


_CONV_COLS = 128


def _conv_fwd(bcx, cw):
    s = bcx.shape[0]
    tc = _CONV_COLS
    nc = D_MODEL // tc

    def body(b_ref, c_ref, xp_ref, w_ref, o_ref):
        rows = lax.broadcasted_iota(jnp.int32, (s, tc), 0)
        u = c_ref[...].astype(F32) * xp_ref[...].astype(F32)
        w = w_ref[...]
        uc = w[2:3] * u + w[1:2] * _shift_down(u, 1, rows) + w[0:1] * _shift_down(u, 2, rows)
        o_ref[...] = (b_ref[...].astype(F32) * uc).astype(BF16)

    col = lambda off: pl.BlockSpec((s, tc), lambda j: (0, off + j))
    return _pc(
        body, name="conv_fwd", grid=(nc,),
        in_specs=[col(0), col(nc), col(2 * nc), pl.BlockSpec((3, tc), lambda j: (0, j))],
        out_specs=pl.BlockSpec((s, tc), lambda j: (0, j)),
        out_shape=_sds((s, D_MODEL), BF16),
    )(bcx, bcx, bcx, cw)


def _final_loss(h, gain, target):
    s = h.shape[0]
    tm = _row_block(s)

    def body(h_ref, g_ref, t_ref, dh_ref, dg_ref, loss_ref):
        i = pl.program_id(0)

        @pl.when(i == 0)
        def _():
            dg_ref[...] = jnp.zeros_like(dg_ref)
            loss_ref[...] = jnp.zeros_like(loss_ref)

        hb = h_ref[...]
        e = _rms(hb, g_ref[...]) - t_ref[...]
        loss_ref[...] += 0.5 * jnp.sum(jnp.mean(e * e, axis=-1, keepdims=True))
        dx, dg = _rms_bwd(e * (1.0 / D_MODEL), hb, g_ref[...])
        dh_ref[...] = dx
        dg_ref[...] += dg

    row = pl.BlockSpec((tm, D_MODEL), lambda i: (i, 0))
    vec = pl.BlockSpec((1, D_MODEL), lambda i: (0, 0))
    return _pc(
        body, name="final_loss", grid=(s // tm,),
        in_specs=[row, vec, row],
        out_specs=[row, vec, pl.BlockSpec((8, D_MODEL), lambda i: (0, 0))],
        out_shape=[_sds((s, D_MODEL), F32), _sds((1, D_MODEL), F32), _sds((8, D_MODEL), F32)],
    )(h, gain, target)


def _ffn_bwd_x(dho, h, gain, gate, up, wg, wu, wd, name, carry=None):
    s = h.shape[0]
    tm = _row_block(s)
    tf = 512
    nf = FF_PAD // tf

    def body(dho_ref, h_ref, g_ref, gate_ref, up_ref, wg_ref, wu_ref, wd_ref,
             dh_ref, dhb_ref, dhob_ref, dgate_ref, dup_ref, act_ref, dgain_ref, acc_ref):
        i = pl.program_id(0)
        j = pl.program_id(1)

        @pl.when(j == 0)
        def _():
            dhob_ref[...] = dho_ref[...].astype(BF16)
            acc_ref[...] = jnp.zeros_like(acc_ref)

        @pl.when((i == 0) & (j == 0))
        def _():
            dgain_ref[...] = jnp.zeros_like(dgain_ref)

        dact = _dot_nt(dhob_ref[...], wd_ref[...])
        g = gate_ref[...].astype(F32)
        u = up_ref[...].astype(F32)
        sg = jax.nn.sigmoid(g)
        silu = g * sg
        dg = (dact * u * (sg * (1.0 + g * (1.0 - sg)))).astype(BF16)
        du = (dact * silu).astype(BF16)
        dgate_ref[...] = dg
        dup_ref[...] = du
        act_ref[...] = (silu * u).astype(BF16)
        acc_ref[...] += _dot_nt(dg, wg_ref[...]) + _dot_nt(du, wu_ref[...])

        @pl.when(j == nf - 1)
        def _():
            dx, dgain = _rms_bwd(acc_ref[...], h_ref[...], g_ref[...])
            dh = dho_ref[...] + dx
            dh_ref[...] = dh
            dhb_ref[...] = dh.astype(BF16)
            dgain_ref[...] += dgain

    row = pl.BlockSpec((tm, D_MODEL), lambda i, j: (i, 0))
    vec = pl.BlockSpec((1, D_MODEL), lambda i, j: (0, 0))
    hid = pl.BlockSpec((tm, tf), lambda i, j: (i, j))
    wcol = pl.BlockSpec((D_MODEL, tf), lambda i, j: (0, j))
    wrow = pl.BlockSpec((tf, D_MODEL), lambda i, j: (j, 0))
    return _pc_carrying(
        body, carry, (dho, h, gain, gate, up, wg, wu, wd), name=name, grid=(s // tm, nf),
        in_specs=[row, row, vec, hid, hid, wcol, wcol, wrow],
        out_specs=[row, row, row, hid, hid, hid, vec],
        out_shape=[_sds((s, D_MODEL), F32), _sds((s, D_MODEL), BF16), _sds((s, D_MODEL), BF16),
                   _sds((s, FF_PAD), BF16), _sds((s, FF_PAD), BF16), _sds((s, FF_PAD), BF16),
                   _sds((1, D_MODEL), F32)],
        scratch_shapes=[pltpu.VMEM((tm, D_MODEL), F32)],
    )


def _nt_rmsbwd(a, w, h, gain, dho, name, carry=None):
    s, kd = a.shape
    tm = _row_block(s)
    tk = kd if kd <= 1024 else 1024
    nk = kd // tk

    def body(a_ref, w_ref, h_ref, g_ref, dho_ref, dh_ref, dhb_ref, dgain_ref, acc_ref):
        i = pl.program_id(0)
        j = pl.program_id(1)

        @pl.when(j == 0)
        def _():
            acc_ref[...] = jnp.zeros_like(acc_ref)

        @pl.when((i == 0) & (j == 0))
        def _():
            dgain_ref[...] = jnp.zeros_like(dgain_ref)

        acc_ref[...] += _dot_nt(a_ref[...], w_ref[...])

        @pl.when(j == nk - 1)
        def _():
            dx, dgain = _rms_bwd(acc_ref[...], h_ref[...], g_ref[...])
            dh = dho_ref[...] + dx
            dh_ref[...] = dh
            dhb_ref[...] = dh.astype(BF16)
            dgain_ref[...] += dgain

    row = pl.BlockSpec((tm, D_MODEL), lambda i, j: (i, 0))
    vec = pl.BlockSpec((1, D_MODEL), lambda i, j: (0, 0))
    return _pc_carrying(
        body, carry, (a, w, h, gain, dho), name=name, grid=(s // tm, nk),
        in_specs=[pl.BlockSpec((tm, tk), lambda i, j: (i, j)), pl.BlockSpec((D_MODEL, tk), lambda i, j: (0, j)),
                  row, vec, row],
        out_specs=[row, row, vec],
        out_shape=[_sds((s, D_MODEL), F32), _sds((s, D_MODEL), BF16), _sds((1, D_MODEL), F32)],
        scratch_shapes=[pltpu.VMEM((tm, D_MODEL), F32)],
    )


def _matmul_nt(a, w, name):
    s, kd = a.shape
    n = w.shape[0]
    tm = _row_block(s)

    def body(a_ref, w_ref, o_ref):
        o_ref[...] = _dot_nt(a_ref[...], w_ref[...]).astype(BF16)

    return _pc(
        body, name=name, grid=(s // tm,),
        in_specs=[pl.BlockSpec((tm, kd), lambda i: (i, 0)), pl.BlockSpec((n, kd), lambda i: (0, 0))],
        out_specs=pl.BlockSpec((tm, n), lambda i: (i, 0)),
        out_shape=_sds((s, n), BF16),
    )(a, w)


def _matmul_tn(a, b, name, carry=None):
    s, m = a.shape
    n = b.shape[1]
    tmm = min(m, 1024)
    tn = n if n <= 1024 else 1024
    tk = _row_block(s)
    nk = s // tk

    def body(a_ref, b_ref, o_ref, acc_ref):
        k = pl.program_id(2)

        @pl.when(k == 0)
        def _():
            acc_ref[...] = jnp.zeros_like(acc_ref)

        acc_ref[...] += _dot_tn(a_ref[...], b_ref[...])

        @pl.when(k == nk - 1)
        def _():
            o_ref[...] = acc_ref[...].astype(BF16)

    (out,), got = _pc_carrying(
        body, carry, (a, b), name=name, grid=(m // tmm, n // tn, nk),
        in_specs=[pl.BlockSpec((tk, tmm), lambda i, j, k: (k, i)), pl.BlockSpec((tk, tn), lambda i, j, k: (k, j))],
        out_specs=[pl.BlockSpec((tmm, tn), lambda i, j, k: (i, j))],
        out_shape=[_sds((m, n), BF16)],
        scratch_shapes=[pltpu.VMEM((tmm, tn), F32)],
    )
    return out if carry is None else (out, got)


def _conv_bwd(dbuc, bcx, cw):
    s = bcx.shape[0]
    tc = _CONV_COLS
    nc = D_MODEL // tc

    def body(d_ref, b_ref, c_ref, xp_ref, w_ref, db_ref, dc_ref, dxp_ref, dw_ref):
        rows = lax.broadcasted_iota(jnp.int32, (s, tc), 0)
        c = c_ref[...].astype(F32)
        xp = xp_ref[...].astype(F32)
        u = c * xp
        u1 = _shift_down(u, 1, rows)
        u2 = _shift_down(u, 2, rows)
        w = w_ref[...]
        uc = w[2:3] * u + w[1:2] * u1 + w[0:1] * u2
        d = d_ref[...].astype(F32)
        db_ref[...] = (d * uc).astype(BF16)
        duc = d * b_ref[...].astype(F32)
        du = w[2:3] * duc + w[1:2] * _shift_up(duc, 1, rows, s) + w[0:1] * _shift_up(duc, 2, rows, s)
        dc_ref[...] = (du * xp).astype(BF16)
        dxp_ref[...] = (du * c).astype(BF16)
        dw_ref[0:1, :] = jnp.sum(duc * u2, axis=0, keepdims=True)
        dw_ref[1:2, :] = jnp.sum(duc * u1, axis=0, keepdims=True)
        dw_ref[2:3, :] = jnp.sum(duc * u, axis=0, keepdims=True)

    col = lambda off: pl.BlockSpec((s, tc), lambda j: (0, off + j))
    tap = pl.BlockSpec((3, tc), lambda j: (0, j))
    return _pc(
        body, name="conv_bwd", grid=(nc,),
        in_specs=[col(0), col(0), col(nc), col(2 * nc), tap],
        out_specs=[col(0), col(0), col(0), tap],
        out_shape=[_sds((s, D_MODEL), BF16), _sds((s, D_MODEL), BF16), _sds((s, D_MODEL), BF16),
                   _sds((3, D_MODEL), F32)],
    )(dbuc, bcx, bcx, bcx, cw)


def _attn_bwd(q, k, v, do, o, lse, cc, sn, sp, carry=None):
    s = q.shape[1]
    bk = _row_block(s)
    nb = s // bk

    def body(q_ref, k_ref, v_ref, do_ref, o_ref, lse_ref, cc_ref, sn_ref, sp_ref,
             dq_ref, dk_ref, dv_ref, dqacc_ref, delta_ref, bias_ref):
        j = pl.program_id(1)
        kb = k_ref[0]
        vb = v_ref[0]

        @pl.when((pl.program_id(0) == 0) & (j == 0))
        def _():
            bias_ref[...] = _chunk_bias(bk, bk)

        @pl.when(j == 0)
        def _():
            dqacc_ref[...] = jnp.zeros_like(dqacc_ref)

            def fill(i, _):
                rows = pl.ds(pl.multiple_of(i * bk, bk), bk)
                d = jnp.sum(do_ref[rows, :].astype(F32) * o_ref[rows, :].astype(F32), axis=-1, keepdims=True)
                delta_ref[rows, :] = jnp.broadcast_to(d, (bk, 128))
                return 0

            lax.fori_loop(0, nb, fill, 0)

        def step(i, carry, masked):
            dk, dv = carry
            rows = pl.ds(pl.multiple_of(i * bk, bk), bk)
            qb = q_ref[0, rows, :]
            dob = do_ref[rows, :]
            sc = _dot_nt(qb, kb)
            if masked:
                sc = sc + bias_ref[...]
            p = jnp.exp2(sc - lse_ref[0, rows, :][:, :1])
            dv = dv + _dot_tn(p.astype(BF16), dob)
            ds = (p * (_dot_nt(dob, vb) - delta_ref[rows, :][:, :1])).astype(BF16)
            dk = dk + _dot_tn(ds, qb)
            dqacc_ref[rows, :] += _dot(ds, kb)
            return dk, dv

        carry = step(j, (jnp.zeros((bk, HEAD_PAD), F32), jnp.zeros((bk, V_DIM), F32)), True)
        dq = dqacc_ref[pl.ds(pl.multiple_of(j * bk, bk), bk), :] * ATT_SCALE
        dq_ref[:, :NOPE] = dq[:, :NOPE].astype(BF16)
        dq_ref[:, NOPE:] = _rope_bwd(dq[:, NOPE:], cc_ref[...], sn_ref[...], sp_ref[...]).astype(BF16)
        dk, dv = lax.fori_loop(j + 1, nb, lambda i, c: step(i, c, False), carry)
        dk_ref[0] = (dk * LN_2).astype(BF16)
        dv_ref[0] = dv.astype(BF16)

    blk = lambda n: pl.BlockSpec((1, bk, n), lambda h, j: (h, j, 0))
    whole = lambda n: pl.BlockSpec((1, s, n), lambda h, j: (h, 0, 0))
    cols = pl.BlockSpec((s, V_DIM), lambda h, j: (0, h))
    tab = pl.BlockSpec((bk, 128), lambda h, j: (j, 0))
    return _pc_carrying(
        body, carry, (q, k, v, do, o, lse, cc, sn, sp), name="attn_bwd", grid=(N_HEADS, nb),
        in_specs=[whole(HEAD_PAD), blk(HEAD_PAD), blk(V_DIM), cols, cols, whole(128), tab, tab, tab],
        out_specs=[pl.BlockSpec((bk, HEAD_PAD), lambda h, j: (j, h)), blk(HEAD_PAD), blk(V_DIM)],
        out_shape=[_sds((s, N_HEADS * HEAD_PAD), BF16), _sds((N_HEADS, s, HEAD_PAD), BF16),
                   _sds((N_HEADS, s, V_DIM), BF16)],
        scratch_shapes=[pltpu.VMEM((s, HEAD_PAD), F32), pltpu.VMEM((s, 128), F32), pltpu.VMEM((bk, bk), F32)],
    )


def _mla_mid_bwd(dq, dk, dv, proj, w_uq, w_ukv, g_cq, g_ckv, cc, sn, sp, carry=None):
    s = dq.shape[0]
    tm = _row_block(s)

    def body(dq_ref, dk_ref, dv_ref, proj_ref, wuq_ref, wukv_ref, gcq_ref, gckv_ref, cc_ref, sn_ref, sp_ref,
             dproj_ref, dkv_ref, dgcq_ref, dgckv_ref, acq_ref, ackv_ref, akr_ref):
        i = pl.program_id(0)
        h = pl.program_id(1)

        @pl.when(h == 0)
        def _():
            acq_ref[...] = jnp.zeros_like(acq_ref)
            ackv_ref[...] = jnp.zeros_like(ackv_ref)
            akr_ref[...] = jnp.zeros_like(akr_ref)

        @pl.when((i == 0) & (h == 0))
        def _():
            dgcq_ref[...] = jnp.zeros_like(dgcq_ref)
            dgckv_ref[...] = jnp.zeros_like(dgckv_ref)

        dkb = dk_ref[0]
        dkv_ref[:, :NOPE] = dkb[:, :NOPE]
        dkv_ref[:, NOPE:] = dv_ref[0]
        acq_ref[...] += _dot_nt(dq_ref[...], wuq_ref[0])
        ackv_ref[...] += _dot_nt(dkv_ref[...], wukv_ref[0])
        akr_ref[...] += dkb[:, NOPE:].astype(F32)

        @pl.when(h == N_HEADS - 1)
        def _():
            proj = proj_ref[...]
            dcq, dgcq = _rms_bwd(acq_ref[...], proj[:, :CQ], gcq_ref[...])
            dckv, dgckv = _rms_bwd(ackv_ref[...], proj[:, CQ:CQ + CKV], gckv_ref[...])
            dproj_ref[:, :CQ] = dcq.astype(BF16)
            dproj_ref[:, CQ:CQ + CKV] = dckv.astype(BF16)
            dproj_ref[:, CQ + CKV:] = _rope_bwd(akr_ref[...], cc_ref[...], sn_ref[...], sp_ref[...]).astype(BF16)
            dgcq_ref[...] += dgcq
            dgckv_ref[...] += dgckv

    head_blk = lambda n: pl.BlockSpec((1, tm, n), lambda i, h: (h, i, 0))
    head_cols = pl.BlockSpec((tm, HEAD_PAD), lambda i, h: (i, h))
    head_w = lambda a: pl.BlockSpec((1,) + a.shape[1:], lambda i, h: (h, 0, 0))
    row = lambda n: pl.BlockSpec((tm, n), lambda i, h: (i, 0))
    vec = lambda n: pl.BlockSpec((1, n), lambda i, h: (0, 0))
    return _pc_carrying(
        body, carry, (dq, dk, dv, proj, w_uq, w_ukv, g_cq, g_ckv, cc, sn, sp),
        name="mla_mid_bwd", grid=(s // tm, N_HEADS),
        in_specs=[head_cols, head_blk(HEAD_PAD), head_blk(V_DIM), row(PROJ_PAD), head_w(w_uq), head_w(w_ukv),
                  vec(CQ), vec(CKV), row(128), row(128), row(128)],
        out_specs=[row(PROJ_PAD), head_cols, vec(CQ), vec(CKV)],
        out_shape=[_sds((s, PROJ_PAD), BF16), _sds((s, N_HEADS * HEAD_PAD), BF16), _sds((1, CQ), F32),
                   _sds((1, CKV), F32)],
        scratch_shapes=[pltpu.VMEM((tm, CQ), F32), pltpu.VMEM((tm, CKV), F32), pltpu.VMEM((tm, 128), F32)],
    )


def _peer(k):
    x, y, c = lax.axis_index("x"), lax.axis_index("y"), lax.axis_index("c")
    px = 1 - x if k & 4 else x
    py = 1 - y if k & 2 else y
    pc = 1 - c if k & 1 else c
    return (px, py, pc), 4 * px + 2 * py + pc


def _exchange_ops(arrays, scatter):
    nw = len(arrays)
    by_cols = [isinstance(a, _Cols) for a in arrays]
    arrays = [a.array if isinstance(a, _Cols) else a for a in arrays]
    direct = range(1, N_DEV) if scatter else (1, 2, 4, 6)

    def columns(ref, idx, width):
        return ref.at[:, pl.ds(pl.multiple_of(idx * width, 128), width)]

    def sent(ins, w, idx):
        if not scatter:
            return ins[w]
        return columns(ins[w], idx, arrays[w].shape[1] // N_DEV) if by_cols[w] else ins[w].at[idx]

    def slot(outs, w, idx):
        if by_cols[w] and not scatter:
            return columns(outs[w], idx, arrays[w].shape[1])
        return outs[w].at[idx]

    def copy(w, k, src, dst, to, send_sems, recv_sems):
        return pltpu.make_async_remote_copy(
            src_ref=src, dst_ref=dst, send_sem=send_sems.at[w * N_DEV + k], recv_sem=recv_sems.at[w * N_DEV + k],
            device_id=to, device_id_type=pl.DeviceIdType.MESH)

    def own_copies(ins, outs, local_sems):
        _, me = _peer(0)
        return [pltpu.make_async_copy(sent(ins, w, me), slot(outs, w, me), local_sems.at[w]) for w in range(nw)]

    def sends(ins, outs, send_sems, recv_sems):
        _, me = _peer(0)
        out = []
        for k in direct:
            dev, idx = _peer(k)
            for w in range(nw):
                out.append(copy(w, k, sent(ins, w, idx), slot(outs, w, me), dev, send_sems, recv_sems))
        return out

    def relays(outs, send_sems, recv_sems):
        sibling, _ = _peer(1)
        out = []
        for k in (2, 4, 6):
            _, idx = _peer(k)
            for w in range(nw):
                out.append(copy(w, k + 1, slot(outs, w, idx), slot(outs, w, idx), sibling, send_sems, recv_sems))
        return out

    def arrival(outs, w, k, send_sems, recv_sems):
        dev, idx = _peer(k)
        return copy(w, k, slot(outs, w, idx), slot(outs, w, idx), dev, send_sems, recv_sems)

    def start(ins, outs, send_sems, recv_sems, local_sems):
        for cp in own_copies(ins, outs, local_sems) + sends(ins, outs, send_sems, recv_sems):
            cp.start()

    def relay(ins, outs, send_sems, recv_sems, local_sems):
        for k in (2, 4, 6):
            for w in range(nw):
                arrival(outs, w, k, send_sems, recv_sems).wait_recv()
        for cp in relays(outs, send_sems, recv_sems):
            cp.start()

    def wait(ins, outs, send_sems, recv_sems, local_sems):
        for cp in own_copies(ins, outs, local_sems):
            cp.wait()
        for cp in sends(ins, outs, send_sems, recv_sems) + ([] if scatter else relays(outs, send_sems, recv_sems)):
            cp.wait_send()
        for k in (range(1, N_DEV) if scatter else (1, 3, 5, 7)):
            for w in range(nw):
                arrival(outs, w, k, send_sems, recv_sems).wait_recv()

    def landed_shape(a, cols):
        if not cols:
            return a.shape if scatter else (N_DEV,) + a.shape
        r, c = a.shape
        return (N_DEV, r, c // N_DEV) if scatter else (r, N_DEV * c)

    landed = [_sds(landed_shape(a, cols), a.dtype) for a, cols in zip(arrays, by_cols)]
    sems = [pltpu.SemaphoreType.DMA((nw * N_DEV,)), pltpu.SemaphoreType.DMA((nw * N_DEV,)),
            pltpu.SemaphoreType.DMA((nw,))]
    return start, (None if scatter else relay), wait, landed, sems, arrays


class _Cols:
    def __init__(self, array):
        self.array = array


def _exchange(arrays, scatter, name):
    nw = len(arrays)
    start, relay, wait, landed, sems, arrays = _exchange_ops(arrays, scatter)

    def body(*refs):
        ins, outs, csem = refs[:nw], refs[nw:2 * nw], refs[2 * nw:]
        start(ins, outs, *csem)
        if relay is not None:
            relay(ins, outs, *csem)
        wait(ins, outs, *csem)

    any_spec = pl.BlockSpec(memory_space=pl.ANY)
    return _pc(body, name=name, in_specs=[any_spec] * nw, out_specs=[any_spec] * nw, out_shape=landed,
               scratch_shapes=sems)(*arrays)


def _adam_math(g, w, m, v):
    m = ADAM_B1 * m + (1.0 - ADAM_B1) * g
    v = ADAM_B2 * v + (1.0 - ADAM_B2) * jnp.square(g)
    m_hat = m / (1.0 - ADAM_B1 ** ADAM_STEP)
    v_hat = v / (1.0 - ADAM_B2 ** ADAM_STEP)
    delta = -ADAM_LR * (m_hat / (jnp.sqrt(v_hat) + ADAM_EPS) + ADAM_WD * w)
    return delta, m, v


def _adam_rows(r):
    for t in (256, 128, 64, 32, 16):
        if r % t == 0:
            return t
    return r


def _adamw(parts, w, m, v, name):
    nl, r, c = w.shape
    tr = _adam_rows(r)
    nr = r // tr

    def body(*refs):
        p_refs, (w_ref, m_ref, v_ref, g_ref, d_ref, mo_ref, vo_ref) = refs[:nl], refs[nl:]
        for layer in range(nl):
            @pl.when(pl.program_id(0) == layer)
            def _(p_ref=p_refs[layer]):
                g = p_ref[0].astype(F32)
                for src in range(1, N_DEV):
                    g = g + p_ref[src].astype(F32)
                delta, m2, v2 = _adam_math(g, w_ref[0], m_ref[0], v_ref[0])
                g_ref[0] = g
                d_ref[0] = delta
                mo_ref[0] = m2
                vo_ref[0] = v2

    def part_spec(layer):
        def index(l, i):
            return 0, jnp.where(l == layer, i, jnp.where(l < layer, 0, nr - 1)), 0
        return pl.BlockSpec((N_DEV, tr, c), index)

    blk = pl.BlockSpec((1, tr, c), lambda l, i: (l, i, 0))
    return _pc(
        body, name=name, grid=(nl, nr),
        in_specs=[part_spec(layer) for layer in range(nl)] + [blk, blk, blk],
        out_specs=[blk, blk, blk, blk],
        out_shape=[_sds((nl, r, c), F32)] * 4,
    )(*parts, w, m, v)


def _small_allreduce_adamw(gpack, wpack, mpack, vpack, last_grads):
    shape = gpack.shape
    compact = (SMALL_ROWS, D_MODEL)
    nw = len(last_grads)
    start, _, wait, landed, exchange_sems, last_grads = _exchange_ops(last_grads, True)

    def body(*refs):
        (g_ref, w_ref, m_ref, v_ref), rest = refs[:4], refs[4:]
        cin, rest = rest[:nw], rest[nw:]
        (go_ref, d_ref, mo_ref, vo_ref), rest = rest[:4], rest[4:]
        cout, rest = rest[:nw], rest[nw:]
        (comp_ref, gath_ref, send_sems, recv_sems), csem = rest[:4], rest[4:]
        start(cin, cout, *csem)
        _, me = _peer(0)
        comp_ref[...] = jnp.zeros(compact, F32)
        r = 0
        for p, n in enumerate(SMALL_PIECES):
            comp_ref[r:r + n, :] = g_ref[8 * p:8 * p + n, :]
            r += n
        gath_ref[me] = comp_ref[...]
        copies = []
        for k in range(1, N_DEV):
            dev, idx = _peer(k)
            copies.append(pltpu.make_async_remote_copy(
                src_ref=comp_ref, dst_ref=gath_ref.at[me], send_sem=send_sems.at[k], recv_sem=recv_sems.at[k],
                device_id=dev, device_id_type=pl.DeviceIdType.MESH))
        for cp in copies:
            cp.start()
        for cp in copies:
            cp.wait_send()
        for k in range(1, N_DEV):
            dev, idx = _peer(k)
            pltpu.make_async_remote_copy(
                src_ref=comp_ref, dst_ref=gath_ref.at[idx], send_sem=send_sems.at[k], recv_sem=recv_sems.at[k],
                device_id=dev, device_id_type=pl.DeviceIdType.MESH).wait_recv()
        g = gath_ref[0]
        for src in range(1, N_DEV):
            g = g + gath_ref[src]
        go_ref[...] = jnp.zeros(shape, F32)
        r = 0
        for p, n in enumerate(SMALL_PIECES):
            go_ref[8 * p:8 * p + n, :] = g[r:r + n, :]
            r += n
        delta, m2, v2 = _adam_math(go_ref[...], w_ref[...], m_ref[...], v_ref[...])
        d_ref[...] = delta
        mo_ref[...] = m2
        vo_ref[...] = v2
        wait(cin, cout, *csem)

    vm = pl.BlockSpec(memory_space=pltpu.VMEM)
    any_spec = pl.BlockSpec(memory_space=pl.ANY)
    res = _pc(
        body, name="small_allreduce_adamw",
        in_specs=[vm] * 4 + [any_spec] * nw, out_specs=[vm] * 4 + [any_spec] * nw,
        out_shape=[_sds(shape, F32)] * 4 + landed,
        scratch_shapes=[pltpu.VMEM(compact, F32), pltpu.VMEM((N_DEV,) + compact, F32),
                        pltpu.SemaphoreType.DMA((N_DEV,)), pltpu.SemaphoreType.DMA((N_DEV,))] + exchange_sems,
    )(gpack, wpack, mpack, vpack, *last_grads)
    return res[:4], res[4:]


def _rows_from_shards(g):
    return g.reshape(N_DEV * g.shape[1], g.shape[2])


def _shards_from_rows(a):
    return a.reshape(N_DEV, a.shape[0] // N_DEV, a.shape[1])


def _pad_to(a, rows, cols):
    return jnp.pad(a, ((0, rows - a.shape[0]), (0, cols - a.shape[1])))


def _rope_tables(pos):
    inv_freq = 1.0 / (ROPE_THETA ** (jnp.arange(0, ROPE, 2, dtype=F32) / ROPE))
    ang = pos.astype(F32)[:, None] * inv_freq
    cos, sin = jnp.cos(ang), jnp.sin(ang)
    z32, z64, z96 = (jnp.zeros((pos.shape[0], n), F32) for n in (32, 64, 96))
    return (jnp.concatenate([cos, cos, z64], axis=1), jnp.concatenate([-sin, z96], axis=1),
            jnp.concatenate([z32, sin, z64], axis=1))


def _pad_row(vec):
    vec = vec.reshape(1, -1)
    return jnp.pad(vec, ((0, 0), (0, D_MODEL - vec.shape[1])))


def _forward_backward(x, target, tables, gains, shards):
    cc, sn, sp = tables
    ffn_g = gains["ffn_norm"]
    by_cols = ("wg0", "wu0", "wg1", "wu1", "c_in")

    def gather(names, relay_at):
        return [_Cols(shards[n]) if n in by_cols else shards[n] for n in names], False, relay_at

    (got,) = _exchange([shards["w_in"]], False, "gather_w_in")
    w_in = _rows_from_shards(got)
    (n0, proj, cqn, ckvn, krr), (w_uq, w_ukv) = _mla_in_fwd(
        x, gains["mla_norm"], w_in, gains["g_cq"], gains["g_ckv"], cc, sn, sp, carry=gather(["w_uq", "w_ukv"], 0.5))
    (q, k, v), (w_o, conv_norm, conv_w) = _qkv_proj(
        cqn, ckvn, krr, w_uq, w_ukv, cc, sn, sp, carry=gather(["w_o", "conv_norm", "conv_w"], 0.5))
    w_o = _rows_from_shards(w_o)
    conv_norm = conv_norm.reshape(1, D_MODEL)
    conv_w = jnp.transpose(conv_w, (1, 0, 2)).reshape(3, D_MODEL)
    (o, lse), (wg0, wu0, wd0, c_in, c_out) = _attn_fwd(
        q, k, v, carry=gather(["wg0", "wu0", "wd0", "c_in", "c_out"], 0.65))
    wd0 = _rows_from_shards(wd0)
    c_out = _rows_from_shards(c_out)
    h1 = _matmul_res(o, w_o, x, "mla_out_fwd")
    (h2, n1, gate0, up0), (wg1, wu1, wd1) = _ffn_fwd(
        h1, ffn_g[0:1], wg0, wu0, wd0, "ffn0_fwd", carry=gather(["wg1", "wu1", "wd1"], 0.85))
    wd1 = _rows_from_shards(wd1)
    n2, bcx = _rms_matmul(h2, conv_norm, c_in, "conv_in_fwd")
    bu = _conv_fwd(bcx, conv_w)
    h3 = _matmul_res(bu, c_out, h2, "conv_out_fwd")
    (h4, n3, gate1, up1), _ = _ffn_fwd(h3, ffn_g[1:2], wg1, wu1, wd1, "ffn1_fwd")
    dh4, d_final, loss = _final_loss(h4, gains["final_norm"], target)

    small = {"final_norm": d_final}
    parts = {}

    def scatter(**blocks):
        return list(blocks), (list(blocks.values()), True, None)

    (dh3, dh3_b, dh4_b, dgate, dup, act, d_ffn1), _ = _ffn_bwd_x(
        dh4, h3, ffn_g[1:2], gate1, up1, wg1, wu1, wd1, "ffn1_bwd")
    dwg1 = _matmul_tn(n3, dgate, "ffn1_dwg")
    dwu1 = _matmul_tn(n3, dup, "ffn1_dwu")
    dwd1 = _matmul_tn(act, dh4_b, "ffn1_dwd")
    dbuc = _matmul_nt(dh3_b, c_out, "conv_out_bwd")
    d_c_out = _matmul_tn(bu, dh3_b, "conv_dwout")
    db, dc, dxp, small["conv_w"] = _conv_bwd(dbuc, bcx, conv_w)
    dbcx = jnp.concatenate([db, dc, dxp], axis=1)
    d_c_in = _matmul_tn(n2, dbcx, "conv_dwin")
    (dh2, dh2_b, small["conv_norm"]), _ = _nt_rmsbwd(dbcx, c_in, h2, conv_norm, dh3, "conv_in_bwd")
    names, carry = scatter(c_in=_Cols(d_c_in), c_out=_shards_from_rows(d_c_out))
    (dh1, dh1_b, dh2_b2, dgate, dup, act, d_ffn0), got = _ffn_bwd_x(
        dh2, h1, ffn_g[0:1], gate0, up0, wg0, wu0, wd0, "ffn0_bwd", carry=carry)
    parts.update(zip(names, got))
    dwg0 = _matmul_tn(n1, dgate, "ffn0_dwg")
    dwu0 = _matmul_tn(n1, dup, "ffn0_dwu")
    dwd0 = _matmul_tn(act, dh2_b2, "ffn0_dwd")
    small["ffn_norm"] = jnp.pad(d_ffn0, ((0, 7), (0, 0))) + jnp.pad(d_ffn1, ((1, 6), (0, 0)))
    do = _matmul_nt(dh1_b, w_o, "mla_out_bwd")
    d_w_o = _matmul_tn(o, dh1_b, "mla_dwo")
    names, carry = scatter(wg0=_Cols(dwg0), wg1=_Cols(dwg1), wu0=_Cols(dwu0), wu1=_Cols(dwu1),
                           wd0=_shards_from_rows(dwd0), wd1=_shards_from_rows(dwd1))
    (dq, dk, dv), got = _attn_bwd(q, k, v, do, o, lse, cc, sn, sp, carry=carry)
    parts.update(zip(names, got))
    d_w_uq = _matmul_tn(cqn, dq, "mla_dwuq")
    names, carry = scatter(w_o=_shards_from_rows(d_w_o), w_uq=_Cols(d_w_uq))
    (dproj, dkv, small["g_cq"], small["g_ckv"]), got = _mla_mid_bwd(
        dq, dk, dv, proj, w_uq, w_ukv, gains["g_cq"], gains["g_ckv"], cc, sn, sp, carry=carry)
    parts.update(zip(names, got))
    d_w_ukv = _matmul_tn(ckvn, dkv, "mla_dwukv")
    names, carry = scatter(w_ukv=_Cols(d_w_ukv))
    d_w_in, got = _matmul_tn(n0, dproj, "mla_dwin", carry=carry)
    parts.update(zip(names, got))
    (dx, _, small["mla_norm"]), _ = _nt_rmsbwd(dproj, w_in, x, gains["mla_norm"], dh1, "mla_in_bwd")
    return loss, dx, parts, _shards_from_rows(d_w_in), small


def kernel(x, positions, mla_norm, mla_w_in, mla_g_cq, mla_g_ckv, mla_w_uq, mla_w_ukv, mla_w_o, conv_norm, conv_w_in, conv_w, conv_w_out, ffn_norm, ffn_w_gate, ffn_w_up, ffn_w_down, final_norm, loss_target, m_mla_norm, m_mla_w_in, m_mla_g_cq, m_mla_g_ckv, m_mla_w_uq, m_mla_w_ukv, m_mla_w_o, m_conv_norm, m_conv_w_in, m_conv_w, m_conv_w_out, m_ffn_norm, m_ffn_w_gate, m_ffn_w_up, m_ffn_w_down, m_final_norm, v_mla_norm, v_mla_w_in, v_mla_g_cq, v_mla_g_ckv, v_mla_w_uq, v_mla_w_ukv, v_mla_w_o, v_conv_norm, v_conv_w_in, v_conv_w, v_conv_w_out, v_ffn_norm, v_ffn_w_gate, v_ffn_w_up, v_ffn_w_down, v_final_norm):
    me = 4 * lax.axis_index("x") + 2 * lax.axis_index("y") + lax.axis_index("c")

    bf = lambda a, rows, cols: _pad_to(a.astype(BF16), rows, cols)
    shards = dict(
        w_in=bf(mla_w_in[0], D_MODEL // N_DEV, PROJ_PAD), w_uq=bf(mla_w_uq[0], CQ, HEAD_PAD),
        w_ukv=mla_w_ukv[0].astype(BF16), w_o=mla_w_o[0].astype(BF16),
        c_in=conv_w_in[0].astype(BF16), c_out=conv_w_out[0].astype(BF16),
        conv_norm=conv_norm, conv_w=conv_w[0])
    for l in range(2):
        shards.update({f"wg{l}": bf(ffn_w_gate[l], D_MODEL, FF_SHARD_PAD), f"wu{l}": bf(ffn_w_up[l], D_MODEL, FF_SHARD_PAD),
                       f"wd{l}": bf(ffn_w_down[l], FF_SHARD_PAD, D_MODEL)})
    gains = dict(mla_norm=mla_norm, g_cq=mla_g_cq, g_ckv=mla_g_ckv, ffn_norm=ffn_norm,
                 final_norm=final_norm.reshape(1, -1))
    loss_local, dx, parts, d_w_in, grads = _forward_backward(
        x[0], loss_target[0], _rope_tables(positions[0]), gains, shards)

    col0 = me * (D_MODEL // N_DEV)

    def place(shard):
        return lax.dynamic_update_slice(jnp.zeros((shard.shape[0], D_MODEL), F32), shard, (0, col0))

    no_loss = jnp.zeros((8, D_MODEL), F32)

    def pack(mla_n, g_cq, g_ckv, ffn_n, fin_n, conv_n, conv_taps, loss_tile):
        rows = [mla_n, _pad_row(g_cq), _pad_row(g_ckv), ffn_n, fin_n.reshape(1, -1), conv_n, conv_taps, loss_tile]
        assert all(r.shape[0] in (n, 8) for r, n in zip(rows, SMALL_PIECES))
        return jnp.concatenate([jnp.pad(r, ((0, 8 - r.shape[0]), (0, 0))) for r in rows], axis=0)

    gpack = pack(grads["mla_norm"], grads["g_cq"], grads["g_ckv"], grads["ffn_norm"], grads["final_norm"],
                 grads["conv_norm"], grads["conv_w"], loss_local)
    wpack = pack(mla_norm, mla_g_cq, mla_g_ckv, ffn_norm, final_norm, place(conv_norm), place(conv_w[0]), no_loss)
    mpack = pack(m_mla_norm, m_mla_g_cq, m_mla_g_ckv, m_ffn_norm, m_final_norm, place(m_conv_norm),
                 place(m_conv_w[0]), no_loss)
    vpack = pack(v_mla_norm, v_mla_g_cq, v_mla_g_ckv, v_ffn_norm, v_final_norm, place(v_conv_norm),
                 place(v_conv_w[0]), no_loss)
    small, (parts["w_in"],) = _small_allreduce_adamw(gpack, wpack, mpack, vpack, [d_w_in])
    loss = small[0][56, 0]

    def adamw(name, w, m, v, partials):
        if w.ndim == 2:
            w, m, v = w[None], m[None], v[None]
        return [o.reshape(w.shape) for o in _adamw(partials, w, m, v, "adamw_" + name)]

    res = dict(
        w_in=adamw("w_in", mla_w_in, m_mla_w_in, v_mla_w_in, [parts["w_in"][:, :, :PROJ]]),
        w_uq=adamw("w_uq", mla_w_uq, m_mla_w_uq, v_mla_w_uq, [parts["w_uq"][:, :, :NOPE + ROPE]]),
        w_ukv=adamw("w_ukv", mla_w_ukv, m_mla_w_ukv, v_mla_w_ukv, [parts["w_ukv"]]),
        w_o=adamw("w_o", mla_w_o, m_mla_w_o, v_mla_w_o, [parts["w_o"]]),
        c_in=adamw("c_in", conv_w_in, m_conv_w_in, v_conv_w_in, [parts["c_in"]]),
        c_out=adamw("c_out", conv_w_out, m_conv_w_out, v_conv_w_out, [parts["c_out"]]),
        wg=adamw("wg", ffn_w_gate, m_ffn_w_gate, v_ffn_w_gate, [parts[n][:, :, :FF_SHARD] for n in ("wg0", "wg1")]),
        wu=adamw("wu", ffn_w_up, m_ffn_w_up, v_ffn_w_up, [parts[n][:, :, :FF_SHARD] for n in ("wu0", "wu1")]),
        wd=adamw("wd", ffn_w_down, m_ffn_w_down, v_ffn_w_down, [parts["wd0"], parts["wd1"]]),
    )

    def unpack(p):
        own = lambda rows: lax.dynamic_slice(rows, (0, col0), (rows.shape[0], D_MODEL // N_DEV))
        return dict(mla_norm=p[0:1], g_cq=p[8:9, :CQ], g_ckv=p[16:17, :CKV], ffn_norm=p[24:26], final_norm=p[32],
                    conv_norm=own(p[40:41]), conv_w=own(p[48:51])[None])

    small = [unpack(p) for p in small]
    order = ["mla_norm", "w_in", "g_cq", "g_ckv", "w_uq", "w_ukv", "w_o", "conv_norm", "c_in", "conv_w", "c_out",
             "ffn_norm", "wg", "wu", "wd", "final_norm"]
    out = [loss, dx[None]]
    for kind in range(4):
        for n in order:
            out.append(res[n][kind] if n in res else small[kind][n])
    return tuple(out)
```

```python
import math

import jax
import jax.numpy as jnp
from jax import lax
from jax.experimental import pallas as pl
from jax.experimental.pallas import tpu as pltpu

F32 = jnp.float32
BF16 = jnp.bfloat16

N_DEV = 8
D_MODEL = 1024
N_HEADS = 8
NOPE = 128
ROPE = 64
V_DIM = 128
HEAD_PAD = 256
CQ = 512
CKV = 256
PROJ = CQ + CKV + ROPE
PROJ_PAD = CQ + CKV + 128
D_FF = 2816
FF_SHARD = D_FF // N_DEV
FF_SHARD_PAD = 384
FF_PAD = FF_SHARD_PAD * N_DEV
CHUNK_SHIFT = 6
RMS_EPS = 1e-6
ROPE_THETA = 10000.0
ATT_SCALE = 1.0 / math.sqrt(NOPE + ROPE)
LOG2_E = math.log2(math.e)
LN_2 = math.log(2.0)
Q_SCALE = ATT_SCALE * LOG2_E
NEG = -1e30

ADAM_LR = 0.001
ADAM_B1 = 0.9
ADAM_B2 = 0.999
ADAM_EPS = 1e-08
ADAM_WD = 0.01
ADAM_STEP = 10

SMALL_PIECES = (1, 1, 1, 2, 1, 1, 3, 1)
SMALL_ROWS = 16

_NT = (((1,), (1,)), ((), ()))
_TN = (((0,), (0,)), ((), ()))


def _pc(body, *, name, out_shape, grid=(), in_specs=None, out_specs=None, scratch_shapes=(), vmem_mb=None):
    params = {}
    if vmem_mb is not None:
        params["vmem_limit_bytes"] = vmem_mb << 20
    kwargs = dict(
        name=name, out_shape=out_shape, grid=grid, scratch_shapes=scratch_shapes,
        compiler_params=pltpu.CompilerParams(**params),
    )
    if in_specs is not None:
        kwargs["in_specs"] = in_specs
    if out_specs is not None:
        kwargs["out_specs"] = out_specs
    return pl.pallas_call(body, **kwargs)


def _pc_carrying(body, carry, operands, *, name, out_shape, grid, in_specs, out_specs, scratch_shapes=()):
    if carry is None:
        return _pc(body, name=name, out_shape=out_shape, grid=grid, in_specs=in_specs, out_specs=out_specs,
                   scratch_shapes=scratch_shapes)(*operands), None
    arrays, scatter, relay_at = carry
    nw, n_in, n_out, n_scr = len(arrays), len(in_specs), len(out_shape), len(scratch_shapes)
    start, relay, wait, landed_shapes, sems, arrays = _exchange_ops(arrays, scatter)
    n_steps = math.prod(grid)
    relay_step = None if relay is None else max(1, min(int(relay_at * n_steps), n_steps - 2))

    def wrapped(*refs):
        ins, rest = refs[:n_in], refs[n_in:]
        cin, rest = rest[:nw], rest[nw:]
        outs, rest = rest[:n_out], rest[n_out:]
        cout, rest = rest[:nw], rest[nw:]
        scr, csem = rest[:n_scr], rest[n_scr:]
        step = pl.program_id(0)
        for a in range(1, len(grid)):
            step = step * grid[a] + pl.program_id(a)

        @pl.when(step == 0)
        def _():
            start(cin, cout, *csem)

        if relay is not None:
            @pl.when(step == relay_step)
            def _():
                relay(cin, cout, *csem)

        body(*ins, *outs, *scr)

        @pl.when(step == n_steps - 1)
        def _():
            wait(cin, cout, *csem)

    any_spec = pl.BlockSpec(memory_space=pl.ANY)
    res = _pc(
        wrapped, name=name, grid=grid,
        in_specs=list(in_specs) + [any_spec] * nw, out_specs=list(out_specs) + [any_spec] * nw,
        out_shape=list(out_shape) + landed_shapes, scratch_shapes=list(scratch_shapes) + sems,
    )(*operands, *arrays)
    return res[:n_out], res[n_out:]


def _sds(shape, dtype):
    return jax.ShapeDtypeStruct(shape, dtype)


def _dot(a, b):
    return jnp.dot(a, b, preferred_element_type=F32)


def _dot_nt(a, b):
    return lax.dot_general(a, b, _NT, preferred_element_type=F32)


def _dot_tn(a, b):
    return lax.dot_general(a, b, _TN, preferred_element_type=F32)


def _rstd(x):
    return lax.rsqrt(jnp.mean(x * x, axis=-1, keepdims=True) + RMS_EPS)


def _rms(x, g):
    return (x * _rstd(x)) * g


def _rms_bwd(dy, x, g):
    r = _rstd(x)
    xhat = x * r
    dxhat = dy * g
    dx = r * (dxhat - xhat * jnp.mean(dxhat * xhat, axis=-1, keepdims=True))
    return dx, jnp.sum(dy * xhat, axis=0, keepdims=True)


def _rope(t, cc, sn, sp):
    return t * cc + pltpu.roll(t, 96, 1) * sn + pltpu.roll(t, 32, 1) * sp


def _rope_bwd(dt, cc, sn, sp):
    return dt * cc + pltpu.roll(dt * sn, 32, 1) + pltpu.roll(dt * sp, 96, 1)


def _row_block(s):
    return min(512, s)


_TN_ROWS = 1024
_FFN_ROWS = 1024


def _mla_in_fwd(x, g0, w_in, g_cq, g_ckv, cc, sn, sp, carry=None):
    s = x.shape[0]
    tm = _row_block(s)

    def body(x_ref, g0_ref, w_ref, gcq_ref, gckv_ref, cc_ref, sn_ref, sp_ref,
             n_ref, proj_ref, cqn_ref, ckvn_ref, krr_ref):
        nb = _rms(x_ref[...], g0_ref[...]).astype(BF16)
        n_ref[...] = nb
        proj = _dot(nb, w_ref[...])
        proj_ref[...] = proj
        cqn_ref[...] = _rms(proj[:, :CQ], gcq_ref[...]).astype(BF16)
        ckvn_ref[...] = _rms(proj[:, CQ:CQ + CKV], gckv_ref[...]).astype(BF16)
        krr_ref[...] = _rope(proj[:, CQ + CKV:], cc_ref[...], sn_ref[...], sp_ref[...]).astype(BF16)

    row = lambda n: pl.BlockSpec((tm, n), lambda i: (i, 0))
    full = lambda a: pl.BlockSpec(a.shape, lambda i: (0, 0))
    return _pc_carrying(
        body, carry, (x, g0, w_in, g_cq, g_ckv, cc, sn, sp), name="mla_in_fwd", grid=(s // tm,),
        in_specs=[row(D_MODEL), full(g0), full(w_in), full(g_cq), full(g_ckv), row(128), row(128), row(128)],
        out_specs=[row(D_MODEL), row(PROJ_PAD), row(CQ), row(CKV), row(128)],
        out_shape=[_sds((s, D_MODEL), BF16), _sds((s, PROJ_PAD), F32), _sds((s, CQ), BF16),
                   _sds((s, CKV), BF16), _sds((s, 128), BF16)],
    )


def _qkv_proj(cqn, ckvn, krr, w_uq, w_ukv, cc, sn, sp, carry=None):
    s = cqn.shape[0]
    tm = _row_block(s)

    def body(cqn_ref, ckvn_ref, krr_ref, wuq_ref, wukv_ref, cc_ref, sn_ref, sp_ref, q_ref, k_ref, v_ref):
        cqn_b, ckvn_b, krr_b = cqn_ref[...], ckvn_ref[...], krr_ref[...]
        cc_b, sn_b, sp_b = cc_ref[...], sn_ref[...], sp_ref[...]
        for h in range(N_HEADS):
            q = _dot(cqn_b, wuq_ref[h]) * Q_SCALE
            q_ref[h, :, :NOPE] = q[:, :NOPE].astype(BF16)
            q_ref[h, :, NOPE:] = _rope(q[:, NOPE:], cc_b, sn_b, sp_b).astype(BF16)
            kv = _dot(ckvn_b, wukv_ref[h])
            k_ref[h, :, :NOPE] = kv[:, :NOPE].astype(BF16)
            k_ref[h, :, NOPE:] = krr_b
            v_ref[h] = kv[:, NOPE:].astype(BF16)

    row = lambda n: pl.BlockSpec((tm, n), lambda i: (i, 0))
    head_w = lambda a: pl.BlockSpec(a.shape, lambda i: (0, 0, 0))
    head_o = lambda n: pl.BlockSpec((N_HEADS, tm, n), lambda i: (0, i, 0))
    return _pc_carrying(
        body, carry, (cqn, ckvn, krr, w_uq, w_ukv, cc, sn, sp), name="qkv_proj", grid=(s // tm,),
        in_specs=[row(CQ), row(CKV), row(128), head_w(w_uq), head_w(w_ukv), row(128), row(128), row(128)],
        out_specs=[head_o(HEAD_PAD), head_o(HEAD_PAD), head_o(V_DIM)],
        out_shape=[_sds((N_HEADS, s, HEAD_PAD), BF16), _sds((N_HEADS, s, HEAD_PAD), BF16),
                   _sds((N_HEADS, s, V_DIM), BF16)],
    )


def _chunk_bias(bq, bk):
    rows = lax.broadcasted_iota(jnp.int32, (bq, bk), 0)
    cols = lax.broadcasted_iota(jnp.int32, (bq, bk), 1)
    visible = jnp.right_shift(cols, CHUNK_SHIFT) <= jnp.right_shift(rows, CHUNK_SHIFT)
    return jnp.where(visible, 0.0, NEG).astype(F32)


def _attn_fwd(q, k, v, carry=None):
    s = q.shape[1]
    bq = _row_block(s)

    def body(q_ref, k_ref, v_ref, o_ref, lse_ref, bias_ref):
        i = pl.program_id(1)
        qb = q_ref[0]

        @pl.when((pl.program_id(0) == 0) & (i == 0))
        def _():
            bias_ref[...] = _chunk_bias(bq, bq)

        def step(j, carry, masked):
            m, l, acc = carry
            start = pl.multiple_of(j * bq, bq)
            kb = k_ref[0, pl.ds(start, bq), :]
            vb = v_ref[0, pl.ds(start, bq), :]
            sc = _dot_nt(qb, kb)
            if masked:
                sc = sc + bias_ref[...]
            m_new = jnp.maximum(m, jnp.max(sc, axis=-1, keepdims=True))
            p = jnp.exp2(sc - m_new)
            alpha = jnp.exp2(m - m_new)
            l = alpha * l + jnp.sum(p, axis=-1, keepdims=True)
            acc = alpha * acc + _dot(p.astype(BF16), vb)
            return m_new, l, acc

        init = (jnp.full((bq, 1), NEG, F32), jnp.zeros((bq, 1), F32), jnp.zeros((bq, V_DIM), F32))
        carry = lax.fori_loop(0, i, lambda j, c: step(j, c, False), init)
        m, l, acc = step(i, carry, True)
        o_ref[...] = (acc / l).astype(BF16)
        lse_ref[0] = jnp.broadcast_to(m + jnp.log(l) * LOG2_E, (bq, 128))

    return _pc_carrying(
        body, carry, (q, k, v), name="attn_fwd", grid=(N_HEADS, s // bq),
        in_specs=[pl.BlockSpec((1, bq, HEAD_PAD), lambda h, i: (h, i, 0)),
                  pl.BlockSpec((1, s, HEAD_PAD), lambda h, i: (h, 0, 0)),
                  pl.BlockSpec((1, s, V_DIM), lambda h, i: (h, 0, 0))],
        out_specs=[pl.BlockSpec((bq, V_DIM), lambda h, i: (i, h)),
                   pl.BlockSpec((1, bq, 128), lambda h, i: (h, i, 0))],
        out_shape=[_sds((s, N_HEADS * V_DIM), BF16), _sds((N_HEADS, s, 128), F32)],
        scratch_shapes=[pltpu.VMEM((bq, bq), F32)],
    )


def _matmul_res(a, w, res, name):
    s, kd = a.shape
    n = w.shape[1]
    tm = _row_block(s)

    def body(a_ref, w_ref, r_ref, o_ref):
        o_ref[...] = r_ref[...] + _dot(a_ref[...], w_ref[...])

    return _pc(
        body, name=name, grid=(s // tm,),
        in_specs=[pl.BlockSpec((tm, kd), lambda i: (i, 0)), pl.BlockSpec((kd, n), lambda i: (0, 0)),
                  pl.BlockSpec((tm, n), lambda i: (i, 0))],
        out_specs=pl.BlockSpec((tm, n), lambda i: (i, 0)),
        out_shape=_sds((s, n), F32),
    )(a, w, res)


def _ffn_fwd(h, gain, wg, wu, wd, name, carry=None):
    s = h.shape[0]
    tm = min(_FFN_ROWS, s)
    tf = 512
    nf = FF_PAD // tf

    def body(h_ref, g_ref, wg_ref, wu_ref, wd_ref, o_ref, n_ref, gate_ref, up_ref, acc_ref):
        j = pl.program_id(1)

        @pl.when(j == 0)
        def _():
            n_ref[...] = _rms(h_ref[...], g_ref[...]).astype(BF16)
            acc_ref[...] = jnp.zeros_like(acc_ref)

        nb = n_ref[...]
        gate = _dot(nb, wg_ref[...])
        up = _dot(nb, wu_ref[...])
        gate_ref[...] = gate.astype(BF16)
        up_ref[...] = up.astype(BF16)
        act = gate * jax.nn.sigmoid(gate) * up
        acc_ref[...] += _dot(act.astype(BF16), wd_ref[...])

        @pl.when(j == nf - 1)
        def _():
            o_ref[...] = h_ref[...] + acc_ref[...]

    return _pc_carrying(
        body, carry, (h, gain, wg, wu, wd), name=name, grid=(s // tm, nf),
        in_specs=[pl.BlockSpec((tm, D_MODEL), lambda i, j: (i, 0)), pl.BlockSpec((1, D_MODEL), lambda i, j: (0, 0)),
                  pl.BlockSpec((D_MODEL, tf), lambda i, j: (0, j)), pl.BlockSpec((D_MODEL, tf), lambda i, j: (0, j)),
                  pl.BlockSpec((tf, D_MODEL), lambda i, j: (j, 0))],
        out_specs=[pl.BlockSpec((tm, D_MODEL), lambda i, j: (i, 0)), pl.BlockSpec((tm, D_MODEL), lambda i, j: (i, 0)),
                   pl.BlockSpec((tm, tf), lambda i, j: (i, j)), pl.BlockSpec((tm, tf), lambda i, j: (i, j))],
        out_shape=[_sds((s, D_MODEL), F32), _sds((s, D_MODEL), BF16), _sds((s, FF_PAD), BF16),
                   _sds((s, FF_PAD), BF16)],
        scratch_shapes=[pltpu.VMEM((tm, D_MODEL), F32)],
    )


def _rms_matmul(h, gain, w, name):
    s = h.shape[0]
    n = w.shape[1]
    tm = _row_block(s)
    tn = 1024
    nn = n // tn

    def body(h_ref, g_ref, w_ref, n_ref, o_ref):
        @pl.when(pl.program_id(1) == 0)
        def _():
            n_ref[...] = _rms(h_ref[...], g_ref[...]).astype(BF16)

        o_ref[...] = _dot(n_ref[...], w_ref[...]).astype(BF16)

    return _pc(
        body, name=name, grid=(s // tm, nn),
        in_specs=[pl.BlockSpec((tm, D_MODEL), lambda i, j: (i, 0)), pl.BlockSpec((1, D_MODEL), lambda i, j: (0, 0)),
                  pl.BlockSpec((D_MODEL, tn), lambda i, j: (0, j))],
        out_specs=[pl.BlockSpec((tm, D_MODEL), lambda i, j: (i, 0)), pl.BlockSpec((tm, tn), lambda i, j: (i, j))],
        out_shape=[_sds((s, D_MODEL), BF16), _sds((s, n), BF16)],
    )(h, gain, w)


def _shift_down(u, k, rows):
    return jnp.where(rows >= k, pltpu.roll(u, k, 0), 0.0)


def _shift_up(u, k, rows, s):
    return jnp.where(rows < s - k, pltpu.roll(u, s - k, 0), 0.0)


_CONV_COLS = 128


def _conv_fwd(bcx, cw):
    s = bcx.shape[0]
    tc = _CONV_COLS
    nc = D_MODEL // tc

    def body(b_ref, c_ref, xp_ref, w_ref, o_ref):
        rows = lax.broadcasted_iota(jnp.int32, (s, tc), 0)
        u = c_ref[...].astype(F32) * xp_ref[...].astype(F32)
        w = w_ref[...]
        uc = w[2:3] * u + w[1:2] * _shift_down(u, 1, rows) + w[0:1] * _shift_down(u, 2, rows)
        o_ref[...] = (b_ref[...].astype(F32) * uc).astype(BF16)

    col = lambda off: pl.BlockSpec((s, tc), lambda j: (0, off + j))
    return _pc(
        body, name="conv_fwd", grid=(nc,),
        in_specs=[col(0), col(nc), col(2 * nc), pl.BlockSpec((3, tc), lambda j: (0, j))],
        out_specs=pl.BlockSpec((s, tc), lambda j: (0, j)),
        out_shape=_sds((s, D_MODEL), BF16),
    )(bcx, bcx, bcx, cw)


def _final_loss(h, gain, target):
    s = h.shape[0]
    tm = _row_block(s)

    def body(h_ref, g_ref, t_ref, dh_ref, dg_ref, loss_ref):
        i = pl.program_id(0)

        @pl.when(i == 0)
        def _():
            dg_ref[...] = jnp.zeros_like(dg_ref)
            loss_ref[...] = jnp.zeros_like(loss_ref)

        hb = h_ref[...]
        e = _rms(hb, g_ref[...]) - t_ref[...]
        loss_ref[...] += 0.5 * jnp.sum(jnp.mean(e * e, axis=-1, keepdims=True))
        dx, dg = _rms_bwd(e * (1.0 / D_MODEL), hb, g_ref[...])
        dh_ref[...] = dx
        dg_ref[...] += dg

    row = pl.BlockSpec((tm, D_MODEL), lambda i: (i, 0))
    vec = pl.BlockSpec((1, D_MODEL), lambda i: (0, 0))
    return _pc(
        body, name="final_loss", grid=(s // tm,),
        in_specs=[row, vec, row],
        out_specs=[row, vec, pl.BlockSpec((8, D_MODEL), lambda i: (0, 0))],
        out_shape=[_sds((s, D_MODEL), F32), _sds((1, D_MODEL), F32), _sds((8, D_MODEL), F32)],
    )(h, gain, target)


def _ffn_bwd_x(dho, h, gain, gate, up, wg, wu, wd, name, carry=None):
    s = h.shape[0]
    tm = _row_block(s)
    tf = 512
    nf = FF_PAD // tf

    def body(dho_ref, h_ref, g_ref, gate_ref, up_ref, wg_ref, wu_ref, wd_ref,
             dh_ref, dhb_ref, dhob_ref, dgate_ref, dup_ref, act_ref, dgain_ref, acc_ref):
        i = pl.program_id(0)
        j = pl.program_id(1)

        @pl.when(j == 0)
        def _():
            dhob_ref[...] = dho_ref[...].astype(BF16)
            acc_ref[...] = jnp.zeros_like(acc_ref)

        @pl.when((i == 0) & (j == 0))
        def _():
            dgain_ref[...] = jnp.zeros_like(dgain_ref)

        dact = _dot_nt(dhob_ref[...], wd_ref[...])
        g = gate_ref[...].astype(F32)
        u = up_ref[...].astype(F32)
        sg = jax.nn.sigmoid(g)
        silu = g * sg
        dg = (dact * u * (sg * (1.0 + g * (1.0 - sg)))).astype(BF16)
        du = (dact * silu).astype(BF16)
        dgate_ref[...] = dg
        dup_ref[...] = du
        act_ref[...] = (silu * u).astype(BF16)
        acc_ref[...] += _dot_nt(dg, wg_ref[...]) + _dot_nt(du, wu_ref[...])

        @pl.when(j == nf - 1)
        def _():
            dx, dgain = _rms_bwd(acc_ref[...], h_ref[...], g_ref[...])
            dh = dho_ref[...] + dx
            dh_ref[...] = dh
            dhb_ref[...] = dh.astype(BF16)
            dgain_ref[...] += dgain

    row = pl.BlockSpec((tm, D_MODEL), lambda i, j: (i, 0))
    vec = pl.BlockSpec((1, D_MODEL), lambda i, j: (0, 0))
    hid = pl.BlockSpec((tm, tf), lambda i, j: (i, j))
    wcol = pl.BlockSpec((D_MODEL, tf), lambda i, j: (0, j))
    wrow = pl.BlockSpec((tf, D_MODEL), lambda i, j: (j, 0))
    return _pc_carrying(
        body, carry, (dho, h, gain, gate, up, wg, wu, wd), name=name, grid=(s // tm, nf),
        in_specs=[row, row, vec, hid, hid, wcol, wcol, wrow],
        out_specs=[row, row, row, hid, hid, hid, vec],
        out_shape=[_sds((s, D_MODEL), F32), _sds((s, D_MODEL), BF16), _sds((s, D_MODEL), BF16),
                   _sds((s, FF_PAD), BF16), _sds((s, FF_PAD), BF16), _sds((s, FF_PAD), BF16),
                   _sds((1, D_MODEL), F32)],
        scratch_shapes=[pltpu.VMEM((tm, D_MODEL), F32)],
    )


def _nt_rmsbwd(a, w, h, gain, dho, name, carry=None):
    stacked = a.ndim == 3
    if stacked:
        nk, s, tk = a.shape
    else:
        s, tk = a.shape
        nk = 1
    tm = _row_block(s)

    def body(a_ref, w_ref, h_ref, g_ref, dho_ref, dh_ref, dhb_ref, dgain_ref, acc_ref):
        i = pl.program_id(0)
        j = pl.program_id(1)
        if stacked:
            a_ref = a_ref.at[0]

        @pl.when(j == 0)
        def _():
            acc_ref[...] = jnp.zeros_like(acc_ref)

        @pl.when((i == 0) & (j == 0))
        def _():
            dgain_ref[...] = jnp.zeros_like(dgain_ref)

        acc_ref[...] += _dot_nt(a_ref[...], w_ref[...])

        @pl.when(j == nk - 1)
        def _():
            dx, dgain = _rms_bwd(acc_ref[...], h_ref[...], g_ref[...])
            dh = dho_ref[...] + dx
            dh_ref[...] = dh
            dhb_ref[...] = dh.astype(BF16)
            dgain_ref[...] += dgain

    row = pl.BlockSpec((tm, D_MODEL), lambda i, j: (i, 0))
    vec = pl.BlockSpec((1, D_MODEL), lambda i, j: (0, 0))
    return _pc_carrying(
        body, carry, (a, w, h, gain, dho), name=name, grid=(s // tm, nk),
        in_specs=[pl.BlockSpec((1, tm, tk), lambda i, j: (j, i, 0)) if stacked else pl.BlockSpec((tm, tk), lambda i, j: (i, 0)),
                  pl.BlockSpec((D_MODEL, tk), lambda i, j: (0, j)), row, vec, row],
        out_specs=[row, row, vec],
        out_shape=[_sds((s, D_MODEL), F32), _sds((s, D_MODEL), BF16), _sds((1, D_MODEL), F32)],
        scratch_shapes=[pltpu.VMEM((tm, D_MODEL), F32)],
    )


def _matmul_nt(a, w, name):
    s, kd = a.shape
    n = w.shape[0]
    tm = _row_block(s)

    def body(a_ref, w_ref, o_ref):
        o_ref[...] = _dot_nt(a_ref[...], w_ref[...]).astype(BF16)

    return _pc(
        body, name=name, grid=(s // tm,),
        in_specs=[pl.BlockSpec((tm, kd), lambda i: (i, 0)), pl.BlockSpec((n, kd), lambda i: (0, 0))],
        out_specs=pl.BlockSpec((tm, n), lambda i: (i, 0)),
        out_shape=_sds((s, n), BF16),
    )(a, w)


def _matmul_tn(a, b, name, carry=None):
    s, m = a.shape
    stacked = b.ndim == 3
    tmm = min(m, 1024)
    if stacked:
        n, tn = b.shape[0] * b.shape[2], b.shape[2]
    else:
        n = b.shape[1]
        tn = n if n <= 1024 else 1024
    tk = min(_TN_ROWS, s)
    nk = s // tk

    def body(a_ref, b_ref, o_ref, acc_ref):
        k = pl.program_id(2)
        if stacked:
            b_ref = b_ref.at[0]

        @pl.when(k == 0)
        def _():
            acc_ref[...] = jnp.zeros_like(acc_ref)

        acc_ref[...] += _dot_tn(a_ref[...], b_ref[...])

        @pl.when(k == nk - 1)
        def _():
            o_ref[...] = acc_ref[...].astype(BF16)

    (out,), got = _pc_carrying(
        body, carry, (a, b), name=name, grid=(m // tmm, n // tn, nk),
        in_specs=[pl.BlockSpec((tk, tmm), lambda i, j, k: (k, i)),
                  pl.BlockSpec((1, tk, tn), lambda i, j, k: (j, k, 0)) if stacked
                  else pl.BlockSpec((tk, tn), lambda i, j, k: (k, j))],
        out_specs=[pl.BlockSpec((tmm, tn), lambda i, j, k: (i, j))],
        out_shape=[_sds((m, n), BF16)],
        scratch_shapes=[pltpu.VMEM((tmm, tn), F32)],
    )
    return out if carry is None else (out, got)


def _conv_bwd(dbuc, bcx, cw):
    s = bcx.shape[0]
    tc = _CONV_COLS
    nc = D_MODEL // tc

    def body(d_ref, b_ref, c_ref, xp_ref, w_ref, dbcx_ref, dw_ref):
        rows = lax.broadcasted_iota(jnp.int32, (s, tc), 0)
        c = c_ref[...].astype(F32)
        xp = xp_ref[...].astype(F32)
        u = c * xp
        u1 = _shift_down(u, 1, rows)
        u2 = _shift_down(u, 2, rows)
        w = w_ref[...]
        uc = w[2:3] * u + w[1:2] * u1 + w[0:1] * u2
        d = d_ref[...].astype(F32)
        dbcx_ref[0] = (d * uc).astype(BF16)
        duc = d * b_ref[...].astype(F32)
        du = w[2:3] * duc + w[1:2] * _shift_up(duc, 1, rows, s) + w[0:1] * _shift_up(duc, 2, rows, s)
        dbcx_ref[1] = (du * xp).astype(BF16)
        dbcx_ref[2] = (du * c).astype(BF16)
        dw_ref[0:1, :] = jnp.sum(duc * u2, axis=0, keepdims=True)
        dw_ref[1:2, :] = jnp.sum(duc * u1, axis=0, keepdims=True)
        dw_ref[2:3, :] = jnp.sum(duc * u, axis=0, keepdims=True)

    col = lambda off: pl.BlockSpec((s, tc), lambda j: (0, off + j))
    tap = pl.BlockSpec((3, tc), lambda j: (0, j))
    return _pc(
        body, name="conv_bwd", grid=(nc,),
        in_specs=[col(0), col(0), col(nc), col(2 * nc), tap],
        out_specs=[pl.BlockSpec((3, s, tc), lambda j: (0, 0, j)), tap],
        out_shape=[_sds((3, s, D_MODEL), BF16), _sds((3, D_MODEL), F32)],
    )(dbuc, bcx, bcx, bcx, cw)


def _attn_bwd(q, k, v, do, o, lse, cc, sn, sp, carry=None):
    s = q.shape[1]
    bk = _row_block(s)
    nb = s // bk

    def body(q_ref, k_ref, v_ref, do_ref, o_ref, lse_ref, cc_ref, sn_ref, sp_ref,
             dq_ref, dk_ref, dv_ref, dqacc_ref, delta_ref, bias_ref):
        j = pl.program_id(1)
        kb = k_ref[0]
        vb = v_ref[0]

        @pl.when((pl.program_id(0) == 0) & (j == 0))
        def _():
            bias_ref[...] = _chunk_bias(bk, bk)

        @pl.when(j == 0)
        def _():
            dqacc_ref[...] = jnp.zeros_like(dqacc_ref)

            def fill(i, _):
                rows = pl.ds(pl.multiple_of(i * bk, bk), bk)
                d = jnp.sum(do_ref[rows, :].astype(F32) * o_ref[rows, :].astype(F32), axis=-1, keepdims=True)
                delta_ref[rows, :] = jnp.broadcast_to(d, (bk, 128))
                return 0

            lax.fori_loop(0, nb, fill, 0)

        def step(i, carry, masked):
            dk, dv = carry
            rows = pl.ds(pl.multiple_of(i * bk, bk), bk)
            qb = q_ref[0, rows, :]
            dob = do_ref[rows, :]
            sc = _dot_nt(qb, kb)
            if masked:
                sc = sc + bias_ref[...]
            p = jnp.exp2(sc - lse_ref[0, rows, :][:, :1])
            dv = dv + _dot_tn(p.astype(BF16), dob)
            ds = (p * (_dot_nt(dob, vb) - delta_ref[rows, :][:, :1])).astype(BF16)
            dk = dk + _dot_tn(ds, qb)
            dqacc_ref[rows, :] += _dot(ds, kb)
            return dk, dv

        carry = step(j, (jnp.zeros((bk, HEAD_PAD), F32), jnp.zeros((bk, V_DIM), F32)), True)
        dq = dqacc_ref[pl.ds(pl.multiple_of(j * bk, bk), bk), :] * ATT_SCALE
        dq_ref[:, :NOPE] = dq[:, :NOPE].astype(BF16)
        dq_ref[:, NOPE:] = _rope_bwd(dq[:, NOPE:], cc_ref[...], sn_ref[...], sp_ref[...]).astype(BF16)
        dk, dv = lax.fori_loop(j + 1, nb, lambda i, c: step(i, c, False), carry)
        dk_ref[0] = (dk * LN_2).astype(BF16)
        dv_ref[0] = dv.astype(BF16)

    blk = lambda n: pl.BlockSpec((1, bk, n), lambda h, j: (h, j, 0))
    whole = lambda n: pl.BlockSpec((1, s, n), lambda h, j: (h, 0, 0))
    cols = pl.BlockSpec((s, V_DIM), lambda h, j: (0, h))
    tab = pl.BlockSpec((bk, 128), lambda h, j: (j, 0))
    return _pc_carrying(
        body, carry, (q, k, v, do, o, lse, cc, sn, sp), name="attn_bwd", grid=(N_HEADS, nb),
        in_specs=[whole(HEAD_PAD), blk(HEAD_PAD), blk(V_DIM), cols, cols, whole(128), tab, tab, tab],
        out_specs=[pl.BlockSpec((bk, HEAD_PAD), lambda h, j: (j, h)), blk(HEAD_PAD), blk(V_DIM)],
        out_shape=[_sds((s, N_HEADS * HEAD_PAD), BF16), _sds((N_HEADS, s, HEAD_PAD), BF16),
                   _sds((N_HEADS, s, V_DIM), BF16)],
        scratch_shapes=[pltpu.VMEM((s, HEAD_PAD), F32), pltpu.VMEM((s, 128), F32), pltpu.VMEM((bk, bk), F32)],
    )


def _mla_mid_bwd(dq, dk, dv, proj, w_uq, w_ukv, g_cq, g_ckv, cc, sn, sp, carry=None):
    s = dq.shape[0]
    tm = _row_block(s)

    def body(dq_ref, dk_ref, dv_ref, proj_ref, wuq_ref, wukv_ref, gcq_ref, gckv_ref, cc_ref, sn_ref, sp_ref,
             dproj_ref, dkv_ref, dgcq_ref, dgckv_ref, acq_ref, ackv_ref, akr_ref):
        @pl.when(pl.program_id(0) == 0)
        def _():
            dgcq_ref[...] = jnp.zeros_like(dgcq_ref)
            dgckv_ref[...] = jnp.zeros_like(dgckv_ref)

        for h in range(N_HEADS):
            cols = slice(h * HEAD_PAD, (h + 1) * HEAD_PAD)
            dkb = dk_ref[h]
            dkv_ref[:, h * HEAD_PAD:h * HEAD_PAD + NOPE] = dkb[:, :NOPE]
            dkv_ref[:, h * HEAD_PAD + NOPE:(h + 1) * HEAD_PAD] = dv_ref[h]
            cq_part = _dot_nt(dq_ref[:, cols], wuq_ref[h])
            ckv_part = _dot_nt(dkv_ref[:, cols], wukv_ref[h])
            kr_part = dkb[:, NOPE:].astype(F32)
            if h == 0:
                acq_ref[...], ackv_ref[...], akr_ref[...] = cq_part, ckv_part, kr_part
            else:
                acq_ref[...] += cq_part
                ackv_ref[...] += ckv_part
                akr_ref[...] += kr_part

        proj = proj_ref[...]
        dcq, dgcq = _rms_bwd(acq_ref[...], proj[:, :CQ], gcq_ref[...])
        dckv, dgckv = _rms_bwd(ackv_ref[...], proj[:, CQ:CQ + CKV], gckv_ref[...])
        dproj_ref[:, :CQ] = dcq.astype(BF16)
        dproj_ref[:, CQ:CQ + CKV] = dckv.astype(BF16)
        dproj_ref[:, CQ + CKV:] = _rope_bwd(akr_ref[...], cc_ref[...], sn_ref[...], sp_ref[...]).astype(BF16)
        dgcq_ref[...] += dgcq
        dgckv_ref[...] += dgckv

    head_blk = lambda n: pl.BlockSpec((N_HEADS, tm, n), lambda i: (0, i, 0))
    head_cols = pl.BlockSpec((tm, N_HEADS * HEAD_PAD), lambda i: (i, 0))
    head_w = lambda a: pl.BlockSpec(a.shape, lambda i: (0, 0, 0))
    row = lambda n: pl.BlockSpec((tm, n), lambda i: (i, 0))
    vec = lambda n: pl.BlockSpec((1, n), lambda i: (0, 0))
    return _pc_carrying(
        body, carry, (dq, dk, dv, proj, w_uq, w_ukv, g_cq, g_ckv, cc, sn, sp),
        name="mla_mid_bwd", grid=(s // tm,),
        in_specs=[head_cols, head_blk(HEAD_PAD), head_blk(V_DIM), row(PROJ_PAD), head_w(w_uq), head_w(w_ukv),
                  vec(CQ), vec(CKV), row(128), row(128), row(128)],
        out_specs=[row(PROJ_PAD), head_cols, vec(CQ), vec(CKV)],
        out_shape=[_sds((s, PROJ_PAD), BF16), _sds((s, N_HEADS * HEAD_PAD), BF16), _sds((1, CQ), F32),
                   _sds((1, CKV), F32)],
        scratch_shapes=[pltpu.VMEM((tm, CQ), F32), pltpu.VMEM((tm, CKV), F32), pltpu.VMEM((tm, 128), F32)],
    )


def _peer(k):
    x, y, c = lax.axis_index("x"), lax.axis_index("y"), lax.axis_index("c")
    px = 1 - x if k & 4 else x
    py = 1 - y if k & 2 else y
    pc = 1 - c if k & 1 else c
    return (px, py, pc), 4 * px + 2 * py + pc


def _exchange_ops(arrays, scatter):
    nw = len(arrays)
    by_cols = [isinstance(a, _Cols) for a in arrays]
    arrays = [a.array if isinstance(a, _Cols) else a for a in arrays]
    direct = range(1, N_DEV) if scatter else (1, 2, 4, 6)

    def columns(ref, idx, width):
        return ref.at[:, pl.ds(pl.multiple_of(idx * width, 128), width)]

    def sent(ins, w, idx):
        if not scatter:
            return ins[w]
        return columns(ins[w], idx, arrays[w].shape[1] // N_DEV) if by_cols[w] else ins[w].at[idx]

    def slot(outs, w, idx):
        if by_cols[w] and not scatter:
            return columns(outs[w], idx, arrays[w].shape[1])
        return outs[w].at[idx]

    def copy(w, k, src, dst, to, send_sems, recv_sems):
        return pltpu.make_async_remote_copy(
            src_ref=src, dst_ref=dst, send_sem=send_sems.at[w * N_DEV + k], recv_sem=recv_sems.at[w * N_DEV + k],
            device_id=to, device_id_type=pl.DeviceIdType.MESH)

    def own_copies(ins, outs, local_sems):
        _, me = _peer(0)
        return [pltpu.make_async_copy(sent(ins, w, me), slot(outs, w, me), local_sems.at[w]) for w in range(nw)]

    def sends(ins, outs, send_sems, recv_sems):
        _, me = _peer(0)
        out = []
        for k in direct:
            dev, idx = _peer(k)
            for w in range(nw):
                out.append(copy(w, k, sent(ins, w, idx), slot(outs, w, me), dev, send_sems, recv_sems))
        return out

    def relays(outs, send_sems, recv_sems):
        sibling, _ = _peer(1)
        out = []
        for k in (2, 4, 6):
            _, idx = _peer(k)
            for w in range(nw):
                out.append(copy(w, k + 1, slot(outs, w, idx), slot(outs, w, idx), sibling, send_sems, recv_sems))
        return out

    def arrival(outs, w, k, send_sems, recv_sems):
        dev, idx = _peer(k)
        return copy(w, k, slot(outs, w, idx), slot(outs, w, idx), dev, send_sems, recv_sems)

    def start(ins, outs, send_sems, recv_sems, local_sems):
        for cp in own_copies(ins, outs, local_sems) + sends(ins, outs, send_sems, recv_sems):
            cp.start()

    def relay(ins, outs, send_sems, recv_sems, local_sems):
        for k in (2, 4, 6):
            for w in range(nw):
                arrival(outs, w, k, send_sems, recv_sems).wait_recv()
        for cp in relays(outs, send_sems, recv_sems):
            cp.start()

    def wait(ins, outs, send_sems, recv_sems, local_sems):
        for cp in own_copies(ins, outs, local_sems):
            cp.wait()
        for cp in sends(ins, outs, send_sems, recv_sems) + ([] if scatter else relays(outs, send_sems, recv_sems)):
            cp.wait_send()
        for k in (range(1, N_DEV) if scatter else (1, 3, 5, 7)):
            for w in range(nw):
                arrival(outs, w, k, send_sems, recv_sems).wait_recv()

    def landed_shape(a, cols):
        if not cols:
            return a.shape if scatter else (N_DEV,) + a.shape
        r, c = a.shape
        return (N_DEV, r, c // N_DEV) if scatter else (r, N_DEV * c)

    landed = [_sds(landed_shape(a, cols), a.dtype) for a, cols in zip(arrays, by_cols)]
    sems = [pltpu.SemaphoreType.DMA((nw * N_DEV,)), pltpu.SemaphoreType.DMA((nw * N_DEV,)),
            pltpu.SemaphoreType.DMA((nw,))]
    return start, (None if scatter else relay), wait, landed, sems, arrays


class _Cols:
    def __init__(self, array):
        self.array = array


def _exchange(arrays, scatter, name):
    nw = len(arrays)
    start, relay, wait, landed, sems, arrays = _exchange_ops(arrays, scatter)

    def body(*refs):
        ins, outs, csem = refs[:nw], refs[nw:2 * nw], refs[2 * nw:]
        start(ins, outs, *csem)
        if relay is not None:
            relay(ins, outs, *csem)
        wait(ins, outs, *csem)

    any_spec = pl.BlockSpec(memory_space=pl.ANY)
    return _pc(body, name=name, in_specs=[any_spec] * nw, out_specs=[any_spec] * nw, out_shape=landed,
               scratch_shapes=sems)(*arrays)


def _adam_math(g, w, m, v):
    m = ADAM_B1 * m + (1.0 - ADAM_B1) * g
    v = ADAM_B2 * v + (1.0 - ADAM_B2) * jnp.square(g)
    m_hat = m / (1.0 - ADAM_B1 ** ADAM_STEP)
    v_hat = v / (1.0 - ADAM_B2 ** ADAM_STEP)
    delta = -ADAM_LR * (m_hat / (jnp.sqrt(v_hat) + ADAM_EPS) + ADAM_WD * w)
    return delta, m, v


def _adam_rows(r):
    for t in (256, 128, 64, 32, 16):
        if r % t == 0:
            return t
    return r


def _adamw(parts, w, m, v, name):
    nl, r, c = w.shape
    tr = _adam_rows(r)
    nr = r // tr

    def body(*refs):
        p_refs, (w_ref, m_ref, v_ref, g_ref, d_ref, mo_ref, vo_ref) = refs[:nl], refs[nl:]
        for layer in range(nl):
            @pl.when(pl.program_id(0) == layer)
            def _(p_ref=p_refs[layer]):
                g = p_ref[0].astype(F32)
                for src in range(1, N_DEV):
                    g = g + p_ref[src].astype(F32)
                delta, m2, v2 = _adam_math(g, w_ref[0], m_ref[0], v_ref[0])
                g_ref[0] = g
                d_ref[0] = delta
                mo_ref[0] = m2
                vo_ref[0] = v2

    def part_spec(layer):
        def index(l, i):
            return 0, jnp.where(l == layer, i, jnp.where(l < layer, 0, nr - 1)), 0
        return pl.BlockSpec((N_DEV, tr, c), index)

    blk = pl.BlockSpec((1, tr, c), lambda l, i: (l, i, 0))
    return _pc(
        body, name=name, grid=(nl, nr),
        in_specs=[part_spec(layer) for layer in range(nl)] + [blk, blk, blk],
        out_specs=[blk, blk, blk, blk],
        out_shape=[_sds((nl, r, c), F32)] * 4,
    )(*parts, w, m, v)


def _small_allreduce_adamw(gpack, wpack, mpack, vpack, last_grads):
    shape = gpack.shape
    compact = (SMALL_ROWS, D_MODEL)
    nw = len(last_grads)
    start, _, wait, landed, exchange_sems, last_grads = _exchange_ops(last_grads, True)

    def body(*refs):
        (g_ref, w_ref, m_ref, v_ref), rest = refs[:4], refs[4:]
        cin, rest = rest[:nw], rest[nw:]
        (go_ref, d_ref, mo_ref, vo_ref), rest = rest[:4], rest[4:]
        cout, rest = rest[:nw], rest[nw:]
        (comp_ref, gath_ref, send_sems, recv_sems), csem = rest[:4], rest[4:]
        start(cin, cout, *csem)
        _, me = _peer(0)
        comp_ref[...] = jnp.zeros(compact, F32)
        r = 0
        for p, n in enumerate(SMALL_PIECES):
            comp_ref[r:r + n, :] = g_ref[8 * p:8 * p + n, :]
            r += n
        gath_ref[me] = comp_ref[...]
        copies = []
        for k in range(1, N_DEV):
            dev, idx = _peer(k)
            copies.append(pltpu.make_async_remote_copy(
                src_ref=comp_ref, dst_ref=gath_ref.at[me], send_sem=send_sems.at[k], recv_sem=recv_sems.at[k],
                device_id=dev, device_id_type=pl.DeviceIdType.MESH))
        for cp in copies:
            cp.start()
        for cp in copies:
            cp.wait_send()
        for k in range(1, N_DEV):
            dev, idx = _peer(k)
            pltpu.make_async_remote_copy(
                src_ref=comp_ref, dst_ref=gath_ref.at[idx], send_sem=send_sems.at[k], recv_sem=recv_sems.at[k],
                device_id=dev, device_id_type=pl.DeviceIdType.MESH).wait_recv()
        g = gath_ref[0]
        for src in range(1, N_DEV):
            g = g + gath_ref[src]
        go_ref[...] = jnp.zeros(shape, F32)
        r = 0
        for p, n in enumerate(SMALL_PIECES):
            go_ref[8 * p:8 * p + n, :] = g[r:r + n, :]
            r += n
        delta, m2, v2 = _adam_math(go_ref[...], w_ref[...], m_ref[...], v_ref[...])
        d_ref[...] = delta
        mo_ref[...] = m2
        vo_ref[...] = v2
        wait(cin, cout, *csem)

    vm = pl.BlockSpec(memory_space=pltpu.VMEM)
    any_spec = pl.BlockSpec(memory_space=pl.ANY)
    res = _pc(
        body, name="small_allreduce_adamw",
        in_specs=[vm] * 4 + [any_spec] * nw, out_specs=[vm] * 4 + [any_spec] * nw,
        out_shape=[_sds(shape, F32)] * 4 + landed,
        scratch_shapes=[pltpu.VMEM(compact, F32), pltpu.VMEM((N_DEV,) + compact, F32),
                        pltpu.SemaphoreType.DMA((N_DEV,)), pltpu.SemaphoreType.DMA((N_DEV,))] + exchange_sems,
    )(gpack, wpack, mpack, vpack, *last_grads)
    return res[:4], res[4:]


def _rows_from_shards(g):
    return g.reshape(N_DEV * g.shape[1], g.shape[2])


def _shards_from_rows(a):
    return a.reshape(N_DEV, a.shape[0] // N_DEV, a.shape[1])


def _pad_to(a, rows, cols):
    return jnp.pad(a, ((0, rows - a.shape[0]), (0, cols - a.shape[1])))


def _rope_tables(pos):
    inv_freq = 1.0 / (ROPE_THETA ** (jnp.arange(0, ROPE, 2, dtype=F32) / ROPE))
    ang = pos.astype(F32)[:, None] * inv_freq
    cos, sin = jnp.cos(ang), jnp.sin(ang)
    z32, z64, z96 = (jnp.zeros((pos.shape[0], n), F32) for n in (32, 64, 96))
    return (jnp.concatenate([cos, cos, z64], axis=1), jnp.concatenate([-sin, z96], axis=1),
            jnp.concatenate([z32, sin, z64], axis=1))


def _pad_row(vec):
    vec = vec.reshape(1, -1)
    return jnp.pad(vec, ((0, 0), (0, D_MODEL - vec.shape[1])))


def _forward_backward(x, target, tables, gains, shards):
    cc, sn, sp = tables
    ffn_g = gains["ffn_norm"]
    by_cols = ("wg0", "wu0", "wg1", "wu1", "c_in")

    def gather(names, relay_at):
        return [_Cols(shards[n]) if n in by_cols else shards[n] for n in names], False, relay_at

    (got,) = _exchange([shards["w_in"]], False, "gather_w_in")
    w_in = _rows_from_shards(got)
    (n0, proj, cqn, ckvn, krr), (w_uq, w_ukv) = _mla_in_fwd(
        x, gains["mla_norm"], w_in, gains["g_cq"], gains["g_ckv"], cc, sn, sp, carry=gather(["w_uq", "w_ukv"], 0.5))
    (q, k, v), (w_o, conv_norm, conv_w) = _qkv_proj(
        cqn, ckvn, krr, w_uq, w_ukv, cc, sn, sp, carry=gather(["w_o", "conv_norm", "conv_w"], 0.5))
    w_o = _rows_from_shards(w_o)
    conv_norm = conv_norm.reshape(1, D_MODEL)
    conv_w = jnp.transpose(conv_w, (1, 0, 2)).reshape(3, D_MODEL)
    (o, lse), (wg0, wu0, wd0, c_in, c_out) = _attn_fwd(
        q, k, v, carry=gather(["wg0", "wu0", "wd0", "c_in", "c_out"], 0.65))
    wd0 = _rows_from_shards(wd0)
    c_out = _rows_from_shards(c_out)
    h1 = _matmul_res(o, w_o, x, "mla_out_fwd")
    (h2, n1, gate0, up0), (wg1, wu1, wd1) = _ffn_fwd(
        h1, ffn_g[0:1], wg0, wu0, wd0, "ffn0_fwd", carry=gather(["wg1", "wu1", "wd1"], 0.85))
    wd1 = _rows_from_shards(wd1)
    n2, bcx = _rms_matmul(h2, conv_norm, c_in, "conv_in_fwd")
    bu = _conv_fwd(bcx, conv_w)
    h3 = _matmul_res(bu, c_out, h2, "conv_out_fwd")
    (h4, n3, gate1, up1), _ = _ffn_fwd(h3, ffn_g[1:2], wg1, wu1, wd1, "ffn1_fwd")
    dh4, d_final, loss = _final_loss(h4, gains["final_norm"], target)

    small = {"final_norm": d_final}
    parts = {}

    def scatter(**blocks):
        return list(blocks), (list(blocks.values()), True, None)

    (dh3, dh3_b, dh4_b, dgate, dup, act, d_ffn1), _ = _ffn_bwd_x(
        dh4, h3, ffn_g[1:2], gate1, up1, wg1, wu1, wd1, "ffn1_bwd")
    dwg1 = _matmul_tn(n3, dgate, "ffn1_dwg")
    dwu1 = _matmul_tn(n3, dup, "ffn1_dwu")
    dwd1 = _matmul_tn(act, dh4_b, "ffn1_dwd")
    dbuc = _matmul_nt(dh3_b, c_out, "conv_out_bwd")
    d_c_out = _matmul_tn(bu, dh3_b, "conv_dwout")
    dbcx, small["conv_w"] = _conv_bwd(dbuc, bcx, conv_w)
    d_c_in = _matmul_tn(n2, dbcx, "conv_dwin")
    (dh2, dh2_b, small["conv_norm"]), _ = _nt_rmsbwd(dbcx, c_in, h2, conv_norm, dh3, "conv_in_bwd")
    names, carry = scatter(c_in=_Cols(d_c_in), c_out=_shards_from_rows(d_c_out))
    (dh1, dh1_b, dh2_b2, dgate, dup, act, d_ffn0), got = _ffn_bwd_x(
        dh2, h1, ffn_g[0:1], gate0, up0, wg0, wu0, wd0, "ffn0_bwd", carry=carry)
    parts.update(zip(names, got))
    dwg0 = _matmul_tn(n1, dgate, "ffn0_dwg")
    dwu0 = _matmul_tn(n1, dup, "ffn0_dwu")
    dwd0 = _matmul_tn(act, dh2_b2, "ffn0_dwd")
    small["ffn_norm"] = jnp.pad(d_ffn0, ((0, 7), (0, 0))) + jnp.pad(d_ffn1, ((1, 6), (0, 0)))
    do = _matmul_nt(dh1_b, w_o, "mla_out_bwd")
    d_w_o = _matmul_tn(o, dh1_b, "mla_dwo")
    names, carry = scatter(wg0=_Cols(dwg0), wg1=_Cols(dwg1), wu0=_Cols(dwu0), wu1=_Cols(dwu1),
                           wd0=_shards_from_rows(dwd0), wd1=_shards_from_rows(dwd1))
    (dq, dk, dv), got = _attn_bwd(q, k, v, do, o, lse, cc, sn, sp, carry=carry)
    parts.update(zip(names, got))
    d_w_uq = _matmul_tn(cqn, dq, "mla_dwuq")
    names, carry = scatter(w_o=_shards_from_rows(d_w_o), w_uq=_Cols(d_w_uq))
    (dproj, dkv, small["g_cq"], small["g_ckv"]), got = _mla_mid_bwd(
        dq, dk, dv, proj, w_uq, w_ukv, gains["g_cq"], gains["g_ckv"], cc, sn, sp, carry=carry)
    parts.update(zip(names, got))
    d_w_ukv = _matmul_tn(ckvn, dkv, "mla_dwukv")
    names, carry = scatter(w_ukv=_Cols(d_w_ukv))
    d_w_in, got = _matmul_tn(n0, dproj, "mla_dwin", carry=carry)
    parts.update(zip(names, got))
    (dx, _, small["mla_norm"]), _ = _nt_rmsbwd(dproj, w_in, x, gains["mla_norm"], dh1, "mla_in_bwd")
    return loss, dx, parts, _shards_from_rows(d_w_in), small


def kernel(x, positions, mla_norm, mla_w_in, mla_g_cq, mla_g_ckv, mla_w_uq, mla_w_ukv, mla_w_o, conv_norm, conv_w_in, conv_w, conv_w_out, ffn_norm, ffn_w_gate, ffn_w_up, ffn_w_down, final_norm, loss_target, m_mla_norm, m_mla_w_in, m_mla_g_cq, m_mla_g_ckv, m_mla_w_uq, m_mla_w_ukv, m_mla_w_o, m_conv_norm, m_conv_w_in, m_conv_w, m_conv_w_out, m_ffn_norm, m_ffn_w_gate, m_ffn_w_up, m_ffn_w_down, m_final_norm, v_mla_norm, v_mla_w_in, v_mla_g_cq, v_mla_g_ckv, v_mla_w_uq, v_mla_w_ukv, v_mla_w_o, v_conv_norm, v_conv_w_in, v_conv_w, v_conv_w_out, v_ffn_norm, v_ffn_w_gate, v_ffn_w_up, v_ffn_w_down, v_final_norm):
    me = 4 * lax.axis_index("x") + 2 * lax.axis_index("y") + lax.axis_index("c")

    bf = lambda a, rows, cols: _pad_to(a.astype(BF16), rows, cols)
    shards = dict(
        w_in=bf(mla_w_in[0], D_MODEL // N_DEV, PROJ_PAD), w_uq=bf(mla_w_uq[0], CQ, HEAD_PAD),
        w_ukv=mla_w_ukv[0].astype(BF16), w_o=mla_w_o[0].astype(BF16),
        c_in=conv_w_in[0].astype(BF16), c_out=conv_w_out[0].astype(BF16),
        conv_norm=conv_norm, conv_w=conv_w[0])
    for l in range(2):
        shards.update({f"wg{l}": bf(ffn_w_gate[l], D_MODEL, FF_SHARD_PAD), f"wu{l}": bf(ffn_w_up[l], D_MODEL, FF_SHARD_PAD),
                       f"wd{l}": bf(ffn_w_down[l], FF_SHARD_PAD, D_MODEL)})
    gains = dict(mla_norm=mla_norm, g_cq=mla_g_cq, g_ckv=mla_g_ckv, ffn_norm=ffn_norm,
                 final_norm=final_norm.reshape(1, -1))
    loss_local, dx, parts, d_w_in, grads = _forward_backward(
        x[0], loss_target[0], _rope_tables(positions[0]), gains, shards)

    col0 = me * (D_MODEL // N_DEV)

    def place(shard):
        return lax.dynamic_update_slice(jnp.zeros((shard.shape[0], D_MODEL), F32), shard, (0, col0))

    no_loss = jnp.zeros((8, D_MODEL), F32)

    def pack(mla_n, g_cq, g_ckv, ffn_n, fin_n, conv_n, conv_taps, loss_tile):
        rows = [mla_n, _pad_row(g_cq), _pad_row(g_ckv), ffn_n, fin_n.reshape(1, -1), conv_n, conv_taps, loss_tile]
        assert all(r.shape[0] in (n, 8) for r, n in zip(rows, SMALL_PIECES))
        return jnp.concatenate([jnp.pad(r, ((0, 8 - r.shape[0]), (0, 0))) for r in rows], axis=0)

    gpack = pack(grads["mla_norm"], grads["g_cq"], grads["g_ckv"], grads["ffn_norm"], grads["final_norm"],
                 grads["conv_norm"], grads["conv_w"], loss_local)
    wpack = pack(mla_norm, mla_g_cq, mla_g_ckv, ffn_norm, final_norm, place(conv_norm), place(conv_w[0]), no_loss)
    mpack = pack(m_mla_norm, m_mla_g_cq, m_mla_g_ckv, m_ffn_norm, m_final_norm, place(m_conv_norm),
                 place(m_conv_w[0]), no_loss)
    vpack = pack(v_mla_norm, v_mla_g_cq, v_mla_g_ckv, v_ffn_norm, v_final_norm, place(v_conv_norm),
                 place(v_conv_w[0]), no_loss)
    small, (parts["w_in"],) = _small_allreduce_adamw(gpack, wpack, mpack, vpack, [d_w_in])
    loss = small[0][56, 0]

    def adamw(name, w, m, v, partials):
        if w.ndim == 2:
            w, m, v = w[None], m[None], v[None]
        return [o.reshape(w.shape) for o in _adamw(partials, w, m, v, "adamw_" + name)]

    res = dict(
        w_in=adamw("w_in", mla_w_in, m_mla_w_in, v_mla_w_in, [parts["w_in"][:, :, :PROJ]]),
        w_uq=adamw("w_uq", mla_w_uq, m_mla_w_uq, v_mla_w_uq, [parts["w_uq"][:, :, :NOPE + ROPE]]),
        w_ukv=adamw("w_ukv", mla_w_ukv, m_mla_w_ukv, v_mla_w_ukv, [parts["w_ukv"]]),
        w_o=adamw("w_o", mla_w_o, m_mla_w_o, v_mla_w_o, [parts["w_o"]]),
        c_in=adamw("c_in", conv_w_in, m_conv_w_in, v_conv_w_in, [parts["c_in"]]),
        c_out=adamw("c_out", conv_w_out, m_conv_w_out, v_conv_w_out, [parts["c_out"]]),
        wg=adamw("wg", ffn_w_gate, m_ffn_w_gate, v_ffn_w_gate, [parts[n][:, :, :FF_SHARD] for n in ("wg0", "wg1")]),
        wu=adamw("wu", ffn_w_up, m_ffn_w_up, v_ffn_w_up, [parts[n][:, :, :FF_SHARD] for n in ("wu0", "wu1")]),
        wd=adamw("wd", ffn_w_down, m_ffn_w_down, v_ffn_w_down, [parts["wd0"], parts["wd1"]]),
    )

    def unpack(p):
        own = lambda rows: lax.dynamic_slice(rows, (0, col0), (rows.shape[0], D_MODEL // N_DEV))
        return dict(mla_norm=p[0:1], g_cq=p[8:9, :CQ], g_ckv=p[16:17, :CKV], ffn_norm=p[24:26], final_norm=p[32],
                    conv_norm=own(p[40:41]), conv_w=own(p[48:51])[None])

    small = [unpack(p) for p in small]
    order = ["mla_norm", "w_in", "g_cq", "g_ckv", "w_uq", "w_ukv", "w_o", "conv_norm", "c_in", "conv_w", "c_out",
             "ffn_norm", "wg", "wu", "wd", "final_norm"]
    out = [loss, dx[None]]
    for kind in range(4):
        for n in order:
            out.append(res[n][kind] if n in res else small[kind][n])
    return tuple(out)
```

```python
import math

import jax
import jax.numpy as jnp
from jax import lax
from jax.experimental import pallas as pl
from jax.experimental.pallas import tpu as pltpu

F32 = jnp.float32
BF16 = jnp.bfloat16

N_DEV = 8
D_MODEL = 1024
N_HEADS = 8
NOPE = 128
ROPE = 64
V_DIM = 128
HEAD_PAD = 256
CQ = 512
CKV = 256
PROJ = CQ + CKV + ROPE
PROJ_PAD = CQ + CKV + 128
D_FF = 2816
FF_SHARD = D_FF // N_DEV
FF_SHARD_PAD = 384
FF_PAD = FF_SHARD_PAD * N_DEV
CHUNK_SHIFT = 6
RMS_EPS = 1e-6
ROPE_THETA = 10000.0
ATT_SCALE = 1.0 / math.sqrt(NOPE + ROPE)
LOG2_E = math.log2(math.e)
LN_2 = math.log(2.0)
Q_SCALE = ATT_SCALE * LOG2_E
NEG = -1e30

ADAM_LR = 0.001
ADAM_B1 = 0.9
ADAM_B2 = 0.999
ADAM_EPS = 1e-08
ADAM_WD = 0.01
ADAM_STEP = 10

SMALL_PIECES = (1, 1, 1, 2, 1, 1, 3, 1)
SMALL_ROWS = 16

_NT = (((1,), (1,)), ((), ()))
_TN = (((0,), (0,)), ((), ()))


def _pc(body, *, name, out_shape, grid=(), in_specs=None, out_specs=None, scratch_shapes=(), vmem_mb=None):
    params = {}
    if vmem_mb is not None:
        params["vmem_limit_bytes"] = vmem_mb << 20
    kwargs = dict(
        name=name, out_shape=out_shape, grid=grid, scratch_shapes=scratch_shapes,
        compiler_params=pltpu.CompilerParams(**params),
    )
    if in_specs is not None:
        kwargs["in_specs"] = in_specs
    if out_specs is not None:
        kwargs["out_specs"] = out_specs
    return pl.pallas_call(body, **kwargs)


def _pc_carrying(body, carry, operands, *, name, out_shape, grid, in_specs, out_specs, scratch_shapes=()):
    if carry is None:
        return _pc(body, name=name, out_shape=out_shape, grid=grid, in_specs=in_specs, out_specs=out_specs,
                   scratch_shapes=scratch_shapes)(*operands), None
    arrays, scatter, relay_at = carry
    nw, n_in, n_out, n_scr = len(arrays), len(in_specs), len(out_shape), len(scratch_shapes)
    start, relay, wait, landed_shapes, sems, arrays = _exchange_ops(arrays, scatter)
    n_steps = math.prod(grid)
    relay_step = None if relay is None else max(1, min(int(relay_at * n_steps), n_steps - 2))

    def wrapped(*refs):
        ins, rest = refs[:n_in], refs[n_in:]
        cin, rest = rest[:nw], rest[nw:]
        outs, rest = rest[:n_out], rest[n_out:]
        cout, rest = rest[:nw], rest[nw:]
        scr, csem = rest[:n_scr], rest[n_scr:]
        step = pl.program_id(0)
        for a in range(1, len(grid)):
            step = step * grid[a] + pl.program_id(a)

        @pl.when(step == 0)
        def _():
            start(cin, cout, *csem)

        if relay is not None:
            @pl.when(step == relay_step)
            def _():
                relay(cin, cout, *csem)

        body(*ins, *outs, *scr)

        @pl.when(step == n_steps - 1)
        def _():
            wait(cin, cout, *csem)

    any_spec = pl.BlockSpec(memory_space=pl.ANY)
    res = _pc(
        wrapped, name=name, grid=grid,
        in_specs=list(in_specs) + [any_spec] * nw, out_specs=list(out_specs) + [any_spec] * nw,
        out_shape=list(out_shape) + landed_shapes, scratch_shapes=list(scratch_shapes) + sems,
    )(*operands, *arrays)
    return res[:n_out], res[n_out:]


def _sds(shape, dtype):
    return jax.ShapeDtypeStruct(shape, dtype)


def _dot(a, b):
    return jnp.dot(a, b, preferred_element_type=F32)


def _dot_nt(a, b):
    return lax.dot_general(a, b, _NT, preferred_element_type=F32)


def _dot_tn(a, b):
    return lax.dot_general(a, b, _TN, preferred_element_type=F32)


def _rstd(x):
    return lax.rsqrt(jnp.mean(x * x, axis=-1, keepdims=True) + RMS_EPS)


def _rms(x, g):
    return (x * _rstd(x)) * g


def _rms_bwd(dy, x, g):
    r = _rstd(x)
    xhat = x * r
    dxhat = dy * g
    dx = r * (dxhat - xhat * jnp.mean(dxhat * xhat, axis=-1, keepdims=True))
    return dx, jnp.sum(dy * xhat, axis=0, keepdims=True)


def _rope(t, cc, sn, sp):
    return t * cc + pltpu.roll(t, 96, 1) * sn + pltpu.roll(t, 32, 1) * sp


def _rope_bwd(dt, cc, sn, sp):
    return dt * cc + pltpu.roll(dt * sn, 32, 1) + pltpu.roll(dt * sp, 96, 1)


def _row_block(s):
    return min(512, s)


_TN_ROWS = 1024
_FFN_ROWS = 1024


def _mla_in_fwd(x, g0, w_in, g_cq, g_ckv, cc, sn, sp, carry=None):
    s = x.shape[0]
    tm = _row_block(s)

    def body(x_ref, g0_ref, w_ref, gcq_ref, gckv_ref, cc_ref, sn_ref, sp_ref,
             n_ref, proj_ref, cqn_ref, ckvn_ref, krr_ref):
        nb = _rms(x_ref[...], g0_ref[...]).astype(BF16)
        n_ref[...] = nb
        proj = _dot(nb, w_ref[...])
        proj_ref[...] = proj
        cqn_ref[...] = _rms(proj[:, :CQ], gcq_ref[...]).astype(BF16)
        ckvn_ref[...] = _rms(proj[:, CQ:CQ + CKV], gckv_ref[...]).astype(BF16)
        krr_ref[...] = _rope(proj[:, CQ + CKV:], cc_ref[...], sn_ref[...], sp_ref[...]).astype(BF16)

    row = lambda n: pl.BlockSpec((tm, n), lambda i: (i, 0))
    full = lambda a: pl.BlockSpec(a.shape, lambda i: (0, 0))
    return _pc_carrying(
        body, carry, (x, g0, w_in, g_cq, g_ckv, cc, sn, sp), name="mla_in_fwd", grid=(s // tm,),
        in_specs=[row(D_MODEL), full(g0), full(w_in), full(g_cq), full(g_ckv), row(128), row(128), row(128)],
        out_specs=[row(D_MODEL), row(PROJ_PAD), row(CQ), row(CKV), row(128)],
        out_shape=[_sds((s, D_MODEL), BF16), _sds((s, PROJ_PAD), F32), _sds((s, CQ), BF16),
                   _sds((s, CKV), BF16), _sds((s, 128), BF16)],
    )


def _qkv_proj(cqn, ckvn, krr, w_uq, w_ukv, cc, sn, sp, carry=None):
    s = cqn.shape[0]
    tm = _row_block(s)

    def body(cqn_ref, ckvn_ref, krr_ref, wuq_ref, wukv_ref, cc_ref, sn_ref, sp_ref, q_ref, k_ref, v_ref):
        cqn_b, ckvn_b, krr_b = cqn_ref[...], ckvn_ref[...], krr_ref[...]
        cc_b, sn_b, sp_b = cc_ref[...], sn_ref[...], sp_ref[...]
        for h in range(N_HEADS):
            q = _dot(cqn_b, wuq_ref[h]) * Q_SCALE
            q_ref[h, :, :NOPE] = q[:, :NOPE].astype(BF16)
            q_ref[h, :, NOPE:] = _rope(q[:, NOPE:], cc_b, sn_b, sp_b).astype(BF16)
            kv = _dot(ckvn_b, wukv_ref[h])
            k_ref[h, :, :NOPE] = kv[:, :NOPE].astype(BF16)
            k_ref[h, :, NOPE:] = krr_b
            v_ref[h] = kv[:, NOPE:].astype(BF16)

    row = lambda n: pl.BlockSpec((tm, n), lambda i: (i, 0))
    head_w = lambda a: pl.BlockSpec(a.shape, lambda i: (0, 0, 0))
    head_o = lambda n: pl.BlockSpec((N_HEADS, tm, n), lambda i: (0, i, 0))
    return _pc_carrying(
        body, carry, (cqn, ckvn, krr, w_uq, w_ukv, cc, sn, sp), name="qkv_proj", grid=(s // tm,),
        in_specs=[row(CQ), row(CKV), row(128), head_w(w_uq), head_w(w_ukv), row(128), row(128), row(128)],
        out_specs=[head_o(HEAD_PAD), head_o(HEAD_PAD), head_o(V_DIM)],
        out_shape=[_sds((N_HEADS, s, HEAD_PAD), BF16), _sds((N_HEADS, s, HEAD_PAD), BF16),
                   _sds((N_HEADS, s, V_DIM), BF16)],
    )


def _chunk_bias(bq, bk):
    rows = lax.broadcasted_iota(jnp.int32, (bq, bk), 0)
    cols = lax.broadcasted_iota(jnp.int32, (bq, bk), 1)
    visible = jnp.right_shift(cols, CHUNK_SHIFT) <= jnp.right_shift(rows, CHUNK_SHIFT)
    return jnp.where(visible, 0.0, NEG).astype(F32)


def _attn_fwd(q, k, v, carry=None):
    s = q.shape[1]
    bq = _row_block(s)

    def body(q_ref, k_ref, v_ref, o_ref, lse_ref, bias_ref):
        i = pl.program_id(1)
        qb = q_ref[0]

        @pl.when((pl.program_id(0) == 0) & (i == 0))
        def _():
            bias_ref[...] = _chunk_bias(bq, bq)

        def step(j, carry, masked):
            m, l, acc = carry
            start = pl.multiple_of(j * bq, bq)
            kb = k_ref[0, pl.ds(start, bq), :]
            vb = v_ref[0, pl.ds(start, bq), :]
            sc = _dot_nt(qb, kb)
            if masked:
                sc = sc + bias_ref[...]
            m_new = jnp.maximum(m, jnp.max(sc, axis=-1, keepdims=True))
            p = jnp.exp2(sc - m_new)
            alpha = jnp.exp2(m - m_new)
            l = alpha * l + jnp.sum(p, axis=-1, keepdims=True)
            acc = alpha * acc + _dot(p.astype(BF16), vb)
            return m_new, l, acc

        init = (jnp.full((bq, 1), NEG, F32), jnp.zeros((bq, 1), F32), jnp.zeros((bq, V_DIM), F32))
        carry = lax.fori_loop(0, i, lambda j, c: step(j, c, False), init)
        m, l, acc = step(i, carry, True)
        o_ref[...] = (acc / l).astype(BF16)
        lse_ref[0] = jnp.broadcast_to(m + jnp.log(l) * LOG2_E, (bq, 128))

    return _pc_carrying(
        body, carry, (q, k, v), name="attn_fwd", grid=(N_HEADS, s // bq),
        in_specs=[pl.BlockSpec((1, bq, HEAD_PAD), lambda h, i: (h, i, 0)),
                  pl.BlockSpec((1, s, HEAD_PAD), lambda h, i: (h, 0, 0)),
                  pl.BlockSpec((1, s, V_DIM), lambda h, i: (h, 0, 0))],
        out_specs=[pl.BlockSpec((bq, V_DIM), lambda h, i: (i, h)),
                   pl.BlockSpec((1, bq, 128), lambda h, i: (h, i, 0))],
        out_shape=[_sds((s, N_HEADS * V_DIM), BF16), _sds((N_HEADS, s, 128), F32)],
        scratch_shapes=[pltpu.VMEM((bq, bq), F32)],
    )


def _matmul_res(a, w, res, name):
    s, kd = a.shape
    n = w.shape[1]
    tm = _row_block(s)

    def body(a_ref, w_ref, r_ref, o_ref):
        o_ref[...] = r_ref[...] + _dot(a_ref[...], w_ref[...])

    return _pc(
        body, name=name, grid=(s // tm,),
        in_specs=[pl.BlockSpec((tm, kd), lambda i: (i, 0)), pl.BlockSpec((kd, n), lambda i: (0, 0)),
                  pl.BlockSpec((tm, n), lambda i: (i, 0))],
        out_specs=pl.BlockSpec((tm, n), lambda i: (i, 0)),
        out_shape=_sds((s, n), F32),
    )(a, w, res)


def _ffn_fwd(h, gain, wg, wu, wd, name, carry=None):
    s = h.shape[0]
    tm = min(_FFN_ROWS, s)
    tf = 512
    nf = FF_PAD // tf

    def body(h_ref, g_ref, wg_ref, wu_ref, wd_ref, o_ref, n_ref, gate_ref, up_ref, acc_ref):
        j = pl.program_id(1)

        @pl.when(j == 0)
        def _():
            n_ref[...] = _rms(h_ref[...], g_ref[...]).astype(BF16)
            acc_ref[...] = jnp.zeros_like(acc_ref)

        nb = n_ref[...]
        gate = _dot(nb, wg_ref[...])
        up = _dot(nb, wu_ref[...])
        gate_ref[...] = gate.astype(BF16)
        up_ref[...] = up.astype(BF16)
        act = gate * jax.nn.sigmoid(gate) * up
        acc_ref[...] += _dot(act.astype(BF16), wd_ref[...])

        @pl.when(j == nf - 1)
        def _():
            o_ref[...] = h_ref[...] + acc_ref[...]

    return _pc_carrying(
        body, carry, (h, gain, wg, wu, wd), name=name, grid=(s // tm, nf),
        in_specs=[pl.BlockSpec((tm, D_MODEL), lambda i, j: (i, 0)), pl.BlockSpec((1, D_MODEL), lambda i, j: (0, 0)),
                  pl.BlockSpec((D_MODEL, tf), lambda i, j: (0, j)), pl.BlockSpec((D_MODEL, tf), lambda i, j: (0, j)),
                  pl.BlockSpec((tf, D_MODEL), lambda i, j: (j, 0))],
        out_specs=[pl.BlockSpec((tm, D_MODEL), lambda i, j: (i, 0)), pl.BlockSpec((tm, D_MODEL), lambda i, j: (i, 0)),
                   pl.BlockSpec((tm, tf), lambda i, j: (i, j)), pl.BlockSpec((tm, tf), lambda i, j: (i, j))],
        out_shape=[_sds((s, D_MODEL), F32), _sds((s, D_MODEL), BF16), _sds((s, FF_PAD), BF16),
                   _sds((s, FF_PAD), BF16)],
        scratch_shapes=[pltpu.VMEM((tm, D_MODEL), F32)],
    )


def _rms_matmul(h, gain, w, name):
    s = h.shape[0]
    n = w.shape[1]
    tm = min(_FFN_ROWS, s)
    tn = 1024
    nn = n // tn

    def body(h_ref, g_ref, w_ref, n_ref, o_ref):
        @pl.when(pl.program_id(1) == 0)
        def _():
            n_ref[...] = _rms(h_ref[...], g_ref[...]).astype(BF16)

        o_ref[...] = _dot(n_ref[...], w_ref[...]).astype(BF16)

    return _pc(
        body, name=name, grid=(s // tm, nn),
        in_specs=[pl.BlockSpec((tm, D_MODEL), lambda i, j: (i, 0)), pl.BlockSpec((1, D_MODEL), lambda i, j: (0, 0)),
                  pl.BlockSpec((D_MODEL, tn), lambda i, j: (0, j))],
        out_specs=[pl.BlockSpec((tm, D_MODEL), lambda i, j: (i, 0)), pl.BlockSpec((tm, tn), lambda i, j: (i, j))],
        out_shape=[_sds((s, D_MODEL), BF16), _sds((s, n), BF16)],
    )(h, gain, w)


def _shift_down(u, k, rows):
    return jnp.where(rows >= k, pltpu.roll(u, k, 0), 0.0)


def _shift_up(u, k, rows, s):
    return jnp.where(rows < s - k, pltpu.roll(u, s - k, 0), 0.0)


_CONV_COLS = 128


def _conv_fwd(bcx, cw):
    s = bcx.shape[0]
    tc = _CONV_COLS
    nc = D_MODEL // tc

    def body(b_ref, c_ref, xp_ref, w_ref, o_ref):
        rows = lax.broadcasted_iota(jnp.int32, (s, tc), 0)
        u = c_ref[...].astype(F32) * xp_ref[...].astype(F32)
        w = w_ref[...]
        uc = w[2:3] * u + w[1:2] * _shift_down(u, 1, rows) + w[0:1] * _shift_down(u, 2, rows)
        o_ref[...] = (b_ref[...].astype(F32) * uc).astype(BF16)

    col = lambda off: pl.BlockSpec((s, tc), lambda j: (0, off + j))
    return _pc(
        body, name="conv_fwd", grid=(nc,),
        in_specs=[col(0), col(nc), col(2 * nc), pl.BlockSpec((3, tc), lambda j: (0, j))],
        out_specs=pl.BlockSpec((s, tc), lambda j: (0, j)),
        out_shape=_sds((s, D_MODEL), BF16),
    )(bcx, bcx, bcx, cw)


def _final_loss(h, gain, target):
    s = h.shape[0]
    tm = _row_block(s)

    def body(h_ref, g_ref, t_ref, dh_ref, dg_ref, loss_ref):
        i = pl.program_id(0)

        @pl.when(i == 0)
        def _():
            dg_ref[...] = jnp.zeros_like(dg_ref)
            loss_ref[...] = jnp.zeros_like(loss_ref)

        hb = h_ref[...]
        e = _rms(hb, g_ref[...]) - t_ref[...]
        loss_ref[...] += 0.5 * jnp.sum(jnp.mean(e * e, axis=-1, keepdims=True))
        dx, dg = _rms_bwd(e * (1.0 / D_MODEL), hb, g_ref[...])
        dh_ref[...] = dx
        dg_ref[...] += dg

    row = pl.BlockSpec((tm, D_MODEL), lambda i: (i, 0))
    vec = pl.BlockSpec((1, D_MODEL), lambda i: (0, 0))
    return _pc(
        body, name="final_loss", grid=(s // tm,),
        in_specs=[row, vec, row],
        out_specs=[row, vec, pl.BlockSpec((8, D_MODEL), lambda i: (0, 0))],
        out_shape=[_sds((s, D_MODEL), F32), _sds((1, D_MODEL), F32), _sds((8, D_MODEL), F32)],
    )(h, gain, target)


def _ffn_bwd_x(dho, h, gain, gate, up, wg, wu, wd, name, carry=None):
    s = h.shape[0]
    tm = _row_block(s)
    tf = 1024
    nf = FF_PAD // tf

    def body(dho_ref, h_ref, g_ref, gate_ref, up_ref, wg_ref, wu_ref, wd_ref,
             dh_ref, dhb_ref, dhob_ref, dgate_ref, dup_ref, act_ref, dgain_ref, acc_ref):
        i = pl.program_id(0)
        j = pl.program_id(1)

        @pl.when(j == 0)
        def _():
            dhob_ref[...] = dho_ref[...].astype(BF16)
            acc_ref[...] = jnp.zeros_like(acc_ref)

        @pl.when((i == 0) & (j == 0))
        def _():
            dgain_ref[...] = jnp.zeros_like(dgain_ref)

        dact = _dot_nt(dhob_ref[...], wd_ref[...])
        g = gate_ref[...].astype(F32)
        u = up_ref[...].astype(F32)
        sg = jax.nn.sigmoid(g)
        silu = g * sg
        dg = (dact * u * (sg * (1.0 + g * (1.0 - sg)))).astype(BF16)
        du = (dact * silu).astype(BF16)
        dgate_ref[...] = dg
        dup_ref[...] = du
        act_ref[...] = (silu * u).astype(BF16)
        acc_ref[...] += _dot_nt(dg, wg_ref[...]) + _dot_nt(du, wu_ref[...])

        @pl.when(j == nf - 1)
        def _():
            dx, dgain = _rms_bwd(acc_ref[...], h_ref[...], g_ref[...])
            dh = dho_ref[...] + dx
            dh_ref[...] = dh
            dhb_ref[...] = dh.astype(BF16)
            dgain_ref[...] += dgain

    row = pl.BlockSpec((tm, D_MODEL), lambda i, j: (i, 0))
    vec = pl.BlockSpec((1, D_MODEL), lambda i, j: (0, 0))
    hid = pl.BlockSpec((tm, tf), lambda i, j: (i, j))
    wcol = pl.BlockSpec((D_MODEL, tf), lambda i, j: (0, j))
    wrow = pl.BlockSpec((tf, D_MODEL), lambda i, j: (j, 0))
    return _pc_carrying(
        body, carry, (dho, h, gain, gate, up, wg, wu, wd), name=name, grid=(s // tm, nf),
        in_specs=[row, row, vec, hid, hid, wcol, wcol, wrow],
        out_specs=[row, row, row, hid, hid, hid, vec],
        out_shape=[_sds((s, D_MODEL), F32), _sds((s, D_MODEL), BF16), _sds((s, D_MODEL), BF16),
                   _sds((s, FF_PAD), BF16), _sds((s, FF_PAD), BF16), _sds((s, FF_PAD), BF16),
                   _sds((1, D_MODEL), F32)],
        scratch_shapes=[pltpu.VMEM((tm, D_MODEL), F32)],
    )


def _nt_rmsbwd(a, w, h, gain, dho, name, carry=None):
    stacked = a.ndim == 3
    if stacked:
        nk, s, tk = a.shape
    else:
        s, tk = a.shape
        nk = 1
    tm = _row_block(s)

    def body(a_ref, w_ref, h_ref, g_ref, dho_ref, dh_ref, dhb_ref, dgain_ref, acc_ref):
        i = pl.program_id(0)
        j = pl.program_id(1)
        if stacked:
            a_ref = a_ref.at[0]

        @pl.when(j == 0)
        def _():
            acc_ref[...] = jnp.zeros_like(acc_ref)

        @pl.when((i == 0) & (j == 0))
        def _():
            dgain_ref[...] = jnp.zeros_like(dgain_ref)

        acc_ref[...] += _dot_nt(a_ref[...], w_ref[...])

        @pl.when(j == nk - 1)
        def _():
            dx, dgain = _rms_bwd(acc_ref[...], h_ref[...], g_ref[...])
            dh = dho_ref[...] + dx
            dh_ref[...] = dh
            dhb_ref[...] = dh.astype(BF16)
            dgain_ref[...] += dgain

    row = pl.BlockSpec((tm, D_MODEL), lambda i, j: (i, 0))
    vec = pl.BlockSpec((1, D_MODEL), lambda i, j: (0, 0))
    return _pc_carrying(
        body, carry, (a, w, h, gain, dho), name=name, grid=(s // tm, nk),
        in_specs=[pl.BlockSpec((1, tm, tk), lambda i, j: (j, i, 0)) if stacked else pl.BlockSpec((tm, tk), lambda i, j: (i, 0)),
                  pl.BlockSpec((D_MODEL, tk), lambda i, j: (0, j)), row, vec, row],
        out_specs=[row, row, vec],
        out_shape=[_sds((s, D_MODEL), F32), _sds((s, D_MODEL), BF16), _sds((1, D_MODEL), F32)],
        scratch_shapes=[pltpu.VMEM((tm, D_MODEL), F32)],
    )


def _matmul_nt(a, w, name):
    s, kd = a.shape
    n = w.shape[0]
    tm = _row_block(s)

    def body(a_ref, w_ref, o_ref):
        o_ref[...] = _dot_nt(a_ref[...], w_ref[...]).astype(BF16)

    return _pc(
        body, name=name, grid=(s // tm,),
        in_specs=[pl.BlockSpec((tm, kd), lambda i: (i, 0)), pl.BlockSpec((n, kd), lambda i: (0, 0))],
        out_specs=pl.BlockSpec((tm, n), lambda i: (i, 0)),
        out_shape=_sds((s, n), BF16),
    )(a, w)


def _matmul_tn(a, b, name, carry=None):
    s, m = a.shape
    stacked = b.ndim == 3
    tmm = min(m, 1024)
    if stacked:
        n, tn = b.shape[0] * b.shape[2], b.shape[2]
    else:
        n = b.shape[1]
        tn = n if n <= 1024 else 1024
    tk = min(_TN_ROWS, s)
    nk = s // tk

    def body(a_ref, b_ref, o_ref, acc_ref):
        k = pl.program_id(2)
        if stacked:
            b_ref = b_ref.at[0]

        @pl.when(k == 0)
        def _():
            acc_ref[...] = jnp.zeros_like(acc_ref)

        acc_ref[...] += _dot_tn(a_ref[...], b_ref[...])

        @pl.when(k == nk - 1)
        def _():
            o_ref[...] = acc_ref[...].astype(BF16)

    (out,), got = _pc_carrying(
        body, carry, (a, b), name=name, grid=(m // tmm, n // tn, nk),
        in_specs=[pl.BlockSpec((tk, tmm), lambda i, j, k: (k, i)),
                  pl.BlockSpec((1, tk, tn), lambda i, j, k: (j, k, 0)) if stacked
                  else pl.BlockSpec((tk, tn), lambda i, j, k: (k, j))],
        out_specs=[pl.BlockSpec((tmm, tn), lambda i, j, k: (i, j))],
        out_shape=[_sds((m, n), BF16)],
        scratch_shapes=[pltpu.VMEM((tmm, tn), F32)],
    )
    return out if carry is None else (out, got)


def _conv_bwd(dbuc, bcx, cw):
    s = bcx.shape[0]
    tc = _CONV_COLS
    nc = D_MODEL // tc

    def body(d_ref, b_ref, c_ref, xp_ref, w_ref, dbcx_ref, dw_ref):
        rows = lax.broadcasted_iota(jnp.int32, (s, tc), 0)
        c = c_ref[...].astype(F32)
        xp = xp_ref[...].astype(F32)
        u = c * xp
        u1 = _shift_down(u, 1, rows)
        u2 = _shift_down(u, 2, rows)
        w = w_ref[...]
        uc = w[2:3] * u + w[1:2] * u1 + w[0:1] * u2
        d = d_ref[...].astype(F32)
        dbcx_ref[0] = (d * uc).astype(BF16)
        duc = d * b_ref[...].astype(F32)
        du = w[2:3] * duc + w[1:2] * _shift_up(duc, 1, rows, s) + w[0:1] * _shift_up(duc, 2, rows, s)
        dbcx_ref[1] = (du * xp).astype(BF16)
        dbcx_ref[2] = (du * c).astype(BF16)
        dw_ref[0:1, :] = jnp.sum(duc * u2, axis=0, keepdims=True)
        dw_ref[1:2, :] = jnp.sum(duc * u1, axis=0, keepdims=True)
        dw_ref[2:3, :] = jnp.sum(duc * u, axis=0, keepdims=True)

    col = lambda off: pl.BlockSpec((s, tc), lambda j: (0, off + j))
    tap = pl.BlockSpec((3, tc), lambda j: (0, j))
    return _pc(
        body, name="conv_bwd", grid=(nc,),
        in_specs=[col(0), col(0), col(nc), col(2 * nc), tap],
        out_specs=[pl.BlockSpec((3, s, tc), lambda j: (0, 0, j)), tap],
        out_shape=[_sds((3, s, D_MODEL), BF16), _sds((3, D_MODEL), F32)],
    )(dbuc, bcx, bcx, bcx, cw)


def _attn_bwd(q, k, v, do, o, lse, cc, sn, sp, carry=None):
    s = q.shape[1]
    bk = _row_block(s)
    nb = s // bk

    def body(q_ref, k_ref, v_ref, do_ref, o_ref, lse_ref, cc_ref, sn_ref, sp_ref,
             dq_ref, dk_ref, dv_ref, dqacc_ref, delta_ref, bias_ref):
        j = pl.program_id(1)
        kb = k_ref[0]
        vb = v_ref[0]

        @pl.when((pl.program_id(0) == 0) & (j == 0))
        def _():
            bias_ref[...] = _chunk_bias(bk, bk)

        @pl.when(j == 0)
        def _():
            dqacc_ref[...] = jnp.zeros_like(dqacc_ref)

            def fill(i, _):
                rows = pl.ds(pl.multiple_of(i * bk, bk), bk)
                d = jnp.sum(do_ref[rows, :].astype(F32) * o_ref[rows, :].astype(F32), axis=-1, keepdims=True)
                delta_ref[rows, :] = jnp.broadcast_to(d, (bk, 128))
                return 0

            lax.fori_loop(0, nb, fill, 0)

        def step(i, carry, masked):
            dk, dv = carry
            rows = pl.ds(pl.multiple_of(i * bk, bk), bk)
            qb = q_ref[0, rows, :]
            dob = do_ref[rows, :]
            sc = _dot_nt(qb, kb)
            if masked:
                sc = sc + bias_ref[...]
            p = jnp.exp2(sc - lse_ref[0, rows, :][:, :1])
            dv = dv + _dot_tn(p.astype(BF16), dob)
            ds = (p * (_dot_nt(dob, vb) - delta_ref[rows, :][:, :1])).astype(BF16)
            dk = dk + _dot_tn(ds, qb)
            dqacc_ref[rows, :] += _dot(ds, kb)
            return dk, dv

        carry = step(j, (jnp.zeros((bk, HEAD_PAD), F32), jnp.zeros((bk, V_DIM), F32)), True)
        dq = dqacc_ref[pl.ds(pl.multiple_of(j * bk, bk), bk), :] * ATT_SCALE
        dq_ref[:, :NOPE] = dq[:, :NOPE].astype(BF16)
        dq_ref[:, NOPE:] = _rope_bwd(dq[:, NOPE:], cc_ref[...], sn_ref[...], sp_ref[...]).astype(BF16)
        dk, dv = lax.fori_loop(j + 1, nb, lambda i, c: step(i, c, False), carry)
        dk_ref[0] = (dk * LN_2).astype(BF16)
        dv_ref[0] = dv.astype(BF16)

    blk = lambda n: pl.BlockSpec((1, bk, n), lambda h, j: (h, j, 0))
    whole = lambda n: pl.BlockSpec((1, s, n), lambda h, j: (h, 0, 0))
    cols = pl.BlockSpec((s, V_DIM), lambda h, j: (0, h))
    tab = pl.BlockSpec((bk, 128), lambda h, j: (j, 0))
    return _pc_carrying(
        body, carry, (q, k, v, do, o, lse, cc, sn, sp), name="attn_bwd", grid=(N_HEADS, nb),
        in_specs=[whole(HEAD_PAD), blk(HEAD_PAD), blk(V_DIM), cols, cols, whole(128), tab, tab, tab],
        out_specs=[pl.BlockSpec((bk, HEAD_PAD), lambda h, j: (j, h)), blk(HEAD_PAD), blk(V_DIM)],
        out_shape=[_sds((s, N_HEADS * HEAD_PAD), BF16), _sds((N_HEADS, s, HEAD_PAD), BF16),
                   _sds((N_HEADS, s, V_DIM), BF16)],
        scratch_shapes=[pltpu.VMEM((s, HEAD_PAD), F32), pltpu.VMEM((s, 128), F32), pltpu.VMEM((bk, bk), F32)],
    )


def _mla_mid_bwd(dq, dk, dv, proj, w_uq, w_ukv, g_cq, g_ckv, cc, sn, sp, carry=None):
    s = dq.shape[0]
    tm = _row_block(s)

    def body(dq_ref, dk_ref, dv_ref, proj_ref, wuq_ref, wukv_ref, gcq_ref, gckv_ref, cc_ref, sn_ref, sp_ref,
             dproj_ref, dkv_ref, dgcq_ref, dgckv_ref, acq_ref, ackv_ref, akr_ref):
        @pl.when(pl.program_id(0) == 0)
        def _():
            dgcq_ref[...] = jnp.zeros_like(dgcq_ref)
            dgckv_ref[...] = jnp.zeros_like(dgckv_ref)

        for h in range(N_HEADS):
            cols = slice(h * HEAD_PAD, (h + 1) * HEAD_PAD)
            dkb = dk_ref[h]
            dkv_ref[:, h * HEAD_PAD:h * HEAD_PAD + NOPE] = dkb[:, :NOPE]
            dkv_ref[:, h * HEAD_PAD + NOPE:(h + 1) * HEAD_PAD] = dv_ref[h]
            cq_part = _dot_nt(dq_ref[:, cols], wuq_ref[h])
            ckv_part = _dot_nt(dkv_ref[:, cols], wukv_ref[h])
            kr_part = dkb[:, NOPE:].astype(F32)
            if h == 0:
                acq_ref[...], ackv_ref[...], akr_ref[...] = cq_part, ckv_part, kr_part
            else:
                acq_ref[...] += cq_part
                ackv_ref[...] += ckv_part
                akr_ref[...] += kr_part

        proj = proj_ref[...]
        dcq, dgcq = _rms_bwd(acq_ref[...], proj[:, :CQ], gcq_ref[...])
        dckv, dgckv = _rms_bwd(ackv_ref[...], proj[:, CQ:CQ + CKV], gckv_ref[...])
        dproj_ref[:, :CQ] = dcq.astype(BF16)
        dproj_ref[:, CQ:CQ + CKV] = dckv.astype(BF16)
        dproj_ref[:, CQ + CKV:] = _rope_bwd(akr_ref[...], cc_ref[...], sn_ref[...], sp_ref[...]).astype(BF16)
        dgcq_ref[...] += dgcq
        dgckv_ref[...] += dgckv

    head_blk = lambda n: pl.BlockSpec((N_HEADS, tm, n), lambda i: (0, i, 0))
    head_cols = pl.BlockSpec((tm, N_HEADS * HEAD_PAD), lambda i: (i, 0))
    head_w = lambda a: pl.BlockSpec(a.shape, lambda i: (0, 0, 0))
    row = lambda n: pl.BlockSpec((tm, n), lambda i: (i, 0))
    vec = lambda n: pl.BlockSpec((1, n), lambda i: (0, 0))
    return _pc_carrying(
        body, carry, (dq, dk, dv, proj, w_uq, w_ukv, g_cq, g_ckv, cc, sn, sp),
        name="mla_mid_bwd", grid=(s // tm,),
        in_specs=[head_cols, head_blk(HEAD_PAD), head_blk(V_DIM), row(PROJ_PAD), head_w(w_uq), head_w(w_ukv),
                  vec(CQ), vec(CKV), row(128), row(128), row(128)],
        out_specs=[row(PROJ_PAD), head_cols, vec(CQ), vec(CKV)],
        out_shape=[_sds((s, PROJ_PAD), BF16), _sds((s, N_HEADS * HEAD_PAD), BF16), _sds((1, CQ), F32),
                   _sds((1, CKV), F32)],
        scratch_shapes=[pltpu.VMEM((tm, CQ), F32), pltpu.VMEM((tm, CKV), F32), pltpu.VMEM((tm, 128), F32)],
    )


def _peer(k):
    x, y, c = lax.axis_index("x"), lax.axis_index("y"), lax.axis_index("c")
    px = 1 - x if k & 4 else x
    py = 1 - y if k & 2 else y
    pc = 1 - c if k & 1 else c
    return (px, py, pc), 4 * px + 2 * py + pc


def _exchange_ops(arrays, scatter):
    nw = len(arrays)
    by_cols = [isinstance(a, _Cols) for a in arrays]
    arrays = [a.array if isinstance(a, _Cols) else a for a in arrays]
    direct = range(1, N_DEV) if scatter else (1, 2, 4, 6)

    def columns(ref, idx, width):
        return ref.at[:, pl.ds(pl.multiple_of(idx * width, 128), width)]

    def sent(ins, w, idx):
        if not scatter:
            return ins[w]
        return columns(ins[w], idx, arrays[w].shape[1] // N_DEV) if by_cols[w] else ins[w].at[idx]

    def slot(outs, w, idx):
        if by_cols[w] and not scatter:
            return columns(outs[w], idx, arrays[w].shape[1])
        return outs[w].at[idx]

    def copy(w, k, src, dst, to, send_sems, recv_sems):
        return pltpu.make_async_remote_copy(
            src_ref=src, dst_ref=dst, send_sem=send_sems.at[w * N_DEV + k], recv_sem=recv_sems.at[w * N_DEV + k],
            device_id=to, device_id_type=pl.DeviceIdType.MESH)

    def own_copies(ins, outs, local_sems):
        _, me = _peer(0)
        return [pltpu.make_async_copy(sent(ins, w, me), slot(outs, w, me), local_sems.at[w]) for w in range(nw)]

    def sends(ins, outs, send_sems, recv_sems):
        _, me = _peer(0)
        out = []
        for k in direct:
            dev, idx = _peer(k)
            for w in range(nw):
                out.append(copy(w, k, sent(ins, w, idx), slot(outs, w, me), dev, send_sems, recv_sems))
        return out

    def relays(outs, send_sems, recv_sems):
        sibling, _ = _peer(1)
        out = []
        for k in (2, 4, 6):
            _, idx = _peer(k)
            for w in range(nw):
                out.append(copy(w, k + 1, slot(outs, w, idx), slot(outs, w, idx), sibling, send_sems, recv_sems))
        return out

    def arrival(outs, w, k, send_sems, recv_sems):
        dev, idx = _peer(k)
        return copy(w, k, slot(outs, w, idx), slot(outs, w, idx), dev, send_sems, recv_sems)

    def start(ins, outs, send_sems, recv_sems, local_sems):
        for cp in own_copies(ins, outs, local_sems) + sends(ins, outs, send_sems, recv_sems):
            cp.start()

    def relay(ins, outs, send_sems, recv_sems, local_sems):
        for k in (2, 4, 6):
            for w in range(nw):
                arrival(outs, w, k, send_sems, recv_sems).wait_recv()
        for cp in relays(outs, send_sems, recv_sems):
            cp.start()

    def wait(ins, outs, send_sems, recv_sems, local_sems):
        for cp in own_copies(ins, outs, local_sems):
            cp.wait()
        for cp in sends(ins, outs, send_sems, recv_sems) + ([] if scatter else relays(outs, send_sems, recv_sems)):
            cp.wait_send()
        for k in (range(1, N_DEV) if scatter else (1, 3, 5, 7)):
            for w in range(nw):
                arrival(outs, w, k, send_sems, recv_sems).wait_recv()

    def landed_shape(a, cols):
        if not cols:
            return a.shape if scatter else (N_DEV,) + a.shape
        r, c = a.shape
        return (N_DEV, r, c // N_DEV) if scatter else (r, N_DEV * c)

    landed = [_sds(landed_shape(a, cols), a.dtype) for a, cols in zip(arrays, by_cols)]
    sems = [pltpu.SemaphoreType.DMA((nw * N_DEV,)), pltpu.SemaphoreType.DMA((nw * N_DEV,)),
            pltpu.SemaphoreType.DMA((nw,))]
    return start, (None if scatter else relay), wait, landed, sems, arrays


class _Cols:
    def __init__(self, array):
        self.array = array


def _exchange(arrays, scatter, name):
    nw = len(arrays)
    start, relay, wait, landed, sems, arrays = _exchange_ops(arrays, scatter)

    def body(*refs):
        ins, outs, csem = refs[:nw], refs[nw:2 * nw], refs[2 * nw:]
        start(ins, outs, *csem)
        if relay is not None:
            relay(ins, outs, *csem)
        wait(ins, outs, *csem)

    any_spec = pl.BlockSpec(memory_space=pl.ANY)
    return _pc(body, name=name, in_specs=[any_spec] * nw, out_specs=[any_spec] * nw, out_shape=landed,
               scratch_shapes=sems)(*arrays)


def _adam_math(g, w, m, v):
    m = ADAM_B1 * m + (1.0 - ADAM_B1) * g
    v = ADAM_B2 * v + (1.0 - ADAM_B2) * jnp.square(g)
    m_hat = m / (1.0 - ADAM_B1 ** ADAM_STEP)
    v_hat = v / (1.0 - ADAM_B2 ** ADAM_STEP)
    delta = -ADAM_LR * (m_hat / (jnp.sqrt(v_hat) + ADAM_EPS) + ADAM_WD * w)
    return delta, m, v


def _adam_rows(r):
    for t in range(min(r, 512) // 16 * 16, 0, -16):
        if r % t == 0:
            return t
    return r


def _adamw(parts, w, m, v, name):
    nl, r, c = w.shape
    tr = _adam_rows(r)
    nr = r // tr

    def body(*refs):
        p_refs, (w_ref, m_ref, v_ref, g_ref, d_ref, mo_ref, vo_ref) = refs[:nl], refs[nl:]
        for layer in range(nl):
            @pl.when(pl.program_id(0) == layer)
            def _(p_ref=p_refs[layer]):
                g = p_ref[0].astype(F32)
                for src in range(1, N_DEV):
                    g = g + p_ref[src].astype(F32)
                delta, m2, v2 = _adam_math(g, w_ref[0], m_ref[0], v_ref[0])
                g_ref[0] = g
                d_ref[0] = delta
                mo_ref[0] = m2
                vo_ref[0] = v2

    def part_spec(layer):
        def index(l, i):
            return 0, jnp.where(l == layer, i, jnp.where(l < layer, 0, nr - 1)), 0
        return pl.BlockSpec((N_DEV, tr, c), index)

    blk = pl.BlockSpec((1, tr, c), lambda l, i: (l, i, 0))
    return _pc(
        body, name=name, grid=(nl, nr),
        in_specs=[part_spec(layer) for layer in range(nl)] + [blk, blk, blk],
        out_specs=[blk, blk, blk, blk],
        out_shape=[_sds((nl, r, c), F32)] * 4,
    )(*parts, w, m, v)


def _small_allreduce_adamw(gpack, wpack, mpack, vpack, last_grads):
    shape = gpack.shape
    compact = (SMALL_ROWS, D_MODEL)
    nw = len(last_grads)
    start, _, wait, landed, exchange_sems, last_grads = _exchange_ops(last_grads, True)

    def body(*refs):
        (g_ref, w_ref, m_ref, v_ref), rest = refs[:4], refs[4:]
        cin, rest = rest[:nw], rest[nw:]
        (go_ref, d_ref, mo_ref, vo_ref), rest = rest[:4], rest[4:]
        cout, rest = rest[:nw], rest[nw:]
        (comp_ref, gath_ref, send_sems, recv_sems), csem = rest[:4], rest[4:]
        start(cin, cout, *csem)
        _, me = _peer(0)
        comp_ref[...] = jnp.zeros(compact, F32)
        r = 0
        for p, n in enumerate(SMALL_PIECES):
            comp_ref[r:r + n, :] = g_ref[8 * p:8 * p + n, :]
            r += n
        gath_ref[me] = comp_ref[...]
        copies = []
        for k in range(1, N_DEV):
            dev, idx = _peer(k)
            copies.append(pltpu.make_async_remote_copy(
                src_ref=comp_ref, dst_ref=gath_ref.at[me], send_sem=send_sems.at[k], recv_sem=recv_sems.at[k],
                device_id=dev, device_id_type=pl.DeviceIdType.MESH))
        for cp in copies:
            cp.start()
        for cp in copies:
            cp.wait_send()
        for k in range(1, N_DEV):
            dev, idx = _peer(k)
            pltpu.make_async_remote_copy(
                src_ref=comp_ref, dst_ref=gath_ref.at[idx], send_sem=send_sems.at[k], recv_sem=recv_sems.at[k],
                device_id=dev, device_id_type=pl.DeviceIdType.MESH).wait_recv()
        g = gath_ref[0]
        for src in range(1, N_DEV):
            g = g + gath_ref[src]
        go_ref[...] = jnp.zeros(shape, F32)
        r = 0
        for p, n in enumerate(SMALL_PIECES):
            go_ref[8 * p:8 * p + n, :] = g[r:r + n, :]
            r += n
        delta, m2, v2 = _adam_math(go_ref[...], w_ref[...], m_ref[...], v_ref[...])
        d_ref[...] = delta
        mo_ref[...] = m2
        vo_ref[...] = v2
        wait(cin, cout, *csem)

    vm = pl.BlockSpec(memory_space=pltpu.VMEM)
    any_spec = pl.BlockSpec(memory_space=pl.ANY)
    res = _pc(
        body, name="small_allreduce_adamw",
        in_specs=[vm] * 4 + [any_spec] * nw, out_specs=[vm] * 4 + [any_spec] * nw,
        out_shape=[_sds(shape, F32)] * 4 + landed,
        scratch_shapes=[pltpu.VMEM(compact, F32), pltpu.VMEM((N_DEV,) + compact, F32),
                        pltpu.SemaphoreType.DMA((N_DEV,)), pltpu.SemaphoreType.DMA((N_DEV,))] + exchange_sems,
    )(gpack, wpack, mpack, vpack, *last_grads)
    return res[:4], res[4:]


def _rows_from_shards(g):
    return g.reshape(N_DEV * g.shape[1], g.shape[2])


def _shards_from_rows(a):
    return a.reshape(N_DEV, a.shape[0] // N_DEV, a.shape[1])


def _pad_to(a, rows, cols):
    return jnp.pad(a, ((0, rows - a.shape[0]), (0, cols - a.shape[1])))


def _rope_tables(pos):
    inv_freq = 1.0 / (ROPE_THETA ** (jnp.arange(0, ROPE, 2, dtype=F32) / ROPE))
    ang = pos.astype(F32)[:, None] * inv_freq
    cos, sin = jnp.cos(ang), jnp.sin(ang)
    z32, z64, z96 = (jnp.zeros((pos.shape[0], n), F32) for n in (32, 64, 96))
    return (jnp.concatenate([cos, cos, z64], axis=1), jnp.concatenate([-sin, z96], axis=1),
            jnp.concatenate([z32, sin, z64], axis=1))


def _pad_row(vec):
    vec = vec.reshape(1, -1)
    return jnp.pad(vec, ((0, 0), (0, D_MODEL - vec.shape[1])))


def _forward_backward(x, target, tables, gains, shards):
    cc, sn, sp = tables
    ffn_g = gains["ffn_norm"]
    by_cols = ("wg0", "wu0", "wg1", "wu1", "c_in")

    def gather(names, relay_at):
        return [_Cols(shards[n]) if n in by_cols else shards[n] for n in names], False, relay_at

    (got,) = _exchange([shards["w_in"]], False, "gather_w_in")
    w_in = _rows_from_shards(got)
    (n0, proj, cqn, ckvn, krr), (w_uq, w_ukv) = _mla_in_fwd(
        x, gains["mla_norm"], w_in, gains["g_cq"], gains["g_ckv"], cc, sn, sp, carry=gather(["w_uq", "w_ukv"], 0.5))
    (q, k, v), (w_o, conv_norm, conv_w) = _qkv_proj(
        cqn, ckvn, krr, w_uq, w_ukv, cc, sn, sp, carry=gather(["w_o", "conv_norm", "conv_w"], 0.5))
    w_o = _rows_from_shards(w_o)
    conv_norm = conv_norm.reshape(1, D_MODEL)
    conv_w = jnp.transpose(conv_w, (1, 0, 2)).reshape(3, D_MODEL)
    (o, lse), (wg0, wu0, wd0, c_in, c_out, wd1) = _attn_fwd(
        q, k, v, carry=gather(["wg0", "wu0", "wd0", "c_in", "c_out", "wd1"], 0.75))
    wd0 = _rows_from_shards(wd0)
    wd1 = _rows_from_shards(wd1)
    c_out = _rows_from_shards(c_out)
    h1 = _matmul_res(o, w_o, x, "mla_out_fwd")
    (h2, n1, gate0, up0), (wg1, wu1) = _ffn_fwd(
        h1, ffn_g[0:1], wg0, wu0, wd0, "ffn0_fwd", carry=gather(["wg1", "wu1"], 0.7))
    n2, bcx = _rms_matmul(h2, conv_norm, c_in, "conv_in_fwd")
    bu = _conv_fwd(bcx, conv_w)
    h3 = _matmul_res(bu, c_out, h2, "conv_out_fwd")
    (h4, n3, gate1, up1), _ = _ffn_fwd(h3, ffn_g[1:2], wg1, wu1, wd1, "ffn1_fwd")
    dh4, d_final, loss = _final_loss(h4, gains["final_norm"], target)

    small = {"final_norm": d_final}
    parts = {}

    def scatter(**blocks):
        return list(blocks), (list(blocks.values()), True, None)

    (dh3, dh3_b, dh4_b, dgate, dup, act, d_ffn1), _ = _ffn_bwd_x(
        dh4, h3, ffn_g[1:2], gate1, up1, wg1, wu1, wd1, "ffn1_bwd")
    dwg1 = _matmul_tn(n3, dgate, "ffn1_dwg")
    dwu1 = _matmul_tn(n3, dup, "ffn1_dwu")
    dwd1 = _matmul_tn(act, dh4_b, "ffn1_dwd")
    dbuc = _matmul_nt(dh3_b, c_out, "conv_out_bwd")
    d_c_out = _matmul_tn(bu, dh3_b, "conv_dwout")
    dbcx, small["conv_w"] = _conv_bwd(dbuc, bcx, conv_w)
    d_c_in = _matmul_tn(n2, dbcx, "conv_dwin")
    (dh2, dh2_b, small["conv_norm"]), _ = _nt_rmsbwd(dbcx, c_in, h2, conv_norm, dh3, "conv_in_bwd")
    names, carry = scatter(c_in=_Cols(d_c_in), c_out=_shards_from_rows(d_c_out))
    (dh1, dh1_b, dh2_b2, dgate, dup, act, d_ffn0), got = _ffn_bwd_x(
        dh2, h1, ffn_g[0:1], gate0, up0, wg0, wu0, wd0, "ffn0_bwd", carry=carry)
    parts.update(zip(names, got))
    dwg0 = _matmul_tn(n1, dgate, "ffn0_dwg")
    dwu0 = _matmul_tn(n1, dup, "ffn0_dwu")
    dwd0 = _matmul_tn(act, dh2_b2, "ffn0_dwd")
    small["ffn_norm"] = jnp.pad(d_ffn0, ((0, 7), (0, 0))) + jnp.pad(d_ffn1, ((1, 6), (0, 0)))
    do = _matmul_nt(dh1_b, w_o, "mla_out_bwd")
    d_w_o = _matmul_tn(o, dh1_b, "mla_dwo")
    names, carry = scatter(wg0=_Cols(dwg0), wg1=_Cols(dwg1), wu0=_Cols(dwu0), wu1=_Cols(dwu1),
                           wd0=_shards_from_rows(dwd0), wd1=_shards_from_rows(dwd1))
    (dq, dk, dv), got = _attn_bwd(q, k, v, do, o, lse, cc, sn, sp, carry=carry)
    parts.update(zip(names, got))
    d_w_uq = _matmul_tn(cqn, dq, "mla_dwuq")
    names, carry = scatter(w_o=_shards_from_rows(d_w_o), w_uq=_Cols(d_w_uq))
    (dproj, dkv, small["g_cq"], small["g_ckv"]), got = _mla_mid_bwd(
        dq, dk, dv, proj, w_uq, w_ukv, gains["g_cq"], gains["g_ckv"], cc, sn, sp, carry=carry)
    parts.update(zip(names, got))
    d_w_ukv = _matmul_tn(ckvn, dkv, "mla_dwukv")
    names, carry = scatter(w_ukv=_Cols(d_w_ukv))
    d_w_in, got = _matmul_tn(n0, dproj, "mla_dwin", carry=carry)
    parts.update(zip(names, got))
    (dx, _, small["mla_norm"]), _ = _nt_rmsbwd(dproj, w_in, x, gains["mla_norm"], dh1, "mla_in_bwd")
    return loss, dx, parts, _shards_from_rows(d_w_in), small


def kernel(x, positions, mla_norm, mla_w_in, mla_g_cq, mla_g_ckv, mla_w_uq, mla_w_ukv, mla_w_o, conv_norm, conv_w_in, conv_w, conv_w_out, ffn_norm, ffn_w_gate, ffn_w_up, ffn_w_down, final_norm, loss_target, m_mla_norm, m_mla_w_in, m_mla_g_cq, m_mla_g_ckv, m_mla_w_uq, m_mla_w_ukv, m_mla_w_o, m_conv_norm, m_conv_w_in, m_conv_w, m_conv_w_out, m_ffn_norm, m_ffn_w_gate, m_ffn_w_up, m_ffn_w_down, m_final_norm, v_mla_norm, v_mla_w_in, v_mla_g_cq, v_mla_g_ckv, v_mla_w_uq, v_mla_w_ukv, v_mla_w_o, v_conv_norm, v_conv_w_in, v_conv_w, v_conv_w_out, v_ffn_norm, v_ffn_w_gate, v_ffn_w_up, v_ffn_w_down, v_final_norm):
    me = 4 * lax.axis_index("x") + 2 * lax.axis_index("y") + lax.axis_index("c")

    bf = lambda a, rows, cols: _pad_to(a.astype(BF16), rows, cols)
    shards = dict(
        w_in=bf(mla_w_in[0], D_MODEL // N_DEV, PROJ_PAD), w_uq=bf(mla_w_uq[0], CQ, HEAD_PAD),
        w_ukv=mla_w_ukv[0].astype(BF16), w_o=mla_w_o[0].astype(BF16),
        c_in=conv_w_in[0].astype(BF16), c_out=conv_w_out[0].astype(BF16),
        conv_norm=conv_norm, conv_w=conv_w[0])
    for l in range(2):
        shards.update({f"wg{l}": bf(ffn_w_gate[l], D_MODEL, FF_SHARD_PAD), f"wu{l}": bf(ffn_w_up[l], D_MODEL, FF_SHARD_PAD),
                       f"wd{l}": bf(ffn_w_down[l], FF_SHARD_PAD, D_MODEL)})
    gains = dict(mla_norm=mla_norm, g_cq=mla_g_cq, g_ckv=mla_g_ckv, ffn_norm=ffn_norm,
                 final_norm=final_norm.reshape(1, -1))
    loss_local, dx, parts, d_w_in, grads = _forward_backward(
        x[0], loss_target[0], _rope_tables(positions[0]), gains, shards)

    col0 = me * (D_MODEL // N_DEV)

    def place(shard):
        return lax.dynamic_update_slice(jnp.zeros((shard.shape[0], D_MODEL), F32), shard, (0, col0))

    no_loss = jnp.zeros((8, D_MODEL), F32)

    def pack(mla_n, g_cq, g_ckv, ffn_n, fin_n, conv_n, conv_taps, loss_tile):
        rows = [mla_n, _pad_row(g_cq), _pad_row(g_ckv), ffn_n, fin_n.reshape(1, -1), conv_n, conv_taps, loss_tile]
        assert all(r.shape[0] in (n, 8) for r, n in zip(rows, SMALL_PIECES))
        return jnp.concatenate([jnp.pad(r, ((0, 8 - r.shape[0]), (0, 0))) for r in rows], axis=0)

    gpack = pack(grads["mla_norm"], grads["g_cq"], grads["g_ckv"], grads["ffn_norm"], grads["final_norm"],
                 grads["conv_norm"], grads["conv_w"], loss_local)
    wpack = pack(mla_norm, mla_g_cq, mla_g_ckv, ffn_norm, final_norm, place(conv_norm), place(conv_w[0]), no_loss)
    mpack = pack(m_mla_norm, m_mla_g_cq, m_mla_g_ckv, m_ffn_norm, m_final_norm, place(m_conv_norm),
                 place(m_conv_w[0]), no_loss)
    vpack = pack(v_mla_norm, v_mla_g_cq, v_mla_g_ckv, v_ffn_norm, v_final_norm, place(v_conv_norm),
                 place(v_conv_w[0]), no_loss)
    small, (parts["w_in"],) = _small_allreduce_adamw(gpack, wpack, mpack, vpack, [d_w_in])
    loss = small[0][56, 0]

    def adamw(name, w, m, v, partials):
        if w.ndim == 2:
            w, m, v = w[None], m[None], v[None]
        return [o.reshape(w.shape) for o in _adamw(partials, w, m, v, "adamw_" + name)]

    res = dict(
        w_in=adamw("w_in", mla_w_in, m_mla_w_in, v_mla_w_in, [parts["w_in"][:, :, :PROJ]]),
        w_uq=adamw("w_uq", mla_w_uq, m_mla_w_uq, v_mla_w_uq, [parts["w_uq"][:, :, :NOPE + ROPE]]),
        w_ukv=adamw("w_ukv", mla_w_ukv, m_mla_w_ukv, v_mla_w_ukv, [parts["w_ukv"]]),
        w_o=adamw("w_o", mla_w_o, m_mla_w_o, v_mla_w_o, [parts["w_o"]]),
        c_in=adamw("c_in", conv_w_in, m_conv_w_in, v_conv_w_in, [parts["c_in"]]),
        c_out=adamw("c_out", conv_w_out, m_conv_w_out, v_conv_w_out, [parts["c_out"]]),
        wg=adamw("wg", ffn_w_gate, m_ffn_w_gate, v_ffn_w_gate, [parts[n][:, :, :FF_SHARD] for n in ("wg0", "wg1")]),
        wu=adamw("wu", ffn_w_up, m_ffn_w_up, v_ffn_w_up, [parts[n][:, :, :FF_SHARD] for n in ("wu0", "wu1")]),
        wd=adamw("wd", ffn_w_down, m_ffn_w_down, v_ffn_w_down, [parts["wd0"], parts["wd1"]]),
    )

    def unpack(p):
        own = lambda rows: lax.dynamic_slice(rows, (0, col0), (rows.shape[0], D_MODEL // N_DEV))
        return dict(mla_norm=p[0:1], g_cq=p[8:9, :CQ], g_ckv=p[16:17, :CKV], ffn_norm=p[24:26], final_norm=p[32],
                    conv_norm=own(p[40:41]), conv_w=own(p[48:51])[None])

    small = [unpack(p) for p in small]
    order = ["mla_norm", "w_in", "g_cq", "g_ckv", "w_uq", "w_ukv", "w_o", "conv_norm", "c_in", "conv_w", "c_out",
             "ffn_norm", "wg", "wu", "wd", "final_norm"]
    out = [loss, dx[None]]
    for kind in range(4):
        for n in order:
            out.append(res[n][kind] if n in res else small[kind][n])
    return tuple(out)
```

```python
import math

import jax
import jax.numpy as jnp
from jax import lax
from jax.experimental import pallas as pl
from jax.experimental.pallas import tpu as pltpu

F32 = jnp.float32
BF16 = jnp.bfloat16

N_DEV = 8
D_MODEL = 1024
N_HEADS = 8
NOPE = 128
ROPE = 64
V_DIM = 128
HEAD_PAD = 256
CQ = 512
CKV = 256
PROJ = CQ + CKV + ROPE
PROJ_PAD = CQ + CKV + 128
D_FF = 2816
FF_SHARD = D_FF // N_DEV
FF_SHARD_PAD = 384
FF_PAD = FF_SHARD_PAD * N_DEV
CHUNK_SHIFT = 6
RMS_EPS = 1e-6
ROPE_THETA = 10000.0
ATT_SCALE = 1.0 / math.sqrt(NOPE + ROPE)
LOG2_E = math.log2(math.e)
LN_2 = math.log(2.0)
Q_SCALE = ATT_SCALE * LOG2_E
NEG = -1e30

ADAM_LR = 0.001
ADAM_B1 = 0.9
ADAM_B2 = 0.999
ADAM_EPS = 1e-08
ADAM_WD = 0.01
ADAM_STEP = 10

SMALL_PIECES = (1, 1, 1, 2, 1, 1, 3, 1)
SMALL_ROWS = 16

_NT = (((1,), (1,)), ((), ()))
_TN = (((0,), (0,)), ((), ()))


def _pc(body, *, name, out_shape, grid=(), in_specs=None, out_specs=None, scratch_shapes=(), vmem_mb=None):
    params = {}
    if vmem_mb is not None:
        params["vmem_limit_bytes"] = vmem_mb << 20
    kwargs = dict(
        name=name, out_shape=out_shape, grid=grid, scratch_shapes=scratch_shapes,
        compiler_params=pltpu.CompilerParams(**params),
    )
    if in_specs is not None:
        kwargs["in_specs"] = in_specs
    if out_specs is not None:
        kwargs["out_specs"] = out_specs
    return pl.pallas_call(body, **kwargs)


def _pc_carrying(body, carry, operands, *, name, out_shape, grid, in_specs, out_specs, scratch_shapes=()):
    if carry is None:
        return _pc(body, name=name, out_shape=out_shape, grid=grid, in_specs=in_specs, out_specs=out_specs,
                   scratch_shapes=scratch_shapes)(*operands), None
    arrays, scatter, relay_at = carry
    nw, n_in, n_out, n_scr = len(arrays), len(in_specs), len(out_shape), len(scratch_shapes)
    start, relay, wait, landed_shapes, sems, arrays = _exchange_ops(arrays, scatter)
    n_steps = math.prod(grid)
    relay_step = None if relay is None else max(1, min(int(relay_at * n_steps), n_steps - 2))

    def wrapped(*refs):
        ins, rest = refs[:n_in], refs[n_in:]
        cin, rest = rest[:nw], rest[nw:]
        outs, rest = rest[:n_out], rest[n_out:]
        cout, rest = rest[:nw], rest[nw:]
        scr, csem = rest[:n_scr], rest[n_scr:]
        step = pl.program_id(0)
        for a in range(1, len(grid)):
            step = step * grid[a] + pl.program_id(a)

        @pl.when(step == 0)
        def _():
            start(cin, cout, *csem)

        if relay is not None:
            @pl.when(step == relay_step)
            def _():
                relay(cin, cout, *csem)

        body(*ins, *outs, *scr)

        @pl.when(step == n_steps - 1)
        def _():
            wait(cin, cout, *csem)

    any_spec = pl.BlockSpec(memory_space=pl.ANY)
    res = _pc(
        wrapped, name=name, grid=grid,
        in_specs=list(in_specs) + [any_spec] * nw, out_specs=list(out_specs) + [any_spec] * nw,
        out_shape=list(out_shape) + landed_shapes, scratch_shapes=list(scratch_shapes) + sems,
    )(*operands, *arrays)
    return res[:n_out], res[n_out:]


def _sds(shape, dtype):
    return jax.ShapeDtypeStruct(shape, dtype)


def _dot(a, b):
    return jnp.dot(a, b, preferred_element_type=F32)


def _dot_nt(a, b):
    return lax.dot_general(a, b, _NT, preferred_element_type=F32)


def _dot_tn(a, b):
    return lax.dot_general(a, b, _TN, preferred_element_type=F32)


def _rstd(x):
    return lax.rsqrt(jnp.mean(x * x, axis=-1, keepdims=True) + RMS_EPS)


def _rms(x, g):
    return (x * _rstd(x)) * g


def _rms_bwd(dy, x, g):
    r = _rstd(x)
    xhat = x * r
    dxhat = dy * g
    dx = r * (dxhat - xhat * jnp.mean(dxhat * xhat, axis=-1, keepdims=True))
    return dx, jnp.sum(dy * xhat, axis=0, keepdims=True)


def _rope(t, cc, sn, sp):
    return t * cc + pltpu.roll(t, 96, 1) * sn + pltpu.roll(t, 32, 1) * sp


def _rope_bwd(dt, cc, sn, sp):
    return dt * cc + pltpu.roll(dt * sn, 32, 1) + pltpu.roll(dt * sp, 96, 1)


def _row_block(s):
    return min(512, s)


_TN_ROWS = 1024
_FFN_ROWS = 1024


def _mla_in_fwd(x, g0, w_in, g_cq, g_ckv, cc, sn, sp, carry=None):
    s = x.shape[0]
    tm = _row_block(s)

    def body(x_ref, g0_ref, w_ref, gcq_ref, gckv_ref, cc_ref, sn_ref, sp_ref,
             n_ref, proj_ref, cqn_ref, ckvn_ref, krr_ref):
        nb = _rms(x_ref[...], g0_ref[...]).astype(BF16)
        n_ref[...] = nb
        proj = _dot(nb, w_ref[...])
        proj_ref[...] = proj
        cqn_ref[...] = _rms(proj[:, :CQ], gcq_ref[...]).astype(BF16)
        ckvn_ref[...] = _rms(proj[:, CQ:CQ + CKV], gckv_ref[...]).astype(BF16)
        krr_ref[...] = _rope(proj[:, CQ + CKV:], cc_ref[...], sn_ref[...], sp_ref[...]).astype(BF16)

    row = lambda n: pl.BlockSpec((tm, n), lambda i: (i, 0))
    full = lambda a: pl.BlockSpec(a.shape, lambda i: (0, 0))
    return _pc_carrying(
        body, carry, (x, g0, w_in, g_cq, g_ckv, cc, sn, sp), name="mla_in_fwd", grid=(s // tm,),
        in_specs=[row(D_MODEL), full(g0), full(w_in), full(g_cq), full(g_ckv), row(128), row(128), row(128)],
        out_specs=[row(D_MODEL), row(PROJ_PAD), row(CQ), row(CKV), row(128)],
        out_shape=[_sds((s, D_MODEL), BF16), _sds((s, PROJ_PAD), F32), _sds((s, CQ), BF16),
                   _sds((s, CKV), BF16), _sds((s, 128), BF16)],
    )


def _qkv_proj(cqn, ckvn, krr, w_uq, w_ukv, cc, sn, sp, carry=None):
    s = cqn.shape[0]
    tm = _row_block(s)

    def body(cqn_ref, ckvn_ref, krr_ref, wuq_ref, wukv_ref, cc_ref, sn_ref, sp_ref, q_ref, k_ref, v_ref):
        cqn_b, ckvn_b, krr_b = cqn_ref[...], ckvn_ref[...], krr_ref[...]
        cc_b, sn_b, sp_b = cc_ref[...], sn_ref[...], sp_ref[...]
        for h in range(N_HEADS):
            q = _dot(cqn_b, wuq_ref[h]) * Q_SCALE
            q_ref[h, :, :NOPE] = q[:, :NOPE].astype(BF16)
            q_ref[h, :, NOPE:] = _rope(q[:, NOPE:], cc_b, sn_b, sp_b).astype(BF16)
            kv = _dot(ckvn_b, wukv_ref[h])
            k_ref[h, :, :NOPE] = kv[:, :NOPE].astype(BF16)
            k_ref[h, :, NOPE:] = krr_b
            v_ref[h] = kv[:, NOPE:].astype(BF16)

    row = lambda n: pl.BlockSpec((tm, n), lambda i: (i, 0))
    head_w = lambda a: pl.BlockSpec(a.shape, lambda i: (0, 0, 0))
    head_o = lambda n: pl.BlockSpec((N_HEADS, tm, n), lambda i: (0, i, 0))
    return _pc_carrying(
        body, carry, (cqn, ckvn, krr, w_uq, w_ukv, cc, sn, sp), name="qkv_proj", grid=(s // tm,),
        in_specs=[row(CQ), row(CKV), row(128), head_w(w_uq), head_w(w_ukv), row(128), row(128), row(128)],
        out_specs=[head_o(HEAD_PAD), head_o(HEAD_PAD), head_o(V_DIM)],
        out_shape=[_sds((N_HEADS, s, HEAD_PAD), BF16), _sds((N_HEADS, s, HEAD_PAD), BF16),
                   _sds((N_HEADS, s, V_DIM), BF16)],
    )


def _chunk_bias(bq, bk, first_key=0):
    rows = lax.broadcasted_iota(jnp.int32, (bq, bk), 0)
    cols = lax.broadcasted_iota(jnp.int32, (bq, bk), 1) + first_key
    visible = jnp.right_shift(cols, CHUNK_SHIFT) <= jnp.right_shift(rows, CHUNK_SHIFT)
    return jnp.where(visible, 0.0, NEG).astype(F32)


_ATTN_FWD_ROWS = 1024


def _attn_fwd(q, k, v, carry=None):
    s = q.shape[1]
    bk = _row_block(s)
    bq = min(_ATTN_FWD_ROWS, s)
    nd = bq // bk

    def body(q_ref, k_ref, v_ref, o_ref, lse_ref, bias_ref):
        i = pl.program_id(1)
        qb = q_ref[0]

        @pl.when((pl.program_id(0) == 0) & (i == 0))
        def _():
            for d in range(nd):
                bias_ref[d] = _chunk_bias(bq, bk, d * bk)

        def step(j, carry, bias=None):
            m, l, acc = carry
            start = pl.multiple_of(j * bk, bk)
            kb = k_ref[0, pl.ds(start, bk), :]
            vb = v_ref[0, pl.ds(start, bk), :]
            sc = _dot_nt(qb, kb)
            if bias is not None:
                sc = sc + bias
            m_new = jnp.maximum(m, jnp.max(sc, axis=-1, keepdims=True))
            p = jnp.exp2(sc - m_new)
            alpha = jnp.exp2(m - m_new)
            l = alpha * l + jnp.sum(p, axis=-1, keepdims=True)
            acc = alpha * acc + _dot(p.astype(BF16), vb)
            return m_new, l, acc

        carry = (jnp.full((bq, 1), NEG, F32), jnp.zeros((bq, 1), F32), jnp.zeros((bq, V_DIM), F32))
        carry = lax.fori_loop(0, i * nd, step, carry)
        for d in range(nd):
            carry = step(i * nd + d, carry, bias_ref[d])
        m, l, acc = carry
        o_ref[...] = (acc / l).astype(BF16)
        lse_ref[0] = jnp.broadcast_to(m + jnp.log(l) * LOG2_E, (bq, 128))

    return _pc_carrying(
        body, carry, (q, k, v), name="attn_fwd", grid=(N_HEADS, s // bq),
        in_specs=[pl.BlockSpec((1, bq, HEAD_PAD), lambda h, i: (h, i, 0)),
                  pl.BlockSpec((1, s, HEAD_PAD), lambda h, i: (h, 0, 0)),
                  pl.BlockSpec((1, s, V_DIM), lambda h, i: (h, 0, 0))],
        out_specs=[pl.BlockSpec((bq, V_DIM), lambda h, i: (i, h)),
                   pl.BlockSpec((1, bq, 128), lambda h, i: (h, i, 0))],
        out_shape=[_sds((s, N_HEADS * V_DIM), BF16), _sds((N_HEADS, s, 128), F32)],
        scratch_shapes=[pltpu.VMEM((nd, bq, bk), F32)],
    )


def _matmul_res(a, w, res, name):
    s, kd = a.shape
    n = w.shape[1]
    tm = _row_block(s)

    def body(a_ref, w_ref, r_ref, o_ref):
        o_ref[...] = r_ref[...] + _dot(a_ref[...], w_ref[...])

    return _pc(
        body, name=name, grid=(s // tm,),
        in_specs=[pl.BlockSpec((tm, kd), lambda i: (i, 0)), pl.BlockSpec((kd, n), lambda i: (0, 0)),
                  pl.BlockSpec((tm, n), lambda i: (i, 0))],
        out_specs=pl.BlockSpec((tm, n), lambda i: (i, 0)),
        out_shape=_sds((s, n), F32),
    )(a, w, res)


def _ffn_fwd(h, gain, wg, wu, wd, name, carry=None):
    s = h.shape[0]
    tm = min(_FFN_ROWS, s)
    tf = 512
    nf = FF_PAD // tf

    def body(h_ref, g_ref, wg_ref, wu_ref, wd_ref, o_ref, n_ref, gate_ref, up_ref, acc_ref):
        j = pl.program_id(1)

        @pl.when(j == 0)
        def _():
            n_ref[...] = _rms(h_ref[...], g_ref[...]).astype(BF16)
            acc_ref[...] = jnp.zeros_like(acc_ref)

        nb = n_ref[...]
        gate = _dot(nb, wg_ref[...])
        up = _dot(nb, wu_ref[...])
        gate_ref[...] = gate.astype(BF16)
        up_ref[...] = up.astype(BF16)
        act = gate * jax.nn.sigmoid(gate) * up
        acc_ref[...] += _dot(act.astype(BF16), wd_ref[...])

        @pl.when(j == nf - 1)
        def _():
            o_ref[...] = h_ref[...] + acc_ref[...]

    return _pc_carrying(
        body, carry, (h, gain, wg, wu, wd), name=name, grid=(s // tm, nf),
        in_specs=[pl.BlockSpec((tm, D_MODEL), lambda i, j: (i, 0)), pl.BlockSpec((1, D_MODEL), lambda i, j: (0, 0)),
                  pl.BlockSpec((D_MODEL, tf), lambda i, j: (0, j)), pl.BlockSpec((D_MODEL, tf), lambda i, j: (0, j)),
                  pl.BlockSpec((tf, D_MODEL), lambda i, j: (j, 0))],
        out_specs=[pl.BlockSpec((tm, D_MODEL), lambda i, j: (i, 0)), pl.BlockSpec((tm, D_MODEL), lambda i, j: (i, 0)),
                   pl.BlockSpec((tm, tf), lambda i, j: (i, j)), pl.BlockSpec((tm, tf), lambda i, j: (i, j))],
        out_shape=[_sds((s, D_MODEL), F32), _sds((s, D_MODEL), BF16), _sds((s, FF_PAD), BF16),
                   _sds((s, FF_PAD), BF16)],
        scratch_shapes=[pltpu.VMEM((tm, D_MODEL), F32)],
    )


def _rms_matmul(h, gain, w, name):
    s = h.shape[0]
    n = w.shape[1]
    tm = min(_FFN_ROWS, s)
    tn = 1024
    nn = n // tn

    def body(h_ref, g_ref, w_ref, n_ref, o_ref):
        @pl.when(pl.program_id(1) == 0)
        def _():
            n_ref[...] = _rms(h_ref[...], g_ref[...]).astype(BF16)

        o_ref[...] = _dot(n_ref[...], w_ref[...]).astype(BF16)

    return _pc(
        body, name=name, grid=(s // tm, nn),
        in_specs=[pl.BlockSpec((tm, D_MODEL), lambda i, j: (i, 0)), pl.BlockSpec((1, D_MODEL), lambda i, j: (0, 0)),
                  pl.BlockSpec((D_MODEL, tn), lambda i, j: (0, j))],
        out_specs=[pl.BlockSpec((tm, D_MODEL), lambda i, j: (i, 0)), pl.BlockSpec((tm, tn), lambda i, j: (i, j))],
        out_shape=[_sds((s, D_MODEL), BF16), _sds((s, n), BF16)],
    )(h, gain, w)


def _shift_down(u, k, rows):
    return jnp.where(rows >= k, pltpu.roll(u, k, 0), 0.0)


def _shift_up(u, k, rows, s):
    return jnp.where(rows < s - k, pltpu.roll(u, s - k, 0), 0.0)


_CONV_COLS = 128


def _conv_fwd(bcx, cw):
    s = bcx.shape[0]
    tc = _CONV_COLS
    nc = D_MODEL // tc

    def body(b_ref, c_ref, xp_ref, w_ref, o_ref):
        rows = lax.broadcasted_iota(jnp.int32, (s, tc), 0)
        u = c_ref[...].astype(F32) * xp_ref[...].astype(F32)
        w = w_ref[...]
        uc = w[2:3] * u + w[1:2] * _shift_down(u, 1, rows) + w[0:1] * _shift_down(u, 2, rows)
        o_ref[...] = (b_ref[...].astype(F32) * uc).astype(BF16)

    col = lambda off: pl.BlockSpec((s, tc), lambda j: (0, off + j))
    return _pc(
        body, name="conv_fwd", grid=(nc,),
        in_specs=[col(0), col(nc), col(2 * nc), pl.BlockSpec((3, tc), lambda j: (0, j))],
        out_specs=pl.BlockSpec((s, tc), lambda j: (0, j)),
        out_shape=_sds((s, D_MODEL), BF16),
    )(bcx, bcx, bcx, cw)


def _final_loss(h, gain, target):
    s = h.shape[0]
    tm = _row_block(s)

    def body(h_ref, g_ref, t_ref, dh_ref, dg_ref, loss_ref):
        i = pl.program_id(0)

        @pl.when(i == 0)
        def _():
            dg_ref[...] = jnp.zeros_like(dg_ref)
            loss_ref[...] = jnp.zeros_like(loss_ref)

        hb = h_ref[...]
        e = _rms(hb, g_ref[...]) - t_ref[...]
        loss_ref[...] += 0.5 * jnp.sum(jnp.mean(e * e, axis=-1, keepdims=True))
        dx, dg = _rms_bwd(e * (1.0 / D_MODEL), hb, g_ref[...])
        dh_ref[...] = dx
        dg_ref[...] += dg

    row = pl.BlockSpec((tm, D_MODEL), lambda i: (i, 0))
    vec = pl.BlockSpec((1, D_MODEL), lambda i: (0, 0))
    return _pc(
        body, name="final_loss", grid=(s // tm,),
        in_specs=[row, vec, row],
        out_specs=[row, vec, pl.BlockSpec((8, D_MODEL), lambda i: (0, 0))],
        out_shape=[_sds((s, D_MODEL), F32), _sds((1, D_MODEL), F32), _sds((8, D_MODEL), F32)],
    )(h, gain, target)


def _ffn_bwd_x(dho, h, gain, gate, up, wg, wu, wd, name, carry=None):
    s = h.shape[0]
    tm = _row_block(s)
    tf = 1024
    nf = FF_PAD // tf

    def body(dho_ref, h_ref, g_ref, gate_ref, up_ref, wg_ref, wu_ref, wd_ref,
             dh_ref, dhb_ref, dhob_ref, dgate_ref, dup_ref, act_ref, dgain_ref, acc_ref):
        i = pl.program_id(0)
        j = pl.program_id(1)

        @pl.when(j == 0)
        def _():
            dhob_ref[...] = dho_ref[...].astype(BF16)
            acc_ref[...] = jnp.zeros_like(acc_ref)

        @pl.when((i == 0) & (j == 0))
        def _():
            dgain_ref[...] = jnp.zeros_like(dgain_ref)

        dact = _dot_nt(dhob_ref[...], wd_ref[...])
        g = gate_ref[...].astype(F32)
        u = up_ref[...].astype(F32)
        sg = jax.nn.sigmoid(g)
        silu = g * sg
        dg = (dact * u * (sg * (1.0 + g * (1.0 - sg)))).astype(BF16)
        du = (dact * silu).astype(BF16)
        dgate_ref[...] = dg
        dup_ref[...] = du
        act_ref[...] = (silu * u).astype(BF16)
        acc_ref[...] += _dot_nt(dg, wg_ref[...]) + _dot_nt(du, wu_ref[...])

        @pl.when(j == nf - 1)
        def _():
            dx, dgain = _rms_bwd(acc_ref[...], h_ref[...], g_ref[...])
            dh = dho_ref[...] + dx
            dh_ref[...] = dh
            dhb_ref[...] = dh.astype(BF16)
            dgain_ref[...] += dgain

    row = pl.BlockSpec((tm, D_MODEL), lambda i, j: (i, 0))
    vec = pl.BlockSpec((1, D_MODEL), lambda i, j: (0, 0))
    hid = pl.BlockSpec((tm, tf), lambda i, j: (i, j))
    wcol = pl.BlockSpec((D_MODEL, tf), lambda i, j: (0, j))
    wrow = pl.BlockSpec((tf, D_MODEL), lambda i, j: (j, 0))
    return _pc_carrying(
        body, carry, (dho, h, gain, gate, up, wg, wu, wd), name=name, grid=(s // tm, nf),
        in_specs=[row, row, vec, hid, hid, wcol, wcol, wrow],
        out_specs=[row, row, row, hid, hid, hid, vec],
        out_shape=[_sds((s, D_MODEL), F32), _sds((s, D_MODEL), BF16), _sds((s, D_MODEL), BF16),
                   _sds((s, FF_PAD), BF16), _sds((s, FF_PAD), BF16), _sds((s, FF_PAD), BF16),
                   _sds((1, D_MODEL), F32)],
        scratch_shapes=[pltpu.VMEM((tm, D_MODEL), F32)],
    )


def _nt_rmsbwd(a, w, h, gain, dho, name, carry=None):
    stacked = a.ndim == 3
    if stacked:
        nk, s, tk = a.shape
    else:
        s, tk = a.shape
        nk = 1
    tm = _row_block(s)

    def body(a_ref, w_ref, h_ref, g_ref, dho_ref, dh_ref, dhb_ref, dgain_ref, acc_ref):
        i = pl.program_id(0)
        j = pl.program_id(1)
        if stacked:
            a_ref = a_ref.at[0]

        @pl.when(j == 0)
        def _():
            acc_ref[...] = jnp.zeros_like(acc_ref)

        @pl.when((i == 0) & (j == 0))
        def _():
            dgain_ref[...] = jnp.zeros_like(dgain_ref)

        acc_ref[...] += _dot_nt(a_ref[...], w_ref[...])

        @pl.when(j == nk - 1)
        def _():
            dx, dgain = _rms_bwd(acc_ref[...], h_ref[...], g_ref[...])
            dh = dho_ref[...] + dx
            dh_ref[...] = dh
            dhb_ref[...] = dh.astype(BF16)
            dgain_ref[...] += dgain

    row = pl.BlockSpec((tm, D_MODEL), lambda i, j: (i, 0))
    vec = pl.BlockSpec((1, D_MODEL), lambda i, j: (0, 0))
    return _pc_carrying(
        body, carry, (a, w, h, gain, dho), name=name, grid=(s // tm, nk),
        in_specs=[pl.BlockSpec((1, tm, tk), lambda i, j: (j, i, 0)) if stacked else pl.BlockSpec((tm, tk), lambda i, j: (i, 0)),
                  pl.BlockSpec((D_MODEL, tk), lambda i, j: (0, j)), row, vec, row],
        out_specs=[row, row, vec],
        out_shape=[_sds((s, D_MODEL), F32), _sds((s, D_MODEL), BF16), _sds((1, D_MODEL), F32)],
        scratch_shapes=[pltpu.VMEM((tm, D_MODEL), F32)],
    )


def _matmul_nt(a, w, name):
    s, kd = a.shape
    n = w.shape[0]
    tm = _row_block(s)

    def body(a_ref, w_ref, o_ref):
        o_ref[...] = _dot_nt(a_ref[...], w_ref[...]).astype(BF16)

    return _pc(
        body, name=name, grid=(s // tm,),
        in_specs=[pl.BlockSpec((tm, kd), lambda i: (i, 0)), pl.BlockSpec((n, kd), lambda i: (0, 0))],
        out_specs=pl.BlockSpec((tm, n), lambda i: (i, 0)),
        out_shape=_sds((s, n), BF16),
    )(a, w)


def _matmul_tn(a, b, name, carry=None):
    s, m = a.shape
    stacked = b.ndim == 3
    tmm = min(m, 1024)
    if stacked:
        n, tn = b.shape[0] * b.shape[2], b.shape[2]
    else:
        n = b.shape[1]
        tn = n if n <= 1024 else 1024
    tk = min(_TN_ROWS, s)
    nk = s // tk

    def body(a_ref, b_ref, o_ref, acc_ref):
        k = pl.program_id(2)
        if stacked:
            b_ref = b_ref.at[0]

        @pl.when(k == 0)
        def _():
            acc_ref[...] = jnp.zeros_like(acc_ref)

        acc_ref[...] += _dot_tn(a_ref[...], b_ref[...])

        @pl.when(k == nk - 1)
        def _():
            o_ref[...] = acc_ref[...].astype(BF16)

    (out,), got = _pc_carrying(
        body, carry, (a, b), name=name, grid=(m // tmm, n // tn, nk),
        in_specs=[pl.BlockSpec((tk, tmm), lambda i, j, k: (k, i)),
                  pl.BlockSpec((1, tk, tn), lambda i, j, k: (j, k, 0)) if stacked
                  else pl.BlockSpec((tk, tn), lambda i, j, k: (k, j))],
        out_specs=[pl.BlockSpec((tmm, tn), lambda i, j, k: (i, j))],
        out_shape=[_sds((m, n), BF16)],
        scratch_shapes=[pltpu.VMEM((tmm, tn), F32)],
    )
    return out if carry is None else (out, got)


def _conv_bwd(dbuc, bcx, cw):
    s = bcx.shape[0]
    tc = _CONV_COLS
    nc = D_MODEL // tc

    def body(d_ref, b_ref, c_ref, xp_ref, w_ref, dbcx_ref, dw_ref):
        rows = lax.broadcasted_iota(jnp.int32, (s, tc), 0)
        c = c_ref[...].astype(F32)
        xp = xp_ref[...].astype(F32)
        u = c * xp
        u1 = _shift_down(u, 1, rows)
        u2 = _shift_down(u, 2, rows)
        w = w_ref[...]
        uc = w[2:3] * u + w[1:2] * u1 + w[0:1] * u2
        d = d_ref[...].astype(F32)
        dbcx_ref[0] = (d * uc).astype(BF16)
        duc = d * b_ref[...].astype(F32)
        du = w[2:3] * duc + w[1:2] * _shift_up(duc, 1, rows, s) + w[0:1] * _shift_up(duc, 2, rows, s)
        dbcx_ref[1] = (du * xp).astype(BF16)
        dbcx_ref[2] = (du * c).astype(BF16)
        dw_ref[0:1, :] = jnp.sum(duc * u2, axis=0, keepdims=True)
        dw_ref[1:2, :] = jnp.sum(duc * u1, axis=0, keepdims=True)
        dw_ref[2:3, :] = jnp.sum(duc * u, axis=0, keepdims=True)

    col = lambda off: pl.BlockSpec((s, tc), lambda j: (0, off + j))
    tap = pl.BlockSpec((3, tc), lambda j: (0, j))
    return _pc(
        body, name="conv_bwd", grid=(nc,),
        in_specs=[col(0), col(0), col(nc), col(2 * nc), tap],
        out_specs=[pl.BlockSpec((3, s, tc), lambda j: (0, 0, j)), tap],
        out_shape=[_sds((3, s, D_MODEL), BF16), _sds((3, D_MODEL), F32)],
    )(dbuc, bcx, bcx, bcx, cw)


def _attn_bwd(q, k, v, do, o, lse, cc, sn, sp, carry=None):
    s = q.shape[1]
    bk = _row_block(s)
    nb = s // bk

    def body(q_ref, k_ref, v_ref, do_ref, o_ref, lse_ref, cc_ref, sn_ref, sp_ref,
             dq_ref, dk_ref, dv_ref, dqacc_ref, delta_ref, bias_ref):
        j = pl.program_id(1)
        kb = k_ref[0]
        vb = v_ref[0]

        @pl.when((pl.program_id(0) == 0) & (j == 0))
        def _():
            bias_ref[...] = _chunk_bias(bk, bk)

        @pl.when(j == 0)
        def _():
            dqacc_ref[...] = jnp.zeros_like(dqacc_ref)

            def fill(i, _):
                rows = pl.ds(pl.multiple_of(i * bk, bk), bk)
                d = jnp.sum(do_ref[rows, :].astype(F32) * o_ref[rows, :].astype(F32), axis=-1, keepdims=True)
                delta_ref[rows, :] = jnp.broadcast_to(d, (bk, 128))
                return 0

            lax.fori_loop(0, nb, fill, 0)

        def step(i, carry, masked):
            dk, dv = carry
            rows = pl.ds(pl.multiple_of(i * bk, bk), bk)
            qb = q_ref[0, rows, :]
            dob = do_ref[rows, :]
            sc = _dot_nt(qb, kb)
            if masked:
                sc = sc + bias_ref[...]
            p = jnp.exp2(sc - lse_ref[0, rows, :][:, :1])
            dv = dv + _dot_tn(p.astype(BF16), dob)
            ds = (p * (_dot_nt(dob, vb) - delta_ref[rows, :][:, :1])).astype(BF16)
            dk = dk + _dot_tn(ds, qb)
            dqacc_ref[rows, :] += _dot(ds, kb)
            return dk, dv

        carry = step(j, (jnp.zeros((bk, HEAD_PAD), F32), jnp.zeros((bk, V_DIM), F32)), True)
        dq = dqacc_ref[pl.ds(pl.multiple_of(j * bk, bk), bk), :] * ATT_SCALE
        dq_ref[:, :NOPE] = dq[:, :NOPE].astype(BF16)
        dq_ref[:, NOPE:] = _rope_bwd(dq[:, NOPE:], cc_ref[...], sn_ref[...], sp_ref[...]).astype(BF16)
        dk, dv = lax.fori_loop(j + 1, nb, lambda i, c: step(i, c, False), carry)
        dk_ref[0] = (dk * LN_2).astype(BF16)
        dv_ref[0] = dv.astype(BF16)

    blk = lambda n: pl.BlockSpec((1, bk, n), lambda h, j: (h, j, 0))
    whole = lambda n: pl.BlockSpec((1, s, n), lambda h, j: (h, 0, 0))
    cols = pl.BlockSpec((s, V_DIM), lambda h, j: (0, h))
    tab = pl.BlockSpec((bk, 128), lambda h, j: (j, 0))
    return _pc_carrying(
        body, carry, (q, k, v, do, o, lse, cc, sn, sp), name="attn_bwd", grid=(N_HEADS, nb),
        in_specs=[whole(HEAD_PAD), blk(HEAD_PAD), blk(V_DIM), cols, cols, whole(128), tab, tab, tab],
        out_specs=[pl.BlockSpec((bk, HEAD_PAD), lambda h, j: (j, h)), blk(HEAD_PAD), blk(V_DIM)],
        out_shape=[_sds((s, N_HEADS * HEAD_PAD), BF16), _sds((N_HEADS, s, HEAD_PAD), BF16),
                   _sds((N_HEADS, s, V_DIM), BF16)],
        scratch_shapes=[pltpu.VMEM((s, HEAD_PAD), F32), pltpu.VMEM((s, 128), F32), pltpu.VMEM((bk, bk), F32)],
    )


def _mla_mid_bwd(dq, dk, dv, proj, w_uq, w_ukv, g_cq, g_ckv, cc, sn, sp, carry=None):
    s = dq.shape[0]
    tm = _row_block(s)

    def body(dq_ref, dk_ref, dv_ref, proj_ref, wuq_ref, wukv_ref, gcq_ref, gckv_ref, cc_ref, sn_ref, sp_ref,
             dproj_ref, dkv_ref, dgcq_ref, dgckv_ref, acq_ref, ackv_ref, akr_ref):
        @pl.when(pl.program_id(0) == 0)
        def _():
            dgcq_ref[...] = jnp.zeros_like(dgcq_ref)
            dgckv_ref[...] = jnp.zeros_like(dgckv_ref)

        for h in range(N_HEADS):
            cols = slice(h * HEAD_PAD, (h + 1) * HEAD_PAD)
            dkb = dk_ref[h]
            dkv_ref[:, h * HEAD_PAD:h * HEAD_PAD + NOPE] = dkb[:, :NOPE]
            dkv_ref[:, h * HEAD_PAD + NOPE:(h + 1) * HEAD_PAD] = dv_ref[h]
            cq_part = _dot_nt(dq_ref[:, cols], wuq_ref[h])
            ckv_part = _dot_nt(dkv_ref[:, cols], wukv_ref[h])
            kr_part = dkb[:, NOPE:].astype(F32)
            if h == 0:
                acq_ref[...], ackv_ref[...], akr_ref[...] = cq_part, ckv_part, kr_part
            else:
                acq_ref[...] += cq_part
                ackv_ref[...] += ckv_part
                akr_ref[...] += kr_part

        proj = proj_ref[...]
        dcq, dgcq = _rms_bwd(acq_ref[...], proj[:, :CQ], gcq_ref[...])
        dckv, dgckv = _rms_bwd(ackv_ref[...], proj[:, CQ:CQ + CKV], gckv_ref[...])
        dproj_ref[:, :CQ] = dcq.astype(BF16)
        dproj_ref[:, CQ:CQ + CKV] = dckv.astype(BF16)
        dproj_ref[:, CQ + CKV:] = _rope_bwd(akr_ref[...], cc_ref[...], sn_ref[...], sp_ref[...]).astype(BF16)
        dgcq_ref[...] += dgcq
        dgckv_ref[...] += dgckv

    head_blk = lambda n: pl.BlockSpec((N_HEADS, tm, n), lambda i: (0, i, 0))
    head_cols = pl.BlockSpec((tm, N_HEADS * HEAD_PAD), lambda i: (i, 0))
    head_w = lambda a: pl.BlockSpec(a.shape, lambda i: (0, 0, 0))
    row = lambda n: pl.BlockSpec((tm, n), lambda i: (i, 0))
    vec = lambda n: pl.BlockSpec((1, n), lambda i: (0, 0))
    return _pc_carrying(
        body, carry, (dq, dk, dv, proj, w_uq, w_ukv, g_cq, g_ckv, cc, sn, sp),
        name="mla_mid_bwd", grid=(s // tm,),
        in_specs=[head_cols, head_blk(HEAD_PAD), head_blk(V_DIM), row(PROJ_PAD), head_w(w_uq), head_w(w_ukv),
                  vec(CQ), vec(CKV), row(128), row(128), row(128)],
        out_specs=[row(PROJ_PAD), head_cols, vec(CQ), vec(CKV)],
        out_shape=[_sds((s, PROJ_PAD), BF16), _sds((s, N_HEADS * HEAD_PAD), BF16), _sds((1, CQ), F32),
                   _sds((1, CKV), F32)],
        scratch_shapes=[pltpu.VMEM((tm, CQ), F32), pltpu.VMEM((tm, CKV), F32), pltpu.VMEM((tm, 128), F32)],
    )


def _peer(k):
    x, y, c = lax.axis_index("x"), lax.axis_index("y"), lax.axis_index("c")
    px = 1 - x if k & 4 else x
    py = 1 - y if k & 2 else y
    pc = 1 - c if k & 1 else c
    return (px, py, pc), 4 * px + 2 * py + pc


def _exchange_ops(arrays, scatter):
    nw = len(arrays)
    by_cols = [isinstance(a, _Cols) for a in arrays]
    arrays = [a.array if isinstance(a, _Cols) else a for a in arrays]
    direct = range(1, N_DEV) if scatter else (1, 2, 4, 6)

    def columns(ref, idx, width):
        return ref.at[:, pl.ds(pl.multiple_of(idx * width, 128), width)]

    def sent(ins, w, idx):
        if not scatter:
            return ins[w]
        return columns(ins[w], idx, arrays[w].shape[1] // N_DEV) if by_cols[w] else ins[w].at[idx]

    def slot(outs, w, idx):
        if by_cols[w] and not scatter:
            return columns(outs[w], idx, arrays[w].shape[1])
        return outs[w].at[idx]

    def copy(w, k, src, dst, to, send_sems, recv_sems):
        return pltpu.make_async_remote_copy(
            src_ref=src, dst_ref=dst, send_sem=send_sems.at[w * N_DEV + k], recv_sem=recv_sems.at[w * N_DEV + k],
            device_id=to, device_id_type=pl.DeviceIdType.MESH)

    def own_copies(ins, outs, local_sems):
        _, me = _peer(0)
        return [pltpu.make_async_copy(sent(ins, w, me), slot(outs, w, me), local_sems.at[w]) for w in range(nw)]

    def sends(ins, outs, send_sems, recv_sems):
        _, me = _peer(0)
        out = []
        for k in direct:
            dev, idx = _peer(k)
            for w in range(nw):
                out.append(copy(w, k, sent(ins, w, idx), slot(outs, w, me), dev, send_sems, recv_sems))
        return out

    def relays(outs, send_sems, recv_sems):
        sibling, _ = _peer(1)
        out = []
        for k in (2, 4, 6):
            _, idx = _peer(k)
            for w in range(nw):
                out.append(copy(w, k + 1, slot(outs, w, idx), slot(outs, w, idx), sibling, send_sems, recv_sems))
        return out

    def arrival(outs, w, k, send_sems, recv_sems):
        dev, idx = _peer(k)
        return copy(w, k, slot(outs, w, idx), slot(outs, w, idx), dev, send_sems, recv_sems)

    def start(ins, outs, send_sems, recv_sems, local_sems):
        for cp in own_copies(ins, outs, local_sems) + sends(ins, outs, send_sems, recv_sems):
            cp.start()

    def relay(ins, outs, send_sems, recv_sems, local_sems):
        for k in (2, 4, 6):
            for w in range(nw):
                arrival(outs, w, k, send_sems, recv_sems).wait_recv()
        for cp in relays(outs, send_sems, recv_sems):
            cp.start()

    def wait(ins, outs, send_sems, recv_sems, local_sems):
        for cp in own_copies(ins, outs, local_sems):
            cp.wait()
        for cp in sends(ins, outs, send_sems, recv_sems) + ([] if scatter else relays(outs, send_sems, recv_sems)):
            cp.wait_send()
        for k in (range(1, N_DEV) if scatter else (1, 3, 5, 7)):
            for w in range(nw):
                arrival(outs, w, k, send_sems, recv_sems).wait_recv()

    def landed_shape(a, cols):
        if not cols:
            return a.shape if scatter else (N_DEV,) + a.shape
        r, c = a.shape
        return (N_DEV, r, c // N_DEV) if scatter else (r, N_DEV * c)

    landed = [_sds(landed_shape(a, cols), a.dtype) for a, cols in zip(arrays, by_cols)]
    sems = [pltpu.SemaphoreType.DMA((nw * N_DEV,)), pltpu.SemaphoreType.DMA((nw * N_DEV,)),
            pltpu.SemaphoreType.DMA((nw,))]
    return start, (None if scatter else relay), wait, landed, sems, arrays


class _Cols:
    def __init__(self, array):
        self.array = array


def _exchange(arrays, scatter, name):
    nw = len(arrays)
    start, relay, wait, landed, sems, arrays = _exchange_ops(arrays, scatter)

    def body(*refs):
        ins, outs, csem = refs[:nw], refs[nw:2 * nw], refs[2 * nw:]
        start(ins, outs, *csem)
        if relay is not None:
            relay(ins, outs, *csem)
        wait(ins, outs, *csem)

    any_spec = pl.BlockSpec(memory_space=pl.ANY)
    return _pc(body, name=name, in_specs=[any_spec] * nw, out_specs=[any_spec] * nw, out_shape=landed,
               scratch_shapes=sems)(*arrays)


def _adam_math(g, w, m, v):
    m = ADAM_B1 * m + (1.0 - ADAM_B1) * g
    v = ADAM_B2 * v + (1.0 - ADAM_B2) * jnp.square(g)
    m_hat = m / (1.0 - ADAM_B1 ** ADAM_STEP)
    v_hat = v / (1.0 - ADAM_B2 ** ADAM_STEP)
    delta = -ADAM_LR * (m_hat / (jnp.sqrt(v_hat) + ADAM_EPS) + ADAM_WD * w)
    return delta, m, v


def _adam_rows(r):
    for t in range(min(r, 512) // 16 * 16, 0, -16):
        if r % t == 0:
            return t
    return r


def _adamw(parts, w, m, v, name):
    nl, r, c = w.shape
    tr = _adam_rows(r)
    nr = r // tr

    def body(*refs):
        p_refs, (w_ref, m_ref, v_ref, g_ref, d_ref, mo_ref, vo_ref) = refs[:nl], refs[nl:]
        for layer in range(nl):
            @pl.when(pl.program_id(0) == layer)
            def _(p_ref=p_refs[layer]):
                g = p_ref[0].astype(F32)
                for src in range(1, N_DEV):
                    g = g + p_ref[src].astype(F32)
                delta, m2, v2 = _adam_math(g, w_ref[0], m_ref[0], v_ref[0])
                g_ref[0] = g
                d_ref[0] = delta
                mo_ref[0] = m2
                vo_ref[0] = v2

    def part_spec(layer):
        def index(l, i):
            return 0, jnp.where(l == layer, i, jnp.where(l < layer, 0, nr - 1)), 0
        return pl.BlockSpec((N_DEV, tr, c), index)

    blk = pl.BlockSpec((1, tr, c), lambda l, i: (l, i, 0))
    return _pc(
        body, name=name, grid=(nl, nr),
        in_specs=[part_spec(layer) for layer in range(nl)] + [blk, blk, blk],
        out_specs=[blk, blk, blk, blk],
        out_shape=[_sds((nl, r, c), F32)] * 4,
    )(*parts, w, m, v)


def _small_allreduce_adamw(gpack, wpack, mpack, vpack, last_grads):
    shape = gpack.shape
    compact = (SMALL_ROWS, D_MODEL)
    nw = len(last_grads)
    start, _, wait, landed, exchange_sems, last_grads = _exchange_ops(last_grads, True)

    def body(*refs):
        (g_ref, w_ref, m_ref, v_ref), rest = refs[:4], refs[4:]
        cin, rest = rest[:nw], rest[nw:]
        (go_ref, d_ref, mo_ref, vo_ref), rest = rest[:4], rest[4:]
        cout, rest = rest[:nw], rest[nw:]
        (comp_ref, gath_ref, send_sems, recv_sems), csem = rest[:4], rest[4:]
        start(cin, cout, *csem)
        _, me = _peer(0)
        comp_ref[...] = jnp.zeros(compact, F32)
        r = 0
        for p, n in enumerate(SMALL_PIECES):
            comp_ref[r:r + n, :] = g_ref[8 * p:8 * p + n, :]
            r += n
        gath_ref[me] = comp_ref[...]
        copies = []
        for k in range(1, N_DEV):
            dev, idx = _peer(k)
            copies.append(pltpu.make_async_remote_copy(
                src_ref=comp_ref, dst_ref=gath_ref.at[me], send_sem=send_sems.at[k], recv_sem=recv_sems.at[k],
                device_id=dev, device_id_type=pl.DeviceIdType.MESH))
        for cp in copies:
            cp.start()
        for cp in copies:
            cp.wait_send()
        for k in range(1, N_DEV):
            dev, idx = _peer(k)
            pltpu.make_async_remote_copy(
                src_ref=comp_ref, dst_ref=gath_ref.at[idx], send_sem=send_sems.at[k], recv_sem=recv_sems.at[k],
                device_id=dev, device_id_type=pl.DeviceIdType.MESH).wait_recv()
        g = gath_ref[0]
        for src in range(1, N_DEV):
            g = g + gath_ref[src]
        go_ref[...] = jnp.zeros(shape, F32)
        r = 0
        for p, n in enumerate(SMALL_PIECES):
            go_ref[8 * p:8 * p + n, :] = g[r:r + n, :]
            r += n
        delta, m2, v2 = _adam_math(go_ref[...], w_ref[...], m_ref[...], v_ref[...])
        d_ref[...] = delta
        mo_ref[...] = m2
        vo_ref[...] = v2
        wait(cin, cout, *csem)

    vm = pl.BlockSpec(memory_space=pltpu.VMEM)
    any_spec = pl.BlockSpec(memory_space=pl.ANY)
    res = _pc(
        body, name="small_allreduce_adamw",
        in_specs=[vm] * 4 + [any_spec] * nw, out_specs=[vm] * 4 + [any_spec] * nw,
        out_shape=[_sds(shape, F32)] * 4 + landed,
        scratch_shapes=[pltpu.VMEM(compact, F32), pltpu.VMEM((N_DEV,) + compact, F32),
                        pltpu.SemaphoreType.DMA((N_DEV,)), pltpu.SemaphoreType.DMA((N_DEV,))] + exchange_sems,
    )(gpack, wpack, mpack, vpack, *last_grads)
    return res[:4], res[4:]


def _rows_from_shards(g):
    return g.reshape(N_DEV * g.shape[1], g.shape[2])


def _shards_from_rows(a):
    return a.reshape(N_DEV, a.shape[0] // N_DEV, a.shape[1])


def _pad_to(a, rows, cols):
    return jnp.pad(a, ((0, rows - a.shape[0]), (0, cols - a.shape[1])))


def _rope_tables(pos):
    inv_freq = 1.0 / (ROPE_THETA ** (jnp.arange(0, ROPE, 2, dtype=F32) / ROPE))
    ang = pos.astype(F32)[:, None] * inv_freq
    cos, sin = jnp.cos(ang), jnp.sin(ang)
    z32, z64, z96 = (jnp.zeros((pos.shape[0], n), F32) for n in (32, 64, 96))
    return (jnp.concatenate([cos, cos, z64], axis=1), jnp.concatenate([-sin, z96], axis=1),
            jnp.concatenate([z32, sin, z64], axis=1))


def _pad_row(vec):
    vec = vec.reshape(1, -1)
    return jnp.pad(vec, ((0, 0), (0, D_MODEL - vec.shape[1])))


def _forward_backward(x, target, tables, gains, shards):
    cc, sn, sp = tables
    ffn_g = gains["ffn_norm"]
    by_cols = ("wg0", "wu0", "wg1", "wu1", "c_in")

    def gather(names, relay_at):
        return [_Cols(shards[n]) if n in by_cols else shards[n] for n in names], False, relay_at

    (got,) = _exchange([shards["w_in"]], False, "gather_w_in")
    w_in = _rows_from_shards(got)
    (n0, proj, cqn, ckvn, krr), (w_uq, w_ukv) = _mla_in_fwd(
        x, gains["mla_norm"], w_in, gains["g_cq"], gains["g_ckv"], cc, sn, sp, carry=gather(["w_uq", "w_ukv"], 0.5))
    (q, k, v), (w_o, conv_norm, conv_w) = _qkv_proj(
        cqn, ckvn, krr, w_uq, w_ukv, cc, sn, sp, carry=gather(["w_o", "conv_norm", "conv_w"], 0.5))
    w_o = _rows_from_shards(w_o)
    conv_norm = conv_norm.reshape(1, D_MODEL)
    conv_w = jnp.transpose(conv_w, (1, 0, 2)).reshape(3, D_MODEL)
    (o, lse), (wg0, wu0, wd0, c_in, c_out, wd1) = _attn_fwd(
        q, k, v, carry=gather(["wg0", "wu0", "wd0", "c_in", "c_out", "wd1"], 0.75))
    wd0 = _rows_from_shards(wd0)
    wd1 = _rows_from_shards(wd1)
    c_out = _rows_from_shards(c_out)
    h1 = _matmul_res(o, w_o, x, "mla_out_fwd")
    (h2, n1, gate0, up0), (wg1, wu1) = _ffn_fwd(
        h1, ffn_g[0:1], wg0, wu0, wd0, "ffn0_fwd", carry=gather(["wg1", "wu1"], 0.7))
    n2, bcx = _rms_matmul(h2, conv_norm, c_in, "conv_in_fwd")
    bu = _conv_fwd(bcx, conv_w)
    h3 = _matmul_res(bu, c_out, h2, "conv_out_fwd")
    (h4, n3, gate1, up1), _ = _ffn_fwd(h3, ffn_g[1:2], wg1, wu1, wd1, "ffn1_fwd")
    dh4, d_final, loss = _final_loss(h4, gains["final_norm"], target)

    small = {"final_norm": d_final}
    parts = {}

    def scatter(**blocks):
        return list(blocks), (list(blocks.values()), True, None)

    (dh3, dh3_b, dh4_b, dgate, dup, act, d_ffn1), _ = _ffn_bwd_x(
        dh4, h3, ffn_g[1:2], gate1, up1, wg1, wu1, wd1, "ffn1_bwd")
    dwg1 = _matmul_tn(n3, dgate, "ffn1_dwg")
    dwu1 = _matmul_tn(n3, dup, "ffn1_dwu")
    dwd1 = _matmul_tn(act, dh4_b, "ffn1_dwd")
    dbuc = _matmul_nt(dh3_b, c_out, "conv_out_bwd")
    d_c_out = _matmul_tn(bu, dh3_b, "conv_dwout")
    dbcx, small["conv_w"] = _conv_bwd(dbuc, bcx, conv_w)
    d_c_in = _matmul_tn(n2, dbcx, "conv_dwin")
    (dh2, dh2_b, small["conv_norm"]), _ = _nt_rmsbwd(dbcx, c_in, h2, conv_norm, dh3, "conv_in_bwd")
    names, carry = scatter(c_in=_Cols(d_c_in), c_out=_shards_from_rows(d_c_out))
    (dh1, dh1_b, dh2_b2, dgate, dup, act, d_ffn0), got = _ffn_bwd_x(
        dh2, h1, ffn_g[0:1], gate0, up0, wg0, wu0, wd0, "ffn0_bwd", carry=carry)
    parts.update(zip(names, got))
    dwg0 = _matmul_tn(n1, dgate, "ffn0_dwg")
    dwu0 = _matmul_tn(n1, dup, "ffn0_dwu")
    dwd0 = _matmul_tn(act, dh2_b2, "ffn0_dwd")
    small["ffn_norm"] = jnp.pad(d_ffn0, ((0, 7), (0, 0))) + jnp.pad(d_ffn1, ((1, 6), (0, 0)))
    do = _matmul_nt(dh1_b, w_o, "mla_out_bwd")
    d_w_o = _matmul_tn(o, dh1_b, "mla_dwo")
    names, carry = scatter(wg0=_Cols(dwg0), wg1=_Cols(dwg1), wu0=_Cols(dwu0), wu1=_Cols(dwu1),
                           wd0=_shards_from_rows(dwd0), wd1=_shards_from_rows(dwd1))
    (dq, dk, dv), got = _attn_bwd(q, k, v, do, o, lse, cc, sn, sp, carry=carry)
    parts.update(zip(names, got))
    d_w_uq = _matmul_tn(cqn, dq, "mla_dwuq")
    names, carry = scatter(w_o=_shards_from_rows(d_w_o), w_uq=_Cols(d_w_uq))
    (dproj, dkv, small["g_cq"], small["g_ckv"]), got = _mla_mid_bwd(
        dq, dk, dv, proj, w_uq, w_ukv, gains["g_cq"], gains["g_ckv"], cc, sn, sp, carry=carry)
    parts.update(zip(names, got))
    d_w_ukv = _matmul_tn(ckvn, dkv, "mla_dwukv")
    names, carry = scatter(w_ukv=_Cols(d_w_ukv))
    d_w_in, got = _matmul_tn(n0, dproj, "mla_dwin", carry=carry)
    parts.update(zip(names, got))
    (dx, _, small["mla_norm"]), _ = _nt_rmsbwd(dproj, w_in, x, gains["mla_norm"], dh1, "mla_in_bwd")
    return loss, dx, parts, _shards_from_rows(d_w_in), small


def kernel(x, positions, mla_norm, mla_w_in, mla_g_cq, mla_g_ckv, mla_w_uq, mla_w_ukv, mla_w_o, conv_norm, conv_w_in, conv_w, conv_w_out, ffn_norm, ffn_w_gate, ffn_w_up, ffn_w_down, final_norm, loss_target, m_mla_norm, m_mla_w_in, m_mla_g_cq, m_mla_g_ckv, m_mla_w_uq, m_mla_w_ukv, m_mla_w_o, m_conv_norm, m_conv_w_in, m_conv_w, m_conv_w_out, m_ffn_norm, m_ffn_w_gate, m_ffn_w_up, m_ffn_w_down, m_final_norm, v_mla_norm, v_mla_w_in, v_mla_g_cq, v_mla_g_ckv, v_mla_w_uq, v_mla_w_ukv, v_mla_w_o, v_conv_norm, v_conv_w_in, v_conv_w, v_conv_w_out, v_ffn_norm, v_ffn_w_gate, v_ffn_w_up, v_ffn_w_down, v_final_norm):
    me = 4 * lax.axis_index("x") + 2 * lax.axis_index("y") + lax.axis_index("c")

    bf = lambda a, rows, cols: _pad_to(a.astype(BF16), rows, cols)
    shards = dict(
        w_in=bf(mla_w_in[0], D_MODEL // N_DEV, PROJ_PAD), w_uq=bf(mla_w_uq[0], CQ, HEAD_PAD),
        w_ukv=mla_w_ukv[0].astype(BF16), w_o=mla_w_o[0].astype(BF16),
        c_in=conv_w_in[0].astype(BF16), c_out=conv_w_out[0].astype(BF16),
        conv_norm=conv_norm, conv_w=conv_w[0])
    for l in range(2):
        shards.update({f"wg{l}": bf(ffn_w_gate[l], D_MODEL, FF_SHARD_PAD), f"wu{l}": bf(ffn_w_up[l], D_MODEL, FF_SHARD_PAD),
                       f"wd{l}": bf(ffn_w_down[l], FF_SHARD_PAD, D_MODEL)})
    gains = dict(mla_norm=mla_norm, g_cq=mla_g_cq, g_ckv=mla_g_ckv, ffn_norm=ffn_norm,
                 final_norm=final_norm.reshape(1, -1))
    loss_local, dx, parts, d_w_in, grads = _forward_backward(
        x[0], loss_target[0], _rope_tables(positions[0]), gains, shards)

    col0 = me * (D_MODEL // N_DEV)

    def place(shard):
        return lax.dynamic_update_slice(jnp.zeros((shard.shape[0], D_MODEL), F32), shard, (0, col0))

    no_loss = jnp.zeros((8, D_MODEL), F32)

    def pack(mla_n, g_cq, g_ckv, ffn_n, fin_n, conv_n, conv_taps, loss_tile):
        rows = [mla_n, _pad_row(g_cq), _pad_row(g_ckv), ffn_n, fin_n.reshape(1, -1), conv_n, conv_taps, loss_tile]
        assert all(r.shape[0] in (n, 8) for r, n in zip(rows, SMALL_PIECES))
        return jnp.concatenate([jnp.pad(r, ((0, 8 - r.shape[0]), (0, 0))) for r in rows], axis=0)

    gpack = pack(grads["mla_norm"], grads["g_cq"], grads["g_ckv"], grads["ffn_norm"], grads["final_norm"],
                 grads["conv_norm"], grads["conv_w"], loss_local)
    wpack = pack(mla_norm, mla_g_cq, mla_g_ckv, ffn_norm, final_norm, place(conv_norm), place(conv_w[0]), no_loss)
    mpack = pack(m_mla_norm, m_mla_g_cq, m_mla_g_ckv, m_ffn_norm, m_final_norm, place(m_conv_norm),
                 place(m_conv_w[0]), no_loss)
    vpack = pack(v_mla_norm, v_mla_g_cq, v_mla_g_ckv, v_ffn_norm, v_final_norm, place(v_conv_norm),
                 place(v_conv_w[0]), no_loss)
    small, (parts["w_in"],) = _small_allreduce_adamw(gpack, wpack, mpack, vpack, [d_w_in])
    loss = small[0][56, 0]

    def adamw(name, w, m, v, partials):
        if w.ndim == 2:
            w, m, v = w[None], m[None], v[None]
        return [o.reshape(w.shape) for o in _adamw(partials, w, m, v, "adamw_" + name)]

    res = dict(
        w_in=adamw("w_in", mla_w_in, m_mla_w_in, v_mla_w_in, [parts["w_in"][:, :, :PROJ]]),
        w_uq=adamw("w_uq", mla_w_uq, m_mla_w_uq, v_mla_w_uq, [parts["w_uq"][:, :, :NOPE + ROPE]]),
        w_ukv=adamw("w_ukv", mla_w_ukv, m_mla_w_ukv, v_mla_w_ukv, [parts["w_ukv"]]),
        w_o=adamw("w_o", mla_w_o, m_mla_w_o, v_mla_w_o, [parts["w_o"]]),
        c_in=adamw("c_in", conv_w_in, m_conv_w_in, v_conv_w_in, [parts["c_in"]]),
        c_out=adamw("c_out", conv_w_out, m_conv_w_out, v_conv_w_out, [parts["c_out"]]),
        wg=adamw("wg", ffn_w_gate, m_ffn_w_gate, v_ffn_w_gate, [parts[n][:, :, :FF_SHARD] for n in ("wg0", "wg1")]),
        wu=adamw("wu", ffn_w_up, m_ffn_w_up, v_ffn_w_up, [parts[n][:, :, :FF_SHARD] for n in ("wu0", "wu1")]),
        wd=adamw("wd", ffn_w_down, m_ffn_w_down, v_ffn_w_down, [parts["wd0"], parts["wd1"]]),
    )

    def unpack(p):
        own = lambda rows: lax.dynamic_slice(rows, (0, col0), (rows.shape[0], D_MODEL // N_DEV))
        return dict(mla_norm=p[0:1], g_cq=p[8:9, :CQ], g_ckv=p[16:17, :CKV], ffn_norm=p[24:26], final_norm=p[32],
                    conv_norm=own(p[40:41]), conv_w=own(p[48:51])[None])

    small = [unpack(p) for p in small]
    order = ["mla_norm", "w_in", "g_cq", "g_ckv", "w_uq", "w_ukv", "w_o", "conv_norm", "c_in", "conv_w", "c_out",
             "ffn_norm", "wg", "wu", "wd", "final_norm"]
    out = [loss, dx[None]]
    for kind in range(4):
        for n in order:
            out.append(res[n][kind] if n in res else small[kind][n])
    return tuple(out)
```

```python
import math

import jax
import jax.numpy as jnp
from jax import lax
from jax.experimental import pallas as pl
from jax.experimental.pallas import tpu as pltpu

F32 = jnp.float32
BF16 = jnp.bfloat16

N_DEV = 8
D_MODEL = 1024
N_HEADS = 8
NOPE = 128
ROPE = 64
V_DIM = 128
HEAD_PAD = 256
CQ = 512
CKV = 256
PROJ = CQ + CKV + ROPE
PROJ_PAD = CQ + CKV + 128
D_FF = 2816
FF_SHARD = D_FF // N_DEV
FF_SHARD_PAD = 384
FF_PAD = FF_SHARD_PAD * N_DEV
CHUNK_SHIFT = 6
RMS_EPS = 1e-6
ROPE_THETA = 10000.0
ATT_SCALE = 1.0 / math.sqrt(NOPE + ROPE)
LOG2_E = math.log2(math.e)
LN_2 = math.log(2.0)
Q_SCALE = ATT_SCALE * LOG2_E
NEG = -1e30

ADAM_LR = 0.001
ADAM_B1 = 0.9
ADAM_B2 = 0.999
ADAM_EPS = 1e-08
ADAM_WD = 0.01
ADAM_STEP = 10

SMALL_PIECES = (1, 1, 1, 2, 1, 1, 3, 1)
SMALL_ROWS = 16

_NT = (((1,), (1,)), ((), ()))
_TN = (((0,), (0,)), ((), ()))


def _pc(body, *, name, out_shape, grid=(), in_specs=None, out_specs=None, scratch_shapes=(), vmem_mb=None):
    params = {}
    if vmem_mb is not None:
        params["vmem_limit_bytes"] = vmem_mb << 20
    kwargs = dict(
        name=name, out_shape=out_shape, grid=grid, scratch_shapes=scratch_shapes,
        compiler_params=pltpu.CompilerParams(**params),
    )
    if in_specs is not None:
        kwargs["in_specs"] = in_specs
    if out_specs is not None:
        kwargs["out_specs"] = out_specs
    return pl.pallas_call(body, **kwargs)


def _pc_carrying(body, carry, operands, *, name, out_shape, grid, in_specs, out_specs, scratch_shapes=()):
    if carry is None:
        return _pc(body, name=name, out_shape=out_shape, grid=grid, in_specs=in_specs, out_specs=out_specs,
                   scratch_shapes=scratch_shapes)(*operands), None
    arrays, scatter, relay_at = carry
    nw, n_in, n_out, n_scr = len(arrays), len(in_specs), len(out_shape), len(scratch_shapes)
    start, relay, wait, landed_shapes, sems, arrays = _exchange_ops(arrays, scatter)
    n_steps = math.prod(grid)
    relay_step = None if relay is None else max(1, min(int(relay_at * n_steps), n_steps - 2))

    def wrapped(*refs):
        ins, rest = refs[:n_in], refs[n_in:]
        cin, rest = rest[:nw], rest[nw:]
        outs, rest = rest[:n_out], rest[n_out:]
        cout, rest = rest[:nw], rest[nw:]
        scr, csem = rest[:n_scr], rest[n_scr:]
        step = pl.program_id(0)
        for a in range(1, len(grid)):
            step = step * grid[a] + pl.program_id(a)

        @pl.when(step == 0)
        def _():
            start(cin, cout, *csem)

        if relay is not None:
            @pl.when(step == relay_step)
            def _():
                relay(cin, cout, *csem)

        body(*ins, *outs, *scr)

        @pl.when(step == n_steps - 1)
        def _():
            wait(cin, cout, *csem)

    any_spec = pl.BlockSpec(memory_space=pl.ANY)
    res = _pc(
        wrapped, name=name, grid=grid,
        in_specs=list(in_specs) + [any_spec] * nw, out_specs=list(out_specs) + [any_spec] * nw,
        out_shape=list(out_shape) + landed_shapes, scratch_shapes=list(scratch_shapes) + sems,
    )(*operands, *arrays)
    return res[:n_out], res[n_out:]


def _sds(shape, dtype):
    return jax.ShapeDtypeStruct(shape, dtype)


def _dot(a, b):
    return jnp.dot(a, b, preferred_element_type=F32)


def _dot_nt(a, b):
    return lax.dot_general(a, b, _NT, preferred_element_type=F32)


def _dot_tn(a, b):
    return lax.dot_general(a, b, _TN, preferred_element_type=F32)


def _rstd(x):
    return lax.rsqrt(jnp.mean(x * x, axis=-1, keepdims=True) + RMS_EPS)


def _rms(x, g):
    return (x * _rstd(x)) * g


def _rms_bwd(dy, x, g):
    r = _rstd(x)
    xhat = x * r
    dxhat = dy * g
    dx = r * (dxhat - xhat * jnp.mean(dxhat * xhat, axis=-1, keepdims=True))
    return dx, jnp.sum(dy * xhat, axis=0, keepdims=True)


def _rope(t, cc, sn, sp):
    return t * cc + pltpu.roll(t, 96, 1) * sn + pltpu.roll(t, 32, 1) * sp


def _rope_bwd(dt, cc, sn, sp):
    return dt * cc + pltpu.roll(dt * sn, 32, 1) + pltpu.roll(dt * sp, 96, 1)


def _row_block(s):
    return min(512, s)


_TN_ROWS = 1024
_FFN_ROWS = 1024


def _mla_in_fwd(x, g0, w_in, g_cq, g_ckv, cc, sn, sp, carry=None):
    s = x.shape[0]
    tm = _row_block(s)

    def body(x_ref, g0_ref, w_ref, gcq_ref, gckv_ref, cc_ref, sn_ref, sp_ref,
             n_ref, proj_ref, cqn_ref, ckvn_ref, krr_ref):
        nb = _rms(x_ref[...], g0_ref[...]).astype(BF16)
        n_ref[...] = nb
        proj = _dot(nb, w_ref[...])
        proj_ref[...] = proj
        cqn_ref[...] = _rms(proj[:, :CQ], gcq_ref[...]).astype(BF16)
        ckvn_ref[...] = _rms(proj[:, CQ:CQ + CKV], gckv_ref[...]).astype(BF16)
        krr_ref[...] = _rope(proj[:, CQ + CKV:], cc_ref[...], sn_ref[...], sp_ref[...]).astype(BF16)

    row = lambda n: pl.BlockSpec((tm, n), lambda i: (i, 0))
    full = lambda a: pl.BlockSpec(a.shape, lambda i: (0, 0))
    return _pc_carrying(
        body, carry, (x, g0, w_in, g_cq, g_ckv, cc, sn, sp), name="mla_in_fwd", grid=(s // tm,),
        in_specs=[row(D_MODEL), full(g0), full(w_in), full(g_cq), full(g_ckv), row(128), row(128), row(128)],
        out_specs=[row(D_MODEL), row(PROJ_PAD), row(CQ), row(CKV), row(128)],
        out_shape=[_sds((s, D_MODEL), BF16), _sds((s, PROJ_PAD), F32), _sds((s, CQ), BF16),
                   _sds((s, CKV), BF16), _sds((s, 128), BF16)],
    )


def _qkv_proj(cqn, ckvn, krr, w_uq, w_ukv, cc, sn, sp, carry=None):
    s = cqn.shape[0]
    tm = _row_block(s)

    def body(cqn_ref, ckvn_ref, krr_ref, wuq_ref, wukv_ref, cc_ref, sn_ref, sp_ref, q_ref, k_ref, v_ref):
        cqn_b, ckvn_b, krr_b = cqn_ref[...], ckvn_ref[...], krr_ref[...]
        cc_b, sn_b, sp_b = cc_ref[...], sn_ref[...], sp_ref[...]
        for h in range(N_HEADS):
            q = _dot(cqn_b, wuq_ref[h]) * Q_SCALE
            q_ref[h, :, :NOPE] = q[:, :NOPE].astype(BF16)
            q_ref[h, :, NOPE:] = _rope(q[:, NOPE:], cc_b, sn_b, sp_b).astype(BF16)
            kv = _dot(ckvn_b, wukv_ref[h])
            k_ref[h, :, :NOPE] = kv[:, :NOPE].astype(BF16)
            k_ref[h, :, NOPE:] = krr_b
            v_ref[h] = kv[:, NOPE:].astype(BF16)

    row = lambda n: pl.BlockSpec((tm, n), lambda i: (i, 0))
    head_w = lambda a: pl.BlockSpec(a.shape, lambda i: (0, 0, 0))
    head_o = lambda n: pl.BlockSpec((N_HEADS, tm, n), lambda i: (0, i, 0))
    return _pc_carrying(
        body, carry, (cqn, ckvn, krr, w_uq, w_ukv, cc, sn, sp), name="qkv_proj", grid=(s // tm,),
        in_specs=[row(CQ), row(CKV), row(128), head_w(w_uq), head_w(w_ukv), row(128), row(128), row(128)],
        out_specs=[head_o(HEAD_PAD), head_o(HEAD_PAD), head_o(V_DIM)],
        out_shape=[_sds((N_HEADS, s, HEAD_PAD), BF16), _sds((N_HEADS, s, HEAD_PAD), BF16),
                   _sds((N_HEADS, s, V_DIM), BF16)],
    )


def _chunk_bias(bq, bk, first_key=0):
    rows = lax.broadcasted_iota(jnp.int32, (bq, bk), 0)
    cols = lax.broadcasted_iota(jnp.int32, (bq, bk), 1) + first_key
    visible = jnp.right_shift(cols, CHUNK_SHIFT) <= jnp.right_shift(rows, CHUNK_SHIFT)
    return jnp.where(visible, 0.0, NEG).astype(F32)


_ATTN_FWD_ROWS = 1024


def _attn_fwd(q, k, v, carry=None):
    s = q.shape[1]
    bk = _row_block(s)
    bq = min(_ATTN_FWD_ROWS, s)
    nd = bq // bk

    def body(q_ref, k_ref, v_ref, o_ref, lse_ref, bias_ref):
        i = pl.program_id(1)
        qb = q_ref[0]

        @pl.when((pl.program_id(0) == 0) & (i == 0))
        def _():
            for d in range(nd):
                bias_ref[d] = _chunk_bias(bq, bk, d * bk)

        def step(j, carry, bias=None):
            m, l, acc = carry
            start = pl.multiple_of(j * bk, bk)
            kb = k_ref[0, pl.ds(start, bk), :]
            vb = v_ref[0, pl.ds(start, bk), :]
            sc = _dot_nt(qb, kb)
            if bias is not None:
                sc = sc + bias
            m_new = jnp.maximum(m, jnp.max(sc, axis=-1, keepdims=True))
            p = jnp.exp2(sc - m_new)
            alpha = jnp.exp2(m - m_new)
            l = alpha * l + jnp.sum(p, axis=-1, keepdims=True)
            acc = alpha * acc + _dot(p.astype(BF16), vb)
            return m_new, l, acc

        carry = (jnp.full((bq, 1), NEG, F32), jnp.zeros((bq, 1), F32), jnp.zeros((bq, V_DIM), F32))
        carry = lax.fori_loop(0, i * nd, step, carry)
        for d in range(nd):
            carry = step(i * nd + d, carry, bias_ref[d])
        m, l, acc = carry
        o_ref[...] = (acc / l).astype(BF16)
        lse_ref[0] = jnp.broadcast_to(m + jnp.log(l) * LOG2_E, (bq, 128))

    return _pc_carrying(
        body, carry, (q, k, v), name="attn_fwd", grid=(N_HEADS, s // bq),
        in_specs=[pl.BlockSpec((1, bq, HEAD_PAD), lambda h, i: (h, i, 0)),
                  pl.BlockSpec((1, s, HEAD_PAD), lambda h, i: (h, 0, 0)),
                  pl.BlockSpec((1, s, V_DIM), lambda h, i: (h, 0, 0))],
        out_specs=[pl.BlockSpec((bq, V_DIM), lambda h, i: (i, h)),
                   pl.BlockSpec((1, bq, 128), lambda h, i: (h, i, 0))],
        out_shape=[_sds((s, N_HEADS * V_DIM), BF16), _sds((N_HEADS, s, 128), F32)],
        scratch_shapes=[pltpu.VMEM((nd, bq, bk), F32)],
    )


def _matmul_res(a, w, res, name):
    s, kd = a.shape
    n = w.shape[1]
    tm = _row_block(s)

    def body(a_ref, w_ref, r_ref, o_ref):
        o_ref[...] = r_ref[...] + _dot(a_ref[...], w_ref[...])

    return _pc(
        body, name=name, grid=(s // tm,),
        in_specs=[pl.BlockSpec((tm, kd), lambda i: (i, 0)), pl.BlockSpec((kd, n), lambda i: (0, 0)),
                  pl.BlockSpec((tm, n), lambda i: (i, 0))],
        out_specs=pl.BlockSpec((tm, n), lambda i: (i, 0)),
        out_shape=_sds((s, n), F32),
    )(a, w, res)


def _ffn_fwd(h, gain, wg, wu, wd, name, carry=None):
    s = h.shape[0]
    tm = min(_FFN_ROWS, s)
    tf = 512
    nf = FF_PAD // tf

    def body(h_ref, g_ref, wg_ref, wu_ref, wd_ref, o_ref, n_ref, gate_ref, up_ref, acc_ref):
        j = pl.program_id(1)

        @pl.when(j == 0)
        def _():
            n_ref[...] = _rms(h_ref[...], g_ref[...]).astype(BF16)
            acc_ref[...] = jnp.zeros_like(acc_ref)

        nb = n_ref[...]
        gate = _dot(nb, wg_ref[...])
        up = _dot(nb, wu_ref[...])
        gate_ref[...] = gate.astype(BF16)
        up_ref[...] = up.astype(BF16)
        act = gate * jax.nn.sigmoid(gate) * up
        acc_ref[...] += _dot(act.astype(BF16), wd_ref[...])

        @pl.when(j == nf - 1)
        def _():
            o_ref[...] = h_ref[...] + acc_ref[...]

    return _pc_carrying(
        body, carry, (h, gain, wg, wu, wd), name=name, grid=(s // tm, nf),
        in_specs=[pl.BlockSpec((tm, D_MODEL), lambda i, j: (i, 0)), pl.BlockSpec((1, D_MODEL), lambda i, j: (0, 0)),
                  pl.BlockSpec((D_MODEL, tf), lambda i, j: (0, j)), pl.BlockSpec((D_MODEL, tf), lambda i, j: (0, j)),
                  pl.BlockSpec((tf, D_MODEL), lambda i, j: (j, 0))],
        out_specs=[pl.BlockSpec((tm, D_MODEL), lambda i, j: (i, 0)), pl.BlockSpec((tm, D_MODEL), lambda i, j: (i, 0)),
                   pl.BlockSpec((tm, tf), lambda i, j: (i, j)), pl.BlockSpec((tm, tf), lambda i, j: (i, j))],
        out_shape=[_sds((s, D_MODEL), F32), _sds((s, D_MODEL), BF16), _sds((s, FF_PAD), BF16),
                   _sds((s, FF_PAD), BF16)],
        scratch_shapes=[pltpu.VMEM((tm, D_MODEL), F32)],
    )


def _rms_matmul(h, gain, w, name):
    s = h.shape[0]
    n = w.shape[1]
    tm = min(_FFN_ROWS, s)
    tn = 1024
    nn = n // tn

    def body(h_ref, g_ref, w_ref, n_ref, o_ref):
        @pl.when(pl.program_id(1) == 0)
        def _():
            n_ref[...] = _rms(h_ref[...], g_ref[...]).astype(BF16)

        o_ref[...] = _dot(n_ref[...], w_ref[...]).astype(BF16)

    return _pc(
        body, name=name, grid=(s // tm, nn),
        in_specs=[pl.BlockSpec((tm, D_MODEL), lambda i, j: (i, 0)), pl.BlockSpec((1, D_MODEL), lambda i, j: (0, 0)),
                  pl.BlockSpec((D_MODEL, tn), lambda i, j: (0, j))],
        out_specs=[pl.BlockSpec((tm, D_MODEL), lambda i, j: (i, 0)), pl.BlockSpec((tm, tn), lambda i, j: (i, j))],
        out_shape=[_sds((s, D_MODEL), BF16), _sds((s, n), BF16)],
    )(h, gain, w)


def _shift_down(u, k, rows):
    return jnp.where(rows >= k, pltpu.roll(u, k, 0), 0.0)


def _shift_up(u, k, rows, s):
    return jnp.where(rows < s - k, pltpu.roll(u, s - k, 0), 0.0)


_CONV_COLS = 128


def _conv_fwd(bcx, cw):
    s = bcx.shape[0]
    tc = _CONV_COLS
    nc = D_MODEL // tc

    def body(b_ref, c_ref, xp_ref, w_ref, o_ref):
        rows = lax.broadcasted_iota(jnp.int32, (s, tc), 0)
        u = c_ref[...].astype(F32) * xp_ref[...].astype(F32)
        w = w_ref[...]
        uc = w[2:3] * u + w[1:2] * _shift_down(u, 1, rows) + w[0:1] * _shift_down(u, 2, rows)
        o_ref[...] = (b_ref[...].astype(F32) * uc).astype(BF16)

    col = lambda off: pl.BlockSpec((s, tc), lambda j: (0, off + j))
    return _pc(
        body, name="conv_fwd", grid=(nc,),
        in_specs=[col(0), col(nc), col(2 * nc), pl.BlockSpec((3, tc), lambda j: (0, j))],
        out_specs=pl.BlockSpec((s, tc), lambda j: (0, j)),
        out_shape=_sds((s, D_MODEL), BF16),
    )(bcx, bcx, bcx, cw)


def _final_loss(h, gain, target):
    s = h.shape[0]
    tm = _row_block(s)

    def body(h_ref, g_ref, t_ref, dh_ref, dg_ref, loss_ref):
        i = pl.program_id(0)

        @pl.when(i == 0)
        def _():
            dg_ref[...] = jnp.zeros_like(dg_ref)
            loss_ref[...] = jnp.zeros_like(loss_ref)

        hb = h_ref[...]
        e = _rms(hb, g_ref[...]) - t_ref[...]
        loss_ref[...] += 0.5 * jnp.sum(jnp.mean(e * e, axis=-1, keepdims=True))
        dx, dg = _rms_bwd(e * (1.0 / D_MODEL), hb, g_ref[...])
        dh_ref[...] = dx
        dg_ref[...] += dg

    row = pl.BlockSpec((tm, D_MODEL), lambda i: (i, 0))
    vec = pl.BlockSpec((1, D_MODEL), lambda i: (0, 0))
    return _pc(
        body, name="final_loss", grid=(s // tm,),
        in_specs=[row, vec, row],
        out_specs=[row, vec, pl.BlockSpec((8, D_MODEL), lambda i: (0, 0))],
        out_shape=[_sds((s, D_MODEL), F32), _sds((1, D_MODEL), F32), _sds((8, D_MODEL), F32)],
    )(h, gain, target)


def _ffn_bwd_x(dho, h, gain, gate, up, wg, wu, wd, name, carry=None):
    s = h.shape[0]
    tm = _row_block(s)
    tf = 1024
    nf = FF_PAD // tf

    def body(dho_ref, h_ref, g_ref, gate_ref, up_ref, wg_ref, wu_ref, wd_ref,
             dh_ref, dhb_ref, dhob_ref, dgate_ref, dup_ref, act_ref, dgain_ref, acc_ref):
        i = pl.program_id(0)
        j = pl.program_id(1)

        @pl.when(j == 0)
        def _():
            dhob_ref[...] = dho_ref[...].astype(BF16)
            acc_ref[...] = jnp.zeros_like(acc_ref)

        @pl.when((i == 0) & (j == 0))
        def _():
            dgain_ref[...] = jnp.zeros_like(dgain_ref)

        dact = _dot_nt(dhob_ref[...], wd_ref[...])
        g = gate_ref[...].astype(F32)
        u = up_ref[...].astype(F32)
        sg = jax.nn.sigmoid(g)
        silu = g * sg
        dg = (dact * u * (sg * (1.0 + g * (1.0 - sg)))).astype(BF16)
        du = (dact * silu).astype(BF16)
        dgate_ref[...] = dg
        dup_ref[...] = du
        act_ref[...] = (silu * u).astype(BF16)
        acc_ref[...] += _dot_nt(dg, wg_ref[...]) + _dot_nt(du, wu_ref[...])

        @pl.when(j == nf - 1)
        def _():
            dx, dgain = _rms_bwd(acc_ref[...], h_ref[...], g_ref[...])
            dh = dho_ref[...] + dx
            dh_ref[...] = dh
            dhb_ref[...] = dh.astype(BF16)
            dgain_ref[...] += dgain

    row = pl.BlockSpec((tm, D_MODEL), lambda i, j: (i, 0))
    vec = pl.BlockSpec((1, D_MODEL), lambda i, j: (0, 0))
    hid = pl.BlockSpec((tm, tf), lambda i, j: (i, j))
    wcol = pl.BlockSpec((D_MODEL, tf), lambda i, j: (0, j))
    wrow = pl.BlockSpec((tf, D_MODEL), lambda i, j: (j, 0))
    return _pc_carrying(
        body, carry, (dho, h, gain, gate, up, wg, wu, wd), name=name, grid=(s // tm, nf),
        in_specs=[row, row, vec, hid, hid, wcol, wcol, wrow],
        out_specs=[row, row, row, hid, hid, hid, vec],
        out_shape=[_sds((s, D_MODEL), F32), _sds((s, D_MODEL), BF16), _sds((s, D_MODEL), BF16),
                   _sds((s, FF_PAD), BF16), _sds((s, FF_PAD), BF16), _sds((s, FF_PAD), BF16),
                   _sds((1, D_MODEL), F32)],
        scratch_shapes=[pltpu.VMEM((tm, D_MODEL), F32)],
    )


def _nt_rmsbwd(a, w, h, gain, dho, name, carry=None):
    stacked = a.ndim == 3
    if stacked:
        nk, s, tk = a.shape
    else:
        s, tk = a.shape
        nk = 1
    tm = _row_block(s)

    def body(a_ref, w_ref, h_ref, g_ref, dho_ref, dh_ref, dhb_ref, dgain_ref, acc_ref):
        i = pl.program_id(0)
        j = pl.program_id(1)
        if stacked:
            a_ref = a_ref.at[0]

        @pl.when(j == 0)
        def _():
            acc_ref[...] = jnp.zeros_like(acc_ref)

        @pl.when((i == 0) & (j == 0))
        def _():
            dgain_ref[...] = jnp.zeros_like(dgain_ref)

        acc_ref[...] += _dot_nt(a_ref[...], w_ref[...])

        @pl.when(j == nk - 1)
        def _():
            dx, dgain = _rms_bwd(acc_ref[...], h_ref[...], g_ref[...])
            dh = dho_ref[...] + dx
            dh_ref[...] = dh
            dhb_ref[...] = dh.astype(BF16)
            dgain_ref[...] += dgain

    row = pl.BlockSpec((tm, D_MODEL), lambda i, j: (i, 0))
    vec = pl.BlockSpec((1, D_MODEL), lambda i, j: (0, 0))
    return _pc_carrying(
        body, carry, (a, w, h, gain, dho), name=name, grid=(s // tm, nk),
        in_specs=[pl.BlockSpec((1, tm, tk), lambda i, j: (j, i, 0)) if stacked else pl.BlockSpec((tm, tk), lambda i, j: (i, 0)),
                  pl.BlockSpec((D_MODEL, tk), lambda i, j: (0, j)), row, vec, row],
        out_specs=[row, row, vec],
        out_shape=[_sds((s, D_MODEL), F32), _sds((s, D_MODEL), BF16), _sds((1, D_MODEL), F32)],
        scratch_shapes=[pltpu.VMEM((tm, D_MODEL), F32)],
    )


def _matmul_nt(a, w, name):
    s, kd = a.shape
    n = w.shape[0]
    tm = _row_block(s)

    def body(a_ref, w_ref, o_ref):
        o_ref[...] = _dot_nt(a_ref[...], w_ref[...]).astype(BF16)

    return _pc(
        body, name=name, grid=(s // tm,),
        in_specs=[pl.BlockSpec((tm, kd), lambda i: (i, 0)), pl.BlockSpec((n, kd), lambda i: (0, 0))],
        out_specs=pl.BlockSpec((tm, n), lambda i: (i, 0)),
        out_shape=_sds((s, n), BF16),
    )(a, w)


def _matmul_tn(a, b, name, carry=None):
    s, m = a.shape
    stacked = b.ndim == 3
    tmm = min(m, 1024)
    if stacked:
        n, tn = b.shape[0] * b.shape[2], b.shape[2]
    else:
        n = b.shape[1]
        tn = n if n <= 1024 else 1024
    tk = min(_TN_ROWS, s)
    nk = s // tk

    def body(a_ref, b_ref, o_ref, acc_ref):
        k = pl.program_id(2)
        if stacked:
            b_ref = b_ref.at[0]

        @pl.when(k == 0)
        def _():
            acc_ref[...] = jnp.zeros_like(acc_ref)

        acc_ref[...] += _dot_tn(a_ref[...], b_ref[...])

        @pl.when(k == nk - 1)
        def _():
            o_ref[...] = acc_ref[...].astype(BF16)

    (out,), got = _pc_carrying(
        body, carry, (a, b), name=name, grid=(m // tmm, n // tn, nk),
        in_specs=[pl.BlockSpec((tk, tmm), lambda i, j, k: (k, i)),
                  pl.BlockSpec((1, tk, tn), lambda i, j, k: (j, k, 0)) if stacked
                  else pl.BlockSpec((tk, tn), lambda i, j, k: (k, j))],
        out_specs=[pl.BlockSpec((tmm, tn), lambda i, j, k: (i, j))],
        out_shape=[_sds((m, n), BF16)],
        scratch_shapes=[pltpu.VMEM((tmm, tn), F32)],
    )
    return out if carry is None else (out, got)


def _conv_bwd(dbuc, bcx, cw):
    s = bcx.shape[0]
    tc = _CONV_COLS
    nc = D_MODEL // tc

    def body(d_ref, b_ref, c_ref, xp_ref, w_ref, dbcx_ref, dw_ref):
        rows = lax.broadcasted_iota(jnp.int32, (s, tc), 0)
        c = c_ref[...].astype(F32)
        xp = xp_ref[...].astype(F32)
        u = c * xp
        u1 = _shift_down(u, 1, rows)
        u2 = _shift_down(u, 2, rows)
        w = w_ref[...]
        uc = w[2:3] * u + w[1:2] * u1 + w[0:1] * u2
        d = d_ref[...].astype(F32)
        dbcx_ref[0] = (d * uc).astype(BF16)
        duc = d * b_ref[...].astype(F32)
        du = w[2:3] * duc + w[1:2] * _shift_up(duc, 1, rows, s) + w[0:1] * _shift_up(duc, 2, rows, s)
        dbcx_ref[1] = (du * xp).astype(BF16)
        dbcx_ref[2] = (du * c).astype(BF16)
        dw_ref[0:1, :] = jnp.sum(duc * u2, axis=0, keepdims=True)
        dw_ref[1:2, :] = jnp.sum(duc * u1, axis=0, keepdims=True)
        dw_ref[2:3, :] = jnp.sum(duc * u, axis=0, keepdims=True)

    col = lambda off: pl.BlockSpec((s, tc), lambda j: (0, off + j))
    tap = pl.BlockSpec((3, tc), lambda j: (0, j))
    return _pc(
        body, name="conv_bwd", grid=(nc,),
        in_specs=[col(0), col(0), col(nc), col(2 * nc), tap],
        out_specs=[pl.BlockSpec((3, s, tc), lambda j: (0, 0, j)), tap],
        out_shape=[_sds((3, s, D_MODEL), BF16), _sds((3, D_MODEL), F32)],
    )(dbuc, bcx, bcx, bcx, cw)


def _attn_bwd(q, k, v, do, o, lse, cc, sn, sp, carry=None):
    s = q.shape[1]
    bk = _row_block(s)
    nb = s // bk

    def body(q_ref, k_ref, v_ref, do_ref, o_ref, lse_ref, cc_ref, sn_ref, sp_ref,
             dq_ref, dk_ref, dv_ref, dqacc_ref, delta_ref, bias_ref):
        j = pl.program_id(1)
        kb = k_ref[0]
        vb = v_ref[0]

        @pl.when((pl.program_id(0) == 0) & (j == 0))
        def _():
            bias_ref[...] = _chunk_bias(bk, bk)

        @pl.when(j == 0)
        def _():
            dqacc_ref[...] = jnp.zeros_like(dqacc_ref)

            def fill(i, _):
                rows = pl.ds(pl.multiple_of(i * bk, bk), bk)
                d = jnp.sum(do_ref[rows, :].astype(F32) * o_ref[rows, :].astype(F32), axis=-1, keepdims=True)
                delta_ref[rows, :] = jnp.broadcast_to(d, (bk, 128))
                return 0

            lax.fori_loop(0, nb, fill, 0)

        def step(i, carry, masked):
            dk, dv = carry
            rows = pl.ds(pl.multiple_of(i * bk, bk), bk)
            qb = q_ref[0, rows, :]
            dob = do_ref[rows, :]
            sc = _dot_nt(qb, kb)
            if masked:
                sc = sc + bias_ref[...]
            p = jnp.exp2(sc - lse_ref[0, rows, :][:, :1])
            dv = dv + _dot_tn(p.astype(BF16), dob)
            ds = (p * (_dot_nt(dob, vb) - delta_ref[rows, :][:, :1])).astype(BF16)
            dk = dk + _dot_tn(ds, qb)
            dqacc_ref[rows, :] += _dot(ds, kb)
            return dk, dv

        carry = step(j, (jnp.zeros((bk, HEAD_PAD), F32), jnp.zeros((bk, V_DIM), F32)), True)
        dq = dqacc_ref[pl.ds(pl.multiple_of(j * bk, bk), bk), :] * ATT_SCALE
        dq_ref[:, :NOPE] = dq[:, :NOPE].astype(BF16)
        dq_ref[:, NOPE:] = _rope_bwd(dq[:, NOPE:], cc_ref[...], sn_ref[...], sp_ref[...]).astype(BF16)
        dk, dv = lax.fori_loop(j + 1, nb, lambda i, c: step(i, c, False), carry)
        dk_ref[0] = (dk * LN_2).astype(BF16)
        dv_ref[0] = dv.astype(BF16)

    blk = lambda n: pl.BlockSpec((1, bk, n), lambda h, j: (h, j, 0))
    whole = lambda n: pl.BlockSpec((1, s, n), lambda h, j: (h, 0, 0))
    cols = pl.BlockSpec((s, V_DIM), lambda h, j: (0, h))
    tab = pl.BlockSpec((bk, 128), lambda h, j: (j, 0))
    return _pc_carrying(
        body, carry, (q, k, v, do, o, lse, cc, sn, sp), name="attn_bwd", grid=(N_HEADS, nb),
        in_specs=[whole(HEAD_PAD), blk(HEAD_PAD), blk(V_DIM), cols, cols, whole(128), tab, tab, tab],
        out_specs=[pl.BlockSpec((bk, HEAD_PAD), lambda h, j: (j, h)), blk(HEAD_PAD), blk(V_DIM)],
        out_shape=[_sds((s, N_HEADS * HEAD_PAD), BF16), _sds((N_HEADS, s, HEAD_PAD), BF16),
                   _sds((N_HEADS, s, V_DIM), BF16)],
        scratch_shapes=[pltpu.VMEM((s, HEAD_PAD), F32), pltpu.VMEM((s, 128), F32), pltpu.VMEM((bk, bk), F32)],
    )


def _mla_mid_bwd(dq, dk, dv, proj, w_uq, w_ukv, g_cq, g_ckv, cc, sn, sp, carry=None):
    s = dq.shape[0]
    tm = _row_block(s)

    def body(dq_ref, dk_ref, dv_ref, proj_ref, wuq_ref, wukv_ref, gcq_ref, gckv_ref, cc_ref, sn_ref, sp_ref,
             dproj_ref, dkv_ref, dgcq_ref, dgckv_ref, acq_ref, ackv_ref, akr_ref):
        @pl.when(pl.program_id(0) == 0)
        def _():
            dgcq_ref[...] = jnp.zeros_like(dgcq_ref)
            dgckv_ref[...] = jnp.zeros_like(dgckv_ref)

        for h in range(N_HEADS):
            cols = slice(h * HEAD_PAD, (h + 1) * HEAD_PAD)
            dkb = dk_ref[h]
            dkv_ref[:, h * HEAD_PAD:h * HEAD_PAD + NOPE] = dkb[:, :NOPE]
            dkv_ref[:, h * HEAD_PAD + NOPE:(h + 1) * HEAD_PAD] = dv_ref[h]
            cq_part = _dot_nt(dq_ref[:, cols], wuq_ref[h])
            ckv_part = _dot_nt(dkv_ref[:, cols], wukv_ref[h])
            kr_part = dkb[:, NOPE:].astype(F32)
            if h == 0:
                acq_ref[...], ackv_ref[...], akr_ref[...] = cq_part, ckv_part, kr_part
            else:
                acq_ref[...] += cq_part
                ackv_ref[...] += ckv_part
                akr_ref[...] += kr_part

        proj = proj_ref[...]
        dcq, dgcq = _rms_bwd(acq_ref[...], proj[:, :CQ], gcq_ref[...])
        dckv, dgckv = _rms_bwd(ackv_ref[...], proj[:, CQ:CQ + CKV], gckv_ref[...])
        dproj_ref[:, :CQ] = dcq.astype(BF16)
        dproj_ref[:, CQ:CQ + CKV] = dckv.astype(BF16)
        dproj_ref[:, CQ + CKV:] = _rope_bwd(akr_ref[...], cc_ref[...], sn_ref[...], sp_ref[...]).astype(BF16)
        dgcq_ref[...] += dgcq
        dgckv_ref[...] += dgckv

    head_blk = lambda n: pl.BlockSpec((N_HEADS, tm, n), lambda i: (0, i, 0))
    head_cols = pl.BlockSpec((tm, N_HEADS * HEAD_PAD), lambda i: (i, 0))
    head_w = lambda a: pl.BlockSpec(a.shape, lambda i: (0, 0, 0))
    row = lambda n: pl.BlockSpec((tm, n), lambda i: (i, 0))
    vec = lambda n: pl.BlockSpec((1, n), lambda i: (0, 0))
    return _pc_carrying(
        body, carry, (dq, dk, dv, proj, w_uq, w_ukv, g_cq, g_ckv, cc, sn, sp),
        name="mla_mid_bwd", grid=(s // tm,),
        in_specs=[head_cols, head_blk(HEAD_PAD), head_blk(V_DIM), row(PROJ_PAD), head_w(w_uq), head_w(w_ukv),
                  vec(CQ), vec(CKV), row(128), row(128), row(128)],
        out_specs=[row(PROJ_PAD), head_cols, vec(CQ), vec(CKV)],
        out_shape=[_sds((s, PROJ_PAD), BF16), _sds((s, N_HEADS * HEAD_PAD), BF16), _sds((1, CQ), F32),
                   _sds((1, CKV), F32)],
        scratch_shapes=[pltpu.VMEM((tm, CQ), F32), pltpu.VMEM((tm, CKV), F32), pltpu.VMEM((tm, 128), F32)],
    )


def _peer(k):
    x, y, c = lax.axis_index("x"), lax.axis_index("y"), lax.axis_index("c")
    px = 1 - x if k & 4 else x
    py = 1 - y if k & 2 else y
    pc = 1 - c if k & 1 else c
    return (px, py, pc), 4 * px + 2 * py + pc


def _exchange_ops(arrays, scatter):
    nw = len(arrays)
    by_cols = [isinstance(a, _Cols) for a in arrays]
    arrays = [a.array if isinstance(a, _Cols) else a for a in arrays]
    direct = range(1, N_DEV) if scatter else (1, 2, 4, 6)

    def columns(ref, idx, width):
        return ref.at[:, pl.ds(pl.multiple_of(idx * width, 128), width)]

    def sent(ins, w, idx):
        if not scatter:
            return ins[w]
        return columns(ins[w], idx, arrays[w].shape[1] // N_DEV) if by_cols[w] else ins[w].at[idx]

    def slot(outs, w, idx):
        if by_cols[w] and not scatter:
            return columns(outs[w], idx, arrays[w].shape[1])
        return outs[w].at[idx]

    def copy(w, k, src, dst, to, send_sems, recv_sems):
        return pltpu.make_async_remote_copy(
            src_ref=src, dst_ref=dst, send_sem=send_sems.at[w * N_DEV + k], recv_sem=recv_sems.at[w * N_DEV + k],
            device_id=to, device_id_type=pl.DeviceIdType.MESH)

    def own_copies(ins, outs, local_sems):
        _, me = _peer(0)
        return [pltpu.make_async_copy(sent(ins, w, me), slot(outs, w, me), local_sems.at[w]) for w in range(nw)]

    def sends(ins, outs, send_sems, recv_sems):
        _, me = _peer(0)
        out = []
        for k in direct:
            dev, idx = _peer(k)
            for w in range(nw):
                out.append(copy(w, k, sent(ins, w, idx), slot(outs, w, me), dev, send_sems, recv_sems))
        return out

    def relays(outs, send_sems, recv_sems):
        sibling, _ = _peer(1)
        out = []
        for k in (2, 4, 6):
            _, idx = _peer(k)
            for w in range(nw):
                out.append(copy(w, k + 1, slot(outs, w, idx), slot(outs, w, idx), sibling, send_sems, recv_sems))
        return out

    def arrival(outs, w, k, send_sems, recv_sems):
        dev, idx = _peer(k)
        return copy(w, k, slot(outs, w, idx), slot(outs, w, idx), dev, send_sems, recv_sems)

    def start(ins, outs, send_sems, recv_sems, local_sems):
        for cp in own_copies(ins, outs, local_sems) + sends(ins, outs, send_sems, recv_sems):
            cp.start()

    def relay(ins, outs, send_sems, recv_sems, local_sems):
        for k in (2, 4, 6):
            for w in range(nw):
                arrival(outs, w, k, send_sems, recv_sems).wait_recv()
        for cp in relays(outs, send_sems, recv_sems):
            cp.start()

    def wait(ins, outs, send_sems, recv_sems, local_sems):
        for cp in own_copies(ins, outs, local_sems):
            cp.wait()
        for cp in sends(ins, outs, send_sems, recv_sems) + ([] if scatter else relays(outs, send_sems, recv_sems)):
            cp.wait_send()
        for k in (range(1, N_DEV) if scatter else (1, 3, 5, 7)):
            for w in range(nw):
                arrival(outs, w, k, send_sems, recv_sems).wait_recv()

    def landed_shape(a, cols):
        if not cols:
            return a.shape if scatter else (N_DEV,) + a.shape
        r, c = a.shape
        return (N_DEV, r, c // N_DEV) if scatter else (r, N_DEV * c)

    landed = [_sds(landed_shape(a, cols), a.dtype) for a, cols in zip(arrays, by_cols)]
    sems = [pltpu.SemaphoreType.DMA((nw * N_DEV,)), pltpu.SemaphoreType.DMA((nw * N_DEV,)),
            pltpu.SemaphoreType.DMA((nw,))]
    return start, (None if scatter else relay), wait, landed, sems, arrays


class _Cols:
    def __init__(self, array):
        self.array = array


def _exchange(arrays, scatter, name):
    nw = len(arrays)
    start, relay, wait, landed, sems, arrays = _exchange_ops(arrays, scatter)

    def body(*refs):
        ins, outs, csem = refs[:nw], refs[nw:2 * nw], refs[2 * nw:]
        start(ins, outs, *csem)
        if relay is not None:
            relay(ins, outs, *csem)
        wait(ins, outs, *csem)

    any_spec = pl.BlockSpec(memory_space=pl.ANY)
    return _pc(body, name=name, in_specs=[any_spec] * nw, out_specs=[any_spec] * nw, out_shape=landed,
               scratch_shapes=sems)(*arrays)


def _adam_math(g, w, m, v):
    m = ADAM_B1 * m + (1.0 - ADAM_B1) * g
    v = ADAM_B2 * v + (1.0 - ADAM_B2) * jnp.square(g)
    m_hat = m / (1.0 - ADAM_B1 ** ADAM_STEP)
    v_hat = v / (1.0 - ADAM_B2 ** ADAM_STEP)
    delta = -ADAM_LR * (m_hat / (jnp.sqrt(v_hat) + ADAM_EPS) + ADAM_WD * w)
    return delta, m, v


def _adam_rows(r):
    for t in range(min(r, 512) // 16 * 16, 0, -16):
        if r % t == 0:
            return t
    return r


def _adamw(parts, w, m, v, name):
    nl, r, c = w.shape
    tr = _adam_rows(r)
    nr = r // tr

    def body(*refs):
        p_refs, (w_ref, m_ref, v_ref, g_ref, d_ref, mo_ref, vo_ref) = refs[:nl], refs[nl:]
        for layer in range(nl):
            @pl.when(pl.program_id(0) == layer)
            def _(p_ref=p_refs[layer]):
                g = p_ref[0].astype(F32)
                for src in range(1, N_DEV):
                    g = g + p_ref[src].astype(F32)
                delta, m2, v2 = _adam_math(g, w_ref[0], m_ref[0], v_ref[0])
                g_ref[0] = g
                d_ref[0] = delta
                mo_ref[0] = m2
                vo_ref[0] = v2

    def part_spec(layer):
        def index(l, i):
            return 0, jnp.where(l == layer, i, jnp.where(l < layer, 0, nr - 1)), 0
        return pl.BlockSpec((N_DEV, tr, c), index)

    blk = pl.BlockSpec((1, tr, c), lambda l, i: (l, i, 0))
    return _pc(
        body, name=name, grid=(nl, nr),
        in_specs=[part_spec(layer) for layer in range(nl)] + [blk, blk, blk],
        out_specs=[blk, blk, blk, blk],
        out_shape=[_sds((nl, r, c), F32)] * 4,
    )(*parts, w, m, v)


def _small_allreduce_adamw(gpack, wpack, mpack, vpack, last_grads):
    shape = gpack.shape
    compact = (SMALL_ROWS, D_MODEL)
    nw = len(last_grads)
    start, _, wait, landed, exchange_sems, last_grads = _exchange_ops(last_grads, True)

    def body(*refs):
        (g_ref, w_ref, m_ref, v_ref), rest = refs[:4], refs[4:]
        cin, rest = rest[:nw], rest[nw:]
        (go_ref, d_ref, mo_ref, vo_ref), rest = rest[:4], rest[4:]
        cout, rest = rest[:nw], rest[nw:]
        (comp_ref, gath_ref, send_sems, recv_sems), csem = rest[:4], rest[4:]
        start(cin, cout, *csem)
        _, me = _peer(0)
        comp_ref[...] = jnp.zeros(compact, F32)
        r = 0
        for p, n in enumerate(SMALL_PIECES):
            comp_ref[r:r + n, :] = g_ref[8 * p:8 * p + n, :]
            r += n
        gath_ref[me] = comp_ref[...]
        copies = []
        for k in range(1, N_DEV):
            dev, idx = _peer(k)
            copies.append(pltpu.make_async_remote_copy(
                src_ref=comp_ref, dst_ref=gath_ref.at[me], send_sem=send_sems.at[k], recv_sem=recv_sems.at[k],
                device_id=dev, device_id_type=pl.DeviceIdType.MESH))
        for cp in copies:
            cp.start()
        for cp in copies:
            cp.wait_send()
        for k in range(1, N_DEV):
            dev, idx = _peer(k)
            pltpu.make_async_remote_copy(
                src_ref=comp_ref, dst_ref=gath_ref.at[idx], send_sem=send_sems.at[k], recv_sem=recv_sems.at[k],
                device_id=dev, device_id_type=pl.DeviceIdType.MESH).wait_recv()
        g = gath_ref[0]
        for src in range(1, N_DEV):
            g = g + gath_ref[src]
        go_ref[...] = jnp.zeros(shape, F32)
        r = 0
        for p, n in enumerate(SMALL_PIECES):
            go_ref[8 * p:8 * p + n, :] = g[r:r + n, :]
            r += n
        delta, m2, v2 = _adam_math(go_ref[...], w_ref[...], m_ref[...], v_ref[...])
        d_ref[...] = delta
        mo_ref[...] = m2
        vo_ref[...] = v2
        wait(cin, cout, *csem)

    vm = pl.BlockSpec(memory_space=pltpu.VMEM)
    any_spec = pl.BlockSpec(memory_space=pl.ANY)
    res = _pc(
        body, name="small_allreduce_adamw",
        in_specs=[vm] * 4 + [any_spec] * nw, out_specs=[vm] * 4 + [any_spec] * nw,
        out_shape=[_sds(shape, F32)] * 4 + landed,
        scratch_shapes=[pltpu.VMEM(compact, F32), pltpu.VMEM((N_DEV,) + compact, F32),
                        pltpu.SemaphoreType.DMA((N_DEV,)), pltpu.SemaphoreType.DMA((N_DEV,))] + exchange_sems,
    )(gpack, wpack, mpack, vpack, *last_grads)
    return res[:4], res[4:]


def _rows_from_shards(g):
    return g.reshape(N_DEV * g.shape[1], g.shape[2])


def _shards_from_rows(a):
    return a.reshape(N_DEV, a.shape[0] // N_DEV, a.shape[1])


def _pad_to(a, rows, cols):
    return jnp.pad(a, ((0, rows - a.shape[0]), (0, cols - a.shape[1])))


def _rope_tables(pos):
    inv_freq = 1.0 / (ROPE_THETA ** (jnp.arange(0, ROPE, 2, dtype=F32) / ROPE))
    ang = pos.astype(F32)[:, None] * inv_freq
    cos, sin = jnp.cos(ang), jnp.sin(ang)
    z32, z64, z96 = (jnp.zeros((pos.shape[0], n), F32) for n in (32, 64, 96))
    return (jnp.concatenate([cos, cos, z64], axis=1), jnp.concatenate([-sin, z96], axis=1),
            jnp.concatenate([z32, sin, z64], axis=1))


def _pad_row(vec):
    vec = vec.reshape(1, -1)
    return jnp.pad(vec, ((0, 0), (0, D_MODEL - vec.shape[1])))


def _forward_backward(x, target, tables, gains, shards):
    cc, sn, sp = tables
    ffn_g = gains["ffn_norm"]
    by_cols = ("wg0", "wu0", "wg1", "wu1", "c_in")

    def gather(names, relay_at):
        return [_Cols(shards[n]) if n in by_cols else shards[n] for n in names], False, relay_at

    (got,) = _exchange([shards["w_in"]], False, "gather_w_in")
    w_in = _rows_from_shards(got)
    (n0, proj, cqn, ckvn, krr), (w_uq, w_ukv) = _mla_in_fwd(
        x, gains["mla_norm"], w_in, gains["g_cq"], gains["g_ckv"], cc, sn, sp, carry=gather(["w_uq", "w_ukv"], 0.5))
    (q, k, v), (w_o, conv_norm, conv_w) = _qkv_proj(
        cqn, ckvn, krr, w_uq, w_ukv, cc, sn, sp, carry=gather(["w_o", "conv_norm", "conv_w"], 0.5))
    w_o = _rows_from_shards(w_o)
    conv_norm = conv_norm.reshape(1, D_MODEL)
    conv_w = jnp.transpose(conv_w, (1, 0, 2)).reshape(3, D_MODEL)
    (o, lse), (wg0, wu0, wd0, c_in, c_out, wd1) = _attn_fwd(
        q, k, v, carry=gather(["wg0", "wu0", "wd0", "c_in", "c_out", "wd1"], 0.75))
    wd0 = _rows_from_shards(wd0)
    wd1 = _rows_from_shards(wd1)
    c_out = _rows_from_shards(c_out)
    h1 = _matmul_res(o, w_o, x, "mla_out_fwd")
    (h2, n1, gate0, up0), (wg1, wu1) = _ffn_fwd(
        h1, ffn_g[0:1], wg0, wu0, wd0, "ffn0_fwd", carry=gather(["wg1", "wu1"], 0.7))
    n2, bcx = _rms_matmul(h2, conv_norm, c_in, "conv_in_fwd")
    bu = _conv_fwd(bcx, conv_w)
    h3 = _matmul_res(bu, c_out, h2, "conv_out_fwd")
    (h4, n3, gate1, up1), _ = _ffn_fwd(h3, ffn_g[1:2], wg1, wu1, wd1, "ffn1_fwd")
    dh4, d_final, loss = _final_loss(h4, gains["final_norm"], target)

    small = {"final_norm": d_final}
    parts = {}

    def scatter(**blocks):
        return list(blocks), (list(blocks.values()), True, None)

    (dh3, dh3_b, dh4_b, dgate, dup, act, d_ffn1), _ = _ffn_bwd_x(
        dh4, h3, ffn_g[1:2], gate1, up1, wg1, wu1, wd1, "ffn1_bwd")
    dwg1 = _matmul_tn(dgate, n3, "ffn1_dwg")
    dwu1 = _matmul_tn(dup, n3, "ffn1_dwu")
    dwd1 = _matmul_tn(act, dh4_b, "ffn1_dwd")
    dbuc = _matmul_nt(dh3_b, c_out, "conv_out_bwd")
    d_c_out = _matmul_tn(bu, dh3_b, "conv_dwout")
    dbcx, small["conv_w"] = _conv_bwd(dbuc, bcx, conv_w)
    d_c_in = _matmul_tn(n2, dbcx, "conv_dwin")
    (dh2, dh2_b, small["conv_norm"]), _ = _nt_rmsbwd(dbcx, c_in, h2, conv_norm, dh3, "conv_in_bwd")
    names, carry = scatter(c_in=_Cols(d_c_in), c_out=_shards_from_rows(d_c_out))
    (dh1, dh1_b, dh2_b2, dgate, dup, act, d_ffn0), got = _ffn_bwd_x(
        dh2, h1, ffn_g[0:1], gate0, up0, wg0, wu0, wd0, "ffn0_bwd", carry=carry)
    parts.update(zip(names, got))
    dwg0 = _matmul_tn(dgate, n1, "ffn0_dwg")
    dwu0 = _matmul_tn(dup, n1, "ffn0_dwu")
    dwd0 = _matmul_tn(act, dh2_b2, "ffn0_dwd")
    small["ffn_norm"] = jnp.pad(d_ffn0, ((0, 7), (0, 0))) + jnp.pad(d_ffn1, ((1, 6), (0, 0)))
    do = _matmul_nt(dh1_b, w_o, "mla_out_bwd")
    d_w_o = _matmul_tn(o, dh1_b, "mla_dwo")
    names, carry = scatter(**{n: _shards_from_rows(d) for n, d in dict(
        wg0=dwg0, wg1=dwg1, wu0=dwu0, wu1=dwu1, wd0=dwd0, wd1=dwd1).items()})
    (dq, dk, dv), got = _attn_bwd(q, k, v, do, o, lse, cc, sn, sp, carry=carry)
    parts.update(zip(names, got))
    d_w_uq = _matmul_tn(dq, cqn, "mla_dwuq")
    names, carry = scatter(w_o=_shards_from_rows(d_w_o), w_uq=_shards_from_rows(d_w_uq))
    (dproj, dkv, small["g_cq"], small["g_ckv"]), got = _mla_mid_bwd(
        dq, dk, dv, proj, w_uq, w_ukv, gains["g_cq"], gains["g_ckv"], cc, sn, sp, carry=carry)
    parts.update(zip(names, got))
    d_w_ukv = _matmul_tn(ckvn, dkv, "mla_dwukv")
    names, carry = scatter(w_ukv=_Cols(d_w_ukv))
    d_w_in, got = _matmul_tn(dproj, n0, "mla_dwin", carry=carry)
    parts.update(zip(names, got))
    (dx, _, small["mla_norm"]), _ = _nt_rmsbwd(dproj, w_in, x, gains["mla_norm"], dh1, "mla_in_bwd")
    return loss, dx, parts, _Cols(d_w_in), small


def kernel(x, positions, mla_norm, mla_w_in, mla_g_cq, mla_g_ckv, mla_w_uq, mla_w_ukv, mla_w_o, conv_norm, conv_w_in, conv_w, conv_w_out, ffn_norm, ffn_w_gate, ffn_w_up, ffn_w_down, final_norm, loss_target, m_mla_norm, m_mla_w_in, m_mla_g_cq, m_mla_g_ckv, m_mla_w_uq, m_mla_w_ukv, m_mla_w_o, m_conv_norm, m_conv_w_in, m_conv_w, m_conv_w_out, m_ffn_norm, m_ffn_w_gate, m_ffn_w_up, m_ffn_w_down, m_final_norm, v_mla_norm, v_mla_w_in, v_mla_g_cq, v_mla_g_ckv, v_mla_w_uq, v_mla_w_ukv, v_mla_w_o, v_conv_norm, v_conv_w_in, v_conv_w, v_conv_w_out, v_ffn_norm, v_ffn_w_gate, v_ffn_w_up, v_ffn_w_down, v_final_norm):
    me = 4 * lax.axis_index("x") + 2 * lax.axis_index("y") + lax.axis_index("c")

    bf = lambda a, rows, cols: _pad_to(a.astype(BF16), rows, cols)
    shards = dict(
        w_in=bf(mla_w_in[0], D_MODEL // N_DEV, PROJ_PAD), w_uq=bf(mla_w_uq[0], CQ, HEAD_PAD),
        w_ukv=mla_w_ukv[0].astype(BF16), w_o=mla_w_o[0].astype(BF16),
        c_in=conv_w_in[0].astype(BF16), c_out=conv_w_out[0].astype(BF16),
        conv_norm=conv_norm, conv_w=conv_w[0])
    for l in range(2):
        shards.update({f"wg{l}": bf(ffn_w_gate[l], D_MODEL, FF_SHARD_PAD), f"wu{l}": bf(ffn_w_up[l], D_MODEL, FF_SHARD_PAD),
                       f"wd{l}": bf(ffn_w_down[l], FF_SHARD_PAD, D_MODEL)})
    gains = dict(mla_norm=mla_norm, g_cq=mla_g_cq, g_ckv=mla_g_ckv, ffn_norm=ffn_norm,
                 final_norm=final_norm.reshape(1, -1))
    loss_local, dx, parts, d_w_in, grads = _forward_backward(
        x[0], loss_target[0], _rope_tables(positions[0]), gains, shards)

    col0 = me * (D_MODEL // N_DEV)

    def place(shard):
        return lax.dynamic_update_slice(jnp.zeros((shard.shape[0], D_MODEL), F32), shard, (0, col0))

    no_loss = jnp.zeros((8, D_MODEL), F32)

    def pack(mla_n, g_cq, g_ckv, ffn_n, fin_n, conv_n, conv_taps, loss_tile):
        rows = [mla_n, _pad_row(g_cq), _pad_row(g_ckv), ffn_n, fin_n.reshape(1, -1), conv_n, conv_taps, loss_tile]
        assert all(r.shape[0] in (n, 8) for r, n in zip(rows, SMALL_PIECES))
        return jnp.concatenate([jnp.pad(r, ((0, 8 - r.shape[0]), (0, 0))) for r in rows], axis=0)

    gpack = pack(grads["mla_norm"], grads["g_cq"], grads["g_ckv"], grads["ffn_norm"], grads["final_norm"],
                 grads["conv_norm"], grads["conv_w"], loss_local)
    wpack = pack(mla_norm, mla_g_cq, mla_g_ckv, ffn_norm, final_norm, place(conv_norm), place(conv_w[0]), no_loss)
    mpack = pack(m_mla_norm, m_mla_g_cq, m_mla_g_ckv, m_ffn_norm, m_final_norm, place(m_conv_norm),
                 place(m_conv_w[0]), no_loss)
    vpack = pack(v_mla_norm, v_mla_g_cq, v_mla_g_ckv, v_ffn_norm, v_final_norm, place(v_conv_norm),
                 place(v_conv_w[0]), no_loss)
    small, (parts["w_in"],) = _small_allreduce_adamw(gpack, wpack, mpack, vpack, [d_w_in])
    loss = small[0][56, 0]

    def adamw(name, w, m, v, partials, transposed=False):
        if transposed:
            w, m, v = (jnp.swapaxes(a, 1, 2) for a in (w, m, v))
        outs = _adamw(partials, w, m, v, "adamw_" + name)
        return [jnp.swapaxes(o, 1, 2) for o in outs] if transposed else outs

    res = dict(
        w_in=adamw("w_in", mla_w_in, m_mla_w_in, v_mla_w_in, [parts["w_in"]], True),
        w_uq=adamw("w_uq", mla_w_uq, m_mla_w_uq, v_mla_w_uq, [parts["w_uq"]], True),
        w_ukv=adamw("w_ukv", mla_w_ukv, m_mla_w_ukv, v_mla_w_ukv, [parts["w_ukv"]]),
        w_o=adamw("w_o", mla_w_o, m_mla_w_o, v_mla_w_o, [parts["w_o"]]),
        c_in=adamw("c_in", conv_w_in, m_conv_w_in, v_conv_w_in, [parts["c_in"]]),
        c_out=adamw("c_out", conv_w_out, m_conv_w_out, v_conv_w_out, [parts["c_out"]]),
        wg=adamw("wg", ffn_w_gate, m_ffn_w_gate, v_ffn_w_gate, [parts["wg0"], parts["wg1"]], True),
        wu=adamw("wu", ffn_w_up, m_ffn_w_up, v_ffn_w_up, [parts["wu0"], parts["wu1"]], True),
        wd=adamw("wd", ffn_w_down, m_ffn_w_down, v_ffn_w_down, [parts["wd0"], parts["wd1"]]),
    )

    def unpack(p):
        own = lambda rows: lax.dynamic_slice(rows, (0, col0), (rows.shape[0], D_MODEL // N_DEV))
        return dict(mla_norm=p[0:1], g_cq=p[8:9, :CQ], g_ckv=p[16:17, :CKV], ffn_norm=p[24:26], final_norm=p[32],
                    conv_norm=own(p[40:41]), conv_w=own(p[48:51])[None])

    small = [unpack(p) for p in small]
    order = ["mla_norm", "w_in", "g_cq", "g_ckv", "w_uq", "w_ukv", "w_o", "conv_norm", "c_in", "conv_w", "c_out",
             "ffn_norm", "wg", "wu", "wd", "final_norm"]
    out = [loss, dx[None]]
    for kind in range(4):
        for n in order:
            out.append(res[n][kind] if n in res else small[kind][n])
    return tuple(out)
```

```python
import math

import jax
import jax.numpy as jnp
from jax import lax
from jax.experimental import pallas as pl
from jax.experimental.pallas import tpu as pltpu

F32 = jnp.float32
BF16 = jnp.bfloat16

N_DEV = 8
D_MODEL = 1024
N_HEADS = 8
NOPE = 128
ROPE = 64
V_DIM = 128
HEAD_PAD = 256
CQ = 512
CKV = 256
PROJ = CQ + CKV + ROPE
PROJ_PAD = CQ + CKV + 128
D_FF = 2816
FF_SHARD = D_FF // N_DEV
FF_SHARD_PAD = 384
FF_PAD = FF_SHARD_PAD * N_DEV
CHUNK_SHIFT = 6
RMS_EPS = 1e-6
ROPE_THETA = 10000.0
ATT_SCALE = 1.0 / math.sqrt(NOPE + ROPE)
LOG2_E = math.log2(math.e)
LN_2 = math.log(2.0)
Q_SCALE = ATT_SCALE * LOG2_E
NEG = -1e30

ADAM_LR = 0.001
ADAM_B1 = 0.9
ADAM_B2 = 0.999
ADAM_EPS = 1e-08
ADAM_WD = 0.01
ADAM_STEP = 10

SMALL_PIECES = (1, 1, 1, 2, 1, 1, 3, 1)
SMALL_ROWS = 16

_NT = (((1,), (1,)), ((), ()))
_TN = (((0,), (0,)), ((), ()))


def _pc(body, *, name, out_shape, grid=(), in_specs=None, out_specs=None, scratch_shapes=(), vmem_mb=None):
    params = {}
    if vmem_mb is not None:
        params["vmem_limit_bytes"] = vmem_mb << 20
    kwargs = dict(
        name=name, out_shape=out_shape, grid=grid, scratch_shapes=scratch_shapes,
        compiler_params=pltpu.CompilerParams(**params),
    )
    if in_specs is not None:
        kwargs["in_specs"] = in_specs
    if out_specs is not None:
        kwargs["out_specs"] = out_specs
    return pl.pallas_call(body, **kwargs)


def _pc_carrying(body, carry, operands, *, name, out_shape, grid, in_specs, out_specs, scratch_shapes=()):
    if carry is None:
        return _pc(body, name=name, out_shape=out_shape, grid=grid, in_specs=in_specs, out_specs=out_specs,
                   scratch_shapes=scratch_shapes)(*operands), None
    arrays, scatter, relay_at = carry
    nw, n_in, n_out, n_scr = len(arrays), len(in_specs), len(out_shape), len(scratch_shapes)
    start, relay, wait, landed_shapes, sems, arrays = _exchange_ops(arrays, scatter)
    n_steps = math.prod(grid)
    relay_step = None if relay is None else max(1, min(int(relay_at * n_steps), n_steps - 2))

    def wrapped(*refs):
        ins, rest = refs[:n_in], refs[n_in:]
        cin, rest = rest[:nw], rest[nw:]
        outs, rest = rest[:n_out], rest[n_out:]
        cout, rest = rest[:nw], rest[nw:]
        scr, csem = rest[:n_scr], rest[n_scr:]
        step = pl.program_id(0)
        for a in range(1, len(grid)):
            step = step * grid[a] + pl.program_id(a)

        @pl.when(step == 0)
        def _():
            start(cin, cout, *csem)

        if relay is not None:
            @pl.when(step == relay_step)
            def _():
                relay(cin, cout, *csem)

        body(*ins, *outs, *scr)

        @pl.when(step == n_steps - 1)
        def _():
            wait(cin, cout, *csem)

    any_spec = pl.BlockSpec(memory_space=pl.ANY)
    res = _pc(
        wrapped, name=name, grid=grid,
        in_specs=list(in_specs) + [any_spec] * nw, out_specs=list(out_specs) + [any_spec] * nw,
        out_shape=list(out_shape) + landed_shapes, scratch_shapes=list(scratch_shapes) + sems,
    )(*operands, *arrays)
    return res[:n_out], res[n_out:]


def _sds(shape, dtype):
    return jax.ShapeDtypeStruct(shape, dtype)


def _dot(a, b):
    return jnp.dot(a, b, preferred_element_type=F32)


def _dot_nt(a, b):
    return lax.dot_general(a, b, _NT, preferred_element_type=F32)


def _dot_tn(a, b):
    return lax.dot_general(a, b, _TN, preferred_element_type=F32)


def _rstd(x):
    return lax.rsqrt(jnp.mean(x * x, axis=-1, keepdims=True) + RMS_EPS)


def _rms(x, g):
    return (x * _rstd(x)) * g


def _rms_bwd(dy, x, g):
    r = _rstd(x)
    xhat = x * r
    dxhat = dy * g
    dx = r * (dxhat - xhat * jnp.mean(dxhat * xhat, axis=-1, keepdims=True))
    return dx, jnp.sum(dy * xhat, axis=0, keepdims=True)


def _rope(t, cc, sn, sp):
    return t * cc + pltpu.roll(t, 96, 1) * sn + pltpu.roll(t, 32, 1) * sp


def _rope_bwd(dt, cc, sn, sp):
    return dt * cc + pltpu.roll(dt * sn, 32, 1) + pltpu.roll(dt * sp, 96, 1)


def _row_block(s):
    return min(512, s)


_TN_ROWS = 1024
_FFN_ROWS = 1024


def _mla_in_fwd(x, g0, w_in, g_cq, g_ckv, cc, sn, sp, carry=None):
    s = x.shape[0]
    tm = _row_block(s)

    def body(x_ref, g0_ref, w_ref, gcq_ref, gckv_ref, cc_ref, sn_ref, sp_ref,
             n_ref, proj_ref, cqn_ref, ckvn_ref, krr_ref):
        nb = _rms(x_ref[...], g0_ref[...]).astype(BF16)
        n_ref[...] = nb
        proj = _dot(nb, w_ref[...])
        proj_ref[...] = proj
        cqn_ref[...] = _rms(proj[:, :CQ], gcq_ref[...]).astype(BF16)
        ckvn_ref[...] = _rms(proj[:, CQ:CQ + CKV], gckv_ref[...]).astype(BF16)
        krr_ref[...] = _rope(proj[:, CQ + CKV:], cc_ref[...], sn_ref[...], sp_ref[...]).astype(BF16)

    row = lambda n: pl.BlockSpec((tm, n), lambda i: (i, 0))
    full = lambda a: pl.BlockSpec(a.shape, lambda i: (0, 0))
    return _pc_carrying(
        body, carry, (x, g0, w_in, g_cq, g_ckv, cc, sn, sp), name="mla_in_fwd", grid=(s // tm,),
        in_specs=[row(D_MODEL), full(g0), full(w_in), full(g_cq), full(g_ckv), row(128), row(128), row(128)],
        out_specs=[row(D_MODEL), row(PROJ_PAD), row(CQ), row(CKV), row(128)],
        out_shape=[_sds((s, D_MODEL), BF16), _sds((s, PROJ_PAD), F32), _sds((s, CQ), BF16),
                   _sds((s, CKV), BF16), _sds((s, 128), BF16)],
    )


def _qkv_proj(cqn, ckvn, krr, w_uq, w_ukv, cc, sn, sp, carry=None):
    s = cqn.shape[0]
    tm = _row_block(s)

    def body(cqn_ref, ckvn_ref, krr_ref, wuq_ref, wukv_ref, cc_ref, sn_ref, sp_ref, q_ref, k_ref, v_ref):
        cqn_b, ckvn_b, krr_b = cqn_ref[...], ckvn_ref[...], krr_ref[...]
        cc_b, sn_b, sp_b = cc_ref[...], sn_ref[...], sp_ref[...]
        for h in range(N_HEADS):
            q = _dot(cqn_b, wuq_ref[h]) * Q_SCALE
            q_ref[h, :, :NOPE] = q[:, :NOPE].astype(BF16)
            q_ref[h, :, NOPE:] = _rope(q[:, NOPE:], cc_b, sn_b, sp_b).astype(BF16)
            kv = _dot(ckvn_b, wukv_ref[h])
            k_ref[h, :, :NOPE] = kv[:, :NOPE].astype(BF16)
            k_ref[h, :, NOPE:] = krr_b
            v_ref[h] = kv[:, NOPE:].astype(BF16)

    row = lambda n: pl.BlockSpec((tm, n), lambda i: (i, 0))
    head_w = lambda a: pl.BlockSpec(a.shape, lambda i: (0, 0, 0))
    head_o = lambda n: pl.BlockSpec((N_HEADS, tm, n), lambda i: (0, i, 0))
    return _pc_carrying(
        body, carry, (cqn, ckvn, krr, w_uq, w_ukv, cc, sn, sp), name="qkv_proj", grid=(s // tm,),
        in_specs=[row(CQ), row(CKV), row(128), head_w(w_uq), head_w(w_ukv), row(128), row(128), row(128)],
        out_specs=[head_o(HEAD_PAD), head_o(HEAD_PAD), head_o(V_DIM)],
        out_shape=[_sds((N_HEADS, s, HEAD_PAD), BF16), _sds((N_HEADS, s, HEAD_PAD), BF16),
                   _sds((N_HEADS, s, V_DIM), BF16)],
    )


def _chunk_bias(bq, bk, first_key=0):
    rows = lax.broadcasted_iota(jnp.int32, (bq, bk), 0)
    cols = lax.broadcasted_iota(jnp.int32, (bq, bk), 1) + first_key
    visible = jnp.right_shift(cols, CHUNK_SHIFT) <= jnp.right_shift(rows, CHUNK_SHIFT)
    return jnp.where(visible, 0.0, NEG).astype(F32)


_ATTN_FWD_ROWS = 1024


def _attn_fwd(q, k, v, carry=None):
    s = q.shape[1]
    bk = _row_block(s)
    bq = min(_ATTN_FWD_ROWS, s)
    nd = bq // bk

    def body(q_ref, k_ref, v_ref, o_ref, lse_ref, bias_ref):
        i = pl.program_id(1)
        qb = q_ref[0]

        @pl.when((pl.program_id(0) == 0) & (i == 0))
        def _():
            bias_ref[...] = _chunk_bias(bq, bk)

        def block(j, carry, queries, bias=None):
            m, l, acc = carry
            start = pl.multiple_of(j * bk, bk)
            kb = k_ref[0, pl.ds(start, bk), :]
            vb = v_ref[0, pl.ds(start, bk), :]
            sc = _dot_nt(queries, kb)
            if bias is not None:
                sc = sc + bias
            m_new = jnp.maximum(m, jnp.max(sc, axis=-1, keepdims=True))
            p = jnp.exp2(sc - m_new)
            alpha = jnp.exp2(m - m_new)
            l = alpha * l + jnp.sum(p, axis=-1, keepdims=True)
            acc = alpha * acc + _dot(p.astype(BF16), vb)
            return m_new, l, acc

        carry = (jnp.full((bq, 1), NEG, F32), jnp.zeros((bq, 1), F32), jnp.zeros((bq, V_DIM), F32))
        carry = lax.fori_loop(0, i * nd, lambda j, c: block(j, c, qb), carry)
        for d in range(nd):
            first = d * bk
            carry = block(i * nd + d, carry, qb[first:], bias_ref[:bq - first, :])
            m, l, acc = (c[:bk] for c in carry)
            o_ref[first:first + bk, :] = (acc / l).astype(BF16)
            lse_ref[0, first:first + bk, :] = jnp.broadcast_to(m + jnp.log(l) * LOG2_E, (bk, 128))
            if d < nd - 1:
                carry = tuple(c[bk:] for c in carry)

    return _pc_carrying(
        body, carry, (q, k, v), name="attn_fwd", grid=(N_HEADS, s // bq),
        in_specs=[pl.BlockSpec((1, bq, HEAD_PAD), lambda h, i: (h, i, 0)),
                  pl.BlockSpec((1, s, HEAD_PAD), lambda h, i: (h, 0, 0)),
                  pl.BlockSpec((1, s, V_DIM), lambda h, i: (h, 0, 0))],
        out_specs=[pl.BlockSpec((bq, V_DIM), lambda h, i: (i, h)),
                   pl.BlockSpec((1, bq, 128), lambda h, i: (h, i, 0))],
        out_shape=[_sds((s, N_HEADS * V_DIM), BF16), _sds((N_HEADS, s, 128), F32)],
        scratch_shapes=[pltpu.VMEM((bq, bk), F32)],
    )


def _matmul_res(a, w, res, name):
    s, kd = a.shape
    n = w.shape[1]
    tm = _row_block(s)

    def body(a_ref, w_ref, r_ref, o_ref):
        o_ref[...] = r_ref[...] + _dot(a_ref[...], w_ref[...])

    return _pc(
        body, name=name, grid=(s // tm,),
        in_specs=[pl.BlockSpec((tm, kd), lambda i: (i, 0)), pl.BlockSpec((kd, n), lambda i: (0, 0)),
                  pl.BlockSpec((tm, n), lambda i: (i, 0))],
        out_specs=pl.BlockSpec((tm, n), lambda i: (i, 0)),
        out_shape=_sds((s, n), F32),
    )(a, w, res)


def _ffn_fwd(h, gain, wg, wu, wd, name, carry=None):
    s = h.shape[0]
    tm = min(_FFN_ROWS, s)
    tf = 512
    nf = FF_PAD // tf

    def body(h_ref, g_ref, wg_ref, wu_ref, wd_ref, o_ref, n_ref, gate_ref, up_ref, acc_ref):
        j = pl.program_id(1)

        @pl.when(j == 0)
        def _():
            n_ref[...] = _rms(h_ref[...], g_ref[...]).astype(BF16)
            acc_ref[...] = jnp.zeros_like(acc_ref)

        nb = n_ref[...]
        gate = _dot(nb, wg_ref[...])
        up = _dot(nb, wu_ref[...])
        gate_ref[...] = gate.astype(BF16)
        up_ref[...] = up.astype(BF16)
        act = gate * jax.nn.sigmoid(gate) * up
        acc_ref[...] += _dot(act.astype(BF16), wd_ref[...])

        @pl.when(j == nf - 1)
        def _():
            o_ref[...] = h_ref[...] + acc_ref[...]

    return _pc_carrying(
        body, carry, (h, gain, wg, wu, wd), name=name, grid=(s // tm, nf),
        in_specs=[pl.BlockSpec((tm, D_MODEL), lambda i, j: (i, 0)), pl.BlockSpec((1, D_MODEL), lambda i, j: (0, 0)),
                  pl.BlockSpec((D_MODEL, tf), lambda i, j: (0, j)), pl.BlockSpec((D_MODEL, tf), lambda i, j: (0, j)),
                  pl.BlockSpec((tf, D_MODEL), lambda i, j: (j, 0))],
        out_specs=[pl.BlockSpec((tm, D_MODEL), lambda i, j: (i, 0)), pl.BlockSpec((tm, D_MODEL), lambda i, j: (i, 0)),
                   pl.BlockSpec((tm, tf), lambda i, j: (i, j)), pl.BlockSpec((tm, tf), lambda i, j: (i, j))],
        out_shape=[_sds((s, D_MODEL), F32), _sds((s, D_MODEL), BF16), _sds((s, FF_PAD), BF16),
                   _sds((s, FF_PAD), BF16)],
        scratch_shapes=[pltpu.VMEM((tm, D_MODEL), F32)],
    )


def _rms_matmul(h, gain, w, name):
    s = h.shape[0]
    n = w.shape[1]
    tm = min(_FFN_ROWS, s)
    tn = 1024
    nn = n // tn

    def body(h_ref, g_ref, w_ref, n_ref, o_ref):
        @pl.when(pl.program_id(1) == 0)
        def _():
            n_ref[...] = _rms(h_ref[...], g_ref[...]).astype(BF16)

        o_ref[...] = _dot(n_ref[...], w_ref[...]).astype(BF16)

    return _pc(
        body, name=name, grid=(s // tm, nn),
        in_specs=[pl.BlockSpec((tm, D_MODEL), lambda i, j: (i, 0)), pl.BlockSpec((1, D_MODEL), lambda i, j: (0, 0)),
                  pl.BlockSpec((D_MODEL, tn), lambda i, j: (0, j))],
        out_specs=[pl.BlockSpec((tm, D_MODEL), lambda i, j: (i, 0)), pl.BlockSpec((tm, tn), lambda i, j: (i, j))],
        out_shape=[_sds((s, D_MODEL), BF16), _sds((s, n), BF16)],
    )(h, gain, w)


def _shift_down(u, k, rows):
    return jnp.where(rows >= k, pltpu.roll(u, k, 0), 0.0)


def _shift_up(u, k, rows, s):
    return jnp.where(rows < s - k, pltpu.roll(u, s - k, 0), 0.0)


_CONV_COLS = 128


def _conv_fwd(bcx, cw):
    s = bcx.shape[0]
    tc = _CONV_COLS
    nc = D_MODEL // tc

    def body(b_ref, c_ref, xp_ref, w_ref, o_ref):
        rows = lax.broadcasted_iota(jnp.int32, (s, tc), 0)
        u = c_ref[...].astype(F32) * xp_ref[...].astype(F32)
        w = w_ref[...]
        uc = w[2:3] * u + w[1:2] * _shift_down(u, 1, rows) + w[0:1] * _shift_down(u, 2, rows)
        o_ref[...] = (b_ref[...].astype(F32) * uc).astype(BF16)

    col = lambda off: pl.BlockSpec((s, tc), lambda j: (0, off + j))
    return _pc(
        body, name="conv_fwd", grid=(nc,),
        in_specs=[col(0), col(nc), col(2 * nc), pl.BlockSpec((3, tc), lambda j: (0, j))],
        out_specs=pl.BlockSpec((s, tc), lambda j: (0, j)),
        out_shape=_sds((s, D_MODEL), BF16),
    )(bcx, bcx, bcx, cw)


def _final_loss(h, gain, target):
    s = h.shape[0]
    tm = _row_block(s)

    def body(h_ref, g_ref, t_ref, dh_ref, dg_ref, loss_ref):
        i = pl.program_id(0)

        @pl.when(i == 0)
        def _():
            dg_ref[...] = jnp.zeros_like(dg_ref)
            loss_ref[...] = jnp.zeros_like(loss_ref)

        hb = h_ref[...]
        e = _rms(hb, g_ref[...]) - t_ref[...]
        loss_ref[...] += 0.5 * jnp.sum(jnp.mean(e * e, axis=-1, keepdims=True))
        dx, dg = _rms_bwd(e * (1.0 / D_MODEL), hb, g_ref[...])
        dh_ref[...] = dx
        dg_ref[...] += dg

    row = pl.BlockSpec((tm, D_MODEL), lambda i: (i, 0))
    vec = pl.BlockSpec((1, D_MODEL), lambda i: (0, 0))
    return _pc(
        body, name="final_loss", grid=(s // tm,),
        in_specs=[row, vec, row],
        out_specs=[row, vec, pl.BlockSpec((8, D_MODEL), lambda i: (0, 0))],
        out_shape=[_sds((s, D_MODEL), F32), _sds((1, D_MODEL), F32), _sds((8, D_MODEL), F32)],
    )(h, gain, target)


def _ffn_bwd_x(dho, h, gain, gate, up, wg, wu, wd, name, carry=None):
    s = h.shape[0]
    tm = _row_block(s)
    tf = 1024
    nf = FF_PAD // tf

    def body(dho_ref, h_ref, g_ref, gate_ref, up_ref, wg_ref, wu_ref, wd_ref,
             dh_ref, dhb_ref, dhob_ref, dgate_ref, dup_ref, act_ref, dgain_ref, acc_ref):
        i = pl.program_id(0)
        j = pl.program_id(1)

        @pl.when(j == 0)
        def _():
            dhob_ref[...] = dho_ref[...].astype(BF16)
            acc_ref[...] = jnp.zeros_like(acc_ref)

        @pl.when((i == 0) & (j == 0))
        def _():
            dgain_ref[...] = jnp.zeros_like(dgain_ref)

        dact = _dot_nt(dhob_ref[...], wd_ref[...])
        g = gate_ref[...].astype(F32)
        u = up_ref[...].astype(F32)
        sg = jax.nn.sigmoid(g)
        silu = g * sg
        dg = (dact * u * (sg * (1.0 + g * (1.0 - sg)))).astype(BF16)
        du = (dact * silu).astype(BF16)
        dgate_ref[...] = dg
        dup_ref[...] = du
        act_ref[...] = (silu * u).astype(BF16)
        acc_ref[...] += _dot_nt(dg, wg_ref[...]) + _dot_nt(du, wu_ref[...])

        @pl.when(j == nf - 1)
        def _():
            dx, dgain = _rms_bwd(acc_ref[...], h_ref[...], g_ref[...])
            dh = dho_ref[...] + dx
            dh_ref[...] = dh
            dhb_ref[...] = dh.astype(BF16)
            dgain_ref[...] += dgain

    row = pl.BlockSpec((tm, D_MODEL), lambda i, j: (i, 0))
    vec = pl.BlockSpec((1, D_MODEL), lambda i, j: (0, 0))
    hid = pl.BlockSpec((tm, tf), lambda i, j: (i, j))
    wcol = pl.BlockSpec((D_MODEL, tf), lambda i, j: (0, j))
    wrow = pl.BlockSpec((tf, D_MODEL), lambda i, j: (j, 0))
    return _pc_carrying(
        body, carry, (dho, h, gain, gate, up, wg, wu, wd), name=name, grid=(s // tm, nf),
        in_specs=[row, row, vec, hid, hid, wcol, wcol, wrow],
        out_specs=[row, row, row, hid, hid, hid, vec],
        out_shape=[_sds((s, D_MODEL), F32), _sds((s, D_MODEL), BF16), _sds((s, D_MODEL), BF16),
                   _sds((s, FF_PAD), BF16), _sds((s, FF_PAD), BF16), _sds((s, FF_PAD), BF16),
                   _sds((1, D_MODEL), F32)],
        scratch_shapes=[pltpu.VMEM((tm, D_MODEL), F32)],
    )


def _nt_rmsbwd(a, w, h, gain, dho, name, carry=None):
    stacked = a.ndim == 3
    if stacked:
        nk, s, tk = a.shape
    else:
        s, tk = a.shape
        nk = 1
    tm = min(_FFN_ROWS, s) if stacked else _row_block(s)

    def body(a_ref, w_ref, h_ref, g_ref, dho_ref, dh_ref, dhb_ref, dgain_ref, acc_ref):
        i = pl.program_id(0)
        j = pl.program_id(1)
        if stacked:
            a_ref = a_ref.at[0]

        @pl.when(j == 0)
        def _():
            acc_ref[...] = jnp.zeros_like(acc_ref)

        @pl.when((i == 0) & (j == 0))
        def _():
            dgain_ref[...] = jnp.zeros_like(dgain_ref)

        acc_ref[...] += _dot_nt(a_ref[...], w_ref[...])

        @pl.when(j == nk - 1)
        def _():
            dx, dgain = _rms_bwd(acc_ref[...], h_ref[...], g_ref[...])
            dh = dho_ref[...] + dx
            dh_ref[...] = dh
            dhb_ref[...] = dh.astype(BF16)
            dgain_ref[...] += dgain

    row = pl.BlockSpec((tm, D_MODEL), lambda i, j: (i, 0))
    vec = pl.BlockSpec((1, D_MODEL), lambda i, j: (0, 0))
    return _pc_carrying(
        body, carry, (a, w, h, gain, dho), name=name, grid=(s // tm, nk),
        in_specs=[pl.BlockSpec((1, tm, tk), lambda i, j: (j, i, 0)) if stacked else pl.BlockSpec((tm, tk), lambda i, j: (i, 0)),
                  pl.BlockSpec((D_MODEL, tk), lambda i, j: (0, j)), row, vec, row],
        out_specs=[row, row, vec],
        out_shape=[_sds((s, D_MODEL), F32), _sds((s, D_MODEL), BF16), _sds((1, D_MODEL), F32)],
        scratch_shapes=[pltpu.VMEM((tm, D_MODEL), F32)],
    )


def _matmul_nt(a, w, name):
    s, kd = a.shape
    n = w.shape[0]
    tm = _row_block(s)

    def body(a_ref, w_ref, o_ref):
        o_ref[...] = _dot_nt(a_ref[...], w_ref[...]).astype(BF16)

    return _pc(
        body, name=name, grid=(s // tm,),
        in_specs=[pl.BlockSpec((tm, kd), lambda i: (i, 0)), pl.BlockSpec((n, kd), lambda i: (0, 0))],
        out_specs=pl.BlockSpec((tm, n), lambda i: (i, 0)),
        out_shape=_sds((s, n), BF16),
    )(a, w)


def _matmul_tn(a, b, name, carry=None):
    s, m = a.shape
    stacked = b.ndim == 3
    tmm = min(m, 1024)
    if stacked:
        n, tn = b.shape[0] * b.shape[2], b.shape[2]
    else:
        n = b.shape[1]
        tn = n if n <= 1024 else 1024
    tk = min(_TN_ROWS, s)
    nk = s // tk

    def body(a_ref, b_ref, o_ref, acc_ref):
        k = pl.program_id(2)
        if stacked:
            b_ref = b_ref.at[0]

        @pl.when(k == 0)
        def _():
            acc_ref[...] = jnp.zeros_like(acc_ref)

        acc_ref[...] += _dot_tn(a_ref[...], b_ref[...])

        @pl.when(k == nk - 1)
        def _():
            o_ref[...] = acc_ref[...].astype(BF16)

    (out,), got = _pc_carrying(
        body, carry, (a, b), name=name, grid=(m // tmm, n // tn, nk),
        in_specs=[pl.BlockSpec((tk, tmm), lambda i, j, k: (k, i)),
                  pl.BlockSpec((1, tk, tn), lambda i, j, k: (j, k, 0)) if stacked
                  else pl.BlockSpec((tk, tn), lambda i, j, k: (k, j))],
        out_specs=[pl.BlockSpec((tmm, tn), lambda i, j, k: (i, j))],
        out_shape=[_sds((m, n), BF16)],
        scratch_shapes=[pltpu.VMEM((tmm, tn), F32)],
    )
    return out if carry is None else (out, got)


def _conv_bwd(dbuc, bcx, cw):
    s = bcx.shape[0]
    tc = _CONV_COLS
    nc = D_MODEL // tc

    def body(d_ref, b_ref, c_ref, xp_ref, w_ref, dbcx_ref, dw_ref):
        rows = lax.broadcasted_iota(jnp.int32, (s, tc), 0)
        c = c_ref[...].astype(F32)
        xp = xp_ref[...].astype(F32)
        u = c * xp
        u1 = _shift_down(u, 1, rows)
        u2 = _shift_down(u, 2, rows)
        w = w_ref[...]
        uc = w[2:3] * u + w[1:2] * u1 + w[0:1] * u2
        d = d_ref[...].astype(F32)
        dbcx_ref[0] = (d * uc).astype(BF16)
        duc = d * b_ref[...].astype(F32)
        du = w[2:3] * duc + w[1:2] * _shift_up(duc, 1, rows, s) + w[0:1] * _shift_up(duc, 2, rows, s)
        dbcx_ref[1] = (du * xp).astype(BF16)
        dbcx_ref[2] = (du * c).astype(BF16)
        dw_ref[0:1, :] = jnp.sum(duc * u2, axis=0, keepdims=True)
        dw_ref[1:2, :] = jnp.sum(duc * u1, axis=0, keepdims=True)
        dw_ref[2:3, :] = jnp.sum(duc * u, axis=0, keepdims=True)

    col = lambda off: pl.BlockSpec((s, tc), lambda j: (0, off + j))
    tap = pl.BlockSpec((3, tc), lambda j: (0, j))
    return _pc(
        body, name="conv_bwd", grid=(nc,),
        in_specs=[col(0), col(0), col(nc), col(2 * nc), tap],
        out_specs=[pl.BlockSpec((3, s, tc), lambda j: (0, 0, j)), tap],
        out_shape=[_sds((3, s, D_MODEL), BF16), _sds((3, D_MODEL), F32)],
    )(dbuc, bcx, bcx, bcx, cw)


def _attn_bwd(q, k, v, do, o, lse, cc, sn, sp, carry=None):
    s = q.shape[1]
    bk = _row_block(s)
    nb = s // bk

    def body(q_ref, k_ref, v_ref, do_ref, o_ref, lse_ref, cc_ref, sn_ref, sp_ref,
             dq_ref, dk_ref, dv_ref, dqacc_ref, delta_ref, bias_ref):
        j = pl.program_id(1)
        kb = k_ref[0]
        vb = v_ref[0]

        @pl.when((pl.program_id(0) == 0) & (j == 0))
        def _():
            bias_ref[...] = _chunk_bias(bk, bk)

        @pl.when(j == 0)
        def _():
            dqacc_ref[...] = jnp.zeros_like(dqacc_ref)

            def fill(i, _):
                rows = pl.ds(pl.multiple_of(i * bk, bk), bk)
                d = jnp.sum(do_ref[rows, :].astype(F32) * o_ref[rows, :].astype(F32), axis=-1, keepdims=True)
                delta_ref[rows, :] = jnp.broadcast_to(d, (bk, 128))
                return 0

            lax.fori_loop(0, nb, fill, 0)

        def step(i, carry, masked):
            dk, dv = carry
            rows = pl.ds(pl.multiple_of(i * bk, bk), bk)
            qb = q_ref[0, rows, :]
            dob = do_ref[rows, :]
            sc = _dot_nt(qb, kb)
            if masked:
                sc = sc + bias_ref[...]
            p = jnp.exp2(sc - lse_ref[0, rows, :][:, :1])
            dv = dv + _dot_tn(p.astype(BF16), dob)
            ds = (p * (_dot_nt(dob, vb) - delta_ref[rows, :][:, :1])).astype(BF16)
            dk = dk + _dot_tn(ds, qb)
            dqacc_ref[rows, :] += _dot(ds, kb)
            return dk, dv

        carry = step(j, (jnp.zeros((bk, HEAD_PAD), F32), jnp.zeros((bk, V_DIM), F32)), True)
        dq = dqacc_ref[pl.ds(pl.multiple_of(j * bk, bk), bk), :] * ATT_SCALE
        dq_ref[:, :NOPE] = dq[:, :NOPE].astype(BF16)
        dq_ref[:, NOPE:] = _rope_bwd(dq[:, NOPE:], cc_ref[...], sn_ref[...], sp_ref[...]).astype(BF16)
        dk, dv = lax.fori_loop(j + 1, nb, lambda i, c: step(i, c, False), carry)
        dk_ref[0] = (dk * LN_2).astype(BF16)
        dv_ref[0] = dv.astype(BF16)

    blk = lambda n: pl.BlockSpec((1, bk, n), lambda h, j: (h, j, 0))
    whole = lambda n: pl.BlockSpec((1, s, n), lambda h, j: (h, 0, 0))
    cols = pl.BlockSpec((s, V_DIM), lambda h, j: (0, h))
    tab = pl.BlockSpec((bk, 128), lambda h, j: (j, 0))
    return _pc_carrying(
        body, carry, (q, k, v, do, o, lse, cc, sn, sp), name="attn_bwd", grid=(N_HEADS, nb),
        in_specs=[whole(HEAD_PAD), blk(HEAD_PAD), blk(V_DIM), cols, cols, whole(128), tab, tab, tab],
        out_specs=[pl.BlockSpec((bk, HEAD_PAD), lambda h, j: (j, h)), blk(HEAD_PAD), blk(V_DIM)],
        out_shape=[_sds((s, N_HEADS * HEAD_PAD), BF16), _sds((N_HEADS, s, HEAD_PAD), BF16),
                   _sds((N_HEADS, s, V_DIM), BF16)],
        scratch_shapes=[pltpu.VMEM((s, HEAD_PAD), F32), pltpu.VMEM((s, 128), F32), pltpu.VMEM((bk, bk), F32)],
    )


def _mla_mid_bwd(dq, dk, dv, proj, w_uq, w_ukv, g_cq, g_ckv, cc, sn, sp, carry=None):
    s = dq.shape[0]
    tm = _row_block(s)

    def body(dq_ref, dk_ref, dv_ref, proj_ref, wuq_ref, wukv_ref, gcq_ref, gckv_ref, cc_ref, sn_ref, sp_ref,
             dproj_ref, dkv_ref, dgcq_ref, dgckv_ref, acq_ref, ackv_ref, akr_ref):
        @pl.when(pl.program_id(0) == 0)
        def _():
            dgcq_ref[...] = jnp.zeros_like(dgcq_ref)
            dgckv_ref[...] = jnp.zeros_like(dgckv_ref)

        for h in range(N_HEADS):
            cols = slice(h * HEAD_PAD, (h + 1) * HEAD_PAD)
            dkb = dk_ref[h]
            dkv_ref[:, h * HEAD_PAD:h * HEAD_PAD + NOPE] = dkb[:, :NOPE]
            dkv_ref[:, h * HEAD_PAD + NOPE:(h + 1) * HEAD_PAD] = dv_ref[h]
            cq_part = _dot_nt(dq_ref[:, cols], wuq_ref[h])
            ckv_part = _dot_nt(dkv_ref[:, cols], wukv_ref[h])
            kr_part = dkb[:, NOPE:].astype(F32)
            if h == 0:
                acq_ref[...], ackv_ref[...], akr_ref[...] = cq_part, ckv_part, kr_part
            else:
                acq_ref[...] += cq_part
                ackv_ref[...] += ckv_part
                akr_ref[...] += kr_part

        proj = proj_ref[...]
        dcq, dgcq = _rms_bwd(acq_ref[...], proj[:, :CQ], gcq_ref[...])
        dckv, dgckv = _rms_bwd(ackv_ref[...], proj[:, CQ:CQ + CKV], gckv_ref[...])
        dproj_ref[:, :CQ] = dcq.astype(BF16)
        dproj_ref[:, CQ:CQ + CKV] = dckv.astype(BF16)
        dproj_ref[:, CQ + CKV:] = _rope_bwd(akr_ref[...], cc_ref[...], sn_ref[...], sp_ref[...]).astype(BF16)
        dgcq_ref[...] += dgcq
        dgckv_ref[...] += dgckv

    head_blk = lambda n: pl.BlockSpec((N_HEADS, tm, n), lambda i: (0, i, 0))
    head_cols = pl.BlockSpec((tm, N_HEADS * HEAD_PAD), lambda i: (i, 0))
    head_w = lambda a: pl.BlockSpec(a.shape, lambda i: (0, 0, 0))
    row = lambda n: pl.BlockSpec((tm, n), lambda i: (i, 0))
    vec = lambda n: pl.BlockSpec((1, n), lambda i: (0, 0))
    return _pc_carrying(
        body, carry, (dq, dk, dv, proj, w_uq, w_ukv, g_cq, g_ckv, cc, sn, sp),
        name="mla_mid_bwd", grid=(s // tm,),
        in_specs=[head_cols, head_blk(HEAD_PAD), head_blk(V_DIM), row(PROJ_PAD), head_w(w_uq), head_w(w_ukv),
                  vec(CQ), vec(CKV), row(128), row(128), row(128)],
        out_specs=[row(PROJ_PAD), head_cols, vec(CQ), vec(CKV)],
        out_shape=[_sds((s, PROJ_PAD), BF16), _sds((s, N_HEADS * HEAD_PAD), BF16), _sds((1, CQ), F32),
                   _sds((1, CKV), F32)],
        scratch_shapes=[pltpu.VMEM((tm, CQ), F32), pltpu.VMEM((tm, CKV), F32), pltpu.VMEM((tm, 128), F32)],
    )


def _peer(k):
    x, y, c = lax.axis_index("x"), lax.axis_index("y"), lax.axis_index("c")
    px = 1 - x if k & 4 else x
    py = 1 - y if k & 2 else y
    pc = 1 - c if k & 1 else c
    return (px, py, pc), 4 * px + 2 * py + pc


def _exchange_ops(arrays, scatter):
    nw = len(arrays)
    by_cols = [isinstance(a, _Cols) for a in arrays]
    arrays = [a.array if isinstance(a, _Cols) else a for a in arrays]
    direct = range(1, N_DEV) if scatter else (1, 2, 4, 6)

    def columns(ref, idx, width):
        return ref.at[:, pl.ds(pl.multiple_of(idx * width, 128), width)]

    def sent(ins, w, idx):
        if not scatter:
            return ins[w]
        return columns(ins[w], idx, arrays[w].shape[1] // N_DEV) if by_cols[w] else ins[w].at[idx]

    def slot(outs, w, idx):
        if by_cols[w] and not scatter:
            return columns(outs[w], idx, arrays[w].shape[1])
        return outs[w].at[idx]

    def copy(w, k, src, dst, to, send_sems, recv_sems):
        return pltpu.make_async_remote_copy(
            src_ref=src, dst_ref=dst, send_sem=send_sems.at[w * N_DEV + k], recv_sem=recv_sems.at[w * N_DEV + k],
            device_id=to, device_id_type=pl.DeviceIdType.MESH)

    def own_copies(ins, outs, local_sems):
        _, me = _peer(0)
        return [pltpu.make_async_copy(sent(ins, w, me), slot(outs, w, me), local_sems.at[w]) for w in range(nw)]

    def sends(ins, outs, send_sems, recv_sems):
        _, me = _peer(0)
        out = []
        for k in direct:
            dev, idx = _peer(k)
            for w in range(nw):
                out.append(copy(w, k, sent(ins, w, idx), slot(outs, w, me), dev, send_sems, recv_sems))
        return out

    def relays(outs, send_sems, recv_sems):
        sibling, _ = _peer(1)
        out = []
        for k in (2, 4, 6):
            _, idx = _peer(k)
            for w in range(nw):
                out.append(copy(w, k + 1, slot(outs, w, idx), slot(outs, w, idx), sibling, send_sems, recv_sems))
        return out

    def arrival(outs, w, k, send_sems, recv_sems):
        dev, idx = _peer(k)
        return copy(w, k, slot(outs, w, idx), slot(outs, w, idx), dev, send_sems, recv_sems)

    def start(ins, outs, send_sems, recv_sems, local_sems):
        for cp in own_copies(ins, outs, local_sems) + sends(ins, outs, send_sems, recv_sems):
            cp.start()

    def relay(ins, outs, send_sems, recv_sems, local_sems):
        for k in (2, 4, 6):
            for w in range(nw):
                arrival(outs, w, k, send_sems, recv_sems).wait_recv()
        for cp in relays(outs, send_sems, recv_sems):
            cp.start()

    def wait(ins, outs, send_sems, recv_sems, local_sems):
        for cp in own_copies(ins, outs, local_sems):
            cp.wait()
        for cp in sends(ins, outs, send_sems, recv_sems) + ([] if scatter else relays(outs, send_sems, recv_sems)):
            cp.wait_send()
        for k in (range(1, N_DEV) if scatter else (1, 3, 5, 7)):
            for w in range(nw):
                arrival(outs, w, k, send_sems, recv_sems).wait_recv()

    def landed_shape(a, cols):
        if not cols:
            return a.shape if scatter else (N_DEV,) + a.shape
        r, c = a.shape
        return (N_DEV, r, c // N_DEV) if scatter else (r, N_DEV * c)

    landed = [_sds(landed_shape(a, cols), a.dtype) for a, cols in zip(arrays, by_cols)]
    sems = [pltpu.SemaphoreType.DMA((nw * N_DEV,)), pltpu.SemaphoreType.DMA((nw * N_DEV,)),
            pltpu.SemaphoreType.DMA((nw,))]
    return start, (None if scatter else relay), wait, landed, sems, arrays


class _Cols:
    def __init__(self, array):
        self.array = array


def _exchange(arrays, scatter, name):
    nw = len(arrays)
    start, relay, wait, landed, sems, arrays = _exchange_ops(arrays, scatter)

    def body(*refs):
        ins, outs, csem = refs[:nw], refs[nw:2 * nw], refs[2 * nw:]
        start(ins, outs, *csem)
        if relay is not None:
            relay(ins, outs, *csem)
        wait(ins, outs, *csem)

    any_spec = pl.BlockSpec(memory_space=pl.ANY)
    return _pc(body, name=name, in_specs=[any_spec] * nw, out_specs=[any_spec] * nw, out_shape=landed,
               scratch_shapes=sems)(*arrays)


def _adam_math(g, w, m, v):
    m = ADAM_B1 * m + (1.0 - ADAM_B1) * g
    v = ADAM_B2 * v + (1.0 - ADAM_B2) * jnp.square(g)
    m_hat = m / (1.0 - ADAM_B1 ** ADAM_STEP)
    v_hat = v / (1.0 - ADAM_B2 ** ADAM_STEP)
    delta = -ADAM_LR * (m_hat / (jnp.sqrt(v_hat) + ADAM_EPS) + ADAM_WD * w)
    return delta, m, v


def _adam_rows(r):
    for t in range(min(r, 512) // 16 * 16, 0, -16):
        if r % t == 0:
            return t
    return r


def _adamw(parts, w, m, v, name):
    nl, r, c = w.shape
    tr = _adam_rows(r)
    nr = r // tr

    def body(*refs):
        p_refs, (w_ref, m_ref, v_ref, g_ref, d_ref, mo_ref, vo_ref) = refs[:nl], refs[nl:]
        for layer in range(nl):
            @pl.when(pl.program_id(0) == layer)
            def _(p_ref=p_refs[layer]):
                g = p_ref[0].astype(F32)
                for src in range(1, N_DEV):
                    g = g + p_ref[src].astype(F32)
                delta, m2, v2 = _adam_math(g, w_ref[0], m_ref[0], v_ref[0])
                g_ref[0] = g
                d_ref[0] = delta
                mo_ref[0] = m2
                vo_ref[0] = v2

    def part_spec(layer):
        def index(l, i):
            return 0, jnp.where(l == layer, i, jnp.where(l < layer, 0, nr - 1)), 0
        return pl.BlockSpec((N_DEV, tr, c), index)

    blk = pl.BlockSpec((1, tr, c), lambda l, i: (l, i, 0))
    return _pc(
        body, name=name, grid=(nl, nr),
        in_specs=[part_spec(layer) for layer in range(nl)] + [blk, blk, blk],
        out_specs=[blk, blk, blk, blk],
        out_shape=[_sds((nl, r, c), F32)] * 4,
    )(*parts, w, m, v)


def _small_allreduce_adamw(gpack, wpack, mpack, vpack, last_grads):
    shape = gpack.shape
    compact = (SMALL_ROWS, D_MODEL)
    nw = len(last_grads)
    start, _, wait, landed, exchange_sems, last_grads = _exchange_ops(last_grads, True)

    def body(*refs):
        (g_ref, w_ref, m_ref, v_ref), rest = refs[:4], refs[4:]
        cin, rest = rest[:nw], rest[nw:]
        (go_ref, d_ref, mo_ref, vo_ref), rest = rest[:4], rest[4:]
        cout, rest = rest[:nw], rest[nw:]
        (comp_ref, gath_ref, send_sems, recv_sems), csem = rest[:4], rest[4:]
        start(cin, cout, *csem)
        _, me = _peer(0)
        comp_ref[...] = jnp.zeros(compact, F32)
        r = 0
        for p, n in enumerate(SMALL_PIECES):
            comp_ref[r:r + n, :] = g_ref[8 * p:8 * p + n, :]
            r += n
        gath_ref[me] = comp_ref[...]
        copies = []
        for k in range(1, N_DEV):
            dev, idx = _peer(k)
            copies.append(pltpu.make_async_remote_copy(
                src_ref=comp_ref, dst_ref=gath_ref.at[me], send_sem=send_sems.at[k], recv_sem=recv_sems.at[k],
                device_id=dev, device_id_type=pl.DeviceIdType.MESH))
        for cp in copies:
            cp.start()
        for cp in copies:
            cp.wait_send()
        for k in range(1, N_DEV):
            dev, idx = _peer(k)
            pltpu.make_async_remote_copy(
                src_ref=comp_ref, dst_ref=gath_ref.at[idx], send_sem=send_sems.at[k], recv_sem=recv_sems.at[k],
                device_id=dev, device_id_type=pl.DeviceIdType.MESH).wait_recv()
        g = gath_ref[0]
        for src in range(1, N_DEV):
            g = g + gath_ref[src]
        go_ref[...] = jnp.zeros(shape, F32)
        r = 0
        for p, n in enumerate(SMALL_PIECES):
            go_ref[8 * p:8 * p + n, :] = g[r:r + n, :]
            r += n
        delta, m2, v2 = _adam_math(go_ref[...], w_ref[...], m_ref[...], v_ref[...])
        d_ref[...] = delta
        mo_ref[...] = m2
        vo_ref[...] = v2
        wait(cin, cout, *csem)

    vm = pl.BlockSpec(memory_space=pltpu.VMEM)
    any_spec = pl.BlockSpec(memory_space=pl.ANY)
    res = _pc(
        body, name="small_allreduce_adamw",
        in_specs=[vm] * 4 + [any_spec] * nw, out_specs=[vm] * 4 + [any_spec] * nw,
        out_shape=[_sds(shape, F32)] * 4 + landed,
        scratch_shapes=[pltpu.VMEM(compact, F32), pltpu.VMEM((N_DEV,) + compact, F32),
                        pltpu.SemaphoreType.DMA((N_DEV,)), pltpu.SemaphoreType.DMA((N_DEV,))] + exchange_sems,
    )(gpack, wpack, mpack, vpack, *last_grads)
    return res[:4], res[4:]


def _rows_from_shards(g):
    return g.reshape(N_DEV * g.shape[1], g.shape[2])


def _shards_from_rows(a):
    return a.reshape(N_DEV, a.shape[0] // N_DEV, a.shape[1])


def _pad_to(a, rows, cols):
    return jnp.pad(a, ((0, rows - a.shape[0]), (0, cols - a.shape[1])))


def _rope_tables(pos):
    inv_freq = 1.0 / (ROPE_THETA ** (jnp.arange(0, ROPE, 2, dtype=F32) / ROPE))
    ang = pos.astype(F32)[:, None] * inv_freq
    cos, sin = jnp.cos(ang), jnp.sin(ang)
    z32, z64, z96 = (jnp.zeros((pos.shape[0], n), F32) for n in (32, 64, 96))
    return (jnp.concatenate([cos, cos, z64], axis=1), jnp.concatenate([-sin, z96], axis=1),
            jnp.concatenate([z32, sin, z64], axis=1))


def _pad_row(vec):
    vec = vec.reshape(1, -1)
    return jnp.pad(vec, ((0, 0), (0, D_MODEL - vec.shape[1])))


def _forward_backward(x, target, tables, gains, shards):
    cc, sn, sp = tables
    ffn_g = gains["ffn_norm"]
    by_cols = ("wg0", "wu0", "wg1", "wu1", "c_in")

    def gather(names, relay_at):
        return [_Cols(shards[n]) if n in by_cols else shards[n] for n in names], False, relay_at

    (got,) = _exchange([shards["w_in"]], False, "gather_w_in")
    w_in = _rows_from_shards(got)
    (n0, proj, cqn, ckvn, krr), (w_uq, w_ukv) = _mla_in_fwd(
        x, gains["mla_norm"], w_in, gains["g_cq"], gains["g_ckv"], cc, sn, sp, carry=gather(["w_uq", "w_ukv"], 0.5))
    (q, k, v), (w_o, conv_norm, conv_w) = _qkv_proj(
        cqn, ckvn, krr, w_uq, w_ukv, cc, sn, sp, carry=gather(["w_o", "conv_norm", "conv_w"], 0.5))
    w_o = _rows_from_shards(w_o)
    conv_norm = conv_norm.reshape(1, D_MODEL)
    conv_w = jnp.transpose(conv_w, (1, 0, 2)).reshape(3, D_MODEL)
    (o, lse), (wg0, wu0, wd0, c_in, c_out, wd1) = _attn_fwd(
        q, k, v, carry=gather(["wg0", "wu0", "wd0", "c_in", "c_out", "wd1"], 0.75))
    wd0 = _rows_from_shards(wd0)
    wd1 = _rows_from_shards(wd1)
    c_out = _rows_from_shards(c_out)
    h1 = _matmul_res(o, w_o, x, "mla_out_fwd")
    (h2, n1, gate0, up0), (wg1, wu1) = _ffn_fwd(
        h1, ffn_g[0:1], wg0, wu0, wd0, "ffn0_fwd", carry=gather(["wg1", "wu1"], 0.7))
    n2, bcx = _rms_matmul(h2, conv_norm, c_in, "conv_in_fwd")
    bu = _conv_fwd(bcx, conv_w)
    h3 = _matmul_res(bu, c_out, h2, "conv_out_fwd")
    (h4, n3, gate1, up1), _ = _ffn_fwd(h3, ffn_g[1:2], wg1, wu1, wd1, "ffn1_fwd")
    dh4, d_final, loss = _final_loss(h4, gains["final_norm"], target)

    small = {"final_norm": d_final}
    parts = {}

    def scatter(**blocks):
        return list(blocks), (list(blocks.values()), True, None)

    (dh3, dh3_b, dh4_b, dgate, dup, act, d_ffn1), _ = _ffn_bwd_x(
        dh4, h3, ffn_g[1:2], gate1, up1, wg1, wu1, wd1, "ffn1_bwd")
    dwg1 = _matmul_tn(dgate, n3, "ffn1_dwg")
    dwu1 = _matmul_tn(dup, n3, "ffn1_dwu")
    dwd1 = _matmul_tn(act, dh4_b, "ffn1_dwd")
    dbuc = _matmul_nt(dh3_b, c_out, "conv_out_bwd")
    d_c_out = _matmul_tn(bu, dh3_b, "conv_dwout")
    dbcx, small["conv_w"] = _conv_bwd(dbuc, bcx, conv_w)
    d_c_in = _matmul_tn(n2, dbcx, "conv_dwin")
    (dh2, dh2_b, small["conv_norm"]), _ = _nt_rmsbwd(dbcx, c_in, h2, conv_norm, dh3, "conv_in_bwd")
    names, carry = scatter(c_in=_Cols(d_c_in), c_out=_shards_from_rows(d_c_out))
    (dh1, dh1_b, dh2_b2, dgate, dup, act, d_ffn0), got = _ffn_bwd_x(
        dh2, h1, ffn_g[0:1], gate0, up0, wg0, wu0, wd0, "ffn0_bwd", carry=carry)
    parts.update(zip(names, got))
    dwg0 = _matmul_tn(dgate, n1, "ffn0_dwg")
    dwu0 = _matmul_tn(dup, n1, "ffn0_dwu")
    dwd0 = _matmul_tn(act, dh2_b2, "ffn0_dwd")
    small["ffn_norm"] = jnp.pad(d_ffn0, ((0, 7), (0, 0))) + jnp.pad(d_ffn1, ((1, 6), (0, 0)))
    do = _matmul_nt(dh1_b, w_o, "mla_out_bwd")
    d_w_o = _matmul_tn(o, dh1_b, "mla_dwo")
    names, carry = scatter(**{n: _shards_from_rows(d) for n, d in dict(
        wg0=dwg0, wg1=dwg1, wu0=dwu0, wu1=dwu1, wd0=dwd0, wd1=dwd1).items()})
    (dq, dk, dv), got = _attn_bwd(q, k, v, do, o, lse, cc, sn, sp, carry=carry)
    parts.update(zip(names, got))
    d_w_uq = _matmul_tn(dq, cqn, "mla_dwuq")
    names, carry = scatter(w_o=_shards_from_rows(d_w_o), w_uq=_shards_from_rows(d_w_uq))
    (dproj, dkv, small["g_cq"], small["g_ckv"]), got = _mla_mid_bwd(
        dq, dk, dv, proj, w_uq, w_ukv, gains["g_cq"], gains["g_ckv"], cc, sn, sp, carry=carry)
    parts.update(zip(names, got))
    d_w_ukv = _matmul_tn(ckvn, dkv, "mla_dwukv")
    names, carry = scatter(w_ukv=_Cols(d_w_ukv))
    d_w_in, got = _matmul_tn(dproj, n0, "mla_dwin", carry=carry)
    parts.update(zip(names, got))
    (dx, _, small["mla_norm"]), _ = _nt_rmsbwd(dproj, w_in, x, gains["mla_norm"], dh1, "mla_in_bwd")
    return loss, dx, parts, _Cols(d_w_in), small


def kernel(x, positions, mla_norm, mla_w_in, mla_g_cq, mla_g_ckv, mla_w_uq, mla_w_ukv, mla_w_o, conv_norm, conv_w_in, conv_w, conv_w_out, ffn_norm, ffn_w_gate, ffn_w_up, ffn_w_down, final_norm, loss_target, m_mla_norm, m_mla_w_in, m_mla_g_cq, m_mla_g_ckv, m_mla_w_uq, m_mla_w_ukv, m_mla_w_o, m_conv_norm, m_conv_w_in, m_conv_w, m_conv_w_out, m_ffn_norm, m_ffn_w_gate, m_ffn_w_up, m_ffn_w_down, m_final_norm, v_mla_norm, v_mla_w_in, v_mla_g_cq, v_mla_g_ckv, v_mla_w_uq, v_mla_w_ukv, v_mla_w_o, v_conv_norm, v_conv_w_in, v_conv_w, v_conv_w_out, v_ffn_norm, v_ffn_w_gate, v_ffn_w_up, v_ffn_w_down, v_final_norm):
    me = 4 * lax.axis_index("x") + 2 * lax.axis_index("y") + lax.axis_index("c")

    bf = lambda a, rows, cols: _pad_to(a.astype(BF16), rows, cols)
    shards = dict(
        w_in=bf(mla_w_in[0], D_MODEL // N_DEV, PROJ_PAD), w_uq=bf(mla_w_uq[0], CQ, HEAD_PAD),
        w_ukv=mla_w_ukv[0].astype(BF16), w_o=mla_w_o[0].astype(BF16),
        c_in=conv_w_in[0].astype(BF16), c_out=conv_w_out[0].astype(BF16),
        conv_norm=conv_norm, conv_w=conv_w[0])
    for l in range(2):
        shards.update({f"wg{l}": bf(ffn_w_gate[l], D_MODEL, FF_SHARD_PAD), f"wu{l}": bf(ffn_w_up[l], D_MODEL, FF_SHARD_PAD),
                       f"wd{l}": bf(ffn_w_down[l], FF_SHARD_PAD, D_MODEL)})
    gains = dict(mla_norm=mla_norm, g_cq=mla_g_cq, g_ckv=mla_g_ckv, ffn_norm=ffn_norm,
                 final_norm=final_norm.reshape(1, -1))
    loss_local, dx, parts, d_w_in, grads = _forward_backward(
        x[0], loss_target[0], _rope_tables(positions[0]), gains, shards)

    col0 = me * (D_MODEL // N_DEV)

    def place(shard):
        return lax.dynamic_update_slice(jnp.zeros((shard.shape[0], D_MODEL), F32), shard, (0, col0))

    no_loss = jnp.zeros((8, D_MODEL), F32)

    def pack(mla_n, g_cq, g_ckv, ffn_n, fin_n, conv_n, conv_taps, loss_tile):
        rows = [mla_n, _pad_row(g_cq), _pad_row(g_ckv), ffn_n, fin_n.reshape(1, -1), conv_n, conv_taps, loss_tile]
        assert all(r.shape[0] in (n, 8) for r, n in zip(rows, SMALL_PIECES))
        return jnp.concatenate([jnp.pad(r, ((0, 8 - r.shape[0]), (0, 0))) for r in rows], axis=0)

    gpack = pack(grads["mla_norm"], grads["g_cq"], grads["g_ckv"], grads["ffn_norm"], grads["final_norm"],
                 grads["conv_norm"], grads["conv_w"], loss_local)
    wpack = pack(mla_norm, mla_g_cq, mla_g_ckv, ffn_norm, final_norm, place(conv_norm), place(conv_w[0]), no_loss)
    mpack = pack(m_mla_norm, m_mla_g_cq, m_mla_g_ckv, m_ffn_norm, m_final_norm, place(m_conv_norm),
                 place(m_conv_w[0]), no_loss)
    vpack = pack(v_mla_norm, v_mla_g_cq, v_mla_g_ckv, v_ffn_norm, v_final_norm, place(v_conv_norm),
                 place(v_conv_w[0]), no_loss)
    small, (parts["w_in"],) = _small_allreduce_adamw(gpack, wpack, mpack, vpack, [d_w_in])
    loss = small[0][56, 0]

    def adamw(name, w, m, v, partials, transposed=False):
        if transposed:
            w, m, v = (jnp.swapaxes(a, 1, 2) for a in (w, m, v))
        outs = _adamw(partials, w, m, v, "adamw_" + name)
        return [jnp.swapaxes(o, 1, 2) for o in outs] if transposed else outs

    res = dict(
        w_in=adamw("w_in", mla_w_in, m_mla_w_in, v_mla_w_in, [parts["w_in"]], True),
        w_uq=adamw("w_uq", mla_w_uq, m_mla_w_uq, v_mla_w_uq, [parts["w_uq"]], True),
        w_ukv=adamw("w_ukv", mla_w_ukv, m_mla_w_ukv, v_mla_w_ukv, [parts["w_ukv"]]),
        w_o=adamw("w_o", mla_w_o, m_mla_w_o, v_mla_w_o, [parts["w_o"]]),
        c_in=adamw("c_in", conv_w_in, m_conv_w_in, v_conv_w_in, [parts["c_in"]]),
        c_out=adamw("c_out", conv_w_out, m_conv_w_out, v_conv_w_out, [parts["c_out"]]),
        wg=adamw("wg", ffn_w_gate, m_ffn_w_gate, v_ffn_w_gate, [parts["wg0"], parts["wg1"]], True),
        wu=adamw("wu", ffn_w_up, m_ffn_w_up, v_ffn_w_up, [parts["wu0"], parts["wu1"]], True),
        wd=adamw("wd", ffn_w_down, m_ffn_w_down, v_ffn_w_down, [parts["wd0"], parts["wd1"]]),
    )

    def unpack(p):
        own = lambda rows: lax.dynamic_slice(rows, (0, col0), (rows.shape[0], D_MODEL // N_DEV))
        return dict(mla_norm=p[0:1], g_cq=p[8:9, :CQ], g_ckv=p[16:17, :CKV], ffn_norm=p[24:26], final_norm=p[32],
                    conv_norm=own(p[40:41]), conv_w=own(p[48:51])[None])

    small = [unpack(p) for p in small]
    order = ["mla_norm", "w_in", "g_cq", "g_ckv", "w_uq", "w_ukv", "w_o", "conv_norm", "c_in", "conv_w", "c_out",
             "ffn_norm", "wg", "wu", "wd", "final_norm"]
    out = [loss, dx[None]]
    for kind in range(4):
        for n in order:
            out.append(res[n][kind] if n in res else small[kind][n])
    return tuple(out)
```

```python
import math

import jax
import jax.numpy as jnp
from jax import lax
from jax.experimental import pallas as pl
from jax.experimental.pallas import tpu as pltpu

F32 = jnp.float32
BF16 = jnp.bfloat16

N_DEV = 8
D_MODEL = 1024
N_HEADS = 8
NOPE = 128
ROPE = 64
V_DIM = 128
HEAD_PAD = 256
CQ = 512
CKV = 256
PROJ = CQ + CKV + ROPE
PROJ_PAD = CQ + CKV + 128
D_FF = 2816
FF_SHARD = D_FF // N_DEV
FF_SHARD_PAD = 384
FF_PAD = FF_SHARD_PAD * N_DEV
CHUNK_SHIFT = 6
RMS_EPS = 1e-6
ROPE_THETA = 10000.0
ATT_SCALE = 1.0 / math.sqrt(NOPE + ROPE)
LOG2_E = math.log2(math.e)
LN_2 = math.log(2.0)
Q_SCALE = ATT_SCALE * LOG2_E
NEG = -1e30

ADAM_LR = 0.001
ADAM_B1 = 0.9
ADAM_B2 = 0.999
ADAM_EPS = 1e-08
ADAM_WD = 0.01
ADAM_STEP = 10

SMALL_PIECES = (1, 1, 1, 2, 1, 1, 3, 1)
SMALL_ROWS = 16

_NT = (((1,), (1,)), ((), ()))
_TN = (((0,), (0,)), ((), ()))


def _pc(body, *, name, out_shape, grid=(), in_specs=None, out_specs=None, scratch_shapes=(), vmem_mb=None):
    params = {}
    if vmem_mb is not None:
        params["vmem_limit_bytes"] = vmem_mb << 20
    kwargs = dict(
        name=name, out_shape=out_shape, grid=grid, scratch_shapes=scratch_shapes,
        compiler_params=pltpu.CompilerParams(**params),
    )
    if in_specs is not None:
        kwargs["in_specs"] = in_specs
    if out_specs is not None:
        kwargs["out_specs"] = out_specs
    return pl.pallas_call(body, **kwargs)


def _pc_carrying(body, carry, operands, *, name, out_shape, grid, in_specs, out_specs, scratch_shapes=()):
    if carry is None:
        return _pc(body, name=name, out_shape=out_shape, grid=grid, in_specs=in_specs, out_specs=out_specs,
                   scratch_shapes=scratch_shapes)(*operands), None
    arrays, scatter, relay_at = carry
    nw, n_in, n_out, n_scr = len(arrays), len(in_specs), len(out_shape), len(scratch_shapes)
    start, relay, wait, landed_shapes, sems, arrays = _exchange_ops(arrays, scatter)
    n_steps = math.prod(grid)
    relay_step = None if relay is None else min(int(relay_at * n_steps), n_steps - 1)

    def wrapped(*refs):
        ins, rest = refs[:n_in], refs[n_in:]
        cin, rest = rest[:nw], rest[nw:]
        outs, rest = rest[:n_out], rest[n_out:]
        cout, rest = rest[:nw], rest[nw:]
        scr, csem = rest[:n_scr], rest[n_scr:]
        step = pl.program_id(0)
        for a in range(1, len(grid)):
            step = step * grid[a] + pl.program_id(a)

        @pl.when(step == 0)
        def _():
            start(cin, cout, *csem)

        if relay is not None:
            @pl.when(step == relay_step)
            def _():
                relay(cin, cout, *csem)

        body(*ins, *outs, *scr)

        @pl.when(step == n_steps - 1)
        def _():
            wait(cin, cout, *csem)

    any_spec = pl.BlockSpec(memory_space=pl.ANY)
    res = _pc(
        wrapped, name=name, grid=grid,
        in_specs=list(in_specs) + [any_spec] * nw, out_specs=list(out_specs) + [any_spec] * nw,
        out_shape=list(out_shape) + landed_shapes, scratch_shapes=list(scratch_shapes) + sems,
    )(*operands, *arrays)
    return res[:n_out], res[n_out:]


def _sds(shape, dtype):
    return jax.ShapeDtypeStruct(shape, dtype)


def _dot(a, b):
    return jnp.dot(a, b, preferred_element_type=F32)


def _dot_nt(a, b):
    return lax.dot_general(a, b, _NT, preferred_element_type=F32)


def _dot_tn(a, b):
    return lax.dot_general(a, b, _TN, preferred_element_type=F32)


def _rstd(x):
    return lax.rsqrt(jnp.mean(x * x, axis=-1, keepdims=True) + RMS_EPS)


def _rms(x, g):
    return (x * _rstd(x)) * g


def _rms_bwd(dy, x, g):
    r = _rstd(x)
    xhat = x * r
    dxhat = dy * g
    dx = r * (dxhat - xhat * jnp.mean(dxhat * xhat, axis=-1, keepdims=True))
    return dx, jnp.sum(dy * xhat, axis=0, keepdims=True)


def _rope(t, cc, sn, sp):
    return t * cc + pltpu.roll(t, 96, 1) * sn + pltpu.roll(t, 32, 1) * sp


def _rope_bwd(dt, cc, sn, sp):
    return dt * cc + pltpu.roll(dt * sn, 32, 1) + pltpu.roll(dt * sp, 96, 1)


def _row_block(s):
    return min(512, s)


_TN_ROWS = 2048
_FFN_ROWS = 1024


def _mla_in_fwd(x, g0, w_in, g_cq, g_ckv, cc, sn, sp, carry=None):
    s = x.shape[0]
    tm = _row_block(s)

    def body(x_ref, g0_ref, w_ref, gcq_ref, gckv_ref, cc_ref, sn_ref, sp_ref,
             n_ref, proj_ref, cqn_ref, ckvn_ref, krr_ref):
        nb = _rms(x_ref[...], g0_ref[...]).astype(BF16)
        n_ref[...] = nb
        proj = _dot(nb, w_ref[...])
        proj_ref[...] = proj
        cqn_ref[...] = _rms(proj[:, :CQ], gcq_ref[...]).astype(BF16)
        ckvn_ref[...] = _rms(proj[:, CQ:CQ + CKV], gckv_ref[...]).astype(BF16)
        krr_ref[...] = _rope(proj[:, CQ + CKV:], cc_ref[...], sn_ref[...], sp_ref[...]).astype(BF16)

    row = lambda n: pl.BlockSpec((tm, n), lambda i: (i, 0))
    full = lambda a: pl.BlockSpec(a.shape, lambda i: (0, 0))
    return _pc_carrying(
        body, carry, (x, g0, w_in, g_cq, g_ckv, cc, sn, sp), name="mla_in_fwd", grid=(s // tm,),
        in_specs=[row(D_MODEL), full(g0), full(w_in), full(g_cq), full(g_ckv), row(128), row(128), row(128)],
        out_specs=[row(D_MODEL), row(PROJ_PAD), row(CQ), row(CKV), row(128)],
        out_shape=[_sds((s, D_MODEL), BF16), _sds((s, PROJ_PAD), F32), _sds((s, CQ), BF16),
                   _sds((s, CKV), BF16), _sds((s, 128), BF16)],
    )


def _qkv_proj(cqn, ckvn, krr, w_uq, w_ukv, cc, sn, sp, carry=None):
    s = cqn.shape[0]
    tm = min(_FFN_ROWS, s)

    def body(cqn_ref, ckvn_ref, krr_ref, wuq_ref, wukv_ref, cc_ref, sn_ref, sp_ref, q_ref, k_ref, v_ref):
        cqn_b, ckvn_b, krr_b = cqn_ref[...], ckvn_ref[...], krr_ref[...]
        cc_b, sn_b, sp_b = cc_ref[...], sn_ref[...], sp_ref[...]
        for h in range(N_HEADS):
            q = _dot(cqn_b, wuq_ref[h]) * Q_SCALE
            q_ref[h, :, :NOPE] = q[:, :NOPE].astype(BF16)
            q_ref[h, :, NOPE:] = _rope(q[:, NOPE:], cc_b, sn_b, sp_b).astype(BF16)
            kv = _dot(ckvn_b, wukv_ref[h])
            k_ref[h, :, :NOPE] = kv[:, :NOPE].astype(BF16)
            k_ref[h, :, NOPE:] = krr_b
            v_ref[h] = kv[:, NOPE:].astype(BF16)

    row = lambda n: pl.BlockSpec((tm, n), lambda i: (i, 0))
    head_w = lambda a: pl.BlockSpec(a.shape, lambda i: (0, 0, 0))
    head_o = lambda n: pl.BlockSpec((N_HEADS, tm, n), lambda i: (0, i, 0))
    return _pc_carrying(
        body, carry, (cqn, ckvn, krr, w_uq, w_ukv, cc, sn, sp), name="qkv_proj", grid=(s // tm,),
        in_specs=[row(CQ), row(CKV), row(128), head_w(w_uq), head_w(w_ukv), row(128), row(128), row(128)],
        out_specs=[head_o(HEAD_PAD), head_o(HEAD_PAD), head_o(V_DIM)],
        out_shape=[_sds((N_HEADS, s, HEAD_PAD), BF16), _sds((N_HEADS, s, HEAD_PAD), BF16),
                   _sds((N_HEADS, s, V_DIM), BF16)],
    )


def _chunk_bias(bq, bk, first_key=0):
    rows = lax.broadcasted_iota(jnp.int32, (bq, bk), 0)
    cols = lax.broadcasted_iota(jnp.int32, (bq, bk), 1) + first_key
    visible = jnp.right_shift(cols, CHUNK_SHIFT) <= jnp.right_shift(rows, CHUNK_SHIFT)
    return jnp.where(visible, 0.0, NEG).astype(F32)


_ATTN_FWD_ROWS = 1024


def _attn_fwd(q, k, v, carry=None):
    s = q.shape[1]
    bk = _row_block(s)
    bq = min(_ATTN_FWD_ROWS, s)
    nd = bq // bk

    def body(q_ref, k_ref, v_ref, o_ref, lse_ref, bias_ref):
        i = pl.program_id(1)
        qb = q_ref[0]

        @pl.when((pl.program_id(0) == 0) & (i == 0))
        def _():
            bias_ref[...] = _chunk_bias(bq, bk)

        def block(j, carry, queries, bias=None):
            m, l, acc = carry
            start = pl.multiple_of(j * bk, bk)
            kb = k_ref[0, pl.ds(start, bk), :]
            vb = v_ref[0, pl.ds(start, bk), :]
            sc = _dot_nt(queries, kb)
            if bias is not None:
                sc = sc + bias
            m_new = jnp.maximum(m, jnp.max(sc, axis=-1, keepdims=True))
            p = jnp.exp2(sc - m_new)
            alpha = jnp.exp2(m - m_new)
            l = alpha * l + jnp.sum(p, axis=-1, keepdims=True)
            acc = alpha * acc + _dot(p.astype(BF16), vb)
            return m_new, l, acc

        carry = (jnp.full((bq, 1), NEG, F32), jnp.zeros((bq, 1), F32), jnp.zeros((bq, V_DIM), F32))
        carry = lax.fori_loop(0, i * nd, lambda j, c: block(j, c, qb), carry)
        for d in range(nd):
            first = d * bk
            carry = block(i * nd + d, carry, qb[first:], bias_ref[:bq - first, :])
            m, l, acc = (c[:bk] for c in carry)
            o_ref[first:first + bk, :] = (acc / l).astype(BF16)
            lse_ref[0, first:first + bk, :] = jnp.broadcast_to(m + jnp.log(l) * LOG2_E, (bk, 128))
            if d < nd - 1:
                carry = tuple(c[bk:] for c in carry)

    return _pc_carrying(
        body, carry, (q, k, v), name="attn_fwd", grid=(N_HEADS, s // bq),
        in_specs=[pl.BlockSpec((1, bq, HEAD_PAD), lambda h, i: (h, i, 0)),
                  pl.BlockSpec((1, s, HEAD_PAD), lambda h, i: (h, 0, 0)),
                  pl.BlockSpec((1, s, V_DIM), lambda h, i: (h, 0, 0))],
        out_specs=[pl.BlockSpec((bq, V_DIM), lambda h, i: (i, h)),
                   pl.BlockSpec((1, bq, 128), lambda h, i: (h, i, 0))],
        out_shape=[_sds((s, N_HEADS * V_DIM), BF16), _sds((N_HEADS, s, 128), F32)],
        scratch_shapes=[pltpu.VMEM((bq, bk), F32)],
    )


def _matmul_res(a, w, res, name):
    s, kd = a.shape
    n = w.shape[1]
    tm = min(_FFN_ROWS, s)

    def body(a_ref, w_ref, r_ref, o_ref):
        o_ref[...] = r_ref[...] + _dot(a_ref[...], w_ref[...])

    return _pc(
        body, name=name, grid=(s // tm,),
        in_specs=[pl.BlockSpec((tm, kd), lambda i: (i, 0)), pl.BlockSpec((kd, n), lambda i: (0, 0)),
                  pl.BlockSpec((tm, n), lambda i: (i, 0))],
        out_specs=pl.BlockSpec((tm, n), lambda i: (i, 0)),
        out_shape=_sds((s, n), F32),
    )(a, w, res)


def _ffn_fwd(h, gain, wg, wu, wd, name, carry=None):
    s = h.shape[0]
    tm = min(_FFN_ROWS, s)
    tf = 512
    nf = FF_PAD // tf

    def body(h_ref, g_ref, wg_ref, wu_ref, wd_ref, o_ref, n_ref, gate_ref, up_ref, acc_ref):
        j = pl.program_id(1)

        @pl.when(j == 0)
        def _():
            n_ref[...] = _rms(h_ref[...], g_ref[...]).astype(BF16)
            acc_ref[...] = jnp.zeros_like(acc_ref)

        nb = n_ref[...]
        gate = _dot(nb, wg_ref[...])
        up = _dot(nb, wu_ref[...])
        gate_ref[...] = gate.astype(BF16)
        up_ref[...] = up.astype(BF16)
        act = gate * jax.nn.sigmoid(gate) * up
        acc_ref[...] += _dot(act.astype(BF16), wd_ref[...])

        @pl.when(j == nf - 1)
        def _():
            o_ref[...] = h_ref[...] + acc_ref[...]

    return _pc_carrying(
        body, carry, (h, gain, wg, wu, wd), name=name, grid=(s // tm, nf),
        in_specs=[pl.BlockSpec((tm, D_MODEL), lambda i, j: (i, 0)), pl.BlockSpec((1, D_MODEL), lambda i, j: (0, 0)),
                  pl.BlockSpec((D_MODEL, tf), lambda i, j: (0, j)), pl.BlockSpec((D_MODEL, tf), lambda i, j: (0, j)),
                  pl.BlockSpec((tf, D_MODEL), lambda i, j: (j, 0))],
        out_specs=[pl.BlockSpec((tm, D_MODEL), lambda i, j: (i, 0)), pl.BlockSpec((tm, D_MODEL), lambda i, j: (i, 0)),
                   pl.BlockSpec((tm, tf), lambda i, j: (i, j)), pl.BlockSpec((tm, tf), lambda i, j: (i, j))],
        out_shape=[_sds((s, D_MODEL), F32), _sds((s, D_MODEL), BF16), _sds((s, FF_PAD), BF16),
                   _sds((s, FF_PAD), BF16)],
        scratch_shapes=[pltpu.VMEM((tm, D_MODEL), F32)],
    )


def _rms_matmul(h, gain, w, name):
    s = h.shape[0]
    n = w.shape[1]
    tm = min(_FFN_ROWS, s)
    tn = 1024
    nn = n // tn

    def body(h_ref, g_ref, w_ref, n_ref, o_ref):
        @pl.when(pl.program_id(1) == 0)
        def _():
            n_ref[...] = _rms(h_ref[...], g_ref[...]).astype(BF16)

        o_ref[...] = _dot(n_ref[...], w_ref[...]).astype(BF16)

    return _pc(
        body, name=name, grid=(s // tm, nn),
        in_specs=[pl.BlockSpec((tm, D_MODEL), lambda i, j: (i, 0)), pl.BlockSpec((1, D_MODEL), lambda i, j: (0, 0)),
                  pl.BlockSpec((D_MODEL, tn), lambda i, j: (0, j))],
        out_specs=[pl.BlockSpec((tm, D_MODEL), lambda i, j: (i, 0)), pl.BlockSpec((tm, tn), lambda i, j: (i, j))],
        out_shape=[_sds((s, D_MODEL), BF16), _sds((s, n), BF16)],
    )(h, gain, w)


def _shift_down(u, k, rows):
    return jnp.where(rows >= k, pltpu.roll(u, k, 0), 0.0)


def _shift_up(u, k, rows, s):
    return jnp.where(rows < s - k, pltpu.roll(u, s - k, 0), 0.0)


_CONV_COLS = 128


def _conv_fwd(bcx, cw):
    s = bcx.shape[0]
    tc = _CONV_COLS
    nc = D_MODEL // tc

    def body(b_ref, c_ref, xp_ref, w_ref, o_ref):
        rows = lax.broadcasted_iota(jnp.int32, (s, tc), 0)
        u = c_ref[...].astype(F32) * xp_ref[...].astype(F32)
        w = w_ref[...]
        uc = w[2:3] * u + w[1:2] * _shift_down(u, 1, rows) + w[0:1] * _shift_down(u, 2, rows)
        o_ref[...] = (b_ref[...].astype(F32) * uc).astype(BF16)

    col = lambda off: pl.BlockSpec((s, tc), lambda j: (0, off + j))
    return _pc(
        body, name="conv_fwd", grid=(nc,),
        in_specs=[col(0), col(nc), col(2 * nc), pl.BlockSpec((3, tc), lambda j: (0, j))],
        out_specs=pl.BlockSpec((s, tc), lambda j: (0, j)),
        out_shape=_sds((s, D_MODEL), BF16),
    )(bcx, bcx, bcx, cw)


def _final_loss(h, gain, target):
    s = h.shape[0]
    tm = _row_block(s)

    def body(h_ref, g_ref, t_ref, dh_ref, dg_ref, loss_ref):
        i = pl.program_id(0)

        @pl.when(i == 0)
        def _():
            dg_ref[...] = jnp.zeros_like(dg_ref)
            loss_ref[...] = jnp.zeros_like(loss_ref)

        hb = h_ref[...]
        e = _rms(hb, g_ref[...]) - t_ref[...]
        loss_ref[...] += 0.5 * jnp.sum(jnp.mean(e * e, axis=-1, keepdims=True))
        dx, dg = _rms_bwd(e * (1.0 / D_MODEL), hb, g_ref[...])
        dh_ref[...] = dx
        dg_ref[...] += dg

    row = pl.BlockSpec((tm, D_MODEL), lambda i: (i, 0))
    vec = pl.BlockSpec((1, D_MODEL), lambda i: (0, 0))
    return _pc(
        body, name="final_loss", grid=(s // tm,),
        in_specs=[row, vec, row],
        out_specs=[row, vec, pl.BlockSpec((8, D_MODEL), lambda i: (0, 0))],
        out_shape=[_sds((s, D_MODEL), F32), _sds((1, D_MODEL), F32), _sds((8, D_MODEL), F32)],
    )(h, gain, target)


def _ffn_bwd_x(dho, h, gain, gate, up, wg, wu, wd, name, carry=None):
    s = h.shape[0]
    tm = _row_block(s)
    tf = 1024
    nf = FF_PAD // tf

    def body(dho_ref, h_ref, g_ref, gate_ref, up_ref, wg_ref, wu_ref, wd_ref,
             dh_ref, dhb_ref, dhob_ref, dgate_ref, dup_ref, act_ref, dgain_ref, acc_ref):
        i = pl.program_id(0)
        j = pl.program_id(1)

        @pl.when(j == 0)
        def _():
            dhob_ref[...] = dho_ref[...].astype(BF16)
            acc_ref[...] = jnp.zeros_like(acc_ref)

        @pl.when((i == 0) & (j == 0))
        def _():
            dgain_ref[...] = jnp.zeros_like(dgain_ref)

        dact = _dot_nt(dhob_ref[...], wd_ref[...])
        g = gate_ref[...].astype(F32)
        u = up_ref[...].astype(F32)
        sg = jax.nn.sigmoid(g)
        silu = g * sg
        dg = (dact * u * (sg * (1.0 + g * (1.0 - sg)))).astype(BF16)
        du = (dact * silu).astype(BF16)
        dgate_ref[...] = dg
        dup_ref[...] = du
        act_ref[...] = (silu * u).astype(BF16)
        acc_ref[...] += _dot_nt(dg, wg_ref[...]) + _dot_nt(du, wu_ref[...])

        @pl.when(j == nf - 1)
        def _():
            dx, dgain = _rms_bwd(acc_ref[...], h_ref[...], g_ref[...])
            dh = dho_ref[...] + dx
            dh_ref[...] = dh
            dhb_ref[...] = dh.astype(BF16)
            dgain_ref[...] += dgain

    row = pl.BlockSpec((tm, D_MODEL), lambda i, j: (i, 0))
    vec = pl.BlockSpec((1, D_MODEL), lambda i, j: (0, 0))
    hid = pl.BlockSpec((tm, tf), lambda i, j: (i, j))
    wcol = pl.BlockSpec((D_MODEL, tf), lambda i, j: (0, j))
    wrow = pl.BlockSpec((tf, D_MODEL), lambda i, j: (j, 0))
    return _pc_carrying(
        body, carry, (dho, h, gain, gate, up, wg, wu, wd), name=name, grid=(s // tm, nf),
        in_specs=[row, row, vec, hid, hid, wcol, wcol, wrow],
        out_specs=[row, row, row, hid, hid, hid, vec],
        out_shape=[_sds((s, D_MODEL), F32), _sds((s, D_MODEL), BF16), _sds((s, D_MODEL), BF16),
                   _sds((s, FF_PAD), BF16), _sds((s, FF_PAD), BF16), _sds((s, FF_PAD), BF16),
                   _sds((1, D_MODEL), F32)],
        scratch_shapes=[pltpu.VMEM((tm, D_MODEL), F32)],
    )


def _nt_rmsbwd(a, w, h, gain, dho, name, carry=None):
    stacked = a.ndim == 3
    if stacked:
        nk, s, tk = a.shape
    else:
        s, tk = a.shape
        nk = 1
    tm = min(_FFN_ROWS, s) if stacked else _row_block(s)

    def body(a_ref, w_ref, h_ref, g_ref, dho_ref, dh_ref, dhb_ref, dgain_ref, acc_ref):
        i = pl.program_id(0)
        j = pl.program_id(1)
        if stacked:
            a_ref = a_ref.at[0]

        @pl.when(j == 0)
        def _():
            acc_ref[...] = jnp.zeros_like(acc_ref)

        @pl.when((i == 0) & (j == 0))
        def _():
            dgain_ref[...] = jnp.zeros_like(dgain_ref)

        acc_ref[...] += _dot_nt(a_ref[...], w_ref[...])

        @pl.when(j == nk - 1)
        def _():
            dx, dgain = _rms_bwd(acc_ref[...], h_ref[...], g_ref[...])
            dh = dho_ref[...] + dx
            dh_ref[...] = dh
            dhb_ref[...] = dh.astype(BF16)
            dgain_ref[...] += dgain

    row = pl.BlockSpec((tm, D_MODEL), lambda i, j: (i, 0))
    vec = pl.BlockSpec((1, D_MODEL), lambda i, j: (0, 0))
    return _pc_carrying(
        body, carry, (a, w, h, gain, dho), name=name, grid=(s // tm, nk),
        in_specs=[pl.BlockSpec((1, tm, tk), lambda i, j: (j, i, 0)) if stacked else pl.BlockSpec((tm, tk), lambda i, j: (i, 0)),
                  pl.BlockSpec((D_MODEL, tk), lambda i, j: (0, j)), row, vec, row],
        out_specs=[row, row, vec],
        out_shape=[_sds((s, D_MODEL), F32), _sds((s, D_MODEL), BF16), _sds((1, D_MODEL), F32)],
        scratch_shapes=[pltpu.VMEM((tm, D_MODEL), F32)],
    )


def _matmul_nt(a, w, name):
    s, kd = a.shape
    n = w.shape[0]
    tm = min(_FFN_ROWS, s)

    def body(a_ref, w_ref, o_ref):
        o_ref[...] = _dot_nt(a_ref[...], w_ref[...]).astype(BF16)

    return _pc(
        body, name=name, grid=(s // tm,),
        in_specs=[pl.BlockSpec((tm, kd), lambda i: (i, 0)), pl.BlockSpec((n, kd), lambda i: (0, 0))],
        out_specs=pl.BlockSpec((tm, n), lambda i: (i, 0)),
        out_shape=_sds((s, n), BF16),
    )(a, w)


def _matmul_tn(a, b, name, carry=None):
    s, m = a.shape
    stacked = b.ndim == 3
    tmm = min(m, 1024)
    if stacked:
        n, tn = b.shape[0] * b.shape[2], b.shape[2]
    else:
        n = b.shape[1]
        tn = n if n <= 1024 else 1024
    tk = min(_TN_ROWS, s)
    nk = s // tk

    def body(a_ref, b_ref, o_ref, acc_ref):
        k = pl.program_id(2)
        if stacked:
            b_ref = b_ref.at[0]

        @pl.when(k == 0)
        def _():
            acc_ref[...] = jnp.zeros_like(acc_ref)

        acc_ref[...] += _dot_tn(a_ref[...], b_ref[...])

        @pl.when(k == nk - 1)
        def _():
            o_ref[...] = acc_ref[...].astype(BF16)

    (out,), got = _pc_carrying(
        body, carry, (a, b), name=name, grid=(m // tmm, n // tn, nk),
        in_specs=[pl.BlockSpec((tk, tmm), lambda i, j, k: (k, i)),
                  pl.BlockSpec((1, tk, tn), lambda i, j, k: (j, k, 0)) if stacked
                  else pl.BlockSpec((tk, tn), lambda i, j, k: (k, j))],
        out_specs=[pl.BlockSpec((tmm, tn), lambda i, j, k: (i, j))],
        out_shape=[_sds((m, n), BF16)],
        scratch_shapes=[pltpu.VMEM((tmm, tn), F32)],
    )
    return out if carry is None else (out, got)


def _conv_bwd(dbuc, bcx, cw):
    s = bcx.shape[0]
    tc = _CONV_COLS
    nc = D_MODEL // tc

    def body(d_ref, b_ref, c_ref, xp_ref, w_ref, dbcx_ref, dw_ref):
        rows = lax.broadcasted_iota(jnp.int32, (s, tc), 0)
        c = c_ref[...].astype(F32)
        xp = xp_ref[...].astype(F32)
        u = c * xp
        u1 = _shift_down(u, 1, rows)
        u2 = _shift_down(u, 2, rows)
        w = w_ref[...]
        uc = w[2:3] * u + w[1:2] * u1 + w[0:1] * u2
        d = d_ref[...].astype(F32)
        dbcx_ref[0] = (d * uc).astype(BF16)
        duc = d * b_ref[...].astype(F32)
        du = w[2:3] * duc + w[1:2] * _shift_up(duc, 1, rows, s) + w[0:1] * _shift_up(duc, 2, rows, s)
        dbcx_ref[1] = (du * xp).astype(BF16)
        dbcx_ref[2] = (du * c).astype(BF16)
        dw_ref[0:1, :] = jnp.sum(duc * u2, axis=0, keepdims=True)
        dw_ref[1:2, :] = jnp.sum(duc * u1, axis=0, keepdims=True)
        dw_ref[2:3, :] = jnp.sum(duc * u, axis=0, keepdims=True)

    col = lambda off: pl.BlockSpec((s, tc), lambda j: (0, off + j))
    tap = pl.BlockSpec((3, tc), lambda j: (0, j))
    return _pc(
        body, name="conv_bwd", grid=(nc,),
        in_specs=[col(0), col(0), col(nc), col(2 * nc), tap],
        out_specs=[pl.BlockSpec((3, s, tc), lambda j: (0, 0, j)), tap],
        out_shape=[_sds((3, s, D_MODEL), BF16), _sds((3, D_MODEL), F32)],
    )(dbuc, bcx, bcx, bcx, cw)


def _attn_bwd(q, k, v, do, o, lse, cc, sn, sp, carry=None):
    s = q.shape[1]
    bk = _row_block(s)
    nb = s // bk

    def body(q_ref, k_ref, v_ref, do_ref, o_ref, lse_ref, cc_ref, sn_ref, sp_ref,
             dq_ref, dk_ref, dv_ref, dqacc_ref, delta_ref, bias_ref):
        j = pl.program_id(1)
        kb = k_ref[0]
        vb = v_ref[0]

        @pl.when((pl.program_id(0) == 0) & (j == 0))
        def _():
            bias_ref[...] = _chunk_bias(bk, bk)

        @pl.when(j == 0)
        def _():
            dqacc_ref[...] = jnp.zeros_like(dqacc_ref)

            def fill(i, _):
                rows = pl.ds(pl.multiple_of(i * bk, bk), bk)
                d = jnp.sum(do_ref[rows, :].astype(F32) * o_ref[rows, :].astype(F32), axis=-1, keepdims=True)
                delta_ref[rows, :] = jnp.broadcast_to(d, (bk, 128))
                return 0

            lax.fori_loop(0, nb, fill, 0)

        def step(i, carry, masked):
            dk, dv = carry
            rows = pl.ds(pl.multiple_of(i * bk, bk), bk)
            qb = q_ref[0, rows, :]
            dob = do_ref[rows, :]
            sc = _dot_nt(qb, kb)
            if masked:
                sc = sc + bias_ref[...]
            p = jnp.exp2(sc - lse_ref[0, rows, :][:, :1])
            dv = dv + _dot_tn(p.astype(BF16), dob)
            ds = (p * (_dot_nt(dob, vb) - delta_ref[rows, :][:, :1])).astype(BF16)
            dk = dk + _dot_tn(ds, qb)
            dqacc_ref[rows, :] += _dot(ds, kb)
            return dk, dv

        carry = step(j, (jnp.zeros((bk, HEAD_PAD), F32), jnp.zeros((bk, V_DIM), F32)), True)
        dq = dqacc_ref[pl.ds(pl.multiple_of(j * bk, bk), bk), :] * ATT_SCALE
        dq_ref[:, :NOPE] = dq[:, :NOPE].astype(BF16)
        dq_ref[:, NOPE:] = _rope_bwd(dq[:, NOPE:], cc_ref[...], sn_ref[...], sp_ref[...]).astype(BF16)
        dk, dv = lax.fori_loop(j + 1, nb, lambda i, c: step(i, c, False), carry)
        dk_ref[0] = (dk * LN_2).astype(BF16)
        dv_ref[0] = dv.astype(BF16)

    blk = lambda n: pl.BlockSpec((1, bk, n), lambda h, j: (h, j, 0))
    whole = lambda n: pl.BlockSpec((1, s, n), lambda h, j: (h, 0, 0))
    cols = pl.BlockSpec((s, V_DIM), lambda h, j: (0, h))
    tab = pl.BlockSpec((bk, 128), lambda h, j: (j, 0))
    return _pc_carrying(
        body, carry, (q, k, v, do, o, lse, cc, sn, sp), name="attn_bwd", grid=(N_HEADS, nb),
        in_specs=[whole(HEAD_PAD), blk(HEAD_PAD), blk(V_DIM), cols, cols, whole(128), tab, tab, tab],
        out_specs=[pl.BlockSpec((bk, HEAD_PAD), lambda h, j: (j, h)), blk(HEAD_PAD), blk(V_DIM)],
        out_shape=[_sds((s, N_HEADS * HEAD_PAD), BF16), _sds((N_HEADS, s, HEAD_PAD), BF16),
                   _sds((N_HEADS, s, V_DIM), BF16)],
        scratch_shapes=[pltpu.VMEM((s, HEAD_PAD), F32), pltpu.VMEM((s, 128), F32), pltpu.VMEM((bk, bk), F32)],
    )


def _mla_mid_bwd(dq, dk, dv, proj, w_uq, w_ukv, g_cq, g_ckv, cc, sn, sp, carry=None):
    s = dq.shape[0]
    tm = _row_block(s)

    def body(dq_ref, dk_ref, dv_ref, proj_ref, wuq_ref, wukv_ref, gcq_ref, gckv_ref, cc_ref, sn_ref, sp_ref,
             dproj_ref, dkv_ref, dgcq_ref, dgckv_ref, acq_ref, ackv_ref, akr_ref):
        @pl.when(pl.program_id(0) == 0)
        def _():
            dgcq_ref[...] = jnp.zeros_like(dgcq_ref)
            dgckv_ref[...] = jnp.zeros_like(dgckv_ref)

        for h in range(N_HEADS):
            cols = slice(h * HEAD_PAD, (h + 1) * HEAD_PAD)
            dkb = dk_ref[h]
            dkv_ref[:, h * HEAD_PAD:h * HEAD_PAD + NOPE] = dkb[:, :NOPE]
            dkv_ref[:, h * HEAD_PAD + NOPE:(h + 1) * HEAD_PAD] = dv_ref[h]
            cq_part = _dot_nt(dq_ref[:, cols], wuq_ref[h])
            ckv_part = _dot_nt(dkv_ref[:, cols], wukv_ref[h])
            kr_part = dkb[:, NOPE:].astype(F32)
            if h == 0:
                acq_ref[...], ackv_ref[...], akr_ref[...] = cq_part, ckv_part, kr_part
            else:
                acq_ref[...] += cq_part
                ackv_ref[...] += ckv_part
                akr_ref[...] += kr_part

        proj = proj_ref[...]
        dcq, dgcq = _rms_bwd(acq_ref[...], proj[:, :CQ], gcq_ref[...])
        dckv, dgckv = _rms_bwd(ackv_ref[...], proj[:, CQ:CQ + CKV], gckv_ref[...])
        dproj_ref[:, :CQ] = dcq.astype(BF16)
        dproj_ref[:, CQ:CQ + CKV] = dckv.astype(BF16)
        dproj_ref[:, CQ + CKV:] = _rope_bwd(akr_ref[...], cc_ref[...], sn_ref[...], sp_ref[...]).astype(BF16)
        dgcq_ref[...] += dgcq
        dgckv_ref[...] += dgckv

    head_blk = lambda n: pl.BlockSpec((N_HEADS, tm, n), lambda i: (0, i, 0))
    head_cols = pl.BlockSpec((tm, N_HEADS * HEAD_PAD), lambda i: (i, 0))
    head_w = lambda a: pl.BlockSpec(a.shape, lambda i: (0, 0, 0))
    row = lambda n: pl.BlockSpec((tm, n), lambda i: (i, 0))
    vec = lambda n: pl.BlockSpec((1, n), lambda i: (0, 0))
    return _pc_carrying(
        body, carry, (dq, dk, dv, proj, w_uq, w_ukv, g_cq, g_ckv, cc, sn, sp),
        name="mla_mid_bwd", grid=(s // tm,),
        in_specs=[head_cols, head_blk(HEAD_PAD), head_blk(V_DIM), row(PROJ_PAD), head_w(w_uq), head_w(w_ukv),
                  vec(CQ), vec(CKV), row(128), row(128), row(128)],
        out_specs=[row(PROJ_PAD), head_cols, vec(CQ), vec(CKV)],
        out_shape=[_sds((s, PROJ_PAD), BF16), _sds((s, N_HEADS * HEAD_PAD), BF16), _sds((1, CQ), F32),
                   _sds((1, CKV), F32)],
        scratch_shapes=[pltpu.VMEM((tm, CQ), F32), pltpu.VMEM((tm, CKV), F32), pltpu.VMEM((tm, 128), F32)],
    )


def _peer(k):
    x, y, c = lax.axis_index("x"), lax.axis_index("y"), lax.axis_index("c")
    px = 1 - x if k & 4 else x
    py = 1 - y if k & 2 else y
    pc = 1 - c if k & 1 else c
    return (px, py, pc), 4 * px + 2 * py + pc


def _exchange_ops(arrays, scatter):
    nw = len(arrays)
    by_cols = [isinstance(a, _Cols) for a in arrays]
    arrays = [a.array if isinstance(a, _Cols) else a for a in arrays]
    direct = range(1, N_DEV) if scatter else (1, 2, 4, 6)

    def columns(ref, idx, width):
        return ref.at[:, pl.ds(pl.multiple_of(idx * width, 128), width)]

    def sent(ins, w, idx):
        if not scatter:
            return ins[w]
        return columns(ins[w], idx, arrays[w].shape[1] // N_DEV) if by_cols[w] else ins[w].at[idx]

    def slot(outs, w, idx):
        if by_cols[w] and not scatter:
            return columns(outs[w], idx, arrays[w].shape[1])
        return outs[w].at[idx]

    def copy(w, k, src, dst, to, send_sems, recv_sems):
        return pltpu.make_async_remote_copy(
            src_ref=src, dst_ref=dst, send_sem=send_sems.at[w * N_DEV + k], recv_sem=recv_sems.at[w * N_DEV + k],
            device_id=to, device_id_type=pl.DeviceIdType.MESH)

    def own_copies(ins, outs, local_sems):
        _, me = _peer(0)
        return [pltpu.make_async_copy(sent(ins, w, me), slot(outs, w, me), local_sems.at[w]) for w in range(nw)]

    def sends(ins, outs, send_sems, recv_sems):
        _, me = _peer(0)
        out = []
        for k in direct:
            dev, idx = _peer(k)
            for w in range(nw):
                out.append(copy(w, k, sent(ins, w, idx), slot(outs, w, me), dev, send_sems, recv_sems))
        return out

    def relays(outs, send_sems, recv_sems):
        sibling, _ = _peer(1)
        out = []
        for k in (2, 4, 6):
            _, idx = _peer(k)
            for w in range(nw):
                out.append(copy(w, k + 1, slot(outs, w, idx), slot(outs, w, idx), sibling, send_sems, recv_sems))
        return out

    def arrival(outs, w, k, send_sems, recv_sems):
        dev, idx = _peer(k)
        return copy(w, k, slot(outs, w, idx), slot(outs, w, idx), dev, send_sems, recv_sems)

    def start(ins, outs, send_sems, recv_sems, local_sems):
        for cp in own_copies(ins, outs, local_sems) + sends(ins, outs, send_sems, recv_sems):
            cp.start()

    def relay(ins, outs, send_sems, recv_sems, local_sems):
        for k in (2, 4, 6):
            for w in range(nw):
                arrival(outs, w, k, send_sems, recv_sems).wait_recv()
        for cp in relays(outs, send_sems, recv_sems):
            cp.start()

    def wait(ins, outs, send_sems, recv_sems, local_sems):
        for cp in own_copies(ins, outs, local_sems):
            cp.wait()
        for cp in sends(ins, outs, send_sems, recv_sems) + ([] if scatter else relays(outs, send_sems, recv_sems)):
            cp.wait_send()
        for k in (range(1, N_DEV) if scatter else (1, 3, 5, 7)):
            for w in range(nw):
                arrival(outs, w, k, send_sems, recv_sems).wait_recv()

    def landed_shape(a, cols):
        if not cols:
            return a.shape if scatter else (N_DEV,) + a.shape
        r, c = a.shape
        return (N_DEV, r, c // N_DEV) if scatter else (r, N_DEV * c)

    landed = [_sds(landed_shape(a, cols), a.dtype) for a, cols in zip(arrays, by_cols)]
    sems = [pltpu.SemaphoreType.DMA((nw * N_DEV,)), pltpu.SemaphoreType.DMA((nw * N_DEV,)),
            pltpu.SemaphoreType.DMA((nw,))]
    return start, (None if scatter else relay), wait, landed, sems, arrays


class _Cols:
    def __init__(self, array):
        self.array = array


def _exchange(arrays, scatter, name):
    nw = len(arrays)
    start, relay, wait, landed, sems, arrays = _exchange_ops(arrays, scatter)

    def body(*refs):
        ins, outs, csem = refs[:nw], refs[nw:2 * nw], refs[2 * nw:]
        start(ins, outs, *csem)
        if relay is not None:
            relay(ins, outs, *csem)
        wait(ins, outs, *csem)

    any_spec = pl.BlockSpec(memory_space=pl.ANY)
    return _pc(body, name=name, in_specs=[any_spec] * nw, out_specs=[any_spec] * nw, out_shape=landed,
               scratch_shapes=sems)(*arrays)


def _adam_math(g, w, m, v):
    m = ADAM_B1 * m + (1.0 - ADAM_B1) * g
    v = ADAM_B2 * v + (1.0 - ADAM_B2) * jnp.square(g)
    m_hat = m / (1.0 - ADAM_B1 ** ADAM_STEP)
    v_hat = v / (1.0 - ADAM_B2 ** ADAM_STEP)
    delta = -ADAM_LR * (m_hat / (jnp.sqrt(v_hat) + ADAM_EPS) + ADAM_WD * w)
    return delta, m, v


def _adam_rows(r):
    for t in range(min(r, 512) // 16 * 16, 0, -16):
        if r % t == 0:
            return t
    return r


def _adamw(parts, w, m, v, name):
    nl, r, c = w.shape
    tr = _adam_rows(r)
    nr = r // tr

    def body(*refs):
        p_refs, (w_ref, m_ref, v_ref, g_ref, d_ref, mo_ref, vo_ref) = refs[:nl], refs[nl:]
        for layer in range(nl):
            @pl.when(pl.program_id(0) == layer)
            def _(p_ref=p_refs[layer]):
                g = p_ref[0].astype(F32)
                for src in range(1, N_DEV):
                    g = g + p_ref[src].astype(F32)
                delta, m2, v2 = _adam_math(g, w_ref[0], m_ref[0], v_ref[0])
                g_ref[0] = g
                d_ref[0] = delta
                mo_ref[0] = m2
                vo_ref[0] = v2

    def part_spec(layer):
        def index(l, i):
            return 0, jnp.where(l == layer, i, jnp.where(l < layer, 0, nr - 1)), 0
        return pl.BlockSpec((N_DEV, tr, c), index)

    blk = pl.BlockSpec((1, tr, c), lambda l, i: (l, i, 0))
    return _pc(
        body, name=name, grid=(nl, nr),
        in_specs=[part_spec(layer) for layer in range(nl)] + [blk, blk, blk],
        out_specs=[blk, blk, blk, blk],
        out_shape=[_sds((nl, r, c), F32)] * 4,
    )(*parts, w, m, v)


def _small_allreduce_adamw(gpack, wpack, mpack, vpack, last_grads):
    shape = gpack.shape
    compact = (SMALL_ROWS, D_MODEL)
    nw = len(last_grads)
    start, _, wait, landed, exchange_sems, last_grads = _exchange_ops(last_grads, True)

    def body(*refs):
        (g_ref, w_ref, m_ref, v_ref), rest = refs[:4], refs[4:]
        cin, rest = rest[:nw], rest[nw:]
        (go_ref, d_ref, mo_ref, vo_ref), rest = rest[:4], rest[4:]
        cout, rest = rest[:nw], rest[nw:]
        (comp_ref, gath_ref, send_sems, recv_sems), csem = rest[:4], rest[4:]
        start(cin, cout, *csem)
        _, me = _peer(0)
        comp_ref[...] = jnp.zeros(compact, F32)
        r = 0
        for p, n in enumerate(SMALL_PIECES):
            comp_ref[r:r + n, :] = g_ref[8 * p:8 * p + n, :]
            r += n
        gath_ref[me] = comp_ref[...]
        copies = []
        for k in range(1, N_DEV):
            dev, idx = _peer(k)
            copies.append(pltpu.make_async_remote_copy(
                src_ref=comp_ref, dst_ref=gath_ref.at[me], send_sem=send_sems.at[k], recv_sem=recv_sems.at[k],
                device_id=dev, device_id_type=pl.DeviceIdType.MESH))
        for cp in copies:
            cp.start()
        for cp in copies:
            cp.wait_send()
        for k in range(1, N_DEV):
            dev, idx = _peer(k)
            pltpu.make_async_remote_copy(
                src_ref=comp_ref, dst_ref=gath_ref.at[idx], send_sem=send_sems.at[k], recv_sem=recv_sems.at[k],
                device_id=dev, device_id_type=pl.DeviceIdType.MESH).wait_recv()
        g = gath_ref[0]
        for src in range(1, N_DEV):
            g = g + gath_ref[src]
        go_ref[...] = jnp.zeros(shape, F32)
        r = 0
        for p, n in enumerate(SMALL_PIECES):
            go_ref[8 * p:8 * p + n, :] = g[r:r + n, :]
            r += n
        delta, m2, v2 = _adam_math(go_ref[...], w_ref[...], m_ref[...], v_ref[...])
        d_ref[...] = delta
        mo_ref[...] = m2
        vo_ref[...] = v2
        wait(cin, cout, *csem)

    vm = pl.BlockSpec(memory_space=pltpu.VMEM)
    any_spec = pl.BlockSpec(memory_space=pl.ANY)
    res = _pc(
        body, name="small_allreduce_adamw",
        in_specs=[vm] * 4 + [any_spec] * nw, out_specs=[vm] * 4 + [any_spec] * nw,
        out_shape=[_sds(shape, F32)] * 4 + landed,
        scratch_shapes=[pltpu.VMEM(compact, F32), pltpu.VMEM((N_DEV,) + compact, F32),
                        pltpu.SemaphoreType.DMA((N_DEV,)), pltpu.SemaphoreType.DMA((N_DEV,))] + exchange_sems,
    )(gpack, wpack, mpack, vpack, *last_grads)
    return res[:4], res[4:]


def _rows_from_shards(g):
    return g.reshape(N_DEV * g.shape[1], g.shape[2])


def _shards_from_rows(a):
    return a.reshape(N_DEV, a.shape[0] // N_DEV, a.shape[1])


def _pad_to(a, rows, cols):
    return jnp.pad(a, ((0, rows - a.shape[0]), (0, cols - a.shape[1])))


def _rope_tables(pos):
    inv_freq = 1.0 / (ROPE_THETA ** (jnp.arange(0, ROPE, 2, dtype=F32) / ROPE))
    ang = pos.astype(F32)[:, None] * inv_freq
    cos, sin = jnp.cos(ang), jnp.sin(ang)
    z32, z64, z96 = (jnp.zeros((pos.shape[0], n), F32) for n in (32, 64, 96))
    return (jnp.concatenate([cos, cos, z64], axis=1), jnp.concatenate([-sin, z96], axis=1),
            jnp.concatenate([z32, sin, z64], axis=1))


def _pad_row(vec):
    vec = vec.reshape(1, -1)
    return jnp.pad(vec, ((0, 0), (0, D_MODEL - vec.shape[1])))


def _forward_backward(x, target, tables, gains, shards):
    cc, sn, sp = tables
    ffn_g = gains["ffn_norm"]
    by_cols = ("wg0", "wu0", "wg1", "wu1", "c_in")

    def gather(names, relay_at):
        return [_Cols(shards[n]) if n in by_cols else shards[n] for n in names], False, relay_at

    (got,) = _exchange([shards["w_in"]], False, "gather_w_in")
    w_in = _rows_from_shards(got)
    (n0, proj, cqn, ckvn, krr), (w_uq, w_ukv) = _mla_in_fwd(
        x, gains["mla_norm"], w_in, gains["g_cq"], gains["g_ckv"], cc, sn, sp, carry=gather(["w_uq", "w_ukv"], 0.5))
    (q, k, v), (w_o, conv_norm, conv_w) = _qkv_proj(
        cqn, ckvn, krr, w_uq, w_ukv, cc, sn, sp, carry=gather(["w_o", "conv_norm", "conv_w"], 0.5))
    w_o = _rows_from_shards(w_o)
    conv_norm = conv_norm.reshape(1, D_MODEL)
    conv_w = jnp.transpose(conv_w, (1, 0, 2)).reshape(3, D_MODEL)
    (o, lse), (wg0, wu0, wd0, c_in, c_out, wd1) = _attn_fwd(
        q, k, v, carry=gather(["wg0", "wu0", "wd0", "c_in", "c_out", "wd1"], 0.75))
    wd0 = _rows_from_shards(wd0)
    wd1 = _rows_from_shards(wd1)
    c_out = _rows_from_shards(c_out)
    h1 = _matmul_res(o, w_o, x, "mla_out_fwd")
    (h2, n1, gate0, up0), (wg1, wu1) = _ffn_fwd(
        h1, ffn_g[0:1], wg0, wu0, wd0, "ffn0_fwd", carry=gather(["wg1", "wu1"], 0.7))
    n2, bcx = _rms_matmul(h2, conv_norm, c_in, "conv_in_fwd")
    bu = _conv_fwd(bcx, conv_w)
    h3 = _matmul_res(bu, c_out, h2, "conv_out_fwd")
    (h4, n3, gate1, up1), _ = _ffn_fwd(h3, ffn_g[1:2], wg1, wu1, wd1, "ffn1_fwd")
    dh4, d_final, loss = _final_loss(h4, gains["final_norm"], target)

    small = {"final_norm": d_final}
    parts = {}

    def scatter(**blocks):
        return list(blocks), (list(blocks.values()), True, None)

    (dh3, dh3_b, dh4_b, dgate, dup, act, d_ffn1), _ = _ffn_bwd_x(
        dh4, h3, ffn_g[1:2], gate1, up1, wg1, wu1, wd1, "ffn1_bwd")
    dwg1 = _matmul_tn(dgate, n3, "ffn1_dwg")
    dwu1 = _matmul_tn(dup, n3, "ffn1_dwu")
    dwd1 = _matmul_tn(act, dh4_b, "ffn1_dwd")
    dbuc = _matmul_nt(dh3_b, c_out, "conv_out_bwd")
    d_c_out = _matmul_tn(bu, dh3_b, "conv_dwout")
    dbcx, small["conv_w"] = _conv_bwd(dbuc, bcx, conv_w)
    d_c_in = _matmul_tn(n2, dbcx, "conv_dwin")
    (dh2, dh2_b, small["conv_norm"]), _ = _nt_rmsbwd(dbcx, c_in, h2, conv_norm, dh3, "conv_in_bwd")
    names, carry = scatter(c_in=_Cols(d_c_in), c_out=_shards_from_rows(d_c_out))
    (dh1, dh1_b, dh2_b2, dgate, dup, act, d_ffn0), got = _ffn_bwd_x(
        dh2, h1, ffn_g[0:1], gate0, up0, wg0, wu0, wd0, "ffn0_bwd", carry=carry)
    parts.update(zip(names, got))
    dwg0 = _matmul_tn(dgate, n1, "ffn0_dwg")
    dwu0 = _matmul_tn(dup, n1, "ffn0_dwu")
    dwd0 = _matmul_tn(act, dh2_b2, "ffn0_dwd")
    small["ffn_norm"] = jnp.pad(d_ffn0, ((0, 7), (0, 0))) + jnp.pad(d_ffn1, ((1, 6), (0, 0)))
    do = _matmul_nt(dh1_b, w_o, "mla_out_bwd")
    d_w_o = _matmul_tn(o, dh1_b, "mla_dwo")
    names, carry = scatter(**{n: _shards_from_rows(d) for n, d in dict(
        wg0=dwg0, wg1=dwg1, wu0=dwu0, wu1=dwu1, wd0=dwd0, wd1=dwd1).items()})
    (dq, dk, dv), got = _attn_bwd(q, k, v, do, o, lse, cc, sn, sp, carry=carry)
    parts.update(zip(names, got))
    d_w_uq = _matmul_tn(dq, cqn, "mla_dwuq")
    names, carry = scatter(w_o=_shards_from_rows(d_w_o), w_uq=_shards_from_rows(d_w_uq))
    (dproj, dkv, small["g_cq"], small["g_ckv"]), got = _mla_mid_bwd(
        dq, dk, dv, proj, w_uq, w_ukv, gains["g_cq"], gains["g_ckv"], cc, sn, sp, carry=carry)
    parts.update(zip(names, got))
    d_w_ukv = _matmul_tn(ckvn, dkv, "mla_dwukv")
    names, carry = scatter(w_ukv=_Cols(d_w_ukv))
    d_w_in, got = _matmul_tn(dproj, n0, "mla_dwin", carry=carry)
    parts.update(zip(names, got))
    (dx, _, small["mla_norm"]), _ = _nt_rmsbwd(dproj, w_in, x, gains["mla_norm"], dh1, "mla_in_bwd")
    return loss, dx, parts, _Cols(d_w_in), small


def kernel(x, positions, mla_norm, mla_w_in, mla_g_cq, mla_g_ckv, mla_w_uq, mla_w_ukv, mla_w_o, conv_norm, conv_w_in, conv_w, conv_w_out, ffn_norm, ffn_w_gate, ffn_w_up, ffn_w_down, final_norm, loss_target, m_mla_norm, m_mla_w_in, m_mla_g_cq, m_mla_g_ckv, m_mla_w_uq, m_mla_w_ukv, m_mla_w_o, m_conv_norm, m_conv_w_in, m_conv_w, m_conv_w_out, m_ffn_norm, m_ffn_w_gate, m_ffn_w_up, m_ffn_w_down, m_final_norm, v_mla_norm, v_mla_w_in, v_mla_g_cq, v_mla_g_ckv, v_mla_w_uq, v_mla_w_ukv, v_mla_w_o, v_conv_norm, v_conv_w_in, v_conv_w, v_conv_w_out, v_ffn_norm, v_ffn_w_gate, v_ffn_w_up, v_ffn_w_down, v_final_norm):
    me = 4 * lax.axis_index("x") + 2 * lax.axis_index("y") + lax.axis_index("c")

    bf = lambda a, rows, cols: _pad_to(a.astype(BF16), rows, cols)
    shards = dict(
        w_in=bf(mla_w_in[0], D_MODEL // N_DEV, PROJ_PAD), w_uq=bf(mla_w_uq[0], CQ, HEAD_PAD),
        w_ukv=mla_w_ukv[0].astype(BF16), w_o=mla_w_o[0].astype(BF16),
        c_in=conv_w_in[0].astype(BF16), c_out=conv_w_out[0].astype(BF16),
        conv_norm=conv_norm, conv_w=conv_w[0])
    for l in range(2):
        shards.update({f"wg{l}": bf(ffn_w_gate[l], D_MODEL, FF_SHARD_PAD), f"wu{l}": bf(ffn_w_up[l], D_MODEL, FF_SHARD_PAD),
                       f"wd{l}": bf(ffn_w_down[l], FF_SHARD_PAD, D_MODEL)})
    gains = dict(mla_norm=mla_norm, g_cq=mla_g_cq, g_ckv=mla_g_ckv, ffn_norm=ffn_norm,
                 final_norm=final_norm.reshape(1, -1))
    loss_local, dx, parts, d_w_in, grads = _forward_backward(
        x[0], loss_target[0], _rope_tables(positions[0]), gains, shards)

    col0 = me * (D_MODEL // N_DEV)

    def place(shard):
        return lax.dynamic_update_slice(jnp.zeros((shard.shape[0], D_MODEL), F32), shard, (0, col0))

    no_loss = jnp.zeros((8, D_MODEL), F32)

    def pack(mla_n, g_cq, g_ckv, ffn_n, fin_n, conv_n, conv_taps, loss_tile):
        rows = [mla_n, _pad_row(g_cq), _pad_row(g_ckv), ffn_n, fin_n.reshape(1, -1), conv_n, conv_taps, loss_tile]
        assert all(r.shape[0] in (n, 8) for r, n in zip(rows, SMALL_PIECES))
        return jnp.concatenate([jnp.pad(r, ((0, 8 - r.shape[0]), (0, 0))) for r in rows], axis=0)

    gpack = pack(grads["mla_norm"], grads["g_cq"], grads["g_ckv"], grads["ffn_norm"], grads["final_norm"],
                 grads["conv_norm"], grads["conv_w"], loss_local)
    wpack = pack(mla_norm, mla_g_cq, mla_g_ckv, ffn_norm, final_norm, place(conv_norm), place(conv_w[0]), no_loss)
    mpack = pack(m_mla_norm, m_mla_g_cq, m_mla_g_ckv, m_ffn_norm, m_final_norm, place(m_conv_norm),
                 place(m_conv_w[0]), no_loss)
    vpack = pack(v_mla_norm, v_mla_g_cq, v_mla_g_ckv, v_ffn_norm, v_final_norm, place(v_conv_norm),
                 place(v_conv_w[0]), no_loss)
    small, (parts["w_in"],) = _small_allreduce_adamw(gpack, wpack, mpack, vpack, [d_w_in])
    loss = small[0][56, 0]

    def adamw(name, w, m, v, partials, transposed=False):
        if transposed:
            w, m, v = (jnp.swapaxes(a, 1, 2) for a in (w, m, v))
        outs = _adamw(partials, w, m, v, "adamw_" + name)
        return [jnp.swapaxes(o, 1, 2) for o in outs] if transposed else outs

    res = dict(
        w_in=adamw("w_in", mla_w_in, m_mla_w_in, v_mla_w_in, [parts["w_in"]], True),
        w_uq=adamw("w_uq", mla_w_uq, m_mla_w_uq, v_mla_w_uq, [parts["w_uq"]], True),
        w_ukv=adamw("w_ukv", mla_w_ukv, m_mla_w_ukv, v_mla_w_ukv, [parts["w_ukv"]]),
        w_o=adamw("w_o", mla_w_o, m_mla_w_o, v_mla_w_o, [parts["w_o"]]),
        c_in=adamw("c_in", conv_w_in, m_conv_w_in, v_conv_w_in, [parts["c_in"]]),
        c_out=adamw("c_out", conv_w_out, m_conv_w_out, v_conv_w_out, [parts["c_out"]]),
        wg=adamw("wg", ffn_w_gate, m_ffn_w_gate, v_ffn_w_gate, [parts["wg0"], parts["wg1"]], True),
        wu=adamw("wu", ffn_w_up, m_ffn_w_up, v_ffn_w_up, [parts["wu0"], parts["wu1"]], True),
        wd=adamw("wd", ffn_w_down, m_ffn_w_down, v_ffn_w_down, [parts["wd0"], parts["wd1"]]),
    )

    def unpack(p):
        own = lambda rows: lax.dynamic_slice(rows, (0, col0), (rows.shape[0], D_MODEL // N_DEV))
        return dict(mla_norm=p[0:1], g_cq=p[8:9, :CQ], g_ckv=p[16:17, :CKV], ffn_norm=p[24:26], final_norm=p[32],
                    conv_norm=own(p[40:41]), conv_w=own(p[48:51])[None])

    small = [unpack(p) for p in small]
    order = ["mla_norm", "w_in", "g_cq", "g_ckv", "w_uq", "w_ukv", "w_o", "conv_norm", "c_in", "conv_w", "c_out",
             "ffn_norm", "wg", "wu", "wd", "final_norm"]
    out = [loss, dx[None]]
    for kind in range(4):
        for n in order:
            out.append(res[n][kind] if n in res else small[kind][n])
    return tuple(out)
```

```python
import math

import jax
import jax.numpy as jnp
from jax import lax
from jax.experimental import pallas as pl
from jax.experimental.pallas import tpu as pltpu

F32 = jnp.float32
BF16 = jnp.bfloat16

N_DEV = 8
D_MODEL = 1024
N_HEADS = 8
NOPE = 128
ROPE = 64
V_DIM = 128
HEAD_PAD = 256
CQ = 512
CKV = 256
PROJ = CQ + CKV + ROPE
PROJ_PAD = CQ + CKV + 128
D_FF = 2816
FF_SHARD = D_FF // N_DEV
FF_SHARD_PAD = 384
FF_PAD = FF_SHARD_PAD * N_DEV
CHUNK_SHIFT = 6
RMS_EPS = 1e-6
ROPE_THETA = 10000.0
ATT_SCALE = 1.0 / math.sqrt(NOPE + ROPE)
LOG2_E = math.log2(math.e)
LN_2 = math.log(2.0)
Q_SCALE = ATT_SCALE * LOG2_E
NEG = -1e30

ADAM_LR = 0.001
ADAM_B1 = 0.9
ADAM_B2 = 0.999
ADAM_EPS = 1e-08
ADAM_WD = 0.01
ADAM_STEP = 10

SMALL_PIECES = (1, 1, 1, 2, 1, 1, 3, 1)
SMALL_ROWS = 16

_NT = (((1,), (1,)), ((), ()))
_TN = (((0,), (0,)), ((), ()))


def _pc(body, *, name, out_shape, grid=(), in_specs=None, out_specs=None, scratch_shapes=(), vmem_mb=None):
    params = {}
    if vmem_mb is not None:
        params["vmem_limit_bytes"] = vmem_mb << 20
    kwargs = dict(
        name=name, out_shape=out_shape, grid=grid, scratch_shapes=scratch_shapes,
        compiler_params=pltpu.CompilerParams(**params),
    )
    if in_specs is not None:
        kwargs["in_specs"] = in_specs
    if out_specs is not None:
        kwargs["out_specs"] = out_specs
    return pl.pallas_call(body, **kwargs)


def _pc_carrying(body, carry, operands, *, name, out_shape, grid, in_specs, out_specs, scratch_shapes=()):
    if carry is None:
        return _pc(body, name=name, out_shape=out_shape, grid=grid, in_specs=in_specs, out_specs=out_specs,
                   scratch_shapes=scratch_shapes)(*operands), None
    arrays, scatter, relay_at = carry
    nw, n_in, n_out, n_scr = len(arrays), len(in_specs), len(out_shape), len(scratch_shapes)
    start, relay, wait, landed_shapes, sems, arrays = _exchange_ops(arrays, scatter)
    n_steps = math.prod(grid)
    relay_step = None if relay is None else min(int(relay_at * n_steps), n_steps - 1)

    def wrapped(*refs):
        ins, rest = refs[:n_in], refs[n_in:]
        cin, rest = rest[:nw], rest[nw:]
        outs, rest = rest[:n_out], rest[n_out:]
        cout, rest = rest[:nw], rest[nw:]
        scr, csem = rest[:n_scr], rest[n_scr:]
        step = pl.program_id(0)
        for a in range(1, len(grid)):
            step = step * grid[a] + pl.program_id(a)

        @pl.when(step == 0)
        def _():
            start(cin, cout, *csem)

        if relay is not None:
            @pl.when(step == relay_step)
            def _():
                relay(cin, cout, *csem)

        body(*ins, *outs, *scr)

        @pl.when(step == n_steps - 1)
        def _():
            wait(cin, cout, *csem)

    any_spec = pl.BlockSpec(memory_space=pl.ANY)
    res = _pc(
        wrapped, name=name, grid=grid,
        in_specs=list(in_specs) + [any_spec] * nw, out_specs=list(out_specs) + [any_spec] * nw,
        out_shape=list(out_shape) + landed_shapes, scratch_shapes=list(scratch_shapes) + sems,
    )(*operands, *arrays)
    return res[:n_out], res[n_out:]


def _sds(shape, dtype):
    return jax.ShapeDtypeStruct(shape, dtype)


def _dot(a, b):
    return jnp.dot(a, b, preferred_element_type=F32)


def _dot_nt(a, b):
    return lax.dot_general(a, b, _NT, preferred_element_type=F32)


def _dot_tn(a, b):
    return lax.dot_general(a, b, _TN, preferred_element_type=F32)


def _rstd(x):
    return lax.rsqrt(jnp.mean(x * x, axis=-1, keepdims=True) + RMS_EPS)


def _rms(x, g):
    return (x * _rstd(x)) * g


def _rms_bwd(dy, x, g):
    r = _rstd(x)
    xhat = x * r
    dxhat = dy * g
    dx = r * (dxhat - xhat * jnp.mean(dxhat * xhat, axis=-1, keepdims=True))
    return dx, jnp.sum(dy * xhat, axis=0, keepdims=True)


def _rope(t, cc, sn, sp):
    return t * cc + pltpu.roll(t, 96, 1) * sn + pltpu.roll(t, 32, 1) * sp


def _rope_bwd(dt, cc, sn, sp):
    return dt * cc + pltpu.roll(dt * sn, 32, 1) + pltpu.roll(dt * sp, 96, 1)


def _row_block(s):
    return min(512, s)


_TN_ROWS = 2048
_FFN_ROWS = 1024


def _mla_in_fwd(x, g0, w_in, g_cq, g_ckv, cc, sn, sp, carry=None):
    s = x.shape[0]
    tm = _row_block(s)

    def body(x_ref, g0_ref, w_ref, gcq_ref, gckv_ref, cc_ref, sn_ref, sp_ref,
             n_ref, proj_ref, cqn_ref, ckvn_ref, krr_ref):
        nb = _rms(x_ref[...], g0_ref[...]).astype(BF16)
        n_ref[...] = nb
        proj = _dot(nb, w_ref[...])
        proj_ref[...] = proj
        cqn_ref[...] = _rms(proj[:, :CQ], gcq_ref[...]).astype(BF16)
        ckvn_ref[...] = _rms(proj[:, CQ:CQ + CKV], gckv_ref[...]).astype(BF16)
        krr_ref[...] = _rope(proj[:, CQ + CKV:], cc_ref[...], sn_ref[...], sp_ref[...]).astype(BF16)

    row = lambda n: pl.BlockSpec((tm, n), lambda i: (i, 0))
    full = lambda a: pl.BlockSpec(a.shape, lambda i: (0, 0))
    return _pc_carrying(
        body, carry, (x, g0, w_in, g_cq, g_ckv, cc, sn, sp), name="mla_in_fwd", grid=(s // tm,),
        in_specs=[row(D_MODEL), full(g0), full(w_in), full(g_cq), full(g_ckv), row(128), row(128), row(128)],
        out_specs=[row(D_MODEL), row(PROJ_PAD), row(CQ), row(CKV), row(128)],
        out_shape=[_sds((s, D_MODEL), BF16), _sds((s, PROJ_PAD), F32), _sds((s, CQ), BF16),
                   _sds((s, CKV), BF16), _sds((s, 128), BF16)],
    )


def _qkv_proj(cqn, ckvn, krr, w_uq, w_ukv, cc, sn, sp, carry=None):
    s = cqn.shape[0]
    tm = min(_FFN_ROWS, s)

    def body(cqn_ref, ckvn_ref, krr_ref, wuq_ref, wukv_ref, cc_ref, sn_ref, sp_ref, q_ref, k_ref, v_ref):
        cqn_b, ckvn_b, krr_b = cqn_ref[...], ckvn_ref[...], krr_ref[...]
        cc_b, sn_b, sp_b = cc_ref[...], sn_ref[...], sp_ref[...]
        for h in range(N_HEADS):
            q = _dot(cqn_b, wuq_ref[h]) * Q_SCALE
            q_ref[h, :, :NOPE] = q[:, :NOPE].astype(BF16)
            q_ref[h, :, NOPE:] = _rope(q[:, NOPE:], cc_b, sn_b, sp_b).astype(BF16)
            kv = _dot(ckvn_b, wukv_ref[h])
            k_ref[h, :, :NOPE] = kv[:, :NOPE].astype(BF16)
            k_ref[h, :, NOPE:] = krr_b
            v_ref[h] = kv[:, NOPE:].astype(BF16)

    row = lambda n: pl.BlockSpec((tm, n), lambda i: (i, 0))
    head_w = lambda a: pl.BlockSpec(a.shape, lambda i: (0, 0, 0))
    head_o = lambda n: pl.BlockSpec((N_HEADS, tm, n), lambda i: (0, i, 0))
    return _pc_carrying(
        body, carry, (cqn, ckvn, krr, w_uq, w_ukv, cc, sn, sp), name="qkv_proj", grid=(s // tm,),
        in_specs=[row(CQ), row(CKV), row(128), head_w(w_uq), head_w(w_ukv), row(128), row(128), row(128)],
        out_specs=[head_o(HEAD_PAD), head_o(HEAD_PAD), head_o(V_DIM)],
        out_shape=[_sds((N_HEADS, s, HEAD_PAD), BF16), _sds((N_HEADS, s, HEAD_PAD), BF16),
                   _sds((N_HEADS, s, V_DIM), BF16)],
    )


def _chunk_bias(bq, bk, first_key=0):
    rows = lax.broadcasted_iota(jnp.int32, (bq, bk), 0)
    cols = lax.broadcasted_iota(jnp.int32, (bq, bk), 1) + first_key
    visible = jnp.right_shift(cols, CHUNK_SHIFT) <= jnp.right_shift(rows, CHUNK_SHIFT)
    return jnp.where(visible, 0.0, NEG).astype(F32)


_ATTN_FWD_ROWS = 1024


def _attn_fwd(q, k, v, carry=None):
    s = q.shape[1]
    bk = _row_block(s)
    bq = min(_ATTN_FWD_ROWS, s)
    nd = bq // bk

    def body(q_ref, k_ref, v_ref, o_ref, lse_ref, bias_ref):
        i = pl.program_id(1)
        qb = q_ref[0]

        @pl.when((pl.program_id(0) == 0) & (i == 0))
        def _():
            bias_ref[...] = _chunk_bias(bq, bk)

        def block(j, carry, queries, bias=None):
            m, l, acc = carry
            start = pl.multiple_of(j * bk, bk)
            kb = k_ref[0, pl.ds(start, bk), :]
            vb = v_ref[0, pl.ds(start, bk), :]
            sc = _dot_nt(queries, kb)
            if bias is not None:
                sc = sc + bias
            m_new = jnp.maximum(m, jnp.max(sc, axis=-1, keepdims=True))
            p = jnp.exp2(sc - m_new)
            alpha = jnp.exp2(m - m_new)
            l = alpha * l + jnp.sum(p, axis=-1, keepdims=True)
            acc = alpha * acc + _dot(p.astype(BF16), vb)
            return m_new, l, acc

        carry = (jnp.full((bq, 1), NEG, F32), jnp.zeros((bq, 1), F32), jnp.zeros((bq, V_DIM), F32))
        carry = lax.fori_loop(0, i * nd, lambda j, c: block(j, c, qb), carry)
        for d in range(nd):
            first = d * bk
            carry = block(i * nd + d, carry, qb[first:], bias_ref[:bq - first, :])
            m, l, acc = (c[:bk] for c in carry)
            o_ref[first:first + bk, :] = (acc / l).astype(BF16)
            lse_ref[0, first:first + bk, :] = jnp.broadcast_to(m + jnp.log(l) * LOG2_E, (bk, 128))
            if d < nd - 1:
                carry = tuple(c[bk:] for c in carry)

    return _pc_carrying(
        body, carry, (q, k, v), name="attn_fwd", grid=(N_HEADS, s // bq),
        in_specs=[pl.BlockSpec((1, bq, HEAD_PAD), lambda h, i: (h, i, 0)),
                  pl.BlockSpec((1, s, HEAD_PAD), lambda h, i: (h, 0, 0)),
                  pl.BlockSpec((1, s, V_DIM), lambda h, i: (h, 0, 0))],
        out_specs=[pl.BlockSpec((bq, V_DIM), lambda h, i: (i, h)),
                   pl.BlockSpec((1, bq, 128), lambda h, i: (h, i, 0))],
        out_shape=[_sds((s, N_HEADS * V_DIM), BF16), _sds((N_HEADS, s, 128), F32)],
        scratch_shapes=[pltpu.VMEM((bq, bk), F32)],
    )


def _matmul_res(a, w, res, name):
    s, kd = a.shape
    n = w.shape[1]
    tm = min(_FFN_ROWS, s)

    def body(a_ref, w_ref, r_ref, o_ref):
        o_ref[...] = r_ref[...] + _dot(a_ref[...], w_ref[...])

    return _pc(
        body, name=name, grid=(s // tm,),
        in_specs=[pl.BlockSpec((tm, kd), lambda i: (i, 0)), pl.BlockSpec((kd, n), lambda i: (0, 0)),
                  pl.BlockSpec((tm, n), lambda i: (i, 0))],
        out_specs=pl.BlockSpec((tm, n), lambda i: (i, 0)),
        out_shape=_sds((s, n), F32),
    )(a, w, res)


def _ffn_fwd(h, gain, wg, wu, wd, name, carry=None):
    s = h.shape[0]
    tm = min(_FFN_ROWS, s)
    tf = 512
    nf = FF_PAD // tf

    def body(h_ref, g_ref, wg_ref, wu_ref, wd_ref, o_ref, n_ref, gate_ref, up_ref, acc_ref):
        j = pl.program_id(1)

        @pl.when(j == 0)
        def _():
            n_ref[...] = _rms(h_ref[...], g_ref[...]).astype(BF16)
            acc_ref[...] = jnp.zeros_like(acc_ref)

        nb = n_ref[...]
        gate = _dot(nb, wg_ref[...])
        up = _dot(nb, wu_ref[...])
        gate_ref[...] = gate.astype(BF16)
        up_ref[...] = up.astype(BF16)
        act = gate * jax.nn.sigmoid(gate) * up
        acc_ref[...] += _dot(act.astype(BF16), wd_ref[...])

        @pl.when(j == nf - 1)
        def _():
            o_ref[...] = h_ref[...] + acc_ref[...]

    return _pc_carrying(
        body, carry, (h, gain, wg, wu, wd), name=name, grid=(s // tm, nf),
        in_specs=[pl.BlockSpec((tm, D_MODEL), lambda i, j: (i, 0)), pl.BlockSpec((1, D_MODEL), lambda i, j: (0, 0)),
                  pl.BlockSpec((D_MODEL, tf), lambda i, j: (0, j)), pl.BlockSpec((D_MODEL, tf), lambda i, j: (0, j)),
                  pl.BlockSpec((tf, D_MODEL), lambda i, j: (j, 0))],
        out_specs=[pl.BlockSpec((tm, D_MODEL), lambda i, j: (i, 0)), pl.BlockSpec((tm, D_MODEL), lambda i, j: (i, 0)),
                   pl.BlockSpec((tm, tf), lambda i, j: (i, j)), pl.BlockSpec((tm, tf), lambda i, j: (i, j))],
        out_shape=[_sds((s, D_MODEL), F32), _sds((s, D_MODEL), BF16), _sds((s, FF_PAD), BF16),
                   _sds((s, FF_PAD), BF16)],
        scratch_shapes=[pltpu.VMEM((tm, D_MODEL), F32)],
    )


def _rms_matmul(h, gain, w, name):
    s = h.shape[0]
    n = w.shape[1]
    tm = min(_FFN_ROWS, s)
    tn = 1024
    nn = n // tn

    def body(h_ref, g_ref, w_ref, n_ref, o_ref):
        @pl.when(pl.program_id(1) == 0)
        def _():
            n_ref[...] = _rms(h_ref[...], g_ref[...]).astype(BF16)

        o_ref[...] = _dot(n_ref[...], w_ref[...]).astype(BF16)

    return _pc(
        body, name=name, grid=(s // tm, nn),
        in_specs=[pl.BlockSpec((tm, D_MODEL), lambda i, j: (i, 0)), pl.BlockSpec((1, D_MODEL), lambda i, j: (0, 0)),
                  pl.BlockSpec((D_MODEL, tn), lambda i, j: (0, j))],
        out_specs=[pl.BlockSpec((tm, D_MODEL), lambda i, j: (i, 0)), pl.BlockSpec((tm, tn), lambda i, j: (i, j))],
        out_shape=[_sds((s, D_MODEL), BF16), _sds((s, n), BF16)],
    )(h, gain, w)


def _shift_down(u, k, rows):
    return jnp.where(rows >= k, pltpu.roll(u, k, 0), 0.0)


def _shift_up(u, k, rows, s):
    return jnp.where(rows < s - k, pltpu.roll(u, s - k, 0), 0.0)


_CONV_COLS = 128


def _conv_fwd(bcx, cw):
    s = bcx.shape[0]
    tc = _CONV_COLS
    nc = D_MODEL // tc

    def body(b_ref, c_ref, xp_ref, w_ref, o_ref):
        rows = lax.broadcasted_iota(jnp.int32, (s, tc), 0)
        u = c_ref[...].astype(F32) * xp_ref[...].astype(F32)
        w = w_ref[...]
        uc = w[2:3] * u + w[1:2] * _shift_down(u, 1, rows) + w[0:1] * _shift_down(u, 2, rows)
        o_ref[...] = (b_ref[...].astype(F32) * uc).astype(BF16)

    col = lambda off: pl.BlockSpec((s, tc), lambda j: (0, off + j))
    return _pc(
        body, name="conv_fwd", grid=(nc,),
        in_specs=[col(0), col(nc), col(2 * nc), pl.BlockSpec((3, tc), lambda j: (0, j))],
        out_specs=pl.BlockSpec((s, tc), lambda j: (0, j)),
        out_shape=_sds((s, D_MODEL), BF16),
    )(bcx, bcx, bcx, cw)


def _final_loss(h, gain, target):
    s = h.shape[0]
    tm = _row_block(s)

    def body(h_ref, g_ref, t_ref, dh_ref, dg_ref, loss_ref):
        i = pl.program_id(0)

        @pl.when(i == 0)
        def _():
            dg_ref[...] = jnp.zeros_like(dg_ref)
            loss_ref[...] = jnp.zeros_like(loss_ref)

        hb = h_ref[...]
        e = _rms(hb, g_ref[...]) - t_ref[...]
        loss_ref[...] += 0.5 * jnp.sum(jnp.mean(e * e, axis=-1, keepdims=True))
        dx, dg = _rms_bwd(e * (1.0 / D_MODEL), hb, g_ref[...])
        dh_ref[...] = dx
        dg_ref[...] += dg

    row = pl.BlockSpec((tm, D_MODEL), lambda i: (i, 0))
    vec = pl.BlockSpec((1, D_MODEL), lambda i: (0, 0))
    return _pc(
        body, name="final_loss", grid=(s // tm,),
        in_specs=[row, vec, row],
        out_specs=[row, vec, pl.BlockSpec((8, D_MODEL), lambda i: (0, 0))],
        out_shape=[_sds((s, D_MODEL), F32), _sds((1, D_MODEL), F32), _sds((8, D_MODEL), F32)],
    )(h, gain, target)


def _ffn_bwd_x(dho, h, gain, gate, up, wg, wu, wd, name, carry=None):
    s = h.shape[0]
    tm = _row_block(s)
    tf = 1024
    nf = FF_PAD // tf

    def body(dho_ref, h_ref, g_ref, gate_ref, up_ref, wg_ref, wu_ref, wd_ref,
             dh_ref, dhb_ref, dhob_ref, dgate_ref, dup_ref, act_ref, dgain_ref, acc_ref):
        i = pl.program_id(0)
        j = pl.program_id(1)

        @pl.when(j == 0)
        def _():
            dhob_ref[...] = dho_ref[...].astype(BF16)
            acc_ref[...] = jnp.zeros_like(acc_ref)

        @pl.when((i == 0) & (j == 0))
        def _():
            dgain_ref[...] = jnp.zeros_like(dgain_ref)

        dact = _dot_nt(dhob_ref[...], wd_ref[...])
        g = gate_ref[...].astype(F32)
        u = up_ref[...].astype(F32)
        sg = jax.nn.sigmoid(g)
        silu = g * sg
        dg = (dact * u * (sg * (1.0 + g * (1.0 - sg)))).astype(BF16)
        du = (dact * silu).astype(BF16)
        dgate_ref[...] = dg
        dup_ref[...] = du
        act_ref[...] = (silu * u).astype(BF16)
        acc_ref[...] += _dot_nt(dg, wg_ref[...]) + _dot_nt(du, wu_ref[...])

        @pl.when(j == nf - 1)
        def _():
            dx, dgain = _rms_bwd(acc_ref[...], h_ref[...], g_ref[...])
            dh = dho_ref[...] + dx
            dh_ref[...] = dh
            dhb_ref[...] = dh.astype(BF16)
            dgain_ref[...] += dgain

    row = pl.BlockSpec((tm, D_MODEL), lambda i, j: (i, 0))
    vec = pl.BlockSpec((1, D_MODEL), lambda i, j: (0, 0))
    hid = pl.BlockSpec((tm, tf), lambda i, j: (i, j))
    wcol = pl.BlockSpec((D_MODEL, tf), lambda i, j: (0, j))
    wrow = pl.BlockSpec((tf, D_MODEL), lambda i, j: (j, 0))
    return _pc_carrying(
        body, carry, (dho, h, gain, gate, up, wg, wu, wd), name=name, grid=(s // tm, nf),
        in_specs=[row, row, vec, hid, hid, wcol, wcol, wrow],
        out_specs=[row, row, row, hid, hid, hid, vec],
        out_shape=[_sds((s, D_MODEL), F32), _sds((s, D_MODEL), BF16), _sds((s, D_MODEL), BF16),
                   _sds((s, FF_PAD), BF16), _sds((s, FF_PAD), BF16), _sds((s, FF_PAD), BF16),
                   _sds((1, D_MODEL), F32)],
        scratch_shapes=[pltpu.VMEM((tm, D_MODEL), F32)],
    )


def _nt_rmsbwd(a, w, h, gain, dho, name, carry=None):
    stacked = a.ndim == 3
    if stacked:
        nk, s, tk = a.shape
    else:
        s, tk = a.shape
        nk = 1
    tm = min(_FFN_ROWS, s) if stacked else _row_block(s)

    def body(a_ref, w_ref, h_ref, g_ref, dho_ref, dh_ref, dhb_ref, dgain_ref, acc_ref):
        i = pl.program_id(0)
        j = pl.program_id(1)
        if stacked:
            a_ref = a_ref.at[0]

        @pl.when(j == 0)
        def _():
            acc_ref[...] = jnp.zeros_like(acc_ref)

        @pl.when((i == 0) & (j == 0))
        def _():
            dgain_ref[...] = jnp.zeros_like(dgain_ref)

        acc_ref[...] += _dot_nt(a_ref[...], w_ref[...])

        @pl.when(j == nk - 1)
        def _():
            dx, dgain = _rms_bwd(acc_ref[...], h_ref[...], g_ref[...])
            dh = dho_ref[...] + dx
            dh_ref[...] = dh
            dhb_ref[...] = dh.astype(BF16)
            dgain_ref[...] += dgain

    row = pl.BlockSpec((tm, D_MODEL), lambda i, j: (i, 0))
    vec = pl.BlockSpec((1, D_MODEL), lambda i, j: (0, 0))
    return _pc_carrying(
        body, carry, (a, w, h, gain, dho), name=name, grid=(s // tm, nk),
        in_specs=[pl.BlockSpec((1, tm, tk), lambda i, j: (j, i, 0)) if stacked else pl.BlockSpec((tm, tk), lambda i, j: (i, 0)),
                  pl.BlockSpec((D_MODEL, tk), lambda i, j: (0, j)), row, vec, row],
        out_specs=[row, row, vec],
        out_shape=[_sds((s, D_MODEL), F32), _sds((s, D_MODEL), BF16), _sds((1, D_MODEL), F32)],
        scratch_shapes=[pltpu.VMEM((tm, D_MODEL), F32)],
    )


def _matmul_nt(a, w, name):
    s, kd = a.shape
    n = w.shape[0]
    tm = min(_FFN_ROWS, s)

    def body(a_ref, w_ref, o_ref):
        o_ref[...] = _dot_nt(a_ref[...], w_ref[...]).astype(BF16)

    return _pc(
        body, name=name, grid=(s // tm,),
        in_specs=[pl.BlockSpec((tm, kd), lambda i: (i, 0)), pl.BlockSpec((n, kd), lambda i: (0, 0))],
        out_specs=pl.BlockSpec((tm, n), lambda i: (i, 0)),
        out_shape=_sds((s, n), BF16),
    )(a, w)


def _matmul_tn(a, b, name, carry=None):
    s, m = a.shape
    stacked = b.ndim == 3
    tmm = min(m, 1024)
    if stacked:
        n, tn = b.shape[0] * b.shape[2], b.shape[2]
    else:
        n = b.shape[1]
        tn = n if n <= 1024 else 1024
    tk = min(_TN_ROWS, s)
    nk = s // tk

    def body(a_ref, b_ref, o_ref, acc_ref):
        k = pl.program_id(2)
        if stacked:
            b_ref = b_ref.at[0]

        @pl.when(k == 0)
        def _():
            acc_ref[...] = jnp.zeros_like(acc_ref)

        acc_ref[...] += _dot_tn(a_ref[...], b_ref[...])

        @pl.when(k == nk - 1)
        def _():
            o_ref[...] = acc_ref[...].astype(BF16)

    (out,), got = _pc_carrying(
        body, carry, (a, b), name=name, grid=(m // tmm, n // tn, nk),
        in_specs=[pl.BlockSpec((tk, tmm), lambda i, j, k: (k, i)),
                  pl.BlockSpec((1, tk, tn), lambda i, j, k: (j, k, 0)) if stacked
                  else pl.BlockSpec((tk, tn), lambda i, j, k: (k, j))],
        out_specs=[pl.BlockSpec((tmm, tn), lambda i, j, k: (i, j))],
        out_shape=[_sds((m, n), BF16)],
        scratch_shapes=[pltpu.VMEM((tmm, tn), F32)],
    )
    return out if carry is None else (out, got)


def _conv_bwd(dbuc, bcx, cw):
    s = bcx.shape[0]
    tc = _CONV_COLS
    nc = D_MODEL // tc

    def body(d_ref, b_ref, c_ref, xp_ref, w_ref, dbcx_ref, dw_ref):
        rows = lax.broadcasted_iota(jnp.int32, (s, tc), 0)
        c = c_ref[...].astype(F32)
        xp = xp_ref[...].astype(F32)
        u = c * xp
        u1 = _shift_down(u, 1, rows)
        u2 = _shift_down(u, 2, rows)
        w = w_ref[...]
        uc = w[2:3] * u + w[1:2] * u1 + w[0:1] * u2
        d = d_ref[...].astype(F32)
        dbcx_ref[0] = (d * uc).astype(BF16)
        duc = d * b_ref[...].astype(F32)
        du = w[2:3] * duc + w[1:2] * _shift_up(duc, 1, rows, s) + w[0:1] * _shift_up(duc, 2, rows, s)
        dbcx_ref[1] = (du * xp).astype(BF16)
        dbcx_ref[2] = (du * c).astype(BF16)
        dw_ref[0:1, :] = jnp.sum(duc * u2, axis=0, keepdims=True)
        dw_ref[1:2, :] = jnp.sum(duc * u1, axis=0, keepdims=True)
        dw_ref[2:3, :] = jnp.sum(duc * u, axis=0, keepdims=True)

    col = lambda off: pl.BlockSpec((s, tc), lambda j: (0, off + j))
    tap = pl.BlockSpec((3, tc), lambda j: (0, j))
    return _pc(
        body, name="conv_bwd", grid=(nc,),
        in_specs=[col(0), col(0), col(nc), col(2 * nc), tap],
        out_specs=[pl.BlockSpec((3, s, tc), lambda j: (0, 0, j)), tap],
        out_shape=[_sds((3, s, D_MODEL), BF16), _sds((3, D_MODEL), F32)],
    )(dbuc, bcx, bcx, bcx, cw)


def _attn_bwd(q, k, v, do, o, lse, cc, sn, sp, carry=None):
    s = q.shape[1]
    bk = _row_block(s)
    nb = s // bk

    def body(q_ref, k_ref, v_ref, do_ref, o_ref, lse_ref, cc_ref, sn_ref, sp_ref,
             dq_ref, dk_ref, dv_ref, dqacc_ref, delta_ref, bias_ref):
        j = pl.program_id(1)
        kb = k_ref[0]
        vb = v_ref[0]

        @pl.when((pl.program_id(0) == 0) & (j == 0))
        def _():
            bias_ref[...] = _chunk_bias(bk, bk)

        @pl.when(j == 0)
        def _():
            dqacc_ref[...] = jnp.zeros_like(dqacc_ref)

            def fill(i, _):
                rows = pl.ds(pl.multiple_of(i * bk, bk), bk)
                d = jnp.sum(do_ref[rows, :].astype(F32) * o_ref[rows, :].astype(F32), axis=-1, keepdims=True)
                delta_ref[rows, :] = jnp.broadcast_to(d, (bk, 128))
                return 0

            lax.fori_loop(0, nb, fill, 0)

        def step(i, carry, masked, blocks=1):
            dk, dv = carry
            rows = pl.ds(pl.multiple_of(i * bk, bk), blocks * bk)
            qb = q_ref[0, rows, :]
            dob = do_ref[rows, :]
            sc = _dot_nt(qb, kb)
            if masked:
                sc = sc + bias_ref[...]
            p = jnp.exp2(sc - lse_ref[0, rows, :][:, :1])
            dv = dv + _dot_tn(p.astype(BF16), dob)
            ds = (p * (_dot_nt(dob, vb) - delta_ref[rows, :][:, :1])).astype(BF16)
            dk = dk + _dot_tn(ds, qb)
            dqacc_ref[rows, :] += _dot(ds, kb)
            return dk, dv

        carry = step(j, (jnp.zeros((bk, HEAD_PAD), F32), jnp.zeros((bk, V_DIM), F32)), True)
        dq = dqacc_ref[pl.ds(pl.multiple_of(j * bk, bk), bk), :] * ATT_SCALE
        dq_ref[:, :NOPE] = dq[:, :NOPE].astype(BF16)
        dq_ref[:, NOPE:] = _rope_bwd(dq[:, NOPE:], cc_ref[...], sn_ref[...], sp_ref[...]).astype(BF16)
        rest = nb - 1 - j
        odd = lax.rem(rest, 2)
        carry = lax.fori_loop(0, odd, lambda t, c: step(j + 1, c, False), carry)
        dk, dv = lax.fori_loop(0, rest // 2, lambda t, c: step(j + 1 + odd + 2 * t, c, False, blocks=2), carry)
        dk_ref[0] = (dk * LN_2).astype(BF16)
        dv_ref[0] = dv.astype(BF16)

    blk = lambda n: pl.BlockSpec((1, bk, n), lambda h, j: (h, j, 0))
    whole = lambda n: pl.BlockSpec((1, s, n), lambda h, j: (h, 0, 0))
    cols = pl.BlockSpec((s, V_DIM), lambda h, j: (0, h))
    tab = pl.BlockSpec((bk, 128), lambda h, j: (j, 0))
    return _pc_carrying(
        body, carry, (q, k, v, do, o, lse, cc, sn, sp), name="attn_bwd", grid=(N_HEADS, nb),
        in_specs=[whole(HEAD_PAD), blk(HEAD_PAD), blk(V_DIM), cols, cols, whole(128), tab, tab, tab],
        out_specs=[pl.BlockSpec((bk, HEAD_PAD), lambda h, j: (j, h)), blk(HEAD_PAD), blk(V_DIM)],
        out_shape=[_sds((s, N_HEADS * HEAD_PAD), BF16), _sds((N_HEADS, s, HEAD_PAD), BF16),
                   _sds((N_HEADS, s, V_DIM), BF16)],
        scratch_shapes=[pltpu.VMEM((s, HEAD_PAD), F32), pltpu.VMEM((s, 128), F32), pltpu.VMEM((bk, bk), F32)],
    )


def _mla_mid_bwd(dq, dk, dv, proj, w_uq, w_ukv, g_cq, g_ckv, cc, sn, sp, carry=None):
    s = dq.shape[0]
    tm = _row_block(s)

    def body(dq_ref, dk_ref, dv_ref, proj_ref, wuq_ref, wukv_ref, gcq_ref, gckv_ref, cc_ref, sn_ref, sp_ref,
             dproj_ref, dkv_ref, dgcq_ref, dgckv_ref, acq_ref, ackv_ref, akr_ref):
        @pl.when(pl.program_id(0) == 0)
        def _():
            dgcq_ref[...] = jnp.zeros_like(dgcq_ref)
            dgckv_ref[...] = jnp.zeros_like(dgckv_ref)

        for h in range(N_HEADS):
            cols = slice(h * HEAD_PAD, (h + 1) * HEAD_PAD)
            dkb = dk_ref[h]
            dkv_ref[:, h * HEAD_PAD:h * HEAD_PAD + NOPE] = dkb[:, :NOPE]
            dkv_ref[:, h * HEAD_PAD + NOPE:(h + 1) * HEAD_PAD] = dv_ref[h]
            cq_part = _dot_nt(dq_ref[:, cols], wuq_ref[h])
            ckv_part = _dot_nt(dkv_ref[:, cols], wukv_ref[h])
            kr_part = dkb[:, NOPE:].astype(F32)
            if h == 0:
                acq_ref[...], ackv_ref[...], akr_ref[...] = cq_part, ckv_part, kr_part
            else:
                acq_ref[...] += cq_part
                ackv_ref[...] += ckv_part
                akr_ref[...] += kr_part

        proj = proj_ref[...]
        dcq, dgcq = _rms_bwd(acq_ref[...], proj[:, :CQ], gcq_ref[...])
        dckv, dgckv = _rms_bwd(ackv_ref[...], proj[:, CQ:CQ + CKV], gckv_ref[...])
        dproj_ref[:, :CQ] = dcq.astype(BF16)
        dproj_ref[:, CQ:CQ + CKV] = dckv.astype(BF16)
        dproj_ref[:, CQ + CKV:] = _rope_bwd(akr_ref[...], cc_ref[...], sn_ref[...], sp_ref[...]).astype(BF16)
        dgcq_ref[...] += dgcq
        dgckv_ref[...] += dgckv

    head_blk = lambda n: pl.BlockSpec((N_HEADS, tm, n), lambda i: (0, i, 0))
    head_cols = pl.BlockSpec((tm, N_HEADS * HEAD_PAD), lambda i: (i, 0))
    head_w = lambda a: pl.BlockSpec(a.shape, lambda i: (0, 0, 0))
    row = lambda n: pl.BlockSpec((tm, n), lambda i: (i, 0))
    vec = lambda n: pl.BlockSpec((1, n), lambda i: (0, 0))
    return _pc_carrying(
        body, carry, (dq, dk, dv, proj, w_uq, w_ukv, g_cq, g_ckv, cc, sn, sp),
        name="mla_mid_bwd", grid=(s // tm,),
        in_specs=[head_cols, head_blk(HEAD_PAD), head_blk(V_DIM), row(PROJ_PAD), head_w(w_uq), head_w(w_ukv),
                  vec(CQ), vec(CKV), row(128), row(128), row(128)],
        out_specs=[row(PROJ_PAD), head_cols, vec(CQ), vec(CKV)],
        out_shape=[_sds((s, PROJ_PAD), BF16), _sds((s, N_HEADS * HEAD_PAD), BF16), _sds((1, CQ), F32),
                   _sds((1, CKV), F32)],
        scratch_shapes=[pltpu.VMEM((tm, CQ), F32), pltpu.VMEM((tm, CKV), F32), pltpu.VMEM((tm, 128), F32)],
    )


def _peer(k):
    x, y, c = lax.axis_index("x"), lax.axis_index("y"), lax.axis_index("c")
    px = 1 - x if k & 4 else x
    py = 1 - y if k & 2 else y
    pc = 1 - c if k & 1 else c
    return (px, py, pc), 4 * px + 2 * py + pc


def _exchange_ops(arrays, scatter):
    nw = len(arrays)
    by_cols = [isinstance(a, _Cols) for a in arrays]
    arrays = [a.array if isinstance(a, _Cols) else a for a in arrays]
    direct = range(1, N_DEV) if scatter else (1, 2, 4, 6)

    def columns(ref, idx, width):
        return ref.at[:, pl.ds(pl.multiple_of(idx * width, 128), width)]

    def sent(ins, w, idx):
        if not scatter:
            return ins[w]
        return columns(ins[w], idx, arrays[w].shape[1] // N_DEV) if by_cols[w] else ins[w].at[idx]

    def slot(outs, w, idx):
        if by_cols[w] and not scatter:
            return columns(outs[w], idx, arrays[w].shape[1])
        return outs[w].at[idx]

    def copy(w, k, src, dst, to, send_sems, recv_sems):
        return pltpu.make_async_remote_copy(
            src_ref=src, dst_ref=dst, send_sem=send_sems.at[w * N_DEV + k], recv_sem=recv_sems.at[w * N_DEV + k],
            device_id=to, device_id_type=pl.DeviceIdType.MESH)

    def own_copies(ins, outs, local_sems):
        _, me = _peer(0)
        return [pltpu.make_async_copy(sent(ins, w, me), slot(outs, w, me), local_sems.at[w]) for w in range(nw)]

    def sends(ins, outs, send_sems, recv_sems):
        _, me = _peer(0)
        out = []
        for k in direct:
            dev, idx = _peer(k)
            for w in range(nw):
                out.append(copy(w, k, sent(ins, w, idx), slot(outs, w, me), dev, send_sems, recv_sems))
        return out

    def relays(outs, send_sems, recv_sems):
        sibling, _ = _peer(1)
        out = []
        for k in (2, 4, 6):
            _, idx = _peer(k)
            for w in range(nw):
                out.append(copy(w, k + 1, slot(outs, w, idx), slot(outs, w, idx), sibling, send_sems, recv_sems))
        return out

    def arrival(outs, w, k, send_sems, recv_sems):
        dev, idx = _peer(k)
        return copy(w, k, slot(outs, w, idx), slot(outs, w, idx), dev, send_sems, recv_sems)

    def start(ins, outs, send_sems, recv_sems, local_sems):
        for cp in own_copies(ins, outs, local_sems) + sends(ins, outs, send_sems, recv_sems):
            cp.start()

    def relay(ins, outs, send_sems, recv_sems, local_sems):
        for k in (2, 4, 6):
            for w in range(nw):
                arrival(outs, w, k, send_sems, recv_sems).wait_recv()
        for cp in relays(outs, send_sems, recv_sems):
            cp.start()

    def wait(ins, outs, send_sems, recv_sems, local_sems):
        for cp in own_copies(ins, outs, local_sems):
            cp.wait()
        for cp in sends(ins, outs, send_sems, recv_sems) + ([] if scatter else relays(outs, send_sems, recv_sems)):
            cp.wait_send()
        for k in (range(1, N_DEV) if scatter else (1, 3, 5, 7)):
            for w in range(nw):
                arrival(outs, w, k, send_sems, recv_sems).wait_recv()

    def landed_shape(a, cols):
        if not cols:
            return a.shape if scatter else (N_DEV,) + a.shape
        r, c = a.shape
        return (N_DEV, r, c // N_DEV) if scatter else (r, N_DEV * c)

    landed = [_sds(landed_shape(a, cols), a.dtype) for a, cols in zip(arrays, by_cols)]
    sems = [pltpu.SemaphoreType.DMA((nw * N_DEV,)), pltpu.SemaphoreType.DMA((nw * N_DEV,)),
            pltpu.SemaphoreType.DMA((nw,))]
    return start, (None if scatter else relay), wait, landed, sems, arrays


class _Cols:
    def __init__(self, array):
        self.array = array


def _exchange(arrays, scatter, name):
    nw = len(arrays)
    start, relay, wait, landed, sems, arrays = _exchange_ops(arrays, scatter)

    def body(*refs):
        ins, outs, csem = refs[:nw], refs[nw:2 * nw], refs[2 * nw:]
        start(ins, outs, *csem)
        if relay is not None:
            relay(ins, outs, *csem)
        wait(ins, outs, *csem)

    any_spec = pl.BlockSpec(memory_space=pl.ANY)
    return _pc(body, name=name, in_specs=[any_spec] * nw, out_specs=[any_spec] * nw, out_shape=landed,
               scratch_shapes=sems)(*arrays)


def _adam_math(g, w, m, v):
    m = ADAM_B1 * m + (1.0 - ADAM_B1) * g
    v = ADAM_B2 * v + (1.0 - ADAM_B2) * jnp.square(g)
    m_hat = m / (1.0 - ADAM_B1 ** ADAM_STEP)
    v_hat = v / (1.0 - ADAM_B2 ** ADAM_STEP)
    delta = -ADAM_LR * (m_hat / (jnp.sqrt(v_hat) + ADAM_EPS) + ADAM_WD * w)
    return delta, m, v


def _adam_rows(r):
    for t in range(min(r, 512) // 16 * 16, 0, -16):
        if r % t == 0:
            return t
    return r


def _adamw(parts, w, m, v, name):
    nl, r, c = w.shape
    tr = _adam_rows(r)
    nr = r // tr

    def body(*refs):
        p_refs, (w_ref, m_ref, v_ref, g_ref, d_ref, mo_ref, vo_ref) = refs[:nl], refs[nl:]
        for layer in range(nl):
            @pl.when(pl.program_id(0) == layer)
            def _(p_ref=p_refs[layer]):
                g = p_ref[0].astype(F32)
                for src in range(1, N_DEV):
                    g = g + p_ref[src].astype(F32)
                delta, m2, v2 = _adam_math(g, w_ref[0], m_ref[0], v_ref[0])
                g_ref[0] = g
                d_ref[0] = delta
                mo_ref[0] = m2
                vo_ref[0] = v2

    def part_spec(layer):
        def index(l, i):
            return 0, jnp.where(l == layer, i, jnp.where(l < layer, 0, nr - 1)), 0
        return pl.BlockSpec((N_DEV, tr, c), index)

    blk = pl.BlockSpec((1, tr, c), lambda l, i: (l, i, 0))
    return _pc(
        body, name=name, grid=(nl, nr),
        in_specs=[part_spec(layer) for layer in range(nl)] + [blk, blk, blk],
        out_specs=[blk, blk, blk, blk],
        out_shape=[_sds((nl, r, c), F32)] * 4,
    )(*parts, w, m, v)


def _small_allreduce_adamw(gpack, wpack, mpack, vpack, last_grads):
    shape = gpack.shape
    compact = (SMALL_ROWS, D_MODEL)
    nw = len(last_grads)
    start, _, wait, landed, exchange_sems, last_grads = _exchange_ops(last_grads, True)

    def body(*refs):
        (g_ref, w_ref, m_ref, v_ref), rest = refs[:4], refs[4:]
        cin, rest = rest[:nw], rest[nw:]
        (go_ref, d_ref, mo_ref, vo_ref), rest = rest[:4], rest[4:]
        cout, rest = rest[:nw], rest[nw:]
        (comp_ref, gath_ref, send_sems, recv_sems), csem = rest[:4], rest[4:]
        start(cin, cout, *csem)
        _, me = _peer(0)
        comp_ref[...] = jnp.zeros(compact, F32)
        r = 0
        for p, n in enumerate(SMALL_PIECES):
            comp_ref[r:r + n, :] = g_ref[8 * p:8 * p + n, :]
            r += n
        gath_ref[me] = comp_ref[...]
        copies = []
        for k in range(1, N_DEV):
            dev, idx = _peer(k)
            copies.append(pltpu.make_async_remote_copy(
                src_ref=comp_ref, dst_ref=gath_ref.at[me], send_sem=send_sems.at[k], recv_sem=recv_sems.at[k],
                device_id=dev, device_id_type=pl.DeviceIdType.MESH))
        for cp in copies:
            cp.start()
        for cp in copies:
            cp.wait_send()
        for k in range(1, N_DEV):
            dev, idx = _peer(k)
            pltpu.make_async_remote_copy(
                src_ref=comp_ref, dst_ref=gath_ref.at[idx], send_sem=send_sems.at[k], recv_sem=recv_sems.at[k],
                device_id=dev, device_id_type=pl.DeviceIdType.MESH).wait_recv()
        g = gath_ref[0]
        for src in range(1, N_DEV):
            g = g + gath_ref[src]
        go_ref[...] = jnp.zeros(shape, F32)
        r = 0
        for p, n in enumerate(SMALL_PIECES):
            go_ref[8 * p:8 * p + n, :] = g[r:r + n, :]
            r += n
        delta, m2, v2 = _adam_math(go_ref[...], w_ref[...], m_ref[...], v_ref[...])
        d_ref[...] = delta
        mo_ref[...] = m2
        vo_ref[...] = v2
        wait(cin, cout, *csem)

    vm = pl.BlockSpec(memory_space=pltpu.VMEM)
    any_spec = pl.BlockSpec(memory_space=pl.ANY)
    res = _pc(
        body, name="small_allreduce_adamw",
        in_specs=[vm] * 4 + [any_spec] * nw, out_specs=[vm] * 4 + [any_spec] * nw,
        out_shape=[_sds(shape, F32)] * 4 + landed,
        scratch_shapes=[pltpu.VMEM(compact, F32), pltpu.VMEM((N_DEV,) + compact, F32),
                        pltpu.SemaphoreType.DMA((N_DEV,)), pltpu.SemaphoreType.DMA((N_DEV,))] + exchange_sems,
    )(gpack, wpack, mpack, vpack, *last_grads)
    return res[:4], res[4:]


def _rows_from_shards(g):
    return g.reshape(N_DEV * g.shape[1], g.shape[2])


def _shards_from_rows(a):
    return a.reshape(N_DEV, a.shape[0] // N_DEV, a.shape[1])


def _pad_to(a, rows, cols):
    return jnp.pad(a, ((0, rows - a.shape[0]), (0, cols - a.shape[1])))


def _rope_tables(pos):
    inv_freq = 1.0 / (ROPE_THETA ** (jnp.arange(0, ROPE, 2, dtype=F32) / ROPE))
    ang = pos.astype(F32)[:, None] * inv_freq
    cos, sin = jnp.cos(ang), jnp.sin(ang)
    z32, z64, z96 = (jnp.zeros((pos.shape[0], n), F32) for n in (32, 64, 96))
    return (jnp.concatenate([cos, cos, z64], axis=1), jnp.concatenate([-sin, z96], axis=1),
            jnp.concatenate([z32, sin, z64], axis=1))


def _pad_row(vec):
    vec = vec.reshape(1, -1)
    return jnp.pad(vec, ((0, 0), (0, D_MODEL - vec.shape[1])))


def _forward_backward(x, target, tables, gains, shards):
    cc, sn, sp = tables
    ffn_g = gains["ffn_norm"]
    by_cols = ("wg0", "wu0", "wg1", "wu1", "c_in")

    def gather(names, relay_at):
        return [_Cols(shards[n]) if n in by_cols else shards[n] for n in names], False, relay_at

    (got,) = _exchange([shards["w_in"]], False, "gather_w_in")
    w_in = _rows_from_shards(got)
    (n0, proj, cqn, ckvn, krr), (w_uq, w_ukv) = _mla_in_fwd(
        x, gains["mla_norm"], w_in, gains["g_cq"], gains["g_ckv"], cc, sn, sp, carry=gather(["w_uq", "w_ukv"], 0.5))
    (q, k, v), (w_o, conv_norm, conv_w) = _qkv_proj(
        cqn, ckvn, krr, w_uq, w_ukv, cc, sn, sp, carry=gather(["w_o", "conv_norm", "conv_w"], 0.5))
    w_o = _rows_from_shards(w_o)
    conv_norm = conv_norm.reshape(1, D_MODEL)
    conv_w = jnp.transpose(conv_w, (1, 0, 2)).reshape(3, D_MODEL)
    (o, lse), (wg0, wu0, wd0, c_in, c_out, wd1) = _attn_fwd(
        q, k, v, carry=gather(["wg0", "wu0", "wd0", "c_in", "c_out", "wd1"], 0.75))
    wd0 = _rows_from_shards(wd0)
    wd1 = _rows_from_shards(wd1)
    c_out = _rows_from_shards(c_out)
    h1 = _matmul_res(o, w_o, x, "mla_out_fwd")
    (h2, n1, gate0, up0), (wg1, wu1) = _ffn_fwd(
        h1, ffn_g[0:1], wg0, wu0, wd0, "ffn0_fwd", carry=gather(["wg1", "wu1"], 0.7))
    n2, bcx = _rms_matmul(h2, conv_norm, c_in, "conv_in_fwd")
    bu = _conv_fwd(bcx, conv_w)
    h3 = _matmul_res(bu, c_out, h2, "conv_out_fwd")
    (h4, n3, gate1, up1), _ = _ffn_fwd(h3, ffn_g[1:2], wg1, wu1, wd1, "ffn1_fwd")
    dh4, d_final, loss = _final_loss(h4, gains["final_norm"], target)

    small = {"final_norm": d_final}
    parts = {}

    def scatter(**blocks):
        return list(blocks), (list(blocks.values()), True, None)

    (dh3, dh3_b, dh4_b, dgate, dup, act, d_ffn1), _ = _ffn_bwd_x(
        dh4, h3, ffn_g[1:2], gate1, up1, wg1, wu1, wd1, "ffn1_bwd")
    dwg1 = _matmul_tn(dgate, n3, "ffn1_dwg")
    dwu1 = _matmul_tn(dup, n3, "ffn1_dwu")
    dwd1 = _matmul_tn(act, dh4_b, "ffn1_dwd")
    dbuc = _matmul_nt(dh3_b, c_out, "conv_out_bwd")
    d_c_out = _matmul_tn(bu, dh3_b, "conv_dwout")
    dbcx, small["conv_w"] = _conv_bwd(dbuc, bcx, conv_w)
    d_c_in = _matmul_tn(n2, dbcx, "conv_dwin")
    (dh2, dh2_b, small["conv_norm"]), _ = _nt_rmsbwd(dbcx, c_in, h2, conv_norm, dh3, "conv_in_bwd")
    names, carry = scatter(c_in=_Cols(d_c_in), c_out=_shards_from_rows(d_c_out))
    (dh1, dh1_b, dh2_b2, dgate, dup, act, d_ffn0), got = _ffn_bwd_x(
        dh2, h1, ffn_g[0:1], gate0, up0, wg0, wu0, wd0, "ffn0_bwd", carry=carry)
    parts.update(zip(names, got))
    dwg0 = _matmul_tn(dgate, n1, "ffn0_dwg")
    dwu0 = _matmul_tn(dup, n1, "ffn0_dwu")
    dwd0 = _matmul_tn(act, dh2_b2, "ffn0_dwd")
    small["ffn_norm"] = jnp.pad(d_ffn0, ((0, 7), (0, 0))) + jnp.pad(d_ffn1, ((1, 6), (0, 0)))
    do = _matmul_nt(dh1_b, w_o, "mla_out_bwd")
    d_w_o = _matmul_tn(o, dh1_b, "mla_dwo")
    names, carry = scatter(**{n: _shards_from_rows(d) for n, d in dict(
        wg0=dwg0, wg1=dwg1, wu0=dwu0, wu1=dwu1, wd0=dwd0, wd1=dwd1).items()})
    (dq, dk, dv), got = _attn_bwd(q, k, v, do, o, lse, cc, sn, sp, carry=carry)
    parts.update(zip(names, got))
    d_w_uq = _matmul_tn(dq, cqn, "mla_dwuq")
    names, carry = scatter(w_o=_shards_from_rows(d_w_o), w_uq=_shards_from_rows(d_w_uq))
    (dproj, dkv, small["g_cq"], small["g_ckv"]), got = _mla_mid_bwd(
        dq, dk, dv, proj, w_uq, w_ukv, gains["g_cq"], gains["g_ckv"], cc, sn, sp, carry=carry)
    parts.update(zip(names, got))
    d_w_ukv = _matmul_tn(ckvn, dkv, "mla_dwukv")
    names, carry = scatter(w_ukv=_Cols(d_w_ukv))
    d_w_in, got = _matmul_tn(dproj, n0, "mla_dwin", carry=carry)
    parts.update(zip(names, got))
    (dx, _, small["mla_norm"]), _ = _nt_rmsbwd(dproj, w_in, x, gains["mla_norm"], dh1, "mla_in_bwd")
    return loss, dx, parts, _Cols(d_w_in), small


def kernel(x, positions, mla_norm, mla_w_in, mla_g_cq, mla_g_ckv, mla_w_uq, mla_w_ukv, mla_w_o, conv_norm, conv_w_in, conv_w, conv_w_out, ffn_norm, ffn_w_gate, ffn_w_up, ffn_w_down, final_norm, loss_target, m_mla_norm, m_mla_w_in, m_mla_g_cq, m_mla_g_ckv, m_mla_w_uq, m_mla_w_ukv, m_mla_w_o, m_conv_norm, m_conv_w_in, m_conv_w, m_conv_w_out, m_ffn_norm, m_ffn_w_gate, m_ffn_w_up, m_ffn_w_down, m_final_norm, v_mla_norm, v_mla_w_in, v_mla_g_cq, v_mla_g_ckv, v_mla_w_uq, v_mla_w_ukv, v_mla_w_o, v_conv_norm, v_conv_w_in, v_conv_w, v_conv_w_out, v_ffn_norm, v_ffn_w_gate, v_ffn_w_up, v_ffn_w_down, v_final_norm):
    me = 4 * lax.axis_index("x") + 2 * lax.axis_index("y") + lax.axis_index("c")

    bf = lambda a, rows, cols: _pad_to(a.astype(BF16), rows, cols)
    shards = dict(
        w_in=bf(mla_w_in[0], D_MODEL // N_DEV, PROJ_PAD), w_uq=bf(mla_w_uq[0], CQ, HEAD_PAD),
        w_ukv=mla_w_ukv[0].astype(BF16), w_o=mla_w_o[0].astype(BF16),
        c_in=conv_w_in[0].astype(BF16), c_out=conv_w_out[0].astype(BF16),
        conv_norm=conv_norm, conv_w=conv_w[0])
    for l in range(2):
        shards.update({f"wg{l}": bf(ffn_w_gate[l], D_MODEL, FF_SHARD_PAD), f"wu{l}": bf(ffn_w_up[l], D_MODEL, FF_SHARD_PAD),
                       f"wd{l}": bf(ffn_w_down[l], FF_SHARD_PAD, D_MODEL)})
    gains = dict(mla_norm=mla_norm, g_cq=mla_g_cq, g_ckv=mla_g_ckv, ffn_norm=ffn_norm,
                 final_norm=final_norm.reshape(1, -1))
    loss_local, dx, parts, d_w_in, grads = _forward_backward(
        x[0], loss_target[0], _rope_tables(positions[0]), gains, shards)

    col0 = me * (D_MODEL // N_DEV)

    def place(shard):
        return lax.dynamic_update_slice(jnp.zeros((shard.shape[0], D_MODEL), F32), shard, (0, col0))

    no_loss = jnp.zeros((8, D_MODEL), F32)

    def pack(mla_n, g_cq, g_ckv, ffn_n, fin_n, conv_n, conv_taps, loss_tile):
        rows = [mla_n, _pad_row(g_cq), _pad_row(g_ckv), ffn_n, fin_n.reshape(1, -1), conv_n, conv_taps, loss_tile]
        assert all(r.shape[0] in (n, 8) for r, n in zip(rows, SMALL_PIECES))
        return jnp.concatenate([jnp.pad(r, ((0, 8 - r.shape[0]), (0, 0))) for r in rows], axis=0)

    gpack = pack(grads["mla_norm"], grads["g_cq"], grads["g_ckv"], grads["ffn_norm"], grads["final_norm"],
                 grads["conv_norm"], grads["conv_w"], loss_local)
    wpack = pack(mla_norm, mla_g_cq, mla_g_ckv, ffn_norm, final_norm, place(conv_norm), place(conv_w[0]), no_loss)
    mpack = pack(m_mla_norm, m_mla_g_cq, m_mla_g_ckv, m_ffn_norm, m_final_norm, place(m_conv_norm),
                 place(m_conv_w[0]), no_loss)
    vpack = pack(v_mla_norm, v_mla_g_cq, v_mla_g_ckv, v_ffn_norm, v_final_norm, place(v_conv_norm),
                 place(v_conv_w[0]), no_loss)
    small, (parts["w_in"],) = _small_allreduce_adamw(gpack, wpack, mpack, vpack, [d_w_in])
    loss = small[0][56, 0]

    def adamw(name, w, m, v, partials, transposed=False):
        if transposed:
            w, m, v = (jnp.swapaxes(a, 1, 2) for a in (w, m, v))
        outs = _adamw(partials, w, m, v, "adamw_" + name)
        return [jnp.swapaxes(o, 1, 2) for o in outs] if transposed else outs

    res = dict(
        w_in=adamw("w_in", mla_w_in, m_mla_w_in, v_mla_w_in, [parts["w_in"]], True),
        w_uq=adamw("w_uq", mla_w_uq, m_mla_w_uq, v_mla_w_uq, [parts["w_uq"]], True),
        w_ukv=adamw("w_ukv", mla_w_ukv, m_mla_w_ukv, v_mla_w_ukv, [parts["w_ukv"]]),
        w_o=adamw("w_o", mla_w_o, m_mla_w_o, v_mla_w_o, [parts["w_o"]]),
        c_in=adamw("c_in", conv_w_in, m_conv_w_in, v_conv_w_in, [parts["c_in"]]),
        c_out=adamw("c_out", conv_w_out, m_conv_w_out, v_conv_w_out, [parts["c_out"]]),
        wg=adamw("wg", ffn_w_gate, m_ffn_w_gate, v_ffn_w_gate, [parts["wg0"], parts["wg1"]], True),
        wu=adamw("wu", ffn_w_up, m_ffn_w_up, v_ffn_w_up, [parts["wu0"], parts["wu1"]], True),
        wd=adamw("wd", ffn_w_down, m_ffn_w_down, v_ffn_w_down, [parts["wd0"], parts["wd1"]]),
    )

    def unpack(p):
        own = lambda rows: lax.dynamic_slice(rows, (0, col0), (rows.shape[0], D_MODEL // N_DEV))
        return dict(mla_norm=p[0:1], g_cq=p[8:9, :CQ], g_ckv=p[16:17, :CKV], ffn_norm=p[24:26], final_norm=p[32],
                    conv_norm=own(p[40:41]), conv_w=own(p[48:51])[None])

    small = [unpack(p) for p in small]
    order = ["mla_norm", "w_in", "g_cq", "g_ckv", "w_uq", "w_ukv", "w_o", "conv_norm", "c_in", "conv_w", "c_out",
             "ffn_norm", "wg", "wu", "wd", "final_norm"]
    out = [loss, dx[None]]
    for kind in range(4):
        for n in order:
            out.append(res[n][kind] if n in res else small[kind][n])
    return tuple(out)
```

```python
import math

import jax
import jax.numpy as jnp
from jax import lax
from jax.experimental import pallas as pl
from jax.experimental.pallas import tpu as pltpu

F32 = jnp.float32
BF16 = jnp.bfloat16

N_DEV = 8
D_MODEL = 1024
N_HEADS = 8
NOPE = 128
ROPE = 64
V_DIM = 128
HEAD_PAD = 256
CQ = 512
CKV = 256
PROJ = CQ + CKV + ROPE
PROJ_PAD = CQ + CKV + 128
D_FF = 2816
FF_SHARD = D_FF // N_DEV
FF_SHARD_PAD = 384
FF_PAD = FF_SHARD_PAD * N_DEV
CHUNK_SHIFT = 6
RMS_EPS = 1e-6
ROPE_THETA = 10000.0
ATT_SCALE = 1.0 / math.sqrt(NOPE + ROPE)
LOG2_E = math.log2(math.e)
LN_2 = math.log(2.0)
Q_SCALE = ATT_SCALE * LOG2_E
NEG = -1e30

ADAM_LR = 0.001
ADAM_B1 = 0.9
ADAM_B2 = 0.999
ADAM_EPS = 1e-08
ADAM_WD = 0.01
ADAM_STEP = 10

SMALL_PIECES = (1, 1, 1, 2, 1, 1, 3, 1)
SMALL_ROWS = 16

_NT = (((1,), (1,)), ((), ()))
_TN = (((0,), (0,)), ((), ()))


def _pc(body, *, name, out_shape, grid=(), in_specs=None, out_specs=None, scratch_shapes=(), vmem_mb=None):
    params = {}
    if vmem_mb is not None:
        params["vmem_limit_bytes"] = vmem_mb << 20
    kwargs = dict(
        name=name, out_shape=out_shape, grid=grid, scratch_shapes=scratch_shapes,
        compiler_params=pltpu.CompilerParams(**params),
    )
    if in_specs is not None:
        kwargs["in_specs"] = in_specs
    if out_specs is not None:
        kwargs["out_specs"] = out_specs
    return pl.pallas_call(body, **kwargs)


def _pc_carrying(body, carry, operands, *, name, out_shape, grid, in_specs, out_specs, scratch_shapes=()):
    if carry is None:
        return _pc(body, name=name, out_shape=out_shape, grid=grid, in_specs=in_specs, out_specs=out_specs,
                   scratch_shapes=scratch_shapes)(*operands), None
    arrays, scatter, relay_at = carry
    nw, n_in, n_out, n_scr = len(arrays), len(in_specs), len(out_shape), len(scratch_shapes)
    start, relay, wait, landed_shapes, sems, arrays = _exchange_ops(arrays, scatter)
    n_steps = math.prod(grid)
    relay_step = None if relay is None else min(int(relay_at * n_steps), n_steps - 1)

    def wrapped(*refs):
        ins, rest = refs[:n_in], refs[n_in:]
        cin, rest = rest[:nw], rest[nw:]
        outs, rest = rest[:n_out], rest[n_out:]
        cout, rest = rest[:nw], rest[nw:]
        scr, csem = rest[:n_scr], rest[n_scr:]
        step = pl.program_id(0)
        for a in range(1, len(grid)):
            step = step * grid[a] + pl.program_id(a)

        @pl.when(step == 0)
        def _():
            start(cin, cout, *csem)

        if relay is not None:
            @pl.when(step == relay_step)
            def _():
                relay(cin, cout, *csem)

        body(*ins, *outs, *scr)

        @pl.when(step == n_steps - 1)
        def _():
            wait(cin, cout, *csem)

    any_spec = pl.BlockSpec(memory_space=pl.ANY)
    res = _pc(
        wrapped, name=name, grid=grid,
        in_specs=list(in_specs) + [any_spec] * nw, out_specs=list(out_specs) + [any_spec] * nw,
        out_shape=list(out_shape) + landed_shapes, scratch_shapes=list(scratch_shapes) + sems,
    )(*operands, *arrays)
    return res[:n_out], res[n_out:]


def _sds(shape, dtype):
    return jax.ShapeDtypeStruct(shape, dtype)


def _dot(a, b):
    return jnp.dot(a, b, preferred_element_type=F32)


def _dot_nt(a, b):
    return lax.dot_general(a, b, _NT, preferred_element_type=F32)


def _dot_tn(a, b):
    return lax.dot_general(a, b, _TN, preferred_element_type=F32)


def _rstd(x):
    return lax.rsqrt(jnp.mean(x * x, axis=-1, keepdims=True) + RMS_EPS)


def _rms(x, g):
    return (x * _rstd(x)) * g


def _rms_bwd(dy, x, g):
    r = _rstd(x)
    xhat = x * r
    dxhat = dy * g
    dx = r * (dxhat - xhat * jnp.mean(dxhat * xhat, axis=-1, keepdims=True))
    return dx, jnp.sum(dy * xhat, axis=0, keepdims=True)


def _rope(t, cc, sn, sp):
    return t * cc + pltpu.roll(t, 96, 1) * sn + pltpu.roll(t, 32, 1) * sp


def _rope_bwd(dt, cc, sn, sp):
    return dt * cc + pltpu.roll(dt * sn, 32, 1) + pltpu.roll(dt * sp, 96, 1)


def _row_block(s):
    return min(512, s)


_TN_ROWS = 2048
_FFN_ROWS = 1024


def _mla_in_fwd(x, g0, w_in, g_cq, g_ckv, cc, sn, sp, carry=None):
    s = x.shape[0]
    tm = _row_block(s)

    def body(x_ref, g0_ref, w_ref, gcq_ref, gckv_ref, cc_ref, sn_ref, sp_ref,
             n_ref, proj_ref, cqn_ref, ckvn_ref, krr_ref):
        nb = _rms(x_ref[...], g0_ref[...]).astype(BF16)
        n_ref[...] = nb
        proj = _dot(nb, w_ref[...])
        proj_ref[...] = proj
        cqn_ref[...] = _rms(proj[:, :CQ], gcq_ref[...]).astype(BF16)
        ckvn_ref[...] = _rms(proj[:, CQ:CQ + CKV], gckv_ref[...]).astype(BF16)
        krr_ref[...] = _rope(proj[:, CQ + CKV:], cc_ref[...], sn_ref[...], sp_ref[...]).astype(BF16)

    row = lambda n: pl.BlockSpec((tm, n), lambda i: (i, 0))
    full = lambda a: pl.BlockSpec(a.shape, lambda i: (0, 0))
    return _pc_carrying(
        body, carry, (x, g0, w_in, g_cq, g_ckv, cc, sn, sp), name="mla_in_fwd", grid=(s // tm,),
        in_specs=[row(D_MODEL), full(g0), full(w_in), full(g_cq), full(g_ckv), row(128), row(128), row(128)],
        out_specs=[row(D_MODEL), row(PROJ_PAD), row(CQ), row(CKV), row(128)],
        out_shape=[_sds((s, D_MODEL), BF16), _sds((s, PROJ_PAD), F32), _sds((s, CQ), BF16),
                   _sds((s, CKV), BF16), _sds((s, 128), BF16)],
    )


def _qkv_proj(cqn, ckvn, krr, w_uq, w_ukv, cc, sn, sp, carry=None):
    s = cqn.shape[0]
    tm = min(_FFN_ROWS, s)

    def body(cqn_ref, ckvn_ref, krr_ref, wuq_ref, wukv_ref, cc_ref, sn_ref, sp_ref, q_ref, k_ref, v_ref):
        cqn_b, ckvn_b, krr_b = cqn_ref[...], ckvn_ref[...], krr_ref[...]
        cc_b, sn_b, sp_b = cc_ref[...], sn_ref[...], sp_ref[...]
        for h in range(N_HEADS):
            q = _dot(cqn_b, wuq_ref[h]) * Q_SCALE
            q_ref[h, :, :NOPE] = q[:, :NOPE].astype(BF16)
            q_ref[h, :, NOPE:] = _rope(q[:, NOPE:], cc_b, sn_b, sp_b).astype(BF16)
            kv = _dot(ckvn_b, wukv_ref[h])
            k_ref[h, :, :NOPE] = kv[:, :NOPE].astype(BF16)
            k_ref[h, :, NOPE:] = krr_b
            v_ref[h] = kv[:, NOPE:].astype(BF16)

    row = lambda n: pl.BlockSpec((tm, n), lambda i: (i, 0))
    head_w = lambda a: pl.BlockSpec(a.shape, lambda i: (0, 0, 0))
    head_o = lambda n: pl.BlockSpec((N_HEADS, tm, n), lambda i: (0, i, 0))
    return _pc_carrying(
        body, carry, (cqn, ckvn, krr, w_uq, w_ukv, cc, sn, sp), name="qkv_proj", grid=(s // tm,),
        in_specs=[row(CQ), row(CKV), row(128), head_w(w_uq), head_w(w_ukv), row(128), row(128), row(128)],
        out_specs=[head_o(HEAD_PAD), head_o(HEAD_PAD), head_o(V_DIM)],
        out_shape=[_sds((N_HEADS, s, HEAD_PAD), BF16), _sds((N_HEADS, s, HEAD_PAD), BF16),
                   _sds((N_HEADS, s, V_DIM), BF16)],
    )


def _chunk_bias(bq, bk, first_key=0):
    rows = lax.broadcasted_iota(jnp.int32, (bq, bk), 0)
    cols = lax.broadcasted_iota(jnp.int32, (bq, bk), 1) + first_key
    visible = jnp.right_shift(cols, CHUNK_SHIFT) <= jnp.right_shift(rows, CHUNK_SHIFT)
    return jnp.where(visible, 0.0, NEG).astype(F32)


_ATTN_FWD_ROWS = 1024


def _attn_fwd(q, k, v, carry=None):
    s = q.shape[1]
    bk = _row_block(s)
    bq = min(_ATTN_FWD_ROWS, s)
    nd = bq // bk

    def body(q_ref, k_ref, v_ref, o_ref, lse_ref, bias_ref):
        i = pl.program_id(1)
        qb = q_ref[0]

        @pl.when((pl.program_id(0) == 0) & (i == 0))
        def _():
            bias_ref[...] = _chunk_bias(bq, bk)

        def block(j, carry, queries, bias=None):
            m, l, acc = carry
            start = pl.multiple_of(j * bk, bk)
            kb = k_ref[0, pl.ds(start, bk), :]
            vb = v_ref[0, pl.ds(start, bk), :]
            sc = _dot_nt(queries, kb)
            if bias is not None:
                sc = sc + bias
            m_new = jnp.maximum(m, jnp.max(sc, axis=-1, keepdims=True))
            p = jnp.exp2(sc - m_new)
            alpha = jnp.exp2(m - m_new)
            l = alpha * l + jnp.sum(p, axis=-1, keepdims=True)
            acc = alpha * acc + _dot(p.astype(BF16), vb)
            return m_new, l, acc

        carry = (jnp.full((bq, 1), NEG, F32), jnp.zeros((bq, 1), F32), jnp.zeros((bq, V_DIM), F32))
        carry = lax.fori_loop(0, i * nd, lambda j, c: block(j, c, qb), carry)
        for d in range(nd):
            first = d * bk
            carry = block(i * nd + d, carry, qb[first:], bias_ref[:bq - first, :])
            m, l, acc = (c[:bk] for c in carry)
            o_ref[first:first + bk, :] = (acc / l).astype(BF16)
            lse_ref[0, first:first + bk, :] = jnp.broadcast_to(m + jnp.log(l) * LOG2_E, (bk, 128))
            if d < nd - 1:
                carry = tuple(c[bk:] for c in carry)

    return _pc_carrying(
        body, carry, (q, k, v), name="attn_fwd", grid=(N_HEADS, s // bq),
        in_specs=[pl.BlockSpec((1, bq, HEAD_PAD), lambda h, i: (h, i, 0)),
                  pl.BlockSpec((1, s, HEAD_PAD), lambda h, i: (h, 0, 0)),
                  pl.BlockSpec((1, s, V_DIM), lambda h, i: (h, 0, 0))],
        out_specs=[pl.BlockSpec((bq, V_DIM), lambda h, i: (i, h)),
                   pl.BlockSpec((1, bq, 128), lambda h, i: (h, i, 0))],
        out_shape=[_sds((s, N_HEADS * V_DIM), BF16), _sds((N_HEADS, s, 128), F32)],
        scratch_shapes=[pltpu.VMEM((bq, bk), F32)],
    )


def _matmul_res(a, w, res, name):
    s, kd = a.shape
    n = w.shape[1]
    tm = min(_FFN_ROWS, s)

    def body(a_ref, w_ref, r_ref, o_ref):
        o_ref[...] = r_ref[...] + _dot(a_ref[...], w_ref[...])

    return _pc(
        body, name=name, grid=(s // tm,),
        in_specs=[pl.BlockSpec((tm, kd), lambda i: (i, 0)), pl.BlockSpec((kd, n), lambda i: (0, 0)),
                  pl.BlockSpec((tm, n), lambda i: (i, 0))],
        out_specs=pl.BlockSpec((tm, n), lambda i: (i, 0)),
        out_shape=_sds((s, n), F32),
    )(a, w, res)


def _ffn_fwd(h, gain, wg, wu, wd, name, carry=None):
    s = h.shape[0]
    tm = min(_FFN_ROWS, s)
    tf = 512
    nf = FF_PAD // tf

    def body(h_ref, g_ref, wg_ref, wu_ref, wd_ref, o_ref, n_ref, gate_ref, up_ref, acc_ref):
        j = pl.program_id(1)

        @pl.when(j == 0)
        def _():
            n_ref[...] = _rms(h_ref[...], g_ref[...]).astype(BF16)
            acc_ref[...] = jnp.zeros_like(acc_ref)

        nb = n_ref[...]
        gate = _dot(nb, wg_ref[...])
        up = _dot(nb, wu_ref[...])
        gate_ref[...] = gate.astype(BF16)
        up_ref[...] = up.astype(BF16)
        act = gate * jax.nn.sigmoid(gate) * up
        acc_ref[...] += _dot(act.astype(BF16), wd_ref[...])

        @pl.when(j == nf - 1)
        def _():
            o_ref[...] = h_ref[...] + acc_ref[...]

    return _pc_carrying(
        body, carry, (h, gain, wg, wu, wd), name=name, grid=(s // tm, nf),
        in_specs=[pl.BlockSpec((tm, D_MODEL), lambda i, j: (i, 0)), pl.BlockSpec((1, D_MODEL), lambda i, j: (0, 0)),
                  pl.BlockSpec((D_MODEL, tf), lambda i, j: (0, j)), pl.BlockSpec((D_MODEL, tf), lambda i, j: (0, j)),
                  pl.BlockSpec((tf, D_MODEL), lambda i, j: (j, 0))],
        out_specs=[pl.BlockSpec((tm, D_MODEL), lambda i, j: (i, 0)), pl.BlockSpec((tm, D_MODEL), lambda i, j: (i, 0)),
                   pl.BlockSpec((tm, tf), lambda i, j: (i, j)), pl.BlockSpec((tm, tf), lambda i, j: (i, j))],
        out_shape=[_sds((s, D_MODEL), F32), _sds((s, D_MODEL), BF16), _sds((s, FF_PAD), BF16),
                   _sds((s, FF_PAD), BF16)],
        scratch_shapes=[pltpu.VMEM((tm, D_MODEL), F32)],
    )


def _rms_matmul(h, gain, w, name):
    s = h.shape[0]
    n = w.shape[1]
    tm = min(_FFN_ROWS, s)
    tn = 1024
    nn = n // tn

    def body(h_ref, g_ref, w_ref, n_ref, o_ref):
        @pl.when(pl.program_id(1) == 0)
        def _():
            n_ref[...] = _rms(h_ref[...], g_ref[...]).astype(BF16)

        o_ref[...] = _dot(n_ref[...], w_ref[...]).astype(BF16)

    return _pc(
        body, name=name, grid=(s // tm, nn),
        in_specs=[pl.BlockSpec((tm, D_MODEL), lambda i, j: (i, 0)), pl.BlockSpec((1, D_MODEL), lambda i, j: (0, 0)),
                  pl.BlockSpec((D_MODEL, tn), lambda i, j: (0, j))],
        out_specs=[pl.BlockSpec((tm, D_MODEL), lambda i, j: (i, 0)), pl.BlockSpec((tm, tn), lambda i, j: (i, j))],
        out_shape=[_sds((s, D_MODEL), BF16), _sds((s, n), BF16)],
    )(h, gain, w)


def _shift_down(u, k, rows):
    return jnp.where(rows >= k, pltpu.roll(u, k, 0), 0.0)


def _shift_up(u, k, rows, s):
    return jnp.where(rows < s - k, pltpu.roll(u, s - k, 0), 0.0)


_CONV_COLS = 128


def _conv_fwd(bcx, cw):
    s = bcx.shape[0]
    tc = _CONV_COLS
    nc = D_MODEL // tc

    def body(b_ref, c_ref, xp_ref, w_ref, o_ref):
        rows = lax.broadcasted_iota(jnp.int32, (s, tc), 0)
        u = c_ref[...].astype(F32) * xp_ref[...].astype(F32)
        w = w_ref[...]
        uc = w[2:3] * u + w[1:2] * _shift_down(u, 1, rows) + w[0:1] * _shift_down(u, 2, rows)
        o_ref[...] = (b_ref[...].astype(F32) * uc).astype(BF16)

    col = lambda off: pl.BlockSpec((s, tc), lambda j: (0, off + j))
    return _pc(
        body, name="conv_fwd", grid=(nc,),
        in_specs=[col(0), col(nc), col(2 * nc), pl.BlockSpec((3, tc), lambda j: (0, j))],
        out_specs=pl.BlockSpec((s, tc), lambda j: (0, j)),
        out_shape=_sds((s, D_MODEL), BF16),
    )(bcx, bcx, bcx, cw)


def _final_loss(h, gain, target):
    s = h.shape[0]
    tm = _row_block(s)

    def body(h_ref, g_ref, t_ref, dh_ref, dg_ref, loss_ref):
        i = pl.program_id(0)

        @pl.when(i == 0)
        def _():
            dg_ref[...] = jnp.zeros_like(dg_ref)
            loss_ref[...] = jnp.zeros_like(loss_ref)

        hb = h_ref[...]
        e = _rms(hb, g_ref[...]) - t_ref[...]
        loss_ref[...] += 0.5 * jnp.sum(jnp.mean(e * e, axis=-1, keepdims=True))
        dx, dg = _rms_bwd(e * (1.0 / D_MODEL), hb, g_ref[...])
        dh_ref[...] = dx
        dg_ref[...] += dg

    row = pl.BlockSpec((tm, D_MODEL), lambda i: (i, 0))
    vec = pl.BlockSpec((1, D_MODEL), lambda i: (0, 0))
    return _pc(
        body, name="final_loss", grid=(s // tm,),
        in_specs=[row, vec, row],
        out_specs=[row, vec, pl.BlockSpec((8, D_MODEL), lambda i: (0, 0))],
        out_shape=[_sds((s, D_MODEL), F32), _sds((1, D_MODEL), F32), _sds((8, D_MODEL), F32)],
    )(h, gain, target)


def _ffn_bwd_x(dho, h, gain, gate, up, wg, wu, wd, name, carry=None):
    s = h.shape[0]
    tm = _row_block(s)
    tf = 1024
    nf = FF_PAD // tf

    def body(dho_ref, h_ref, g_ref, gate_ref, up_ref, wg_ref, wu_ref, wd_ref,
             dh_ref, dhb_ref, dhob_ref, dgate_ref, dup_ref, act_ref, dgain_ref, acc_ref):
        i = pl.program_id(0)
        j = pl.program_id(1)

        @pl.when(j == 0)
        def _():
            dhob_ref[...] = dho_ref[...].astype(BF16)
            acc_ref[...] = jnp.zeros_like(acc_ref)

        @pl.when((i == 0) & (j == 0))
        def _():
            dgain_ref[...] = jnp.zeros_like(dgain_ref)

        dact = _dot_nt(dhob_ref[...], wd_ref[...])
        g = gate_ref[...].astype(F32)
        u = up_ref[...].astype(F32)
        sg = jax.nn.sigmoid(g)
        silu = g * sg
        dg = (dact * u * (sg * (1.0 + g * (1.0 - sg)))).astype(BF16)
        du = (dact * silu).astype(BF16)
        dgate_ref[...] = dg
        dup_ref[...] = du
        act_ref[...] = (silu * u).astype(BF16)
        acc_ref[...] += _dot_nt(dg, wg_ref[...]) + _dot_nt(du, wu_ref[...])

        @pl.when(j == nf - 1)
        def _():
            dx, dgain = _rms_bwd(acc_ref[...], h_ref[...], g_ref[...])
            dh = dho_ref[...] + dx
            dh_ref[...] = dh
            dhb_ref[...] = dh.astype(BF16)
            dgain_ref[...] += dgain

    row = pl.BlockSpec((tm, D_MODEL), lambda i, j: (i, 0))
    vec = pl.BlockSpec((1, D_MODEL), lambda i, j: (0, 0))
    hid = pl.BlockSpec((tm, tf), lambda i, j: (i, j))
    wcol = pl.BlockSpec((D_MODEL, tf), lambda i, j: (0, j))
    wrow = pl.BlockSpec((tf, D_MODEL), lambda i, j: (j, 0))
    return _pc_carrying(
        body, carry, (dho, h, gain, gate, up, wg, wu, wd), name=name, grid=(s // tm, nf),
        in_specs=[row, row, vec, hid, hid, wcol, wcol, wrow],
        out_specs=[row, row, row, hid, hid, hid, vec],
        out_shape=[_sds((s, D_MODEL), F32), _sds((s, D_MODEL), BF16), _sds((s, D_MODEL), BF16),
                   _sds((s, FF_PAD), BF16), _sds((s, FF_PAD), BF16), _sds((s, FF_PAD), BF16),
                   _sds((1, D_MODEL), F32)],
        scratch_shapes=[pltpu.VMEM((tm, D_MODEL), F32)],
    )


def _nt_rmsbwd(a, w, h, gain, dho, name, carry=None):
    stacked = a.ndim == 3
    if stacked:
        nk, s, tk = a.shape
    else:
        s, tk = a.shape
        nk = 1
    tm = min(_FFN_ROWS, s) if stacked else _row_block(s)

    def body(a_ref, w_ref, h_ref, g_ref, dho_ref, dh_ref, dhb_ref, dgain_ref, acc_ref):
        i = pl.program_id(0)
        j = pl.program_id(1)
        if stacked:
            a_ref = a_ref.at[0]

        @pl.when(j == 0)
        def _():
            acc_ref[...] = jnp.zeros_like(acc_ref)

        @pl.when((i == 0) & (j == 0))
        def _():
            dgain_ref[...] = jnp.zeros_like(dgain_ref)

        acc_ref[...] += _dot_nt(a_ref[...], w_ref[...])

        @pl.when(j == nk - 1)
        def _():
            dx, dgain = _rms_bwd(acc_ref[...], h_ref[...], g_ref[...])
            dh = dho_ref[...] + dx
            dh_ref[...] = dh
            dhb_ref[...] = dh.astype(BF16)
            dgain_ref[...] += dgain

    row = pl.BlockSpec((tm, D_MODEL), lambda i, j: (i, 0))
    vec = pl.BlockSpec((1, D_MODEL), lambda i, j: (0, 0))
    return _pc_carrying(
        body, carry, (a, w, h, gain, dho), name=name, grid=(s // tm, nk),
        in_specs=[pl.BlockSpec((1, tm, tk), lambda i, j: (j, i, 0)) if stacked else pl.BlockSpec((tm, tk), lambda i, j: (i, 0)),
                  pl.BlockSpec((D_MODEL, tk), lambda i, j: (0, j)), row, vec, row],
        out_specs=[row, row, vec],
        out_shape=[_sds((s, D_MODEL), F32), _sds((s, D_MODEL), BF16), _sds((1, D_MODEL), F32)],
        scratch_shapes=[pltpu.VMEM((tm, D_MODEL), F32)],
    )


def _matmul_nt(a, w, name):
    s, kd = a.shape
    n = w.shape[0]
    tm = min(_FFN_ROWS, s)

    def body(a_ref, w_ref, o_ref):
        o_ref[...] = _dot_nt(a_ref[...], w_ref[...]).astype(BF16)

    return _pc(
        body, name=name, grid=(s // tm,),
        in_specs=[pl.BlockSpec((tm, kd), lambda i: (i, 0)), pl.BlockSpec((n, kd), lambda i: (0, 0))],
        out_specs=pl.BlockSpec((tm, n), lambda i: (i, 0)),
        out_shape=_sds((s, n), BF16),
    )(a, w)


def _matmul_tn(a, b, name, carry=None):
    s, m = a.shape
    stacked = b.ndim == 3
    tmm = min(m, 1024)
    if stacked:
        n, tn = b.shape[0] * b.shape[2], b.shape[2]
    else:
        n = b.shape[1]
        tn = n if n <= 1024 else 1024
    tk = min(_TN_ROWS, s)
    nk = s // tk

    def body(a_ref, b_ref, o_ref, acc_ref):
        k = pl.program_id(2)
        if stacked:
            b_ref = b_ref.at[0]

        @pl.when(k == 0)
        def _():
            acc_ref[...] = jnp.zeros_like(acc_ref)

        acc_ref[...] += _dot_tn(a_ref[...], b_ref[...])

        @pl.when(k == nk - 1)
        def _():
            o_ref[...] = acc_ref[...].astype(BF16)

    (out,), got = _pc_carrying(
        body, carry, (a, b), name=name, grid=(m // tmm, n // tn, nk),
        in_specs=[pl.BlockSpec((tk, tmm), lambda i, j, k: (k, i)),
                  pl.BlockSpec((1, tk, tn), lambda i, j, k: (j, k, 0)) if stacked
                  else pl.BlockSpec((tk, tn), lambda i, j, k: (k, j))],
        out_specs=[pl.BlockSpec((tmm, tn), lambda i, j, k: (i, j))],
        out_shape=[_sds((m, n), BF16)],
        scratch_shapes=[pltpu.VMEM((tmm, tn), F32)],
    )
    return out if carry is None else (out, got)


def _conv_bwd(dbuc, bcx, cw):
    s = bcx.shape[0]
    tc = _CONV_COLS
    nc = D_MODEL // tc

    def body(d_ref, b_ref, c_ref, xp_ref, w_ref, dbcx_ref, dw_ref):
        rows = lax.broadcasted_iota(jnp.int32, (s, tc), 0)
        c = c_ref[...].astype(F32)
        xp = xp_ref[...].astype(F32)
        u = c * xp
        u1 = _shift_down(u, 1, rows)
        u2 = _shift_down(u, 2, rows)
        w = w_ref[...]
        uc = w[2:3] * u + w[1:2] * u1 + w[0:1] * u2
        d = d_ref[...].astype(F32)
        dbcx_ref[0] = (d * uc).astype(BF16)
        duc = d * b_ref[...].astype(F32)
        du = w[2:3] * duc + w[1:2] * _shift_up(duc, 1, rows, s) + w[0:1] * _shift_up(duc, 2, rows, s)
        dbcx_ref[1] = (du * xp).astype(BF16)
        dbcx_ref[2] = (du * c).astype(BF16)
        dw_ref[0:1, :] = jnp.sum(duc * u2, axis=0, keepdims=True)
        dw_ref[1:2, :] = jnp.sum(duc * u1, axis=0, keepdims=True)
        dw_ref[2:3, :] = jnp.sum(duc * u, axis=0, keepdims=True)

    col = lambda off: pl.BlockSpec((s, tc), lambda j: (0, off + j))
    tap = pl.BlockSpec((3, tc), lambda j: (0, j))
    return _pc(
        body, name="conv_bwd", grid=(nc,),
        in_specs=[col(0), col(0), col(nc), col(2 * nc), tap],
        out_specs=[pl.BlockSpec((3, s, tc), lambda j: (0, 0, j)), tap],
        out_shape=[_sds((3, s, D_MODEL), BF16), _sds((3, D_MODEL), F32)],
    )(dbuc, bcx, bcx, bcx, cw)


def _attn_bwd(q, k, v, do, o, lse, cc, sn, sp, carry=None):
    s = q.shape[1]
    bk = _row_block(s)
    nb = s // bk

    def body(q_ref, k_ref, v_ref, do_ref, o_ref, lse_ref, cc_ref, sn_ref, sp_ref,
             dq_ref, dk_ref, dv_ref, dqacc_ref, delta_ref, bias_ref):
        j = pl.program_id(1)
        kb = k_ref[0]
        vb = v_ref[0]

        @pl.when((pl.program_id(0) == 0) & (j == 0))
        def _():
            bias_ref[...] = _chunk_bias(2 * bk, bk)

        @pl.when(j == 0)
        def _():
            dqacc_ref[...] = jnp.zeros_like(dqacc_ref)

            def fill(i, _):
                rows = pl.ds(pl.multiple_of(i * bk, bk), bk)
                d = jnp.sum(do_ref[rows, :].astype(F32) * o_ref[rows, :].astype(F32), axis=-1, keepdims=True)
                delta_ref[rows, :] = jnp.broadcast_to(d, (bk, 128))
                return 0

            lax.fori_loop(0, nb, fill, 0)

        def step(i, carry, masked, blocks=1):
            dk, dv = carry
            rows = pl.ds(pl.multiple_of(i * bk, bk), blocks * bk)
            qb = q_ref[0, rows, :]
            dob = do_ref[rows, :]
            sc = _dot_nt(qb, kb)
            if masked:
                sc = sc + bias_ref[:blocks * bk, :]
            p = jnp.exp2(sc - lse_ref[0, rows, :][:, :1])
            dv = dv + _dot_tn(p.astype(BF16), dob)
            ds = (p * (_dot_nt(dob, vb) - delta_ref[rows, :][:, :1])).astype(BF16)
            dk = dk + _dot_tn(ds, qb)
            dqacc_ref[rows, :] += _dot(ds, kb)
            return dk, dv

        rest = nb - 1 - j
        odd = lax.rem(rest, 2)
        carry = (jnp.zeros((bk, HEAD_PAD), F32), jnp.zeros((bk, V_DIM), F32))
        carry = lax.fori_loop(0, 1 - odd, lambda t, c: step(j, c, True), carry)
        if nb > 1:
            carry = lax.fori_loop(0, odd, lambda t, c: step(j, c, True, blocks=2), carry)
        dq = dqacc_ref[pl.ds(pl.multiple_of(j * bk, bk), bk), :] * ATT_SCALE
        dq_ref[:, :NOPE] = dq[:, :NOPE].astype(BF16)
        dq_ref[:, NOPE:] = _rope_bwd(dq[:, NOPE:], cc_ref[...], sn_ref[...], sp_ref[...]).astype(BF16)
        if nb > 1:
            carry = lax.fori_loop(
                0, rest // 2, lambda t, c: step(j + 1 + odd + 2 * t, c, False, blocks=2), carry)
        dk, dv = carry
        dk_ref[0] = (dk * LN_2).astype(BF16)
        dv_ref[0] = dv.astype(BF16)

    blk = lambda n: pl.BlockSpec((1, bk, n), lambda h, j: (h, j, 0))
    whole = lambda n: pl.BlockSpec((1, s, n), lambda h, j: (h, 0, 0))
    cols = pl.BlockSpec((s, V_DIM), lambda h, j: (0, h))
    tab = pl.BlockSpec((bk, 128), lambda h, j: (j, 0))
    return _pc_carrying(
        body, carry, (q, k, v, do, o, lse, cc, sn, sp), name="attn_bwd", grid=(N_HEADS, nb),
        in_specs=[whole(HEAD_PAD), blk(HEAD_PAD), blk(V_DIM), cols, cols, whole(128), tab, tab, tab],
        out_specs=[pl.BlockSpec((bk, HEAD_PAD), lambda h, j: (j, h)), blk(HEAD_PAD), blk(V_DIM)],
        out_shape=[_sds((s, N_HEADS * HEAD_PAD), BF16), _sds((N_HEADS, s, HEAD_PAD), BF16),
                   _sds((N_HEADS, s, V_DIM), BF16)],
        scratch_shapes=[pltpu.VMEM((s, HEAD_PAD), F32), pltpu.VMEM((s, 128), F32), pltpu.VMEM((2 * bk, bk), F32)],
    )


def _mla_mid_bwd(dq, dk, dv, proj, w_uq, w_ukv, g_cq, g_ckv, cc, sn, sp, carry=None):
    s = dq.shape[0]
    tm = _row_block(s)

    def body(dq_ref, dk_ref, dv_ref, proj_ref, wuq_ref, wukv_ref, gcq_ref, gckv_ref, cc_ref, sn_ref, sp_ref,
             dproj_ref, dkv_ref, dgcq_ref, dgckv_ref, acq_ref, ackv_ref, akr_ref):
        @pl.when(pl.program_id(0) == 0)
        def _():
            dgcq_ref[...] = jnp.zeros_like(dgcq_ref)
            dgckv_ref[...] = jnp.zeros_like(dgckv_ref)

        for h in range(N_HEADS):
            cols = slice(h * HEAD_PAD, (h + 1) * HEAD_PAD)
            dkb = dk_ref[h]
            dkv_ref[:, h * HEAD_PAD:h * HEAD_PAD + NOPE] = dkb[:, :NOPE]
            dkv_ref[:, h * HEAD_PAD + NOPE:(h + 1) * HEAD_PAD] = dv_ref[h]
            cq_part = _dot_nt(dq_ref[:, cols], wuq_ref[h])
            ckv_part = _dot_nt(dkv_ref[:, cols], wukv_ref[h])
            kr_part = dkb[:, NOPE:].astype(F32)
            if h == 0:
                acq_ref[...], ackv_ref[...], akr_ref[...] = cq_part, ckv_part, kr_part
            else:
                acq_ref[...] += cq_part
                ackv_ref[...] += ckv_part
                akr_ref[...] += kr_part

        proj = proj_ref[...]
        dcq, dgcq = _rms_bwd(acq_ref[...], proj[:, :CQ], gcq_ref[...])
        dckv, dgckv = _rms_bwd(ackv_ref[...], proj[:, CQ:CQ + CKV], gckv_ref[...])
        dproj_ref[:, :CQ] = dcq.astype(BF16)
        dproj_ref[:, CQ:CQ + CKV] = dckv.astype(BF16)
        dproj_ref[:, CQ + CKV:] = _rope_bwd(akr_ref[...], cc_ref[...], sn_ref[...], sp_ref[...]).astype(BF16)
        dgcq_ref[...] += dgcq
        dgckv_ref[...] += dgckv

    head_blk = lambda n: pl.BlockSpec((N_HEADS, tm, n), lambda i: (0, i, 0))
    head_cols = pl.BlockSpec((tm, N_HEADS * HEAD_PAD), lambda i: (i, 0))
    head_w = lambda a: pl.BlockSpec(a.shape, lambda i: (0, 0, 0))
    row = lambda n: pl.BlockSpec((tm, n), lambda i: (i, 0))
    vec = lambda n: pl.BlockSpec((1, n), lambda i: (0, 0))
    return _pc_carrying(
        body, carry, (dq, dk, dv, proj, w_uq, w_ukv, g_cq, g_ckv, cc, sn, sp),
        name="mla_mid_bwd", grid=(s // tm,),
        in_specs=[head_cols, head_blk(HEAD_PAD), head_blk(V_DIM), row(PROJ_PAD), head_w(w_uq), head_w(w_ukv),
                  vec(CQ), vec(CKV), row(128), row(128), row(128)],
        out_specs=[row(PROJ_PAD), head_cols, vec(CQ), vec(CKV)],
        out_shape=[_sds((s, PROJ_PAD), BF16), _sds((s, N_HEADS * HEAD_PAD), BF16), _sds((1, CQ), F32),
                   _sds((1, CKV), F32)],
        scratch_shapes=[pltpu.VMEM((tm, CQ), F32), pltpu.VMEM((tm, CKV), F32), pltpu.VMEM((tm, 128), F32)],
    )


def _peer(k):
    x, y, c = lax.axis_index("x"), lax.axis_index("y"), lax.axis_index("c")
    px = 1 - x if k & 4 else x
    py = 1 - y if k & 2 else y
    pc = 1 - c if k & 1 else c
    return (px, py, pc), 4 * px + 2 * py + pc


def _exchange_ops(arrays, scatter):
    nw = len(arrays)
    by_cols = [isinstance(a, _Cols) for a in arrays]
    arrays = [a.array if isinstance(a, _Cols) else a for a in arrays]
    direct = range(1, N_DEV) if scatter else (1, 2, 4, 6)

    def columns(ref, idx, width):
        return ref.at[:, pl.ds(pl.multiple_of(idx * width, 128), width)]

    def sent(ins, w, idx):
        if not scatter:
            return ins[w]
        return columns(ins[w], idx, arrays[w].shape[1] // N_DEV) if by_cols[w] else ins[w].at[idx]

    def slot(outs, w, idx):
        if by_cols[w] and not scatter:
            return columns(outs[w], idx, arrays[w].shape[1])
        return outs[w].at[idx]

    def copy(w, k, src, dst, to, send_sems, recv_sems):
        return pltpu.make_async_remote_copy(
            src_ref=src, dst_ref=dst, send_sem=send_sems.at[w * N_DEV + k], recv_sem=recv_sems.at[w * N_DEV + k],
            device_id=to, device_id_type=pl.DeviceIdType.MESH)

    def own_copies(ins, outs, local_sems):
        _, me = _peer(0)
        return [pltpu.make_async_copy(sent(ins, w, me), slot(outs, w, me), local_sems.at[w]) for w in range(nw)]

    def sends(ins, outs, send_sems, recv_sems):
        _, me = _peer(0)
        out = []
        for k in direct:
            dev, idx = _peer(k)
            for w in range(nw):
                out.append(copy(w, k, sent(ins, w, idx), slot(outs, w, me), dev, send_sems, recv_sems))
        return out

    def relays(outs, send_sems, recv_sems):
        sibling, _ = _peer(1)
        out = []
        for k in (2, 4, 6):
            _, idx = _peer(k)
            for w in range(nw):
                out.append(copy(w, k + 1, slot(outs, w, idx), slot(outs, w, idx), sibling, send_sems, recv_sems))
        return out

    def arrival(outs, w, k, send_sems, recv_sems):
        dev, idx = _peer(k)
        return copy(w, k, slot(outs, w, idx), slot(outs, w, idx), dev, send_sems, recv_sems)

    def start(ins, outs, send_sems, recv_sems, local_sems):
        for cp in own_copies(ins, outs, local_sems) + sends(ins, outs, send_sems, recv_sems):
            cp.start()

    def relay(ins, outs, send_sems, recv_sems, local_sems):
        for k in (2, 4, 6):
            for w in range(nw):
                arrival(outs, w, k, send_sems, recv_sems).wait_recv()
        for cp in relays(outs, send_sems, recv_sems):
            cp.start()

    def wait(ins, outs, send_sems, recv_sems, local_sems):
        for cp in own_copies(ins, outs, local_sems):
            cp.wait()
        for cp in sends(ins, outs, send_sems, recv_sems) + ([] if scatter else relays(outs, send_sems, recv_sems)):
            cp.wait_send()
        for k in (range(1, N_DEV) if scatter else (1, 3, 5, 7)):
            for w in range(nw):
                arrival(outs, w, k, send_sems, recv_sems).wait_recv()

    def landed_shape(a, cols):
        if not cols:
            return a.shape if scatter else (N_DEV,) + a.shape
        r, c = a.shape
        return (N_DEV, r, c // N_DEV) if scatter else (r, N_DEV * c)

    landed = [_sds(landed_shape(a, cols), a.dtype) for a, cols in zip(arrays, by_cols)]
    sems = [pltpu.SemaphoreType.DMA((nw * N_DEV,)), pltpu.SemaphoreType.DMA((nw * N_DEV,)),
            pltpu.SemaphoreType.DMA((nw,))]
    return start, (None if scatter else relay), wait, landed, sems, arrays


class _Cols:
    def __init__(self, array):
        self.array = array


def _exchange(arrays, scatter, name):
    nw = len(arrays)
    start, relay, wait, landed, sems, arrays = _exchange_ops(arrays, scatter)

    def body(*refs):
        ins, outs, csem = refs[:nw], refs[nw:2 * nw], refs[2 * nw:]
        start(ins, outs, *csem)
        if relay is not None:
            relay(ins, outs, *csem)
        wait(ins, outs, *csem)

    any_spec = pl.BlockSpec(memory_space=pl.ANY)
    return _pc(body, name=name, in_specs=[any_spec] * nw, out_specs=[any_spec] * nw, out_shape=landed,
               scratch_shapes=sems)(*arrays)


def _adam_math(g, w, m, v):
    m = ADAM_B1 * m + (1.0 - ADAM_B1) * g
    v = ADAM_B2 * v + (1.0 - ADAM_B2) * jnp.square(g)
    m_hat = m / (1.0 - ADAM_B1 ** ADAM_STEP)
    v_hat = v / (1.0 - ADAM_B2 ** ADAM_STEP)
    delta = -ADAM_LR * (m_hat / (jnp.sqrt(v_hat) + ADAM_EPS) + ADAM_WD * w)
    return delta, m, v


def _adam_rows(r):
    for t in range(min(r, 512) // 16 * 16, 0, -16):
        if r % t == 0:
            return t
    return r


def _adamw(parts, w, m, v, name):
    nl, r, c = w.shape
    tr = _adam_rows(r)
    nr = r // tr

    def body(*refs):
        p_refs, (w_ref, m_ref, v_ref, g_ref, d_ref, mo_ref, vo_ref) = refs[:nl], refs[nl:]
        for layer in range(nl):
            @pl.when(pl.program_id(0) == layer)
            def _(p_ref=p_refs[layer]):
                g = p_ref[0].astype(F32)
                for src in range(1, N_DEV):
                    g = g + p_ref[src].astype(F32)
                delta, m2, v2 = _adam_math(g, w_ref[0], m_ref[0], v_ref[0])
                g_ref[0] = g
                d_ref[0] = delta
                mo_ref[0] = m2
                vo_ref[0] = v2

    def part_spec(layer):
        def index(l, i):
            return 0, jnp.where(l == layer, i, jnp.where(l < layer, 0, nr - 1)), 0
        return pl.BlockSpec((N_DEV, tr, c), index)

    blk = pl.BlockSpec((1, tr, c), lambda l, i: (l, i, 0))
    return _pc(
        body, name=name, grid=(nl, nr),
        in_specs=[part_spec(layer) for layer in range(nl)] + [blk, blk, blk],
        out_specs=[blk, blk, blk, blk],
        out_shape=[_sds((nl, r, c), F32)] * 4,
    )(*parts, w, m, v)


def _small_allreduce_adamw(gpack, wpack, mpack, vpack, last_grads):
    shape = gpack.shape
    compact = (SMALL_ROWS, D_MODEL)
    nw = len(last_grads)
    start, _, wait, landed, exchange_sems, last_grads = _exchange_ops(last_grads, True)

    def body(*refs):
        (g_ref, w_ref, m_ref, v_ref), rest = refs[:4], refs[4:]
        cin, rest = rest[:nw], rest[nw:]
        (go_ref, d_ref, mo_ref, vo_ref), rest = rest[:4], rest[4:]
        cout, rest = rest[:nw], rest[nw:]
        (comp_ref, gath_ref, send_sems, recv_sems), csem = rest[:4], rest[4:]
        start(cin, cout, *csem)
        _, me = _peer(0)
        comp_ref[...] = jnp.zeros(compact, F32)
        r = 0
        for p, n in enumerate(SMALL_PIECES):
            comp_ref[r:r + n, :] = g_ref[8 * p:8 * p + n, :]
            r += n
        gath_ref[me] = comp_ref[...]
        copies = []
        for k in range(1, N_DEV):
            dev, idx = _peer(k)
            copies.append(pltpu.make_async_remote_copy(
                src_ref=comp_ref, dst_ref=gath_ref.at[me], send_sem=send_sems.at[k], recv_sem=recv_sems.at[k],
                device_id=dev, device_id_type=pl.DeviceIdType.MESH))
        for cp in copies:
            cp.start()
        for cp in copies:
            cp.wait_send()
        for k in range(1, N_DEV):
            dev, idx = _peer(k)
            pltpu.make_async_remote_copy(
                src_ref=comp_ref, dst_ref=gath_ref.at[idx], send_sem=send_sems.at[k], recv_sem=recv_sems.at[k],
                device_id=dev, device_id_type=pl.DeviceIdType.MESH).wait_recv()
        g = gath_ref[0]
        for src in range(1, N_DEV):
            g = g + gath_ref[src]
        go_ref[...] = jnp.zeros(shape, F32)
        r = 0
        for p, n in enumerate(SMALL_PIECES):
            go_ref[8 * p:8 * p + n, :] = g[r:r + n, :]
            r += n
        delta, m2, v2 = _adam_math(go_ref[...], w_ref[...], m_ref[...], v_ref[...])
        d_ref[...] = delta
        mo_ref[...] = m2
        vo_ref[...] = v2
        wait(cin, cout, *csem)

    vm = pl.BlockSpec(memory_space=pltpu.VMEM)
    any_spec = pl.BlockSpec(memory_space=pl.ANY)
    res = _pc(
        body, name="small_allreduce_adamw",
        in_specs=[vm] * 4 + [any_spec] * nw, out_specs=[vm] * 4 + [any_spec] * nw,
        out_shape=[_sds(shape, F32)] * 4 + landed,
        scratch_shapes=[pltpu.VMEM(compact, F32), pltpu.VMEM((N_DEV,) + compact, F32),
                        pltpu.SemaphoreType.DMA((N_DEV,)), pltpu.SemaphoreType.DMA((N_DEV,))] + exchange_sems,
    )(gpack, wpack, mpack, vpack, *last_grads)
    return res[:4], res[4:]


def _rows_from_shards(g):
    return g.reshape(N_DEV * g.shape[1], g.shape[2])


def _shards_from_rows(a):
    return a.reshape(N_DEV, a.shape[0] // N_DEV, a.shape[1])


def _pad_to(a, rows, cols):
    return jnp.pad(a, ((0, rows - a.shape[0]), (0, cols - a.shape[1])))


def _rope_tables(pos):
    inv_freq = 1.0 / (ROPE_THETA ** (jnp.arange(0, ROPE, 2, dtype=F32) / ROPE))
    ang = pos.astype(F32)[:, None] * inv_freq
    cos, sin = jnp.cos(ang), jnp.sin(ang)
    z32, z64, z96 = (jnp.zeros((pos.shape[0], n), F32) for n in (32, 64, 96))
    return (jnp.concatenate([cos, cos, z64], axis=1), jnp.concatenate([-sin, z96], axis=1),
            jnp.concatenate([z32, sin, z64], axis=1))


def _pad_row(vec):
    vec = vec.reshape(1, -1)
    return jnp.pad(vec, ((0, 0), (0, D_MODEL - vec.shape[1])))


def _forward_backward(x, target, tables, gains, shards):
    cc, sn, sp = tables
    ffn_g = gains["ffn_norm"]
    by_cols = ("wg0", "wu0", "wg1", "wu1", "c_in")

    def gather(names, relay_at):
        return [_Cols(shards[n]) if n in by_cols else shards[n] for n in names], False, relay_at

    (got,) = _exchange([shards["w_in"]], False, "gather_w_in")
    w_in = _rows_from_shards(got)
    (n0, proj, cqn, ckvn, krr), (w_uq, w_ukv) = _mla_in_fwd(
        x, gains["mla_norm"], w_in, gains["g_cq"], gains["g_ckv"], cc, sn, sp, carry=gather(["w_uq", "w_ukv"], 0.5))
    (q, k, v), (w_o, conv_norm, conv_w) = _qkv_proj(
        cqn, ckvn, krr, w_uq, w_ukv, cc, sn, sp, carry=gather(["w_o", "conv_norm", "conv_w"], 0.5))
    w_o = _rows_from_shards(w_o)
    conv_norm = conv_norm.reshape(1, D_MODEL)
    conv_w = jnp.transpose(conv_w, (1, 0, 2)).reshape(3, D_MODEL)
    (o, lse), (wg0, wu0, wd0, c_in, c_out, wd1) = _attn_fwd(
        q, k, v, carry=gather(["wg0", "wu0", "wd0", "c_in", "c_out", "wd1"], 0.75))
    wd0 = _rows_from_shards(wd0)
    wd1 = _rows_from_shards(wd1)
    c_out = _rows_from_shards(c_out)
    h1 = _matmul_res(o, w_o, x, "mla_out_fwd")
    (h2, n1, gate0, up0), (wg1, wu1) = _ffn_fwd(
        h1, ffn_g[0:1], wg0, wu0, wd0, "ffn0_fwd", carry=gather(["wg1", "wu1"], 0.7))
    n2, bcx = _rms_matmul(h2, conv_norm, c_in, "conv_in_fwd")
    bu = _conv_fwd(bcx, conv_w)
    h3 = _matmul_res(bu, c_out, h2, "conv_out_fwd")
    (h4, n3, gate1, up1), _ = _ffn_fwd(h3, ffn_g[1:2], wg1, wu1, wd1, "ffn1_fwd")
    dh4, d_final, loss = _final_loss(h4, gains["final_norm"], target)

    small = {"final_norm": d_final}
    parts = {}

    def scatter(**blocks):
        return list(blocks), (list(blocks.values()), True, None)

    (dh3, dh3_b, dh4_b, dgate, dup, act, d_ffn1), _ = _ffn_bwd_x(
        dh4, h3, ffn_g[1:2], gate1, up1, wg1, wu1, wd1, "ffn1_bwd")
    dwg1 = _matmul_tn(dgate, n3, "ffn1_dwg")
    dwu1 = _matmul_tn(dup, n3, "ffn1_dwu")
    dwd1 = _matmul_tn(act, dh4_b, "ffn1_dwd")
    dbuc = _matmul_nt(dh3_b, c_out, "conv_out_bwd")
    d_c_out = _matmul_tn(bu, dh3_b, "conv_dwout")
    dbcx, small["conv_w"] = _conv_bwd(dbuc, bcx, conv_w)
    d_c_in = _matmul_tn(n2, dbcx, "conv_dwin")
    names, carry = scatter(c_out=_shards_from_rows(d_c_out))
    (dh2, dh2_b, small["conv_norm"]), got = _nt_rmsbwd(
        dbcx, c_in, h2, conv_norm, dh3, "conv_in_bwd", carry=carry)
    parts.update(zip(names, got))
    names, carry = scatter(c_in=_Cols(d_c_in), wd1=_shards_from_rows(dwd1))
    (dh1, dh1_b, dh2_b2, dgate, dup, act, d_ffn0), got = _ffn_bwd_x(
        dh2, h1, ffn_g[0:1], gate0, up0, wg0, wu0, wd0, "ffn0_bwd", carry=carry)
    parts.update(zip(names, got))
    dwg0 = _matmul_tn(dgate, n1, "ffn0_dwg")
    dwu0 = _matmul_tn(dup, n1, "ffn0_dwu")
    dwd0 = _matmul_tn(act, dh2_b2, "ffn0_dwd")
    small["ffn_norm"] = jnp.pad(d_ffn0, ((0, 7), (0, 0))) + jnp.pad(d_ffn1, ((1, 6), (0, 0)))
    do = _matmul_nt(dh1_b, w_o, "mla_out_bwd")
    d_w_o = _matmul_tn(o, dh1_b, "mla_dwo")
    names, carry = scatter(**{n: _shards_from_rows(d) for n, d in dict(
        wg0=dwg0, wg1=dwg1, wu0=dwu0, wu1=dwu1, wd0=dwd0).items()})
    (dq, dk, dv), got = _attn_bwd(q, k, v, do, o, lse, cc, sn, sp, carry=carry)
    parts.update(zip(names, got))
    d_w_uq = _matmul_tn(dq, cqn, "mla_dwuq")
    names, carry = scatter(w_o=_shards_from_rows(d_w_o), w_uq=_shards_from_rows(d_w_uq))
    (dproj, dkv, small["g_cq"], small["g_ckv"]), got = _mla_mid_bwd(
        dq, dk, dv, proj, w_uq, w_ukv, gains["g_cq"], gains["g_ckv"], cc, sn, sp, carry=carry)
    parts.update(zip(names, got))
    d_w_ukv = _matmul_tn(ckvn, dkv, "mla_dwukv")
    names, carry = scatter(w_ukv=_Cols(d_w_ukv))
    d_w_in, got = _matmul_tn(dproj, n0, "mla_dwin", carry=carry)
    parts.update(zip(names, got))
    (dx, _, small["mla_norm"]), _ = _nt_rmsbwd(dproj, w_in, x, gains["mla_norm"], dh1, "mla_in_bwd")
    return loss, dx, parts, _Cols(d_w_in), small


def kernel(x, positions, mla_norm, mla_w_in, mla_g_cq, mla_g_ckv, mla_w_uq, mla_w_ukv, mla_w_o, conv_norm, conv_w_in, conv_w, conv_w_out, ffn_norm, ffn_w_gate, ffn_w_up, ffn_w_down, final_norm, loss_target, m_mla_norm, m_mla_w_in, m_mla_g_cq, m_mla_g_ckv, m_mla_w_uq, m_mla_w_ukv, m_mla_w_o, m_conv_norm, m_conv_w_in, m_conv_w, m_conv_w_out, m_ffn_norm, m_ffn_w_gate, m_ffn_w_up, m_ffn_w_down, m_final_norm, v_mla_norm, v_mla_w_in, v_mla_g_cq, v_mla_g_ckv, v_mla_w_uq, v_mla_w_ukv, v_mla_w_o, v_conv_norm, v_conv_w_in, v_conv_w, v_conv_w_out, v_ffn_norm, v_ffn_w_gate, v_ffn_w_up, v_ffn_w_down, v_final_norm):
    me = 4 * lax.axis_index("x") + 2 * lax.axis_index("y") + lax.axis_index("c")

    bf = lambda a, rows, cols: _pad_to(a.astype(BF16), rows, cols)
    shards = dict(
        w_in=bf(mla_w_in[0], D_MODEL // N_DEV, PROJ_PAD), w_uq=bf(mla_w_uq[0], CQ, HEAD_PAD),
        w_ukv=mla_w_ukv[0].astype(BF16), w_o=mla_w_o[0].astype(BF16),
        c_in=conv_w_in[0].astype(BF16), c_out=conv_w_out[0].astype(BF16),
        conv_norm=conv_norm, conv_w=conv_w[0])
    for l in range(2):
        shards.update({f"wg{l}": bf(ffn_w_gate[l], D_MODEL, FF_SHARD_PAD), f"wu{l}": bf(ffn_w_up[l], D_MODEL, FF_SHARD_PAD),
                       f"wd{l}": bf(ffn_w_down[l], FF_SHARD_PAD, D_MODEL)})
    gains = dict(mla_norm=mla_norm, g_cq=mla_g_cq, g_ckv=mla_g_ckv, ffn_norm=ffn_norm,
                 final_norm=final_norm.reshape(1, -1))
    loss_local, dx, parts, d_w_in, grads = _forward_backward(
        x[0], loss_target[0], _rope_tables(positions[0]), gains, shards)

    col0 = me * (D_MODEL // N_DEV)

    def place(shard):
        return lax.dynamic_update_slice(jnp.zeros((shard.shape[0], D_MODEL), F32), shard, (0, col0))

    no_loss = jnp.zeros((8, D_MODEL), F32)

    def pack(mla_n, g_cq, g_ckv, ffn_n, fin_n, conv_n, conv_taps, loss_tile):
        rows = [mla_n, _pad_row(g_cq), _pad_row(g_ckv), ffn_n, fin_n.reshape(1, -1), conv_n, conv_taps, loss_tile]
        assert all(r.shape[0] in (n, 8) for r, n in zip(rows, SMALL_PIECES))
        return jnp.concatenate([jnp.pad(r, ((0, 8 - r.shape[0]), (0, 0))) for r in rows], axis=0)

    gpack = pack(grads["mla_norm"], grads["g_cq"], grads["g_ckv"], grads["ffn_norm"], grads["final_norm"],
                 grads["conv_norm"], grads["conv_w"], loss_local)
    wpack = pack(mla_norm, mla_g_cq, mla_g_ckv, ffn_norm, final_norm, place(conv_norm), place(conv_w[0]), no_loss)
    mpack = pack(m_mla_norm, m_mla_g_cq, m_mla_g_ckv, m_ffn_norm, m_final_norm, place(m_conv_norm),
                 place(m_conv_w[0]), no_loss)
    vpack = pack(v_mla_norm, v_mla_g_cq, v_mla_g_ckv, v_ffn_norm, v_final_norm, place(v_conv_norm),
                 place(v_conv_w[0]), no_loss)
    small, (parts["w_in"],) = _small_allreduce_adamw(gpack, wpack, mpack, vpack, [d_w_in])
    loss = small[0][56, 0]

    def adamw(name, w, m, v, partials, transposed=False):
        if transposed:
            w, m, v = (jnp.swapaxes(a, 1, 2) for a in (w, m, v))
        outs = _adamw(partials, w, m, v, "adamw_" + name)
        return [jnp.swapaxes(o, 1, 2) for o in outs] if transposed else outs

    res = dict(
        w_in=adamw("w_in", mla_w_in, m_mla_w_in, v_mla_w_in, [parts["w_in"]], True),
        w_uq=adamw("w_uq", mla_w_uq, m_mla_w_uq, v_mla_w_uq, [parts["w_uq"]], True),
        w_ukv=adamw("w_ukv", mla_w_ukv, m_mla_w_ukv, v_mla_w_ukv, [parts["w_ukv"]]),
        w_o=adamw("w_o", mla_w_o, m_mla_w_o, v_mla_w_o, [parts["w_o"]]),
        c_in=adamw("c_in", conv_w_in, m_conv_w_in, v_conv_w_in, [parts["c_in"]]),
        c_out=adamw("c_out", conv_w_out, m_conv_w_out, v_conv_w_out, [parts["c_out"]]),
        wg=adamw("wg", ffn_w_gate, m_ffn_w_gate, v_ffn_w_gate, [parts["wg0"], parts["wg1"]], True),
        wu=adamw("wu", ffn_w_up, m_ffn_w_up, v_ffn_w_up, [parts["wu0"], parts["wu1"]], True),
        wd=adamw("wd", ffn_w_down, m_ffn_w_down, v_ffn_w_down, [parts["wd0"], parts["wd1"]]),
    )

    def unpack(p):
        own = lambda rows: lax.dynamic_slice(rows, (0, col0), (rows.shape[0], D_MODEL // N_DEV))
        return dict(mla_norm=p[0:1], g_cq=p[8:9, :CQ], g_ckv=p[16:17, :CKV], ffn_norm=p[24:26], final_norm=p[32],
                    conv_norm=own(p[40:41]), conv_w=own(p[48:51])[None])

    small = [unpack(p) for p in small]
    order = ["mla_norm", "w_in", "g_cq", "g_ckv", "w_uq", "w_ukv", "w_o", "conv_norm", "c_in", "conv_w", "c_out",
             "ffn_norm", "wg", "wu", "wd", "final_norm"]
    out = [loss, dx[None]]
    for kind in range(4):
        for n in order:
            out.append(res[n][kind] if n in res else small[kind][n])
    return tuple(out)
```

```python
import math

import jax
import jax.numpy as jnp
from jax import lax
from jax.experimental import pallas as pl
from jax.experimental.pallas import tpu as pltpu

F32 = jnp.float32
BF16 = jnp.bfloat16

N_DEV = 8
D_MODEL = 1024
N_HEADS = 8
NOPE = 128
ROPE = 64
V_DIM = 128
HEAD_PAD = 256
CQ = 512
CKV = 256
PROJ_PAD = CQ + CKV + 128
FF_SHARD = 352
FF_SHARD_PAD = 384
FF_PAD = FF_SHARD_PAD * N_DEV
CHUNK_SHIFT = 6
RMS_EPS = 1e-6
ROPE_THETA = 10000.0
ATT_SCALE = 1.0 / math.sqrt(NOPE + ROPE)
LOG2_E = math.log2(math.e)
LN_2 = math.log(2.0)
Q_SCALE = ATT_SCALE * LOG2_E
NEG = -1e30

ADAM_LR = 0.001
ADAM_B1 = 0.9
ADAM_B2 = 0.999
ADAM_EPS = 1e-08
ADAM_WD = 0.01
ADAM_STEP = 10

SMALL_PIECES = (1, 1, 1, 2, 1, 1, 3, 1)
SMALL_ROWS = 16

_NT = (((1,), (1,)), ((), ()))
_TN = (((0,), (0,)), ((), ()))


def _pc(body, *, name, out_shape, grid=(), in_specs=None, out_specs=None, scratch_shapes=()):
    kwargs = dict(
        name=name, out_shape=out_shape, grid=grid, scratch_shapes=scratch_shapes,
        compiler_params=pltpu.CompilerParams(),
    )
    if in_specs is not None:
        kwargs["in_specs"] = in_specs
    if out_specs is not None:
        kwargs["out_specs"] = out_specs
    return pl.pallas_call(body, **kwargs)


def _pc_carrying(body, carry, operands, *, name, out_shape, grid, in_specs, out_specs, scratch_shapes=()):
    if carry is None:
        return _pc(body, name=name, out_shape=out_shape, grid=grid, in_specs=in_specs, out_specs=out_specs,
                   scratch_shapes=scratch_shapes)(*operands), None
    arrays, scatter, relay_at = carry
    nw, n_in, n_out, n_scr = len(arrays), len(in_specs), len(out_shape), len(scratch_shapes)
    start, relay, wait, landed_shapes, sems, arrays = _exchange_ops(arrays, scatter)
    n_steps = math.prod(grid)
    relay_step = None if relay is None else min(int(relay_at * n_steps), n_steps - 1)

    def wrapped(*refs):
        ins, rest = refs[:n_in], refs[n_in:]
        cin, rest = rest[:nw], rest[nw:]
        outs, rest = rest[:n_out], rest[n_out:]
        cout, rest = rest[:nw], rest[nw:]
        scr, csem = rest[:n_scr], rest[n_scr:]
        step = pl.program_id(0)
        for a in range(1, len(grid)):
            step = step * grid[a] + pl.program_id(a)

        @pl.when(step == 0)
        def _():
            start(cin, cout, *csem)

        if relay is not None:
            @pl.when(step == relay_step)
            def _():
                relay(cin, cout, *csem)

        body(*ins, *outs, *scr)

        @pl.when(step == n_steps - 1)
        def _():
            wait(cin, cout, *csem)

    any_spec = pl.BlockSpec(memory_space=pl.ANY)
    res = _pc(
        wrapped, name=name, grid=grid,
        in_specs=list(in_specs) + [any_spec] * nw, out_specs=list(out_specs) + [any_spec] * nw,
        out_shape=list(out_shape) + landed_shapes, scratch_shapes=list(scratch_shapes) + sems,
    )(*operands, *arrays)
    return res[:n_out], res[n_out:]


def _sds(shape, dtype):
    return jax.ShapeDtypeStruct(shape, dtype)


def _dot(a, b):
    return jnp.dot(a, b, preferred_element_type=F32)


def _dot_nt(a, b):
    return lax.dot_general(a, b, _NT, preferred_element_type=F32)


def _dot_tn(a, b):
    return lax.dot_general(a, b, _TN, preferred_element_type=F32)


def _rstd(x):
    return lax.rsqrt(jnp.mean(x * x, axis=-1, keepdims=True) + RMS_EPS)


def _rms(x, g):
    return (x * _rstd(x)) * g


def _rms_bwd(dy, x, g):
    r = _rstd(x)
    xhat = x * r
    dxhat = dy * g
    dx = r * (dxhat - xhat * jnp.mean(dxhat * xhat, axis=-1, keepdims=True))
    return dx, jnp.sum(dy * xhat, axis=0, keepdims=True)


def _rope(t, cc, sn, sp):
    return t * cc + pltpu.roll(t, 96, 1) * sn + pltpu.roll(t, 32, 1) * sp


def _rope_bwd(dt, cc, sn, sp):
    return dt * cc + pltpu.roll(dt * sn, 32, 1) + pltpu.roll(dt * sp, 96, 1)


def _row_block(s):
    return min(512, s)


_TN_ROWS = 2048
_FFN_ROWS = 1024


def _mla_in_fwd(x, g0, w_in, g_cq, g_ckv, cc, sn, sp, carry=None):
    s = x.shape[0]
    tm = _row_block(s)

    def body(x_ref, g0_ref, w_ref, gcq_ref, gckv_ref, cc_ref, sn_ref, sp_ref,
             n_ref, proj_ref, cqn_ref, ckvn_ref, krr_ref):
        nb = _rms(x_ref[...], g0_ref[...]).astype(BF16)
        n_ref[...] = nb
        proj = _dot(nb, w_ref[...])
        proj_ref[...] = proj
        cqn_ref[...] = _rms(proj[:, :CQ], gcq_ref[...]).astype(BF16)
        ckvn_ref[...] = _rms(proj[:, CQ:CQ + CKV], gckv_ref[...]).astype(BF16)
        krr_ref[...] = _rope(proj[:, CQ + CKV:], cc_ref[...], sn_ref[...], sp_ref[...]).astype(BF16)

    row = lambda n: pl.BlockSpec((tm, n), lambda i: (i, 0))
    full = lambda a: pl.BlockSpec(a.shape, lambda i: (0, 0))
    return _pc_carrying(
        body, carry, (x, g0, w_in, g_cq, g_ckv, cc, sn, sp), name="mla_in_fwd", grid=(s // tm,),
        in_specs=[row(D_MODEL), full(g0), full(w_in), full(g_cq), full(g_ckv), row(128), row(128), row(128)],
        out_specs=[row(D_MODEL), row(PROJ_PAD), row(CQ), row(CKV), row(128)],
        out_shape=[_sds((s, D_MODEL), BF16), _sds((s, PROJ_PAD), F32), _sds((s, CQ), BF16),
                   _sds((s, CKV), BF16), _sds((s, 128), BF16)],
    )


def _qkv_proj(cqn, ckvn, krr, w_uq, w_ukv, cc, sn, sp, carry=None):
    s = cqn.shape[0]
    tm = min(_FFN_ROWS, s)

    def body(cqn_ref, ckvn_ref, krr_ref, wuq_ref, wukv_ref, cc_ref, sn_ref, sp_ref, q_ref, k_ref, v_ref):
        cqn_b, ckvn_b, krr_b = cqn_ref[...], ckvn_ref[...], krr_ref[...]
        cc_b, sn_b, sp_b = cc_ref[...], sn_ref[...], sp_ref[...]
        for h in range(N_HEADS):
            q = _dot(cqn_b, wuq_ref[h]) * Q_SCALE
            q_ref[h, :, :NOPE] = q[:, :NOPE].astype(BF16)
            q_ref[h, :, NOPE:] = _rope(q[:, NOPE:], cc_b, sn_b, sp_b).astype(BF16)
            kv = _dot(ckvn_b, wukv_ref[h])
            k_ref[h, :, :NOPE] = kv[:, :NOPE].astype(BF16)
            k_ref[h, :, NOPE:] = krr_b
            v_ref[h] = kv[:, NOPE:].astype(BF16)

    row = lambda n: pl.BlockSpec((tm, n), lambda i: (i, 0))
    head_w = lambda a: pl.BlockSpec(a.shape, lambda i: (0, 0, 0))
    head_o = lambda n: pl.BlockSpec((N_HEADS, tm, n), lambda i: (0, i, 0))
    return _pc_carrying(
        body, carry, (cqn, ckvn, krr, w_uq, w_ukv, cc, sn, sp), name="qkv_proj", grid=(s // tm,),
        in_specs=[row(CQ), row(CKV), row(128), head_w(w_uq), head_w(w_ukv), row(128), row(128), row(128)],
        out_specs=[head_o(HEAD_PAD), head_o(HEAD_PAD), head_o(V_DIM)],
        out_shape=[_sds((N_HEADS, s, HEAD_PAD), BF16), _sds((N_HEADS, s, HEAD_PAD), BF16),
                   _sds((N_HEADS, s, V_DIM), BF16)],
    )


def _chunk_bias(bq, bk, first_key=0):
    rows = lax.broadcasted_iota(jnp.int32, (bq, bk), 0)
    cols = lax.broadcasted_iota(jnp.int32, (bq, bk), 1) + first_key
    visible = jnp.right_shift(cols, CHUNK_SHIFT) <= jnp.right_shift(rows, CHUNK_SHIFT)
    return jnp.where(visible, 0.0, NEG).astype(F32)


_ATTN_FWD_ROWS = 1024


def _attn_fwd(q, k, v, carry=None):
    s = q.shape[1]
    bk = _row_block(s)
    bq = min(_ATTN_FWD_ROWS, s)
    nd = bq // bk

    def body(q_ref, k_ref, v_ref, o_ref, lse_ref, bias_ref):
        i = pl.program_id(1)
        qb = q_ref[0]

        @pl.when((pl.program_id(0) == 0) & (i == 0))
        def _():
            bias_ref[...] = _chunk_bias(bq, bk)

        def block(j, carry, queries, bias=None):
            m, l, acc = carry
            start = pl.multiple_of(j * bk, bk)
            kb = k_ref[0, pl.ds(start, bk), :]
            vb = v_ref[0, pl.ds(start, bk), :]
            sc = _dot_nt(queries, kb)
            if bias is not None:
                sc = sc + bias
            m_new = jnp.maximum(m, jnp.max(sc, axis=-1, keepdims=True))
            p = jnp.exp2(sc - m_new)
            alpha = jnp.exp2(m - m_new)
            l = alpha * l + jnp.sum(p, axis=-1, keepdims=True)
            acc = alpha * acc + _dot(p.astype(BF16), vb)
            return m_new, l, acc

        carry = (jnp.full((bq, 1), NEG, F32), jnp.zeros((bq, 1), F32), jnp.zeros((bq, V_DIM), F32))
        carry = lax.fori_loop(0, i * nd, lambda j, c: block(j, c, qb), carry)
        for d in range(nd):
            first = d * bk
            carry = block(i * nd + d, carry, qb[first:], bias_ref[:bq - first, :])
            m, l, acc = (c[:bk] for c in carry)
            o_ref[first:first + bk, :] = (acc / l).astype(BF16)
            lse_ref[0, first:first + bk, :] = jnp.broadcast_to(m + jnp.log(l) * LOG2_E, (bk, 128))
            if d < nd - 1:
                carry = tuple(c[bk:] for c in carry)

    return _pc_carrying(
        body, carry, (q, k, v), name="attn_fwd", grid=(N_HEADS, s // bq),
        in_specs=[pl.BlockSpec((1, bq, HEAD_PAD), lambda h, i: (h, i, 0)),
                  pl.BlockSpec((1, s, HEAD_PAD), lambda h, i: (h, 0, 0)),
                  pl.BlockSpec((1, s, V_DIM), lambda h, i: (h, 0, 0))],
        out_specs=[pl.BlockSpec((bq, V_DIM), lambda h, i: (i, h)),
                   pl.BlockSpec((1, bq, 128), lambda h, i: (h, i, 0))],
        out_shape=[_sds((s, N_HEADS * V_DIM), BF16), _sds((N_HEADS, s, 128), F32)],
        scratch_shapes=[pltpu.VMEM((bq, bk), F32)],
    )


def _matmul_res(a, w, res, name):
    s, kd = a.shape
    n = w.shape[1]
    tm = min(_FFN_ROWS, s)

    def body(a_ref, w_ref, r_ref, o_ref):
        o_ref[...] = r_ref[...] + _dot(a_ref[...], w_ref[...])

    return _pc(
        body, name=name, grid=(s // tm,),
        in_specs=[pl.BlockSpec((tm, kd), lambda i: (i, 0)), pl.BlockSpec((kd, n), lambda i: (0, 0)),
                  pl.BlockSpec((tm, n), lambda i: (i, 0))],
        out_specs=pl.BlockSpec((tm, n), lambda i: (i, 0)),
        out_shape=_sds((s, n), F32),
    )(a, w, res)


def _ffn_fwd(h, gain, wg, wu, wd, name, carry=None):
    s = h.shape[0]
    tm = min(_FFN_ROWS, s)
    tf = 512
    nf = FF_PAD // tf

    def body(h_ref, g_ref, wg_ref, wu_ref, wd_ref, o_ref, n_ref, gate_ref, up_ref, acc_ref):
        j = pl.program_id(1)

        @pl.when(j == 0)
        def _():
            n_ref[...] = _rms(h_ref[...], g_ref[...]).astype(BF16)
            acc_ref[...] = jnp.zeros_like(acc_ref)

        nb = n_ref[...]
        gate = _dot(nb, wg_ref[...])
        up = _dot(nb, wu_ref[...])
        gate_ref[...] = gate.astype(BF16)
        up_ref[...] = up.astype(BF16)
        act = gate * jax.nn.sigmoid(gate) * up
        acc_ref[...] += _dot(act.astype(BF16), wd_ref[...])

        @pl.when(j == nf - 1)
        def _():
            o_ref[...] = h_ref[...] + acc_ref[...]

    return _pc_carrying(
        body, carry, (h, gain, wg, wu, wd), name=name, grid=(s // tm, nf),
        in_specs=[pl.BlockSpec((tm, D_MODEL), lambda i, j: (i, 0)), pl.BlockSpec((1, D_MODEL), lambda i, j: (0, 0)),
                  pl.BlockSpec((D_MODEL, tf), lambda i, j: (0, j)), pl.BlockSpec((D_MODEL, tf), lambda i, j: (0, j)),
                  pl.BlockSpec((tf, D_MODEL), lambda i, j: (j, 0))],
        out_specs=[pl.BlockSpec((tm, D_MODEL), lambda i, j: (i, 0)), pl.BlockSpec((tm, D_MODEL), lambda i, j: (i, 0)),
                   pl.BlockSpec((tm, tf), lambda i, j: (i, j)), pl.BlockSpec((tm, tf), lambda i, j: (i, j))],
        out_shape=[_sds((s, D_MODEL), F32), _sds((s, D_MODEL), BF16), _sds((s, FF_PAD), BF16),
                   _sds((s, FF_PAD), BF16)],
        scratch_shapes=[pltpu.VMEM((tm, D_MODEL), F32)],
    )


def _rms_matmul(h, gain, w, name):
    s = h.shape[0]
    n = w.shape[1]
    tm = min(_FFN_ROWS, s)
    tn = 1024
    nn = n // tn

    def body(h_ref, g_ref, w_ref, n_ref, o_ref):
        @pl.when(pl.program_id(1) == 0)
        def _():
            n_ref[...] = _rms(h_ref[...], g_ref[...]).astype(BF16)

        o_ref[...] = _dot(n_ref[...], w_ref[...]).astype(BF16)

    return _pc(
        body, name=name, grid=(s // tm, nn),
        in_specs=[pl.BlockSpec((tm, D_MODEL), lambda i, j: (i, 0)), pl.BlockSpec((1, D_MODEL), lambda i, j: (0, 0)),
                  pl.BlockSpec((D_MODEL, tn), lambda i, j: (0, j))],
        out_specs=[pl.BlockSpec((tm, D_MODEL), lambda i, j: (i, 0)), pl.BlockSpec((tm, tn), lambda i, j: (i, j))],
        out_shape=[_sds((s, D_MODEL), BF16), _sds((s, n), BF16)],
    )(h, gain, w)


def _shift_down(u, k, rows):
    return jnp.where(rows >= k, pltpu.roll(u, k, 0), 0.0)


def _shift_up(u, k, rows, s):
    return jnp.where(rows < s - k, pltpu.roll(u, s - k, 0), 0.0)


_CONV_COLS = 128


def _conv_fwd(bcx, cw):
    s = bcx.shape[0]
    tc = _CONV_COLS
    nc = D_MODEL // tc

    def body(b_ref, c_ref, xp_ref, w_ref, o_ref):
        rows = lax.broadcasted_iota(jnp.int32, (s, tc), 0)
        u = c_ref[...].astype(F32) * xp_ref[...].astype(F32)
        w = w_ref[...]
        uc = w[2:3] * u + w[1:2] * _shift_down(u, 1, rows) + w[0:1] * _shift_down(u, 2, rows)
        o_ref[...] = (b_ref[...].astype(F32) * uc).astype(BF16)

    col = lambda off: pl.BlockSpec((s, tc), lambda j: (0, off + j))
    return _pc(
        body, name="conv_fwd", grid=(nc,),
        in_specs=[col(0), col(nc), col(2 * nc), pl.BlockSpec((3, tc), lambda j: (0, j))],
        out_specs=pl.BlockSpec((s, tc), lambda j: (0, j)),
        out_shape=_sds((s, D_MODEL), BF16),
    )(bcx, bcx, bcx, cw)


def _final_loss(h, gain, target):
    s = h.shape[0]
    tm = _row_block(s)

    def body(h_ref, g_ref, t_ref, dh_ref, dg_ref, loss_ref):
        i = pl.program_id(0)

        @pl.when(i == 0)
        def _():
            dg_ref[...] = jnp.zeros_like(dg_ref)
            loss_ref[...] = jnp.zeros_like(loss_ref)

        hb = h_ref[...]
        e = _rms(hb, g_ref[...]) - t_ref[...]
        loss_ref[...] += 0.5 * jnp.sum(jnp.mean(e * e, axis=-1, keepdims=True))
        dx, dg = _rms_bwd(e * (1.0 / D_MODEL), hb, g_ref[...])
        dh_ref[...] = dx
        dg_ref[...] += dg

    row = pl.BlockSpec((tm, D_MODEL), lambda i: (i, 0))
    vec = pl.BlockSpec((1, D_MODEL), lambda i: (0, 0))
    return _pc(
        body, name="final_loss", grid=(s // tm,),
        in_specs=[row, vec, row],
        out_specs=[row, vec, pl.BlockSpec((8, D_MODEL), lambda i: (0, 0))],
        out_shape=[_sds((s, D_MODEL), F32), _sds((1, D_MODEL), F32), _sds((8, D_MODEL), F32)],
    )(h, gain, target)


def _ffn_bwd_x(dho, h, gain, gate, up, wg, wu, wd, name, carry=None):
    s = h.shape[0]
    tm = _row_block(s)
    tf = 1024
    nf = FF_PAD // tf

    def body(dho_ref, h_ref, g_ref, gate_ref, up_ref, wg_ref, wu_ref, wd_ref,
             dh_ref, dhb_ref, dhob_ref, dgate_ref, dup_ref, act_ref, dgain_ref, acc_ref):
        i = pl.program_id(0)
        j = pl.program_id(1)

        @pl.when(j == 0)
        def _():
            dhob_ref[...] = dho_ref[...].astype(BF16)
            acc_ref[...] = jnp.zeros_like(acc_ref)

        @pl.when((i == 0) & (j == 0))
        def _():
            dgain_ref[...] = jnp.zeros_like(dgain_ref)

        dact = _dot_nt(dhob_ref[...], wd_ref[...])
        g = gate_ref[...].astype(F32)
        u = up_ref[...].astype(F32)
        sg = jax.nn.sigmoid(g)
        silu = g * sg
        dg = (dact * u * (sg * (1.0 + g * (1.0 - sg)))).astype(BF16)
        du = (dact * silu).astype(BF16)
        dgate_ref[...] = dg
        dup_ref[...] = du
        act_ref[...] = (silu * u).astype(BF16)
        acc_ref[...] += _dot_nt(dg, wg_ref[...]) + _dot_nt(du, wu_ref[...])

        @pl.when(j == nf - 1)
        def _():
            dx, dgain = _rms_bwd(acc_ref[...], h_ref[...], g_ref[...])
            dh = dho_ref[...] + dx
            dh_ref[...] = dh
            dhb_ref[...] = dh.astype(BF16)
            dgain_ref[...] += dgain

    row = pl.BlockSpec((tm, D_MODEL), lambda i, j: (i, 0))
    vec = pl.BlockSpec((1, D_MODEL), lambda i, j: (0, 0))
    hid = pl.BlockSpec((tm, tf), lambda i, j: (i, j))
    wcol = pl.BlockSpec((D_MODEL, tf), lambda i, j: (0, j))
    wrow = pl.BlockSpec((tf, D_MODEL), lambda i, j: (j, 0))
    return _pc_carrying(
        body, carry, (dho, h, gain, gate, up, wg, wu, wd), name=name, grid=(s // tm, nf),
        in_specs=[row, row, vec, hid, hid, wcol, wcol, wrow],
        out_specs=[row, row, row, hid, hid, hid, vec],
        out_shape=[_sds((s, D_MODEL), F32), _sds((s, D_MODEL), BF16), _sds((s, D_MODEL), BF16),
                   _sds((s, FF_PAD), BF16), _sds((s, FF_PAD), BF16), _sds((s, FF_PAD), BF16),
                   _sds((1, D_MODEL), F32)],
        scratch_shapes=[pltpu.VMEM((tm, D_MODEL), F32)],
    )


def _nt_rmsbwd(a, w, h, gain, dho, name, carry=None):
    stacked = a.ndim == 3
    if stacked:
        nk, s, tk = a.shape
    else:
        s, tk = a.shape
        nk = 1
    tm = min(_FFN_ROWS, s) if stacked else _row_block(s)

    def body(a_ref, w_ref, h_ref, g_ref, dho_ref, dh_ref, dhb_ref, dgain_ref, acc_ref):
        i = pl.program_id(0)
        j = pl.program_id(1)
        if stacked:
            a_ref = a_ref.at[0]

        @pl.when(j == 0)
        def _():
            acc_ref[...] = jnp.zeros_like(acc_ref)

        @pl.when((i == 0) & (j == 0))
        def _():
            dgain_ref[...] = jnp.zeros_like(dgain_ref)

        acc_ref[...] += _dot_nt(a_ref[...], w_ref[...])

        @pl.when(j == nk - 1)
        def _():
            dx, dgain = _rms_bwd(acc_ref[...], h_ref[...], g_ref[...])
            dh = dho_ref[...] + dx
            dh_ref[...] = dh
            dhb_ref[...] = dh.astype(BF16)
            dgain_ref[...] += dgain

    row = pl.BlockSpec((tm, D_MODEL), lambda i, j: (i, 0))
    vec = pl.BlockSpec((1, D_MODEL), lambda i, j: (0, 0))
    return _pc_carrying(
        body, carry, (a, w, h, gain, dho), name=name, grid=(s // tm, nk),
        in_specs=[pl.BlockSpec((1, tm, tk), lambda i, j: (j, i, 0)) if stacked else pl.BlockSpec((tm, tk), lambda i, j: (i, 0)),
                  pl.BlockSpec((D_MODEL, tk), lambda i, j: (0, j)), row, vec, row],
        out_specs=[row, row, vec],
        out_shape=[_sds((s, D_MODEL), F32), _sds((s, D_MODEL), BF16), _sds((1, D_MODEL), F32)],
        scratch_shapes=[pltpu.VMEM((tm, D_MODEL), F32)],
    )


def _matmul_nt(a, w, name):
    s, kd = a.shape
    n = w.shape[0]
    tm = min(_FFN_ROWS, s)

    def body(a_ref, w_ref, o_ref):
        o_ref[...] = _dot_nt(a_ref[...], w_ref[...]).astype(BF16)

    return _pc(
        body, name=name, grid=(s // tm,),
        in_specs=[pl.BlockSpec((tm, kd), lambda i: (i, 0)), pl.BlockSpec((n, kd), lambda i: (0, 0))],
        out_specs=pl.BlockSpec((tm, n), lambda i: (i, 0)),
        out_shape=_sds((s, n), BF16),
    )(a, w)


def _matmul_tn(a, b, name, carry=None):
    s, m = a.shape
    stacked = b.ndim == 3
    tmm = min(m, 1024)
    if stacked:
        n, tn = b.shape[0] * b.shape[2], b.shape[2]
    else:
        n = b.shape[1]
        tn = n if n <= 1024 else 1024
    tk = min(_TN_ROWS, s)
    nk = s // tk

    def body(a_ref, b_ref, o_ref, acc_ref):
        k = pl.program_id(2)
        if stacked:
            b_ref = b_ref.at[0]

        @pl.when(k == 0)
        def _():
            acc_ref[...] = jnp.zeros_like(acc_ref)

        acc_ref[...] += _dot_tn(a_ref[...], b_ref[...])

        @pl.when(k == nk - 1)
        def _():
            o_ref[...] = acc_ref[...].astype(BF16)

    (out,), got = _pc_carrying(
        body, carry, (a, b), name=name, grid=(m // tmm, n // tn, nk),
        in_specs=[pl.BlockSpec((tk, tmm), lambda i, j, k: (k, i)),
                  pl.BlockSpec((1, tk, tn), lambda i, j, k: (j, k, 0)) if stacked
                  else pl.BlockSpec((tk, tn), lambda i, j, k: (k, j))],
        out_specs=[pl.BlockSpec((tmm, tn), lambda i, j, k: (i, j))],
        out_shape=[_sds((m, n), BF16)],
        scratch_shapes=[pltpu.VMEM((tmm, tn), F32)],
    )
    return out if carry is None else (out, got)


def _conv_bwd(dbuc, bcx, cw):
    s = bcx.shape[0]
    tc = _CONV_COLS
    nc = D_MODEL // tc

    def body(d_ref, b_ref, c_ref, xp_ref, w_ref, dbcx_ref, dw_ref):
        rows = lax.broadcasted_iota(jnp.int32, (s, tc), 0)
        c = c_ref[...].astype(F32)
        xp = xp_ref[...].astype(F32)
        u = c * xp
        u1 = _shift_down(u, 1, rows)
        u2 = _shift_down(u, 2, rows)
        w = w_ref[...]
        uc = w[2:3] * u + w[1:2] * u1 + w[0:1] * u2
        d = d_ref[...].astype(F32)
        dbcx_ref[0] = (d * uc).astype(BF16)
        duc = d * b_ref[...].astype(F32)
        du = w[2:3] * duc + w[1:2] * _shift_up(duc, 1, rows, s) + w[0:1] * _shift_up(duc, 2, rows, s)
        dbcx_ref[1] = (du * xp).astype(BF16)
        dbcx_ref[2] = (du * c).astype(BF16)
        dw_ref[0:1, :] = jnp.sum(duc * u2, axis=0, keepdims=True)
        dw_ref[1:2, :] = jnp.sum(duc * u1, axis=0, keepdims=True)
        dw_ref[2:3, :] = jnp.sum(duc * u, axis=0, keepdims=True)

    col = lambda off: pl.BlockSpec((s, tc), lambda j: (0, off + j))
    tap = pl.BlockSpec((3, tc), lambda j: (0, j))
    return _pc(
        body, name="conv_bwd", grid=(nc,),
        in_specs=[col(0), col(0), col(nc), col(2 * nc), tap],
        out_specs=[pl.BlockSpec((3, s, tc), lambda j: (0, 0, j)), tap],
        out_shape=[_sds((3, s, D_MODEL), BF16), _sds((3, D_MODEL), F32)],
    )(dbuc, bcx, bcx, bcx, cw)


def _attn_bwd(q, k, v, do, o, lse, cc, sn, sp, carry=None):
    s = q.shape[1]
    bk = _row_block(s)
    nb = s // bk

    def body(q_ref, k_ref, v_ref, do_ref, o_ref, lse_ref, cc_ref, sn_ref, sp_ref,
             dq_ref, dk_ref, dv_ref, dqacc_ref, delta_ref, bias_ref):
        j = pl.program_id(1)
        kb = k_ref[0]
        vb = v_ref[0]

        @pl.when((pl.program_id(0) == 0) & (j == 0))
        def _():
            bias_ref[...] = _chunk_bias(2 * bk, bk)

        @pl.when(j == 0)
        def _():
            dqacc_ref[...] = jnp.zeros_like(dqacc_ref)

            def fill(i, _):
                rows = pl.ds(pl.multiple_of(i * bk, bk), bk)
                d = jnp.sum(do_ref[rows, :].astype(F32) * o_ref[rows, :].astype(F32), axis=-1, keepdims=True)
                delta_ref[rows, :] = jnp.broadcast_to(d, (bk, 128))
                return 0

            lax.fori_loop(0, nb, fill, 0)

        def step(i, carry, masked, blocks=1):
            dk, dv = carry
            rows = pl.ds(pl.multiple_of(i * bk, bk), blocks * bk)
            qb = q_ref[0, rows, :]
            dob = do_ref[rows, :]
            sc = _dot_nt(qb, kb)
            if masked:
                sc = sc + bias_ref[:blocks * bk, :]
            p = jnp.exp2(sc - lse_ref[0, rows, :][:, :1])
            dv = dv + _dot_tn(p.astype(BF16), dob)
            ds = (p * (_dot_nt(dob, vb) - delta_ref[rows, :][:, :1])).astype(BF16)
            dk = dk + _dot_tn(ds, qb)
            dqacc_ref[rows, :] += _dot(ds, kb)
            return dk, dv

        rest = nb - 1 - j
        odd = lax.rem(rest, 2)
        carry = (jnp.zeros((bk, HEAD_PAD), F32), jnp.zeros((bk, V_DIM), F32))
        carry = lax.fori_loop(0, 1 - odd, lambda t, c: step(j, c, True), carry)
        if nb > 1:
            carry = lax.fori_loop(0, odd, lambda t, c: step(j, c, True, blocks=2), carry)
        dq = dqacc_ref[pl.ds(pl.multiple_of(j * bk, bk), bk), :] * ATT_SCALE
        dq_ref[:, :NOPE] = dq[:, :NOPE].astype(BF16)
        dq_ref[:, NOPE:] = _rope_bwd(dq[:, NOPE:], cc_ref[...], sn_ref[...], sp_ref[...]).astype(BF16)
        if nb > 1:
            carry = lax.fori_loop(
                0, rest // 2, lambda t, c: step(j + 1 + odd + 2 * t, c, False, blocks=2), carry)
        dk, dv = carry
        dk_ref[0] = (dk * LN_2).astype(BF16)
        dv_ref[0] = dv.astype(BF16)

    blk = lambda n: pl.BlockSpec((1, bk, n), lambda h, j: (h, j, 0))
    whole = lambda n: pl.BlockSpec((1, s, n), lambda h, j: (h, 0, 0))
    cols = pl.BlockSpec((s, V_DIM), lambda h, j: (0, h))
    tab = pl.BlockSpec((bk, 128), lambda h, j: (j, 0))
    return _pc_carrying(
        body, carry, (q, k, v, do, o, lse, cc, sn, sp), name="attn_bwd", grid=(N_HEADS, nb),
        in_specs=[whole(HEAD_PAD), blk(HEAD_PAD), blk(V_DIM), cols, cols, whole(128), tab, tab, tab],
        out_specs=[pl.BlockSpec((bk, HEAD_PAD), lambda h, j: (j, h)), blk(HEAD_PAD), blk(V_DIM)],
        out_shape=[_sds((s, N_HEADS * HEAD_PAD), BF16), _sds((N_HEADS, s, HEAD_PAD), BF16),
                   _sds((N_HEADS, s, V_DIM), BF16)],
        scratch_shapes=[pltpu.VMEM((s, HEAD_PAD), F32), pltpu.VMEM((s, 128), F32), pltpu.VMEM((2 * bk, bk), F32)],
    )


def _mla_mid_bwd(dq, dk, dv, proj, w_uq, w_ukv, g_cq, g_ckv, cc, sn, sp, carry=None):
    s = dq.shape[0]
    tm = _row_block(s)

    def body(dq_ref, dk_ref, dv_ref, proj_ref, wuq_ref, wukv_ref, gcq_ref, gckv_ref, cc_ref, sn_ref, sp_ref,
             dproj_ref, dkv_ref, dgcq_ref, dgckv_ref, acq_ref, ackv_ref, akr_ref):
        @pl.when(pl.program_id(0) == 0)
        def _():
            dgcq_ref[...] = jnp.zeros_like(dgcq_ref)
            dgckv_ref[...] = jnp.zeros_like(dgckv_ref)

        for h in range(N_HEADS):
            cols = slice(h * HEAD_PAD, (h + 1) * HEAD_PAD)
            dkb = dk_ref[h]
            dkv_ref[:, h * HEAD_PAD:h * HEAD_PAD + NOPE] = dkb[:, :NOPE]
            dkv_ref[:, h * HEAD_PAD + NOPE:(h + 1) * HEAD_PAD] = dv_ref[h]
            cq_part = _dot_nt(dq_ref[:, cols], wuq_ref[h])
            ckv_part = _dot_nt(dkv_ref[:, cols], wukv_ref[h])
            kr_part = dkb[:, NOPE:].astype(F32)
            if h == 0:
                acq_ref[...], ackv_ref[...], akr_ref[...] = cq_part, ckv_part, kr_part
            else:
                acq_ref[...] += cq_part
                ackv_ref[...] += ckv_part
                akr_ref[...] += kr_part

        proj = proj_ref[...]
        dcq, dgcq = _rms_bwd(acq_ref[...], proj[:, :CQ], gcq_ref[...])
        dckv, dgckv = _rms_bwd(ackv_ref[...], proj[:, CQ:CQ + CKV], gckv_ref[...])
        dproj_ref[:, :CQ] = dcq.astype(BF16)
        dproj_ref[:, CQ:CQ + CKV] = dckv.astype(BF16)
        dproj_ref[:, CQ + CKV:] = _rope_bwd(akr_ref[...], cc_ref[...], sn_ref[...], sp_ref[...]).astype(BF16)
        dgcq_ref[...] += dgcq
        dgckv_ref[...] += dgckv

    head_blk = lambda n: pl.BlockSpec((N_HEADS, tm, n), lambda i: (0, i, 0))
    head_cols = pl.BlockSpec((tm, N_HEADS * HEAD_PAD), lambda i: (i, 0))
    head_w = lambda a: pl.BlockSpec(a.shape, lambda i: (0, 0, 0))
    row = lambda n: pl.BlockSpec((tm, n), lambda i: (i, 0))
    vec = lambda n: pl.BlockSpec((1, n), lambda i: (0, 0))
    return _pc_carrying(
        body, carry, (dq, dk, dv, proj, w_uq, w_ukv, g_cq, g_ckv, cc, sn, sp),
        name="mla_mid_bwd", grid=(s // tm,),
        in_specs=[head_cols, head_blk(HEAD_PAD), head_blk(V_DIM), row(PROJ_PAD), head_w(w_uq), head_w(w_ukv),
                  vec(CQ), vec(CKV), row(128), row(128), row(128)],
        out_specs=[row(PROJ_PAD), head_cols, vec(CQ), vec(CKV)],
        out_shape=[_sds((s, PROJ_PAD), BF16), _sds((s, N_HEADS * HEAD_PAD), BF16), _sds((1, CQ), F32),
                   _sds((1, CKV), F32)],
        scratch_shapes=[pltpu.VMEM((tm, CQ), F32), pltpu.VMEM((tm, CKV), F32), pltpu.VMEM((tm, 128), F32)],
    )


def _peer(k):
    x, y, c = lax.axis_index("x"), lax.axis_index("y"), lax.axis_index("c")
    px = 1 - x if k & 4 else x
    py = 1 - y if k & 2 else y
    pc = 1 - c if k & 1 else c
    return (px, py, pc), 4 * px + 2 * py + pc


def _exchange_ops(arrays, scatter):
    nw = len(arrays)
    by_cols = [isinstance(a, _Cols) for a in arrays]
    used_rows = [a.rows if isinstance(a, _Rows) else None for a in arrays]
    arrays = [a.array if isinstance(a, (_Cols, _Rows)) else a for a in arrays]
    direct = range(1, N_DEV) if scatter else (1, 2, 4, 6)
    assert scatter or not any(used_rows)

    def columns(ref, idx, width):
        return ref.at[:, pl.ds(pl.multiple_of(idx * width, 128), width)]

    def sent(ins, w, idx):
        if not scatter:
            return ins[w]
        if by_cols[w]:
            return columns(ins[w], idx, arrays[w].shape[1] // N_DEV)
        return ins[w].at[idx, pl.ds(0, used_rows[w])] if used_rows[w] else ins[w].at[idx]

    def slot(outs, w, idx):
        if by_cols[w] and not scatter:
            return columns(outs[w], idx, arrays[w].shape[1])
        return outs[w].at[idx]

    def copy(w, k, src, dst, to, send_sems, recv_sems):
        return pltpu.make_async_remote_copy(
            src_ref=src, dst_ref=dst, send_sem=send_sems.at[w * N_DEV + k], recv_sem=recv_sems.at[w * N_DEV + k],
            device_id=to, device_id_type=pl.DeviceIdType.MESH)

    def own_copies(ins, outs, local_sems):
        _, me = _peer(0)
        return [pltpu.make_async_copy(sent(ins, w, me), slot(outs, w, me), local_sems.at[w]) for w in range(nw)]

    def sends(ins, outs, send_sems, recv_sems):
        _, me = _peer(0)
        out = []
        for k in direct:
            dev, idx = _peer(k)
            for w in range(nw):
                out.append(copy(w, k, sent(ins, w, idx), slot(outs, w, me), dev, send_sems, recv_sems))
        return out

    def relays(outs, send_sems, recv_sems):
        sibling, _ = _peer(1)
        out = []
        for k in (2, 4, 6):
            _, idx = _peer(k)
            for w in range(nw):
                out.append(copy(w, k + 1, slot(outs, w, idx), slot(outs, w, idx), sibling, send_sems, recv_sems))
        return out

    def arrival(outs, w, k, send_sems, recv_sems):
        dev, idx = _peer(k)
        return copy(w, k, slot(outs, w, idx), slot(outs, w, idx), dev, send_sems, recv_sems)

    def start(ins, outs, send_sems, recv_sems, local_sems):
        for cp in own_copies(ins, outs, local_sems) + sends(ins, outs, send_sems, recv_sems):
            cp.start()

    def relay(ins, outs, send_sems, recv_sems, local_sems):
        for k in (2, 4, 6):
            for w in range(nw):
                arrival(outs, w, k, send_sems, recv_sems).wait_recv()
        for cp in relays(outs, send_sems, recv_sems):
            cp.start()

    def wait(ins, outs, send_sems, recv_sems, local_sems):
        for cp in own_copies(ins, outs, local_sems):
            cp.wait()
        for cp in sends(ins, outs, send_sems, recv_sems) + ([] if scatter else relays(outs, send_sems, recv_sems)):
            cp.wait_send()
        for k in (range(1, N_DEV) if scatter else (1, 3, 5, 7)):
            for w in range(nw):
                arrival(outs, w, k, send_sems, recv_sems).wait_recv()

    def landed_shape(a, cols, rows):
        if rows:
            return (N_DEV, rows) + a.shape[2:]
        if not cols:
            return a.shape if scatter else (N_DEV,) + a.shape
        r, c = a.shape
        return (N_DEV, r, c // N_DEV) if scatter else (r, N_DEV * c)

    landed = [_sds(landed_shape(a, cols, rows), a.dtype) for a, cols, rows in zip(arrays, by_cols, used_rows)]
    sems = [pltpu.SemaphoreType.DMA((nw * N_DEV,)), pltpu.SemaphoreType.DMA((nw * N_DEV,)),
            pltpu.SemaphoreType.DMA((nw,))]
    return start, (None if scatter else relay), wait, landed, sems, arrays


class _Cols:
    def __init__(self, array):
        self.array = array


class _Rows:
    def __init__(self, array, rows):
        self.array, self.rows = array, rows


def _exchange(arrays, scatter, name):
    nw = len(arrays)
    start, relay, wait, landed, sems, arrays = _exchange_ops(arrays, scatter)

    def body(*refs):
        ins, outs, csem = refs[:nw], refs[nw:2 * nw], refs[2 * nw:]
        start(ins, outs, *csem)
        if relay is not None:
            relay(ins, outs, *csem)
        wait(ins, outs, *csem)

    any_spec = pl.BlockSpec(memory_space=pl.ANY)
    return _pc(body, name=name, in_specs=[any_spec] * nw, out_specs=[any_spec] * nw, out_shape=landed,
               scratch_shapes=sems)(*arrays)


def _adam_math(g, w, m, v):
    m = ADAM_B1 * m + (1.0 - ADAM_B1) * g
    v = ADAM_B2 * v + (1.0 - ADAM_B2) * jnp.square(g)
    m_hat = m / (1.0 - ADAM_B1 ** ADAM_STEP)
    v_hat = v / (1.0 - ADAM_B2 ** ADAM_STEP)
    delta = -ADAM_LR * (m_hat / (jnp.sqrt(v_hat) + ADAM_EPS) + ADAM_WD * w)
    return delta, m, v


def _adam_rows(r):
    for t in range(min(r, 512) // 16 * 16, 0, -16):
        if r % t == 0:
            return t
    return r


def _adamw(parts, w, m, v, name):
    nl, r, c = w.shape
    tr = _adam_rows(r)
    nr = r // tr

    def body(*refs):
        p_refs, (w_ref, m_ref, v_ref, g_ref, d_ref, mo_ref, vo_ref) = refs[:nl], refs[nl:]
        for layer in range(nl):
            @pl.when(pl.program_id(0) == layer)
            def _(p_ref=p_refs[layer]):
                g = p_ref[0].astype(F32)
                for src in range(1, N_DEV):
                    g = g + p_ref[src].astype(F32)
                delta, m2, v2 = _adam_math(g, w_ref[0], m_ref[0], v_ref[0])
                g_ref[0] = g
                d_ref[0] = delta
                mo_ref[0] = m2
                vo_ref[0] = v2

    def part_spec(layer):
        def index(l, i):
            return 0, jnp.where(l == layer, i, jnp.where(l < layer, 0, nr - 1)), 0
        return pl.BlockSpec((N_DEV, tr, c), index)

    blk = pl.BlockSpec((1, tr, c), lambda l, i: (l, i, 0))
    return _pc(
        body, name=name, grid=(nl, nr),
        in_specs=[part_spec(layer) for layer in range(nl)] + [blk, blk, blk],
        out_specs=[blk, blk, blk, blk],
        out_shape=[_sds((nl, r, c), F32)] * 4,
    )(*parts, w, m, v)


def _small_allreduce_adamw(gpack, wpack, mpack, vpack, last_grads):
    shape = gpack.shape
    compact = (SMALL_ROWS, D_MODEL)
    nw = len(last_grads)
    start, _, wait, landed, exchange_sems, last_grads = _exchange_ops(last_grads, True)

    def body(*refs):
        (g_ref, w_ref, m_ref, v_ref), rest = refs[:4], refs[4:]
        cin, rest = rest[:nw], rest[nw:]
        (go_ref, d_ref, mo_ref, vo_ref), rest = rest[:4], rest[4:]
        cout, rest = rest[:nw], rest[nw:]
        (comp_ref, gath_ref, send_sems, recv_sems), csem = rest[:4], rest[4:]
        start(cin, cout, *csem)
        _, me = _peer(0)
        comp_ref[...] = jnp.zeros(compact, F32)
        r = 0
        for p, n in enumerate(SMALL_PIECES):
            comp_ref[r:r + n, :] = g_ref[8 * p:8 * p + n, :]
            r += n
        gath_ref[me] = comp_ref[...]
        copies = []
        for k in range(1, N_DEV):
            dev, idx = _peer(k)
            copies.append(pltpu.make_async_remote_copy(
                src_ref=comp_ref, dst_ref=gath_ref.at[me], send_sem=send_sems.at[k], recv_sem=recv_sems.at[k],
                device_id=dev, device_id_type=pl.DeviceIdType.MESH))
        for cp in copies:
            cp.start()
        for cp in copies:
            cp.wait_send()
        for k in range(1, N_DEV):
            dev, idx = _peer(k)
            pltpu.make_async_remote_copy(
                src_ref=comp_ref, dst_ref=gath_ref.at[idx], send_sem=send_sems.at[k], recv_sem=recv_sems.at[k],
                device_id=dev, device_id_type=pl.DeviceIdType.MESH).wait_recv()
        g = gath_ref[0]
        for src in range(1, N_DEV):
            g = g + gath_ref[src]
        go_ref[...] = jnp.zeros(shape, F32)
        r = 0
        for p, n in enumerate(SMALL_PIECES):
            go_ref[8 * p:8 * p + n, :] = g[r:r + n, :]
            r += n
        delta, m2, v2 = _adam_math(go_ref[...], w_ref[...], m_ref[...], v_ref[...])
        d_ref[...] = delta
        mo_ref[...] = m2
        vo_ref[...] = v2
        wait(cin, cout, *csem)

    vm = pl.BlockSpec(memory_space=pltpu.VMEM)
    any_spec = pl.BlockSpec(memory_space=pl.ANY)
    res = _pc(
        body, name="small_allreduce_adamw",
        in_specs=[vm] * 4 + [any_spec] * nw, out_specs=[vm] * 4 + [any_spec] * nw,
        out_shape=[_sds(shape, F32)] * 4 + landed,
        scratch_shapes=[pltpu.VMEM(compact, F32), pltpu.VMEM((N_DEV,) + compact, F32),
                        pltpu.SemaphoreType.DMA((N_DEV,)), pltpu.SemaphoreType.DMA((N_DEV,))] + exchange_sems,
    )(gpack, wpack, mpack, vpack, *last_grads)
    return res[:4], res[4:]


def _rows_from_shards(g):
    return g.reshape(N_DEV * g.shape[1], g.shape[2])


def _shards_from_rows(a):
    return a.reshape(N_DEV, a.shape[0] // N_DEV, a.shape[1])


def _pad_to(a, rows, cols):
    return jnp.pad(a, ((0, rows - a.shape[0]), (0, cols - a.shape[1])))


def _rope_tables(pos):
    inv_freq = 1.0 / (ROPE_THETA ** (jnp.arange(0, ROPE, 2, dtype=F32) / ROPE))
    ang = pos.astype(F32)[:, None] * inv_freq
    cos, sin = jnp.cos(ang), jnp.sin(ang)
    z32, z64, z96 = (jnp.zeros((pos.shape[0], n), F32) for n in (32, 64, 96))
    return (jnp.concatenate([cos, cos, z64], axis=1), jnp.concatenate([-sin, z96], axis=1),
            jnp.concatenate([z32, sin, z64], axis=1))


def _pad_row(vec):
    vec = vec.reshape(1, -1)
    return jnp.pad(vec, ((0, 0), (0, D_MODEL - vec.shape[1])))


def _forward_backward(x, target, tables, gains, shards):
    cc, sn, sp = tables
    ffn_g = gains["ffn_norm"]
    by_cols = ("wg0", "wu0", "wg1", "wu1", "c_in")

    def gather(names, relay_at):
        return [_Cols(shards[n]) if n in by_cols else shards[n] for n in names], False, relay_at

    (got,) = _exchange([shards["w_in"]], False, "gather_w_in")
    w_in = _rows_from_shards(got)
    (n0, proj, cqn, ckvn, krr), (w_uq, w_ukv) = _mla_in_fwd(
        x, gains["mla_norm"], w_in, gains["g_cq"], gains["g_ckv"], cc, sn, sp, carry=gather(["w_uq", "w_ukv"], 0.5))
    (q, k, v), (w_o, conv_norm, conv_w) = _qkv_proj(
        cqn, ckvn, krr, w_uq, w_ukv, cc, sn, sp, carry=gather(["w_o", "conv_norm", "conv_w"], 0.5))
    w_o = _rows_from_shards(w_o)
    conv_norm = conv_norm.reshape(1, D_MODEL)
    conv_w = jnp.transpose(conv_w, (1, 0, 2)).reshape(3, D_MODEL)
    (o, lse), (wg0, wu0, wd0, c_in, c_out, wd1) = _attn_fwd(
        q, k, v, carry=gather(["wg0", "wu0", "wd0", "c_in", "c_out", "wd1"], 0.75))
    wd0 = _rows_from_shards(wd0)
    wd1 = _rows_from_shards(wd1)
    c_out = _rows_from_shards(c_out)
    h1 = _matmul_res(o, w_o, x, "mla_out_fwd")
    (h2, n1, gate0, up0), (wg1, wu1) = _ffn_fwd(
        h1, ffn_g[0:1], wg0, wu0, wd0, "ffn0_fwd", carry=gather(["wg1", "wu1"], 0.7))
    n2, bcx = _rms_matmul(h2, conv_norm, c_in, "conv_in_fwd")
    bu = _conv_fwd(bcx, conv_w)
    h3 = _matmul_res(bu, c_out, h2, "conv_out_fwd")
    (h4, n3, gate1, up1), _ = _ffn_fwd(h3, ffn_g[1:2], wg1, wu1, wd1, "ffn1_fwd")
    dh4, d_final, loss = _final_loss(h4, gains["final_norm"], target)

    small = {"final_norm": d_final}
    parts = {}

    def scatter(**blocks):
        return list(blocks), (list(blocks.values()), True, None)

    (dh3, dh3_b, dh4_b, dgate, dup, act, d_ffn1), _ = _ffn_bwd_x(
        dh4, h3, ffn_g[1:2], gate1, up1, wg1, wu1, wd1, "ffn1_bwd")
    dwg1 = _matmul_tn(dgate, n3, "ffn1_dwg")
    dwu1 = _matmul_tn(dup, n3, "ffn1_dwu")
    dwd1 = _matmul_tn(act, dh4_b, "ffn1_dwd")
    dbuc = _matmul_nt(dh3_b, c_out, "conv_out_bwd")
    d_c_out = _matmul_tn(bu, dh3_b, "conv_dwout")
    dbcx, small["conv_w"] = _conv_bwd(dbuc, bcx, conv_w)
    d_c_in = _matmul_tn(n2, dbcx, "conv_dwin")
    names, carry = scatter(c_out=_shards_from_rows(d_c_out))
    (dh2, dh2_b, small["conv_norm"]), got = _nt_rmsbwd(
        dbcx, c_in, h2, conv_norm, dh3, "conv_in_bwd", carry=carry)
    parts.update(zip(names, got))
    ffn_blocks = lambda d: _Rows(_shards_from_rows(d), FF_SHARD)
    names, carry = scatter(c_in=_Cols(d_c_in), wd1=ffn_blocks(dwd1))
    (dh1, dh1_b, dh2_b2, dgate, dup, act, d_ffn0), got = _ffn_bwd_x(
        dh2, h1, ffn_g[0:1], gate0, up0, wg0, wu0, wd0, "ffn0_bwd", carry=carry)
    parts.update(zip(names, got))
    dwg0 = _matmul_tn(dgate, n1, "ffn0_dwg")
    dwu0 = _matmul_tn(dup, n1, "ffn0_dwu")
    dwd0 = _matmul_tn(act, dh2_b2, "ffn0_dwd")
    small["ffn_norm"] = jnp.pad(d_ffn0, ((0, 7), (0, 0))) + jnp.pad(d_ffn1, ((1, 6), (0, 0)))
    do = _matmul_nt(dh1_b, w_o, "mla_out_bwd")
    d_w_o = _matmul_tn(o, dh1_b, "mla_dwo")
    names, carry = scatter(**{n: ffn_blocks(d) for n, d in dict(
        wg0=dwg0, wg1=dwg1, wu0=dwu0, wu1=dwu1, wd0=dwd0).items()})
    (dq, dk, dv), got = _attn_bwd(q, k, v, do, o, lse, cc, sn, sp, carry=carry)
    parts.update(zip(names, got))
    d_w_uq = _matmul_tn(dq, cqn, "mla_dwuq")
    names, carry = scatter(w_o=_shards_from_rows(d_w_o), w_uq=_Rows(_shards_from_rows(d_w_uq), NOPE + ROPE))
    (dproj, dkv, small["g_cq"], small["g_ckv"]), got = _mla_mid_bwd(
        dq, dk, dv, proj, w_uq, w_ukv, gains["g_cq"], gains["g_ckv"], cc, sn, sp, carry=carry)
    parts.update(zip(names, got))
    d_w_ukv = _matmul_tn(ckvn, dkv, "mla_dwukv")
    names, carry = scatter(w_ukv=_Cols(d_w_ukv))
    d_w_in, got = _matmul_tn(dproj, n0, "mla_dwin", carry=carry)
    parts.update(zip(names, got))
    (dx, _, small["mla_norm"]), _ = _nt_rmsbwd(dproj, w_in, x, gains["mla_norm"], dh1, "mla_in_bwd")
    return loss, dx, parts, _Cols(d_w_in), small


def kernel(x, positions, mla_norm, mla_w_in, mla_g_cq, mla_g_ckv, mla_w_uq, mla_w_ukv, mla_w_o, conv_norm, conv_w_in, conv_w, conv_w_out, ffn_norm, ffn_w_gate, ffn_w_up, ffn_w_down, final_norm, loss_target, m_mla_norm, m_mla_w_in, m_mla_g_cq, m_mla_g_ckv, m_mla_w_uq, m_mla_w_ukv, m_mla_w_o, m_conv_norm, m_conv_w_in, m_conv_w, m_conv_w_out, m_ffn_norm, m_ffn_w_gate, m_ffn_w_up, m_ffn_w_down, m_final_norm, v_mla_norm, v_mla_w_in, v_mla_g_cq, v_mla_g_ckv, v_mla_w_uq, v_mla_w_ukv, v_mla_w_o, v_conv_norm, v_conv_w_in, v_conv_w, v_conv_w_out, v_ffn_norm, v_ffn_w_gate, v_ffn_w_up, v_ffn_w_down, v_final_norm):
    me = 4 * lax.axis_index("x") + 2 * lax.axis_index("y") + lax.axis_index("c")

    bf = lambda a, rows, cols: _pad_to(a.astype(BF16), rows, cols)
    shards = dict(
        w_in=bf(mla_w_in[0], D_MODEL // N_DEV, PROJ_PAD), w_uq=bf(mla_w_uq[0], CQ, HEAD_PAD),
        w_ukv=mla_w_ukv[0].astype(BF16), w_o=mla_w_o[0].astype(BF16),
        c_in=conv_w_in[0].astype(BF16), c_out=conv_w_out[0].astype(BF16),
        conv_norm=conv_norm, conv_w=conv_w[0])
    for l in range(2):
        shards.update({f"wg{l}": bf(ffn_w_gate[l], D_MODEL, FF_SHARD_PAD), f"wu{l}": bf(ffn_w_up[l], D_MODEL, FF_SHARD_PAD),
                       f"wd{l}": bf(ffn_w_down[l], FF_SHARD_PAD, D_MODEL)})
    gains = dict(mla_norm=mla_norm, g_cq=mla_g_cq, g_ckv=mla_g_ckv, ffn_norm=ffn_norm,
                 final_norm=final_norm.reshape(1, -1))
    loss_local, dx, parts, d_w_in, grads = _forward_backward(
        x[0], loss_target[0], _rope_tables(positions[0]), gains, shards)

    col0 = me * (D_MODEL // N_DEV)

    def place(shard):
        return lax.dynamic_update_slice(jnp.zeros((shard.shape[0], D_MODEL), F32), shard, (0, col0))

    no_loss = jnp.zeros((8, D_MODEL), F32)

    def pack(mla_n, g_cq, g_ckv, ffn_n, fin_n, conv_n, conv_taps, loss_tile):
        rows = [mla_n, _pad_row(g_cq), _pad_row(g_ckv), ffn_n, fin_n.reshape(1, -1), conv_n, conv_taps, loss_tile]
        assert all(r.shape[0] in (n, 8) for r, n in zip(rows, SMALL_PIECES))
        return jnp.concatenate([jnp.pad(r, ((0, 8 - r.shape[0]), (0, 0))) for r in rows], axis=0)

    gpack = pack(grads["mla_norm"], grads["g_cq"], grads["g_ckv"], grads["ffn_norm"], grads["final_norm"],
                 grads["conv_norm"], grads["conv_w"], loss_local)
    wpack = pack(mla_norm, mla_g_cq, mla_g_ckv, ffn_norm, final_norm, place(conv_norm), place(conv_w[0]), no_loss)
    mpack = pack(m_mla_norm, m_mla_g_cq, m_mla_g_ckv, m_ffn_norm, m_final_norm, place(m_conv_norm),
                 place(m_conv_w[0]), no_loss)
    vpack = pack(v_mla_norm, v_mla_g_cq, v_mla_g_ckv, v_ffn_norm, v_final_norm, place(v_conv_norm),
                 place(v_conv_w[0]), no_loss)
    small, (parts["w_in"],) = _small_allreduce_adamw(gpack, wpack, mpack, vpack, [d_w_in])
    loss = small[0][56, 0]

    def adamw(name, w, m, v, partials, transposed=False):
        if transposed:
            w, m, v = (jnp.swapaxes(a, 1, 2) for a in (w, m, v))
        outs = _adamw(partials, w, m, v, "adamw_" + name)
        return [jnp.swapaxes(o, 1, 2) for o in outs] if transposed else outs

    res = dict(
        w_in=adamw("w_in", mla_w_in, m_mla_w_in, v_mla_w_in, [parts["w_in"]], True),
        w_uq=adamw("w_uq", mla_w_uq, m_mla_w_uq, v_mla_w_uq, [parts["w_uq"]], True),
        w_ukv=adamw("w_ukv", mla_w_ukv, m_mla_w_ukv, v_mla_w_ukv, [parts["w_ukv"]]),
        w_o=adamw("w_o", mla_w_o, m_mla_w_o, v_mla_w_o, [parts["w_o"]]),
        c_in=adamw("c_in", conv_w_in, m_conv_w_in, v_conv_w_in, [parts["c_in"]]),
        c_out=adamw("c_out", conv_w_out, m_conv_w_out, v_conv_w_out, [parts["c_out"]]),
        wg=adamw("wg", ffn_w_gate, m_ffn_w_gate, v_ffn_w_gate, [parts["wg0"], parts["wg1"]], True),
        wu=adamw("wu", ffn_w_up, m_ffn_w_up, v_ffn_w_up, [parts["wu0"], parts["wu1"]], True),
        wd=adamw("wd", ffn_w_down, m_ffn_w_down, v_ffn_w_down, [parts["wd0"], parts["wd1"]]),
    )

    def unpack(p):
        own = lambda rows: lax.dynamic_slice(rows, (0, col0), (rows.shape[0], D_MODEL // N_DEV))
        return dict(mla_norm=p[0:1], g_cq=p[8:9, :CQ], g_ckv=p[16:17, :CKV], ffn_norm=p[24:26], final_norm=p[32],
                    conv_norm=own(p[40:41]), conv_w=own(p[48:51])[None])

    small = [unpack(p) for p in small]
    order = ["mla_norm", "w_in", "g_cq", "g_ckv", "w_uq", "w_ukv", "w_o", "conv_norm", "c_in", "conv_w", "c_out",
             "ffn_norm", "wg", "wu", "wd", "final_norm"]
    out = [loss, dx[None]]
    for kind in range(4):
        for n in order:
            out.append(res[n][kind] if n in res else small[kind][n])
    return tuple(out)
```

```python
import math

import jax
import jax.numpy as jnp
from jax import lax
from jax.experimental import pallas as pl
from jax.experimental.pallas import tpu as pltpu

F32 = jnp.float32
BF16 = jnp.bfloat16

N_DEV = 8
D_MODEL = 1024
N_HEADS = 8
NOPE = 128
ROPE = 64
V_DIM = 128
HEAD_PAD = 256
CQ = 512
CKV = 256
PROJ_PAD = CQ + CKV + 128
FF_SHARD = 352
FF_SHARD_PAD = 384
FF_PAD = FF_SHARD_PAD * N_DEV
CHUNK_SHIFT = 6
RMS_EPS = 1e-6
ROPE_THETA = 10000.0
ATT_SCALE = 1.0 / math.sqrt(NOPE + ROPE)
LOG2_E = math.log2(math.e)
LN_2 = math.log(2.0)
Q_SCALE = ATT_SCALE * LOG2_E
NEG = -1e30

ADAM_LR = 0.001
ADAM_B1 = 0.9
ADAM_B2 = 0.999
ADAM_EPS = 1e-08
ADAM_WD = 0.01
ADAM_STEP = 10

SMALL_PIECES = (1, 1, 1, 2, 1, 1, 3, 1)
SMALL_ROWS = 16

_NT = (((1,), (1,)), ((), ()))
_TN = (((0,), (0,)), ((), ()))


def _pc(body, *, name, out_shape, grid=(), in_specs=None, out_specs=None, scratch_shapes=()):
    kwargs = dict(
        name=name, out_shape=out_shape, grid=grid, scratch_shapes=scratch_shapes,
        compiler_params=pltpu.CompilerParams(),
    )
    if in_specs is not None:
        kwargs["in_specs"] = in_specs
    if out_specs is not None:
        kwargs["out_specs"] = out_specs
    return pl.pallas_call(body, **kwargs)


def _pc_carrying(body, carry, operands, *, name, out_shape, grid, in_specs, out_specs, scratch_shapes=()):
    if carry is None:
        return _pc(body, name=name, out_shape=out_shape, grid=grid, in_specs=in_specs, out_specs=out_specs,
                   scratch_shapes=scratch_shapes)(*operands), None
    arrays, scatter, relay_at = carry
    nw, n_in, n_out, n_scr = len(arrays), len(in_specs), len(out_shape), len(scratch_shapes)
    start, relay, wait, landed_shapes, sems, arrays = _exchange_ops(arrays, scatter)
    n_steps = math.prod(grid)
    relay_step = None if relay is None else min(int(relay_at * n_steps), n_steps - 1)

    def wrapped(*refs):
        ins, rest = refs[:n_in], refs[n_in:]
        cin, rest = rest[:nw], rest[nw:]
        outs, rest = rest[:n_out], rest[n_out:]
        cout, rest = rest[:nw], rest[nw:]
        scr, csem = rest[:n_scr], rest[n_scr:]
        step = pl.program_id(0)
        for a in range(1, len(grid)):
            step = step * grid[a] + pl.program_id(a)

        @pl.when(step == 0)
        def _():
            start(cin, cout, *csem)

        if relay is not None:
            @pl.when(step == relay_step)
            def _():
                relay(cin, cout, *csem)

        body(*ins, *outs, *scr)

        @pl.when(step == n_steps - 1)
        def _():
            wait(cin, cout, *csem)

    any_spec = pl.BlockSpec(memory_space=pl.ANY)
    res = _pc(
        wrapped, name=name, grid=grid,
        in_specs=list(in_specs) + [any_spec] * nw, out_specs=list(out_specs) + [any_spec] * nw,
        out_shape=list(out_shape) + landed_shapes, scratch_shapes=list(scratch_shapes) + sems,
    )(*operands, *arrays)
    return res[:n_out], res[n_out:]


def _sds(shape, dtype):
    return jax.ShapeDtypeStruct(shape, dtype)


def _dot(a, b):
    return jnp.dot(a, b, preferred_element_type=F32)


def _dot_nt(a, b):
    return lax.dot_general(a, b, _NT, preferred_element_type=F32)


def _dot_tn(a, b):
    return lax.dot_general(a, b, _TN, preferred_element_type=F32)


def _rstd(x):
    return lax.rsqrt(jnp.mean(x * x, axis=-1, keepdims=True) + RMS_EPS)


def _rms(x, g):
    return (x * _rstd(x)) * g


def _rms_bwd(dy, x, g):
    r = _rstd(x)
    xhat = x * r
    dxhat = dy * g
    dx = r * (dxhat - xhat * jnp.mean(dxhat * xhat, axis=-1, keepdims=True))
    return dx, jnp.sum(dy * xhat, axis=0, keepdims=True)


def _rope(t, cc, sn, sp):
    return t * cc + pltpu.roll(t, 96, 1) * sn + pltpu.roll(t, 32, 1) * sp


def _rope_bwd(dt, cc, sn, sp):
    return dt * cc + pltpu.roll(dt * sn, 32, 1) + pltpu.roll(dt * sp, 96, 1)


def _row_block(s):
    return min(512, s)


_TN_ROWS = 2048
_FFN_ROWS = 1024


def _mla_in_fwd(x, g0, w_in, g_cq, g_ckv, cc, sn, sp, carry=None):
    s = x.shape[0]
    tm = _row_block(s)

    def body(x_ref, g0_ref, w_ref, gcq_ref, gckv_ref, cc_ref, sn_ref, sp_ref,
             n_ref, proj_ref, cqn_ref, ckvn_ref, krr_ref):
        nb = _rms(x_ref[...], g0_ref[...]).astype(BF16)
        n_ref[...] = nb
        proj = _dot(nb, w_ref[...])
        proj_ref[...] = proj
        cqn_ref[...] = _rms(proj[:, :CQ], gcq_ref[...]).astype(BF16)
        ckvn_ref[...] = _rms(proj[:, CQ:CQ + CKV], gckv_ref[...]).astype(BF16)
        krr_ref[...] = _rope(proj[:, CQ + CKV:], cc_ref[...], sn_ref[...], sp_ref[...]).astype(BF16)

    row = lambda n: pl.BlockSpec((tm, n), lambda i: (i, 0))
    full = lambda a: pl.BlockSpec(a.shape, lambda i: (0, 0))
    return _pc_carrying(
        body, carry, (x, g0, w_in, g_cq, g_ckv, cc, sn, sp), name="mla_in_fwd", grid=(s // tm,),
        in_specs=[row(D_MODEL), full(g0), full(w_in), full(g_cq), full(g_ckv), row(128), row(128), row(128)],
        out_specs=[row(D_MODEL), row(PROJ_PAD), row(CQ), row(CKV), row(128)],
        out_shape=[_sds((s, D_MODEL), BF16), _sds((s, PROJ_PAD), F32), _sds((s, CQ), BF16),
                   _sds((s, CKV), BF16), _sds((s, 128), BF16)],
    )


def _qkv_proj(cqn, ckvn, krr, w_uq, w_ukv, cc, sn, sp, carry=None):
    s = cqn.shape[0]
    tm = min(_FFN_ROWS, s)

    def body(cqn_ref, ckvn_ref, krr_ref, wuq_ref, wukv_ref, cc_ref, sn_ref, sp_ref, q_ref, k_ref, v_ref):
        cqn_b, ckvn_b, krr_b = cqn_ref[...], ckvn_ref[...], krr_ref[...]
        cc_b, sn_b, sp_b = cc_ref[...], sn_ref[...], sp_ref[...]
        for h in range(N_HEADS):
            q = _dot(cqn_b, wuq_ref[h]) * Q_SCALE
            q_ref[h, :, :NOPE] = q[:, :NOPE].astype(BF16)
            q_ref[h, :, NOPE:] = _rope(q[:, NOPE:], cc_b, sn_b, sp_b).astype(BF16)
            kv = _dot(ckvn_b, wukv_ref[h])
            k_ref[h, :, :NOPE] = kv[:, :NOPE].astype(BF16)
            k_ref[h, :, NOPE:] = krr_b
            v_ref[h] = kv[:, NOPE:].astype(BF16)

    row = lambda n: pl.BlockSpec((tm, n), lambda i: (i, 0))
    head_w = lambda a: pl.BlockSpec(a.shape, lambda i: (0, 0, 0))
    head_o = lambda n: pl.BlockSpec((N_HEADS, tm, n), lambda i: (0, i, 0))
    return _pc_carrying(
        body, carry, (cqn, ckvn, krr, w_uq, w_ukv, cc, sn, sp), name="qkv_proj", grid=(s // tm,),
        in_specs=[row(CQ), row(CKV), row(128), head_w(w_uq), head_w(w_ukv), row(128), row(128), row(128)],
        out_specs=[head_o(HEAD_PAD), head_o(HEAD_PAD), head_o(V_DIM)],
        out_shape=[_sds((N_HEADS, s, HEAD_PAD), BF16), _sds((N_HEADS, s, HEAD_PAD), BF16),
                   _sds((N_HEADS, s, V_DIM), BF16)],
    )


def _chunk_bias(bq, bk, first_key=0):
    rows = lax.broadcasted_iota(jnp.int32, (bq, bk), 0)
    cols = lax.broadcasted_iota(jnp.int32, (bq, bk), 1) + first_key
    visible = jnp.right_shift(cols, CHUNK_SHIFT) <= jnp.right_shift(rows, CHUNK_SHIFT)
    return jnp.where(visible, 0.0, NEG).astype(F32)


_ATTN_FWD_ROWS = 2048


def _attn_fwd(q, k, v, carry=None):
    s = q.shape[1]
    bk = _row_block(s)
    bq = min(_ATTN_FWD_ROWS, s)
    nd = bq // bk

    def body(q_ref, k_ref, v_ref, o_ref, lse_ref, bias_ref):
        i = pl.program_id(1)
        qb = q_ref[0]

        @pl.when((pl.program_id(0) == 0) & (i == 0))
        def _():
            bias_ref[...] = _chunk_bias(bq, bk)

        def block(j, carry, queries, bias=None):
            m, l, acc = carry
            start = pl.multiple_of(j * bk, bk)
            kb = k_ref[0, pl.ds(start, bk), :]
            vb = v_ref[0, pl.ds(start, bk), :]
            sc = _dot_nt(queries, kb)
            if bias is not None:
                sc = sc + bias
            m_new = jnp.maximum(m, jnp.max(sc, axis=-1, keepdims=True))
            p = jnp.exp2(sc - m_new)
            alpha = jnp.exp2(m - m_new)
            l = alpha * l + jnp.sum(p, axis=-1, keepdims=True)
            acc = alpha * acc + _dot(p.astype(BF16), vb)
            return m_new, l, acc

        carry = (jnp.full((bq, 1), NEG, F32), jnp.zeros((bq, 1), F32), jnp.zeros((bq, V_DIM), F32))
        carry = lax.fori_loop(0, i * nd, lambda j, c: block(j, c, qb), carry)
        for d in range(nd):
            first = d * bk
            carry = block(i * nd + d, carry, qb[first:], bias_ref[:bq - first, :])
            m, l, acc = (c[:bk] for c in carry)
            o_ref[first:first + bk, :] = (acc / l).astype(BF16)
            lse_ref[0, first:first + bk, :] = jnp.broadcast_to(m + jnp.log(l) * LOG2_E, (bk, 128))
            if d < nd - 1:
                carry = tuple(c[bk:] for c in carry)

    return _pc_carrying(
        body, carry, (q, k, v), name="attn_fwd", grid=(N_HEADS, s // bq),
        in_specs=[pl.BlockSpec((1, bq, HEAD_PAD), lambda h, i: (h, i, 0)),
                  pl.BlockSpec((1, s, HEAD_PAD), lambda h, i: (h, 0, 0)),
                  pl.BlockSpec((1, s, V_DIM), lambda h, i: (h, 0, 0))],
        out_specs=[pl.BlockSpec((bq, V_DIM), lambda h, i: (i, h)),
                   pl.BlockSpec((1, bq, 128), lambda h, i: (h, i, 0))],
        out_shape=[_sds((s, N_HEADS * V_DIM), BF16), _sds((N_HEADS, s, 128), F32)],
        scratch_shapes=[pltpu.VMEM((bq, bk), F32)],
    )


def _matmul_res(a, w, res, name):
    s, kd = a.shape
    n = w.shape[1]
    tm = min(_FFN_ROWS, s)

    def body(a_ref, w_ref, r_ref, o_ref):
        o_ref[...] = r_ref[...] + _dot(a_ref[...], w_ref[...])

    return _pc(
        body, name=name, grid=(s // tm,),
        in_specs=[pl.BlockSpec((tm, kd), lambda i: (i, 0)), pl.BlockSpec((kd, n), lambda i: (0, 0)),
                  pl.BlockSpec((tm, n), lambda i: (i, 0))],
        out_specs=pl.BlockSpec((tm, n), lambda i: (i, 0)),
        out_shape=_sds((s, n), F32),
    )(a, w, res)


def _ffn_fwd(h, gain, wg, wu, wd, name, carry=None):
    s = h.shape[0]
    tm = min(_FFN_ROWS, s)
    tf = 512
    nf = FF_PAD // tf

    def body(h_ref, g_ref, wg_ref, wu_ref, wd_ref, o_ref, n_ref, gate_ref, up_ref, acc_ref):
        j = pl.program_id(1)

        @pl.when(j == 0)
        def _():
            n_ref[...] = _rms(h_ref[...], g_ref[...]).astype(BF16)
            acc_ref[...] = jnp.zeros_like(acc_ref)

        nb = n_ref[...]
        gate = _dot(nb, wg_ref[...])
        up = _dot(nb, wu_ref[...])
        gate_ref[...] = gate.astype(BF16)
        up_ref[...] = up.astype(BF16)
        act = gate * jax.nn.sigmoid(gate) * up
        acc_ref[...] += _dot(act.astype(BF16), wd_ref[...])

        @pl.when(j == nf - 1)
        def _():
            o_ref[...] = h_ref[...] + acc_ref[...]

    return _pc_carrying(
        body, carry, (h, gain, wg, wu, wd), name=name, grid=(s // tm, nf),
        in_specs=[pl.BlockSpec((tm, D_MODEL), lambda i, j: (i, 0)), pl.BlockSpec((1, D_MODEL), lambda i, j: (0, 0)),
                  pl.BlockSpec((D_MODEL, tf), lambda i, j: (0, j)), pl.BlockSpec((D_MODEL, tf), lambda i, j: (0, j)),
                  pl.BlockSpec((tf, D_MODEL), lambda i, j: (j, 0))],
        out_specs=[pl.BlockSpec((tm, D_MODEL), lambda i, j: (i, 0)), pl.BlockSpec((tm, D_MODEL), lambda i, j: (i, 0)),
                   pl.BlockSpec((tm, tf), lambda i, j: (i, j)), pl.BlockSpec((tm, tf), lambda i, j: (i, j))],
        out_shape=[_sds((s, D_MODEL), F32), _sds((s, D_MODEL), BF16), _sds((s, FF_PAD), BF16),
                   _sds((s, FF_PAD), BF16)],
        scratch_shapes=[pltpu.VMEM((tm, D_MODEL), F32)],
    )


def _rms_matmul(h, gain, w, name):
    s = h.shape[0]
    n = w.shape[1]
    tm = min(_FFN_ROWS, s)
    tn = 1024
    nn = n // tn

    def body(h_ref, g_ref, w_ref, n_ref, o_ref):
        @pl.when(pl.program_id(1) == 0)
        def _():
            n_ref[...] = _rms(h_ref[...], g_ref[...]).astype(BF16)

        o_ref[...] = _dot(n_ref[...], w_ref[...]).astype(BF16)

    return _pc(
        body, name=name, grid=(s // tm, nn),
        in_specs=[pl.BlockSpec((tm, D_MODEL), lambda i, j: (i, 0)), pl.BlockSpec((1, D_MODEL), lambda i, j: (0, 0)),
                  pl.BlockSpec((D_MODEL, tn), lambda i, j: (0, j))],
        out_specs=[pl.BlockSpec((tm, D_MODEL), lambda i, j: (i, 0)), pl.BlockSpec((tm, tn), lambda i, j: (i, j))],
        out_shape=[_sds((s, D_MODEL), BF16), _sds((s, n), BF16)],
    )(h, gain, w)


def _shift_down(u, k, rows):
    return jnp.where(rows >= k, pltpu.roll(u, k, 0), 0.0)


def _shift_up(u, k, rows, s):
    return jnp.where(rows < s - k, pltpu.roll(u, s - k, 0), 0.0)


_CONV_COLS = 128


def _conv_fwd(bcx, cw):
    s = bcx.shape[0]
    tc = _CONV_COLS
    nc = D_MODEL // tc

    def body(b_ref, c_ref, xp_ref, w_ref, o_ref):
        rows = lax.broadcasted_iota(jnp.int32, (s, tc), 0)
        u = c_ref[...].astype(F32) * xp_ref[...].astype(F32)
        w = w_ref[...]
        uc = w[2:3] * u + w[1:2] * _shift_down(u, 1, rows) + w[0:1] * _shift_down(u, 2, rows)
        o_ref[...] = (b_ref[...].astype(F32) * uc).astype(BF16)

    col = lambda off: pl.BlockSpec((s, tc), lambda j: (0, off + j))
    return _pc(
        body, name="conv_fwd", grid=(nc,),
        in_specs=[col(0), col(nc), col(2 * nc), pl.BlockSpec((3, tc), lambda j: (0, j))],
        out_specs=pl.BlockSpec((s, tc), lambda j: (0, j)),
        out_shape=_sds((s, D_MODEL), BF16),
    )(bcx, bcx, bcx, cw)


def _final_loss(h, gain, target):
    s = h.shape[0]
    tm = _row_block(s)

    def body(h_ref, g_ref, t_ref, dh_ref, dg_ref, loss_ref):
        i = pl.program_id(0)

        @pl.when(i == 0)
        def _():
            dg_ref[...] = jnp.zeros_like(dg_ref)
            loss_ref[...] = jnp.zeros_like(loss_ref)

        hb = h_ref[...]
        e = _rms(hb, g_ref[...]) - t_ref[...]
        loss_ref[...] += 0.5 * jnp.sum(jnp.mean(e * e, axis=-1, keepdims=True))
        dx, dg = _rms_bwd(e * (1.0 / D_MODEL), hb, g_ref[...])
        dh_ref[...] = dx
        dg_ref[...] += dg

    row = pl.BlockSpec((tm, D_MODEL), lambda i: (i, 0))
    vec = pl.BlockSpec((1, D_MODEL), lambda i: (0, 0))
    return _pc(
        body, name="final_loss", grid=(s // tm,),
        in_specs=[row, vec, row],
        out_specs=[row, vec, pl.BlockSpec((8, D_MODEL), lambda i: (0, 0))],
        out_shape=[_sds((s, D_MODEL), F32), _sds((1, D_MODEL), F32), _sds((8, D_MODEL), F32)],
    )(h, gain, target)


def _ffn_bwd_x(dho, h, gain, gate, up, wg, wu, wd, name, carry=None):
    s = h.shape[0]
    tm = _row_block(s)
    tf = 1024
    nf = FF_PAD // tf

    def body(dho_ref, h_ref, g_ref, gate_ref, up_ref, wg_ref, wu_ref, wd_ref,
             dh_ref, dhb_ref, dhob_ref, dgate_ref, dup_ref, act_ref, dgain_ref, acc_ref):
        i = pl.program_id(0)
        j = pl.program_id(1)

        @pl.when(j == 0)
        def _():
            dhob_ref[...] = dho_ref[...].astype(BF16)
            acc_ref[...] = jnp.zeros_like(acc_ref)

        @pl.when((i == 0) & (j == 0))
        def _():
            dgain_ref[...] = jnp.zeros_like(dgain_ref)

        dact = _dot_nt(dhob_ref[...], wd_ref[...])
        g = gate_ref[...].astype(F32)
        u = up_ref[...].astype(F32)
        sg = jax.nn.sigmoid(g)
        silu = g * sg
        dg = (dact * u * (sg * (1.0 + g * (1.0 - sg)))).astype(BF16)
        du = (dact * silu).astype(BF16)
        dgate_ref[...] = dg
        dup_ref[...] = du
        act_ref[...] = (silu * u).astype(BF16)
        acc_ref[...] += _dot_nt(dg, wg_ref[...]) + _dot_nt(du, wu_ref[...])

        @pl.when(j == nf - 1)
        def _():
            dx, dgain = _rms_bwd(acc_ref[...], h_ref[...], g_ref[...])
            dh = dho_ref[...] + dx
            dh_ref[...] = dh
            dhb_ref[...] = dh.astype(BF16)
            dgain_ref[...] += dgain

    row = pl.BlockSpec((tm, D_MODEL), lambda i, j: (i, 0))
    vec = pl.BlockSpec((1, D_MODEL), lambda i, j: (0, 0))
    hid = pl.BlockSpec((tm, tf), lambda i, j: (i, j))
    wcol = pl.BlockSpec((D_MODEL, tf), lambda i, j: (0, j))
    wrow = pl.BlockSpec((tf, D_MODEL), lambda i, j: (j, 0))
    return _pc_carrying(
        body, carry, (dho, h, gain, gate, up, wg, wu, wd), name=name, grid=(s // tm, nf),
        in_specs=[row, row, vec, hid, hid, wcol, wcol, wrow],
        out_specs=[row, row, row, hid, hid, hid, vec],
        out_shape=[_sds((s, D_MODEL), F32), _sds((s, D_MODEL), BF16), _sds((s, D_MODEL), BF16),
                   _sds((s, FF_PAD), BF16), _sds((s, FF_PAD), BF16), _sds((s, FF_PAD), BF16),
                   _sds((1, D_MODEL), F32)],
        scratch_shapes=[pltpu.VMEM((tm, D_MODEL), F32)],
    )


def _nt_rmsbwd(a, w, h, gain, dho, name, carry=None):
    stacked = a.ndim == 3
    if stacked:
        nk, s, tk = a.shape
    else:
        s, tk = a.shape
        nk = 1
    tm = min(_FFN_ROWS, s) if stacked else _row_block(s)

    def body(a_ref, w_ref, h_ref, g_ref, dho_ref, dh_ref, dhb_ref, dgain_ref, acc_ref):
        i = pl.program_id(0)
        j = pl.program_id(1)
        if stacked:
            a_ref = a_ref.at[0]

        @pl.when(j == 0)
        def _():
            acc_ref[...] = jnp.zeros_like(acc_ref)

        @pl.when((i == 0) & (j == 0))
        def _():
            dgain_ref[...] = jnp.zeros_like(dgain_ref)

        acc_ref[...] += _dot_nt(a_ref[...], w_ref[...])

        @pl.when(j == nk - 1)
        def _():
            dx, dgain = _rms_bwd(acc_ref[...], h_ref[...], g_ref[...])
            dh = dho_ref[...] + dx
            dh_ref[...] = dh
            dhb_ref[...] = dh.astype(BF16)
            dgain_ref[...] += dgain

    row = pl.BlockSpec((tm, D_MODEL), lambda i, j: (i, 0))
    vec = pl.BlockSpec((1, D_MODEL), lambda i, j: (0, 0))
    return _pc_carrying(
        body, carry, (a, w, h, gain, dho), name=name, grid=(s // tm, nk),
        in_specs=[pl.BlockSpec((1, tm, tk), lambda i, j: (j, i, 0)) if stacked else pl.BlockSpec((tm, tk), lambda i, j: (i, 0)),
                  pl.BlockSpec((D_MODEL, tk), lambda i, j: (0, j)), row, vec, row],
        out_specs=[row, row, vec],
        out_shape=[_sds((s, D_MODEL), F32), _sds((s, D_MODEL), BF16), _sds((1, D_MODEL), F32)],
        scratch_shapes=[pltpu.VMEM((tm, D_MODEL), F32)],
    )


def _matmul_nt(a, w, name):
    s, kd = a.shape
    n = w.shape[0]
    tm = min(_FFN_ROWS, s)

    def body(a_ref, w_ref, o_ref):
        o_ref[...] = _dot_nt(a_ref[...], w_ref[...]).astype(BF16)

    return _pc(
        body, name=name, grid=(s // tm,),
        in_specs=[pl.BlockSpec((tm, kd), lambda i: (i, 0)), pl.BlockSpec((n, kd), lambda i: (0, 0))],
        out_specs=pl.BlockSpec((tm, n), lambda i: (i, 0)),
        out_shape=_sds((s, n), BF16),
    )(a, w)


def _matmul_tn(a, b, name, carry=None):
    s, m = a.shape
    stacked = b.ndim == 3
    tmm = min(m, 1024)
    if stacked:
        n, tn = b.shape[0] * b.shape[2], b.shape[2]
    else:
        n = b.shape[1]
        tn = n if n <= 1024 else 1024
    tk = min(_TN_ROWS, s)
    nk = s // tk

    def body(a_ref, b_ref, o_ref, acc_ref):
        k = pl.program_id(2)
        if stacked:
            b_ref = b_ref.at[0]

        @pl.when(k == 0)
        def _():
            acc_ref[...] = jnp.zeros_like(acc_ref)

        acc_ref[...] += _dot_tn(a_ref[...], b_ref[...])

        @pl.when(k == nk - 1)
        def _():
            o_ref[...] = acc_ref[...].astype(BF16)

    (out,), got = _pc_carrying(
        body, carry, (a, b), name=name, grid=(m // tmm, n // tn, nk),
        in_specs=[pl.BlockSpec((tk, tmm), lambda i, j, k: (k, i)),
                  pl.BlockSpec((1, tk, tn), lambda i, j, k: (j, k, 0)) if stacked
                  else pl.BlockSpec((tk, tn), lambda i, j, k: (k, j))],
        out_specs=[pl.BlockSpec((tmm, tn), lambda i, j, k: (i, j))],
        out_shape=[_sds((m, n), BF16)],
        scratch_shapes=[pltpu.VMEM((tmm, tn), F32)],
    )
    return out if carry is None else (out, got)


def _conv_bwd(dbuc, bcx, cw):
    s = bcx.shape[0]
    tc = _CONV_COLS
    nc = D_MODEL // tc

    def body(d_ref, b_ref, c_ref, xp_ref, w_ref, dbcx_ref, dw_ref):
        rows = lax.broadcasted_iota(jnp.int32, (s, tc), 0)
        c = c_ref[...].astype(F32)
        xp = xp_ref[...].astype(F32)
        u = c * xp
        u1 = _shift_down(u, 1, rows)
        u2 = _shift_down(u, 2, rows)
        w = w_ref[...]
        uc = w[2:3] * u + w[1:2] * u1 + w[0:1] * u2
        d = d_ref[...].astype(F32)
        dbcx_ref[0] = (d * uc).astype(BF16)
        duc = d * b_ref[...].astype(F32)
        du = w[2:3] * duc + w[1:2] * _shift_up(duc, 1, rows, s) + w[0:1] * _shift_up(duc, 2, rows, s)
        dbcx_ref[1] = (du * xp).astype(BF16)
        dbcx_ref[2] = (du * c).astype(BF16)
        dw_ref[0:1, :] = jnp.sum(duc * u2, axis=0, keepdims=True)
        dw_ref[1:2, :] = jnp.sum(duc * u1, axis=0, keepdims=True)
        dw_ref[2:3, :] = jnp.sum(duc * u, axis=0, keepdims=True)

    col = lambda off: pl.BlockSpec((s, tc), lambda j: (0, off + j))
    tap = pl.BlockSpec((3, tc), lambda j: (0, j))
    return _pc(
        body, name="conv_bwd", grid=(nc,),
        in_specs=[col(0), col(0), col(nc), col(2 * nc), tap],
        out_specs=[pl.BlockSpec((3, s, tc), lambda j: (0, 0, j)), tap],
        out_shape=[_sds((3, s, D_MODEL), BF16), _sds((3, D_MODEL), F32)],
    )(dbuc, bcx, bcx, bcx, cw)


def _attn_bwd(q, k, v, do, o, lse, cc, sn, sp, carry=None):
    s = q.shape[1]
    bk = _row_block(s)
    nb = s // bk

    def body(q_ref, k_ref, v_ref, do_ref, o_ref, lse_ref, cc_ref, sn_ref, sp_ref,
             dq_ref, dk_ref, dv_ref, dqacc_ref, delta_ref, bias_ref):
        j = pl.program_id(1)
        kb = k_ref[0]
        vb = v_ref[0]

        @pl.when((pl.program_id(0) == 0) & (j == 0))
        def _():
            bias_ref[...] = _chunk_bias(2 * bk, bk)

        @pl.when(j == 0)
        def _():
            dqacc_ref[...] = jnp.zeros_like(dqacc_ref)

            def fill(i, _):
                rows = pl.ds(pl.multiple_of(i * bk, bk), bk)
                d = jnp.sum(do_ref[rows, :].astype(F32) * o_ref[rows, :].astype(F32), axis=-1, keepdims=True)
                delta_ref[rows, :] = jnp.broadcast_to(d, (bk, 128))
                return 0

            lax.fori_loop(0, nb, fill, 0)

        def step(i, carry, masked, blocks=1):
            dk, dv = carry
            rows = pl.ds(pl.multiple_of(i * bk, bk), blocks * bk)
            qb = q_ref[0, rows, :]
            dob = do_ref[rows, :]
            sc = _dot_nt(qb, kb)
            if masked:
                sc = sc + bias_ref[:blocks * bk, :]
            p = jnp.exp2(sc - lse_ref[0, rows, :][:, :1])
            dv = dv + _dot_tn(p.astype(BF16), dob)
            ds = (p * (_dot_nt(dob, vb) - delta_ref[rows, :][:, :1])).astype(BF16)
            dk = dk + _dot_tn(ds, qb)
            dqacc_ref[rows, :] += _dot(ds, kb)
            return dk, dv

        rest = nb - 1 - j
        odd = lax.rem(rest, 2)
        carry = (jnp.zeros((bk, HEAD_PAD), F32), jnp.zeros((bk, V_DIM), F32))
        carry = lax.fori_loop(0, 1 - odd, lambda t, c: step(j, c, True), carry)
        if nb > 1:
            carry = lax.fori_loop(0, odd, lambda t, c: step(j, c, True, blocks=2), carry)
        dq = dqacc_ref[pl.ds(pl.multiple_of(j * bk, bk), bk), :] * ATT_SCALE
        dq_ref[:, :NOPE] = dq[:, :NOPE].astype(BF16)
        dq_ref[:, NOPE:] = _rope_bwd(dq[:, NOPE:], cc_ref[...], sn_ref[...], sp_ref[...]).astype(BF16)
        if nb > 1:
            carry = lax.fori_loop(
                0, rest // 2, lambda t, c: step(j + 1 + odd + 2 * t, c, False, blocks=2), carry)
        dk, dv = carry
        dk_ref[0] = (dk * LN_2).astype(BF16)
        dv_ref[0] = dv.astype(BF16)

    blk = lambda n: pl.BlockSpec((1, bk, n), lambda h, j: (h, j, 0))
    whole = lambda n: pl.BlockSpec((1, s, n), lambda h, j: (h, 0, 0))
    cols = pl.BlockSpec((s, V_DIM), lambda h, j: (0, h))
    tab = pl.BlockSpec((bk, 128), lambda h, j: (j, 0))
    return _pc_carrying(
        body, carry, (q, k, v, do, o, lse, cc, sn, sp), name="attn_bwd", grid=(N_HEADS, nb),
        in_specs=[whole(HEAD_PAD), blk(HEAD_PAD), blk(V_DIM), cols, cols, whole(128), tab, tab, tab],
        out_specs=[pl.BlockSpec((bk, HEAD_PAD), lambda h, j: (j, h)), blk(HEAD_PAD), blk(V_DIM)],
        out_shape=[_sds((s, N_HEADS * HEAD_PAD), BF16), _sds((N_HEADS, s, HEAD_PAD), BF16),
                   _sds((N_HEADS, s, V_DIM), BF16)],
        scratch_shapes=[pltpu.VMEM((s, HEAD_PAD), F32), pltpu.VMEM((s, 128), F32), pltpu.VMEM((2 * bk, bk), F32)],
    )


def _mla_mid_bwd(dq, dk, dv, proj, w_uq, w_ukv, g_cq, g_ckv, cc, sn, sp, carry=None):
    s = dq.shape[0]
    tm = _row_block(s)

    def body(dq_ref, dk_ref, dv_ref, proj_ref, wuq_ref, wukv_ref, gcq_ref, gckv_ref, cc_ref, sn_ref, sp_ref,
             dproj_ref, dkv_ref, dgcq_ref, dgckv_ref, acq_ref, ackv_ref, akr_ref):
        @pl.when(pl.program_id(0) == 0)
        def _():
            dgcq_ref[...] = jnp.zeros_like(dgcq_ref)
            dgckv_ref[...] = jnp.zeros_like(dgckv_ref)

        for h in range(N_HEADS):
            cols = slice(h * HEAD_PAD, (h + 1) * HEAD_PAD)
            dkb = dk_ref[h]
            dkv_ref[:, h * HEAD_PAD:h * HEAD_PAD + NOPE] = dkb[:, :NOPE]
            dkv_ref[:, h * HEAD_PAD + NOPE:(h + 1) * HEAD_PAD] = dv_ref[h]
            cq_part = _dot_nt(dq_ref[:, cols], wuq_ref[h])
            ckv_part = _dot_nt(dkv_ref[:, cols], wukv_ref[h])
            kr_part = dkb[:, NOPE:].astype(F32)
            if h == 0:
                acq_ref[...], ackv_ref[...], akr_ref[...] = cq_part, ckv_part, kr_part
            else:
                acq_ref[...] += cq_part
                ackv_ref[...] += ckv_part
                akr_ref[...] += kr_part

        proj = proj_ref[...]
        dcq, dgcq = _rms_bwd(acq_ref[...], proj[:, :CQ], gcq_ref[...])
        dckv, dgckv = _rms_bwd(ackv_ref[...], proj[:, CQ:CQ + CKV], gckv_ref[...])
        dproj_ref[:, :CQ] = dcq.astype(BF16)
        dproj_ref[:, CQ:CQ + CKV] = dckv.astype(BF16)
        dproj_ref[:, CQ + CKV:] = _rope_bwd(akr_ref[...], cc_ref[...], sn_ref[...], sp_ref[...]).astype(BF16)
        dgcq_ref[...] += dgcq
        dgckv_ref[...] += dgckv

    head_blk = lambda n: pl.BlockSpec((N_HEADS, tm, n), lambda i: (0, i, 0))
    head_cols = pl.BlockSpec((tm, N_HEADS * HEAD_PAD), lambda i: (i, 0))
    head_w = lambda a: pl.BlockSpec(a.shape, lambda i: (0, 0, 0))
    row = lambda n: pl.BlockSpec((tm, n), lambda i: (i, 0))
    vec = lambda n: pl.BlockSpec((1, n), lambda i: (0, 0))
    return _pc_carrying(
        body, carry, (dq, dk, dv, proj, w_uq, w_ukv, g_cq, g_ckv, cc, sn, sp),
        name="mla_mid_bwd", grid=(s // tm,),
        in_specs=[head_cols, head_blk(HEAD_PAD), head_blk(V_DIM), row(PROJ_PAD), head_w(w_uq), head_w(w_ukv),
                  vec(CQ), vec(CKV), row(128), row(128), row(128)],
        out_specs=[row(PROJ_PAD), head_cols, vec(CQ), vec(CKV)],
        out_shape=[_sds((s, PROJ_PAD), BF16), _sds((s, N_HEADS * HEAD_PAD), BF16), _sds((1, CQ), F32),
                   _sds((1, CKV), F32)],
        scratch_shapes=[pltpu.VMEM((tm, CQ), F32), pltpu.VMEM((tm, CKV), F32), pltpu.VMEM((tm, 128), F32)],
    )


def _peer(k):
    x, y, c = lax.axis_index("x"), lax.axis_index("y"), lax.axis_index("c")
    px = 1 - x if k & 4 else x
    py = 1 - y if k & 2 else y
    pc = 1 - c if k & 1 else c
    return (px, py, pc), 4 * px + 2 * py + pc


def _exchange_ops(arrays, scatter):
    nw = len(arrays)
    by_cols = [isinstance(a, _Cols) for a in arrays]
    used_rows = [a.rows if isinstance(a, _Rows) else None for a in arrays]
    arrays = [a.array if isinstance(a, (_Cols, _Rows)) else a for a in arrays]
    direct = range(1, N_DEV) if scatter else (1, 2, 4, 6)
    assert scatter or not any(used_rows)

    def columns(ref, idx, width):
        return ref.at[:, pl.ds(pl.multiple_of(idx * width, 128), width)]

    def sent(ins, w, idx):
        if not scatter:
            return ins[w]
        if by_cols[w]:
            return columns(ins[w], idx, arrays[w].shape[1] // N_DEV)
        return ins[w].at[idx, pl.ds(0, used_rows[w])] if used_rows[w] else ins[w].at[idx]

    def slot(outs, w, idx):
        if by_cols[w] and not scatter:
            return columns(outs[w], idx, arrays[w].shape[1])
        return outs[w].at[idx]

    def copy(w, k, src, dst, to, send_sems, recv_sems):
        return pltpu.make_async_remote_copy(
            src_ref=src, dst_ref=dst, send_sem=send_sems.at[w * N_DEV + k], recv_sem=recv_sems.at[w * N_DEV + k],
            device_id=to, device_id_type=pl.DeviceIdType.MESH)

    def own_copies(ins, outs, local_sems):
        _, me = _peer(0)
        return [pltpu.make_async_copy(sent(ins, w, me), slot(outs, w, me), local_sems.at[w]) for w in range(nw)]

    def sends(ins, outs, send_sems, recv_sems):
        _, me = _peer(0)
        out = []
        for k in direct:
            dev, idx = _peer(k)
            for w in range(nw):
                out.append(copy(w, k, sent(ins, w, idx), slot(outs, w, me), dev, send_sems, recv_sems))
        return out

    def relays(outs, send_sems, recv_sems):
        sibling, _ = _peer(1)
        out = []
        for k in (2, 4, 6):
            _, idx = _peer(k)
            for w in range(nw):
                out.append(copy(w, k + 1, slot(outs, w, idx), slot(outs, w, idx), sibling, send_sems, recv_sems))
        return out

    def arrival(outs, w, k, send_sems, recv_sems):
        dev, idx = _peer(k)
        return copy(w, k, slot(outs, w, idx), slot(outs, w, idx), dev, send_sems, recv_sems)

    def start(ins, outs, send_sems, recv_sems, local_sems):
        for cp in own_copies(ins, outs, local_sems) + sends(ins, outs, send_sems, recv_sems):
            cp.start()

    def relay(ins, outs, send_sems, recv_sems, local_sems):
        for k in (2, 4, 6):
            for w in range(nw):
                arrival(outs, w, k, send_sems, recv_sems).wait_recv()
        for cp in relays(outs, send_sems, recv_sems):
            cp.start()

    def wait(ins, outs, send_sems, recv_sems, local_sems):
        for cp in own_copies(ins, outs, local_sems):
            cp.wait()
        for cp in sends(ins, outs, send_sems, recv_sems) + ([] if scatter else relays(outs, send_sems, recv_sems)):
            cp.wait_send()
        for k in (range(1, N_DEV) if scatter else (1, 3, 5, 7)):
            for w in range(nw):
                arrival(outs, w, k, send_sems, recv_sems).wait_recv()

    def landed_shape(a, cols, rows):
        if rows:
            return (N_DEV, rows) + a.shape[2:]
        if not cols:
            return a.shape if scatter else (N_DEV,) + a.shape
        r, c = a.shape
        return (N_DEV, r, c // N_DEV) if scatter else (r, N_DEV * c)

    landed = [_sds(landed_shape(a, cols, rows), a.dtype) for a, cols, rows in zip(arrays, by_cols, used_rows)]
    sems = [pltpu.SemaphoreType.DMA((nw * N_DEV,)), pltpu.SemaphoreType.DMA((nw * N_DEV,)),
            pltpu.SemaphoreType.DMA((nw,))]
    return start, (None if scatter else relay), wait, landed, sems, arrays


class _Cols:
    def __init__(self, array):
        self.array = array


class _Rows:
    def __init__(self, array, rows):
        self.array, self.rows = array, rows


def _exchange(arrays, scatter, name):
    nw = len(arrays)
    start, relay, wait, landed, sems, arrays = _exchange_ops(arrays, scatter)

    def body(*refs):
        ins, outs, csem = refs[:nw], refs[nw:2 * nw], refs[2 * nw:]
        start(ins, outs, *csem)
        if relay is not None:
            relay(ins, outs, *csem)
        wait(ins, outs, *csem)

    any_spec = pl.BlockSpec(memory_space=pl.ANY)
    return _pc(body, name=name, in_specs=[any_spec] * nw, out_specs=[any_spec] * nw, out_shape=landed,
               scratch_shapes=sems)(*arrays)


def _adam_math(g, w, m, v):
    m = ADAM_B1 * m + (1.0 - ADAM_B1) * g
    v = ADAM_B2 * v + (1.0 - ADAM_B2) * jnp.square(g)
    m_hat = m / (1.0 - ADAM_B1 ** ADAM_STEP)
    v_hat = v / (1.0 - ADAM_B2 ** ADAM_STEP)
    delta = -ADAM_LR * (m_hat / (jnp.sqrt(v_hat) + ADAM_EPS) + ADAM_WD * w)
    return delta, m, v


def _adam_rows(r):
    for t in range(min(r, 512) // 16 * 16, 0, -16):
        if r % t == 0:
            return t
    return r


def _adamw(parts, w, m, v, name):
    nl, r, c = w.shape
    tr = _adam_rows(r)
    nr = r // tr

    def body(*refs):
        p_refs, (w_ref, m_ref, v_ref, g_ref, d_ref, mo_ref, vo_ref) = refs[:nl], refs[nl:]
        for layer in range(nl):
            @pl.when(pl.program_id(0) == layer)
            def _(p_ref=p_refs[layer]):
                g = p_ref[0].astype(F32)
                for src in range(1, N_DEV):
                    g = g + p_ref[src].astype(F32)
                delta, m2, v2 = _adam_math(g, w_ref[0], m_ref[0], v_ref[0])
                g_ref[0] = g
                d_ref[0] = delta
                mo_ref[0] = m2
                vo_ref[0] = v2

    def part_spec(layer):
        def index(l, i):
            return 0, jnp.where(l == layer, i, jnp.where(l < layer, 0, nr - 1)), 0
        return pl.BlockSpec((N_DEV, tr, c), index)

    blk = pl.BlockSpec((1, tr, c), lambda l, i: (l, i, 0))
    return _pc(
        body, name=name, grid=(nl, nr),
        in_specs=[part_spec(layer) for layer in range(nl)] + [blk, blk, blk],
        out_specs=[blk, blk, blk, blk],
        out_shape=[_sds((nl, r, c), F32)] * 4,
    )(*parts, w, m, v)


def _small_allreduce_adamw(gpack, wpack, mpack, vpack, last_grads):
    shape = gpack.shape
    compact = (SMALL_ROWS, D_MODEL)
    nw = len(last_grads)
    start, _, wait, landed, exchange_sems, last_grads = _exchange_ops(last_grads, True)

    def body(*refs):
        (g_ref, w_ref, m_ref, v_ref), rest = refs[:4], refs[4:]
        cin, rest = rest[:nw], rest[nw:]
        (go_ref, d_ref, mo_ref, vo_ref), rest = rest[:4], rest[4:]
        cout, rest = rest[:nw], rest[nw:]
        (comp_ref, gath_ref, send_sems, recv_sems), csem = rest[:4], rest[4:]
        start(cin, cout, *csem)
        _, me = _peer(0)
        comp_ref[...] = jnp.zeros(compact, F32)
        r = 0
        for p, n in enumerate(SMALL_PIECES):
            comp_ref[r:r + n, :] = g_ref[8 * p:8 * p + n, :]
            r += n
        gath_ref[me] = comp_ref[...]
        copies = []
        for k in range(1, N_DEV):
            dev, idx = _peer(k)
            copies.append(pltpu.make_async_remote_copy(
                src_ref=comp_ref, dst_ref=gath_ref.at[me], send_sem=send_sems.at[k], recv_sem=recv_sems.at[k],
                device_id=dev, device_id_type=pl.DeviceIdType.MESH))
        for cp in copies:
            cp.start()
        for cp in copies:
            cp.wait_send()
        for k in range(1, N_DEV):
            dev, idx = _peer(k)
            pltpu.make_async_remote_copy(
                src_ref=comp_ref, dst_ref=gath_ref.at[idx], send_sem=send_sems.at[k], recv_sem=recv_sems.at[k],
                device_id=dev, device_id_type=pl.DeviceIdType.MESH).wait_recv()
        g = gath_ref[0]
        for src in range(1, N_DEV):
            g = g + gath_ref[src]
        go_ref[...] = jnp.zeros(shape, F32)
        r = 0
        for p, n in enumerate(SMALL_PIECES):
            go_ref[8 * p:8 * p + n, :] = g[r:r + n, :]
            r += n
        delta, m2, v2 = _adam_math(go_ref[...], w_ref[...], m_ref[...], v_ref[...])
        d_ref[...] = delta
        mo_ref[...] = m2
        vo_ref[...] = v2
        wait(cin, cout, *csem)

    vm = pl.BlockSpec(memory_space=pltpu.VMEM)
    any_spec = pl.BlockSpec(memory_space=pl.ANY)
    res = _pc(
        body, name="small_allreduce_adamw",
        in_specs=[vm] * 4 + [any_spec] * nw, out_specs=[vm] * 4 + [any_spec] * nw,
        out_shape=[_sds(shape, F32)] * 4 + landed,
        scratch_shapes=[pltpu.VMEM(compact, F32), pltpu.VMEM((N_DEV,) + compact, F32),
                        pltpu.SemaphoreType.DMA((N_DEV,)), pltpu.SemaphoreType.DMA((N_DEV,))] + exchange_sems,
    )(gpack, wpack, mpack, vpack, *last_grads)
    return res[:4], res[4:]


def _rows_from_shards(g):
    return g.reshape(N_DEV * g.shape[1], g.shape[2])


def _shards_from_rows(a):
    return a.reshape(N_DEV, a.shape[0] // N_DEV, a.shape[1])


def _pad_to(a, rows, cols):
    return jnp.pad(a, ((0, rows - a.shape[0]), (0, cols - a.shape[1])))


def _rope_tables(pos):
    inv_freq = 1.0 / (ROPE_THETA ** (jnp.arange(0, ROPE, 2, dtype=F32) / ROPE))
    ang = pos.astype(F32)[:, None] * inv_freq
    cos, sin = jnp.cos(ang), jnp.sin(ang)
    z32, z64, z96 = (jnp.zeros((pos.shape[0], n), F32) for n in (32, 64, 96))
    return (jnp.concatenate([cos, cos, z64], axis=1), jnp.concatenate([-sin, z96], axis=1),
            jnp.concatenate([z32, sin, z64], axis=1))


def _pad_row(vec):
    vec = vec.reshape(1, -1)
    return jnp.pad(vec, ((0, 0), (0, D_MODEL - vec.shape[1])))


def _forward_backward(x, target, tables, gains, shards):
    cc, sn, sp = tables
    ffn_g = gains["ffn_norm"]
    by_cols = ("wg0", "wu0", "wg1", "wu1", "c_in")

    def gather(names, relay_at):
        return [_Cols(shards[n]) if n in by_cols else shards[n] for n in names], False, relay_at

    (got,) = _exchange([shards["w_in"]], False, "gather_w_in")
    w_in = _rows_from_shards(got)
    (n0, proj, cqn, ckvn, krr), (w_uq, w_ukv) = _mla_in_fwd(
        x, gains["mla_norm"], w_in, gains["g_cq"], gains["g_ckv"], cc, sn, sp, carry=gather(["w_uq", "w_ukv"], 0.5))
    (q, k, v), (w_o, conv_norm, conv_w) = _qkv_proj(
        cqn, ckvn, krr, w_uq, w_ukv, cc, sn, sp, carry=gather(["w_o", "conv_norm", "conv_w"], 0.5))
    w_o = _rows_from_shards(w_o)
    conv_norm = conv_norm.reshape(1, D_MODEL)
    conv_w = jnp.transpose(conv_w, (1, 0, 2)).reshape(3, D_MODEL)
    (o, lse), (wg0, wu0, wd0, c_in, c_out, wd1) = _attn_fwd(
        q, k, v, carry=gather(["wg0", "wu0", "wd0", "c_in", "c_out", "wd1"], 0.75))
    wd0 = _rows_from_shards(wd0)
    wd1 = _rows_from_shards(wd1)
    c_out = _rows_from_shards(c_out)
    h1 = _matmul_res(o, w_o, x, "mla_out_fwd")
    (h2, n1, gate0, up0), (wg1, wu1) = _ffn_fwd(
        h1, ffn_g[0:1], wg0, wu0, wd0, "ffn0_fwd", carry=gather(["wg1", "wu1"], 0.7))
    n2, bcx = _rms_matmul(h2, conv_norm, c_in, "conv_in_fwd")
    bu = _conv_fwd(bcx, conv_w)
    h3 = _matmul_res(bu, c_out, h2, "conv_out_fwd")
    (h4, n3, gate1, up1), _ = _ffn_fwd(h3, ffn_g[1:2], wg1, wu1, wd1, "ffn1_fwd")
    dh4, d_final, loss = _final_loss(h4, gains["final_norm"], target)

    small = {"final_norm": d_final}
    parts = {}

    def scatter(**blocks):
        return list(blocks), (list(blocks.values()), True, None)

    (dh3, dh3_b, dh4_b, dgate, dup, act, d_ffn1), _ = _ffn_bwd_x(
        dh4, h3, ffn_g[1:2], gate1, up1, wg1, wu1, wd1, "ffn1_bwd")
    dwg1 = _matmul_tn(dgate, n3, "ffn1_dwg")
    dwu1 = _matmul_tn(dup, n3, "ffn1_dwu")
    dwd1 = _matmul_tn(act, dh4_b, "ffn1_dwd")
    dbuc = _matmul_nt(dh3_b, c_out, "conv_out_bwd")
    d_c_out = _matmul_tn(bu, dh3_b, "conv_dwout")
    dbcx, small["conv_w"] = _conv_bwd(dbuc, bcx, conv_w)
    d_c_in = _matmul_tn(n2, dbcx, "conv_dwin")
    names, carry = scatter(c_out=_shards_from_rows(d_c_out))
    (dh2, dh2_b, small["conv_norm"]), got = _nt_rmsbwd(
        dbcx, c_in, h2, conv_norm, dh3, "conv_in_bwd", carry=carry)
    parts.update(zip(names, got))
    ffn_blocks = lambda d: _Rows(_shards_from_rows(d), FF_SHARD)
    names, carry = scatter(c_in=_Cols(d_c_in), wd1=ffn_blocks(dwd1))
    (dh1, dh1_b, dh2_b2, dgate, dup, act, d_ffn0), got = _ffn_bwd_x(
        dh2, h1, ffn_g[0:1], gate0, up0, wg0, wu0, wd0, "ffn0_bwd", carry=carry)
    parts.update(zip(names, got))
    dwg0 = _matmul_tn(dgate, n1, "ffn0_dwg")
    dwu0 = _matmul_tn(dup, n1, "ffn0_dwu")
    dwd0 = _matmul_tn(act, dh2_b2, "ffn0_dwd")
    small["ffn_norm"] = jnp.pad(d_ffn0, ((0, 7), (0, 0))) + jnp.pad(d_ffn1, ((1, 6), (0, 0)))
    do = _matmul_nt(dh1_b, w_o, "mla_out_bwd")
    d_w_o = _matmul_tn(o, dh1_b, "mla_dwo")
    names, carry = scatter(**{n: ffn_blocks(d) for n, d in dict(
        wg0=dwg0, wg1=dwg1, wu0=dwu0, wu1=dwu1, wd0=dwd0).items()})
    (dq, dk, dv), got = _attn_bwd(q, k, v, do, o, lse, cc, sn, sp, carry=carry)
    parts.update(zip(names, got))
    d_w_uq = _matmul_tn(dq, cqn, "mla_dwuq")
    names, carry = scatter(w_o=_shards_from_rows(d_w_o), w_uq=_Rows(_shards_from_rows(d_w_uq), NOPE + ROPE))
    (dproj, dkv, small["g_cq"], small["g_ckv"]), got = _mla_mid_bwd(
        dq, dk, dv, proj, w_uq, w_ukv, gains["g_cq"], gains["g_ckv"], cc, sn, sp, carry=carry)
    parts.update(zip(names, got))
    d_w_ukv = _matmul_tn(ckvn, dkv, "mla_dwukv")
    names, carry = scatter(w_ukv=_Cols(d_w_ukv))
    d_w_in, got = _matmul_tn(dproj, n0, "mla_dwin", carry=carry)
    parts.update(zip(names, got))
    (dx, _, small["mla_norm"]), _ = _nt_rmsbwd(dproj, w_in, x, gains["mla_norm"], dh1, "mla_in_bwd")
    return loss, dx, parts, _Cols(d_w_in), small


def kernel(x, positions, mla_norm, mla_w_in, mla_g_cq, mla_g_ckv, mla_w_uq, mla_w_ukv, mla_w_o, conv_norm, conv_w_in, conv_w, conv_w_out, ffn_norm, ffn_w_gate, ffn_w_up, ffn_w_down, final_norm, loss_target, m_mla_norm, m_mla_w_in, m_mla_g_cq, m_mla_g_ckv, m_mla_w_uq, m_mla_w_ukv, m_mla_w_o, m_conv_norm, m_conv_w_in, m_conv_w, m_conv_w_out, m_ffn_norm, m_ffn_w_gate, m_ffn_w_up, m_ffn_w_down, m_final_norm, v_mla_norm, v_mla_w_in, v_mla_g_cq, v_mla_g_ckv, v_mla_w_uq, v_mla_w_ukv, v_mla_w_o, v_conv_norm, v_conv_w_in, v_conv_w, v_conv_w_out, v_ffn_norm, v_ffn_w_gate, v_ffn_w_up, v_ffn_w_down, v_final_norm):
    me = 4 * lax.axis_index("x") + 2 * lax.axis_index("y") + lax.axis_index("c")

    bf = lambda a, rows, cols: _pad_to(a.astype(BF16), rows, cols)
    shards = dict(
        w_in=bf(mla_w_in[0], D_MODEL // N_DEV, PROJ_PAD), w_uq=bf(mla_w_uq[0], CQ, HEAD_PAD),
        w_ukv=mla_w_ukv[0].astype(BF16), w_o=mla_w_o[0].astype(BF16),
        c_in=conv_w_in[0].astype(BF16), c_out=conv_w_out[0].astype(BF16),
        conv_norm=conv_norm, conv_w=conv_w[0])
    for l in range(2):
        shards.update({f"wg{l}": bf(ffn_w_gate[l], D_MODEL, FF_SHARD_PAD), f"wu{l}": bf(ffn_w_up[l], D_MODEL, FF_SHARD_PAD),
                       f"wd{l}": bf(ffn_w_down[l], FF_SHARD_PAD, D_MODEL)})
    gains = dict(mla_norm=mla_norm, g_cq=mla_g_cq, g_ckv=mla_g_ckv, ffn_norm=ffn_norm,
                 final_norm=final_norm.reshape(1, -1))
    loss_local, dx, parts, d_w_in, grads = _forward_backward(
        x[0], loss_target[0], _rope_tables(positions[0]), gains, shards)

    col0 = me * (D_MODEL // N_DEV)

    def place(shard):
        return lax.dynamic_update_slice(jnp.zeros((shard.shape[0], D_MODEL), F32), shard, (0, col0))

    no_loss = jnp.zeros((8, D_MODEL), F32)

    def pack(mla_n, g_cq, g_ckv, ffn_n, fin_n, conv_n, conv_taps, loss_tile):
        rows = [mla_n, _pad_row(g_cq), _pad_row(g_ckv), ffn_n, fin_n.reshape(1, -1), conv_n, conv_taps, loss_tile]
        assert all(r.shape[0] in (n, 8) for r, n in zip(rows, SMALL_PIECES))
        return jnp.concatenate([jnp.pad(r, ((0, 8 - r.shape[0]), (0, 0))) for r in rows], axis=0)

    gpack = pack(grads["mla_norm"], grads["g_cq"], grads["g_ckv"], grads["ffn_norm"], grads["final_norm"],
                 grads["conv_norm"], grads["conv_w"], loss_local)
    wpack = pack(mla_norm, mla_g_cq, mla_g_ckv, ffn_norm, final_norm, place(conv_norm), place(conv_w[0]), no_loss)
    mpack = pack(m_mla_norm, m_mla_g_cq, m_mla_g_ckv, m_ffn_norm, m_final_norm, place(m_conv_norm),
                 place(m_conv_w[0]), no_loss)
    vpack = pack(v_mla_norm, v_mla_g_cq, v_mla_g_ckv, v_ffn_norm, v_final_norm, place(v_conv_norm),
                 place(v_conv_w[0]), no_loss)
    small, (parts["w_in"],) = _small_allreduce_adamw(gpack, wpack, mpack, vpack, [d_w_in])
    loss = small[0][56, 0]

    def adamw(name, w, m, v, partials, transposed=False):
        if transposed:
            w, m, v = (jnp.swapaxes(a, 1, 2) for a in (w, m, v))
        outs = _adamw(partials, w, m, v, "adamw_" + name)
        return [jnp.swapaxes(o, 1, 2) for o in outs] if transposed else outs

    res = dict(
        w_in=adamw("w_in", mla_w_in, m_mla_w_in, v_mla_w_in, [parts["w_in"]], True),
        w_uq=adamw("w_uq", mla_w_uq, m_mla_w_uq, v_mla_w_uq, [parts["w_uq"]], True),
        w_ukv=adamw("w_ukv", mla_w_ukv, m_mla_w_ukv, v_mla_w_ukv, [parts["w_ukv"]]),
        w_o=adamw("w_o", mla_w_o, m_mla_w_o, v_mla_w_o, [parts["w_o"]]),
        c_in=adamw("c_in", conv_w_in, m_conv_w_in, v_conv_w_in, [parts["c_in"]]),
        c_out=adamw("c_out", conv_w_out, m_conv_w_out, v_conv_w_out, [parts["c_out"]]),
        wg=adamw("wg", ffn_w_gate, m_ffn_w_gate, v_ffn_w_gate, [parts["wg0"], parts["wg1"]], True),
        wu=adamw("wu", ffn_w_up, m_ffn_w_up, v_ffn_w_up, [parts["wu0"], parts["wu1"]], True),
        wd=adamw("wd", ffn_w_down, m_ffn_w_down, v_ffn_w_down, [parts["wd0"], parts["wd1"]]),
    )

    def unpack(p):
        own = lambda rows: lax.dynamic_slice(rows, (0, col0), (rows.shape[0], D_MODEL // N_DEV))
        return dict(mla_norm=p[0:1], g_cq=p[8:9, :CQ], g_ckv=p[16:17, :CKV], ffn_norm=p[24:26], final_norm=p[32],
                    conv_norm=own(p[40:41]), conv_w=own(p[48:51])[None])

    small = [unpack(p) for p in small]
    order = ["mla_norm", "w_in", "g_cq", "g_ckv", "w_uq", "w_ukv", "w_o", "conv_norm", "c_in", "conv_w", "c_out",
             "ffn_norm", "wg", "wu", "wd", "final_norm"]
    out = [loss, dx[None]]
    for kind in range(4):
        for n in order:
            out.append(res[n][kind] if n in res else small[kind][n])
    return tuple(out)
```

```python
import math

import jax
import jax.numpy as jnp
from jax import lax
from jax.experimental import pallas as pl
from jax.experimental.pallas import tpu as pltpu

F32 = jnp.float32
BF16 = jnp.bfloat16

N_DEV = 8
D_MODEL = 1024
N_HEADS = 8
NOPE = 128
ROPE = 64
V_DIM = 128
HEAD_PAD = 256
CQ = 512
CKV = 256
PROJ_PAD = CQ + CKV + 128
FF_SHARD = 352
FF_SHARD_PAD = 384
FF_PAD = FF_SHARD_PAD * N_DEV
CHUNK_SHIFT = 6
RMS_EPS = 1e-6
ROPE_THETA = 10000.0
ATT_SCALE = 1.0 / math.sqrt(NOPE + ROPE)
LOG2_E = math.log2(math.e)
LN_2 = math.log(2.0)
Q_SCALE = ATT_SCALE * LOG2_E
NEG = -1e30

ADAM_LR = 0.001
ADAM_B1 = 0.9
ADAM_B2 = 0.999
ADAM_EPS = 1e-08
ADAM_WD = 0.01
ADAM_STEP = 10

SMALL_PIECES = (1, 1, 1, 2, 1, 1, 3, 1)
SMALL_ROWS = 16

_NT = (((1,), (1,)), ((), ()))
_TN = (((0,), (0,)), ((), ()))


def _pc(body, *, name, out_shape, grid=(), in_specs=None, out_specs=None, scratch_shapes=()):
    kwargs = dict(
        name=name, out_shape=out_shape, grid=grid, scratch_shapes=scratch_shapes,
        compiler_params=pltpu.CompilerParams(),
    )
    if in_specs is not None:
        kwargs["in_specs"] = in_specs
    if out_specs is not None:
        kwargs["out_specs"] = out_specs
    return pl.pallas_call(body, **kwargs)


def _pc_carrying(body, carry, operands, *, name, out_shape, grid, in_specs, out_specs, scratch_shapes=()):
    if carry is None:
        return _pc(body, name=name, out_shape=out_shape, grid=grid, in_specs=in_specs, out_specs=out_specs,
                   scratch_shapes=scratch_shapes)(*operands), None
    arrays, scatter, relay_at = carry
    nw, n_in, n_out, n_scr = len(arrays), len(in_specs), len(out_shape), len(scratch_shapes)
    start, relay, wait, landed_shapes, sems, arrays = _exchange_ops(arrays, scatter)
    n_steps = math.prod(grid)
    relay_step = None if relay is None else min(int(relay_at * n_steps), n_steps - 1)

    def wrapped(*refs):
        ins, rest = refs[:n_in], refs[n_in:]
        cin, rest = rest[:nw], rest[nw:]
        outs, rest = rest[:n_out], rest[n_out:]
        cout, rest = rest[:nw], rest[nw:]
        scr, csem = rest[:n_scr], rest[n_scr:]
        step = pl.program_id(0)
        for a in range(1, len(grid)):
            step = step * grid[a] + pl.program_id(a)

        @pl.when(step == 0)
        def _():
            start(cin, cout, *csem)

        if relay is not None:
            @pl.when(step == relay_step)
            def _():
                relay(cin, cout, *csem)

        body(*ins, *outs, *scr)

        @pl.when(step == n_steps - 1)
        def _():
            wait(cin, cout, *csem)

    any_spec = pl.BlockSpec(memory_space=pl.ANY)
    res = _pc(
        wrapped, name=name, grid=grid,
        in_specs=list(in_specs) + [any_spec] * nw, out_specs=list(out_specs) + [any_spec] * nw,
        out_shape=list(out_shape) + landed_shapes, scratch_shapes=list(scratch_shapes) + sems,
    )(*operands, *arrays)
    return res[:n_out], res[n_out:]


def _sds(shape, dtype):
    return jax.ShapeDtypeStruct(shape, dtype)


def _dot(a, b):
    return jnp.dot(a, b, preferred_element_type=F32)


def _dot_nt(a, b):
    return lax.dot_general(a, b, _NT, preferred_element_type=F32)


def _dot_tn(a, b):
    return lax.dot_general(a, b, _TN, preferred_element_type=F32)


def _rstd(x):
    return lax.rsqrt(jnp.mean(x * x, axis=-1, keepdims=True) + RMS_EPS)


def _rms(x, g):
    return (x * _rstd(x)) * g


def _rms_bwd(dy, x, g):
    r = _rstd(x)
    xhat = x * r
    dxhat = dy * g
    dx = r * (dxhat - xhat * jnp.mean(dxhat * xhat, axis=-1, keepdims=True))
    return dx, jnp.sum(dy * xhat, axis=0, keepdims=True)


def _rope(t, cc, sn, sp):
    return t * cc + pltpu.roll(t, 96, 1) * sn + pltpu.roll(t, 32, 1) * sp


def _rope_bwd(dt, cc, sn, sp):
    return dt * cc + pltpu.roll(dt * sn, 32, 1) + pltpu.roll(dt * sp, 96, 1)


def _row_block(s):
    return min(512, s)


_TN_ROWS = 2048
_FFN_ROWS = 1024


def _mla_in_fwd(x, g0, w_in, g_cq, g_ckv, cc, sn, sp, carry=None):
    s = x.shape[0]
    tm = min(_FFN_ROWS, s)

    def body(x_ref, g0_ref, w_ref, gcq_ref, gckv_ref, cc_ref, sn_ref, sp_ref,
             n_ref, proj_ref, cqn_ref, ckvn_ref, krr_ref):
        nb = _rms(x_ref[...], g0_ref[...]).astype(BF16)
        n_ref[...] = nb
        proj = _dot(nb, w_ref[...])
        proj_ref[...] = proj
        cqn_ref[...] = _rms(proj[:, :CQ], gcq_ref[...]).astype(BF16)
        ckvn_ref[...] = _rms(proj[:, CQ:CQ + CKV], gckv_ref[...]).astype(BF16)
        krr_ref[...] = _rope(proj[:, CQ + CKV:], cc_ref[...], sn_ref[...], sp_ref[...]).astype(BF16)

    row = lambda n: pl.BlockSpec((tm, n), lambda i: (i, 0))
    full = lambda a: pl.BlockSpec(a.shape, lambda i: (0, 0))
    return _pc_carrying(
        body, carry, (x, g0, w_in, g_cq, g_ckv, cc, sn, sp), name="mla_in_fwd", grid=(s // tm,),
        in_specs=[row(D_MODEL), full(g0), full(w_in), full(g_cq), full(g_ckv), row(128), row(128), row(128)],
        out_specs=[row(D_MODEL), row(PROJ_PAD), row(CQ), row(CKV), row(128)],
        out_shape=[_sds((s, D_MODEL), BF16), _sds((s, PROJ_PAD), F32), _sds((s, CQ), BF16),
                   _sds((s, CKV), BF16), _sds((s, 128), BF16)],
    )


def _qkv_proj(cqn, ckvn, krr, w_uq, w_ukv, cc, sn, sp, carry=None):
    s = cqn.shape[0]
    tm = min(_FFN_ROWS, s)

    def body(cqn_ref, ckvn_ref, krr_ref, wuq_ref, wukv_ref, cc_ref, sn_ref, sp_ref, q_ref, k_ref, v_ref):
        cqn_b, ckvn_b, krr_b = cqn_ref[...], ckvn_ref[...], krr_ref[...]
        cc_b, sn_b, sp_b = cc_ref[...], sn_ref[...], sp_ref[...]
        for h in range(N_HEADS):
            q = _dot(cqn_b, wuq_ref[h]) * Q_SCALE
            q_ref[h, :, :NOPE] = q[:, :NOPE].astype(BF16)
            q_ref[h, :, NOPE:] = _rope(q[:, NOPE:], cc_b, sn_b, sp_b).astype(BF16)
            kv = _dot(ckvn_b, wukv_ref[h])
            k_ref[h, :, :NOPE] = kv[:, :NOPE].astype(BF16)
            k_ref[h, :, NOPE:] = krr_b
            v_ref[h] = kv[:, NOPE:].astype(BF16)

    row = lambda n: pl.BlockSpec((tm, n), lambda i: (i, 0))
    head_w = lambda a: pl.BlockSpec(a.shape, lambda i: (0, 0, 0))
    head_o = lambda n: pl.BlockSpec((N_HEADS, tm, n), lambda i: (0, i, 0))
    return _pc_carrying(
        body, carry, (cqn, ckvn, krr, w_uq, w_ukv, cc, sn, sp), name="qkv_proj", grid=(s // tm,),
        in_specs=[row(CQ), row(CKV), row(128), head_w(w_uq), head_w(w_ukv), row(128), row(128), row(128)],
        out_specs=[head_o(HEAD_PAD), head_o(HEAD_PAD), head_o(V_DIM)],
        out_shape=[_sds((N_HEADS, s, HEAD_PAD), BF16), _sds((N_HEADS, s, HEAD_PAD), BF16),
                   _sds((N_HEADS, s, V_DIM), BF16)],
    )


def _chunk_bias(bq, bk, first_key=0):
    rows = lax.broadcasted_iota(jnp.int32, (bq, bk), 0)
    cols = lax.broadcasted_iota(jnp.int32, (bq, bk), 1) + first_key
    visible = jnp.right_shift(cols, CHUNK_SHIFT) <= jnp.right_shift(rows, CHUNK_SHIFT)
    return jnp.where(visible, 0.0, NEG).astype(F32)


_ATTN_FWD_ROWS = 2048


def _attn_fwd(q, k, v, carry=None):
    s = q.shape[1]
    bk = _row_block(s)
    bq = min(_ATTN_FWD_ROWS, s)
    nd = bq // bk

    def body(q_ref, k_ref, v_ref, o_ref, lse_ref, bias_ref):
        i = pl.program_id(1)
        qb = q_ref[0]

        @pl.when((pl.program_id(0) == 0) & (i == 0))
        def _():
            bias_ref[...] = _chunk_bias(bq, bk)

        def block(j, carry, queries, bias=None):
            m, l, acc = carry
            start = pl.multiple_of(j * bk, bk)
            kb = k_ref[0, pl.ds(start, bk), :]
            vb = v_ref[0, pl.ds(start, bk), :]
            sc = _dot_nt(queries, kb)
            if bias is not None:
                sc = sc + bias
            m_new = jnp.maximum(m, jnp.max(sc, axis=-1, keepdims=True))
            p = jnp.exp2(sc - m_new)
            alpha = jnp.exp2(m - m_new)
            l = alpha * l + jnp.sum(p, axis=-1, keepdims=True)
            acc = alpha * acc + _dot(p.astype(BF16), vb)
            return m_new, l, acc

        carry = (jnp.full((bq, 1), NEG, F32), jnp.zeros((bq, 1), F32), jnp.zeros((bq, V_DIM), F32))
        carry = lax.fori_loop(0, i * nd, lambda j, c: block(j, c, qb), carry)
        for d in range(nd):
            first = d * bk
            carry = block(i * nd + d, carry, qb[first:], bias_ref[:bq - first, :])
            m, l, acc = (c[:bk] for c in carry)
            o_ref[first:first + bk, :] = (acc / l).astype(BF16)
            lse_ref[0, first:first + bk, :] = jnp.broadcast_to(m + jnp.log(l) * LOG2_E, (bk, 128))
            if d < nd - 1:
                carry = tuple(c[bk:] for c in carry)

    return _pc_carrying(
        body, carry, (q, k, v), name="attn_fwd", grid=(N_HEADS, s // bq),
        in_specs=[pl.BlockSpec((1, bq, HEAD_PAD), lambda h, i: (h, i, 0)),
                  pl.BlockSpec((1, s, HEAD_PAD), lambda h, i: (h, 0, 0)),
                  pl.BlockSpec((1, s, V_DIM), lambda h, i: (h, 0, 0))],
        out_specs=[pl.BlockSpec((bq, V_DIM), lambda h, i: (i, h)),
                   pl.BlockSpec((1, bq, 128), lambda h, i: (h, i, 0))],
        out_shape=[_sds((s, N_HEADS * V_DIM), BF16), _sds((N_HEADS, s, 128), F32)],
        scratch_shapes=[pltpu.VMEM((bq, bk), F32)],
    )


def _matmul_res(a, w, res, name):
    s, kd = a.shape
    n = w.shape[1]
    tm = min(_FFN_ROWS, s)

    def body(a_ref, w_ref, r_ref, o_ref):
        o_ref[...] = r_ref[...] + _dot(a_ref[...], w_ref[...])

    return _pc(
        body, name=name, grid=(s // tm,),
        in_specs=[pl.BlockSpec((tm, kd), lambda i: (i, 0)), pl.BlockSpec((kd, n), lambda i: (0, 0)),
                  pl.BlockSpec((tm, n), lambda i: (i, 0))],
        out_specs=pl.BlockSpec((tm, n), lambda i: (i, 0)),
        out_shape=_sds((s, n), F32),
    )(a, w, res)


def _ffn_fwd(h, gain, wg, wu, wd, name, carry=None):
    s = h.shape[0]
    tm = min(_FFN_ROWS, s)
    tf = 512
    nf = FF_PAD // tf

    def body(h_ref, g_ref, wg_ref, wu_ref, wd_ref, o_ref, n_ref, gate_ref, up_ref, acc_ref):
        j = pl.program_id(1)

        @pl.when(j == 0)
        def _():
            n_ref[...] = _rms(h_ref[...], g_ref[...]).astype(BF16)
            acc_ref[...] = jnp.zeros_like(acc_ref)

        nb = n_ref[...]
        gate = _dot(nb, wg_ref[...])
        up = _dot(nb, wu_ref[...])
        gate_ref[...] = gate.astype(BF16)
        up_ref[...] = up.astype(BF16)
        act = gate * jax.nn.sigmoid(gate) * up
        acc_ref[...] += _dot(act.astype(BF16), wd_ref[...])

        @pl.when(j == nf - 1)
        def _():
            o_ref[...] = h_ref[...] + acc_ref[...]

    return _pc_carrying(
        body, carry, (h, gain, wg, wu, wd), name=name, grid=(s // tm, nf),
        in_specs=[pl.BlockSpec((tm, D_MODEL), lambda i, j: (i, 0)), pl.BlockSpec((1, D_MODEL), lambda i, j: (0, 0)),
                  pl.BlockSpec((D_MODEL, tf), lambda i, j: (0, j)), pl.BlockSpec((D_MODEL, tf), lambda i, j: (0, j)),
                  pl.BlockSpec((tf, D_MODEL), lambda i, j: (j, 0))],
        out_specs=[pl.BlockSpec((tm, D_MODEL), lambda i, j: (i, 0)), pl.BlockSpec((tm, D_MODEL), lambda i, j: (i, 0)),
                   pl.BlockSpec((tm, tf), lambda i, j: (i, j)), pl.BlockSpec((tm, tf), lambda i, j: (i, j))],
        out_shape=[_sds((s, D_MODEL), F32), _sds((s, D_MODEL), BF16), _sds((s, FF_PAD), BF16),
                   _sds((s, FF_PAD), BF16)],
        scratch_shapes=[pltpu.VMEM((tm, D_MODEL), F32)],
    )


def _rms_matmul(h, gain, w, name):
    s = h.shape[0]
    n = w.shape[1]
    tm = min(_FFN_ROWS, s)
    tn = 1024
    nn = n // tn

    def body(h_ref, g_ref, w_ref, n_ref, o_ref):
        @pl.when(pl.program_id(1) == 0)
        def _():
            n_ref[...] = _rms(h_ref[...], g_ref[...]).astype(BF16)

        o_ref[...] = _dot(n_ref[...], w_ref[...]).astype(BF16)

    return _pc(
        body, name=name, grid=(s // tm, nn),
        in_specs=[pl.BlockSpec((tm, D_MODEL), lambda i, j: (i, 0)), pl.BlockSpec((1, D_MODEL), lambda i, j: (0, 0)),
                  pl.BlockSpec((D_MODEL, tn), lambda i, j: (0, j))],
        out_specs=[pl.BlockSpec((tm, D_MODEL), lambda i, j: (i, 0)), pl.BlockSpec((tm, tn), lambda i, j: (i, j))],
        out_shape=[_sds((s, D_MODEL), BF16), _sds((s, n), BF16)],
    )(h, gain, w)


def _shift_down(u, k, rows):
    return jnp.where(rows >= k, pltpu.roll(u, k, 0), 0.0)


def _shift_up(u, k, rows, s):
    return jnp.where(rows < s - k, pltpu.roll(u, s - k, 0), 0.0)


_CONV_COLS = 128


def _conv_fwd(bcx, cw):
    s = bcx.shape[0]
    tc = _CONV_COLS
    nc = D_MODEL // tc

    def body(b_ref, c_ref, xp_ref, w_ref, o_ref):
        rows = lax.broadcasted_iota(jnp.int32, (s, tc), 0)
        u = c_ref[...].astype(F32) * xp_ref[...].astype(F32)
        w = w_ref[...]
        uc = w[2:3] * u + w[1:2] * _shift_down(u, 1, rows) + w[0:1] * _shift_down(u, 2, rows)
        o_ref[...] = (b_ref[...].astype(F32) * uc).astype(BF16)

    col = lambda off: pl.BlockSpec((s, tc), lambda j: (0, off + j))
    return _pc(
        body, name="conv_fwd", grid=(nc,),
        in_specs=[col(0), col(nc), col(2 * nc), pl.BlockSpec((3, tc), lambda j: (0, j))],
        out_specs=pl.BlockSpec((s, tc), lambda j: (0, j)),
        out_shape=_sds((s, D_MODEL), BF16),
    )(bcx, bcx, bcx, cw)


def _final_loss(h, gain, target):
    s = h.shape[0]
    tm = min(_FFN_ROWS, s)

    def body(h_ref, g_ref, t_ref, dh_ref, dg_ref, loss_ref):
        i = pl.program_id(0)

        @pl.when(i == 0)
        def _():
            dg_ref[...] = jnp.zeros_like(dg_ref)
            loss_ref[...] = jnp.zeros_like(loss_ref)

        hb = h_ref[...]
        e = _rms(hb, g_ref[...]) - t_ref[...]
        loss_ref[...] += 0.5 * jnp.sum(jnp.mean(e * e, axis=-1, keepdims=True))
        dx, dg = _rms_bwd(e * (1.0 / D_MODEL), hb, g_ref[...])
        dh_ref[...] = dx
        dg_ref[...] += dg

    row = pl.BlockSpec((tm, D_MODEL), lambda i: (i, 0))
    vec = pl.BlockSpec((1, D_MODEL), lambda i: (0, 0))
    return _pc(
        body, name="final_loss", grid=(s // tm,),
        in_specs=[row, vec, row],
        out_specs=[row, vec, pl.BlockSpec((8, D_MODEL), lambda i: (0, 0))],
        out_shape=[_sds((s, D_MODEL), F32), _sds((1, D_MODEL), F32), _sds((8, D_MODEL), F32)],
    )(h, gain, target)


def _ffn_bwd_x(dho, h, gain, gate, up, wg, wu, wd, name, carry=None):
    s = h.shape[0]
    tm = _row_block(s)
    tf = 1024
    nf = FF_PAD // tf

    def body(dho_ref, h_ref, g_ref, gate_ref, up_ref, wg_ref, wu_ref, wd_ref,
             dh_ref, dhb_ref, dhob_ref, dgate_ref, dup_ref, act_ref, dgain_ref, acc_ref):
        i = pl.program_id(0)
        j = pl.program_id(1)

        @pl.when(j == 0)
        def _():
            dhob_ref[...] = dho_ref[...].astype(BF16)
            acc_ref[...] = jnp.zeros_like(acc_ref)

        @pl.when((i == 0) & (j == 0))
        def _():
            dgain_ref[...] = jnp.zeros_like(dgain_ref)

        dact = _dot_nt(dhob_ref[...], wd_ref[...])
        g = gate_ref[...].astype(F32)
        u = up_ref[...].astype(F32)
        sg = jax.nn.sigmoid(g)
        silu = g * sg
        dg = (dact * u * (sg * (1.0 + g * (1.0 - sg)))).astype(BF16)
        du = (dact * silu).astype(BF16)
        dgate_ref[...] = dg
        dup_ref[...] = du
        act_ref[...] = (silu * u).astype(BF16)
        acc_ref[...] += _dot_nt(dg, wg_ref[...]) + _dot_nt(du, wu_ref[...])

        @pl.when(j == nf - 1)
        def _():
            dx, dgain = _rms_bwd(acc_ref[...], h_ref[...], g_ref[...])
            dh = dho_ref[...] + dx
            dh_ref[...] = dh
            dhb_ref[...] = dh.astype(BF16)
            dgain_ref[...] += dgain

    row = pl.BlockSpec((tm, D_MODEL), lambda i, j: (i, 0))
    vec = pl.BlockSpec((1, D_MODEL), lambda i, j: (0, 0))
    hid = pl.BlockSpec((tm, tf), lambda i, j: (i, j))
    wcol = pl.BlockSpec((D_MODEL, tf), lambda i, j: (0, j))
    wrow = pl.BlockSpec((tf, D_MODEL), lambda i, j: (j, 0))
    return _pc_carrying(
        body, carry, (dho, h, gain, gate, up, wg, wu, wd), name=name, grid=(s // tm, nf),
        in_specs=[row, row, vec, hid, hid, wcol, wcol, wrow],
        out_specs=[row, row, row, hid, hid, hid, vec],
        out_shape=[_sds((s, D_MODEL), F32), _sds((s, D_MODEL), BF16), _sds((s, D_MODEL), BF16),
                   _sds((s, FF_PAD), BF16), _sds((s, FF_PAD), BF16), _sds((s, FF_PAD), BF16),
                   _sds((1, D_MODEL), F32)],
        scratch_shapes=[pltpu.VMEM((tm, D_MODEL), F32)],
    )


def _nt_rmsbwd(a, w, h, gain, dho, name, carry=None):
    stacked = a.ndim == 3
    if stacked:
        nk, s, tk = a.shape
    else:
        s, tk = a.shape
        nk = 1
    tm = min(_FFN_ROWS, s)

    def body(a_ref, w_ref, h_ref, g_ref, dho_ref, dh_ref, dhb_ref, dgain_ref, acc_ref):
        i = pl.program_id(0)
        j = pl.program_id(1)
        if stacked:
            a_ref = a_ref.at[0]

        @pl.when(j == 0)
        def _():
            acc_ref[...] = jnp.zeros_like(acc_ref)

        @pl.when((i == 0) & (j == 0))
        def _():
            dgain_ref[...] = jnp.zeros_like(dgain_ref)

        acc_ref[...] += _dot_nt(a_ref[...], w_ref[...])

        @pl.when(j == nk - 1)
        def _():
            dx, dgain = _rms_bwd(acc_ref[...], h_ref[...], g_ref[...])
            dh = dho_ref[...] + dx
            dh_ref[...] = dh
            dhb_ref[...] = dh.astype(BF16)
            dgain_ref[...] += dgain

    row = pl.BlockSpec((tm, D_MODEL), lambda i, j: (i, 0))
    vec = pl.BlockSpec((1, D_MODEL), lambda i, j: (0, 0))
    return _pc_carrying(
        body, carry, (a, w, h, gain, dho), name=name, grid=(s // tm, nk),
        in_specs=[pl.BlockSpec((1, tm, tk), lambda i, j: (j, i, 0)) if stacked else pl.BlockSpec((tm, tk), lambda i, j: (i, 0)),
                  pl.BlockSpec((D_MODEL, tk), lambda i, j: (0, j)), row, vec, row],
        out_specs=[row, row, vec],
        out_shape=[_sds((s, D_MODEL), F32), _sds((s, D_MODEL), BF16), _sds((1, D_MODEL), F32)],
        scratch_shapes=[pltpu.VMEM((tm, D_MODEL), F32)],
    )


def _matmul_nt(a, w, name):
    s, kd = a.shape
    n = w.shape[0]
    tm = min(_FFN_ROWS, s)

    def body(a_ref, w_ref, o_ref):
        o_ref[...] = _dot_nt(a_ref[...], w_ref[...]).astype(BF16)

    return _pc(
        body, name=name, grid=(s // tm,),
        in_specs=[pl.BlockSpec((tm, kd), lambda i: (i, 0)), pl.BlockSpec((n, kd), lambda i: (0, 0))],
        out_specs=pl.BlockSpec((tm, n), lambda i: (i, 0)),
        out_shape=_sds((s, n), BF16),
    )(a, w)


def _matmul_tn(a, b, name, carry=None):
    s, m = a.shape
    stacked = b.ndim == 3
    tmm = min(m, 1024)
    if stacked:
        n, tn = b.shape[0] * b.shape[2], b.shape[2]
    else:
        n = b.shape[1]
        tn = n if n <= 1024 else 1024
    tk = min(_TN_ROWS, s)
    nk = s // tk

    def body(a_ref, b_ref, o_ref, acc_ref):
        k = pl.program_id(2)
        if stacked:
            b_ref = b_ref.at[0]

        @pl.when(k == 0)
        def _():
            acc_ref[...] = jnp.zeros_like(acc_ref)

        acc_ref[...] += _dot_tn(a_ref[...], b_ref[...])

        @pl.when(k == nk - 1)
        def _():
            o_ref[...] = acc_ref[...].astype(BF16)

    (out,), got = _pc_carrying(
        body, carry, (a, b), name=name, grid=(m // tmm, n // tn, nk),
        in_specs=[pl.BlockSpec((tk, tmm), lambda i, j, k: (k, i)),
                  pl.BlockSpec((1, tk, tn), lambda i, j, k: (j, k, 0)) if stacked
                  else pl.BlockSpec((tk, tn), lambda i, j, k: (k, j))],
        out_specs=[pl.BlockSpec((tmm, tn), lambda i, j, k: (i, j))],
        out_shape=[_sds((m, n), BF16)],
        scratch_shapes=[pltpu.VMEM((tmm, tn), F32)],
    )
    return out if carry is None else (out, got)


def _conv_bwd(dbuc, bcx, cw):
    s = bcx.shape[0]
    tc = _CONV_COLS
    nc = D_MODEL // tc

    def body(d_ref, b_ref, c_ref, xp_ref, w_ref, dbcx_ref, dw_ref):
        rows = lax.broadcasted_iota(jnp.int32, (s, tc), 0)
        c = c_ref[...].astype(F32)
        xp = xp_ref[...].astype(F32)
        u = c * xp
        u1 = _shift_down(u, 1, rows)
        u2 = _shift_down(u, 2, rows)
        w = w_ref[...]
        uc = w[2:3] * u + w[1:2] * u1 + w[0:1] * u2
        d = d_ref[...].astype(F32)
        dbcx_ref[0] = (d * uc).astype(BF16)
        duc = d * b_ref[...].astype(F32)
        du = w[2:3] * duc + w[1:2] * _shift_up(duc, 1, rows, s) + w[0:1] * _shift_up(duc, 2, rows, s)
        dbcx_ref[1] = (du * xp).astype(BF16)
        dbcx_ref[2] = (du * c).astype(BF16)
        dw_ref[0:1, :] = jnp.sum(duc * u2, axis=0, keepdims=True)
        dw_ref[1:2, :] = jnp.sum(duc * u1, axis=0, keepdims=True)
        dw_ref[2:3, :] = jnp.sum(duc * u, axis=0, keepdims=True)

    col = lambda off: pl.BlockSpec((s, tc), lambda j: (0, off + j))
    tap = pl.BlockSpec((3, tc), lambda j: (0, j))
    return _pc(
        body, name="conv_bwd", grid=(nc,),
        in_specs=[col(0), col(0), col(nc), col(2 * nc), tap],
        out_specs=[pl.BlockSpec((3, s, tc), lambda j: (0, 0, j)), tap],
        out_shape=[_sds((3, s, D_MODEL), BF16), _sds((3, D_MODEL), F32)],
    )(dbuc, bcx, bcx, bcx, cw)


def _attn_bwd(q, k, v, do, o, lse, cc, sn, sp, carry=None):
    s = q.shape[1]
    bk = _row_block(s)
    nb = s // bk

    def body(q_ref, k_ref, v_ref, do_ref, o_ref, lse_ref, cc_ref, sn_ref, sp_ref,
             dq_ref, dk_ref, dv_ref, dqacc_ref, delta_ref, bias_ref):
        j = pl.program_id(1)
        kb = k_ref[0]
        vb = v_ref[0]

        @pl.when((pl.program_id(0) == 0) & (j == 0))
        def _():
            bias_ref[...] = _chunk_bias(2 * bk, bk)

        @pl.when(j == 0)
        def _():
            dqacc_ref[...] = jnp.zeros_like(dqacc_ref)

            def fill(i, _):
                rows = pl.ds(pl.multiple_of(i * bk, bk), bk)
                d = jnp.sum(do_ref[rows, :].astype(F32) * o_ref[rows, :].astype(F32), axis=-1, keepdims=True)
                delta_ref[rows, :] = jnp.broadcast_to(d, (bk, 128))
                return 0

            lax.fori_loop(0, nb, fill, 0)

        def step(i, carry, masked, blocks=1):
            dk, dv = carry
            rows = pl.ds(pl.multiple_of(i * bk, bk), blocks * bk)
            qb = q_ref[0, rows, :]
            dob = do_ref[rows, :]
            sc = _dot_nt(qb, kb)
            if masked:
                sc = sc + bias_ref[:blocks * bk, :]
            p = jnp.exp2(sc - lse_ref[0, rows, :][:, :1])
            dv = dv + _dot_tn(p.astype(BF16), dob)
            ds = (p * (_dot_nt(dob, vb) - delta_ref[rows, :][:, :1])).astype(BF16)
            dk = dk + _dot_tn(ds, qb)
            dqacc_ref[rows, :] += _dot(ds, kb)
            return dk, dv

        rest = nb - 1 - j
        odd = lax.rem(rest, 2)
        carry = (jnp.zeros((bk, HEAD_PAD), F32), jnp.zeros((bk, V_DIM), F32))
        carry = lax.fori_loop(0, 1 - odd, lambda t, c: step(j, c, True), carry)
        if nb > 1:
            carry = lax.fori_loop(0, odd, lambda t, c: step(j, c, True, blocks=2), carry)
        dq = dqacc_ref[pl.ds(pl.multiple_of(j * bk, bk), bk), :] * ATT_SCALE
        dq_ref[:, :NOPE] = dq[:, :NOPE].astype(BF16)
        dq_ref[:, NOPE:] = _rope_bwd(dq[:, NOPE:], cc_ref[...], sn_ref[...], sp_ref[...]).astype(BF16)
        if nb > 1:
            carry = lax.fori_loop(
                0, rest // 2, lambda t, c: step(j + 1 + odd + 2 * t, c, False, blocks=2), carry)
        dk, dv = carry
        dk_ref[0] = (dk * LN_2).astype(BF16)
        dv_ref[0] = dv.astype(BF16)

    blk = lambda n: pl.BlockSpec((1, bk, n), lambda h, j: (h, j, 0))
    whole = lambda n: pl.BlockSpec((1, s, n), lambda h, j: (h, 0, 0))
    cols = pl.BlockSpec((s, V_DIM), lambda h, j: (0, h))
    tab = pl.BlockSpec((bk, 128), lambda h, j: (j, 0))
    return _pc_carrying(
        body, carry, (q, k, v, do, o, lse, cc, sn, sp), name="attn_bwd", grid=(N_HEADS, nb),
        in_specs=[whole(HEAD_PAD), blk(HEAD_PAD), blk(V_DIM), cols, cols, whole(128), tab, tab, tab],
        out_specs=[pl.BlockSpec((bk, HEAD_PAD), lambda h, j: (j, h)), blk(HEAD_PAD), blk(V_DIM)],
        out_shape=[_sds((s, N_HEADS * HEAD_PAD), BF16), _sds((N_HEADS, s, HEAD_PAD), BF16),
                   _sds((N_HEADS, s, V_DIM), BF16)],
        scratch_shapes=[pltpu.VMEM((s, HEAD_PAD), F32), pltpu.VMEM((s, 128), F32), pltpu.VMEM((2 * bk, bk), F32)],
    )


def _mla_mid_bwd(dq, dk, dv, proj, w_uq, w_ukv, g_cq, g_ckv, cc, sn, sp, carry=None):
    s = dq.shape[0]
    tm = _row_block(s)

    def body(dq_ref, dk_ref, dv_ref, proj_ref, wuq_ref, wukv_ref, gcq_ref, gckv_ref, cc_ref, sn_ref, sp_ref,
             dproj_ref, dkv_ref, dgcq_ref, dgckv_ref, acq_ref, ackv_ref, akr_ref):
        @pl.when(pl.program_id(0) == 0)
        def _():
            dgcq_ref[...] = jnp.zeros_like(dgcq_ref)
            dgckv_ref[...] = jnp.zeros_like(dgckv_ref)

        for h in range(N_HEADS):
            cols = slice(h * HEAD_PAD, (h + 1) * HEAD_PAD)
            dkb = dk_ref[h]
            dkv_ref[:, h * HEAD_PAD:h * HEAD_PAD + NOPE] = dkb[:, :NOPE]
            dkv_ref[:, h * HEAD_PAD + NOPE:(h + 1) * HEAD_PAD] = dv_ref[h]
            cq_part = _dot_nt(dq_ref[:, cols], wuq_ref[h])
            ckv_part = _dot_nt(dkv_ref[:, cols], wukv_ref[h])
            kr_part = dkb[:, NOPE:].astype(F32)
            if h == 0:
                acq_ref[...], ackv_ref[...], akr_ref[...] = cq_part, ckv_part, kr_part
            else:
                acq_ref[...] += cq_part
                ackv_ref[...] += ckv_part
                akr_ref[...] += kr_part

        proj = proj_ref[...]
        dcq, dgcq = _rms_bwd(acq_ref[...], proj[:, :CQ], gcq_ref[...])
        dckv, dgckv = _rms_bwd(ackv_ref[...], proj[:, CQ:CQ + CKV], gckv_ref[...])
        dproj_ref[:, :CQ] = dcq.astype(BF16)
        dproj_ref[:, CQ:CQ + CKV] = dckv.astype(BF16)
        dproj_ref[:, CQ + CKV:] = _rope_bwd(akr_ref[...], cc_ref[...], sn_ref[...], sp_ref[...]).astype(BF16)
        dgcq_ref[...] += dgcq
        dgckv_ref[...] += dgckv

    head_blk = lambda n: pl.BlockSpec((N_HEADS, tm, n), lambda i: (0, i, 0))
    head_cols = pl.BlockSpec((tm, N_HEADS * HEAD_PAD), lambda i: (i, 0))
    head_w = lambda a: pl.BlockSpec(a.shape, lambda i: (0, 0, 0))
    row = lambda n: pl.BlockSpec((tm, n), lambda i: (i, 0))
    vec = lambda n: pl.BlockSpec((1, n), lambda i: (0, 0))
    return _pc_carrying(
        body, carry, (dq, dk, dv, proj, w_uq, w_ukv, g_cq, g_ckv, cc, sn, sp),
        name="mla_mid_bwd", grid=(s // tm,),
        in_specs=[head_cols, head_blk(HEAD_PAD), head_blk(V_DIM), row(PROJ_PAD), head_w(w_uq), head_w(w_ukv),
                  vec(CQ), vec(CKV), row(128), row(128), row(128)],
        out_specs=[row(PROJ_PAD), head_cols, vec(CQ), vec(CKV)],
        out_shape=[_sds((s, PROJ_PAD), BF16), _sds((s, N_HEADS * HEAD_PAD), BF16), _sds((1, CQ), F32),
                   _sds((1, CKV), F32)],
        scratch_shapes=[pltpu.VMEM((tm, CQ), F32), pltpu.VMEM((tm, CKV), F32), pltpu.VMEM((tm, 128), F32)],
    )


def _peer(k):
    x, y, c = lax.axis_index("x"), lax.axis_index("y"), lax.axis_index("c")
    px = 1 - x if k & 4 else x
    py = 1 - y if k & 2 else y
    pc = 1 - c if k & 1 else c
    return (px, py, pc), 4 * px + 2 * py + pc


def _exchange_ops(arrays, scatter):
    nw = len(arrays)
    by_cols = [isinstance(a, _Cols) for a in arrays]
    used_rows = [a.rows if isinstance(a, _Rows) else None for a in arrays]
    arrays = [a.array if isinstance(a, (_Cols, _Rows)) else a for a in arrays]
    direct = range(1, N_DEV) if scatter else (1, 2, 4, 6)
    assert scatter or not any(used_rows)

    def columns(ref, idx, width):
        return ref.at[:, pl.ds(pl.multiple_of(idx * width, 128), width)]

    def sent(ins, w, idx):
        if not scatter:
            return ins[w]
        if by_cols[w]:
            return columns(ins[w], idx, arrays[w].shape[1] // N_DEV)
        return ins[w].at[idx, pl.ds(0, used_rows[w])] if used_rows[w] else ins[w].at[idx]

    def slot(outs, w, idx):
        if by_cols[w] and not scatter:
            return columns(outs[w], idx, arrays[w].shape[1])
        return outs[w].at[idx]

    def copy(w, k, src, dst, to, send_sems, recv_sems):
        return pltpu.make_async_remote_copy(
            src_ref=src, dst_ref=dst, send_sem=send_sems.at[w * N_DEV + k], recv_sem=recv_sems.at[w * N_DEV + k],
            device_id=to, device_id_type=pl.DeviceIdType.MESH)

    def own_copies(ins, outs, local_sems):
        _, me = _peer(0)
        return [pltpu.make_async_copy(sent(ins, w, me), slot(outs, w, me), local_sems.at[w]) for w in range(nw)]

    def sends(ins, outs, send_sems, recv_sems):
        _, me = _peer(0)
        out = []
        for k in direct:
            dev, idx = _peer(k)
            for w in range(nw):
                out.append(copy(w, k, sent(ins, w, idx), slot(outs, w, me), dev, send_sems, recv_sems))
        return out

    def relays(outs, send_sems, recv_sems):
        sibling, _ = _peer(1)
        out = []
        for k in (2, 4, 6):
            _, idx = _peer(k)
            for w in range(nw):
                out.append(copy(w, k + 1, slot(outs, w, idx), slot(outs, w, idx), sibling, send_sems, recv_sems))
        return out

    def arrival(outs, w, k, send_sems, recv_sems):
        dev, idx = _peer(k)
        return copy(w, k, slot(outs, w, idx), slot(outs, w, idx), dev, send_sems, recv_sems)

    def start(ins, outs, send_sems, recv_sems, local_sems):
        for cp in own_copies(ins, outs, local_sems) + sends(ins, outs, send_sems, recv_sems):
            cp.start()

    def relay(ins, outs, send_sems, recv_sems, local_sems):
        for k in (2, 4, 6):
            for w in range(nw):
                arrival(outs, w, k, send_sems, recv_sems).wait_recv()
        for cp in relays(outs, send_sems, recv_sems):
            cp.start()

    def wait(ins, outs, send_sems, recv_sems, local_sems):
        for cp in own_copies(ins, outs, local_sems):
            cp.wait()
        for cp in sends(ins, outs, send_sems, recv_sems) + ([] if scatter else relays(outs, send_sems, recv_sems)):
            cp.wait_send()
        for k in (range(1, N_DEV) if scatter else (1, 3, 5, 7)):
            for w in range(nw):
                arrival(outs, w, k, send_sems, recv_sems).wait_recv()

    def landed_shape(a, cols, rows):
        if rows:
            return (N_DEV, rows) + a.shape[2:]
        if not cols:
            return a.shape if scatter else (N_DEV,) + a.shape
        r, c = a.shape
        return (N_DEV, r, c // N_DEV) if scatter else (r, N_DEV * c)

    landed = [_sds(landed_shape(a, cols, rows), a.dtype) for a, cols, rows in zip(arrays, by_cols, used_rows)]
    sems = [pltpu.SemaphoreType.DMA((nw * N_DEV,)), pltpu.SemaphoreType.DMA((nw * N_DEV,)),
            pltpu.SemaphoreType.DMA((nw,))]
    return start, (None if scatter else relay), wait, landed, sems, arrays


class _Cols:
    def __init__(self, array):
        self.array = array


class _Rows:
    def __init__(self, array, rows):
        self.array, self.rows = array, rows


def _exchange(arrays, scatter, name):
    nw = len(arrays)
    start, relay, wait, landed, sems, arrays = _exchange_ops(arrays, scatter)

    def body(*refs):
        ins, outs, csem = refs[:nw], refs[nw:2 * nw], refs[2 * nw:]
        start(ins, outs, *csem)
        if relay is not None:
            relay(ins, outs, *csem)
        wait(ins, outs, *csem)

    any_spec = pl.BlockSpec(memory_space=pl.ANY)
    return _pc(body, name=name, in_specs=[any_spec] * nw, out_specs=[any_spec] * nw, out_shape=landed,
               scratch_shapes=sems)(*arrays)


def _adam_math(g, w, m, v):
    m = ADAM_B1 * m + (1.0 - ADAM_B1) * g
    v = ADAM_B2 * v + (1.0 - ADAM_B2) * jnp.square(g)
    m_hat = m / (1.0 - ADAM_B1 ** ADAM_STEP)
    v_hat = v / (1.0 - ADAM_B2 ** ADAM_STEP)
    delta = -ADAM_LR * (m_hat / (jnp.sqrt(v_hat) + ADAM_EPS) + ADAM_WD * w)
    return delta, m, v


def _adam_rows(r):
    for t in range(min(r, 512) // 16 * 16, 0, -16):
        if r % t == 0:
            return t
    return r


def _adamw(parts, w, m, v, name):
    nl, r, c = w.shape
    tr = _adam_rows(r)
    nr = r // tr

    def body(*refs):
        p_refs, (w_ref, m_ref, v_ref, g_ref, d_ref, mo_ref, vo_ref) = refs[:nl], refs[nl:]
        for layer in range(nl):
            @pl.when(pl.program_id(0) == layer)
            def _(p_ref=p_refs[layer]):
                g = p_ref[0].astype(F32)
                for src in range(1, N_DEV):
                    g = g + p_ref[src].astype(F32)
                delta, m2, v2 = _adam_math(g, w_ref[0], m_ref[0], v_ref[0])
                g_ref[0] = g
                d_ref[0] = delta
                mo_ref[0] = m2
                vo_ref[0] = v2

    def part_spec(layer):
        def index(l, i):
            return 0, jnp.where(l == layer, i, jnp.where(l < layer, 0, nr - 1)), 0
        return pl.BlockSpec((N_DEV, tr, c), index)

    blk = pl.BlockSpec((1, tr, c), lambda l, i: (l, i, 0))
    return _pc(
        body, name=name, grid=(nl, nr),
        in_specs=[part_spec(layer) for layer in range(nl)] + [blk, blk, blk],
        out_specs=[blk, blk, blk, blk],
        out_shape=[_sds((nl, r, c), F32)] * 4,
    )(*parts, w, m, v)


def _small_allreduce_adamw(gpack, wpack, mpack, vpack, last_grads):
    shape = gpack.shape
    compact = (SMALL_ROWS, D_MODEL)
    nw = len(last_grads)
    start, _, wait, landed, exchange_sems, last_grads = _exchange_ops(last_grads, True)

    def body(*refs):
        (g_ref, w_ref, m_ref, v_ref), rest = refs[:4], refs[4:]
        cin, rest = rest[:nw], rest[nw:]
        (go_ref, d_ref, mo_ref, vo_ref), rest = rest[:4], rest[4:]
        cout, rest = rest[:nw], rest[nw:]
        (comp_ref, gath_ref, send_sems, recv_sems), csem = rest[:4], rest[4:]
        start(cin, cout, *csem)
        _, me = _peer(0)
        comp_ref[...] = jnp.zeros(compact, F32)
        r = 0
        for p, n in enumerate(SMALL_PIECES):
            comp_ref[r:r + n, :] = g_ref[8 * p:8 * p + n, :]
            r += n
        gath_ref[me] = comp_ref[...]
        copies = []
        for k in range(1, N_DEV):
            dev, idx = _peer(k)
            copies.append(pltpu.make_async_remote_copy(
                src_ref=comp_ref, dst_ref=gath_ref.at[me], send_sem=send_sems.at[k], recv_sem=recv_sems.at[k],
                device_id=dev, device_id_type=pl.DeviceIdType.MESH))
        for cp in copies:
            cp.start()
        for cp in copies:
            cp.wait_send()
        for k in range(1, N_DEV):
            dev, idx = _peer(k)
            pltpu.make_async_remote_copy(
                src_ref=comp_ref, dst_ref=gath_ref.at[idx], send_sem=send_sems.at[k], recv_sem=recv_sems.at[k],
                device_id=dev, device_id_type=pl.DeviceIdType.MESH).wait_recv()
        g = gath_ref[0]
        for src in range(1, N_DEV):
            g = g + gath_ref[src]
        go_ref[...] = jnp.zeros(shape, F32)
        r = 0
        for p, n in enumerate(SMALL_PIECES):
            go_ref[8 * p:8 * p + n, :] = g[r:r + n, :]
            r += n
        delta, m2, v2 = _adam_math(go_ref[...], w_ref[...], m_ref[...], v_ref[...])
        d_ref[...] = delta
        mo_ref[...] = m2
        vo_ref[...] = v2
        wait(cin, cout, *csem)

    vm = pl.BlockSpec(memory_space=pltpu.VMEM)
    any_spec = pl.BlockSpec(memory_space=pl.ANY)
    res = _pc(
        body, name="small_allreduce_adamw",
        in_specs=[vm] * 4 + [any_spec] * nw, out_specs=[vm] * 4 + [any_spec] * nw,
        out_shape=[_sds(shape, F32)] * 4 + landed,
        scratch_shapes=[pltpu.VMEM(compact, F32), pltpu.VMEM((N_DEV,) + compact, F32),
                        pltpu.SemaphoreType.DMA((N_DEV,)), pltpu.SemaphoreType.DMA((N_DEV,))] + exchange_sems,
    )(gpack, wpack, mpack, vpack, *last_grads)
    return res[:4], res[4:]


def _rows_from_shards(g):
    return g.reshape(N_DEV * g.shape[1], g.shape[2])


def _shards_from_rows(a):
    return a.reshape(N_DEV, a.shape[0] // N_DEV, a.shape[1])


def _pad_to(a, rows, cols):
    return jnp.pad(a, ((0, rows - a.shape[0]), (0, cols - a.shape[1])))


def _rope_tables(pos):
    inv_freq = 1.0 / (ROPE_THETA ** (jnp.arange(0, ROPE, 2, dtype=F32) / ROPE))
    ang = pos.astype(F32)[:, None] * inv_freq
    cos, sin = jnp.cos(ang), jnp.sin(ang)
    z32, z64, z96 = (jnp.zeros((pos.shape[0], n), F32) for n in (32, 64, 96))
    return (jnp.concatenate([cos, cos, z64], axis=1), jnp.concatenate([-sin, z96], axis=1),
            jnp.concatenate([z32, sin, z64], axis=1))


def _pad_row(vec):
    vec = vec.reshape(1, -1)
    return jnp.pad(vec, ((0, 0), (0, D_MODEL - vec.shape[1])))


def _forward_backward(x, target, tables, gains, shards):
    cc, sn, sp = tables
    ffn_g = gains["ffn_norm"]
    by_cols = ("wg0", "wu0", "wg1", "wu1", "c_in")

    def gather(names, relay_at):
        return [_Cols(shards[n]) if n in by_cols else shards[n] for n in names], False, relay_at

    (got,) = _exchange([shards["w_in"]], False, "gather_w_in")
    w_in = _rows_from_shards(got)
    (n0, proj, cqn, ckvn, krr), (w_uq, w_ukv) = _mla_in_fwd(
        x, gains["mla_norm"], w_in, gains["g_cq"], gains["g_ckv"], cc, sn, sp, carry=gather(["w_uq", "w_ukv"], 0.5))
    (q, k, v), (w_o, conv_norm, conv_w) = _qkv_proj(
        cqn, ckvn, krr, w_uq, w_ukv, cc, sn, sp, carry=gather(["w_o", "conv_norm", "conv_w"], 0.5))
    w_o = _rows_from_shards(w_o)
    conv_norm = conv_norm.reshape(1, D_MODEL)
    conv_w = jnp.transpose(conv_w, (1, 0, 2)).reshape(3, D_MODEL)
    (o, lse), (wg0, wu0, wd0, c_in, c_out, wd1) = _attn_fwd(
        q, k, v, carry=gather(["wg0", "wu0", "wd0", "c_in", "c_out", "wd1"], 0.75))
    wd0 = _rows_from_shards(wd0)
    wd1 = _rows_from_shards(wd1)
    c_out = _rows_from_shards(c_out)
    h1 = _matmul_res(o, w_o, x, "mla_out_fwd")
    (h2, n1, gate0, up0), (wg1, wu1) = _ffn_fwd(
        h1, ffn_g[0:1], wg0, wu0, wd0, "ffn0_fwd", carry=gather(["wg1", "wu1"], 0.7))
    n2, bcx = _rms_matmul(h2, conv_norm, c_in, "conv_in_fwd")
    bu = _conv_fwd(bcx, conv_w)
    h3 = _matmul_res(bu, c_out, h2, "conv_out_fwd")
    (h4, n3, gate1, up1), _ = _ffn_fwd(h3, ffn_g[1:2], wg1, wu1, wd1, "ffn1_fwd")
    dh4, d_final, loss = _final_loss(h4, gains["final_norm"], target)

    small = {"final_norm": d_final}
    parts = {}

    def scatter(**blocks):
        return list(blocks), (list(blocks.values()), True, None)

    (dh3, dh3_b, dh4_b, dgate, dup, act, d_ffn1), _ = _ffn_bwd_x(
        dh4, h3, ffn_g[1:2], gate1, up1, wg1, wu1, wd1, "ffn1_bwd")
    dwg1 = _matmul_tn(dgate, n3, "ffn1_dwg")
    dwu1 = _matmul_tn(dup, n3, "ffn1_dwu")
    dwd1 = _matmul_tn(act, dh4_b, "ffn1_dwd")
    dbuc = _matmul_nt(dh3_b, c_out, "conv_out_bwd")
    d_c_out = _matmul_tn(bu, dh3_b, "conv_dwout")
    dbcx, small["conv_w"] = _conv_bwd(dbuc, bcx, conv_w)
    d_c_in = _matmul_tn(n2, dbcx, "conv_dwin")
    names, carry = scatter(c_out=_shards_from_rows(d_c_out))
    (dh2, dh2_b, small["conv_norm"]), got = _nt_rmsbwd(
        dbcx, c_in, h2, conv_norm, dh3, "conv_in_bwd", carry=carry)
    parts.update(zip(names, got))
    ffn_blocks = lambda d: _Rows(_shards_from_rows(d), FF_SHARD)
    names, carry = scatter(c_in=_Cols(d_c_in), wd1=ffn_blocks(dwd1))
    (dh1, dh1_b, dh2_b2, dgate, dup, act, d_ffn0), got = _ffn_bwd_x(
        dh2, h1, ffn_g[0:1], gate0, up0, wg0, wu0, wd0, "ffn0_bwd", carry=carry)
    parts.update(zip(names, got))
    dwg0 = _matmul_tn(dgate, n1, "ffn0_dwg")
    dwu0 = _matmul_tn(dup, n1, "ffn0_dwu")
    dwd0 = _matmul_tn(act, dh2_b2, "ffn0_dwd")
    small["ffn_norm"] = jnp.pad(d_ffn0, ((0, 7), (0, 0))) + jnp.pad(d_ffn1, ((1, 6), (0, 0)))
    do = _matmul_nt(dh1_b, w_o, "mla_out_bwd")
    d_w_o = _matmul_tn(o, dh1_b, "mla_dwo")
    names, carry = scatter(**{n: ffn_blocks(d) for n, d in dict(
        wg0=dwg0, wg1=dwg1, wu0=dwu0, wu1=dwu1, wd0=dwd0).items()})
    (dq, dk, dv), got = _attn_bwd(q, k, v, do, o, lse, cc, sn, sp, carry=carry)
    parts.update(zip(names, got))
    d_w_uq = _matmul_tn(dq, cqn, "mla_dwuq")
    names, carry = scatter(w_o=_shards_from_rows(d_w_o), w_uq=_Rows(_shards_from_rows(d_w_uq), NOPE + ROPE))
    (dproj, dkv, small["g_cq"], small["g_ckv"]), got = _mla_mid_bwd(
        dq, dk, dv, proj, w_uq, w_ukv, gains["g_cq"], gains["g_ckv"], cc, sn, sp, carry=carry)
    parts.update(zip(names, got))
    d_w_ukv = _matmul_tn(ckvn, dkv, "mla_dwukv")
    names, carry = scatter(w_ukv=_Cols(d_w_ukv))
    d_w_in, got = _matmul_tn(dproj, n0, "mla_dwin", carry=carry)
    parts.update(zip(names, got))
    (dx, _, small["mla_norm"]), _ = _nt_rmsbwd(dproj, w_in, x, gains["mla_norm"], dh1, "mla_in_bwd")
    return loss, dx, parts, _Cols(d_w_in), small


def kernel(x, positions, mla_norm, mla_w_in, mla_g_cq, mla_g_ckv, mla_w_uq, mla_w_ukv, mla_w_o, conv_norm, conv_w_in, conv_w, conv_w_out, ffn_norm, ffn_w_gate, ffn_w_up, ffn_w_down, final_norm, loss_target, m_mla_norm, m_mla_w_in, m_mla_g_cq, m_mla_g_ckv, m_mla_w_uq, m_mla_w_ukv, m_mla_w_o, m_conv_norm, m_conv_w_in, m_conv_w, m_conv_w_out, m_ffn_norm, m_ffn_w_gate, m_ffn_w_up, m_ffn_w_down, m_final_norm, v_mla_norm, v_mla_w_in, v_mla_g_cq, v_mla_g_ckv, v_mla_w_uq, v_mla_w_ukv, v_mla_w_o, v_conv_norm, v_conv_w_in, v_conv_w, v_conv_w_out, v_ffn_norm, v_ffn_w_gate, v_ffn_w_up, v_ffn_w_down, v_final_norm):
    me = 4 * lax.axis_index("x") + 2 * lax.axis_index("y") + lax.axis_index("c")

    bf = lambda a, rows, cols: _pad_to(a.astype(BF16), rows, cols)
    shards = dict(
        w_in=bf(mla_w_in[0], D_MODEL // N_DEV, PROJ_PAD), w_uq=bf(mla_w_uq[0], CQ, HEAD_PAD),
        w_ukv=mla_w_ukv[0].astype(BF16), w_o=mla_w_o[0].astype(BF16),
        c_in=conv_w_in[0].astype(BF16), c_out=conv_w_out[0].astype(BF16),
        conv_norm=conv_norm, conv_w=conv_w[0])
    for l in range(2):
        shards.update({f"wg{l}": bf(ffn_w_gate[l], D_MODEL, FF_SHARD_PAD), f"wu{l}": bf(ffn_w_up[l], D_MODEL, FF_SHARD_PAD),
                       f"wd{l}": bf(ffn_w_down[l], FF_SHARD_PAD, D_MODEL)})
    gains = dict(mla_norm=mla_norm, g_cq=mla_g_cq, g_ckv=mla_g_ckv, ffn_norm=ffn_norm,
                 final_norm=final_norm.reshape(1, -1))
    loss_local, dx, parts, d_w_in, grads = _forward_backward(
        x[0], loss_target[0], _rope_tables(positions[0]), gains, shards)

    col0 = me * (D_MODEL // N_DEV)

    def place(shard):
        return lax.dynamic_update_slice(jnp.zeros((shard.shape[0], D_MODEL), F32), shard, (0, col0))

    no_loss = jnp.zeros((8, D_MODEL), F32)

    def pack(mla_n, g_cq, g_ckv, ffn_n, fin_n, conv_n, conv_taps, loss_tile):
        rows = [mla_n, _pad_row(g_cq), _pad_row(g_ckv), ffn_n, fin_n.reshape(1, -1), conv_n, conv_taps, loss_tile]
        assert all(r.shape[0] in (n, 8) for r, n in zip(rows, SMALL_PIECES))
        return jnp.concatenate([jnp.pad(r, ((0, 8 - r.shape[0]), (0, 0))) for r in rows], axis=0)

    gpack = pack(grads["mla_norm"], grads["g_cq"], grads["g_ckv"], grads["ffn_norm"], grads["final_norm"],
                 grads["conv_norm"], grads["conv_w"], loss_local)
    wpack = pack(mla_norm, mla_g_cq, mla_g_ckv, ffn_norm, final_norm, place(conv_norm), place(conv_w[0]), no_loss)
    mpack = pack(m_mla_norm, m_mla_g_cq, m_mla_g_ckv, m_ffn_norm, m_final_norm, place(m_conv_norm),
                 place(m_conv_w[0]), no_loss)
    vpack = pack(v_mla_norm, v_mla_g_cq, v_mla_g_ckv, v_ffn_norm, v_final_norm, place(v_conv_norm),
                 place(v_conv_w[0]), no_loss)
    small, (parts["w_in"],) = _small_allreduce_adamw(gpack, wpack, mpack, vpack, [d_w_in])
    loss = small[0][56, 0]

    def adamw(name, w, m, v, partials, transposed=False):
        if transposed:
            w, m, v = (jnp.swapaxes(a, 1, 2) for a in (w, m, v))
        outs = _adamw(partials, w, m, v, "adamw_" + name)
        return [jnp.swapaxes(o, 1, 2) for o in outs] if transposed else outs

    res = dict(
        w_in=adamw("w_in", mla_w_in, m_mla_w_in, v_mla_w_in, [parts["w_in"]], True),
        w_uq=adamw("w_uq", mla_w_uq, m_mla_w_uq, v_mla_w_uq, [parts["w_uq"]], True),
        w_ukv=adamw("w_ukv", mla_w_ukv, m_mla_w_ukv, v_mla_w_ukv, [parts["w_ukv"]]),
        w_o=adamw("w_o", mla_w_o, m_mla_w_o, v_mla_w_o, [parts["w_o"]]),
        c_in=adamw("c_in", conv_w_in, m_conv_w_in, v_conv_w_in, [parts["c_in"]]),
        c_out=adamw("c_out", conv_w_out, m_conv_w_out, v_conv_w_out, [parts["c_out"]]),
        wg=adamw("wg", ffn_w_gate, m_ffn_w_gate, v_ffn_w_gate, [parts["wg0"], parts["wg1"]], True),
        wu=adamw("wu", ffn_w_up, m_ffn_w_up, v_ffn_w_up, [parts["wu0"], parts["wu1"]], True),
        wd=adamw("wd", ffn_w_down, m_ffn_w_down, v_ffn_w_down, [parts["wd0"], parts["wd1"]]),
    )

    def unpack(p):
        own = lambda rows: lax.dynamic_slice(rows, (0, col0), (rows.shape[0], D_MODEL // N_DEV))
        return dict(mla_norm=p[0:1], g_cq=p[8:9, :CQ], g_ckv=p[16:17, :CKV], ffn_norm=p[24:26], final_norm=p[32],
                    conv_norm=own(p[40:41]), conv_w=own(p[48:51])[None])

    small = [unpack(p) for p in small]
    order = ["mla_norm", "w_in", "g_cq", "g_ckv", "w_uq", "w_ukv", "w_o", "conv_norm", "c_in", "conv_w", "c_out",
             "ffn_norm", "wg", "wu", "wd", "final_norm"]
    out = [loss, dx[None]]
    for kind in range(4):
        for n in order:
            out.append(res[n][kind] if n in res else small[kind][n])
    return tuple(out)
```

```python
import math

import jax
import jax.numpy as jnp
from jax import lax
from jax.experimental import pallas as pl
from jax.experimental.pallas import tpu as pltpu

F32 = jnp.float32
BF16 = jnp.bfloat16

N_DEV = 8
D_MODEL = 1024
N_HEADS = 8
NOPE = 128
ROPE = 64
V_DIM = 128
HEAD_PAD = 256
CQ = 512
CKV = 256
PROJ_PAD = CQ + CKV + 128
FF_SHARD = 352
FF_SHARD_PAD = 384
FF_PAD = FF_SHARD_PAD * N_DEV
CHUNK_SHIFT = 6
RMS_EPS = 1e-6
ROPE_THETA = 10000.0
ATT_SCALE = 1.0 / math.sqrt(NOPE + ROPE)
LOG2_E = math.log2(math.e)
LN_2 = math.log(2.0)
Q_SCALE = ATT_SCALE * LOG2_E
NEG = -1e30

ADAM_LR = 0.001
ADAM_B1 = 0.9
ADAM_B2 = 0.999
ADAM_EPS = 1e-08
ADAM_WD = 0.01
ADAM_STEP = 10

SMALL_PIECES = (1, 1, 1, 2, 1, 1, 3, 1)
SMALL_ROWS = 16

_NT = (((1,), (1,)), ((), ()))
_TN = (((0,), (0,)), ((), ()))


def _pc(body, *, name, out_shape, grid=(), in_specs=None, out_specs=None, scratch_shapes=()):
    kwargs = dict(
        name=name, out_shape=out_shape, grid=grid, scratch_shapes=scratch_shapes,
        compiler_params=pltpu.CompilerParams(),
    )
    if in_specs is not None:
        kwargs["in_specs"] = in_specs
    if out_specs is not None:
        kwargs["out_specs"] = out_specs
    return pl.pallas_call(body, **kwargs)


def _pc_carrying(body, carry, operands, *, name, out_shape, grid, in_specs, out_specs, scratch_shapes=()):
    if carry is None:
        return _pc(body, name=name, out_shape=out_shape, grid=grid, in_specs=in_specs, out_specs=out_specs,
                   scratch_shapes=scratch_shapes)(*operands), None
    arrays, scatter, relay_at = carry
    nw, n_in, n_out, n_scr = len(arrays), len(in_specs), len(out_shape), len(scratch_shapes)
    start, relay, wait, landed_shapes, sems, arrays = _exchange_ops(arrays, scatter)
    n_steps = math.prod(grid)
    relay_step = None if relay is None else min(int(relay_at * n_steps), n_steps - 1)

    def wrapped(*refs):
        ins, rest = refs[:n_in], refs[n_in:]
        cin, rest = rest[:nw], rest[nw:]
        outs, rest = rest[:n_out], rest[n_out:]
        cout, rest = rest[:nw], rest[nw:]
        scr, csem = rest[:n_scr], rest[n_scr:]
        step = pl.program_id(0)
        for a in range(1, len(grid)):
            step = step * grid[a] + pl.program_id(a)

        @pl.when(step == 0)
        def _():
            start(cin, cout, *csem)

        if relay is not None:
            @pl.when(step == relay_step)
            def _():
                relay(cin, cout, *csem)

        body(*ins, *outs, *scr)

        @pl.when(step == n_steps - 1)
        def _():
            wait(cin, cout, *csem)

    any_spec = pl.BlockSpec(memory_space=pl.ANY)
    res = _pc(
        wrapped, name=name, grid=grid,
        in_specs=list(in_specs) + [any_spec] * nw, out_specs=list(out_specs) + [any_spec] * nw,
        out_shape=list(out_shape) + landed_shapes, scratch_shapes=list(scratch_shapes) + sems,
    )(*operands, *arrays)
    return res[:n_out], res[n_out:]


def _sds(shape, dtype):
    return jax.ShapeDtypeStruct(shape, dtype)


def _dot(a, b):
    return jnp.dot(a, b, preferred_element_type=F32)


def _dot_nt(a, b):
    return lax.dot_general(a, b, _NT, preferred_element_type=F32)


def _dot_tn(a, b):
    return lax.dot_general(a, b, _TN, preferred_element_type=F32)


def _rstd(x):
    return lax.rsqrt(jnp.mean(x * x, axis=-1, keepdims=True) + RMS_EPS)


def _rms(x, g):
    return (x * _rstd(x)) * g


def _rms_bwd(dy, x, g):
    r = _rstd(x)
    xhat = x * r
    dxhat = dy * g
    dx = r * (dxhat - xhat * jnp.mean(dxhat * xhat, axis=-1, keepdims=True))
    return dx, jnp.sum(dy * xhat, axis=0, keepdims=True)


def _rope(t, cc, sn, sp):
    return t * cc + pltpu.roll(t, 96, 1) * sn + pltpu.roll(t, 32, 1) * sp


def _rope_bwd(dt, cc, sn, sp):
    return dt * cc + pltpu.roll(dt * sn, 32, 1) + pltpu.roll(dt * sp, 96, 1)


def _row_block(s):
    return min(512, s)


_TN_ROWS = 2048
_FFN_ROWS = 1024


def _mla_in_fwd(x, g0, w_in, g_cq, g_ckv, cc, sn, sp, carry=None):
    s = x.shape[0]
    tm = min(_FFN_ROWS, s)

    def body(x_ref, g0_ref, w_ref, gcq_ref, gckv_ref, cc_ref, sn_ref, sp_ref,
             n_ref, proj_ref, cqn_ref, ckvn_ref, krr_ref):
        nb = _rms(x_ref[...], g0_ref[...]).astype(BF16)
        n_ref[...] = nb
        proj = _dot(nb, w_ref[...])
        proj_ref[...] = proj
        cqn_ref[...] = _rms(proj[:, :CQ], gcq_ref[...]).astype(BF16)
        ckvn_ref[...] = _rms(proj[:, CQ:CQ + CKV], gckv_ref[...]).astype(BF16)
        krr_ref[...] = _rope(proj[:, CQ + CKV:], cc_ref[...], sn_ref[...], sp_ref[...]).astype(BF16)

    row = lambda n: pl.BlockSpec((tm, n), lambda i: (i, 0))
    full = lambda a: pl.BlockSpec(a.shape, lambda i: (0, 0))
    return _pc_carrying(
        body, carry, (x, g0, w_in, g_cq, g_ckv, cc, sn, sp), name="mla_in_fwd", grid=(s // tm,),
        in_specs=[row(D_MODEL), full(g0), full(w_in), full(g_cq), full(g_ckv), row(128), row(128), row(128)],
        out_specs=[row(D_MODEL), row(PROJ_PAD), row(CQ), row(CKV), row(128)],
        out_shape=[_sds((s, D_MODEL), BF16), _sds((s, PROJ_PAD), F32), _sds((s, CQ), BF16),
                   _sds((s, CKV), BF16), _sds((s, 128), BF16)],
    )


def _qkv_proj(cqn, ckvn, krr, w_uq, w_ukv, cc, sn, sp, carry=None):
    s = cqn.shape[0]
    tm = min(_FFN_ROWS, s)

    def body(cqn_ref, ckvn_ref, krr_ref, wuq_ref, wukv_ref, cc_ref, sn_ref, sp_ref, q_ref, k_ref, v_ref):
        cqn_b, ckvn_b, krr_b = cqn_ref[...], ckvn_ref[...], krr_ref[...]
        cc_b, sn_b, sp_b = cc_ref[...], sn_ref[...], sp_ref[...]
        for h in range(N_HEADS):
            q = _dot(cqn_b, wuq_ref[h]) * Q_SCALE
            q_ref[h, :, :NOPE] = q[:, :NOPE].astype(BF16)
            q_ref[h, :, NOPE:] = _rope(q[:, NOPE:], cc_b, sn_b, sp_b).astype(BF16)
            kv = _dot(ckvn_b, wukv_ref[h])
            k_ref[h, :, :NOPE] = kv[:, :NOPE].astype(BF16)
            k_ref[h, :, NOPE:] = krr_b
            v_ref[h] = kv[:, NOPE:].astype(BF16)

    row = lambda n: pl.BlockSpec((tm, n), lambda i: (i, 0))
    head_w = lambda a: pl.BlockSpec(a.shape, lambda i: (0, 0, 0))
    head_o = lambda n: pl.BlockSpec((N_HEADS, tm, n), lambda i: (0, i, 0))
    return _pc_carrying(
        body, carry, (cqn, ckvn, krr, w_uq, w_ukv, cc, sn, sp), name="qkv_proj", grid=(s // tm,),
        in_specs=[row(CQ), row(CKV), row(128), head_w(w_uq), head_w(w_ukv), row(128), row(128), row(128)],
        out_specs=[head_o(HEAD_PAD), head_o(HEAD_PAD), head_o(V_DIM)],
        out_shape=[_sds((N_HEADS, s, HEAD_PAD), BF16), _sds((N_HEADS, s, HEAD_PAD), BF16),
                   _sds((N_HEADS, s, V_DIM), BF16)],
    )


def _chunk_bias(bq, bk, first_key=0):
    rows = lax.broadcasted_iota(jnp.int32, (bq, bk), 0)
    cols = lax.broadcasted_iota(jnp.int32, (bq, bk), 1) + first_key
    visible = jnp.right_shift(cols, CHUNK_SHIFT) <= jnp.right_shift(rows, CHUNK_SHIFT)
    return jnp.where(visible, 0.0, NEG).astype(F32)


_ATTN_FWD_ROWS = 2048


def _attn_fwd(q, k, v, carry=None):
    s = q.shape[1]
    bk = _row_block(s)
    bq = min(_ATTN_FWD_ROWS, s)
    nd = bq // bk

    def body(q_ref, k_ref, v_ref, o_ref, lse_ref, bias_ref):
        i = pl.program_id(1)
        qb = q_ref[0]

        @pl.when((pl.program_id(0) == 0) & (i == 0))
        def _():
            bias_ref[...] = _chunk_bias(bq, bk)

        def block(j, carry, queries, bias=None):
            m, l, acc = carry
            start = pl.multiple_of(j * bk, bk)
            kb = k_ref[0, pl.ds(start, bk), :]
            vb = v_ref[0, pl.ds(start, bk), :]
            sc = _dot_nt(queries, kb)
            if bias is not None:
                sc = sc + bias
            m_new = jnp.maximum(m, jnp.max(sc, axis=-1, keepdims=True))
            p = jnp.exp2(sc - m_new)
            alpha = jnp.exp2(m - m_new)
            l = alpha * l + jnp.sum(p, axis=-1, keepdims=True)
            acc = alpha * acc + _dot(p.astype(BF16), vb)
            return m_new, l, acc

        carry = (jnp.full((bq, 1), NEG, F32), jnp.zeros((bq, 1), F32), jnp.zeros((bq, V_DIM), F32))
        carry = lax.fori_loop(0, i * nd, lambda j, c: block(j, c, qb), carry)
        for d in range(nd):
            first = d * bk
            carry = block(i * nd + d, carry, qb[first:], bias_ref[:bq - first, :])
            m, l, acc = (c[:bk] for c in carry)
            o_ref[first:first + bk, :] = (acc / l).astype(BF16)
            lse_ref[0, first:first + bk, :] = jnp.broadcast_to(m + jnp.log(l) * LOG2_E, (bk, 128))
            if d < nd - 1:
                carry = tuple(c[bk:] for c in carry)

    return _pc_carrying(
        body, carry, (q, k, v), name="attn_fwd", grid=(N_HEADS, s // bq),
        in_specs=[pl.BlockSpec((1, bq, HEAD_PAD), lambda h, i: (h, i, 0)),
                  pl.BlockSpec((1, s, HEAD_PAD), lambda h, i: (h, 0, 0)),
                  pl.BlockSpec((1, s, V_DIM), lambda h, i: (h, 0, 0))],
        out_specs=[pl.BlockSpec((bq, V_DIM), lambda h, i: (i, h)),
                   pl.BlockSpec((1, bq, 128), lambda h, i: (h, i, 0))],
        out_shape=[_sds((s, N_HEADS * V_DIM), BF16), _sds((N_HEADS, s, 128), F32)],
        scratch_shapes=[pltpu.VMEM((bq, bk), F32)],
    )


def _matmul_res(a, w, res, name):
    s, kd = a.shape
    n = w.shape[1]
    tm = min(_FFN_ROWS, s)

    def body(a_ref, w_ref, r_ref, o_ref):
        o_ref[...] = r_ref[...] + _dot(a_ref[...], w_ref[...])

    return _pc(
        body, name=name, grid=(s // tm,),
        in_specs=[pl.BlockSpec((tm, kd), lambda i: (i, 0)), pl.BlockSpec((kd, n), lambda i: (0, 0)),
                  pl.BlockSpec((tm, n), lambda i: (i, 0))],
        out_specs=pl.BlockSpec((tm, n), lambda i: (i, 0)),
        out_shape=_sds((s, n), F32),
    )(a, w, res)


def _ffn_fwd(h, gain, wg, wu, wd, name, carry=None):
    s = h.shape[0]
    tm = min(_FFN_ROWS, s)
    tf = 512
    nf = FF_PAD // tf

    def body(h_ref, g_ref, wg_ref, wu_ref, wd_ref, o_ref, n_ref, gate_ref, up_ref, acc_ref):
        j = pl.program_id(1)

        @pl.when(j == 0)
        def _():
            n_ref[...] = _rms(h_ref[...], g_ref[...]).astype(BF16)
            acc_ref[...] = jnp.zeros_like(acc_ref)

        nb = n_ref[...]
        gate = _dot(nb, wg_ref[...])
        up = _dot(nb, wu_ref[...])
        gate_ref[...] = gate.astype(BF16)
        up_ref[...] = up.astype(BF16)
        act = gate * jax.nn.sigmoid(gate) * up
        acc_ref[...] += _dot(act.astype(BF16), wd_ref[...])

        @pl.when(j == nf - 1)
        def _():
            o_ref[...] = h_ref[...] + acc_ref[...]

    return _pc_carrying(
        body, carry, (h, gain, wg, wu, wd), name=name, grid=(s // tm, nf),
        in_specs=[pl.BlockSpec((tm, D_MODEL), lambda i, j: (i, 0)), pl.BlockSpec((1, D_MODEL), lambda i, j: (0, 0)),
                  pl.BlockSpec((D_MODEL, tf), lambda i, j: (0, j)), pl.BlockSpec((D_MODEL, tf), lambda i, j: (0, j)),
                  pl.BlockSpec((tf, D_MODEL), lambda i, j: (j, 0))],
        out_specs=[pl.BlockSpec((tm, D_MODEL), lambda i, j: (i, 0)), pl.BlockSpec((tm, D_MODEL), lambda i, j: (i, 0)),
                   pl.BlockSpec((tm, tf), lambda i, j: (i, j)), pl.BlockSpec((tm, tf), lambda i, j: (i, j))],
        out_shape=[_sds((s, D_MODEL), F32), _sds((s, D_MODEL), BF16), _sds((s, FF_PAD), BF16),
                   _sds((s, FF_PAD), BF16)],
        scratch_shapes=[pltpu.VMEM((tm, D_MODEL), F32)],
    )


def _rms_matmul(h, gain, w, name):
    s = h.shape[0]
    n = w.shape[1]
    tm = min(_FFN_ROWS, s)
    tn = 1024
    nn = n // tn

    def body(h_ref, g_ref, w_ref, n_ref, o_ref):
        @pl.when(pl.program_id(1) == 0)
        def _():
            n_ref[...] = _rms(h_ref[...], g_ref[...]).astype(BF16)

        o_ref[...] = _dot(n_ref[...], w_ref[...]).astype(BF16)

    return _pc(
        body, name=name, grid=(s // tm, nn),
        in_specs=[pl.BlockSpec((tm, D_MODEL), lambda i, j: (i, 0)), pl.BlockSpec((1, D_MODEL), lambda i, j: (0, 0)),
                  pl.BlockSpec((D_MODEL, tn), lambda i, j: (0, j))],
        out_specs=[pl.BlockSpec((tm, D_MODEL), lambda i, j: (i, 0)), pl.BlockSpec((tm, tn), lambda i, j: (i, j))],
        out_shape=[_sds((s, D_MODEL), BF16), _sds((s, n), BF16)],
    )(h, gain, w)


def _shift_down(u, k, rows):
    return jnp.where(rows >= k, pltpu.roll(u, k, 0), 0.0)


def _shift_up(u, k, rows, s):
    return jnp.where(rows < s - k, pltpu.roll(u, s - k, 0), 0.0)


_CONV_COLS = 128


def _conv_fwd(bcx, cw):
    s = bcx.shape[0]
    tc = _CONV_COLS
    nc = D_MODEL // tc

    def body(b_ref, c_ref, xp_ref, w_ref, o_ref):
        rows = lax.broadcasted_iota(jnp.int32, (s, tc), 0)
        u = c_ref[...].astype(F32) * xp_ref[...].astype(F32)
        w = w_ref[...]
        uc = w[2:3] * u + w[1:2] * _shift_down(u, 1, rows) + w[0:1] * _shift_down(u, 2, rows)
        o_ref[...] = (b_ref[...].astype(F32) * uc).astype(BF16)

    col = lambda off: pl.BlockSpec((s, tc), lambda j: (0, off + j))
    return _pc(
        body, name="conv_fwd", grid=(nc,),
        in_specs=[col(0), col(nc), col(2 * nc), pl.BlockSpec((3, tc), lambda j: (0, j))],
        out_specs=pl.BlockSpec((s, tc), lambda j: (0, j)),
        out_shape=_sds((s, D_MODEL), BF16),
    )(bcx, bcx, bcx, cw)


def _final_loss(h, gain, target):
    s = h.shape[0]
    tm = min(_FFN_ROWS, s)

    def body(h_ref, g_ref, t_ref, dh_ref, dg_ref, loss_ref):
        i = pl.program_id(0)

        @pl.when(i == 0)
        def _():
            dg_ref[...] = jnp.zeros_like(dg_ref)
            loss_ref[...] = jnp.zeros_like(loss_ref)

        hb = h_ref[...]
        e = _rms(hb, g_ref[...]) - t_ref[...]
        loss_ref[...] += 0.5 * jnp.sum(jnp.mean(e * e, axis=-1, keepdims=True))
        dx, dg = _rms_bwd(e * (1.0 / D_MODEL), hb, g_ref[...])
        dh_ref[...] = dx
        dg_ref[...] += dg

    row = pl.BlockSpec((tm, D_MODEL), lambda i: (i, 0))
    vec = pl.BlockSpec((1, D_MODEL), lambda i: (0, 0))
    return _pc(
        body, name="final_loss", grid=(s // tm,),
        in_specs=[row, vec, row],
        out_specs=[row, vec, pl.BlockSpec((8, D_MODEL), lambda i: (0, 0))],
        out_shape=[_sds((s, D_MODEL), F32), _sds((1, D_MODEL), F32), _sds((8, D_MODEL), F32)],
    )(h, gain, target)


def _ffn_bwd_x(dho, h, gain, gate, up, wg, wu, wd, name, carry=None):
    s = h.shape[0]
    tm = _row_block(s)
    tf = 1024
    nf = FF_PAD // tf

    def body(dho_ref, h_ref, g_ref, gate_ref, up_ref, wg_ref, wu_ref, wd_ref,
             dh_ref, dhb_ref, dhob_ref, dgate_ref, dup_ref, act_ref, dgain_ref, acc_ref):
        i = pl.program_id(0)
        j = pl.program_id(1)

        @pl.when(j == 0)
        def _():
            dhob_ref[...] = dho_ref[...].astype(BF16)
            acc_ref[...] = jnp.zeros_like(acc_ref)

        @pl.when((i == 0) & (j == 0))
        def _():
            dgain_ref[...] = jnp.zeros_like(dgain_ref)

        dact = _dot_nt(dhob_ref[...], wd_ref[...])
        g = gate_ref[...].astype(F32)
        u = up_ref[...].astype(F32)
        sg = jax.nn.sigmoid(g)
        silu = g * sg
        dg = (dact * u * (sg * (1.0 + g * (1.0 - sg)))).astype(BF16)
        du = (dact * silu).astype(BF16)
        dgate_ref[...] = dg
        dup_ref[...] = du
        act_ref[...] = (silu * u).astype(BF16)
        acc_ref[...] += _dot_nt(dg, wg_ref[...]) + _dot_nt(du, wu_ref[...])

        @pl.when(j == nf - 1)
        def _():
            dx, dgain = _rms_bwd(acc_ref[...], h_ref[...], g_ref[...])
            dh = dho_ref[...] + dx
            dh_ref[...] = dh
            dhb_ref[...] = dh.astype(BF16)
            dgain_ref[...] += dgain

    row = pl.BlockSpec((tm, D_MODEL), lambda i, j: (i, 0))
    vec = pl.BlockSpec((1, D_MODEL), lambda i, j: (0, 0))
    hid = pl.BlockSpec((tm, tf), lambda i, j: (i, j))
    wcol = pl.BlockSpec((D_MODEL, tf), lambda i, j: (0, j))
    wrow = pl.BlockSpec((tf, D_MODEL), lambda i, j: (j, 0))
    return _pc_carrying(
        body, carry, (dho, h, gain, gate, up, wg, wu, wd), name=name, grid=(s // tm, nf),
        in_specs=[row, row, vec, hid, hid, wcol, wcol, wrow],
        out_specs=[row, row, row, hid, hid, hid, vec],
        out_shape=[_sds((s, D_MODEL), F32), _sds((s, D_MODEL), BF16), _sds((s, D_MODEL), BF16),
                   _sds((s, FF_PAD), BF16), _sds((s, FF_PAD), BF16), _sds((s, FF_PAD), BF16),
                   _sds((1, D_MODEL), F32)],
        scratch_shapes=[pltpu.VMEM((tm, D_MODEL), F32)],
    )


def _nt_rmsbwd(a, w, h, gain, dho, name, carry=None):
    stacked = a.ndim == 3
    if stacked:
        nk, s, tk = a.shape
    else:
        s, tk = a.shape
        nk = 1
    tm = min(_FFN_ROWS, s)

    def body(a_ref, w_ref, h_ref, g_ref, dho_ref, dh_ref, dhb_ref, dgain_ref, acc_ref):
        i = pl.program_id(0)
        j = pl.program_id(1)
        if stacked:
            a_ref = a_ref.at[0]

        @pl.when(j == 0)
        def _():
            acc_ref[...] = jnp.zeros_like(acc_ref)

        @pl.when((i == 0) & (j == 0))
        def _():
            dgain_ref[...] = jnp.zeros_like(dgain_ref)

        acc_ref[...] += _dot_nt(a_ref[...], w_ref[...])

        @pl.when(j == nk - 1)
        def _():
            dx, dgain = _rms_bwd(acc_ref[...], h_ref[...], g_ref[...])
            dh = dho_ref[...] + dx
            dh_ref[...] = dh
            dhb_ref[...] = dh.astype(BF16)
            dgain_ref[...] += dgain

    row = pl.BlockSpec((tm, D_MODEL), lambda i, j: (i, 0))
    vec = pl.BlockSpec((1, D_MODEL), lambda i, j: (0, 0))
    return _pc_carrying(
        body, carry, (a, w, h, gain, dho), name=name, grid=(s // tm, nk),
        in_specs=[pl.BlockSpec((1, tm, tk), lambda i, j: (j, i, 0)) if stacked else pl.BlockSpec((tm, tk), lambda i, j: (i, 0)),
                  pl.BlockSpec((D_MODEL, tk), lambda i, j: (0, j)), row, vec, row],
        out_specs=[row, row, vec],
        out_shape=[_sds((s, D_MODEL), F32), _sds((s, D_MODEL), BF16), _sds((1, D_MODEL), F32)],
        scratch_shapes=[pltpu.VMEM((tm, D_MODEL), F32)],
    )


def _matmul_nt(a, w, name):
    s, kd = a.shape
    n = w.shape[0]
    tm = min(_FFN_ROWS, s)

    def body(a_ref, w_ref, o_ref):
        o_ref[...] = _dot_nt(a_ref[...], w_ref[...]).astype(BF16)

    return _pc(
        body, name=name, grid=(s // tm,),
        in_specs=[pl.BlockSpec((tm, kd), lambda i: (i, 0)), pl.BlockSpec((n, kd), lambda i: (0, 0))],
        out_specs=pl.BlockSpec((tm, n), lambda i: (i, 0)),
        out_shape=_sds((s, n), BF16),
    )(a, w)


def _matmul_tn(a, b, name, carry=None):
    s, m = a.shape
    stacked = b.ndim == 3
    tmm = min(m, 1024)
    if stacked:
        n, tn = b.shape[0] * b.shape[2], b.shape[2]
    else:
        n = b.shape[1]
        tn = n if n <= 1024 else 1024
    tk = min(_TN_ROWS, s)
    nk = s // tk

    def body(a_ref, b_ref, o_ref, acc_ref):
        k = pl.program_id(2)
        if stacked:
            b_ref = b_ref.at[0]

        @pl.when(k == 0)
        def _():
            acc_ref[...] = jnp.zeros_like(acc_ref)

        acc_ref[...] += _dot_tn(a_ref[...], b_ref[...])

        @pl.when(k == nk - 1)
        def _():
            o_ref[...] = acc_ref[...].astype(BF16)

    (out,), got = _pc_carrying(
        body, carry, (a, b), name=name, grid=(m // tmm, n // tn, nk),
        in_specs=[pl.BlockSpec((tk, tmm), lambda i, j, k: (k, i)),
                  pl.BlockSpec((1, tk, tn), lambda i, j, k: (j, k, 0)) if stacked
                  else pl.BlockSpec((tk, tn), lambda i, j, k: (k, j))],
        out_specs=[pl.BlockSpec((tmm, tn), lambda i, j, k: (i, j))],
        out_shape=[_sds((m, n), BF16)],
        scratch_shapes=[pltpu.VMEM((tmm, tn), F32)],
    )
    return out if carry is None else (out, got)


def _conv_bwd(dbuc, bcx, cw):
    s = bcx.shape[0]
    tc = _CONV_COLS
    nc = D_MODEL // tc

    def body(d_ref, b_ref, c_ref, xp_ref, w_ref, dbcx_ref, dw_ref):
        rows = lax.broadcasted_iota(jnp.int32, (s, tc), 0)
        c = c_ref[...].astype(F32)
        xp = xp_ref[...].astype(F32)
        u = c * xp
        u1 = _shift_down(u, 1, rows)
        u2 = _shift_down(u, 2, rows)
        w = w_ref[...]
        uc = w[2:3] * u + w[1:2] * u1 + w[0:1] * u2
        d = d_ref[...].astype(F32)
        dbcx_ref[0] = (d * uc).astype(BF16)
        duc = d * b_ref[...].astype(F32)
        du = w[2:3] * duc + w[1:2] * _shift_up(duc, 1, rows, s) + w[0:1] * _shift_up(duc, 2, rows, s)
        dbcx_ref[1] = (du * xp).astype(BF16)
        dbcx_ref[2] = (du * c).astype(BF16)
        dw_ref[0:1, :] = jnp.sum(duc * u2, axis=0, keepdims=True)
        dw_ref[1:2, :] = jnp.sum(duc * u1, axis=0, keepdims=True)
        dw_ref[2:3, :] = jnp.sum(duc * u, axis=0, keepdims=True)

    col = lambda off: pl.BlockSpec((s, tc), lambda j: (0, off + j))
    tap = pl.BlockSpec((3, tc), lambda j: (0, j))
    return _pc(
        body, name="conv_bwd", grid=(nc,),
        in_specs=[col(0), col(0), col(nc), col(2 * nc), tap],
        out_specs=[pl.BlockSpec((3, s, tc), lambda j: (0, 0, j)), tap],
        out_shape=[_sds((3, s, D_MODEL), BF16), _sds((3, D_MODEL), F32)],
    )(dbuc, bcx, bcx, bcx, cw)


def _attn_bwd(q, k, v, do, o, lse, cc, sn, sp, carry=None):
    s = q.shape[1]
    bk = _row_block(s)
    nb = s // bk

    def body(q_ref, k_ref, v_ref, do_ref, o_ref, lse_ref, cc_ref, sn_ref, sp_ref,
             dq_ref, dk_ref, dv_ref, dqacc_ref, delta_ref, bias_ref):
        j = pl.program_id(1)
        kb = k_ref[0]
        vb = v_ref[0]

        @pl.when((pl.program_id(0) == 0) & (j == 0))
        def _():
            bias_ref[...] = _chunk_bias(2 * bk, bk)

        @pl.when(j == 0)
        def _():
            dqacc_ref[...] = jnp.zeros_like(dqacc_ref)

            def fill(i, _):
                rows = pl.ds(pl.multiple_of(i * bk, bk), bk)
                d = jnp.sum(do_ref[rows, :].astype(F32) * o_ref[rows, :].astype(F32), axis=-1, keepdims=True)
                delta_ref[rows, :] = jnp.broadcast_to(d, (bk, 128))
                return 0

            lax.fori_loop(0, nb, fill, 0)

        def step(i, carry, masked, blocks=1):
            dk, dv = carry
            rows = pl.ds(pl.multiple_of(i * bk, bk), blocks * bk)
            qb = q_ref[0, rows, :]
            dob = do_ref[rows, :]
            sc = _dot_nt(qb, kb)
            if masked:
                sc = sc + bias_ref[:blocks * bk, :]
            p = jnp.exp2(sc - lse_ref[0, rows, :][:, :1])
            dv = dv + _dot_tn(p.astype(BF16), dob)
            ds = (p * (_dot_nt(dob, vb) - delta_ref[rows, :][:, :1])).astype(BF16)
            dk = dk + _dot_tn(ds, qb)
            dqacc_ref[rows, :] += _dot(ds, kb)
            return dk, dv

        rest = nb - 1 - j
        odd = lax.rem(rest, 2)
        carry = (jnp.zeros((bk, HEAD_PAD), F32), jnp.zeros((bk, V_DIM), F32))
        carry = lax.fori_loop(0, 1 - odd, lambda t, c: step(j, c, True), carry)
        if nb > 1:
            carry = lax.fori_loop(0, odd, lambda t, c: step(j, c, True, blocks=2), carry)
        dq = dqacc_ref[pl.ds(pl.multiple_of(j * bk, bk), bk), :] * ATT_SCALE
        dq_ref[:, :NOPE] = dq[:, :NOPE].astype(BF16)
        dq_ref[:, NOPE:] = _rope_bwd(dq[:, NOPE:], cc_ref[...], sn_ref[...], sp_ref[...]).astype(BF16)
        if nb > 1:
            carry = lax.fori_loop(
                0, rest // 2, lambda t, c: step(j + 1 + odd + 2 * t, c, False, blocks=2), carry)
        dk, dv = carry
        dk_ref[0] = (dk * LN_2).astype(BF16)
        dv_ref[0] = dv.astype(BF16)

    blk = lambda n: pl.BlockSpec((1, bk, n), lambda h, j: (h, j, 0))
    whole = lambda n: pl.BlockSpec((1, s, n), lambda h, j: (h, 0, 0))
    cols = pl.BlockSpec((s, V_DIM), lambda h, j: (0, h))
    tab = pl.BlockSpec((bk, 128), lambda h, j: (j, 0))
    return _pc_carrying(
        body, carry, (q, k, v, do, o, lse, cc, sn, sp), name="attn_bwd", grid=(N_HEADS, nb),
        in_specs=[whole(HEAD_PAD), blk(HEAD_PAD), blk(V_DIM), cols, cols, whole(128), tab, tab, tab],
        out_specs=[pl.BlockSpec((bk, HEAD_PAD), lambda h, j: (j, h)), blk(HEAD_PAD), blk(V_DIM)],
        out_shape=[_sds((s, N_HEADS * HEAD_PAD), BF16), _sds((N_HEADS, s, HEAD_PAD), BF16),
                   _sds((N_HEADS, s, V_DIM), BF16)],
        scratch_shapes=[pltpu.VMEM((s, HEAD_PAD), F32), pltpu.VMEM((s, 128), F32), pltpu.VMEM((2 * bk, bk), F32)],
    )


def _mla_mid_bwd(dq, dk, dv, proj, w_uq, w_ukv, g_cq, g_ckv, cc, sn, sp, carry=None):
    s = dq.shape[0]
    tm = _row_block(s)

    def body(dq_ref, dk_ref, dv_ref, proj_ref, wuq_ref, wukv_ref, gcq_ref, gckv_ref, cc_ref, sn_ref, sp_ref,
             dproj_ref, dkv_ref, dgcq_ref, dgckv_ref, acq_ref, ackv_ref, akr_ref):
        @pl.when(pl.program_id(0) == 0)
        def _():
            dgcq_ref[...] = jnp.zeros_like(dgcq_ref)
            dgckv_ref[...] = jnp.zeros_like(dgckv_ref)

        for h in range(N_HEADS):
            cols = slice(h * HEAD_PAD, (h + 1) * HEAD_PAD)
            dkb = dk_ref[h]
            dkv_ref[:, h * HEAD_PAD:h * HEAD_PAD + NOPE] = dkb[:, :NOPE]
            dkv_ref[:, h * HEAD_PAD + NOPE:(h + 1) * HEAD_PAD] = dv_ref[h]
            cq_part = _dot_nt(dq_ref[:, cols], wuq_ref[h])
            ckv_part = _dot_nt(dkv_ref[:, cols], wukv_ref[h])
            kr_part = dkb[:, NOPE:].astype(F32)
            if h == 0:
                acq_ref[...], ackv_ref[...], akr_ref[...] = cq_part, ckv_part, kr_part
            else:
                acq_ref[...] += cq_part
                ackv_ref[...] += ckv_part
                akr_ref[...] += kr_part

        proj = proj_ref[...]
        dcq, dgcq = _rms_bwd(acq_ref[...], proj[:, :CQ], gcq_ref[...])
        dckv, dgckv = _rms_bwd(ackv_ref[...], proj[:, CQ:CQ + CKV], gckv_ref[...])
        dproj_ref[:, :CQ] = dcq.astype(BF16)
        dproj_ref[:, CQ:CQ + CKV] = dckv.astype(BF16)
        dproj_ref[:, CQ + CKV:] = _rope_bwd(akr_ref[...], cc_ref[...], sn_ref[...], sp_ref[...]).astype(BF16)
        dgcq_ref[...] += dgcq
        dgckv_ref[...] += dgckv

    head_blk = lambda n: pl.BlockSpec((N_HEADS, tm, n), lambda i: (0, i, 0))
    head_cols = pl.BlockSpec((tm, N_HEADS * HEAD_PAD), lambda i: (i, 0))
    head_w = lambda a: pl.BlockSpec(a.shape, lambda i: (0, 0, 0))
    row = lambda n: pl.BlockSpec((tm, n), lambda i: (i, 0))
    vec = lambda n: pl.BlockSpec((1, n), lambda i: (0, 0))
    return _pc_carrying(
        body, carry, (dq, dk, dv, proj, w_uq, w_ukv, g_cq, g_ckv, cc, sn, sp),
        name="mla_mid_bwd", grid=(s // tm,),
        in_specs=[head_cols, head_blk(HEAD_PAD), head_blk(V_DIM), row(PROJ_PAD), head_w(w_uq), head_w(w_ukv),
                  vec(CQ), vec(CKV), row(128), row(128), row(128)],
        out_specs=[row(PROJ_PAD), head_cols, vec(CQ), vec(CKV)],
        out_shape=[_sds((s, PROJ_PAD), BF16), _sds((s, N_HEADS * HEAD_PAD), BF16), _sds((1, CQ), F32),
                   _sds((1, CKV), F32)],
        scratch_shapes=[pltpu.VMEM((tm, CQ), F32), pltpu.VMEM((tm, CKV), F32), pltpu.VMEM((tm, 128), F32)],
    )


def _peer(k):
    x, y, c = lax.axis_index("x"), lax.axis_index("y"), lax.axis_index("c")
    px = 1 - x if k & 4 else x
    py = 1 - y if k & 2 else y
    pc = 1 - c if k & 1 else c
    return (px, py, pc), 4 * px + 2 * py + pc


def _exchange_ops(arrays, scatter):
    nw = len(arrays)
    by_cols = [isinstance(a, _Cols) for a in arrays]
    used_rows = [a.rows if isinstance(a, _Rows) else None for a in arrays]
    arrays = [a.array if isinstance(a, (_Cols, _Rows)) else a for a in arrays]
    direct = range(1, N_DEV) if scatter else (1, 2, 4, 6)
    assert scatter or not any(used_rows)

    def columns(ref, idx, width):
        return ref.at[:, pl.ds(pl.multiple_of(idx * width, 128), width)]

    def sent(ins, w, idx):
        if not scatter:
            return ins[w]
        if by_cols[w]:
            return columns(ins[w], idx, arrays[w].shape[1] // N_DEV)
        return ins[w].at[idx, pl.ds(0, used_rows[w])] if used_rows[w] else ins[w].at[idx]

    def slot(outs, w, idx):
        if by_cols[w] and not scatter:
            return columns(outs[w], idx, arrays[w].shape[1])
        return outs[w].at[idx]

    def copy(w, k, src, dst, to, send_sems, recv_sems):
        return pltpu.make_async_remote_copy(
            src_ref=src, dst_ref=dst, send_sem=send_sems.at[w * N_DEV + k], recv_sem=recv_sems.at[w * N_DEV + k],
            device_id=to, device_id_type=pl.DeviceIdType.MESH)

    def own_copies(ins, outs, local_sems):
        _, me = _peer(0)
        return [pltpu.make_async_copy(sent(ins, w, me), slot(outs, w, me), local_sems.at[w]) for w in range(nw)]

    def sends(ins, outs, send_sems, recv_sems):
        _, me = _peer(0)
        out = []
        for k in direct:
            dev, idx = _peer(k)
            for w in range(nw):
                out.append(copy(w, k, sent(ins, w, idx), slot(outs, w, me), dev, send_sems, recv_sems))
        return out

    def relays(outs, send_sems, recv_sems):
        sibling, _ = _peer(1)
        out = []
        for k in (2, 4, 6):
            _, idx = _peer(k)
            for w in range(nw):
                out.append(copy(w, k + 1, slot(outs, w, idx), slot(outs, w, idx), sibling, send_sems, recv_sems))
        return out

    def arrival(outs, w, k, send_sems, recv_sems):
        dev, idx = _peer(k)
        return copy(w, k, slot(outs, w, idx), slot(outs, w, idx), dev, send_sems, recv_sems)

    def start(ins, outs, send_sems, recv_sems, local_sems):
        for cp in own_copies(ins, outs, local_sems) + sends(ins, outs, send_sems, recv_sems):
            cp.start()

    def relay(ins, outs, send_sems, recv_sems, local_sems):
        for k in (2, 4, 6):
            for w in range(nw):
                arrival(outs, w, k, send_sems, recv_sems).wait_recv()
        for cp in relays(outs, send_sems, recv_sems):
            cp.start()

    def wait(ins, outs, send_sems, recv_sems, local_sems):
        for cp in own_copies(ins, outs, local_sems):
            cp.wait()
        for cp in sends(ins, outs, send_sems, recv_sems) + ([] if scatter else relays(outs, send_sems, recv_sems)):
            cp.wait_send()
        for k in (range(1, N_DEV) if scatter else (1, 3, 5, 7)):
            for w in range(nw):
                arrival(outs, w, k, send_sems, recv_sems).wait_recv()

    def landed_shape(a, cols, rows):
        if rows:
            return (N_DEV, rows) + a.shape[2:]
        if not cols:
            return a.shape if scatter else (N_DEV,) + a.shape
        r, c = a.shape
        return (N_DEV, r, c // N_DEV) if scatter else (r, N_DEV * c)

    landed = [_sds(landed_shape(a, cols, rows), a.dtype) for a, cols, rows in zip(arrays, by_cols, used_rows)]
    sems = [pltpu.SemaphoreType.DMA((nw * N_DEV,)), pltpu.SemaphoreType.DMA((nw * N_DEV,)),
            pltpu.SemaphoreType.DMA((nw,))]
    return start, (None if scatter else relay), wait, landed, sems, arrays


class _Cols:
    def __init__(self, array):
        self.array = array


class _Rows:
    def __init__(self, array, rows):
        self.array, self.rows = array, rows


def _exchange(arrays, scatter, name):
    nw = len(arrays)
    start, relay, wait, landed, sems, arrays = _exchange_ops(arrays, scatter)

    def body(*refs):
        ins, outs, csem = refs[:nw], refs[nw:2 * nw], refs[2 * nw:]
        start(ins, outs, *csem)
        if relay is not None:
            relay(ins, outs, *csem)
        wait(ins, outs, *csem)

    any_spec = pl.BlockSpec(memory_space=pl.ANY)
    return _pc(body, name=name, in_specs=[any_spec] * nw, out_specs=[any_spec] * nw, out_shape=landed,
               scratch_shapes=sems)(*arrays)


def _adam_math(g, w, m, v):
    m = ADAM_B1 * m + (1.0 - ADAM_B1) * g
    v = ADAM_B2 * v + (1.0 - ADAM_B2) * jnp.square(g)
    m_hat = m / (1.0 - ADAM_B1 ** ADAM_STEP)
    v_hat = v / (1.0 - ADAM_B2 ** ADAM_STEP)
    delta = -ADAM_LR * (m_hat / (jnp.sqrt(v_hat) + ADAM_EPS) + ADAM_WD * w)
    return delta, m, v


def _adam_rows(r):
    for t in range(min(r, 512) // 16 * 16, 0, -16):
        if r % t == 0:
            return t
    return r


def _adamw(parts, w, m, v, name):
    nl, r, c = w.shape
    tr = _adam_rows(r)
    nr = r // tr

    def body(*refs):
        p_refs, (w_ref, m_ref, v_ref, g_ref, d_ref, mo_ref, vo_ref) = refs[:nl], refs[nl:]
        for layer in range(nl):
            @pl.when(pl.program_id(0) == layer)
            def _(p_ref=p_refs[layer]):
                g = p_ref[0].astype(F32)
                for src in range(1, N_DEV):
                    g = g + p_ref[src].astype(F32)
                delta, m2, v2 = _adam_math(g, w_ref[0], m_ref[0], v_ref[0])
                g_ref[0] = g
                d_ref[0] = delta
                mo_ref[0] = m2
                vo_ref[0] = v2

    def part_spec(layer):
        def index(l, i):
            return 0, jnp.where(l == layer, i, jnp.where(l < layer, 0, nr - 1)), 0
        return pl.BlockSpec((N_DEV, tr, c), index)

    blk = pl.BlockSpec((1, tr, c), lambda l, i: (l, i, 0))
    return _pc(
        body, name=name, grid=(nl, nr),
        in_specs=[part_spec(layer) for layer in range(nl)] + [blk, blk, blk],
        out_specs=[blk, blk, blk, blk],
        out_shape=[_sds((nl, r, c), F32)] * 4,
    )(*parts, w, m, v)


def _small_allreduce_adamw(gpack, wpack, mpack, vpack, last_grads):
    shape = gpack.shape
    compact = (SMALL_ROWS, D_MODEL)
    nw = len(last_grads)
    start, _, wait, landed, exchange_sems, last_grads = _exchange_ops(last_grads, True)

    def body(*refs):
        (g_ref, w_ref, m_ref, v_ref), rest = refs[:4], refs[4:]
        cin, rest = rest[:nw], rest[nw:]
        (go_ref, d_ref, mo_ref, vo_ref), rest = rest[:4], rest[4:]
        cout, rest = rest[:nw], rest[nw:]
        (comp_ref, gath_ref, send_sems, recv_sems), csem = rest[:4], rest[4:]
        start(cin, cout, *csem)
        _, me = _peer(0)
        comp_ref[...] = jnp.zeros(compact, F32)
        r = 0
        for p, n in enumerate(SMALL_PIECES):
            comp_ref[r:r + n, :] = g_ref[8 * p:8 * p + n, :]
            r += n
        gath_ref[me] = comp_ref[...]
        copies = []
        for k in range(1, N_DEV):
            dev, idx = _peer(k)
            copies.append(pltpu.make_async_remote_copy(
                src_ref=comp_ref, dst_ref=gath_ref.at[me], send_sem=send_sems.at[k], recv_sem=recv_sems.at[k],
                device_id=dev, device_id_type=pl.DeviceIdType.MESH))
        for cp in copies:
            cp.start()
        for cp in copies:
            cp.wait_send()
        for k in range(1, N_DEV):
            dev, idx = _peer(k)
            pltpu.make_async_remote_copy(
                src_ref=comp_ref, dst_ref=gath_ref.at[idx], send_sem=send_sems.at[k], recv_sem=recv_sems.at[k],
                device_id=dev, device_id_type=pl.DeviceIdType.MESH).wait_recv()
        g = gath_ref[0]
        for src in range(1, N_DEV):
            g = g + gath_ref[src]
        go_ref[...] = jnp.zeros(shape, F32)
        r = 0
        for p, n in enumerate(SMALL_PIECES):
            go_ref[8 * p:8 * p + n, :] = g[r:r + n, :]
            r += n
        delta, m2, v2 = _adam_math(go_ref[...], w_ref[...], m_ref[...], v_ref[...])
        d_ref[...] = delta
        mo_ref[...] = m2
        vo_ref[...] = v2
        wait(cin, cout, *csem)

    vm = pl.BlockSpec(memory_space=pltpu.VMEM)
    any_spec = pl.BlockSpec(memory_space=pl.ANY)
    res = _pc(
        body, name="small_allreduce_adamw",
        in_specs=[vm] * 4 + [any_spec] * nw, out_specs=[vm] * 4 + [any_spec] * nw,
        out_shape=[_sds(shape, F32)] * 4 + landed,
        scratch_shapes=[pltpu.VMEM(compact, F32), pltpu.VMEM((N_DEV,) + compact, F32),
                        pltpu.SemaphoreType.DMA((N_DEV,)), pltpu.SemaphoreType.DMA((N_DEV,))] + exchange_sems,
    )(gpack, wpack, mpack, vpack, *last_grads)
    return res[:4], res[4:]


def _rows_from_shards(g):
    return g.reshape(N_DEV * g.shape[1], g.shape[2])


def _shards_from_rows(a):
    return a.reshape(N_DEV, a.shape[0] // N_DEV, a.shape[1])


def _pad_to(a, rows, cols):
    return jnp.pad(a, ((0, rows - a.shape[0]), (0, cols - a.shape[1])))


def _rope_tables(pos):
    inv_freq = 1.0 / (ROPE_THETA ** (jnp.arange(0, ROPE, 2, dtype=F32) / ROPE))
    ang = pos.astype(F32)[:, None] * inv_freq
    cos, sin = jnp.cos(ang), jnp.sin(ang)
    z32, z64, z96 = (jnp.zeros((pos.shape[0], n), F32) for n in (32, 64, 96))
    return (jnp.concatenate([cos, cos, z64], axis=1), jnp.concatenate([-sin, z96], axis=1),
            jnp.concatenate([z32, sin, z64], axis=1))


def _pad_row(vec):
    vec = vec.reshape(1, -1)
    return jnp.pad(vec, ((0, 0), (0, D_MODEL - vec.shape[1])))


def _forward_backward(x, target, tables, gains, shards):
    cc, sn, sp = tables
    ffn_g = gains["ffn_norm"]
    by_cols = ("wg0", "wu0", "wg1", "wu1", "c_in")

    def gather(names, relay_at):
        return [_Cols(shards[n]) if n in by_cols else shards[n] for n in names], False, relay_at

    (got,) = _exchange([shards["w_in"]], False, "gather_w_in")
    w_in = _rows_from_shards(got)
    (n0, proj, cqn, ckvn, krr), (w_uq, w_ukv) = _mla_in_fwd(
        x, gains["mla_norm"], w_in, gains["g_cq"], gains["g_ckv"], cc, sn, sp, carry=gather(["w_uq", "w_ukv"], 0.75))
    (q, k, v), (w_o, conv_norm, conv_w) = _qkv_proj(
        cqn, ckvn, krr, w_uq, w_ukv, cc, sn, sp, carry=gather(["w_o", "conv_norm", "conv_w"], 0.5))
    w_o = _rows_from_shards(w_o)
    conv_norm = conv_norm.reshape(1, D_MODEL)
    conv_w = jnp.transpose(conv_w, (1, 0, 2)).reshape(3, D_MODEL)
    (o, lse), (wg0, wu0, wd0, c_in, c_out, wd1) = _attn_fwd(
        q, k, v, carry=gather(["wg0", "wu0", "wd0", "c_in", "c_out", "wd1"], 0.85))
    wd0 = _rows_from_shards(wd0)
    wd1 = _rows_from_shards(wd1)
    c_out = _rows_from_shards(c_out)
    h1 = _matmul_res(o, w_o, x, "mla_out_fwd")
    (h2, n1, gate0, up0), (wg1, wu1) = _ffn_fwd(
        h1, ffn_g[0:1], wg0, wu0, wd0, "ffn0_fwd", carry=gather(["wg1", "wu1"], 0.7))
    n2, bcx = _rms_matmul(h2, conv_norm, c_in, "conv_in_fwd")
    bu = _conv_fwd(bcx, conv_w)
    h3 = _matmul_res(bu, c_out, h2, "conv_out_fwd")
    (h4, n3, gate1, up1), _ = _ffn_fwd(h3, ffn_g[1:2], wg1, wu1, wd1, "ffn1_fwd")
    dh4, d_final, loss = _final_loss(h4, gains["final_norm"], target)

    small = {"final_norm": d_final}
    parts = {}

    def scatter(**blocks):
        return list(blocks), (list(blocks.values()), True, None)

    (dh3, dh3_b, dh4_b, dgate, dup, act, d_ffn1), _ = _ffn_bwd_x(
        dh4, h3, ffn_g[1:2], gate1, up1, wg1, wu1, wd1, "ffn1_bwd")
    dwg1 = _matmul_tn(dgate, n3, "ffn1_dwg")
    dwu1 = _matmul_tn(dup, n3, "ffn1_dwu")
    dwd1 = _matmul_tn(act, dh4_b, "ffn1_dwd")
    dbuc = _matmul_nt(dh3_b, c_out, "conv_out_bwd")
    d_c_out = _matmul_tn(bu, dh3_b, "conv_dwout")
    dbcx, small["conv_w"] = _conv_bwd(dbuc, bcx, conv_w)
    d_c_in = _matmul_tn(n2, dbcx, "conv_dwin")
    names, carry = scatter(c_out=_shards_from_rows(d_c_out))
    (dh2, dh2_b, small["conv_norm"]), got = _nt_rmsbwd(
        dbcx, c_in, h2, conv_norm, dh3, "conv_in_bwd", carry=carry)
    parts.update(zip(names, got))
    ffn_blocks = lambda d: _Rows(_shards_from_rows(d), FF_SHARD)
    names, carry = scatter(c_in=_Cols(d_c_in), wd1=ffn_blocks(dwd1))
    (dh1, dh1_b, dh2_b2, dgate, dup, act, d_ffn0), got = _ffn_bwd_x(
        dh2, h1, ffn_g[0:1], gate0, up0, wg0, wu0, wd0, "ffn0_bwd", carry=carry)
    parts.update(zip(names, got))
    dwg0 = _matmul_tn(dgate, n1, "ffn0_dwg")
    dwu0 = _matmul_tn(dup, n1, "ffn0_dwu")
    dwd0 = _matmul_tn(act, dh2_b2, "ffn0_dwd")
    small["ffn_norm"] = jnp.pad(d_ffn0, ((0, 7), (0, 0))) + jnp.pad(d_ffn1, ((1, 6), (0, 0)))
    do = _matmul_nt(dh1_b, w_o, "mla_out_bwd")
    d_w_o = _matmul_tn(o, dh1_b, "mla_dwo")
    names, carry = scatter(**{n: ffn_blocks(d) for n, d in dict(
        wg0=dwg0, wg1=dwg1, wu0=dwu0, wu1=dwu1, wd0=dwd0).items()})
    (dq, dk, dv), got = _attn_bwd(q, k, v, do, o, lse, cc, sn, sp, carry=carry)
    parts.update(zip(names, got))
    d_w_uq = _matmul_tn(dq, cqn, "mla_dwuq")
    names, carry = scatter(w_o=_shards_from_rows(d_w_o), w_uq=_Rows(_shards_from_rows(d_w_uq), NOPE + ROPE))
    (dproj, dkv, small["g_cq"], small["g_ckv"]), got = _mla_mid_bwd(
        dq, dk, dv, proj, w_uq, w_ukv, gains["g_cq"], gains["g_ckv"], cc, sn, sp, carry=carry)
    parts.update(zip(names, got))
    d_w_ukv = _matmul_tn(ckvn, dkv, "mla_dwukv")
    names, carry = scatter(w_ukv=_Cols(d_w_ukv))
    d_w_in, got = _matmul_tn(dproj, n0, "mla_dwin", carry=carry)
    parts.update(zip(names, got))
    (dx, _, small["mla_norm"]), _ = _nt_rmsbwd(dproj, w_in, x, gains["mla_norm"], dh1, "mla_in_bwd")
    return loss, dx, parts, _Cols(d_w_in), small


def kernel(x, positions, mla_norm, mla_w_in, mla_g_cq, mla_g_ckv, mla_w_uq, mla_w_ukv, mla_w_o, conv_norm, conv_w_in, conv_w, conv_w_out, ffn_norm, ffn_w_gate, ffn_w_up, ffn_w_down, final_norm, loss_target, m_mla_norm, m_mla_w_in, m_mla_g_cq, m_mla_g_ckv, m_mla_w_uq, m_mla_w_ukv, m_mla_w_o, m_conv_norm, m_conv_w_in, m_conv_w, m_conv_w_out, m_ffn_norm, m_ffn_w_gate, m_ffn_w_up, m_ffn_w_down, m_final_norm, v_mla_norm, v_mla_w_in, v_mla_g_cq, v_mla_g_ckv, v_mla_w_uq, v_mla_w_ukv, v_mla_w_o, v_conv_norm, v_conv_w_in, v_conv_w, v_conv_w_out, v_ffn_norm, v_ffn_w_gate, v_ffn_w_up, v_ffn_w_down, v_final_norm):
    me = 4 * lax.axis_index("x") + 2 * lax.axis_index("y") + lax.axis_index("c")

    bf = lambda a, rows, cols: _pad_to(a.astype(BF16), rows, cols)
    shards = dict(
        w_in=bf(mla_w_in[0], D_MODEL // N_DEV, PROJ_PAD), w_uq=bf(mla_w_uq[0], CQ, HEAD_PAD),
        w_ukv=mla_w_ukv[0].astype(BF16), w_o=mla_w_o[0].astype(BF16),
        c_in=conv_w_in[0].astype(BF16), c_out=conv_w_out[0].astype(BF16),
        conv_norm=conv_norm, conv_w=conv_w[0])
    for l in range(2):
        shards.update({f"wg{l}": bf(ffn_w_gate[l], D_MODEL, FF_SHARD_PAD), f"wu{l}": bf(ffn_w_up[l], D_MODEL, FF_SHARD_PAD),
                       f"wd{l}": bf(ffn_w_down[l], FF_SHARD_PAD, D_MODEL)})
    gains = dict(mla_norm=mla_norm, g_cq=mla_g_cq, g_ckv=mla_g_ckv, ffn_norm=ffn_norm,
                 final_norm=final_norm.reshape(1, -1))
    loss_local, dx, parts, d_w_in, grads = _forward_backward(
        x[0], loss_target[0], _rope_tables(positions[0]), gains, shards)

    col0 = me * (D_MODEL // N_DEV)

    def place(shard):
        return lax.dynamic_update_slice(jnp.zeros((shard.shape[0], D_MODEL), F32), shard, (0, col0))

    no_loss = jnp.zeros((8, D_MODEL), F32)

    def pack(mla_n, g_cq, g_ckv, ffn_n, fin_n, conv_n, conv_taps, loss_tile):
        rows = [mla_n, _pad_row(g_cq), _pad_row(g_ckv), ffn_n, fin_n.reshape(1, -1), conv_n, conv_taps, loss_tile]
        assert all(r.shape[0] in (n, 8) for r, n in zip(rows, SMALL_PIECES))
        return jnp.concatenate([jnp.pad(r, ((0, 8 - r.shape[0]), (0, 0))) for r in rows], axis=0)

    gpack = pack(grads["mla_norm"], grads["g_cq"], grads["g_ckv"], grads["ffn_norm"], grads["final_norm"],
                 grads["conv_norm"], grads["conv_w"], loss_local)
    wpack = pack(mla_norm, mla_g_cq, mla_g_ckv, ffn_norm, final_norm, place(conv_norm), place(conv_w[0]), no_loss)
    mpack = pack(m_mla_norm, m_mla_g_cq, m_mla_g_ckv, m_ffn_norm, m_final_norm, place(m_conv_norm),
                 place(m_conv_w[0]), no_loss)
    vpack = pack(v_mla_norm, v_mla_g_cq, v_mla_g_ckv, v_ffn_norm, v_final_norm, place(v_conv_norm),
                 place(v_conv_w[0]), no_loss)
    small, (parts["w_in"],) = _small_allreduce_adamw(gpack, wpack, mpack, vpack, [d_w_in])
    loss = small[0][56, 0]

    def adamw(name, w, m, v, partials, transposed=False):
        if transposed:
            w, m, v = (jnp.swapaxes(a, 1, 2) for a in (w, m, v))
        outs = _adamw(partials, w, m, v, "adamw_" + name)
        return [jnp.swapaxes(o, 1, 2) for o in outs] if transposed else outs

    res = dict(
        w_in=adamw("w_in", mla_w_in, m_mla_w_in, v_mla_w_in, [parts["w_in"]], True),
        w_uq=adamw("w_uq", mla_w_uq, m_mla_w_uq, v_mla_w_uq, [parts["w_uq"]], True),
        w_ukv=adamw("w_ukv", mla_w_ukv, m_mla_w_ukv, v_mla_w_ukv, [parts["w_ukv"]]),
        w_o=adamw("w_o", mla_w_o, m_mla_w_o, v_mla_w_o, [parts["w_o"]]),
        c_in=adamw("c_in", conv_w_in, m_conv_w_in, v_conv_w_in, [parts["c_in"]]),
        c_out=adamw("c_out", conv_w_out, m_conv_w_out, v_conv_w_out, [parts["c_out"]]),
        wg=adamw("wg", ffn_w_gate, m_ffn_w_gate, v_ffn_w_gate, [parts["wg0"], parts["wg1"]], True),
        wu=adamw("wu", ffn_w_up, m_ffn_w_up, v_ffn_w_up, [parts["wu0"], parts["wu1"]], True),
        wd=adamw("wd", ffn_w_down, m_ffn_w_down, v_ffn_w_down, [parts["wd0"], parts["wd1"]]),
    )

    def unpack(p):
        own = lambda rows: lax.dynamic_slice(rows, (0, col0), (rows.shape[0], D_MODEL // N_DEV))
        return dict(mla_norm=p[0:1], g_cq=p[8:9, :CQ], g_ckv=p[16:17, :CKV], ffn_norm=p[24:26], final_norm=p[32],
                    conv_norm=own(p[40:41]), conv_w=own(p[48:51])[None])

    small = [unpack(p) for p in small]
    order = ["mla_norm", "w_in", "g_cq", "g_ckv", "w_uq", "w_ukv", "w_o", "conv_norm", "c_in", "conv_w", "c_out",
             "ffn_norm", "wg", "wu", "wd", "final_norm"]
    out = [loss, dx[None]]
    for kind in range(4):
        for n in order:
            out.append(res[n][kind] if n in res else small[kind][n])
    return tuple(out)
```

```python
import math

import jax
import jax.numpy as jnp
from jax import lax
from jax.experimental import pallas as pl
from jax.experimental.pallas import tpu as pltpu

F32 = jnp.float32
BF16 = jnp.bfloat16

N_DEV = 8
D_MODEL = 1024
N_HEADS = 8
NOPE = 128
ROPE = 64
V_DIM = 128
HEAD_PAD = 256
CQ = 512
CKV = 256
PROJ_PAD = CQ + CKV + 128
FF_SHARD = 352
FF_SHARD_PAD = 384
FF_PAD = FF_SHARD_PAD * N_DEV
CHUNK_SHIFT = 6
RMS_EPS = 1e-6
ROPE_THETA = 10000.0
ATT_SCALE = 1.0 / math.sqrt(NOPE + ROPE)
LOG2_E = math.log2(math.e)
LN_2 = math.log(2.0)
Q_SCALE = ATT_SCALE * LOG2_E
NEG = -1e30

ADAM_LR = 0.001
ADAM_B1 = 0.9
ADAM_B2 = 0.999
ADAM_EPS = 1e-08
ADAM_WD = 0.01
ADAM_STEP = 10

SMALL_PIECES = (1, 1, 1, 2, 1, 1, 3, 1)
SMALL_ROWS = 16

_NT = (((1,), (1,)), ((), ()))
_TN = (((0,), (0,)), ((), ()))


def _pc(body, *, name, out_shape, grid=(), in_specs=None, out_specs=None, scratch_shapes=()):
    kwargs = dict(
        name=name, out_shape=out_shape, grid=grid, scratch_shapes=scratch_shapes,
        compiler_params=pltpu.CompilerParams(),
    )
    if in_specs is not None:
        kwargs["in_specs"] = in_specs
    if out_specs is not None:
        kwargs["out_specs"] = out_specs
    return pl.pallas_call(body, **kwargs)


def _pc_carrying(body, carry, operands, *, name, out_shape, grid, in_specs, out_specs, scratch_shapes=()):
    if carry is None:
        return _pc(body, name=name, out_shape=out_shape, grid=grid, in_specs=in_specs, out_specs=out_specs,
                   scratch_shapes=scratch_shapes)(*operands), None
    arrays, scatter, relay_at = carry
    nw, n_in, n_out, n_scr = len(arrays), len(in_specs), len(out_shape), len(scratch_shapes)
    start, relay, wait, landed_shapes, sems, arrays = _exchange_ops(arrays, scatter)
    n_steps = math.prod(grid)
    relay_step = None if relay is None else min(int(relay_at * n_steps), n_steps - 1)

    def wrapped(*refs):
        ins, rest = refs[:n_in], refs[n_in:]
        cin, rest = rest[:nw], rest[nw:]
        outs, rest = rest[:n_out], rest[n_out:]
        cout, rest = rest[:nw], rest[nw:]
        scr, csem = rest[:n_scr], rest[n_scr:]
        step = pl.program_id(0)
        for a in range(1, len(grid)):
            step = step * grid[a] + pl.program_id(a)

        @pl.when(step == 0)
        def _():
            start(cin, cout, *csem)

        if relay is not None:
            @pl.when(step == relay_step)
            def _():
                relay(cin, cout, *csem)

        body(*ins, *outs, *scr)

        @pl.when(step == n_steps - 1)
        def _():
            wait(cin, cout, *csem)

    any_spec = pl.BlockSpec(memory_space=pl.ANY)
    res = _pc(
        wrapped, name=name, grid=grid,
        in_specs=list(in_specs) + [any_spec] * nw, out_specs=list(out_specs) + [any_spec] * nw,
        out_shape=list(out_shape) + landed_shapes, scratch_shapes=list(scratch_shapes) + sems,
    )(*operands, *arrays)
    return res[:n_out], res[n_out:]


def _sds(shape, dtype):
    return jax.ShapeDtypeStruct(shape, dtype)


def _dot(a, b):
    return jnp.dot(a, b, preferred_element_type=F32)


def _dot_nt(a, b):
    return lax.dot_general(a, b, _NT, preferred_element_type=F32)


def _dot_tn(a, b):
    return lax.dot_general(a, b, _TN, preferred_element_type=F32)


def _rstd(x):
    return lax.rsqrt(jnp.mean(x * x, axis=-1, keepdims=True) + RMS_EPS)


def _rms(x, g):
    return (x * _rstd(x)) * g


def _rms_bwd(dy, x, g):
    r = _rstd(x)
    xhat = x * r
    dxhat = dy * g
    dx = r * (dxhat - xhat * jnp.mean(dxhat * xhat, axis=-1, keepdims=True))
    return dx, jnp.sum(dy * xhat, axis=0, keepdims=True)


def _rope(t, cc, sn, sp):
    return t * cc + pltpu.roll(t, 96, 1) * sn + pltpu.roll(t, 32, 1) * sp


def _rope_bwd(dt, cc, sn, sp):
    return dt * cc + pltpu.roll(dt * sn, 32, 1) + pltpu.roll(dt * sp, 96, 1)


def _row_block(s):
    return min(512, s)


_TN_ROWS = 2048
_FFN_ROWS = 1024


def _mla_in_fwd(x, g0, w_in, g_cq, g_ckv, cc, sn, sp, carry=None):
    s = x.shape[0]
    tm = min(_FFN_ROWS, s)

    def body(x_ref, g0_ref, w_ref, gcq_ref, gckv_ref, cc_ref, sn_ref, sp_ref,
             n_ref, proj_ref, cqn_ref, ckvn_ref, krr_ref):
        nb = _rms(x_ref[...], g0_ref[...]).astype(BF16)
        n_ref[...] = nb
        proj = _dot(nb, w_ref[...])
        proj_ref[...] = proj
        cqn_ref[...] = _rms(proj[:, :CQ], gcq_ref[...]).astype(BF16)
        ckvn_ref[...] = _rms(proj[:, CQ:CQ + CKV], gckv_ref[...]).astype(BF16)
        krr_ref[...] = _rope(proj[:, CQ + CKV:], cc_ref[...], sn_ref[...], sp_ref[...]).astype(BF16)

    row = lambda n: pl.BlockSpec((tm, n), lambda i: (i, 0))
    full = lambda a: pl.BlockSpec(a.shape, lambda i: (0, 0))
    return _pc_carrying(
        body, carry, (x, g0, w_in, g_cq, g_ckv, cc, sn, sp), name="mla_in_fwd", grid=(s // tm,),
        in_specs=[row(D_MODEL), full(g0), full(w_in), full(g_cq), full(g_ckv), row(128), row(128), row(128)],
        out_specs=[row(D_MODEL), row(PROJ_PAD), row(CQ), row(CKV), row(128)],
        out_shape=[_sds((s, D_MODEL), BF16), _sds((s, PROJ_PAD), F32), _sds((s, CQ), BF16),
                   _sds((s, CKV), BF16), _sds((s, 128), BF16)],
    )


def _qkv_proj(cqn, ckvn, krr, w_uq, w_ukv, cc, sn, sp, carry=None):
    s = cqn.shape[0]
    tm = min(_FFN_ROWS, s)

    def body(cqn_ref, ckvn_ref, krr_ref, wuq_ref, wukv_ref, cc_ref, sn_ref, sp_ref, q_ref, k_ref, v_ref):
        cqn_b, ckvn_b, krr_b = cqn_ref[...], ckvn_ref[...], krr_ref[...]
        cc_b, sn_b, sp_b = cc_ref[...], sn_ref[...], sp_ref[...]
        for h in range(N_HEADS):
            q = _dot(cqn_b, wuq_ref[h]) * Q_SCALE
            q_ref[h, :, :NOPE] = q[:, :NOPE].astype(BF16)
            q_ref[h, :, NOPE:] = _rope(q[:, NOPE:], cc_b, sn_b, sp_b).astype(BF16)
            kv = _dot(ckvn_b, wukv_ref[h])
            k_ref[h, :, :NOPE] = kv[:, :NOPE].astype(BF16)
            k_ref[h, :, NOPE:] = krr_b
            v_ref[h] = kv[:, NOPE:].astype(BF16)

    row = lambda n: pl.BlockSpec((tm, n), lambda i: (i, 0))
    head_w = lambda a: pl.BlockSpec(a.shape, lambda i: (0, 0, 0))
    head_o = lambda n: pl.BlockSpec((N_HEADS, tm, n), lambda i: (0, i, 0))
    return _pc_carrying(
        body, carry, (cqn, ckvn, krr, w_uq, w_ukv, cc, sn, sp), name="qkv_proj", grid=(s // tm,),
        in_specs=[row(CQ), row(CKV), row(128), head_w(w_uq), head_w(w_ukv), row(128), row(128), row(128)],
        out_specs=[head_o(HEAD_PAD), head_o(HEAD_PAD), head_o(V_DIM)],
        out_shape=[_sds((N_HEADS, s, HEAD_PAD), BF16), _sds((N_HEADS, s, HEAD_PAD), BF16),
                   _sds((N_HEADS, s, V_DIM), BF16)],
    )


def _chunk_bias(bq, bk, first_key=0):
    rows = lax.broadcasted_iota(jnp.int32, (bq, bk), 0)
    cols = lax.broadcasted_iota(jnp.int32, (bq, bk), 1) + first_key
    visible = jnp.right_shift(cols, CHUNK_SHIFT) <= jnp.right_shift(rows, CHUNK_SHIFT)
    return jnp.where(visible, 0.0, NEG).astype(F32)


_ATTN_FWD_ROWS = 2048


def _attn_fwd(q, k, v, carry=None):
    s = q.shape[1]
    bk = _row_block(s)
    bq = min(_ATTN_FWD_ROWS, s)
    nd = bq // bk

    def body(q_ref, k_ref, v_ref, o_ref, lse_ref, bias_ref):
        i = pl.program_id(1)
        qb = q_ref[0]

        @pl.when((pl.program_id(0) == 0) & (i == 0))
        def _():
            bias_ref[...] = _chunk_bias(bq, bk)

        def block(j, carry, queries, bias=None):
            m, l, acc = carry
            start = pl.multiple_of(j * bk, bk)
            kb = k_ref[0, pl.ds(start, bk), :]
            vb = v_ref[0, pl.ds(start, bk), :]
            sc = _dot_nt(queries, kb)
            if bias is not None:
                sc = sc + bias
            m_new = jnp.maximum(m, jnp.max(sc, axis=-1, keepdims=True))
            p = jnp.exp2(sc - m_new)
            alpha = jnp.exp2(m - m_new)
            l = alpha * l + jnp.sum(p, axis=-1, keepdims=True)
            acc = alpha * acc + _dot(p.astype(BF16), vb)
            return m_new, l, acc

        carry = (jnp.full((bq, 1), NEG, F32), jnp.zeros((bq, 1), F32), jnp.zeros((bq, V_DIM), F32))
        carry = lax.fori_loop(0, i * nd, lambda j, c: block(j, c, qb), carry)
        for d in range(nd):
            first = d * bk
            carry = block(i * nd + d, carry, qb[first:], bias_ref[:bq - first, :])
            m, l, acc = (c[:bk] for c in carry)
            o_ref[first:first + bk, :] = (acc / l).astype(BF16)
            lse_ref[0, first:first + bk, :] = jnp.broadcast_to(m + jnp.log(l) * LOG2_E, (bk, 128))
            if d < nd - 1:
                carry = tuple(c[bk:] for c in carry)

    return _pc_carrying(
        body, carry, (q, k, v), name="attn_fwd", grid=(N_HEADS, s // bq),
        in_specs=[pl.BlockSpec((1, bq, HEAD_PAD), lambda h, i: (h, i, 0)),
                  pl.BlockSpec((1, s, HEAD_PAD), lambda h, i: (h, 0, 0)),
                  pl.BlockSpec((1, s, V_DIM), lambda h, i: (h, 0, 0))],
        out_specs=[pl.BlockSpec((bq, V_DIM), lambda h, i: (i, h)),
                   pl.BlockSpec((1, bq, 128), lambda h, i: (h, i, 0))],
        out_shape=[_sds((s, N_HEADS * V_DIM), BF16), _sds((N_HEADS, s, 128), F32)],
        scratch_shapes=[pltpu.VMEM((bq, bk), F32)],
    )


def _matmul_res(a, w, res, name):
    s, kd = a.shape
    n = w.shape[1]
    tm = min(_FFN_ROWS, s)

    def body(a_ref, w_ref, r_ref, o_ref):
        o_ref[...] = r_ref[...] + _dot(a_ref[...], w_ref[...])

    return _pc(
        body, name=name, grid=(s // tm,),
        in_specs=[pl.BlockSpec((tm, kd), lambda i: (i, 0)), pl.BlockSpec((kd, n), lambda i: (0, 0)),
                  pl.BlockSpec((tm, n), lambda i: (i, 0))],
        out_specs=pl.BlockSpec((tm, n), lambda i: (i, 0)),
        out_shape=_sds((s, n), F32),
    )(a, w, res)


def _ffn_fwd(h, gain, wg, wu, wd, name, carry=None):
    s = h.shape[0]
    tm = min(_FFN_ROWS, s)
    tf = 512
    nf = FF_PAD // tf

    def body(h_ref, g_ref, wg_ref, wu_ref, wd_ref, o_ref, n_ref, gate_ref, up_ref, acc_ref):
        j = pl.program_id(1)

        @pl.when(j == 0)
        def _():
            n_ref[...] = _rms(h_ref[...], g_ref[...]).astype(BF16)
            acc_ref[...] = jnp.zeros_like(acc_ref)

        nb = n_ref[...]
        gate = _dot(nb, wg_ref[...])
        up = _dot(nb, wu_ref[...])
        gate_ref[...] = gate.astype(BF16)
        up_ref[...] = up.astype(BF16)
        act = gate * jax.nn.sigmoid(gate) * up
        acc_ref[...] += _dot(act.astype(BF16), wd_ref[...])

        @pl.when(j == nf - 1)
        def _():
            o_ref[...] = h_ref[...] + acc_ref[...]

    return _pc_carrying(
        body, carry, (h, gain, wg, wu, wd), name=name, grid=(s // tm, nf),
        in_specs=[pl.BlockSpec((tm, D_MODEL), lambda i, j: (i, 0)), pl.BlockSpec((1, D_MODEL), lambda i, j: (0, 0)),
                  pl.BlockSpec((D_MODEL, tf), lambda i, j: (0, j)), pl.BlockSpec((D_MODEL, tf), lambda i, j: (0, j)),
                  pl.BlockSpec((tf, D_MODEL), lambda i, j: (j, 0))],
        out_specs=[pl.BlockSpec((tm, D_MODEL), lambda i, j: (i, 0)), pl.BlockSpec((tm, D_MODEL), lambda i, j: (i, 0)),
                   pl.BlockSpec((tm, tf), lambda i, j: (i, j)), pl.BlockSpec((tm, tf), lambda i, j: (i, j))],
        out_shape=[_sds((s, D_MODEL), F32), _sds((s, D_MODEL), BF16), _sds((s, FF_PAD), BF16),
                   _sds((s, FF_PAD), BF16)],
        scratch_shapes=[pltpu.VMEM((tm, D_MODEL), F32)],
    )


def _rms_matmul(h, gain, w, name):
    s = h.shape[0]
    n = w.shape[1]
    tm = min(_FFN_ROWS, s)
    tn = 1024
    nn = n // tn

    def body(h_ref, g_ref, w_ref, n_ref, o_ref):
        @pl.when(pl.program_id(1) == 0)
        def _():
            n_ref[...] = _rms(h_ref[...], g_ref[...]).astype(BF16)

        o_ref[...] = _dot(n_ref[...], w_ref[...]).astype(BF16)

    return _pc(
        body, name=name, grid=(s // tm, nn),
        in_specs=[pl.BlockSpec((tm, D_MODEL), lambda i, j: (i, 0)), pl.BlockSpec((1, D_MODEL), lambda i, j: (0, 0)),
                  pl.BlockSpec((D_MODEL, tn), lambda i, j: (0, j))],
        out_specs=[pl.BlockSpec((tm, D_MODEL), lambda i, j: (i, 0)), pl.BlockSpec((tm, tn), lambda i, j: (i, j))],
        out_shape=[_sds((s, D_MODEL), BF16), _sds((s, n), BF16)],
    )(h, gain, w)


def _shift_down(u, k, rows):
    return jnp.where(rows >= k, pltpu.roll(u, k, 0), 0.0)


def _shift_up(u, k, rows, s):
    return jnp.where(rows < s - k, pltpu.roll(u, s - k, 0), 0.0)


_CONV_COLS = 128


def _conv_fwd(bcx, cw):
    s = bcx.shape[0]
    tc = _CONV_COLS
    nc = D_MODEL // tc

    def body(b_ref, c_ref, xp_ref, w_ref, o_ref):
        rows = lax.broadcasted_iota(jnp.int32, (s, tc), 0)
        u = c_ref[...].astype(F32) * xp_ref[...].astype(F32)
        w = w_ref[...]
        uc = w[2:3] * u + w[1:2] * _shift_down(u, 1, rows) + w[0:1] * _shift_down(u, 2, rows)
        o_ref[...] = (b_ref[...].astype(F32) * uc).astype(BF16)

    col = lambda off: pl.BlockSpec((s, tc), lambda j: (0, off + j))
    return _pc(
        body, name="conv_fwd", grid=(nc,),
        in_specs=[col(0), col(nc), col(2 * nc), pl.BlockSpec((3, tc), lambda j: (0, j))],
        out_specs=pl.BlockSpec((s, tc), lambda j: (0, j)),
        out_shape=_sds((s, D_MODEL), BF16),
    )(bcx, bcx, bcx, cw)


def _final_loss(h, gain, target):
    s = h.shape[0]
    tm = min(_FFN_ROWS, s)

    def body(h_ref, g_ref, t_ref, dh_ref, dg_ref, loss_ref):
        i = pl.program_id(0)

        @pl.when(i == 0)
        def _():
            dg_ref[...] = jnp.zeros_like(dg_ref)
            loss_ref[...] = jnp.zeros_like(loss_ref)

        hb = h_ref[...]
        e = _rms(hb, g_ref[...]) - t_ref[...]
        loss_ref[...] += 0.5 * jnp.sum(jnp.mean(e * e, axis=-1, keepdims=True))
        dx, dg = _rms_bwd(e * (1.0 / D_MODEL), hb, g_ref[...])
        dh_ref[...] = dx
        dg_ref[...] += dg

    row = pl.BlockSpec((tm, D_MODEL), lambda i: (i, 0))
    vec = pl.BlockSpec((1, D_MODEL), lambda i: (0, 0))
    return _pc(
        body, name="final_loss", grid=(s // tm,),
        in_specs=[row, vec, row],
        out_specs=[row, vec, pl.BlockSpec((8, D_MODEL), lambda i: (0, 0))],
        out_shape=[_sds((s, D_MODEL), F32), _sds((1, D_MODEL), F32), _sds((8, D_MODEL), F32)],
    )(h, gain, target)


def _ffn_bwd_x(dho, h, gain, gate, up, wg, wu, wd, name, carry=None):
    s = h.shape[0]
    tm = _row_block(s)
    tf = 1024
    nf = FF_PAD // tf

    def body(dho_ref, h_ref, g_ref, gate_ref, up_ref, wg_ref, wu_ref, wd_ref,
             dh_ref, dhb_ref, dhob_ref, dgate_ref, dup_ref, act_ref, dgain_ref, acc_ref):
        i = pl.program_id(0)
        j = pl.program_id(1)

        @pl.when(j == 0)
        def _():
            dhob_ref[...] = dho_ref[...].astype(BF16)
            acc_ref[...] = jnp.zeros_like(acc_ref)

        @pl.when((i == 0) & (j == 0))
        def _():
            dgain_ref[...] = jnp.zeros_like(dgain_ref)

        dact = _dot_nt(dhob_ref[...], wd_ref[...])
        g = gate_ref[...].astype(F32)
        u = up_ref[...].astype(F32)
        sg = jax.nn.sigmoid(g)
        silu = g * sg
        dg = (dact * u * (sg * (1.0 + g * (1.0 - sg)))).astype(BF16)
        du = (dact * silu).astype(BF16)
        dgate_ref[...] = dg
        dup_ref[...] = du
        act_ref[...] = (silu * u).astype(BF16)
        acc_ref[...] += _dot_nt(dg, wg_ref[...]) + _dot_nt(du, wu_ref[...])

        @pl.when(j == nf - 1)
        def _():
            dx, dgain = _rms_bwd(acc_ref[...], h_ref[...], g_ref[...])
            dh = dho_ref[...] + dx
            dh_ref[...] = dh
            dhb_ref[...] = dh.astype(BF16)
            dgain_ref[...] += dgain

    row = pl.BlockSpec((tm, D_MODEL), lambda i, j: (i, 0))
    vec = pl.BlockSpec((1, D_MODEL), lambda i, j: (0, 0))
    hid = pl.BlockSpec((tm, tf), lambda i, j: (i, j))
    wcol = pl.BlockSpec((D_MODEL, tf), lambda i, j: (0, j))
    wrow = pl.BlockSpec((tf, D_MODEL), lambda i, j: (j, 0))
    return _pc_carrying(
        body, carry, (dho, h, gain, gate, up, wg, wu, wd), name=name, grid=(s // tm, nf),
        in_specs=[row, row, vec, hid, hid, wcol, wcol, wrow],
        out_specs=[row, row, row, hid, hid, hid, vec],
        out_shape=[_sds((s, D_MODEL), F32), _sds((s, D_MODEL), BF16), _sds((s, D_MODEL), BF16),
                   _sds((s, FF_PAD), BF16), _sds((s, FF_PAD), BF16), _sds((s, FF_PAD), BF16),
                   _sds((1, D_MODEL), F32)],
        scratch_shapes=[pltpu.VMEM((tm, D_MODEL), F32)],
    )


def _nt_rmsbwd(a, w, h, gain, dho, name, carry=None):
    stacked = a.ndim == 3
    if stacked:
        nk, s, tk = a.shape
    else:
        s, tk = a.shape
        nk = 1
    tm = min(_FFN_ROWS, s)

    def body(a_ref, w_ref, h_ref, g_ref, dho_ref, dh_ref, dhb_ref, dgain_ref, acc_ref):
        i = pl.program_id(0)
        j = pl.program_id(1)
        if stacked:
            a_ref = a_ref.at[0]

        @pl.when(j == 0)
        def _():
            acc_ref[...] = jnp.zeros_like(acc_ref)

        @pl.when((i == 0) & (j == 0))
        def _():
            dgain_ref[...] = jnp.zeros_like(dgain_ref)

        acc_ref[...] += _dot_nt(a_ref[...], w_ref[...])

        @pl.when(j == nk - 1)
        def _():
            dx, dgain = _rms_bwd(acc_ref[...], h_ref[...], g_ref[...])
            dh = dho_ref[...] + dx
            dh_ref[...] = dh
            dhb_ref[...] = dh.astype(BF16)
            dgain_ref[...] += dgain

    row = pl.BlockSpec((tm, D_MODEL), lambda i, j: (i, 0))
    vec = pl.BlockSpec((1, D_MODEL), lambda i, j: (0, 0))
    return _pc_carrying(
        body, carry, (a, w, h, gain, dho), name=name, grid=(s // tm, nk),
        in_specs=[pl.BlockSpec((1, tm, tk), lambda i, j: (j, i, 0)) if stacked else pl.BlockSpec((tm, tk), lambda i, j: (i, 0)),
                  pl.BlockSpec((D_MODEL, tk), lambda i, j: (0, j)), row, vec, row],
        out_specs=[row, row, vec],
        out_shape=[_sds((s, D_MODEL), F32), _sds((s, D_MODEL), BF16), _sds((1, D_MODEL), F32)],
        scratch_shapes=[pltpu.VMEM((tm, D_MODEL), F32)],
    )


def _matmul_nt(a, w, name):
    s, kd = a.shape
    n = w.shape[0]
    tm = min(_FFN_ROWS, s)

    def body(a_ref, w_ref, o_ref):
        o_ref[...] = _dot_nt(a_ref[...], w_ref[...]).astype(BF16)

    return _pc(
        body, name=name, grid=(s // tm,),
        in_specs=[pl.BlockSpec((tm, kd), lambda i: (i, 0)), pl.BlockSpec((n, kd), lambda i: (0, 0))],
        out_specs=pl.BlockSpec((tm, n), lambda i: (i, 0)),
        out_shape=_sds((s, n), BF16),
    )(a, w)


def _matmul_tn(a, b, name, carry=None):
    s, m = a.shape
    stacked = b.ndim == 3
    tmm = min(m, 1024)
    if stacked:
        n, tn = b.shape[0] * b.shape[2], b.shape[2]
    else:
        n = b.shape[1]
        tn = n if n <= 1024 else 1024
    tk = min(_TN_ROWS, s)
    nk = s // tk

    def body(a_ref, b_ref, o_ref, acc_ref):
        k = pl.program_id(2)
        if stacked:
            b_ref = b_ref.at[0]

        @pl.when(k == 0)
        def _():
            acc_ref[...] = jnp.zeros_like(acc_ref)

        acc_ref[...] += _dot_tn(a_ref[...], b_ref[...])

        @pl.when(k == nk - 1)
        def _():
            o_ref[...] = acc_ref[...].astype(BF16)

    (out,), got = _pc_carrying(
        body, carry, (a, b), name=name, grid=(m // tmm, n // tn, nk),
        in_specs=[pl.BlockSpec((tk, tmm), lambda i, j, k: (k, i)),
                  pl.BlockSpec((1, tk, tn), lambda i, j, k: (j, k, 0)) if stacked
                  else pl.BlockSpec((tk, tn), lambda i, j, k: (k, j))],
        out_specs=[pl.BlockSpec((tmm, tn), lambda i, j, k: (i, j))],
        out_shape=[_sds((m, n), BF16)],
        scratch_shapes=[pltpu.VMEM((tmm, tn), F32)],
    )
    return out if carry is None else (out, got)


def _conv_bwd(dbuc, bcx, cw):
    s = bcx.shape[0]
    tc = _CONV_COLS
    nc = D_MODEL // tc

    def body(d_ref, b_ref, c_ref, xp_ref, w_ref, dbcx_ref, dw_ref):
        rows = lax.broadcasted_iota(jnp.int32, (s, tc), 0)
        c = c_ref[...].astype(F32)
        xp = xp_ref[...].astype(F32)
        u = c * xp
        u1 = _shift_down(u, 1, rows)
        u2 = _shift_down(u, 2, rows)
        w = w_ref[...]
        uc = w[2:3] * u + w[1:2] * u1 + w[0:1] * u2
        d = d_ref[...].astype(F32)
        dbcx_ref[0] = (d * uc).astype(BF16)
        duc = d * b_ref[...].astype(F32)
        du = w[2:3] * duc + w[1:2] * _shift_up(duc, 1, rows, s) + w[0:1] * _shift_up(duc, 2, rows, s)
        dbcx_ref[1] = (du * xp).astype(BF16)
        dbcx_ref[2] = (du * c).astype(BF16)
        dw_ref[0:1, :] = jnp.sum(duc * u2, axis=0, keepdims=True)
        dw_ref[1:2, :] = jnp.sum(duc * u1, axis=0, keepdims=True)
        dw_ref[2:3, :] = jnp.sum(duc * u, axis=0, keepdims=True)

    col = lambda off: pl.BlockSpec((s, tc), lambda j: (0, off + j))
    tap = pl.BlockSpec((3, tc), lambda j: (0, j))
    return _pc(
        body, name="conv_bwd", grid=(nc,),
        in_specs=[col(0), col(0), col(nc), col(2 * nc), tap],
        out_specs=[pl.BlockSpec((3, s, tc), lambda j: (0, 0, j)), tap],
        out_shape=[_sds((3, s, D_MODEL), BF16), _sds((3, D_MODEL), F32)],
    )(dbuc, bcx, bcx, bcx, cw)


def _attn_bwd(q, k, v, do, o, lse, cc, sn, sp, carry=None):
    s = q.shape[1]
    bk = _row_block(s)
    nb = s // bk

    def body(q_ref, k_ref, v_ref, do_ref, o_ref, lse_ref, cc_ref, sn_ref, sp_ref,
             dq_ref, dk_ref, dv_ref, dqacc_ref, delta_ref, bias_ref):
        j = pl.program_id(1)
        kb = k_ref[0]
        vb = v_ref[0]

        @pl.when((pl.program_id(0) == 0) & (j == 0))
        def _():
            bias_ref[...] = _chunk_bias(2 * bk, bk)

        @pl.when(j == 0)
        def _():
            dqacc_ref[...] = jnp.zeros_like(dqacc_ref)

            def fill(i, _):
                rows = pl.ds(pl.multiple_of(i * bk, bk), bk)
                d = jnp.sum(do_ref[rows, :].astype(F32) * o_ref[rows, :].astype(F32), axis=-1, keepdims=True)
                delta_ref[rows, :] = jnp.broadcast_to(d, (bk, 128))
                return 0

            lax.fori_loop(0, nb, fill, 0)

        def step(i, carry, masked, blocks=1):
            dk, dv = carry
            rows = pl.ds(pl.multiple_of(i * bk, bk), blocks * bk)
            qb = q_ref[0, rows, :]
            dob = do_ref[rows, :]
            sc = _dot_nt(qb, kb)
            if masked:
                sc = sc + bias_ref[:blocks * bk, :]
            p = jnp.exp2(sc - lse_ref[0, rows, :][:, :1])
            dv = dv + _dot_tn(p.astype(BF16), dob)
            ds = (p * (_dot_nt(dob, vb) - delta_ref[rows, :][:, :1])).astype(BF16)
            dk = dk + _dot_tn(ds, qb)
            dqacc_ref[rows, :] += _dot(ds, kb)
            return dk, dv

        rest = nb - 1 - j
        odd = lax.rem(rest, 2)
        carry = (jnp.zeros((bk, HEAD_PAD), F32), jnp.zeros((bk, V_DIM), F32))
        carry = lax.fori_loop(0, 1 - odd, lambda t, c: step(j, c, True), carry)
        if nb > 1:
            carry = lax.fori_loop(0, odd, lambda t, c: step(j, c, True, blocks=2), carry)
        dq = dqacc_ref[pl.ds(pl.multiple_of(j * bk, bk), bk), :] * ATT_SCALE
        dq_ref[:, :NOPE] = dq[:, :NOPE].astype(BF16)
        dq_ref[:, NOPE:] = _rope_bwd(dq[:, NOPE:], cc_ref[...], sn_ref[...], sp_ref[...]).astype(BF16)
        if nb > 1:
            carry = lax.fori_loop(
                0, rest // 2, lambda t, c: step(j + 1 + odd + 2 * t, c, False, blocks=2), carry)
        dk, dv = carry
        dk_ref[0] = (dk * LN_2).astype(BF16)
        dv_ref[0] = dv.astype(BF16)

    blk = lambda n: pl.BlockSpec((1, bk, n), lambda h, j: (h, j, 0))
    whole = lambda n: pl.BlockSpec((1, s, n), lambda h, j: (h, 0, 0))
    cols = pl.BlockSpec((s, V_DIM), lambda h, j: (0, h))
    tab = pl.BlockSpec((bk, 128), lambda h, j: (j, 0))
    return _pc_carrying(
        body, carry, (q, k, v, do, o, lse, cc, sn, sp), name="attn_bwd", grid=(N_HEADS, nb),
        in_specs=[whole(HEAD_PAD), blk(HEAD_PAD), blk(V_DIM), cols, cols, whole(128), tab, tab, tab],
        out_specs=[pl.BlockSpec((bk, HEAD_PAD), lambda h, j: (j, h)), blk(HEAD_PAD), blk(V_DIM)],
        out_shape=[_sds((s, N_HEADS * HEAD_PAD), BF16), _sds((N_HEADS, s, HEAD_PAD), BF16),
                   _sds((N_HEADS, s, V_DIM), BF16)],
        scratch_shapes=[pltpu.VMEM((s, HEAD_PAD), F32), pltpu.VMEM((s, 128), F32), pltpu.VMEM((2 * bk, bk), F32)],
    )


def _mla_mid_bwd(dq, dk, dv, proj, w_uq, w_ukv, g_cq, g_ckv, cc, sn, sp, carry=None):
    s = dq.shape[0]
    tm = _row_block(s)

    def body(dq_ref, dk_ref, dv_ref, proj_ref, wuq_ref, wukv_ref, gcq_ref, gckv_ref, cc_ref, sn_ref, sp_ref,
             dproj_ref, dkv_ref, dgcq_ref, dgckv_ref, acq_ref, ackv_ref, akr_ref):
        @pl.when(pl.program_id(0) == 0)
        def _():
            dgcq_ref[...] = jnp.zeros_like(dgcq_ref)
            dgckv_ref[...] = jnp.zeros_like(dgckv_ref)

        for h in range(N_HEADS):
            cols = slice(h * HEAD_PAD, (h + 1) * HEAD_PAD)
            dkb = dk_ref[h]
            dkv_ref[:, h * HEAD_PAD:h * HEAD_PAD + NOPE] = dkb[:, :NOPE]
            dkv_ref[:, h * HEAD_PAD + NOPE:(h + 1) * HEAD_PAD] = dv_ref[h]
            cq_part = _dot_nt(dq_ref[:, cols], wuq_ref[h])
            ckv_part = _dot_nt(dkv_ref[:, cols], wukv_ref[h])
            kr_part = dkb[:, NOPE:].astype(F32)
            if h == 0:
                acq_ref[...], ackv_ref[...], akr_ref[...] = cq_part, ckv_part, kr_part
            else:
                acq_ref[...] += cq_part
                ackv_ref[...] += ckv_part
                akr_ref[...] += kr_part

        proj = proj_ref[...]
        dcq, dgcq = _rms_bwd(acq_ref[...], proj[:, :CQ], gcq_ref[...])
        dckv, dgckv = _rms_bwd(ackv_ref[...], proj[:, CQ:CQ + CKV], gckv_ref[...])
        dproj_ref[:, :CQ] = dcq.astype(BF16)
        dproj_ref[:, CQ:CQ + CKV] = dckv.astype(BF16)
        dproj_ref[:, CQ + CKV:] = _rope_bwd(akr_ref[...], cc_ref[...], sn_ref[...], sp_ref[...]).astype(BF16)
        dgcq_ref[...] += dgcq
        dgckv_ref[...] += dgckv

    head_blk = lambda n: pl.BlockSpec((N_HEADS, tm, n), lambda i: (0, i, 0))
    head_cols = pl.BlockSpec((tm, N_HEADS * HEAD_PAD), lambda i: (i, 0))
    head_w = lambda a: pl.BlockSpec(a.shape, lambda i: (0, 0, 0))
    row = lambda n: pl.BlockSpec((tm, n), lambda i: (i, 0))
    vec = lambda n: pl.BlockSpec((1, n), lambda i: (0, 0))
    return _pc_carrying(
        body, carry, (dq, dk, dv, proj, w_uq, w_ukv, g_cq, g_ckv, cc, sn, sp),
        name="mla_mid_bwd", grid=(s // tm,),
        in_specs=[head_cols, head_blk(HEAD_PAD), head_blk(V_DIM), row(PROJ_PAD), head_w(w_uq), head_w(w_ukv),
                  vec(CQ), vec(CKV), row(128), row(128), row(128)],
        out_specs=[row(PROJ_PAD), head_cols, vec(CQ), vec(CKV)],
        out_shape=[_sds((s, PROJ_PAD), BF16), _sds((s, N_HEADS * HEAD_PAD), BF16), _sds((1, CQ), F32),
                   _sds((1, CKV), F32)],
        scratch_shapes=[pltpu.VMEM((tm, CQ), F32), pltpu.VMEM((tm, CKV), F32), pltpu.VMEM((tm, 128), F32)],
    )


def _peer(k):
    x, y, c = lax.axis_index("x"), lax.axis_index("y"), lax.axis_index("c")
    px = 1 - x if k & 4 else x
    py = 1 - y if k & 2 else y
    pc = 1 - c if k & 1 else c
    return (px, py, pc), 4 * px + 2 * py + pc


def _exchange_ops(arrays, scatter):
    nw = len(arrays)
    by_cols = [isinstance(a, _Cols) for a in arrays]
    used_rows = [a.rows if isinstance(a, _Rows) else None for a in arrays]
    arrays = [a.array if isinstance(a, (_Cols, _Rows)) else a for a in arrays]
    direct = range(1, N_DEV) if scatter else (1, 2, 4, 6)
    assert scatter or not any(used_rows)

    def columns(ref, idx, width):
        return ref.at[:, pl.ds(pl.multiple_of(idx * width, 128), width)]

    def sent(ins, w, idx):
        if not scatter:
            return ins[w]
        if by_cols[w]:
            return columns(ins[w], idx, arrays[w].shape[1] // N_DEV)
        return ins[w].at[idx, pl.ds(0, used_rows[w])] if used_rows[w] else ins[w].at[idx]

    def slot(outs, w, idx):
        if by_cols[w] and not scatter:
            return columns(outs[w], idx, arrays[w].shape[1])
        return outs[w].at[idx]

    def copy(w, k, src, dst, to, send_sems, recv_sems):
        return pltpu.make_async_remote_copy(
            src_ref=src, dst_ref=dst, send_sem=send_sems.at[w * N_DEV + k], recv_sem=recv_sems.at[w * N_DEV + k],
            device_id=to, device_id_type=pl.DeviceIdType.MESH)

    def own_copies(ins, outs, local_sems):
        _, me = _peer(0)
        return [pltpu.make_async_copy(sent(ins, w, me), slot(outs, w, me), local_sems.at[w]) for w in range(nw)]

    def sends(ins, outs, send_sems, recv_sems):
        _, me = _peer(0)
        out = []
        for k in direct:
            dev, idx = _peer(k)
            for w in range(nw):
                out.append(copy(w, k, sent(ins, w, idx), slot(outs, w, me), dev, send_sems, recv_sems))
        return out

    def relays(outs, send_sems, recv_sems):
        sibling, _ = _peer(1)
        out = []
        for k in (2, 4, 6):
            _, idx = _peer(k)
            for w in range(nw):
                out.append(copy(w, k + 1, slot(outs, w, idx), slot(outs, w, idx), sibling, send_sems, recv_sems))
        return out

    def arrival(outs, w, k, send_sems, recv_sems):
        dev, idx = _peer(k)
        return copy(w, k, slot(outs, w, idx), slot(outs, w, idx), dev, send_sems, recv_sems)

    def start(ins, outs, send_sems, recv_sems, local_sems):
        for cp in own_copies(ins, outs, local_sems) + sends(ins, outs, send_sems, recv_sems):
            cp.start()

    def relay(ins, outs, send_sems, recv_sems, local_sems):
        for k in (2, 4, 6):
            for w in range(nw):
                arrival(outs, w, k, send_sems, recv_sems).wait_recv()
        for cp in relays(outs, send_sems, recv_sems):
            cp.start()

    def wait(ins, outs, send_sems, recv_sems, local_sems):
        for cp in own_copies(ins, outs, local_sems):
            cp.wait()
        for cp in sends(ins, outs, send_sems, recv_sems) + ([] if scatter else relays(outs, send_sems, recv_sems)):
            cp.wait_send()
        for k in (range(1, N_DEV) if scatter else (1, 3, 5, 7)):
            for w in range(nw):
                arrival(outs, w, k, send_sems, recv_sems).wait_recv()

    def landed_shape(a, cols, rows):
        if rows:
            return (N_DEV, rows) + a.shape[2:]
        if not cols:
            return a.shape if scatter else (N_DEV,) + a.shape
        r, c = a.shape
        return (N_DEV, r, c // N_DEV) if scatter else (r, N_DEV * c)

    landed = [_sds(landed_shape(a, cols, rows), a.dtype) for a, cols, rows in zip(arrays, by_cols, used_rows)]
    sems = [pltpu.SemaphoreType.DMA((nw * N_DEV,)), pltpu.SemaphoreType.DMA((nw * N_DEV,)),
            pltpu.SemaphoreType.DMA((nw,))]
    return start, (None if scatter else relay), wait, landed, sems, arrays


class _Cols:
    def __init__(self, array):
        self.array = array


class _Rows:
    def __init__(self, array, rows):
        self.array, self.rows = array, rows


def _exchange(arrays, scatter, name):
    nw = len(arrays)
    start, relay, wait, landed, sems, arrays = _exchange_ops(arrays, scatter)

    def body(*refs):
        ins, outs, csem = refs[:nw], refs[nw:2 * nw], refs[2 * nw:]
        start(ins, outs, *csem)
        if relay is not None:
            relay(ins, outs, *csem)
        wait(ins, outs, *csem)

    any_spec = pl.BlockSpec(memory_space=pl.ANY)
    return _pc(body, name=name, in_specs=[any_spec] * nw, out_specs=[any_spec] * nw, out_shape=landed,
               scratch_shapes=sems)(*arrays)


def _adam_math(g, w, m, v):
    m = ADAM_B1 * m + (1.0 - ADAM_B1) * g
    v = ADAM_B2 * v + (1.0 - ADAM_B2) * jnp.square(g)
    m_hat = m / (1.0 - ADAM_B1 ** ADAM_STEP)
    v_hat = v / (1.0 - ADAM_B2 ** ADAM_STEP)
    delta = -ADAM_LR * (m_hat / (jnp.sqrt(v_hat) + ADAM_EPS) + ADAM_WD * w)
    return delta, m, v


def _adam_rows(r):
    for t in range(min(r, 512) // 16 * 16, 0, -16):
        if r % t == 0:
            return t
    return r


def _adamw(parts, w, m, v, name):
    nl, r, c = w.shape
    tr = _adam_rows(r)
    nr = r // tr

    def body(*refs):
        p_refs, (w_ref, m_ref, v_ref, g_ref, d_ref, mo_ref, vo_ref) = refs[:nl], refs[nl:]
        for layer in range(nl):
            @pl.when(pl.program_id(0) == layer)
            def _(p_ref=p_refs[layer]):
                g = p_ref[0].astype(F32)
                for src in range(1, N_DEV):
                    g = g + p_ref[src].astype(F32)
                delta, m2, v2 = _adam_math(g, w_ref[0], m_ref[0], v_ref[0])
                g_ref[0] = g
                d_ref[0] = delta
                mo_ref[0] = m2
                vo_ref[0] = v2

    def part_spec(layer):
        def index(l, i):
            return 0, jnp.where(l == layer, i, jnp.where(l < layer, 0, nr - 1)), 0
        return pl.BlockSpec((N_DEV, tr, c), index)

    blk = pl.BlockSpec((1, tr, c), lambda l, i: (l, i, 0))
    return _pc(
        body, name=name, grid=(nl, nr),
        in_specs=[part_spec(layer) for layer in range(nl)] + [blk, blk, blk],
        out_specs=[blk, blk, blk, blk],
        out_shape=[_sds((nl, r, c), F32)] * 4,
    )(*parts, w, m, v)


def _small_allreduce_adamw(gpack, wpack, mpack, vpack, last_grads):
    shape = gpack.shape
    compact = (SMALL_ROWS, D_MODEL)
    nw = len(last_grads)
    start, _, wait, landed, exchange_sems, last_grads = _exchange_ops(last_grads, True)

    def body(*refs):
        (g_ref, w_ref, m_ref, v_ref), rest = refs[:4], refs[4:]
        cin, rest = rest[:nw], rest[nw:]
        (go_ref, d_ref, mo_ref, vo_ref), rest = rest[:4], rest[4:]
        cout, rest = rest[:nw], rest[nw:]
        (comp_ref, gath_ref, send_sems, recv_sems), csem = rest[:4], rest[4:]
        start(cin, cout, *csem)
        _, me = _peer(0)
        comp_ref[...] = jnp.zeros(compact, F32)
        r = 0
        for p, n in enumerate(SMALL_PIECES):
            comp_ref[r:r + n, :] = g_ref[8 * p:8 * p + n, :]
            r += n
        gath_ref[me] = comp_ref[...]
        copies = []
        for k in range(1, N_DEV):
            dev, idx = _peer(k)
            copies.append(pltpu.make_async_remote_copy(
                src_ref=comp_ref, dst_ref=gath_ref.at[me], send_sem=send_sems.at[k], recv_sem=recv_sems.at[k],
                device_id=dev, device_id_type=pl.DeviceIdType.MESH))
        for cp in copies:
            cp.start()
        for cp in copies:
            cp.wait_send()
        for k in range(1, N_DEV):
            dev, idx = _peer(k)
            pltpu.make_async_remote_copy(
                src_ref=comp_ref, dst_ref=gath_ref.at[idx], send_sem=send_sems.at[k], recv_sem=recv_sems.at[k],
                device_id=dev, device_id_type=pl.DeviceIdType.MESH).wait_recv()
        g = gath_ref[0]
        for src in range(1, N_DEV):
            g = g + gath_ref[src]
        go_ref[...] = jnp.zeros(shape, F32)
        r = 0
        for p, n in enumerate(SMALL_PIECES):
            go_ref[8 * p:8 * p + n, :] = g[r:r + n, :]
            r += n
        delta, m2, v2 = _adam_math(go_ref[...], w_ref[...], m_ref[...], v_ref[...])
        d_ref[...] = delta
        mo_ref[...] = m2
        vo_ref[...] = v2
        wait(cin, cout, *csem)

    vm = pl.BlockSpec(memory_space=pltpu.VMEM)
    any_spec = pl.BlockSpec(memory_space=pl.ANY)
    res = _pc(
        body, name="small_allreduce_adamw",
        in_specs=[vm] * 4 + [any_spec] * nw, out_specs=[vm] * 4 + [any_spec] * nw,
        out_shape=[_sds(shape, F32)] * 4 + landed,
        scratch_shapes=[pltpu.VMEM(compact, F32), pltpu.VMEM((N_DEV,) + compact, F32),
                        pltpu.SemaphoreType.DMA((N_DEV,)), pltpu.SemaphoreType.DMA((N_DEV,))] + exchange_sems,
    )(gpack, wpack, mpack, vpack, *last_grads)
    return res[:4], res[4:]


def _rows_from_shards(g):
    return g.reshape(N_DEV * g.shape[1], g.shape[2])


def _shards_from_rows(a):
    return a.reshape(N_DEV, a.shape[0] // N_DEV, a.shape[1])


def _pad_to(a, rows, cols):
    return jnp.pad(a, ((0, rows - a.shape[0]), (0, cols - a.shape[1])))


def _rope_tables(pos):
    inv_freq = 1.0 / (ROPE_THETA ** (jnp.arange(0, ROPE, 2, dtype=F32) / ROPE))
    ang = pos.astype(F32)[:, None] * inv_freq
    cos, sin = jnp.cos(ang), jnp.sin(ang)
    z32, z64, z96 = (jnp.zeros((pos.shape[0], n), F32) for n in (32, 64, 96))
    return (jnp.concatenate([cos, cos, z64], axis=1), jnp.concatenate([-sin, z96], axis=1),
            jnp.concatenate([z32, sin, z64], axis=1))


def _pad_row(vec):
    vec = vec.reshape(1, -1)
    return jnp.pad(vec, ((0, 0), (0, D_MODEL - vec.shape[1])))


def _forward_backward(x, target, tables, gains, shards):
    cc, sn, sp = tables
    ffn_g = gains["ffn_norm"]
    by_cols = ("wg0", "wu0", "wg1", "wu1", "c_in")

    def gather(names, relay_at):
        return [_Cols(shards[n]) if n in by_cols else shards[n] for n in names], False, relay_at

    (got,) = _exchange([shards["w_in"]], False, "gather_w_in")
    w_in = _rows_from_shards(got)
    (n0, proj, cqn, ckvn, krr), (w_uq, w_ukv) = _mla_in_fwd(
        x, gains["mla_norm"], w_in, gains["g_cq"], gains["g_ckv"], cc, sn, sp, carry=gather(["w_uq", "w_ukv"], 0.99))
    (q, k, v), (w_o, conv_norm, conv_w) = _qkv_proj(
        cqn, ckvn, krr, w_uq, w_ukv, cc, sn, sp, carry=gather(["w_o", "conv_norm", "conv_w"], 0.5))
    w_o = _rows_from_shards(w_o)
    conv_norm = conv_norm.reshape(1, D_MODEL)
    conv_w = jnp.transpose(conv_w, (1, 0, 2)).reshape(3, D_MODEL)
    (o, lse), (wg0, wu0, wd0, c_in, c_out, wd1) = _attn_fwd(
        q, k, v, carry=gather(["wg0", "wu0", "wd0", "c_in", "c_out", "wd1"], 0.9))
    wd0 = _rows_from_shards(wd0)
    wd1 = _rows_from_shards(wd1)
    c_out = _rows_from_shards(c_out)
    h1 = _matmul_res(o, w_o, x, "mla_out_fwd")
    (h2, n1, gate0, up0), (wg1, wu1) = _ffn_fwd(
        h1, ffn_g[0:1], wg0, wu0, wd0, "ffn0_fwd", carry=gather(["wg1", "wu1"], 0.7))
    n2, bcx = _rms_matmul(h2, conv_norm, c_in, "conv_in_fwd")
    bu = _conv_fwd(bcx, conv_w)
    h3 = _matmul_res(bu, c_out, h2, "conv_out_fwd")
    (h4, n3, gate1, up1), _ = _ffn_fwd(h3, ffn_g[1:2], wg1, wu1, wd1, "ffn1_fwd")
    dh4, d_final, loss = _final_loss(h4, gains["final_norm"], target)

    small = {"final_norm": d_final}
    parts = {}

    def scatter(**blocks):
        return list(blocks), (list(blocks.values()), True, None)

    (dh3, dh3_b, dh4_b, dgate, dup, act, d_ffn1), _ = _ffn_bwd_x(
        dh4, h3, ffn_g[1:2], gate1, up1, wg1, wu1, wd1, "ffn1_bwd")
    dwg1 = _matmul_tn(dgate, n3, "ffn1_dwg")
    dwu1 = _matmul_tn(dup, n3, "ffn1_dwu")
    dwd1 = _matmul_tn(act, dh4_b, "ffn1_dwd")
    dbuc = _matmul_nt(dh3_b, c_out, "conv_out_bwd")
    d_c_out = _matmul_tn(bu, dh3_b, "conv_dwout")
    dbcx, small["conv_w"] = _conv_bwd(dbuc, bcx, conv_w)
    d_c_in = _matmul_tn(n2, dbcx, "conv_dwin")
    names, carry = scatter(c_out=_shards_from_rows(d_c_out))
    (dh2, dh2_b, small["conv_norm"]), got = _nt_rmsbwd(
        dbcx, c_in, h2, conv_norm, dh3, "conv_in_bwd", carry=carry)
    parts.update(zip(names, got))
    ffn_blocks = lambda d: _Rows(_shards_from_rows(d), FF_SHARD)
    names, carry = scatter(c_in=_Cols(d_c_in), wd1=ffn_blocks(dwd1))
    (dh1, dh1_b, dh2_b2, dgate, dup, act, d_ffn0), got = _ffn_bwd_x(
        dh2, h1, ffn_g[0:1], gate0, up0, wg0, wu0, wd0, "ffn0_bwd", carry=carry)
    parts.update(zip(names, got))
    dwg0 = _matmul_tn(dgate, n1, "ffn0_dwg")
    dwu0 = _matmul_tn(dup, n1, "ffn0_dwu")
    dwd0 = _matmul_tn(act, dh2_b2, "ffn0_dwd")
    small["ffn_norm"] = jnp.pad(d_ffn0, ((0, 7), (0, 0))) + jnp.pad(d_ffn1, ((1, 6), (0, 0)))
    do = _matmul_nt(dh1_b, w_o, "mla_out_bwd")
    d_w_o = _matmul_tn(o, dh1_b, "mla_dwo")
    names, carry = scatter(**{n: ffn_blocks(d) for n, d in dict(
        wg0=dwg0, wg1=dwg1, wu0=dwu0, wu1=dwu1, wd0=dwd0).items()})
    (dq, dk, dv), got = _attn_bwd(q, k, v, do, o, lse, cc, sn, sp, carry=carry)
    parts.update(zip(names, got))
    d_w_uq = _matmul_tn(dq, cqn, "mla_dwuq")
    names, carry = scatter(w_o=_shards_from_rows(d_w_o), w_uq=_Rows(_shards_from_rows(d_w_uq), NOPE + ROPE))
    (dproj, dkv, small["g_cq"], small["g_ckv"]), got = _mla_mid_bwd(
        dq, dk, dv, proj, w_uq, w_ukv, gains["g_cq"], gains["g_ckv"], cc, sn, sp, carry=carry)
    parts.update(zip(names, got))
    d_w_ukv = _matmul_tn(ckvn, dkv, "mla_dwukv")
    names, carry = scatter(w_ukv=_Cols(d_w_ukv))
    d_w_in, got = _matmul_tn(dproj, n0, "mla_dwin", carry=carry)
    parts.update(zip(names, got))
    (dx, _, small["mla_norm"]), _ = _nt_rmsbwd(dproj, w_in, x, gains["mla_norm"], dh1, "mla_in_bwd")
    return loss, dx, parts, _Cols(d_w_in), small


def kernel(x, positions, mla_norm, mla_w_in, mla_g_cq, mla_g_ckv, mla_w_uq, mla_w_ukv, mla_w_o, conv_norm, conv_w_in, conv_w, conv_w_out, ffn_norm, ffn_w_gate, ffn_w_up, ffn_w_down, final_norm, loss_target, m_mla_norm, m_mla_w_in, m_mla_g_cq, m_mla_g_ckv, m_mla_w_uq, m_mla_w_ukv, m_mla_w_o, m_conv_norm, m_conv_w_in, m_conv_w, m_conv_w_out, m_ffn_norm, m_ffn_w_gate, m_ffn_w_up, m_ffn_w_down, m_final_norm, v_mla_norm, v_mla_w_in, v_mla_g_cq, v_mla_g_ckv, v_mla_w_uq, v_mla_w_ukv, v_mla_w_o, v_conv_norm, v_conv_w_in, v_conv_w, v_conv_w_out, v_ffn_norm, v_ffn_w_gate, v_ffn_w_up, v_ffn_w_down, v_final_norm):
    me = 4 * lax.axis_index("x") + 2 * lax.axis_index("y") + lax.axis_index("c")

    bf = lambda a, rows, cols: _pad_to(a.astype(BF16), rows, cols)
    shards = dict(
        w_in=bf(mla_w_in[0], D_MODEL // N_DEV, PROJ_PAD), w_uq=bf(mla_w_uq[0], CQ, HEAD_PAD),
        w_ukv=mla_w_ukv[0].astype(BF16), w_o=mla_w_o[0].astype(BF16),
        c_in=conv_w_in[0].astype(BF16), c_out=conv_w_out[0].astype(BF16),
        conv_norm=conv_norm, conv_w=conv_w[0])
    for l in range(2):
        shards.update({f"wg{l}": bf(ffn_w_gate[l], D_MODEL, FF_SHARD_PAD), f"wu{l}": bf(ffn_w_up[l], D_MODEL, FF_SHARD_PAD),
                       f"wd{l}": bf(ffn_w_down[l], FF_SHARD_PAD, D_MODEL)})
    gains = dict(mla_norm=mla_norm, g_cq=mla_g_cq, g_ckv=mla_g_ckv, ffn_norm=ffn_norm,
                 final_norm=final_norm.reshape(1, -1))
    loss_local, dx, parts, d_w_in, grads = _forward_backward(
        x[0], loss_target[0], _rope_tables(positions[0]), gains, shards)

    col0 = me * (D_MODEL // N_DEV)

    def place(shard):
        return lax.dynamic_update_slice(jnp.zeros((shard.shape[0], D_MODEL), F32), shard, (0, col0))

    no_loss = jnp.zeros((8, D_MODEL), F32)

    def pack(mla_n, g_cq, g_ckv, ffn_n, fin_n, conv_n, conv_taps, loss_tile):
        rows = [mla_n, _pad_row(g_cq), _pad_row(g_ckv), ffn_n, fin_n.reshape(1, -1), conv_n, conv_taps, loss_tile]
        assert all(r.shape[0] in (n, 8) for r, n in zip(rows, SMALL_PIECES))
        return jnp.concatenate([jnp.pad(r, ((0, 8 - r.shape[0]), (0, 0))) for r in rows], axis=0)

    gpack = pack(grads["mla_norm"], grads["g_cq"], grads["g_ckv"], grads["ffn_norm"], grads["final_norm"],
                 grads["conv_norm"], grads["conv_w"], loss_local)
    wpack = pack(mla_norm, mla_g_cq, mla_g_ckv, ffn_norm, final_norm, place(conv_norm), place(conv_w[0]), no_loss)
    mpack = pack(m_mla_norm, m_mla_g_cq, m_mla_g_ckv, m_ffn_norm, m_final_norm, place(m_conv_norm),
                 place(m_conv_w[0]), no_loss)
    vpack = pack(v_mla_norm, v_mla_g_cq, v_mla_g_ckv, v_ffn_norm, v_final_norm, place(v_conv_norm),
                 place(v_conv_w[0]), no_loss)
    small, (parts["w_in"],) = _small_allreduce_adamw(gpack, wpack, mpack, vpack, [d_w_in])
    loss = small[0][56, 0]

    def adamw(name, w, m, v, partials, transposed=False):
        if transposed:
            w, m, v = (jnp.swapaxes(a, 1, 2) for a in (w, m, v))
        outs = _adamw(partials, w, m, v, "adamw_" + name)
        return [jnp.swapaxes(o, 1, 2) for o in outs] if transposed else outs

    res = dict(
        w_in=adamw("w_in", mla_w_in, m_mla_w_in, v_mla_w_in, [parts["w_in"]], True),
        w_uq=adamw("w_uq", mla_w_uq, m_mla_w_uq, v_mla_w_uq, [parts["w_uq"]], True),
        w_ukv=adamw("w_ukv", mla_w_ukv, m_mla_w_ukv, v_mla_w_ukv, [parts["w_ukv"]]),
        w_o=adamw("w_o", mla_w_o, m_mla_w_o, v_mla_w_o, [parts["w_o"]]),
        c_in=adamw("c_in", conv_w_in, m_conv_w_in, v_conv_w_in, [parts["c_in"]]),
        c_out=adamw("c_out", conv_w_out, m_conv_w_out, v_conv_w_out, [parts["c_out"]]),
        wg=adamw("wg", ffn_w_gate, m_ffn_w_gate, v_ffn_w_gate, [parts["wg0"], parts["wg1"]], True),
        wu=adamw("wu", ffn_w_up, m_ffn_w_up, v_ffn_w_up, [parts["wu0"], parts["wu1"]], True),
        wd=adamw("wd", ffn_w_down, m_ffn_w_down, v_ffn_w_down, [parts["wd0"], parts["wd1"]]),
    )

    def unpack(p):
        own = lambda rows: lax.dynamic_slice(rows, (0, col0), (rows.shape[0], D_MODEL // N_DEV))
        return dict(mla_norm=p[0:1], g_cq=p[8:9, :CQ], g_ckv=p[16:17, :CKV], ffn_norm=p[24:26], final_norm=p[32],
                    conv_norm=own(p[40:41]), conv_w=own(p[48:51])[None])

    small = [unpack(p) for p in small]
    order = ["mla_norm", "w_in", "g_cq", "g_ckv", "w_uq", "w_ukv", "w_o", "conv_norm", "c_in", "conv_w", "c_out",
             "ffn_norm", "wg", "wu", "wd", "final_norm"]
    out = [loss, dx[None]]
    for kind in range(4):
        for n in order:
            out.append(res[n][kind] if n in res else small[kind][n])
    return tuple(out)
```

```python
import math

import jax
import jax.numpy as jnp
from jax import lax
from jax.experimental import pallas as pl
from jax.experimental.pallas import tpu as pltpu

F32 = jnp.float32
BF16 = jnp.bfloat16

N_DEV = 8
D_MODEL = 1024
N_HEADS = 8
NOPE = 128
ROPE = 64
V_DIM = 128
HEAD_PAD = 256
CQ = 512
CKV = 256
PROJ_PAD = CQ + CKV + 128
FF_SHARD = 352
FF_SHARD_PAD = 384
FF_PAD = FF_SHARD_PAD * N_DEV
CHUNK_SHIFT = 6
RMS_EPS = 1e-6
ROPE_THETA = 10000.0
ATT_SCALE = 1.0 / math.sqrt(NOPE + ROPE)
LOG2_E = math.log2(math.e)
LN_2 = math.log(2.0)
Q_SCALE = ATT_SCALE * LOG2_E
NEG = -1e30

ADAM_LR = 0.001
ADAM_B1 = 0.9
ADAM_B2 = 0.999
ADAM_EPS = 1e-08
ADAM_WD = 0.01
ADAM_STEP = 10

SMALL_PIECES = (1, 1, 1, 2, 1, 1, 3, 1)
SMALL_ROWS = 16

_NT = (((1,), (1,)), ((), ()))
_TN = (((0,), (0,)), ((), ()))


def _pc(body, *, name, out_shape, grid=(), in_specs=None, out_specs=None, scratch_shapes=()):
    kwargs = dict(
        name=name, out_shape=out_shape, grid=grid, scratch_shapes=scratch_shapes,
        compiler_params=pltpu.CompilerParams(),
    )
    if in_specs is not None:
        kwargs["in_specs"] = in_specs
    if out_specs is not None:
        kwargs["out_specs"] = out_specs
    return pl.pallas_call(body, **kwargs)


def _pc_carrying(body, carry, operands, *, name, out_shape, grid, in_specs, out_specs, scratch_shapes=()):
    if carry is None:
        return _pc(body, name=name, out_shape=out_shape, grid=grid, in_specs=in_specs, out_specs=out_specs,
                   scratch_shapes=scratch_shapes)(*operands), None
    arrays, scatter, relay_at = carry
    nw, n_in, n_out, n_scr = len(arrays), len(in_specs), len(out_shape), len(scratch_shapes)
    start, relay, wait, landed_shapes, sems, arrays = _exchange_ops(arrays, scatter)
    n_steps = math.prod(grid)
    relay_step = None if relay is None else min(int(relay_at * n_steps), n_steps - 1)

    def wrapped(*refs):
        ins, rest = refs[:n_in], refs[n_in:]
        cin, rest = rest[:nw], rest[nw:]
        outs, rest = rest[:n_out], rest[n_out:]
        cout, rest = rest[:nw], rest[nw:]
        scr, csem = rest[:n_scr], rest[n_scr:]
        step = pl.program_id(0)
        for a in range(1, len(grid)):
            step = step * grid[a] + pl.program_id(a)

        @pl.when(step == 0)
        def _():
            start(cin, cout, *csem)

        if relay is not None:
            @pl.when(step == relay_step)
            def _():
                relay(cin, cout, *csem)

        body(*ins, *outs, *scr)

        @pl.when(step == n_steps - 1)
        def _():
            wait(cin, cout, *csem)

    any_spec = pl.BlockSpec(memory_space=pl.ANY)
    res = _pc(
        wrapped, name=name, grid=grid,
        in_specs=list(in_specs) + [any_spec] * nw, out_specs=list(out_specs) + [any_spec] * nw,
        out_shape=list(out_shape) + landed_shapes, scratch_shapes=list(scratch_shapes) + sems,
    )(*operands, *arrays)
    return res[:n_out], res[n_out:]


def _sds(shape, dtype):
    return jax.ShapeDtypeStruct(shape, dtype)


def _dot(a, b):
    return jnp.dot(a, b, preferred_element_type=F32)


def _dot_nt(a, b):
    return lax.dot_general(a, b, _NT, preferred_element_type=F32)


def _dot_tn(a, b):
    return lax.dot_general(a, b, _TN, preferred_element_type=F32)


def _rstd(x):
    return lax.rsqrt(jnp.mean(x * x, axis=-1, keepdims=True) + RMS_EPS)


def _rms(x, g):
    return (x * _rstd(x)) * g


def _rms_bwd(dy, x, g):
    r = _rstd(x)
    xhat = x * r
    dxhat = dy * g
    dx = r * (dxhat - xhat * jnp.mean(dxhat * xhat, axis=-1, keepdims=True))
    return dx, jnp.sum(dy * xhat, axis=0, keepdims=True)


def _rope(t, cc, sn, sp):
    return t * cc + pltpu.roll(t, 96, 1) * sn + pltpu.roll(t, 32, 1) * sp


def _rope_bwd(dt, cc, sn, sp):
    return dt * cc + pltpu.roll(dt * sn, 32, 1) + pltpu.roll(dt * sp, 96, 1)


def _row_block(s):
    return min(512, s)


_TN_ROWS = 2048
_FFN_ROWS = 1024


def _mla_in_fwd(x, g0, w_in, g_cq, g_ckv, cc, sn, sp, carry=None):
    s = x.shape[0]
    tm = min(_FFN_ROWS, s)

    def body(x_ref, g0_ref, w_ref, gcq_ref, gckv_ref, cc_ref, sn_ref, sp_ref,
             n_ref, proj_ref, cqn_ref, ckvn_ref, krr_ref):
        nb = _rms(x_ref[...], g0_ref[...]).astype(BF16)
        n_ref[...] = nb
        proj = _dot(nb, w_ref[...])
        proj_ref[...] = proj
        cqn_ref[...] = _rms(proj[:, :CQ], gcq_ref[...]).astype(BF16)
        ckvn_ref[...] = _rms(proj[:, CQ:CQ + CKV], gckv_ref[...]).astype(BF16)
        krr_ref[...] = _rope(proj[:, CQ + CKV:], cc_ref[...], sn_ref[...], sp_ref[...]).astype(BF16)

    row = lambda n: pl.BlockSpec((tm, n), lambda i: (i, 0))
    full = lambda a: pl.BlockSpec(a.shape, lambda i: (0, 0))
    return _pc_carrying(
        body, carry, (x, g0, w_in, g_cq, g_ckv, cc, sn, sp), name="mla_in_fwd", grid=(s // tm,),
        in_specs=[row(D_MODEL), full(g0), full(w_in), full(g_cq), full(g_ckv), row(128), row(128), row(128)],
        out_specs=[row(D_MODEL), row(PROJ_PAD), row(CQ), row(CKV), row(128)],
        out_shape=[_sds((s, D_MODEL), BF16), _sds((s, PROJ_PAD), F32), _sds((s, CQ), BF16),
                   _sds((s, CKV), BF16), _sds((s, 128), BF16)],
    )


def _qkv_proj(cqn, ckvn, krr, w_uq, w_ukv, cc, sn, sp, carry=None):
    s = cqn.shape[0]
    tm = min(_FFN_ROWS, s)

    def body(cqn_ref, ckvn_ref, krr_ref, wuq_ref, wukv_ref, cc_ref, sn_ref, sp_ref, q_ref, k_ref, v_ref):
        cqn_b, ckvn_b, krr_b = cqn_ref[...], ckvn_ref[...], krr_ref[...]
        cc_b, sn_b, sp_b = cc_ref[...], sn_ref[...], sp_ref[...]
        for h in range(N_HEADS):
            q = _dot(cqn_b, wuq_ref[h]) * Q_SCALE
            q_ref[h, :, :NOPE] = q[:, :NOPE].astype(BF16)
            q_ref[h, :, NOPE:] = _rope(q[:, NOPE:], cc_b, sn_b, sp_b).astype(BF16)
            kv = _dot(ckvn_b, wukv_ref[h])
            k_ref[h, :, :NOPE] = kv[:, :NOPE].astype(BF16)
            k_ref[h, :, NOPE:] = krr_b
            v_ref[h] = kv[:, NOPE:].astype(BF16)

    row = lambda n: pl.BlockSpec((tm, n), lambda i: (i, 0))
    head_w = lambda a: pl.BlockSpec(a.shape, lambda i: (0, 0, 0))
    head_o = lambda n: pl.BlockSpec((N_HEADS, tm, n), lambda i: (0, i, 0))
    return _pc_carrying(
        body, carry, (cqn, ckvn, krr, w_uq, w_ukv, cc, sn, sp), name="qkv_proj", grid=(s // tm,),
        in_specs=[row(CQ), row(CKV), row(128), head_w(w_uq), head_w(w_ukv), row(128), row(128), row(128)],
        out_specs=[head_o(HEAD_PAD), head_o(HEAD_PAD), head_o(V_DIM)],
        out_shape=[_sds((N_HEADS, s, HEAD_PAD), BF16), _sds((N_HEADS, s, HEAD_PAD), BF16),
                   _sds((N_HEADS, s, V_DIM), BF16)],
    )


def _chunk_bias(bq, bk, first_key=0):
    rows = lax.broadcasted_iota(jnp.int32, (bq, bk), 0)
    cols = lax.broadcasted_iota(jnp.int32, (bq, bk), 1) + first_key
    visible = jnp.right_shift(cols, CHUNK_SHIFT) <= jnp.right_shift(rows, CHUNK_SHIFT)
    return jnp.where(visible, 0.0, NEG).astype(F32)


_ATTN_FWD_ROWS = 2048


def _attn_fwd(q, k, v, carry=None):
    s = q.shape[1]
    bk = _row_block(s)
    bq = min(_ATTN_FWD_ROWS, s)
    nd = bq // bk

    def body(q_ref, k_ref, v_ref, o_ref, lse_ref, bias_ref):
        i = pl.program_id(1)
        qb = q_ref[0]

        @pl.when((pl.program_id(0) == 0) & (i == 0))
        def _():
            bias_ref[...] = _chunk_bias(bq, bk)

        def block(j, carry, queries, bias=None):
            m, l, acc = carry
            start = pl.multiple_of(j * bk, bk)
            kb = k_ref[0, pl.ds(start, bk), :]
            vb = v_ref[0, pl.ds(start, bk), :]
            sc = _dot_nt(queries, kb)
            if bias is not None:
                sc = sc + bias
            m_new = jnp.maximum(m, jnp.max(sc, axis=-1, keepdims=True))
            p = jnp.exp2(sc - m_new)
            alpha = jnp.exp2(m - m_new)
            l = alpha * l + jnp.sum(p, axis=-1, keepdims=True)
            acc = alpha * acc + _dot(p.astype(BF16), vb)
            return m_new, l, acc

        carry = (jnp.full((bq, 1), NEG, F32), jnp.zeros((bq, 1), F32), jnp.zeros((bq, V_DIM), F32))
        carry = lax.fori_loop(0, i * nd, lambda j, c: block(j, c, qb), carry)
        for d in range(nd):
            first = d * bk
            carry = block(i * nd + d, carry, qb[first:], bias_ref[:bq - first, :])
            m, l, acc = (c[:bk] for c in carry)
            o_ref[first:first + bk, :] = (acc / l).astype(BF16)
            lse_ref[0, first:first + bk, :] = jnp.broadcast_to(m + jnp.log(l) * LOG2_E, (bk, 128))
            if d < nd - 1:
                carry = tuple(c[bk:] for c in carry)

    return _pc_carrying(
        body, carry, (q, k, v), name="attn_fwd", grid=(N_HEADS, s // bq),
        in_specs=[pl.BlockSpec((1, bq, HEAD_PAD), lambda h, i: (h, i, 0)),
                  pl.BlockSpec((1, s, HEAD_PAD), lambda h, i: (h, 0, 0)),
                  pl.BlockSpec((1, s, V_DIM), lambda h, i: (h, 0, 0))],
        out_specs=[pl.BlockSpec((bq, V_DIM), lambda h, i: (i, h)),
                   pl.BlockSpec((1, bq, 128), lambda h, i: (h, i, 0))],
        out_shape=[_sds((s, N_HEADS * V_DIM), BF16), _sds((N_HEADS, s, 128), F32)],
        scratch_shapes=[pltpu.VMEM((bq, bk), F32)],
    )


def _matmul_res(a, w, res, name):
    s, kd = a.shape
    n = w.shape[1]
    tm = min(_FFN_ROWS, s)

    def body(a_ref, w_ref, r_ref, o_ref):
        o_ref[...] = r_ref[...] + _dot(a_ref[...], w_ref[...])

    return _pc(
        body, name=name, grid=(s // tm,),
        in_specs=[pl.BlockSpec((tm, kd), lambda i: (i, 0)), pl.BlockSpec((kd, n), lambda i: (0, 0)),
                  pl.BlockSpec((tm, n), lambda i: (i, 0))],
        out_specs=pl.BlockSpec((tm, n), lambda i: (i, 0)),
        out_shape=_sds((s, n), F32),
    )(a, w, res)


def _ffn_fwd(h, gain, wg, wu, wd, name, carry=None):
    s = h.shape[0]
    tm = min(_FFN_ROWS, s)
    tf = 512
    nf = FF_PAD // tf

    def body(h_ref, g_ref, wg_ref, wu_ref, wd_ref, o_ref, n_ref, gate_ref, up_ref, acc_ref):
        j = pl.program_id(1)

        @pl.when(j == 0)
        def _():
            n_ref[...] = _rms(h_ref[...], g_ref[...]).astype(BF16)
            acc_ref[...] = jnp.zeros_like(acc_ref)

        nb = n_ref[...]
        gate = _dot(nb, wg_ref[...])
        up = _dot(nb, wu_ref[...])
        gate_ref[...] = gate.astype(BF16)
        up_ref[...] = up.astype(BF16)
        act = gate * jax.nn.sigmoid(gate) * up
        acc_ref[...] += _dot(act.astype(BF16), wd_ref[...])

        @pl.when(j == nf - 1)
        def _():
            o_ref[...] = h_ref[...] + acc_ref[...]

    return _pc_carrying(
        body, carry, (h, gain, wg, wu, wd), name=name, grid=(s // tm, nf),
        in_specs=[pl.BlockSpec((tm, D_MODEL), lambda i, j: (i, 0)), pl.BlockSpec((1, D_MODEL), lambda i, j: (0, 0)),
                  pl.BlockSpec((D_MODEL, tf), lambda i, j: (0, j)), pl.BlockSpec((D_MODEL, tf), lambda i, j: (0, j)),
                  pl.BlockSpec((tf, D_MODEL), lambda i, j: (j, 0))],
        out_specs=[pl.BlockSpec((tm, D_MODEL), lambda i, j: (i, 0)), pl.BlockSpec((tm, D_MODEL), lambda i, j: (i, 0)),
                   pl.BlockSpec((tm, tf), lambda i, j: (i, j)), pl.BlockSpec((tm, tf), lambda i, j: (i, j))],
        out_shape=[_sds((s, D_MODEL), F32), _sds((s, D_MODEL), BF16), _sds((s, FF_PAD), BF16),
                   _sds((s, FF_PAD), BF16)],
        scratch_shapes=[pltpu.VMEM((tm, D_MODEL), F32)],
    )


def _rms_matmul(h, gain, w, name):
    s = h.shape[0]
    n = w.shape[1]
    tm = min(_FFN_ROWS, s)
    tn = 1024
    nn = n // tn

    def body(h_ref, g_ref, w_ref, n_ref, o_ref):
        @pl.when(pl.program_id(1) == 0)
        def _():
            n_ref[...] = _rms(h_ref[...], g_ref[...]).astype(BF16)

        o_ref[...] = _dot(n_ref[...], w_ref[...]).astype(BF16)

    return _pc(
        body, name=name, grid=(s // tm, nn),
        in_specs=[pl.BlockSpec((tm, D_MODEL), lambda i, j: (i, 0)), pl.BlockSpec((1, D_MODEL), lambda i, j: (0, 0)),
                  pl.BlockSpec((D_MODEL, tn), lambda i, j: (0, j))],
        out_specs=[pl.BlockSpec((tm, D_MODEL), lambda i, j: (i, 0)), pl.BlockSpec((tm, tn), lambda i, j: (i, j))],
        out_shape=[_sds((s, D_MODEL), BF16), _sds((s, n), BF16)],
    )(h, gain, w)


def _shift_down(u, k, rows):
    return jnp.where(rows >= k, pltpu.roll(u, k, 0), 0.0)


def _shift_up(u, k, rows, s):
    return jnp.where(rows < s - k, pltpu.roll(u, s - k, 0), 0.0)


_CONV_COLS = 128


def _conv_fwd(bcx, cw):
    s = bcx.shape[0]
    tc = _CONV_COLS
    nc = D_MODEL // tc

    def body(b_ref, c_ref, xp_ref, w_ref, o_ref):
        rows = lax.broadcasted_iota(jnp.int32, (s, tc), 0)
        u = c_ref[...].astype(F32) * xp_ref[...].astype(F32)
        w = w_ref[...]
        uc = w[2:3] * u + w[1:2] * _shift_down(u, 1, rows) + w[0:1] * _shift_down(u, 2, rows)
        o_ref[...] = (b_ref[...].astype(F32) * uc).astype(BF16)

    col = lambda off: pl.BlockSpec((s, tc), lambda j: (0, off + j))
    return _pc(
        body, name="conv_fwd", grid=(nc,),
        in_specs=[col(0), col(nc), col(2 * nc), pl.BlockSpec((3, tc), lambda j: (0, j))],
        out_specs=pl.BlockSpec((s, tc), lambda j: (0, j)),
        out_shape=_sds((s, D_MODEL), BF16),
    )(bcx, bcx, bcx, cw)


def _final_loss(h, gain, target):
    s = h.shape[0]
    tm = min(_FFN_ROWS, s)

    def body(h_ref, g_ref, t_ref, dh_ref, dg_ref, loss_ref):
        i = pl.program_id(0)

        @pl.when(i == 0)
        def _():
            dg_ref[...] = jnp.zeros_like(dg_ref)
            loss_ref[...] = jnp.zeros_like(loss_ref)

        hb = h_ref[...]
        e = _rms(hb, g_ref[...]) - t_ref[...]
        loss_ref[...] += 0.5 * jnp.sum(jnp.mean(e * e, axis=-1, keepdims=True))
        dx, dg = _rms_bwd(e * (1.0 / D_MODEL), hb, g_ref[...])
        dh_ref[...] = dx
        dg_ref[...] += dg

    row = pl.BlockSpec((tm, D_MODEL), lambda i: (i, 0))
    vec = pl.BlockSpec((1, D_MODEL), lambda i: (0, 0))
    return _pc(
        body, name="final_loss", grid=(s // tm,),
        in_specs=[row, vec, row],
        out_specs=[row, vec, pl.BlockSpec((8, D_MODEL), lambda i: (0, 0))],
        out_shape=[_sds((s, D_MODEL), F32), _sds((1, D_MODEL), F32), _sds((8, D_MODEL), F32)],
    )(h, gain, target)


def _ffn_bwd_x(dho, h, gain, gate, up, wg, wu, wd, name, carry=None):
    s = h.shape[0]
    tm = _row_block(s)
    tf = 1024
    nf = FF_PAD // tf

    def body(dho_ref, h_ref, g_ref, gate_ref, up_ref, wg_ref, wu_ref, wd_ref,
             dh_ref, dhb_ref, dhob_ref, dgate_ref, dup_ref, act_ref, dgain_ref, acc_ref):
        i = pl.program_id(0)
        j = pl.program_id(1)

        @pl.when(j == 0)
        def _():
            dhob_ref[...] = dho_ref[...].astype(BF16)
            acc_ref[...] = jnp.zeros_like(acc_ref)

        @pl.when((i == 0) & (j == 0))
        def _():
            dgain_ref[...] = jnp.zeros_like(dgain_ref)

        dact = _dot_nt(dhob_ref[...], wd_ref[...])
        g = gate_ref[...].astype(F32)
        u = up_ref[...].astype(F32)
        sg = jax.nn.sigmoid(g)
        silu = g * sg
        dg = (dact * u * (sg * (1.0 + g * (1.0 - sg)))).astype(BF16)
        du = (dact * silu).astype(BF16)
        dgate_ref[...] = dg
        dup_ref[...] = du
        act_ref[...] = (silu * u).astype(BF16)
        acc_ref[...] += _dot_nt(dg, wg_ref[...]) + _dot_nt(du, wu_ref[...])

        @pl.when(j == nf - 1)
        def _():
            dx, dgain = _rms_bwd(acc_ref[...], h_ref[...], g_ref[...])
            dh = dho_ref[...] + dx
            dh_ref[...] = dh
            dhb_ref[...] = dh.astype(BF16)
            dgain_ref[...] += dgain

    row = pl.BlockSpec((tm, D_MODEL), lambda i, j: (i, 0))
    vec = pl.BlockSpec((1, D_MODEL), lambda i, j: (0, 0))
    hid = pl.BlockSpec((tm, tf), lambda i, j: (i, j))
    wcol = pl.BlockSpec((D_MODEL, tf), lambda i, j: (0, j))
    wrow = pl.BlockSpec((tf, D_MODEL), lambda i, j: (j, 0))
    return _pc_carrying(
        body, carry, (dho, h, gain, gate, up, wg, wu, wd), name=name, grid=(s // tm, nf),
        in_specs=[row, row, vec, hid, hid, wcol, wcol, wrow],
        out_specs=[row, row, row, hid, hid, hid, vec],
        out_shape=[_sds((s, D_MODEL), F32), _sds((s, D_MODEL), BF16), _sds((s, D_MODEL), BF16),
                   _sds((s, FF_PAD), BF16), _sds((s, FF_PAD), BF16), _sds((s, FF_PAD), BF16),
                   _sds((1, D_MODEL), F32)],
        scratch_shapes=[pltpu.VMEM((tm, D_MODEL), F32)],
    )


def _nt_rmsbwd(a, w, h, gain, dho, name, carry=None):
    stacked = a.ndim == 3
    if stacked:
        nk, s, tk = a.shape
    else:
        s, tk = a.shape
        nk = 1
    tm = min(_FFN_ROWS, s)

    def body(a_ref, w_ref, h_ref, g_ref, dho_ref, dh_ref, dhb_ref, dgain_ref, acc_ref):
        i = pl.program_id(0)
        j = pl.program_id(1)
        if stacked:
            a_ref = a_ref.at[0]

        @pl.when(j == 0)
        def _():
            acc_ref[...] = jnp.zeros_like(acc_ref)

        @pl.when((i == 0) & (j == 0))
        def _():
            dgain_ref[...] = jnp.zeros_like(dgain_ref)

        acc_ref[...] += _dot_nt(a_ref[...], w_ref[...])

        @pl.when(j == nk - 1)
        def _():
            dx, dgain = _rms_bwd(acc_ref[...], h_ref[...], g_ref[...])
            dh = dho_ref[...] + dx
            dh_ref[...] = dh
            dhb_ref[...] = dh.astype(BF16)
            dgain_ref[...] += dgain

    row = pl.BlockSpec((tm, D_MODEL), lambda i, j: (i, 0))
    vec = pl.BlockSpec((1, D_MODEL), lambda i, j: (0, 0))
    return _pc_carrying(
        body, carry, (a, w, h, gain, dho), name=name, grid=(s // tm, nk),
        in_specs=[pl.BlockSpec((1, tm, tk), lambda i, j: (j, i, 0)) if stacked else pl.BlockSpec((tm, tk), lambda i, j: (i, 0)),
                  pl.BlockSpec((D_MODEL, tk), lambda i, j: (0, j)), row, vec, row],
        out_specs=[row, row, vec],
        out_shape=[_sds((s, D_MODEL), F32), _sds((s, D_MODEL), BF16), _sds((1, D_MODEL), F32)],
        scratch_shapes=[pltpu.VMEM((tm, D_MODEL), F32)],
    )


def _matmul_nt(a, w, name):
    s, kd = a.shape
    n = w.shape[0]
    tm = min(_FFN_ROWS, s)

    def body(a_ref, w_ref, o_ref):
        o_ref[...] = _dot_nt(a_ref[...], w_ref[...]).astype(BF16)

    return _pc(
        body, name=name, grid=(s // tm,),
        in_specs=[pl.BlockSpec((tm, kd), lambda i: (i, 0)), pl.BlockSpec((n, kd), lambda i: (0, 0))],
        out_specs=pl.BlockSpec((tm, n), lambda i: (i, 0)),
        out_shape=_sds((s, n), BF16),
    )(a, w)


def _matmul_tn(a, b, name, carry=None):
    s, m = a.shape
    stacked = b.ndim == 3
    tmm = min(m, 1024)
    if stacked:
        n, tn = b.shape[0] * b.shape[2], b.shape[2]
    else:
        n = b.shape[1]
        tn = n if n <= 1024 else 1024
    tk = min(_TN_ROWS, s)
    nk = s // tk

    def body(a_ref, b_ref, o_ref, acc_ref):
        k = pl.program_id(2)
        if stacked:
            b_ref = b_ref.at[0]

        @pl.when(k == 0)
        def _():
            acc_ref[...] = jnp.zeros_like(acc_ref)

        acc_ref[...] += _dot_tn(a_ref[...], b_ref[...])

        @pl.when(k == nk - 1)
        def _():
            o_ref[...] = acc_ref[...].astype(BF16)

    (out,), got = _pc_carrying(
        body, carry, (a, b), name=name, grid=(m // tmm, n // tn, nk),
        in_specs=[pl.BlockSpec((tk, tmm), lambda i, j, k: (k, i)),
                  pl.BlockSpec((1, tk, tn), lambda i, j, k: (j, k, 0)) if stacked
                  else pl.BlockSpec((tk, tn), lambda i, j, k: (k, j))],
        out_specs=[pl.BlockSpec((tmm, tn), lambda i, j, k: (i, j))],
        out_shape=[_sds((m, n), BF16)],
        scratch_shapes=[pltpu.VMEM((tmm, tn), F32)],
    )
    return out if carry is None else (out, got)


def _conv_bwd(dbuc, bcx, cw):
    s = bcx.shape[0]
    tc = _CONV_COLS
    nc = D_MODEL // tc

    def body(d_ref, b_ref, c_ref, xp_ref, w_ref, dbcx_ref, dw_ref):
        rows = lax.broadcasted_iota(jnp.int32, (s, tc), 0)
        c = c_ref[...].astype(F32)
        xp = xp_ref[...].astype(F32)
        u = c * xp
        u1 = _shift_down(u, 1, rows)
        u2 = _shift_down(u, 2, rows)
        w = w_ref[...]
        uc = w[2:3] * u + w[1:2] * u1 + w[0:1] * u2
        d = d_ref[...].astype(F32)
        dbcx_ref[0] = (d * uc).astype(BF16)
        duc = d * b_ref[...].astype(F32)
        du = w[2:3] * duc + w[1:2] * _shift_up(duc, 1, rows, s) + w[0:1] * _shift_up(duc, 2, rows, s)
        dbcx_ref[1] = (du * xp).astype(BF16)
        dbcx_ref[2] = (du * c).astype(BF16)
        dw_ref[0:1, :] = jnp.sum(duc * u2, axis=0, keepdims=True)
        dw_ref[1:2, :] = jnp.sum(duc * u1, axis=0, keepdims=True)
        dw_ref[2:3, :] = jnp.sum(duc * u, axis=0, keepdims=True)

    col = lambda off: pl.BlockSpec((s, tc), lambda j: (0, off + j))
    tap = pl.BlockSpec((3, tc), lambda j: (0, j))
    return _pc(
        body, name="conv_bwd", grid=(nc,),
        in_specs=[col(0), col(0), col(nc), col(2 * nc), tap],
        out_specs=[pl.BlockSpec((3, s, tc), lambda j: (0, 0, j)), tap],
        out_shape=[_sds((3, s, D_MODEL), BF16), _sds((3, D_MODEL), F32)],
    )(dbuc, bcx, bcx, bcx, cw)


def _attn_bwd(q, k, v, do, o, lse, cc, sn, sp, carry=None):
    s = q.shape[1]
    bk = _row_block(s)
    nb = s // bk

    def body(q_ref, k_ref, v_ref, do_ref, o_ref, lse_ref, cc_ref, sn_ref, sp_ref,
             dq_ref, dk_ref, dv_ref, dqacc_ref, delta_ref, bias_ref):
        j = pl.program_id(1)
        kb = k_ref[0]
        vb = v_ref[0]

        @pl.when((pl.program_id(0) == 0) & (j == 0))
        def _():
            bias_ref[...] = _chunk_bias(2 * bk, bk)

        @pl.when(j == 0)
        def _():
            dqacc_ref[...] = jnp.zeros_like(dqacc_ref)

            def fill(i, _):
                rows = pl.ds(pl.multiple_of(i * bk, bk), bk)
                d = jnp.sum(do_ref[rows, :].astype(F32) * o_ref[rows, :].astype(F32), axis=-1, keepdims=True)
                delta_ref[rows, :] = jnp.broadcast_to(d, (bk, 128))
                return 0

            lax.fori_loop(0, nb, fill, 0)

        def step(i, carry, masked, blocks=1):
            dk, dv = carry
            rows = pl.ds(pl.multiple_of(i * bk, bk), blocks * bk)
            qb = q_ref[0, rows, :]
            dob = do_ref[rows, :]
            sc = _dot_nt(qb, kb)
            if masked:
                sc = sc + bias_ref[:blocks * bk, :]
            p = jnp.exp2(sc - lse_ref[0, rows, :][:, :1])
            dv = dv + _dot_tn(p.astype(BF16), dob)
            ds = (p * (_dot_nt(dob, vb) - delta_ref[rows, :][:, :1])).astype(BF16)
            dk = dk + _dot_tn(ds, qb)
            dqacc_ref[rows, :] += _dot(ds, kb)
            return dk, dv

        rest = nb - 1 - j
        odd = lax.rem(rest, 2)
        carry = (jnp.zeros((bk, HEAD_PAD), F32), jnp.zeros((bk, V_DIM), F32))
        carry = lax.fori_loop(0, 1 - odd, lambda t, c: step(j, c, True), carry)
        if nb > 1:
            carry = lax.fori_loop(0, odd, lambda t, c: step(j, c, True, blocks=2), carry)
        dq = dqacc_ref[pl.ds(pl.multiple_of(j * bk, bk), bk), :] * ATT_SCALE
        dq_ref[:, :NOPE] = dq[:, :NOPE].astype(BF16)
        dq_ref[:, NOPE:] = _rope_bwd(dq[:, NOPE:], cc_ref[...], sn_ref[...], sp_ref[...]).astype(BF16)
        if nb > 1:
            carry = lax.fori_loop(
                0, rest // 2, lambda t, c: step(j + 1 + odd + 2 * t, c, False, blocks=2), carry)
        dk, dv = carry
        dk_ref[0] = (dk * LN_2).astype(BF16)
        dv_ref[0] = dv.astype(BF16)

    blk = lambda n: pl.BlockSpec((1, bk, n), lambda h, j: (h, j, 0))
    whole = lambda n: pl.BlockSpec((1, s, n), lambda h, j: (h, 0, 0))
    cols = pl.BlockSpec((s, V_DIM), lambda h, j: (0, h))
    tab = pl.BlockSpec((bk, 128), lambda h, j: (j, 0))
    return _pc_carrying(
        body, carry, (q, k, v, do, o, lse, cc, sn, sp), name="attn_bwd", grid=(N_HEADS, nb),
        in_specs=[whole(HEAD_PAD), blk(HEAD_PAD), blk(V_DIM), cols, cols, whole(128), tab, tab, tab],
        out_specs=[pl.BlockSpec((bk, HEAD_PAD), lambda h, j: (j, h)), blk(HEAD_PAD), blk(V_DIM)],
        out_shape=[_sds((s, N_HEADS * HEAD_PAD), BF16), _sds((N_HEADS, s, HEAD_PAD), BF16),
                   _sds((N_HEADS, s, V_DIM), BF16)],
        scratch_shapes=[pltpu.VMEM((s, HEAD_PAD), F32), pltpu.VMEM((s, 128), F32), pltpu.VMEM((2 * bk, bk), F32)],
    )


def _mla_mid_bwd(dq, dk, dv, proj, w_uq, w_ukv, g_cq, g_ckv, cc, sn, sp, carry=None):
    s = dq.shape[0]
    tm = _row_block(s)

    def body(dq_ref, dk_ref, dv_ref, proj_ref, wuq_ref, wukv_ref, gcq_ref, gckv_ref, cc_ref, sn_ref, sp_ref,
             dproj_ref, dkv_ref, dgcq_ref, dgckv_ref, acq_ref, ackv_ref, akr_ref):
        @pl.when(pl.program_id(0) == 0)
        def _():
            dgcq_ref[...] = jnp.zeros_like(dgcq_ref)
            dgckv_ref[...] = jnp.zeros_like(dgckv_ref)

        for h in range(N_HEADS):
            cols = slice(h * HEAD_PAD, (h + 1) * HEAD_PAD)
            dkb = dk_ref[h]
            dkv_ref[:, h * HEAD_PAD:h * HEAD_PAD + NOPE] = dkb[:, :NOPE]
            dkv_ref[:, h * HEAD_PAD + NOPE:(h + 1) * HEAD_PAD] = dv_ref[h]
            cq_part = _dot_nt(dq_ref[:, cols], wuq_ref[h])
            ckv_part = _dot_nt(dkv_ref[:, cols], wukv_ref[h])
            kr_part = dkb[:, NOPE:].astype(F32)
            if h == 0:
                acq_ref[...], ackv_ref[...], akr_ref[...] = cq_part, ckv_part, kr_part
            else:
                acq_ref[...] += cq_part
                ackv_ref[...] += ckv_part
                akr_ref[...] += kr_part

        proj = proj_ref[...]
        dcq, dgcq = _rms_bwd(acq_ref[...], proj[:, :CQ], gcq_ref[...])
        dckv, dgckv = _rms_bwd(ackv_ref[...], proj[:, CQ:CQ + CKV], gckv_ref[...])
        dproj_ref[:, :CQ] = dcq.astype(BF16)
        dproj_ref[:, CQ:CQ + CKV] = dckv.astype(BF16)
        dproj_ref[:, CQ + CKV:] = _rope_bwd(akr_ref[...], cc_ref[...], sn_ref[...], sp_ref[...]).astype(BF16)
        dgcq_ref[...] += dgcq
        dgckv_ref[...] += dgckv

    head_blk = lambda n: pl.BlockSpec((N_HEADS, tm, n), lambda i: (0, i, 0))
    head_cols = pl.BlockSpec((tm, N_HEADS * HEAD_PAD), lambda i: (i, 0))
    head_w = lambda a: pl.BlockSpec(a.shape, lambda i: (0, 0, 0))
    row = lambda n: pl.BlockSpec((tm, n), lambda i: (i, 0))
    vec = lambda n: pl.BlockSpec((1, n), lambda i: (0, 0))
    return _pc_carrying(
        body, carry, (dq, dk, dv, proj, w_uq, w_ukv, g_cq, g_ckv, cc, sn, sp),
        name="mla_mid_bwd", grid=(s // tm,),
        in_specs=[head_cols, head_blk(HEAD_PAD), head_blk(V_DIM), row(PROJ_PAD), head_w(w_uq), head_w(w_ukv),
                  vec(CQ), vec(CKV), row(128), row(128), row(128)],
        out_specs=[row(PROJ_PAD), head_cols, vec(CQ), vec(CKV)],
        out_shape=[_sds((s, PROJ_PAD), BF16), _sds((s, N_HEADS * HEAD_PAD), BF16), _sds((1, CQ), F32),
                   _sds((1, CKV), F32)],
        scratch_shapes=[pltpu.VMEM((tm, CQ), F32), pltpu.VMEM((tm, CKV), F32), pltpu.VMEM((tm, 128), F32)],
    )


def _peer(k):
    x, y, c = lax.axis_index("x"), lax.axis_index("y"), lax.axis_index("c")
    px = 1 - x if k & 4 else x
    py = 1 - y if k & 2 else y
    pc = 1 - c if k & 1 else c
    return (px, py, pc), 4 * px + 2 * py + pc


def _exchange_ops(arrays, scatter):
    nw = len(arrays)
    by_cols = [isinstance(a, _Cols) for a in arrays]
    used_rows = [a.rows if isinstance(a, _Rows) else None for a in arrays]
    arrays = [a.array if isinstance(a, (_Cols, _Rows)) else a for a in arrays]
    direct = range(1, N_DEV) if scatter else (1, 2, 4, 6)
    assert scatter or not any(used_rows)

    def columns(ref, idx, width):
        return ref.at[:, pl.ds(pl.multiple_of(idx * width, 128), width)]

    def sent(ins, w, idx):
        if not scatter:
            return ins[w]
        if by_cols[w]:
            return columns(ins[w], idx, arrays[w].shape[1] // N_DEV)
        return ins[w].at[idx, pl.ds(0, used_rows[w])] if used_rows[w] else ins[w].at[idx]

    def slot(outs, w, idx):
        if by_cols[w] and not scatter:
            return columns(outs[w], idx, arrays[w].shape[1])
        return outs[w].at[idx]

    def copy(w, k, src, dst, to, send_sems, recv_sems):
        return pltpu.make_async_remote_copy(
            src_ref=src, dst_ref=dst, send_sem=send_sems.at[w * N_DEV + k], recv_sem=recv_sems.at[w * N_DEV + k],
            device_id=to, device_id_type=pl.DeviceIdType.MESH)

    def own_copies(ins, outs, local_sems):
        _, me = _peer(0)
        return [pltpu.make_async_copy(sent(ins, w, me), slot(outs, w, me), local_sems.at[w]) for w in range(nw)]

    def sends(ins, outs, send_sems, recv_sems):
        _, me = _peer(0)
        out = []
        for k in direct:
            dev, idx = _peer(k)
            for w in range(nw):
                out.append(copy(w, k, sent(ins, w, idx), slot(outs, w, me), dev, send_sems, recv_sems))
        return out

    def relays(outs, send_sems, recv_sems):
        sibling, _ = _peer(1)
        out = []
        for k in (2, 4, 6):
            _, idx = _peer(k)
            for w in range(nw):
                out.append(copy(w, k + 1, slot(outs, w, idx), slot(outs, w, idx), sibling, send_sems, recv_sems))
        return out

    def arrival(outs, w, k, send_sems, recv_sems):
        dev, idx = _peer(k)
        return copy(w, k, slot(outs, w, idx), slot(outs, w, idx), dev, send_sems, recv_sems)

    def start(ins, outs, send_sems, recv_sems, local_sems):
        for cp in own_copies(ins, outs, local_sems) + sends(ins, outs, send_sems, recv_sems):
            cp.start()

    def relay(ins, outs, send_sems, recv_sems, local_sems):
        for k in (2, 4, 6):
            for w in range(nw):
                arrival(outs, w, k, send_sems, recv_sems).wait_recv()
        for cp in relays(outs, send_sems, recv_sems):
            cp.start()

    def wait(ins, outs, send_sems, recv_sems, local_sems):
        for cp in own_copies(ins, outs, local_sems):
            cp.wait()
        for cp in sends(ins, outs, send_sems, recv_sems) + ([] if scatter else relays(outs, send_sems, recv_sems)):
            cp.wait_send()
        for k in (range(1, N_DEV) if scatter else (1, 3, 5, 7)):
            for w in range(nw):
                arrival(outs, w, k, send_sems, recv_sems).wait_recv()

    def landed_shape(a, cols, rows):
        if rows:
            return (N_DEV, rows) + a.shape[2:]
        if not cols:
            return a.shape if scatter else (N_DEV,) + a.shape
        r, c = a.shape
        return (N_DEV, r, c // N_DEV) if scatter else (r, N_DEV * c)

    landed = [_sds(landed_shape(a, cols, rows), a.dtype) for a, cols, rows in zip(arrays, by_cols, used_rows)]
    sems = [pltpu.SemaphoreType.DMA((nw * N_DEV,)), pltpu.SemaphoreType.DMA((nw * N_DEV,)),
            pltpu.SemaphoreType.DMA((nw,))]
    return start, (None if scatter else relay), wait, landed, sems, arrays


class _Cols:
    def __init__(self, array):
        self.array = array


class _Rows:
    def __init__(self, array, rows):
        self.array, self.rows = array, rows


def _exchange(arrays, scatter, name):
    nw = len(arrays)
    start, relay, wait, landed, sems, arrays = _exchange_ops(arrays, scatter)

    def body(*refs):
        ins, outs, csem = refs[:nw], refs[nw:2 * nw], refs[2 * nw:]
        start(ins, outs, *csem)
        if relay is not None:
            relay(ins, outs, *csem)
        wait(ins, outs, *csem)

    any_spec = pl.BlockSpec(memory_space=pl.ANY)
    return _pc(body, name=name, in_specs=[any_spec] * nw, out_specs=[any_spec] * nw, out_shape=landed,
               scratch_shapes=sems)(*arrays)


def _adam_math(g, w, m, v):
    m = ADAM_B1 * m + (1.0 - ADAM_B1) * g
    v = ADAM_B2 * v + (1.0 - ADAM_B2) * jnp.square(g)
    m_hat = m / (1.0 - ADAM_B1 ** ADAM_STEP)
    v_hat = v / (1.0 - ADAM_B2 ** ADAM_STEP)
    delta = -ADAM_LR * (m_hat / (jnp.sqrt(v_hat) + ADAM_EPS) + ADAM_WD * w)
    return delta, m, v


def _adam_rows(r):
    for t in range(min(r, 512) // 16 * 16, 0, -16):
        if r % t == 0:
            return t
    return r


def _adamw(parts, w, m, v, name):
    nl, r, c = w.shape
    tr = _adam_rows(r)
    nr = r // tr

    def body(*refs):
        p_refs, (w_ref, m_ref, v_ref, g_ref, d_ref, mo_ref, vo_ref) = refs[:nl], refs[nl:]
        for layer in range(nl):
            @pl.when(pl.program_id(0) == layer)
            def _(p_ref=p_refs[layer]):
                g = p_ref[0].astype(F32)
                for src in range(1, N_DEV):
                    g = g + p_ref[src].astype(F32)
                delta, m2, v2 = _adam_math(g, w_ref[0], m_ref[0], v_ref[0])
                g_ref[0] = g
                d_ref[0] = delta
                mo_ref[0] = m2
                vo_ref[0] = v2

    def part_spec(layer):
        def index(l, i):
            return 0, jnp.where(l == layer, i, jnp.where(l < layer, 0, nr - 1)), 0
        return pl.BlockSpec((N_DEV, tr, c), index)

    blk = pl.BlockSpec((1, tr, c), lambda l, i: (l, i, 0))
    return _pc(
        body, name=name, grid=(nl, nr),
        in_specs=[part_spec(layer) for layer in range(nl)] + [blk, blk, blk],
        out_specs=[blk, blk, blk, blk],
        out_shape=[_sds((nl, r, c), F32)] * 4,
    )(*parts, w, m, v)


def _small_allreduce_adamw(gpack, wpack, mpack, vpack, last_grads):
    shape = gpack.shape
    compact = (SMALL_ROWS, D_MODEL)
    nw = len(last_grads)
    start, _, wait, landed, exchange_sems, last_grads = _exchange_ops(last_grads, True)

    def body(*refs):
        (g_ref, w_ref, m_ref, v_ref), rest = refs[:4], refs[4:]
        cin, rest = rest[:nw], rest[nw:]
        (go_ref, d_ref, mo_ref, vo_ref), rest = rest[:4], rest[4:]
        cout, rest = rest[:nw], rest[nw:]
        (comp_ref, gath_ref, send_sems, recv_sems), csem = rest[:4], rest[4:]
        start(cin, cout, *csem)
        _, me = _peer(0)
        comp_ref[...] = jnp.zeros(compact, F32)
        r = 0
        for p, n in enumerate(SMALL_PIECES):
            comp_ref[r:r + n, :] = g_ref[8 * p:8 * p + n, :]
            r += n
        gath_ref[me] = comp_ref[...]
        copies = []
        for k in range(1, N_DEV):
            dev, idx = _peer(k)
            copies.append(pltpu.make_async_remote_copy(
                src_ref=comp_ref, dst_ref=gath_ref.at[me], send_sem=send_sems.at[k], recv_sem=recv_sems.at[k],
                device_id=dev, device_id_type=pl.DeviceIdType.MESH))
        for cp in copies:
            cp.start()
        for cp in copies:
            cp.wait_send()
        for k in range(1, N_DEV):
            dev, idx = _peer(k)
            pltpu.make_async_remote_copy(
                src_ref=comp_ref, dst_ref=gath_ref.at[idx], send_sem=send_sems.at[k], recv_sem=recv_sems.at[k],
                device_id=dev, device_id_type=pl.DeviceIdType.MESH).wait_recv()
        g = gath_ref[0]
        for src in range(1, N_DEV):
            g = g + gath_ref[src]
        go_ref[...] = jnp.zeros(shape, F32)
        r = 0
        for p, n in enumerate(SMALL_PIECES):
            go_ref[8 * p:8 * p + n, :] = g[r:r + n, :]
            r += n
        delta, m2, v2 = _adam_math(go_ref[...], w_ref[...], m_ref[...], v_ref[...])
        d_ref[...] = delta
        mo_ref[...] = m2
        vo_ref[...] = v2
        wait(cin, cout, *csem)

    vm = pl.BlockSpec(memory_space=pltpu.VMEM)
    any_spec = pl.BlockSpec(memory_space=pl.ANY)
    res = _pc(
        body, name="small_allreduce_adamw",
        in_specs=[vm] * 4 + [any_spec] * nw, out_specs=[vm] * 4 + [any_spec] * nw,
        out_shape=[_sds(shape, F32)] * 4 + landed,
        scratch_shapes=[pltpu.VMEM(compact, F32), pltpu.VMEM((N_DEV,) + compact, F32),
                        pltpu.SemaphoreType.DMA((N_DEV,)), pltpu.SemaphoreType.DMA((N_DEV,))] + exchange_sems,
    )(gpack, wpack, mpack, vpack, *last_grads)
    return res[:4], res[4:]


def _rows_from_shards(g):
    return g.reshape(N_DEV * g.shape[1], g.shape[2])


def _shards_from_rows(a):
    return a.reshape(N_DEV, a.shape[0] // N_DEV, a.shape[1])


def _pad_to(a, rows, cols):
    return jnp.pad(a, ((0, rows - a.shape[0]), (0, cols - a.shape[1])))


def _rope_tables(pos):
    inv_freq = 1.0 / (ROPE_THETA ** (jnp.arange(0, ROPE, 2, dtype=F32) / ROPE))
    ang = pos.astype(F32)[:, None] * inv_freq
    cos, sin = jnp.cos(ang), jnp.sin(ang)
    z32, z64, z96 = (jnp.zeros((pos.shape[0], n), F32) for n in (32, 64, 96))
    return (jnp.concatenate([cos, cos, z64], axis=1), jnp.concatenate([-sin, z96], axis=1),
            jnp.concatenate([z32, sin, z64], axis=1))


def _pad_row(vec):
    vec = vec.reshape(1, -1)
    return jnp.pad(vec, ((0, 0), (0, D_MODEL - vec.shape[1])))


def _forward_backward(x, target, tables, gains, shards):
    cc, sn, sp = tables
    ffn_g = gains["ffn_norm"]
    by_cols = ("wg0", "wu0", "wg1", "wu1", "c_in")

    def gather(names, relay_at):
        return [_Cols(shards[n]) if n in by_cols else shards[n] for n in names], False, relay_at

    (got,) = _exchange([shards["w_in"]], False, "gather_w_in")
    w_in = _rows_from_shards(got)
    (n0, proj, cqn, ckvn, krr), (w_uq, w_ukv) = _mla_in_fwd(
        x, gains["mla_norm"], w_in, gains["g_cq"], gains["g_ckv"], cc, sn, sp, carry=gather(["w_uq", "w_ukv"], 0.99))
    (q, k, v), (w_o, conv_norm, conv_w) = _qkv_proj(
        cqn, ckvn, krr, w_uq, w_ukv, cc, sn, sp, carry=gather(["w_o", "conv_norm", "conv_w"], 0.5))
    w_o = _rows_from_shards(w_o)
    conv_norm = conv_norm.reshape(1, D_MODEL)
    conv_w = jnp.transpose(conv_w, (1, 0, 2)).reshape(3, D_MODEL)
    (o, lse), (wg0, wu0, wd0, c_in, c_out, wd1) = _attn_fwd(
        q, k, v, carry=gather(["wg0", "wu0", "wd0", "c_in", "c_out", "wd1"], 0.9))
    wd0 = _rows_from_shards(wd0)
    wd1 = _rows_from_shards(wd1)
    c_out = _rows_from_shards(c_out)
    h1 = _matmul_res(o, w_o, x, "mla_out_fwd")
    (h2, n1, gate0, up0), (wg1, wu1) = _ffn_fwd(
        h1, ffn_g[0:1], wg0, wu0, wd0, "ffn0_fwd", carry=gather(["wg1", "wu1"], 0.85))
    n2, bcx = _rms_matmul(h2, conv_norm, c_in, "conv_in_fwd")
    bu = _conv_fwd(bcx, conv_w)
    h3 = _matmul_res(bu, c_out, h2, "conv_out_fwd")
    (h4, n3, gate1, up1), _ = _ffn_fwd(h3, ffn_g[1:2], wg1, wu1, wd1, "ffn1_fwd")
    dh4, d_final, loss = _final_loss(h4, gains["final_norm"], target)

    small = {"final_norm": d_final}
    parts = {}

    def scatter(**blocks):
        return list(blocks), (list(blocks.values()), True, None)

    (dh3, dh3_b, dh4_b, dgate, dup, act, d_ffn1), _ = _ffn_bwd_x(
        dh4, h3, ffn_g[1:2], gate1, up1, wg1, wu1, wd1, "ffn1_bwd")
    dwg1 = _matmul_tn(dgate, n3, "ffn1_dwg")
    dwu1 = _matmul_tn(dup, n3, "ffn1_dwu")
    dwd1 = _matmul_tn(act, dh4_b, "ffn1_dwd")
    dbuc = _matmul_nt(dh3_b, c_out, "conv_out_bwd")
    d_c_out = _matmul_tn(bu, dh3_b, "conv_dwout")
    dbcx, small["conv_w"] = _conv_bwd(dbuc, bcx, conv_w)
    d_c_in = _matmul_tn(n2, dbcx, "conv_dwin")
    names, carry = scatter(c_out=_shards_from_rows(d_c_out))
    (dh2, dh2_b, small["conv_norm"]), got = _nt_rmsbwd(
        dbcx, c_in, h2, conv_norm, dh3, "conv_in_bwd", carry=carry)
    parts.update(zip(names, got))
    ffn_blocks = lambda d: _Rows(_shards_from_rows(d), FF_SHARD)
    names, carry = scatter(c_in=_Cols(d_c_in), wd1=ffn_blocks(dwd1))
    (dh1, dh1_b, dh2_b2, dgate, dup, act, d_ffn0), got = _ffn_bwd_x(
        dh2, h1, ffn_g[0:1], gate0, up0, wg0, wu0, wd0, "ffn0_bwd", carry=carry)
    parts.update(zip(names, got))
    dwg0 = _matmul_tn(dgate, n1, "ffn0_dwg")
    dwu0 = _matmul_tn(dup, n1, "ffn0_dwu")
    dwd0 = _matmul_tn(act, dh2_b2, "ffn0_dwd")
    small["ffn_norm"] = jnp.pad(d_ffn0, ((0, 7), (0, 0))) + jnp.pad(d_ffn1, ((1, 6), (0, 0)))
    do = _matmul_nt(dh1_b, w_o, "mla_out_bwd")
    d_w_o = _matmul_tn(o, dh1_b, "mla_dwo")
    names, carry = scatter(**{n: ffn_blocks(d) for n, d in dict(
        wg0=dwg0, wg1=dwg1, wu0=dwu0, wu1=dwu1, wd0=dwd0).items()})
    (dq, dk, dv), got = _attn_bwd(q, k, v, do, o, lse, cc, sn, sp, carry=carry)
    parts.update(zip(names, got))
    d_w_uq = _matmul_tn(dq, cqn, "mla_dwuq")
    names, carry = scatter(w_o=_shards_from_rows(d_w_o), w_uq=_Rows(_shards_from_rows(d_w_uq), NOPE + ROPE))
    (dproj, dkv, small["g_cq"], small["g_ckv"]), got = _mla_mid_bwd(
        dq, dk, dv, proj, w_uq, w_ukv, gains["g_cq"], gains["g_ckv"], cc, sn, sp, carry=carry)
    parts.update(zip(names, got))
    d_w_ukv = _matmul_tn(ckvn, dkv, "mla_dwukv")
    names, carry = scatter(w_ukv=_Cols(d_w_ukv))
    d_w_in, got = _matmul_tn(dproj, n0, "mla_dwin", carry=carry)
    parts.update(zip(names, got))
    (dx, _, small["mla_norm"]), _ = _nt_rmsbwd(dproj, w_in, x, gains["mla_norm"], dh1, "mla_in_bwd")
    return loss, dx, parts, _Cols(d_w_in), small


def kernel(x, positions, mla_norm, mla_w_in, mla_g_cq, mla_g_ckv, mla_w_uq, mla_w_ukv, mla_w_o, conv_norm, conv_w_in, conv_w, conv_w_out, ffn_norm, ffn_w_gate, ffn_w_up, ffn_w_down, final_norm, loss_target, m_mla_norm, m_mla_w_in, m_mla_g_cq, m_mla_g_ckv, m_mla_w_uq, m_mla_w_ukv, m_mla_w_o, m_conv_norm, m_conv_w_in, m_conv_w, m_conv_w_out, m_ffn_norm, m_ffn_w_gate, m_ffn_w_up, m_ffn_w_down, m_final_norm, v_mla_norm, v_mla_w_in, v_mla_g_cq, v_mla_g_ckv, v_mla_w_uq, v_mla_w_ukv, v_mla_w_o, v_conv_norm, v_conv_w_in, v_conv_w, v_conv_w_out, v_ffn_norm, v_ffn_w_gate, v_ffn_w_up, v_ffn_w_down, v_final_norm):
    me = 4 * lax.axis_index("x") + 2 * lax.axis_index("y") + lax.axis_index("c")

    bf = lambda a, rows, cols: _pad_to(a.astype(BF16), rows, cols)
    shards = dict(
        w_in=bf(mla_w_in[0], D_MODEL // N_DEV, PROJ_PAD), w_uq=bf(mla_w_uq[0], CQ, HEAD_PAD),
        w_ukv=mla_w_ukv[0].astype(BF16), w_o=mla_w_o[0].astype(BF16),
        c_in=conv_w_in[0].astype(BF16), c_out=conv_w_out[0].astype(BF16),
        conv_norm=conv_norm, conv_w=conv_w[0])
    for l in range(2):
        shards.update({f"wg{l}": bf(ffn_w_gate[l], D_MODEL, FF_SHARD_PAD), f"wu{l}": bf(ffn_w_up[l], D_MODEL, FF_SHARD_PAD),
                       f"wd{l}": bf(ffn_w_down[l], FF_SHARD_PAD, D_MODEL)})
    gains = dict(mla_norm=mla_norm, g_cq=mla_g_cq, g_ckv=mla_g_ckv, ffn_norm=ffn_norm,
                 final_norm=final_norm.reshape(1, -1))
    loss_local, dx, parts, d_w_in, grads = _forward_backward(
        x[0], loss_target[0], _rope_tables(positions[0]), gains, shards)

    col0 = me * (D_MODEL // N_DEV)

    def place(shard):
        return lax.dynamic_update_slice(jnp.zeros((shard.shape[0], D_MODEL), F32), shard, (0, col0))

    no_loss = jnp.zeros((8, D_MODEL), F32)

    def pack(mla_n, g_cq, g_ckv, ffn_n, fin_n, conv_n, conv_taps, loss_tile):
        rows = [mla_n, _pad_row(g_cq), _pad_row(g_ckv), ffn_n, fin_n.reshape(1, -1), conv_n, conv_taps, loss_tile]
        assert all(r.shape[0] in (n, 8) for r, n in zip(rows, SMALL_PIECES))
        return jnp.concatenate([jnp.pad(r, ((0, 8 - r.shape[0]), (0, 0))) for r in rows], axis=0)

    gpack = pack(grads["mla_norm"], grads["g_cq"], grads["g_ckv"], grads["ffn_norm"], grads["final_norm"],
                 grads["conv_norm"], grads["conv_w"], loss_local)
    wpack = pack(mla_norm, mla_g_cq, mla_g_ckv, ffn_norm, final_norm, place(conv_norm), place(conv_w[0]), no_loss)
    mpack = pack(m_mla_norm, m_mla_g_cq, m_mla_g_ckv, m_ffn_norm, m_final_norm, place(m_conv_norm),
                 place(m_conv_w[0]), no_loss)
    vpack = pack(v_mla_norm, v_mla_g_cq, v_mla_g_ckv, v_ffn_norm, v_final_norm, place(v_conv_norm),
                 place(v_conv_w[0]), no_loss)
    small, (parts["w_in"],) = _small_allreduce_adamw(gpack, wpack, mpack, vpack, [d_w_in])
    loss = small[0][56, 0]

    def adamw(name, w, m, v, partials, transposed=False):
        if transposed:
            w, m, v = (jnp.swapaxes(a, 1, 2) for a in (w, m, v))
        outs = _adamw(partials, w, m, v, "adamw_" + name)
        return [jnp.swapaxes(o, 1, 2) for o in outs] if transposed else outs

    res = dict(
        w_in=adamw("w_in", mla_w_in, m_mla_w_in, v_mla_w_in, [parts["w_in"]], True),
        w_uq=adamw("w_uq", mla_w_uq, m_mla_w_uq, v_mla_w_uq, [parts["w_uq"]], True),
        w_ukv=adamw("w_ukv", mla_w_ukv, m_mla_w_ukv, v_mla_w_ukv, [parts["w_ukv"]]),
        w_o=adamw("w_o", mla_w_o, m_mla_w_o, v_mla_w_o, [parts["w_o"]]),
        c_in=adamw("c_in", conv_w_in, m_conv_w_in, v_conv_w_in, [parts["c_in"]]),
        c_out=adamw("c_out", conv_w_out, m_conv_w_out, v_conv_w_out, [parts["c_out"]]),
        wg=adamw("wg", ffn_w_gate, m_ffn_w_gate, v_ffn_w_gate, [parts["wg0"], parts["wg1"]], True),
        wu=adamw("wu", ffn_w_up, m_ffn_w_up, v_ffn_w_up, [parts["wu0"], parts["wu1"]], True),
        wd=adamw("wd", ffn_w_down, m_ffn_w_down, v_ffn_w_down, [parts["wd0"], parts["wd1"]]),
    )

    def unpack(p):
        own = lambda rows: lax.dynamic_slice(rows, (0, col0), (rows.shape[0], D_MODEL // N_DEV))
        return dict(mla_norm=p[0:1], g_cq=p[8:9, :CQ], g_ckv=p[16:17, :CKV], ffn_norm=p[24:26], final_norm=p[32],
                    conv_norm=own(p[40:41]), conv_w=own(p[48:51])[None])

    small = [unpack(p) for p in small]
    order = ["mla_norm", "w_in", "g_cq", "g_ckv", "w_uq", "w_ukv", "w_o", "conv_norm", "c_in", "conv_w", "c_out",
             "ffn_norm", "wg", "wu", "wd", "final_norm"]
    out = [loss, dx[None]]
    for kind in range(4):
        for n in order:
            out.append(res[n][kind] if n in res else small[kind][n])
    return tuple(out)
```
